```python
import math
import jax, jax.numpy as jnp
from jax import lax
import numpy as np

D_MODEL = 2048
BATCH = 8
SEQ = 8192
DEPTH = 2

N_HEADS = 16
HEAD_DIM = D_MODEL // N_HEADS
D_FF = 4 * D_MODEL
CONV_WIDTH = 31
DILATED_BRANCHES = ((128, 1), (512, 4), (2048, 16))
BAND = 128
REL_BUCKETS = 32
REL_MAX_DIST = 2048
N_A = DEPTH // 2
N_B = DEPTH - N_A
ALPHA = (2 * DEPTH) ** 0.25
BETA = (8 * DEPTH) ** -0.25
LN_EPS = 1e-5

kernel_name = "yoco_conformer_dilated_hybrid"


def layer_norm(x, g, b):
    xf = x.astype(jnp.float32)
    mu = jnp.mean(xf, axis=-1, keepdims=True)
    var = jnp.mean(jnp.square(xf - mu), axis=-1, keepdims=True)
    y = (xf - mu) * lax.rsqrt(var + LN_EPS) * g.astype(jnp.float32) + b.astype(jnp.float32)
    return y.astype(x.dtype)


def conv_module(x, pw1_w, pw1_b, dw_w, dw_b, ln_g, ln_b, pw2_w, pw2_b):
    h = x @ pw1_w + pw1_b
    a, gate = jnp.split(h, 2, axis=-1)
    h = a * jax.nn.sigmoid(gate)
    h = lax.conv_general_dilated(
        h, dw_w[:, None, :].astype(h.dtype), window_strides=(1,),
        padding=[(CONV_WIDTH - 1, 0)],
        dimension_numbers=('NWC', 'WIO', 'NWC'),
        feature_group_count=D_MODEL) + dw_b
    h = jax.nn.silu(layer_norm(h, ln_g, ln_b))
    return h @ pw2_w + pw2_b


def sq_relu_mlp(x, w1, w2):
    return jnp.square(jax.nn.relu(x @ w1)) @ w2


def t5_bucket(dist):
    max_exact = REL_BUCKETS // 2
    large = max_exact + (np.log(np.maximum(dist, 1) / max_exact)
                         / math.log(REL_MAX_DIST / max_exact)
                         * (REL_BUCKETS - max_exact)).astype(np.int32)
    large = np.minimum(large, REL_BUCKETS - 1)
    return np.where(dist < max_exact, dist, large).astype(np.int32)


def dilated_branch(q, k, v, rel_bias, window, dil):
    bsz, seq, nh, dh = q.shape
    n_keys = window // dil
    L = seq // dil
    nb = -(-L // BAND)
    Lp = nb * BAND

    def by_residue(t):
        return t.reshape(bsz, L, dil, nh, dh).transpose(0, 2, 3, 1, 4)

    qd = jnp.pad(by_residue(q), ((0, 0), (0, 0), (0, 0), (0, Lp - L), (0, 0)))
    kd = jnp.pad(by_residue(k), ((0, 0), (0, 0), (0, 0), (BAND, Lp - L), (0, 0)))
    vd = jnp.pad(by_residue(v), ((0, 0), (0, 0), (0, 0), (BAND, Lp - L), (0, 0)))
    qb = qd.reshape(bsz, dil, nh, nb, BAND, dh)
    kb = kd.reshape(bsz, dil, nh, nb + 1, BAND, dh)
    vb = vd.reshape(bsz, dil, nh, nb + 1, BAND, dh)
    kc = jnp.concatenate([kb[:, :, :, :-1], kb[:, :, :, 1:]], axis=4)
    vc = jnp.concatenate([vb[:, :, :, :-1], vb[:, :, :, 1:]], axis=4)

    i = np.arange(BAND)[:, None]
    j = np.arange(2 * BAND)[None, :]
    delta = i - j + BAND
    band_ok = (delta >= 0) & (delta <= n_keys)
    blk = np.arange(nb)[:, None, None]
    valid = band_ok[None] & ~((blk == 0) & (j[None] < BAND))
    bucket = t5_bucket(np.clip(delta, 0, None) * dil)
    bias = jnp.transpose(rel_bias[bucket], (2, 0, 1)).astype(jnp.float32)

    s = jnp.einsum('bdhnqe,bdhnke->bdhnqk', qb, kc) * (dh ** -0.5)
    s = s + bias[None, None, :, None]
    s = jnp.where(jnp.asarray(valid)[None, None, None], s, -jnp.inf)
    m = jnp.max(s, axis=-1, keepdims=True)
    p = jnp.exp(s - m)
    den = jnp.sum(p, axis=-1, keepdims=True)
    o = jnp.einsum('bdhnqk,bdhnke->bdhnqe', p, vc) / den
    lse = (m + jnp.log(den))[..., 0]

    o = o.reshape(bsz, dil, nh, Lp, dh)[:, :, :, :L].transpose(0, 3, 1, 2, 4).reshape(bsz, seq, nh, dh)
    lse = lse.reshape(bsz, dil, nh, Lp)[:, :, :, :L].transpose(0, 3, 1, 2).reshape(bsz, seq, nh)
    return o, lse


def dilated_attention(x, k_sh, v_sh, wq, wo, rel_bias):
    bsz, seq, _ = x.shape
    q = (x @ wq).reshape(bsz, seq, N_HEADS, HEAD_DIM).astype(jnp.float32)
    outs, lses = [], []
    for window, dil in DILATED_BRANCHES:
        o, lse = dilated_branch(q, k_sh, v_sh, rel_bias, window, dil)
        outs.append(o)
        lses.append(lse)
    w = jax.nn.softmax(jnp.stack(lses, axis=0), axis=0)
    o = jnp.sum(w[..., None] * jnp.stack(outs, axis=0), axis=0)
    return o.reshape(bsz, seq, N_HEADS * HEAD_DIM).astype(x.dtype) @ wo


def _fwd_setup_inputs(seed: int = 0) -> dict:
    key = jax.random.key(seed)
    ks = jax.random.split(key, 20)
    D = D_MODEL
    HD = N_HEADS * HEAD_DIM

    def nrm(k, shape, scale):
        return jax.random.normal(k, shape, jnp.float32) * scale

    w_k = nrm(ks[9], (D, HD), D ** -0.5)
    w_v = nrm(ks[10], (D, HD), D ** -0.5 * BETA)
    return {
        "x": nrm(ks[0], (BATCH, SEQ, D), 1.0),
        "conv_pw1_w": nrm(ks[1], (N_A, D, 2 * D), D ** -0.5),
        "conv_pw1_b": nrm(ks[2], (N_A, 2 * D), 0.02),
        "conv_dw_w": nrm(ks[3], (N_A, CONV_WIDTH, D), CONV_WIDTH ** -0.5),
        "conv_dw_b": nrm(ks[4], (N_A, D), 0.02),
        "conv_ln_g": 1.0 + nrm(ks[5], (N_A, D), 0.02),
        "conv_ln_b": nrm(ks[6], (N_A, D), 0.02),
        "conv_pw2_w": nrm(ks[7], (N_A, D, D), D ** -0.5 * BETA),
        "conv_pw2_b": nrm(ks[8], (N_A, D), 0.02),
        "w_kv": jnp.concatenate([w_k, w_v], axis=1),
        "attn_wq": nrm(ks[11], (N_B, D, HD), D ** -0.5),
        "attn_wo": nrm(ks[12], (N_B, HD, D), HD ** -0.5 * BETA),
        "rel_bias": nrm(ks[13], (REL_BUCKETS, N_HEADS), 0.2),
        "mlp_w1": nrm(ks[14], (DEPTH, D, D_FF), D ** -0.5 * BETA),
        "mlp_w2": nrm(ks[15], (DEPTH, D_FF, D), D_FF ** -0.5 * BETA),
        "ln_mix_g": 1.0 + nrm(ks[16], (DEPTH, D), 0.02),
        "ln_mix_b": nrm(ks[17], (DEPTH, D), 0.02),
        "ln_mlp_g": 1.0 + nrm(ks[18], (DEPTH, D), 0.02),
        "ln_mlp_b": nrm(ks[19], (DEPTH, D), 0.02),
    }


def _fwd_reference(x, conv_pw1_w, conv_pw1_b, conv_dw_w, conv_dw_b, conv_ln_g, conv_ln_b,
              conv_pw2_w, conv_pw2_b, w_kv, attn_wq, attn_wo, rel_bias,
              mlp_w1, mlp_w2, ln_mix_g, ln_mix_b, ln_mlp_g, ln_mlp_b):
    bsz, seq, _ = x.shape
    k_sh = None
    v_sh = None
    for layer in range(DEPTH):
        if layer < N_A:
            i = layer
            mix = conv_module(x, conv_pw1_w[i], conv_pw1_b[i], conv_dw_w[i], conv_dw_b[i],
                              conv_ln_g[i], conv_ln_b[i], conv_pw2_w[i], conv_pw2_b[i])
        else:
            if layer == N_A:
                kv = (x @ w_kv).astype(jnp.float32)
                k_sh, v_sh = jnp.split(kv, 2, axis=-1)
                k_sh = k_sh.reshape(bsz, seq, N_HEADS, HEAD_DIM)
                v_sh = v_sh.reshape(bsz, seq, N_HEADS, HEAD_DIM)
            j = layer - N_A
            mix = dilated_attention(x, k_sh, v_sh, attn_wq[j], attn_wo[j], rel_bias)
        x = layer_norm(ALPHA * x + mix, ln_mix_g[layer], ln_mix_b[layer])
        x = layer_norm(ALPHA * x + sq_relu_mlp(x, mlp_w1[layer], mlp_w2[layer]),
                       ln_mlp_g[layer], ln_mlp_b[layer])
    return x


import jax as _jax
import jax.numpy as _jnp

TWIN_FORMAT = 'train_step'
FWD_PARAMS = ['x', 'conv_pw1_w', 'conv_pw1_b', 'conv_dw_w', 'conv_dw_b', 'conv_ln_g', 'conv_ln_b', 'conv_pw2_w', 'conv_pw2_b', 'w_kv', 'attn_wq', 'attn_wo', 'rel_bias', 'mlp_w1', 'mlp_w2', 'ln_mix_g', 'ln_mix_b', 'ln_mlp_g', 'ln_mlp_b']
TWIN_WEIGHTS = ['conv_pw1_w', 'conv_pw1_b', 'conv_dw_w', 'conv_dw_b', 'conv_ln_g', 'conv_ln_b', 'conv_pw2_w', 'conv_pw2_b', 'w_kv', 'attn_wq', 'attn_wo', 'rel_bias', 'mlp_w1', 'mlp_w2', 'ln_mix_g', 'ln_mix_b', 'ln_mlp_g', 'ln_mlp_b']
TWIN_DIFF_INPUT = 'x'
TWIN_INPUTS = ['x', 'conv_pw1_w', 'conv_pw1_b', 'conv_dw_w', 'conv_dw_b', 'conv_ln_g', 'conv_ln_b', 'conv_pw2_w', 'conv_pw2_b', 'w_kv', 'attn_wq', 'attn_wo', 'rel_bias', 'mlp_w1', 'mlp_w2', 'ln_mix_g', 'ln_mix_b', 'ln_mlp_g', 'ln_mlp_b', 'loss_target', 'm_conv_pw1_w', 'm_conv_pw1_b', 'm_conv_dw_w', 'm_conv_dw_b', 'm_conv_ln_g', 'm_conv_ln_b', 'm_conv_pw2_w', 'm_conv_pw2_b', 'm_w_kv', 'm_attn_wq', 'm_attn_wo', 'm_rel_bias', 'm_mlp_w1', 'm_mlp_w2', 'm_ln_mix_g', 'm_ln_mix_b', 'm_ln_mlp_g', 'm_ln_mlp_b', 'v_conv_pw1_w', 'v_conv_pw1_b', 'v_conv_dw_w', 'v_conv_dw_b', 'v_conv_ln_g', 'v_conv_ln_b', 'v_conv_pw2_w', 'v_conv_pw2_b', 'v_w_kv', 'v_attn_wq', 'v_attn_wo', 'v_rel_bias', 'v_mlp_w1', 'v_mlp_w2', 'v_ln_mix_g', 'v_ln_mix_b', 'v_ln_mlp_g', 'v_ln_mlp_b']
TWIN_OUTPUTS = ['loss', 'grad_x', 'grad_conv_pw1_w', 'grad_conv_pw1_b', 'grad_conv_dw_w', 'grad_conv_dw_b', 'grad_conv_ln_g', 'grad_conv_ln_b', 'grad_conv_pw2_w', 'grad_conv_pw2_b', 'grad_w_kv', 'grad_attn_wq', 'grad_attn_wo', 'grad_rel_bias', 'grad_mlp_w1', 'grad_mlp_w2', 'grad_ln_mix_g', 'grad_ln_mix_b', 'grad_ln_mlp_g', 'grad_ln_mlp_b', 'delta_conv_pw1_w', 'delta_conv_pw1_b', 'delta_conv_dw_w', 'delta_conv_dw_b', 'delta_conv_ln_g', 'delta_conv_ln_b', 'delta_conv_pw2_w', 'delta_conv_pw2_b', 'delta_w_kv', 'delta_attn_wq', 'delta_attn_wo', 'delta_rel_bias', 'delta_mlp_w1', 'delta_mlp_w2', 'delta_ln_mix_g', 'delta_ln_mix_b', 'delta_ln_mlp_g', 'delta_ln_mlp_b', 'new_m_conv_pw1_w', 'new_m_conv_pw1_b', 'new_m_conv_dw_w', 'new_m_conv_dw_b', 'new_m_conv_ln_g', 'new_m_conv_ln_b', 'new_m_conv_pw2_w', 'new_m_conv_pw2_b', 'new_m_w_kv', 'new_m_attn_wq', 'new_m_attn_wo', 'new_m_rel_bias', 'new_m_mlp_w1', 'new_m_mlp_w2', 'new_m_ln_mix_g', 'new_m_ln_mix_b', 'new_m_ln_mlp_g', 'new_m_ln_mlp_b', 'new_v_conv_pw1_w', 'new_v_conv_pw1_b', 'new_v_conv_dw_w', 'new_v_conv_dw_b', 'new_v_conv_ln_g', 'new_v_conv_ln_b', 'new_v_conv_pw2_w', 'new_v_conv_pw2_b', 'new_v_w_kv', 'new_v_attn_wq', 'new_v_attn_wo', 'new_v_rel_bias', 'new_v_mlp_w1', 'new_v_mlp_w2', 'new_v_ln_mix_g', 'new_v_ln_mix_b', 'new_v_ln_mlp_g', 'new_v_ln_mlp_b']
TWIN_LEAF_KINDS = {'loss': 'loss', 'grad_x': 'grad_x', 'grad_conv_pw1_w': 'grad_w', 'grad_conv_pw1_b': 'grad_w', 'grad_conv_dw_w': 'grad_w', 'grad_conv_dw_b': 'grad_w', 'grad_conv_ln_g': 'grad_w', 'grad_conv_ln_b': 'grad_w', 'grad_conv_pw2_w': 'grad_w', 'grad_conv_pw2_b': 'grad_w', 'grad_w_kv': 'grad_w', 'grad_attn_wq': 'grad_w', 'grad_attn_wo': 'grad_w', 'grad_rel_bias': 'grad_w', 'grad_mlp_w1': 'grad_w', 'grad_mlp_w2': 'grad_w', 'grad_ln_mix_g': 'grad_w', 'grad_ln_mix_b': 'grad_w', 'grad_ln_mlp_g': 'grad_w', 'grad_ln_mlp_b': 'grad_w', 'delta_conv_pw1_w': 'delta_w', 'delta_conv_pw1_b': 'delta_w', 'delta_conv_dw_w': 'delta_w', 'delta_conv_dw_b': 'delta_w', 'delta_conv_ln_g': 'delta_w', 'delta_conv_ln_b': 'delta_w', 'delta_conv_pw2_w': 'delta_w', 'delta_conv_pw2_b': 'delta_w', 'delta_w_kv': 'delta_w', 'delta_attn_wq': 'delta_w', 'delta_attn_wo': 'delta_w', 'delta_rel_bias': 'delta_w', 'delta_mlp_w1': 'delta_w', 'delta_mlp_w2': 'delta_w', 'delta_ln_mix_g': 'delta_w', 'delta_ln_mix_b': 'delta_w', 'delta_ln_mlp_g': 'delta_w', 'delta_ln_mlp_b': 'delta_w', 'new_m_conv_pw1_w': 'new_m', 'new_m_conv_pw1_b': 'new_m', 'new_m_conv_dw_w': 'new_m', 'new_m_conv_dw_b': 'new_m', 'new_m_conv_ln_g': 'new_m', 'new_m_conv_ln_b': 'new_m', 'new_m_conv_pw2_w': 'new_m', 'new_m_conv_pw2_b': 'new_m', 'new_m_w_kv': 'new_m', 'new_m_attn_wq': 'new_m', 'new_m_attn_wo': 'new_m', 'new_m_rel_bias': 'new_m', 'new_m_mlp_w1': 'new_m', 'new_m_mlp_w2': 'new_m', 'new_m_ln_mix_g': 'new_m', 'new_m_ln_mix_b': 'new_m', 'new_m_ln_mlp_g': 'new_m', 'new_m_ln_mlp_b': 'new_m', 'new_v_conv_pw1_w': 'new_v', 'new_v_conv_pw1_b': 'new_v', 'new_v_conv_dw_w': 'new_v', 'new_v_conv_dw_b': 'new_v', 'new_v_conv_ln_g': 'new_v', 'new_v_conv_ln_b': 'new_v', 'new_v_conv_pw2_w': 'new_v', 'new_v_conv_pw2_b': 'new_v', 'new_v_w_kv': 'new_v', 'new_v_attn_wq': 'new_v', 'new_v_attn_wo': 'new_v', 'new_v_rel_bias': 'new_v', 'new_v_mlp_w1': 'new_v', 'new_v_mlp_w2': 'new_v', 'new_v_ln_mix_g': 'new_v', 'new_v_ln_mix_b': 'new_v', 'new_v_ln_mlp_g': 'new_v', 'new_v_ln_mlp_b': 'new_v'}


def _forward(args):
    return _fwd_reference(*[args[k] for k in FWD_PARAMS])


def _output_shape():
    def fwd():
        inp = _fwd_setup_inputs(0)
        return _fwd_reference(*[inp[k] for k in FWD_PARAMS])
    out = _jax.eval_shape(fwd)
    return out.shape, out.dtype

N_MICROBATCH = 1
ADAM_LR = 0.001
ADAM_B1 = 0.9
ADAM_B2 = 0.999
ADAM_EPS = 1e-08
ADAM_WD = 0.01
ADAM_STEP = 10
PER_EXAMPLE_BATCH_AXIS = {'x': 0, 'loss_target': 0}
SHARED_INPUTS = []
_WEIGHT_DTYPES = {'conv_pw1_w': _jnp.float32, 'conv_pw1_b': _jnp.float32, 'conv_dw_w': _jnp.float32, 'conv_dw_b': _jnp.float32, 'conv_ln_g': _jnp.float32, 'conv_ln_b': _jnp.float32, 'conv_pw2_w': _jnp.float32, 'conv_pw2_b': _jnp.float32, 'w_kv': _jnp.float32, 'attn_wq': _jnp.float32, 'attn_wo': _jnp.float32, 'rel_bias': _jnp.float32, 'mlp_w1': _jnp.float32, 'mlp_w2': _jnp.float32, 'ln_mix_g': _jnp.float32, 'ln_mix_b': _jnp.float32, 'ln_mlp_g': _jnp.float32, 'ln_mlp_b': _jnp.float32}
MOMENT_SCALE = {'conv_pw1_w': 2.049717e-02, 'conv_pw1_b': 5.524299e-02, 'conv_dw_w': 2.812510e-02, 'conv_dw_b': 1.427303e-01, 'conv_ln_g': 6.266385e-02, 'conv_ln_b': 8.778117e-02, 'conv_pw2_w': 7.505244e-02, 'conv_pw2_b': 3.519246e-01, 'w_kv': 8.780192e-03, 'attn_wq': 4.165650e-03, 'attn_wo': 1.173241e-02, 'rel_bias': 7.981934e-03, 'mlp_w1': 1.574380e-02, 'mlp_w2': 4.345915e-02, 'ln_mix_g': 1.146655e+00, 'ln_mix_b': 5.055541e-01, 'ln_mlp_g': 2.267266e+01, 'ln_mlp_b': 2.209567e+00}


def _to_microbatches(a, axis):
    t = _jnp.moveaxis(a, axis, 0)
    t = t.reshape((N_MICROBATCH, t.shape[0] // N_MICROBATCH) + t.shape[1:])
    return _jnp.moveaxis(t, 1, axis + 1)


def setup_inputs(seed: int = 0) -> dict:
    inp = _fwd_setup_inputs(seed)
    key = _jax.random.fold_in(_jax.random.key(seed), 7919)
    shape, _ = _output_shape()
    out = dict(inp)
    out["loss_target"] = _jax.random.normal(_jax.random.fold_in(key, 0), shape, _jnp.float32)
    for i, name in enumerate(TWIN_WEIGHTS):
        w = inp[name].astype(_jnp.float32)
        if MOMENT_SCALE is None:
            s = _jnp.sqrt(_jnp.mean(_jnp.square(w)) + 1e-30)
        else:
            s = MOMENT_SCALE[name]
        km, kv = _jax.random.split(_jax.random.fold_in(key, i + 1))
        out[name] = w
        out["m_" + name] = s * _jax.random.normal(km, w.shape, _jnp.float32)
        out["v_" + name] = (s * s) * _jax.random.uniform(kv, w.shape, _jnp.float32, 0.5, 1.5)
    if N_MICROBATCH > 1:
        for name, axis in PER_EXAMPLE_BATCH_AXIS.items():
            out[name] = _to_microbatches(out[name], axis)
    return {'x': out['x'], 'conv_pw1_w': out['conv_pw1_w'], 'conv_pw1_b': out['conv_pw1_b'], 'conv_dw_w': out['conv_dw_w'], 'conv_dw_b': out['conv_dw_b'], 'conv_ln_g': out['conv_ln_g'], 'conv_ln_b': out['conv_ln_b'], 'conv_pw2_w': out['conv_pw2_w'], 'conv_pw2_b': out['conv_pw2_b'], 'w_kv': out['w_kv'], 'attn_wq': out['attn_wq'], 'attn_wo': out['attn_wo'], 'rel_bias': out['rel_bias'], 'mlp_w1': out['mlp_w1'], 'mlp_w2': out['mlp_w2'], 'ln_mix_g': out['ln_mix_g'], 'ln_mix_b': out['ln_mix_b'], 'ln_mlp_g': out['ln_mlp_g'], 'ln_mlp_b': out['ln_mlp_b'], 'loss_target': out['loss_target'], 'm_conv_pw1_w': out['m_conv_pw1_w'], 'm_conv_pw1_b': out['m_conv_pw1_b'], 'm_conv_dw_w': out['m_conv_dw_w'], 'm_conv_dw_b': out['m_conv_dw_b'], 'm_conv_ln_g': out['m_conv_ln_g'], 'm_conv_ln_b': out['m_conv_ln_b'], 'm_conv_pw2_w': out['m_conv_pw2_w'], 'm_conv_pw2_b': out['m_conv_pw2_b'], 'm_w_kv': out['m_w_kv'], 'm_attn_wq': out['m_attn_wq'], 'm_attn_wo': out['m_attn_wo'], 'm_rel_bias': out['m_rel_bias'], 'm_mlp_w1': out['m_mlp_w1'], 'm_mlp_w2': out['m_mlp_w2'], 'm_ln_mix_g': out['m_ln_mix_g'], 'm_ln_mix_b': out['m_ln_mix_b'], 'm_ln_mlp_g': out['m_ln_mlp_g'], 'm_ln_mlp_b': out['m_ln_mlp_b'], 'v_conv_pw1_w': out['v_conv_pw1_w'], 'v_conv_pw1_b': out['v_conv_pw1_b'], 'v_conv_dw_w': out['v_conv_dw_w'], 'v_conv_dw_b': out['v_conv_dw_b'], 'v_conv_ln_g': out['v_conv_ln_g'], 'v_conv_ln_b': out['v_conv_ln_b'], 'v_conv_pw2_w': out['v_conv_pw2_w'], 'v_conv_pw2_b': out['v_conv_pw2_b'], 'v_w_kv': out['v_w_kv'], 'v_attn_wq': out['v_attn_wq'], 'v_attn_wo': out['v_attn_wo'], 'v_rel_bias': out['v_rel_bias'], 'v_mlp_w1': out['v_mlp_w1'], 'v_mlp_w2': out['v_mlp_w2'], 'v_ln_mix_g': out['v_ln_mix_g'], 'v_ln_mix_b': out['v_ln_mix_b'], 'v_ln_mlp_g': out['v_ln_mlp_g'], 'v_ln_mlp_b': out['v_ln_mlp_b']}


def _loss(weights, diff, rest, loss_target):
    with _jax.named_scope("forward"):
        args = {**rest, TWIN_DIFF_INPUT: diff, **{k: w.astype(_WEIGHT_DTYPES[k]) for k, w in weights.items()}}
        y = _forward(args)
    with _jax.named_scope("loss_head"):
        err = _jnp.square(y.astype(_jnp.float32) - loss_target)
        return 0.5 * _jnp.sum(_jnp.mean(err, axis=-1)) if err.ndim else 0.5 * err


def _adamw(w, g, m, v):
    m = ADAM_B1 * m + (1.0 - ADAM_B1) * g
    v = ADAM_B2 * v + (1.0 - ADAM_B2) * _jnp.square(g)
    m_hat = m / (1.0 - ADAM_B1 ** ADAM_STEP)
    v_hat = v / (1.0 - ADAM_B2 ** ADAM_STEP)
    delta = -ADAM_LR * (m_hat / (_jnp.sqrt(v_hat) + ADAM_EPS) + ADAM_WD * w)
    return delta, m, v


def reference(x, conv_pw1_w, conv_pw1_b, conv_dw_w, conv_dw_b, conv_ln_g, conv_ln_b, conv_pw2_w, conv_pw2_b, w_kv, attn_wq, attn_wo, rel_bias, mlp_w1, mlp_w2, ln_mix_g, ln_mix_b, ln_mlp_g, ln_mlp_b, loss_target, m_conv_pw1_w, m_conv_pw1_b, m_conv_dw_w, m_conv_dw_b, m_conv_ln_g, m_conv_ln_b, m_conv_pw2_w, m_conv_pw2_b, m_w_kv, m_attn_wq, m_attn_wo, m_rel_bias, m_mlp_w1, m_mlp_w2, m_ln_mix_g, m_ln_mix_b, m_ln_mlp_g, m_ln_mlp_b, v_conv_pw1_w, v_conv_pw1_b, v_conv_dw_w, v_conv_dw_b, v_conv_ln_g, v_conv_ln_b, v_conv_pw2_w, v_conv_pw2_b, v_w_kv, v_attn_wq, v_attn_wo, v_rel_bias, v_mlp_w1, v_mlp_w2, v_ln_mix_g, v_ln_mix_b, v_ln_mlp_g, v_ln_mlp_b):
    given = dict(x=x, conv_pw1_w=conv_pw1_w, conv_pw1_b=conv_pw1_b, conv_dw_w=conv_dw_w, conv_dw_b=conv_dw_b, conv_ln_g=conv_ln_g, conv_ln_b=conv_ln_b, conv_pw2_w=conv_pw2_w, conv_pw2_b=conv_pw2_b, w_kv=w_kv, attn_wq=attn_wq, attn_wo=attn_wo, rel_bias=rel_bias, mlp_w1=mlp_w1, mlp_w2=mlp_w2, ln_mix_g=ln_mix_g, ln_mix_b=ln_mix_b, ln_mlp_g=ln_mlp_g, ln_mlp_b=ln_mlp_b, loss_target=loss_target, m_conv_pw1_w=m_conv_pw1_w, m_conv_pw1_b=m_conv_pw1_b, m_conv_dw_w=m_conv_dw_w, m_conv_dw_b=m_conv_dw_b, m_conv_ln_g=m_conv_ln_g, m_conv_ln_b=m_conv_ln_b, m_conv_pw2_w=m_conv_pw2_w, m_conv_pw2_b=m_conv_pw2_b, m_w_kv=m_w_kv, m_attn_wq=m_attn_wq, m_attn_wo=m_attn_wo, m_rel_bias=m_rel_bias, m_mlp_w1=m_mlp_w1, m_mlp_w2=m_mlp_w2, m_ln_mix_g=m_ln_mix_g, m_ln_mix_b=m_ln_mix_b, m_ln_mlp_g=m_ln_mlp_g, m_ln_mlp_b=m_ln_mlp_b, v_conv_pw1_w=v_conv_pw1_w, v_conv_pw1_b=v_conv_pw1_b, v_conv_dw_w=v_conv_dw_w, v_conv_dw_b=v_conv_dw_b, v_conv_ln_g=v_conv_ln_g, v_conv_ln_b=v_conv_ln_b, v_conv_pw2_w=v_conv_pw2_w, v_conv_pw2_b=v_conv_pw2_b, v_w_kv=v_w_kv, v_attn_wq=v_attn_wq, v_attn_wo=v_attn_wo, v_rel_bias=v_rel_bias, v_mlp_w1=v_mlp_w1, v_mlp_w2=v_mlp_w2, v_ln_mix_g=v_ln_mix_g, v_ln_mix_b=v_ln_mix_b, v_ln_mlp_g=v_ln_mlp_g, v_ln_mlp_b=v_ln_mlp_b)
    weights = {n: given[n] for n in TWIN_WEIGHTS}
    shared = {n: given[n] for n in SHARED_INPUTS}
    per_example = {n: given[n] for n in ['x']}
    grad_fn = _jax.value_and_grad(_loss, argnums=(0, 1))

    def one_microbatch(ex, loss_target):
        ex = dict(ex)
        diff = ex.pop(TWIN_DIFF_INPUT)
        return grad_fn(weights, diff, {**shared, **ex}, loss_target)

    if N_MICROBATCH == 1:
        loss, (grad_w, grad_x) = one_microbatch(per_example, given["loss_target"])
    else:
        def body(carry, xs):
            loss_sum, grad_sum = carry
            l_k, (gw_k, gx_k) = one_microbatch(xs[0], xs[1])
            with _jax.named_scope("update"):
                return (loss_sum + l_k, _jax.tree.map(_jnp.add, grad_sum, gw_k)), gx_k

        init = (_jnp.zeros((), _jnp.float32), _jax.tree.map(_jnp.zeros_like, weights))
        (loss, grad_w), grad_x = _jax.lax.scan(body, init, (per_example, given["loss_target"]))
    with _jax.named_scope("update"):
        delta_w, new_m, new_v = {}, {}, {}
        for n in TWIN_WEIGHTS:
            delta_w[n], new_m[n], new_v[n] = _adamw(weights[n], grad_w[n], given["m_" + n], given["v_" + n])
    return (loss, grad_x, *[grad_w[n] for n in TWIN_WEIGHTS], *[delta_w[n] for n in TWIN_WEIGHTS],
            *[new_m[n] for n in TWIN_WEIGHTS], *[new_v[n] for n in TWIN_WEIGHTS])
```

```python
import functools
import math

import numpy as np
import jax
import jax.numpy as jnp
from jax import lax
from jax.experimental import pallas as pl
from jax.experimental.pallas import tpu as pltpu

F32 = jnp.float32
BF16 = jnp.bfloat16

HEAD_DIM = 128
BAND = 128
BRANCHES = ((128, 1), (512, 4), (2048, 16))
CONV_WIDTH = 31
CONV_HALO = 32
REL_BUCKETS = 32
REL_MAX_DIST = 2048
DEPTH = 2
ALPHA = (2 * DEPTH) ** 0.25
LN_EPS = 1e-5
ADAM_LR, ADAM_B1, ADAM_B2, ADAM_EPS, ADAM_WD, ADAM_STEP = 0.001, 0.9, 0.999, 1e-08, 0.01, 10

N_CHIPS = 4
N_DEV = 8
MESH = pl.DeviceIdType.MESH
VMEM_LIMIT_BYTES = 56 * 1024 * 1024
MM_TM, MM_TN, MM_TK = 1024, 1024, 512
ROW_TILE = 256
CONV_TILE = 128
ATTN_CHUNK = 1024
NEG_BIG = -1e30


def _cparams(sem):
    return pltpu.CompilerParams(dimension_semantics=sem, vmem_limit_bytes=VMEM_LIMIT_BYTES)


def _sigmoid(x):
    return 1.0 / (1.0 + jnp.exp(-x))


def _wspec(wshape, axis, br, bc, rsel, csel):
    _, R, C = wshape
    if axis == "col":
        nb = C // bc
        assert nb * bc == C, (wshape, bc)
        return pl.BlockSpec((None, br, bc), lambda *g: (csel(*g) // nb, rsel(*g), csel(*g) % nb))
    nb = R // br
    assert nb * br == R, (wshape, br)
    return pl.BlockSpec((None, br, bc), lambda *g: (rsel(*g) // nb, rsel(*g) % nb, csel(*g)))


def _full_dims(wshape, axis):
    _, R, C = wshape
    return (R, N_CHIPS * C) if axis == "col" else (N_CHIPS * R, C)


def _mm_body(nk, n_extra, n_out, dims, epilogue):
    def body(*refs):
        a_ref, b_ref = refs[0], refs[1]
        extra = refs[2:2 + n_extra]
        outs = refs[2 + n_extra:2 + n_extra + n_out]
        acc_ref = refs[2 + n_extra + n_out]
        k = pl.program_id(2)
        part = lax.dot_general(a_ref[...].astype(BF16), b_ref[...].astype(BF16), (dims, ((), ())),
                               preferred_element_type=F32)

        @pl.when(k == 0)
        def _():
            acc_ref[...] = part

        @pl.when(k > 0)
        def _():
            acc_ref[...] += part

        @pl.when(k == nk - 1)
        def _():
            res = epilogue(acc_ref[...], *[e[...] for e in extra])
            for r, o in zip(res, outs):
                o[...] = r.astype(o.dtype)
    return body


def _extra_specs(extras, tm, tn):
    specs = []
    for arr, kind in extras:
        if kind == "tile":
            specs.append(pl.BlockSpec((tm, tn), lambda i, j, k: (i, j)))
        else:
            specs.append(pl.BlockSpec((1, tn), lambda i, j, k: (0, j)))
    return specs


def mm_nn(name, a, w, axis, epilogue, out_dtypes, extras=()):
    M, K = a.shape
    Kw, N = _full_dims(w.shape, axis)
    assert K == Kw
    tm, tn, tk = min(MM_TM, M), min(MM_TN, N), min(MM_TK, K)
    if axis == "col":
        tn = min(tn, w.shape[2])
    else:
        tk = min(tk, w.shape[1])
    nk = K // tk
    in_specs = [pl.BlockSpec((tm, tk), lambda i, j, k: (i, k)),
                _wspec(w.shape, axis, tk, tn, lambda i, j, k: k, lambda i, j, k: j)]
    in_specs += _extra_specs(extras, tm, tn)
    body = _mm_body(nk, len(extras), len(out_dtypes), ((1,), (0,)), epilogue)
    return pl.pallas_call(
        body, name=name, grid=(M // tm, N // tn, nk), in_specs=in_specs,
        out_specs=[pl.BlockSpec((tm, tn), lambda i, j, k: (i, j)) for _ in out_dtypes],
        out_shape=[jax.ShapeDtypeStruct((M, N), d) for d in out_dtypes],
        scratch_shapes=[pltpu.VMEM((tm, tn), F32)],
        compiler_params=_cparams(("parallel", "parallel", "arbitrary")),
    )(a, w, *[e for e, _ in extras])


def mm_nt(name, g, w, axis, epilogue, out_dtypes, extras=()):
    M, N = g.shape
    K, Nw = _full_dims(w.shape, axis)
    assert N == Nw
    tm, tn, tk = min(MM_TM, M), min(MM_TN, K), min(MM_TK, N)
    if axis == "col":
        tk = min(tk, w.shape[2])
    else:
        tn = min(tn, w.shape[1])
    nk = N // tk
    in_specs = [pl.BlockSpec((tm, tk), lambda i, j, k: (i, k)),
                _wspec(w.shape, axis, tn, tk, lambda i, j, k: j, lambda i, j, k: k)]
    in_specs += _extra_specs(extras, tm, tn)
    body = _mm_body(nk, len(extras), len(out_dtypes), ((1,), (1,)), epilogue)
    return pl.pallas_call(
        body, name=name, grid=(M // tm, K // tn, nk), in_specs=in_specs,
        out_specs=[pl.BlockSpec((tm, tn), lambda i, j, k: (i, j)) for _ in out_dtypes],
        out_shape=[jax.ShapeDtypeStruct((M, K), d) for d in out_dtypes],
        scratch_shapes=[pltpu.VMEM((tm, tn), F32)],
        compiler_params=_cparams(("parallel", "parallel", "arbitrary")),
    )(g, w, *[e for e, _ in extras])


def mm_tn(name, a, g, wshape, axis):
    M, K = a.shape
    Mg, N = g.shape
    assert M == Mg and (K, N) == _full_dims(wshape, axis)
    tm, tn, tk = min(MM_TM, K), min(MM_TN, N), min(MM_TK, M)
    if axis == "col":
        tn = min(tn, wshape[2])
    else:
        tm = min(tm, wshape[1])
    nk = M // tk
    body = _mm_body(nk, 0, 1, ((0,), (0,)), lambda acc: (acc,))
    return pl.pallas_call(
        body, name=name, grid=(K // tm, N // tn, nk),
        in_specs=[pl.BlockSpec((tk, tm), lambda i, j, k: (k, i)),
                  pl.BlockSpec((tk, tn), lambda i, j, k: (k, j))],
        out_specs=[_wspec(wshape, axis, tm, tn, lambda i, j, k: i, lambda i, j, k: j)],
        out_shape=[jax.ShapeDtypeStruct(wshape, F32)],
        scratch_shapes=[pltpu.VMEM((tm, tn), F32)],
        compiler_params=_cparams(("parallel", "parallel", "arbitrary")),
    )(a, g)[0]


def _row_spec(tr, width):
    return pl.BlockSpec((tr, width), lambda i: (i, 0))


def _vec_spec(width):
    return pl.BlockSpec((1, width), lambda i: (0, 0))


def _fold8(x):
    r, d = x.shape
    return jnp.sum(x.reshape(r // 8, 8, d), axis=0)


def ln_fwd(name, f, prev, prev_g=None, prev_b=None):
    T, D = f.shape
    tr = min(ROW_TILE, T)
    affine = prev_g is not None

    def body(*refs):
        if affine:
            f_ref, p_ref, pg_ref, pb_ref, g_ref, b_ref, xhat_ref, rstd_ref, xbf_ref = refs
            xprev = p_ref[...] * pg_ref[...] + pb_ref[...]
        else:
            f_ref, p_ref, g_ref, b_ref, xhat_ref, rstd_ref, xbf_ref = refs
            xprev = p_ref[...]
        r = ALPHA * xprev + f_ref[...]
        mu = jnp.mean(r, axis=-1, keepdims=True)
        cen = r - mu
        var = jnp.mean(cen * cen, axis=-1, keepdims=True)
        rstd = lax.rsqrt(var + LN_EPS)
        xhat = cen * rstd
        xhat_ref[...] = xhat
        rstd_ref[...] = rstd
        xbf_ref[...] = (xhat * g_ref[...] + b_ref[...]).astype(BF16)

    def call(g, b):
        ins = [f, prev] + ([prev_g, prev_b] if affine else []) + [g, b]
        specs = [_row_spec(tr, D), _row_spec(tr, D)] + ([_vec_spec(D)] * 2 if affine else []) + [_vec_spec(D)] * 2
        return pl.pallas_call(
            body, name=name, grid=(T // tr,), in_specs=specs,
            out_specs=[_row_spec(tr, D), _row_spec(tr, 1), _row_spec(tr, D)],
            out_shape=[jax.ShapeDtypeStruct((T, D), F32), jax.ShapeDtypeStruct((T, 1), F32),
                       jax.ShapeDtypeStruct((T, D), BF16)],
            compiler_params=_cparams(("parallel",)),
        )(*ins)
    return call


def ln_bwd(name, xhat, rstd, gamma, dy=None, target=None, beta=None):
    T, D = xhat.shape
    tr = min(ROW_TILE, T)
    nt = T // tr
    head = target is not None

    def body(*refs):
        if head:
            xhat_ref, rstd_ref, g_ref, tgt_ref, b_ref = refs[:5]
            outs = refs[5:]
        else:
            xhat_ref, rstd_ref, g_ref, dy_ref = refs[:4]
            outs = refs[4:]
        dr_ref, drbf_ref, dg_ref, db_ref, cs_ref = outs[:5]
        rest = outs[5:]
        if head:
            loss_ref, acc_ref = rest
        else:
            (acc_ref,) = rest
        i = pl.program_id(0)
        xhat_v = xhat_ref[...]
        gam = g_ref[...]
        if head:
            diff = xhat_v * gam + b_ref[...] - tgt_ref[...]
            dyv = diff * (1.0 / D)
        else:
            dyv = dy_ref[...]
        dxh = dyv * gam
        m1 = jnp.mean(dxh, axis=-1, keepdims=True)
        m2 = jnp.mean(dxh * xhat_v, axis=-1, keepdims=True)
        dr = rstd_ref[...] * (dxh - m1 - xhat_v * m2)
        dr_ref[...] = dr
        drbf_ref[...] = dr.astype(BF16)

        @pl.when(i == 0)
        def _():
            acc_ref[...] = jnp.zeros_like(acc_ref)

        acc_ref[0] += _fold8(dyv * xhat_v)
        acc_ref[1] += _fold8(dyv)
        acc_ref[2] += _fold8(dr)
        if head:
            acc_ref[3] += _fold8(diff * diff)

        @pl.when(i == nt - 1)
        def _():
            dg_ref[...] = jnp.sum(acc_ref[0], axis=0, keepdims=True)
            db_ref[...] = jnp.sum(acc_ref[1], axis=0, keepdims=True)
            cs_ref[...] = jnp.sum(acc_ref[2], axis=0, keepdims=True)
            if head:
                loss_ref[...] = jnp.sum(jnp.sum(acc_ref[3], axis=0, keepdims=True), axis=1, keepdims=True)

    ins = [xhat, rstd, gamma] + ([target, beta] if head else [dy])
    specs = [_row_spec(tr, D), _row_spec(tr, 1), _vec_spec(D)] + ([_row_spec(tr, D), _vec_spec(D)] if head else [_row_spec(tr, D)])
    out_specs = [_row_spec(tr, D), _row_spec(tr, D), _vec_spec(D), _vec_spec(D), _vec_spec(D)]
    out_shape = [jax.ShapeDtypeStruct((T, D), F32), jax.ShapeDtypeStruct((T, D), BF16)] + [jax.ShapeDtypeStruct((1, D), F32)] * 3
    if head:
        out_specs.append(pl.BlockSpec((1, 1), lambda i: (0, 0)))
        out_shape.append(jax.ShapeDtypeStruct((1, 1), F32))
    return pl.pallas_call(
        body, name=name, grid=(nt,), in_specs=specs, out_specs=out_specs, out_shape=out_shape,
        scratch_shapes=[pltpu.VMEM((4, 8, D), F32)],
        compiler_params=_cparams(("arbitrary",)),
    )(*ins)


CONV_ROWS, CONV_COLS = 64, 512


def _tap_chunks(tt, D):
    for r0 in range(0, tt, min(CONV_ROWS, tt)):
        for c0 in range(0, D, min(CONV_COLS, D)):
            yield r0, min(CONV_ROWS, tt), c0, min(CONV_COLS, D)


def conv_fwd(name, h1, dw, dwb, lng, lnb):
    T, D2 = h1.shape
    D = D2 // 2
    tt = min(CONV_TILE, T)
    hb = tt // CONV_HALO
    KW = dw.shape[0]
    lead = CONV_HALO - (KW - 1)

    def body(a_ref, g_ref, ah_ref, gh_ref, dw_ref, dwb_ref, lng_ref, lnb_ref, u_ref, c_ref, s_ref, ext_ref):
        i = pl.program_id(0)
        u = a_ref[...] * _sigmoid(g_ref[...])
        u_ref[...] = u
        uh = ah_ref[...] * _sigmoid(gh_ref[...])
        ext_ref[pl.ds(0, CONV_HALO), :] = jnp.where(i > 0, uh, 0.0)
        ext_ref[pl.ds(CONV_HALO, tt), :] = u
        for r0, nr, c0, nc in _tap_chunks(tt, D):
            acc = jnp.zeros((nr, nc), F32) + dwb_ref[:, pl.ds(c0, nc)]
            for k in range(KW):
                acc = acc + dw_ref[pl.ds(k, 1), pl.ds(c0, nc)] * ext_ref[pl.ds(r0 + lead + k, nr), pl.ds(c0, nc)]
            c_ref[pl.ds(r0, nr), pl.ds(c0, nc)] = acc
        c = c_ref[...]
        mu = jnp.mean(c, axis=-1, keepdims=True)
        cen = c - mu
        var = jnp.mean(cen * cen, axis=-1, keepdims=True)
        n = cen * lax.rsqrt(var + LN_EPS) * lng_ref[...] + lnb_ref[...]
        s_ref[...] = (n * _sigmoid(n)).astype(BF16)

    halo = lambda col: pl.BlockSpec((CONV_HALO, D), lambda i: (jnp.maximum(i * hb - 1, 0), col))
    return pl.pallas_call(
        body, name=name, grid=(T // tt,),
        in_specs=[pl.BlockSpec((tt, D), lambda i: (i, 0)), pl.BlockSpec((tt, D), lambda i: (i, 1)), halo(0), halo(1),
                  pl.BlockSpec((KW, D), lambda i: (0, 0)), _vec_spec(D), _vec_spec(D), _vec_spec(D)],
        out_specs=[_row_spec(tt, D)] * 3,
        out_shape=[jax.ShapeDtypeStruct((T, D), F32), jax.ShapeDtypeStruct((T, D), F32), jax.ShapeDtypeStruct((T, D), BF16)],
        scratch_shapes=[pltpu.VMEM((tt + CONV_HALO, D), F32)],
        compiler_params=_cparams(("parallel",)),
    )(h1, h1, h1, h1, dw, dwb, lng, lnb)


def conv_bwd_ln(name, ds, c, lng, lnb):
    T, D = c.shape
    tr = min(ROW_TILE, T)
    nt = T // tr

    def body(ds_ref, c_ref, g_ref, b_ref, dc_ref, dg_ref, db_ref, cs_ref, acc_ref):
        i = pl.program_id(0)
        cv = c_ref[...]
        mu = jnp.mean(cv, axis=-1, keepdims=True)
        cen = cv - mu
        var = jnp.mean(cen * cen, axis=-1, keepdims=True)
        rstd = lax.rsqrt(var + LN_EPS)
        chat = cen * rstd
        n = chat * g_ref[...] + b_ref[...]
        sg = _sigmoid(n)
        dn = ds_ref[...] * (sg * (1.0 + n * (1.0 - sg)))
        dxh = dn * g_ref[...]
        m1 = jnp.mean(dxh, axis=-1, keepdims=True)
        m2 = jnp.mean(dxh * chat, axis=-1, keepdims=True)
        dc = rstd * (dxh - m1 - chat * m2)
        dc_ref[...] = dc

        @pl.when(i == 0)
        def _():
            acc_ref[...] = jnp.zeros_like(acc_ref)

        acc_ref[0] += _fold8(dn * chat)
        acc_ref[1] += _fold8(dn)
        acc_ref[2] += _fold8(dc)

        @pl.when(i == nt - 1)
        def _():
            dg_ref[...] = jnp.sum(acc_ref[0], axis=0, keepdims=True)
            db_ref[...] = jnp.sum(acc_ref[1], axis=0, keepdims=True)
            cs_ref[...] = jnp.sum(acc_ref[2], axis=0, keepdims=True)

    return pl.pallas_call(
        body, name=name, grid=(nt,),
        in_specs=[_row_spec(tr, D), _row_spec(tr, D), _vec_spec(D), _vec_spec(D)],
        out_specs=[_row_spec(tr, D), _vec_spec(D), _vec_spec(D), _vec_spec(D)],
        out_shape=[jax.ShapeDtypeStruct((T, D), F32)] + [jax.ShapeDtypeStruct((1, D), F32)] * 3,
        scratch_shapes=[pltpu.VMEM((3, 8, D), F32)],
        compiler_params=_cparams(("arbitrary",)),
    )(ds, c, lng, lnb)


def conv_bwd_taps(name, dc, u, h1, dw):
    T, D = dc.shape
    tt = min(CONV_TILE, T)
    nt = T // tt
    hb = tt // CONV_HALO
    nhb = T // CONV_HALO
    KW = dw.shape[0]
    lead = CONV_HALO - (KW - 1)

    def body(dc_ref, dcn_ref, u_ref, uh_ref, a_ref, g_ref, dw_ref, dh1_ref, db1_ref, ddw_ref,
             edc_ref, eu_ref, du_ref, accw_ref, accb_ref):
        i = pl.program_id(0)

        @pl.when(i == 0)
        def _():
            accw_ref[...] = jnp.zeros_like(accw_ref)
            accb_ref[...] = jnp.zeros_like(accb_ref)

        edc_ref[pl.ds(0, tt), :] = dc_ref[...]
        edc_ref[pl.ds(tt, CONV_HALO), :] = jnp.where(i < nt - 1, dcn_ref[...], 0.0)
        eu_ref[pl.ds(0, CONV_HALO), :] = jnp.where(i > 0, uh_ref[...], 0.0)
        eu_ref[pl.ds(CONV_HALO, tt), :] = u_ref[...]
        for r0, nr, c0, nc in _tap_chunks(tt, D):
            dcv = dc_ref[pl.ds(r0, nr), pl.ds(c0, nc)]
            acc = jnp.zeros((nr, nc), F32)
            for k in range(KW):
                acc = acc + dw_ref[pl.ds(k, 1), pl.ds(c0, nc)] * edc_ref[pl.ds(r0 + (KW - 1) - k, nr), pl.ds(c0, nc)]
                accw_ref[k, :, pl.ds(c0, nc)] += _fold8(dcv * eu_ref[pl.ds(r0 + lead + k, nr), pl.ds(c0, nc)])
            du_ref[pl.ds(r0, nr), pl.ds(c0, nc)] = acc
        du = du_ref[...]
        sg = _sigmoid(g_ref[...])
        da = du * sg
        dg = du * a_ref[...] * sg * (1.0 - sg)
        dh1_ref[:, pl.ds(0, D)] = da.astype(BF16)
        dh1_ref[:, pl.ds(D, D)] = dg.astype(BF16)
        accb_ref[:, pl.ds(0, D)] += _fold8(da)
        accb_ref[:, pl.ds(D, D)] += _fold8(dg)

        @pl.when(i == nt - 1)
        def _():
            db1_ref[...] = jnp.sum(accb_ref[...], axis=0, keepdims=True)
            ddw_ref[...] = jnp.sum(accw_ref[...], axis=1)

    return pl.pallas_call(
        body, name=name, grid=(nt,),
        in_specs=[_row_spec(tt, D),
                  pl.BlockSpec((CONV_HALO, D), lambda i: (jnp.minimum((i + 1) * hb, nhb - 1), 0)),
                  _row_spec(tt, D),
                  pl.BlockSpec((CONV_HALO, D), lambda i: (jnp.maximum(i * hb - 1, 0), 0)),
                  pl.BlockSpec((tt, D), lambda i: (i, 0)), pl.BlockSpec((tt, D), lambda i: (i, 1)),
                  pl.BlockSpec((KW, D), lambda i: (0, 0))],
        out_specs=[_row_spec(tt, 2 * D), _vec_spec(2 * D), pl.BlockSpec((KW, D), lambda i: (0, 0))],
        out_shape=[jax.ShapeDtypeStruct((T, 2 * D), BF16), jax.ShapeDtypeStruct((1, 2 * D), F32),
                   jax.ShapeDtypeStruct((KW, D), F32)],
        scratch_shapes=[pltpu.VMEM((tt + CONV_HALO, D), F32), pltpu.VMEM((tt + CONV_HALO, D), F32),
                        pltpu.VMEM((tt, D), F32), pltpu.VMEM((KW, 8, D), F32), pltpu.VMEM((8, 2 * D), F32)],
        compiler_params=_cparams(("arbitrary",)),
    )(dc, dc, u, u, h1, h1, dw)


def _t5_bucket(dist):
    max_exact = REL_BUCKETS // 2
    large = max_exact + (np.log(np.maximum(dist, 1) / max_exact) / math.log(REL_MAX_DIST / max_exact)
                         * (REL_BUCKETS - max_exact)).astype(np.int32)
    large = np.minimum(large, REL_BUCKETS - 1)
    return np.where(dist < max_exact, dist, large).astype(np.int32)


def _bucket_table(dil):
    i = np.arange(BAND)[:, None]
    j = np.arange(2 * BAND)[None, :]
    delta = i - j + BAND
    return _t5_bucket(np.clip(delta, 0, None) * dil)


def bias_expand(name, rel_bias, dil):
    n_heads = rel_bias.shape[1]
    idx = jnp.asarray(_bucket_table(dil))

    def body(rel_ref, idx_ref, out_ref):
        h = pl.program_id(0)
        idxv = idx_ref[...]
        b = jnp.zeros((BAND, 2 * BAND), F32)
        for bk in range(REL_BUCKETS):
            b = jnp.where(idxv == bk, rel_ref[bk, h], b)
        out_ref[...] = b

    return pl.pallas_call(
        body, name=name, grid=(n_heads,),
        in_specs=[pl.BlockSpec(memory_space=pltpu.SMEM), pl.BlockSpec((BAND, 2 * BAND), lambda h: (0, 0))],
        out_specs=pl.BlockSpec((None, BAND, 2 * BAND), lambda h: (h, 0, 0)),
        out_shape=jax.ShapeDtypeStruct((n_heads, BAND, 2 * BAND), F32),
        compiler_params=_cparams(("arbitrary",)),
    )(rel_bias, idx)


def relbias_grad(name, dsb_list):
    n_heads = dsb_list[0].shape[0]
    idxs = [jnp.asarray(_bucket_table(d)) for _, d in BRANCHES]
    nb = len(BRANCHES)

    def body(*refs):
        ds_refs, idx_refs, out_ref = refs[:nb], refs[nb:2 * nb], refs[2 * nb]
        lane = lax.broadcasted_iota(jnp.int32, (1, 128), 1)
        row = jnp.zeros((1, 128), F32)
        for bk in range(REL_BUCKETS):
            tot = jnp.zeros((1, 1), F32)
            for ds_ref, idx_ref in zip(ds_refs, idx_refs):
                sel = jnp.where(idx_ref[...] == bk, ds_ref[...], 0.0)
                tot = tot + jnp.sum(jnp.sum(sel, axis=0, keepdims=True), axis=1, keepdims=True)
            row = jnp.where(lane == bk, tot, row)
        out_ref[...] = row

    return pl.pallas_call(
        body, name=name, grid=(n_heads,),
        in_specs=[pl.BlockSpec((None, BAND, 2 * BAND), lambda h: (h, 0, 0))] * nb
                 + [pl.BlockSpec((BAND, 2 * BAND), lambda h: (0, 0))] * nb,
        out_specs=pl.BlockSpec((None, 1, 128), lambda h: (h, 0, 0)),
        out_shape=jax.ShapeDtypeStruct((n_heads, 1, 128), F32),
        compiler_params=_cparams(("arbitrary",)),
    )(*dsb_list, *idxs)


def _band_mask():
    i = lax.broadcasted_iota(jnp.int32, (BAND, 2 * BAND), 0)
    j = lax.broadcasted_iota(jnp.int32, (BAND, 2 * BAND), 1)
    return (j >= i) & (j <= i + BAND), j


def _rep2(x):
    return jnp.concatenate([x, x], axis=1)


def _attn_views(T, dil):
    L = T // dil
    lc = min(ATTN_CHUNK, L)
    return L, lc, L // lc, lc // BAND


def attn_fwd(name, q, kv, bias, dil):
    T, D = q.shape
    n_heads = D // HEAD_DIM
    L, lc, nchunk, nsub = _attn_views(T, dil)
    scale = HEAD_DIM ** -0.5
    q3 = q.reshape(L, dil * D)
    kv3 = kv.reshape(L, dil * 2 * D)

    def body(q_ref, k_ref, v_ref, kp_ref, vp_ref, b_ref, o_ref, lse_ref, kext_ref, vext_ref):
        c = pl.program_id(2)
        kext_ref[pl.ds(0, BAND), :] = kp_ref[...]
        vext_ref[pl.ds(0, BAND), :] = vp_ref[...]
        kext_ref[pl.ds(BAND, lc), :] = k_ref[...]
        vext_ref[pl.ds(BAND, lc), :] = v_ref[...]
        band, jcol = _band_mask()
        bias_v = b_ref[...]

        def sub(a, carry):
            off = pl.multiple_of(a * BAND, BAND)
            qa = q_ref[pl.ds(off, BAND), :]
            kw = kext_ref[pl.ds(off, 2 * BAND), :]
            vw = vext_ref[pl.ds(off, 2 * BAND), :]
            s = lax.dot_general(qa, kw, (((1,), (1,)), ((), ())), preferred_element_type=F32) * scale + bias_v
            first = jnp.logical_and(c == 0, a == 0)
            valid = band & jnp.logical_or(jcol >= BAND, jnp.logical_not(first))
            s = jnp.where(valid, s, NEG_BIG)
            m = jnp.max(s, axis=-1, keepdims=True)
            p = jnp.exp(s - m)
            den = jnp.sum(p, axis=-1, keepdims=True)
            pv = lax.dot_general(p.astype(BF16), vw, (((1,), (0,)), ((), ())), preferred_element_type=F32)
            o_ref[pl.ds(off, BAND), :] = pv / den
            lse_ref[pl.ds(off, BAND), :] = jnp.broadcast_to(m + jnp.log(den), (BAND, HEAD_DIM))
            return carry

        lax.fori_loop(0, nsub, sub, 0)

    nsb = lc // BAND
    blk = lambda w, col0: pl.BlockSpec((lc, HEAD_DIM), lambda h, r, c: (c, r * w + col0 + h))
    halo = lambda w, col0: pl.BlockSpec((BAND, HEAD_DIM), lambda h, r, c: (jnp.maximum(c * nsb - 1, 0), r * w + col0 + h))
    nh = n_heads
    o3, lse3 = pl.pallas_call(
        body, name=name, grid=(n_heads, dil, nchunk),
        in_specs=[blk(nh, 0), blk(2 * nh, 0), blk(2 * nh, nh), halo(2 * nh, 0), halo(2 * nh, nh),
                  pl.BlockSpec((None, BAND, 2 * BAND), lambda h, r, c: (h, 0, 0))],
        out_specs=[blk(nh, 0), blk(nh, 0)],
        out_shape=[jax.ShapeDtypeStruct((L, dil * D), F32)] * 2,
        scratch_shapes=[pltpu.VMEM((lc + BAND, HEAD_DIM), BF16)] * 2,
        compiler_params=_cparams(("arbitrary", "arbitrary", "arbitrary")),
    )(q3, kv3, kv3, kv3, kv3, bias)
    return o3.reshape(T, D), lse3.reshape(T, D)


def attn_merge(name, outs, lses):
    T, D = outs[0].shape
    tr = min(ROW_TILE, T)
    nb = len(outs)

    def body(*refs):
        o_refs, l_refs = refs[:nb], refs[nb:2 * nb]
        o_ref, obf_ref, lse_ref = refs[2 * nb:]
        ls = [r[...] for r in l_refs]
        m = functools.reduce(jnp.maximum, ls)
        es = [jnp.exp(l - m) for l in ls]
        tot = functools.reduce(lambda x, y: x + y, es)
        o = functools.reduce(lambda x, y: x + y, [(e / tot) * r[...] for e, r in zip(es, o_refs)])
        o_ref[...] = o
        obf_ref[...] = o.astype(BF16)
        lse_ref[...] = m + jnp.log(tot)

    return pl.pallas_call(
        body, name=name, grid=(T // tr,), in_specs=[_row_spec(tr, D)] * (2 * nb),
        out_specs=[_row_spec(tr, D)] * 3,
        out_shape=[jax.ShapeDtypeStruct((T, D), F32), jax.ShapeDtypeStruct((T, D), BF16), jax.ShapeDtypeStruct((T, D), F32)],
        compiler_params=_cparams(("parallel",)),
    )(*outs, *lses)


def attn_bwd_prep(name, do, o):
    T, D = o.shape
    n_heads = D // HEAD_DIM
    tr = min(ROW_TILE, T)

    def body(do_ref, o_ref, dobf_ref, dsum_ref):
        dobf_ref[...] = do_ref[...].astype(BF16)
        for h in range(n_heads):
            cols = pl.ds(h * HEAD_DIM, HEAD_DIM)
            d = jnp.sum(do_ref[:, cols] * o_ref[:, cols], axis=-1, keepdims=True)
            dsum_ref[:, cols] = jnp.broadcast_to(d, (tr, HEAD_DIM))

    return pl.pallas_call(
        body, name=name, grid=(T // tr,), in_specs=[_row_spec(tr, D)] * 2, out_specs=[_row_spec(tr, D)] * 2,
        out_shape=[jax.ShapeDtypeStruct((T, D), BF16), jax.ShapeDtypeStruct((T, D), F32)],
        compiler_params=_cparams(("parallel",)),
    )(do, o)


def attn_bwd(name, q, kv, do, lse, dsum, bias, dil):
    T, D = q.shape
    n_heads = D // HEAD_DIM
    L, lc, nchunk, nsub = _attn_views(T, dil)
    scale = HEAD_DIM ** -0.5
    q3, do3, lse3, dsum3 = (t.reshape(L, dil * D) for t in (q, do, lse, dsum))
    kv3 = kv.reshape(L, dil * 2 * D)
    nt_dims = (((1,), (1,)), ((), ()))
    tn_dims = (((0,), (0,)), ((), ()))
    nn_dims = (((1,), (0,)), ((), ()))

    def body(q_ref, k_ref, v_ref, do_ref, lse_ref, ds_ref, kp_ref, vp_ref, qn_ref, don_ref, lsen_ref, dsn_ref, b_ref,
             dq_ref, dk_ref, dv_ref, dsb_ref, kext_ref, vext_ref, dkext_ref, dvext_ref):
        r = pl.program_id(1)
        c = pl.program_id(2)
        kext_ref[pl.ds(0, BAND), :] = kp_ref[...]
        vext_ref[pl.ds(0, BAND), :] = vp_ref[...]
        kext_ref[pl.ds(BAND, lc), :] = k_ref[...]
        vext_ref[pl.ds(BAND, lc), :] = v_ref[...]
        dkext_ref[...] = jnp.zeros_like(dkext_ref)
        dvext_ref[...] = jnp.zeros_like(dvext_ref)
        band, jcol = _band_mask()
        bias_v = b_ref[...]

        @pl.when(jnp.logical_and(r == 0, c == 0))
        def _():
            dsb_ref[...] = jnp.zeros_like(dsb_ref)

        def sub(a, carry):
            off = pl.multiple_of(a * BAND, BAND)
            qa = q_ref[pl.ds(off, BAND), :]
            doa = do_ref[pl.ds(off, BAND), :]
            kw = kext_ref[pl.ds(off, 2 * BAND), :]
            vw = vext_ref[pl.ds(off, 2 * BAND), :]
            s = lax.dot_general(qa, kw, nt_dims, preferred_element_type=F32) * scale + bias_v
            first = jnp.logical_and(c == 0, a == 0)
            valid = band & jnp.logical_or(jcol >= BAND, jnp.logical_not(first))
            p = jnp.where(valid, jnp.exp(s - _rep2(lse_ref[pl.ds(off, BAND), :])), 0.0)
            dp = lax.dot_general(doa, vw, nt_dims, preferred_element_type=F32)
            ds = p * (dp - _rep2(ds_ref[pl.ds(off, BAND), :]))
            dsb_ref[...] += ds
            dsb16 = ds.astype(BF16)
            dq_ref[pl.ds(off, BAND), :] = lax.dot_general(dsb16, kw, nn_dims, preferred_element_type=F32) * scale
            dkext_ref[pl.ds(off, 2 * BAND), :] += lax.dot_general(dsb16, qa, tn_dims, preferred_element_type=F32) * scale
            dvext_ref[pl.ds(off, 2 * BAND), :] += lax.dot_general(p.astype(BF16), doa, tn_dims, preferred_element_type=F32)
            return carry

        lax.fori_loop(0, nsub, sub, 0)

        @pl.when(c < nchunk - 1)
        def _():
            qn = qn_ref[...]
            don = don_ref[...]
            kl = kext_ref[pl.ds(lc, BAND), :]
            vl = vext_ref[pl.ds(lc, BAND), :]
            s = lax.dot_general(qn, kl, nt_dims, preferred_element_type=F32) * scale + bias_v[:, :BAND]
            p = jnp.where(band[:, :BAND], jnp.exp(s - lsen_ref[...]), 0.0)
            dp = lax.dot_general(don, vl, nt_dims, preferred_element_type=F32)
            ds = p * (dp - dsn_ref[...])
            dkext_ref[pl.ds(lc, BAND), :] += lax.dot_general(ds.astype(BF16), qn, tn_dims, preferred_element_type=F32) * scale
            dvext_ref[pl.ds(lc, BAND), :] += lax.dot_general(p.astype(BF16), don, tn_dims, preferred_element_type=F32)

        dk_ref[...] = dkext_ref[pl.ds(BAND, lc), :]
        dv_ref[...] = dvext_ref[pl.ds(BAND, lc), :]

    nsb = lc // BAND
    nblk = L // BAND
    nh = n_heads
    blk = lambda w, col0: pl.BlockSpec((lc, HEAD_DIM), lambda h, r, c: (c, r * w + col0 + h))
    prev = lambda w, col0: pl.BlockSpec((BAND, HEAD_DIM), lambda h, r, c: (jnp.maximum(c * nsb - 1, 0), r * w + col0 + h))
    nxt = lambda: pl.BlockSpec((BAND, HEAD_DIM), lambda h, r, c: (jnp.minimum((c + 1) * nsb, nblk - 1), r * nh + h))
    one = blk(nh, 0)
    dq3, dk3, dv3, dsb = pl.pallas_call(
        body, name=name, grid=(n_heads, dil, nchunk),
        in_specs=[one, blk(2 * nh, 0), blk(2 * nh, nh), one, one, one, prev(2 * nh, 0), prev(2 * nh, nh),
                  nxt(), nxt(), nxt(), nxt(), pl.BlockSpec((None, BAND, 2 * BAND), lambda h, r, c: (h, 0, 0))],
        out_specs=[one, one, one, pl.BlockSpec((None, BAND, 2 * BAND), lambda h, r, c: (h, 0, 0))],
        out_shape=[jax.ShapeDtypeStruct((L, dil * D), F32)] * 3 + [jax.ShapeDtypeStruct((n_heads, BAND, 2 * BAND), F32)],
        scratch_shapes=[pltpu.VMEM((lc + BAND, HEAD_DIM), BF16)] * 2 + [pltpu.VMEM((lc + BAND, HEAD_DIM), F32)] * 2,
        compiler_params=_cparams(("arbitrary", "arbitrary", "arbitrary")),
    )(q3, kv3, kv3, do3, lse3, dsum3, kv3, kv3, q3, do3, lse3, dsum3, bias)
    return dq3.reshape(T, D), dk3.reshape(T, D), dv3.reshape(T, D), dsb


def sum_branches(name, dqs, dks, dvs):
    T, D = dqs[0].shape
    tr = min(ROW_TILE, T)
    nb = len(dqs)

    def body(*refs):
        dq_refs, dk_refs, dv_refs = refs[:nb], refs[nb:2 * nb], refs[2 * nb:3 * nb]
        dq_ref, dkv_ref = refs[3 * nb:]
        add = lambda rs: functools.reduce(lambda x, y: x + y, [r[...] for r in rs])
        dq_ref[...] = add(dq_refs).astype(BF16)
        dkv_ref[:, pl.ds(0, D)] = add(dk_refs).astype(BF16)
        dkv_ref[:, pl.ds(D, D)] = add(dv_refs).astype(BF16)

    return pl.pallas_call(
        body, name=name, grid=(T // tr,), in_specs=[_row_spec(tr, D)] * (3 * nb),
        out_specs=[_row_spec(tr, D), _row_spec(tr, 2 * D)],
        out_shape=[jax.ShapeDtypeStruct((T, D), BF16), jax.ShapeDtypeStruct((T, 2 * D), BF16)],
        compiler_params=_cparams(("parallel",)),
    )(*dqs, *dks, *dvs)


def _divisor_tile(n, cap, mult):
    if n <= cap:
        return n
    t = cap - cap % mult
    while n % t:
        t -= mult
    return t


def _tile2(R, C):
    return _divisor_tile(R, 512, 8), _divisor_tile(C, 1024, 128)


def half_cast(name, dw, core):
    S, R, C = dw.shape
    hr = R // 2
    tr, tc = _tile2(hr, C)
    nrb = hr // tr

    def body(c_ref, x_ref, o_ref):
        o_ref[...] = x_ref[...].astype(BF16)

    return pl.pallas_call(
        body, name=name,
        grid_spec=pltpu.PrefetchScalarGridSpec(
            num_scalar_prefetch=1, grid=(S, nrb, C // tc),
            in_specs=[pl.BlockSpec((None, tr, tc), lambda s, i, j, c: (s, (1 - c[0]) * nrb + i, j))],
            out_specs=pl.BlockSpec((None, tr, tc), lambda s, i, j, c: (s, i, j))),
        out_shape=jax.ShapeDtypeStruct((S, hr, C), BF16),
        compiler_params=_cparams(("parallel", "parallel", "parallel")),
    )(core, dw)


def pair_sum(name, dw, recv, core):
    S, R, C = dw.shape
    hr = R // 2
    tr, tc = _tile2(hr, C)
    nrb = hr // tr

    def body(c_ref, x_ref, r_ref, p_ref, pbf_ref):
        p = x_ref[...] + r_ref[...].astype(F32)
        p_ref[...] = p
        pbf_ref[...] = p.astype(BF16)

    out = pl.BlockSpec((None, tr, tc), lambda s, i, j, c: (s, i, j))
    return pl.pallas_call(
        body, name=name,
        grid_spec=pltpu.PrefetchScalarGridSpec(
            num_scalar_prefetch=1, grid=(S, nrb, C // tc),
            in_specs=[pl.BlockSpec((None, tr, tc), lambda s, i, j, c: (s, c[0] * nrb + i, j)), out],
            out_specs=[out, out]),
        out_shape=[jax.ShapeDtypeStruct((S, hr, C), F32), jax.ShapeDtypeStruct((S, hr, C), BF16)],
        compiler_params=_cparams(("parallel", "parallel", "parallel")),
    )(core, dw, recv)


def chip_sum(name, p, recv, chip_core):
    S, hr, C = p.shape
    tr, tc = _tile2(hr, C)
    nrb = hr // tr

    def body(s_ref, p_ref, r_ref, o_ref):
        acc = p_ref[...]
        for t in range(N_CHIPS - 1):
            acc = acc + r_ref[t].astype(F32)
        o_ref[...] = acc

    return pl.pallas_call(
        body, name=name,
        grid_spec=pltpu.PrefetchScalarGridSpec(
            num_scalar_prefetch=1, grid=(nrb, C // tc),
            in_specs=[pl.BlockSpec((None, tr, tc), lambda i, j, s: (s[0], i, j)),
                      pl.BlockSpec((N_CHIPS - 1, tr, tc), lambda i, j, s: (0, i, j))],
            out_specs=pl.BlockSpec((tr, tc), lambda i, j, s: (s[1] * nrb + i, j))),
        out_shape=jax.ShapeDtypeStruct((2 * hr, C), F32),
        compiler_params=_cparams(("parallel", "parallel")),
    )(chip_core, p, recv)


def adamw(name, w, g, m, v):
    R, C = w.shape
    tr, tc = _tile2(R, C)
    c1 = 1.0 - ADAM_B1 ** ADAM_STEP
    c2 = 1.0 - ADAM_B2 ** ADAM_STEP

    def body(w_ref, g_ref, m_ref, v_ref, d_ref, nm_ref, nv_ref):
        gv = g_ref[...]
        nm = ADAM_B1 * m_ref[...] + (1.0 - ADAM_B1) * gv
        nv = ADAM_B2 * v_ref[...] + (1.0 - ADAM_B2) * (gv * gv)
        nm_ref[...] = nm
        nv_ref[...] = nv
        d_ref[...] = -ADAM_LR * ((nm / c1) / (jnp.sqrt(nv / c2) + ADAM_EPS) + ADAM_WD * w_ref[...])

    spec = pl.BlockSpec((tr, tc), lambda i, j: (i, j))
    return pl.pallas_call(
        body, name=name, grid=(R // tr, C // tc), in_specs=[spec] * 4, out_specs=[spec] * 3,
        out_shape=[jax.ShapeDtypeStruct((R, C), F32)] * 3,
        compiler_params=_cparams(("parallel", "parallel")),
    )(w, g, m, v)


def sum_devices(name, gathered):
    n, R, C = gathered.shape

    def body(x_ref, o_ref):
        acc = x_ref[0]
        for d in range(1, n):
            acc = acc + x_ref[d]
        o_ref[...] = acc

    return pl.pallas_call(
        body, name=name, in_specs=[pl.BlockSpec(memory_space=pltpu.VMEM)],
        out_specs=pl.BlockSpec(memory_space=pltpu.VMEM),
        out_shape=jax.ShapeDtypeStruct((R, C), F32),
    )(gathered)


def _place():
    x, y, c = lax.axis_index("x"), lax.axis_index("y"), lax.axis_index("c")
    return x, y, c


def _other_chips(x, y):
    return [(1 - x, y), (x, 1 - y), (1 - x, 1 - y)]


def all_gather8(name, block):
    R, C = block.shape

    def body(x_ref, out_ref, send_sems, recv_sems, local_sem):
        x, y, c = _place()
        me, sibling = (x, y, c), (x, y, 1 - c)
        chips = _other_chips(x, y)

        def rows(px, py, pc):
            return out_ref.at[4 * px + 2 * py + pc]

        def copy(k, blk, to, src=None):
            return pltpu.make_async_remote_copy(
                src_ref=rows(*blk) if src is None else src, dst_ref=rows(*blk),
                send_sem=send_sems.at[k], recv_sem=recv_sems.at[k], device_id=to, device_id_type=MESH)

        mine = pltpu.make_async_copy(x_ref, rows(*me), local_sem)
        mine.start()
        first = [copy(0, me, sibling, src=x_ref)]
        first += [copy(1 + j, me, (*chip, c), src=x_ref) for j, chip in enumerate(chips)]
        for cp in first:
            cp.start()
        passed = [copy(4 + j, (*chip, c), sibling) for j, chip in enumerate(chips)]
        for j, chip in enumerate(chips):
            copy(1 + j, (*chip, c), me).wait_recv()
            passed[j].start()
        copy(0, sibling, me).wait_recv()
        for j, chip in enumerate(chips):
            copy(4 + j, (*chip, 1 - c), me).wait_recv()
        for cp in first + passed:
            cp.wait_send()
        mine.wait()

    return pl.pallas_call(
        body, name=name, out_shape=jax.ShapeDtypeStruct((N_DEV, R, C), block.dtype),
        in_specs=[pl.BlockSpec(memory_space=pltpu.VMEM)], out_specs=pl.BlockSpec(memory_space=pltpu.VMEM),
        scratch_shapes=[pltpu.SemaphoreType.DMA((7,)), pltpu.SemaphoreType.DMA((7,)), pltpu.SemaphoreType.DMA],
    )(block)


def gather_weights(name, shards):
    n = len(shards)

    def body(*refs):
        ins, outs = refs[:n], refs[n:2 * n]
        ici_send, ici_recv, d2d_send, d2d_recv, local_sems = refs[2 * n:]
        x, y, c = _place()
        me_chip = 2 * x + y
        sibling = (x, y, 1 - c)
        chips = _other_chips(x, y)

        def half(ref, h):
            hr = ref.shape[0] // 2
            return ref.at[pl.ds(h * hr, hr)]

        local = [pltpu.make_async_copy(ins[w], outs[w].at[me_chip], local_sems.at[w]) for w in range(n)]
        for cp in local:
            cp.start()

        def ici(w, t, chip, src_chip_slot):
            return pltpu.make_async_remote_copy(
                src_ref=half(ins[w], c), dst_ref=half(outs[w].at[src_chip_slot], c),
                send_sem=ici_send.at[w * 3 + t], recv_sem=ici_recv.at[w * 3 + t],
                device_id=(*chip, c), device_id_type=MESH)

        def d2d(w, t, slot, h):
            return pltpu.make_async_remote_copy(
                src_ref=half(outs[w].at[slot], h), dst_ref=half(outs[w].at[slot], h),
                send_sem=d2d_send.at[w * 3 + t], recv_sem=d2d_recv.at[w * 3 + t],
                device_id=sibling, device_id_type=MESH)

        sends = [ici(w, t, chip, me_chip) for w in range(n) for t, chip in enumerate(chips)]
        for cp in sends:
            cp.start()
        fwd = []
        for w in range(n):
            for t, chip in enumerate(chips):
                slot = 2 * chip[0] + chip[1]
                ici(w, t, chip, slot).wait_recv()
                cp = d2d(w, t, slot, c)
                cp.start()
                fwd.append(cp)
        for w in range(n):
            for t, chip in enumerate(chips):
                d2d(w, t, 2 * chip[0] + chip[1], 1 - c).wait_recv()
        for cp in sends + fwd:
            cp.wait_send()
        for cp in local:
            cp.wait()

    any_spec = pl.BlockSpec(memory_space=pl.ANY)
    return pl.pallas_call(
        body, name=name, in_specs=[any_spec] * n, out_specs=[any_spec] * n,
        out_shape=[jax.ShapeDtypeStruct((N_CHIPS,) + s.shape, s.dtype) for s in shards],
        scratch_shapes=[pltpu.SemaphoreType.DMA((3 * n,))] * 4 + [pltpu.SemaphoreType.DMA((n,))],
    )(*shards)


def swap_with_sibling(name, bufs):
    n = len(bufs)

    def body(*refs):
        ins, outs, send_sems, recv_sems = refs[:n], refs[n:2 * n], refs[2 * n], refs[2 * n + 1]
        x, y, c = _place()
        cps = [pltpu.make_async_remote_copy(src_ref=ins[w], dst_ref=outs[w], send_sem=send_sems.at[w],
                                            recv_sem=recv_sems.at[w], device_id=(x, y, 1 - c), device_id_type=MESH)
               for w in range(n)]
        for cp in cps:
            cp.start()
        for cp in cps:
            cp.wait()

    any_spec = pl.BlockSpec(memory_space=pl.ANY)
    return pl.pallas_call(
        body, name=name, in_specs=[any_spec] * n, out_specs=[any_spec] * n,
        out_shape=[jax.ShapeDtypeStruct(b.shape, b.dtype) for b in bufs],
        scratch_shapes=[pltpu.SemaphoreType.DMA((n,))] * 2,
    )(*bufs)


def scatter_to_chips(name, bufs):
    n = len(bufs)

    def body(*refs):
        ins, outs, send_sems, recv_sems = refs[:n], refs[n:2 * n], refs[2 * n], refs[2 * n + 1]
        x, y, c = _place()
        chips = _other_chips(x, y)
        cps = [pltpu.make_async_remote_copy(src_ref=ins[w].at[2 * chip[0] + chip[1]], dst_ref=outs[w].at[t],
                                            send_sem=send_sems.at[w * 3 + t], recv_sem=recv_sems.at[w * 3 + t],
                                            device_id=(*chip, c), device_id_type=MESH)
               for w in range(n) for t, chip in enumerate(chips)]
        for cp in cps:
            cp.start()
        for cp in cps:
            cp.wait()

    any_spec = pl.BlockSpec(memory_space=pl.ANY)
    return pl.pallas_call(
        body, name=name, in_specs=[any_spec] * n, out_specs=[any_spec] * n,
        out_shape=[jax.ShapeDtypeStruct((N_CHIPS - 1,) + b.shape[1:], b.dtype) for b in bufs],
        scratch_shapes=[pltpu.SemaphoreType.DMA((3 * n,))] * 2,
    )(*bufs)


def share_half_with_sibling(name, bufs):
    n = len(bufs)

    def body(*refs):
        ins, outs, send_sems, recv_sems = refs[:n], refs[n:2 * n], refs[2 * n], refs[2 * n + 1]
        x, y, c = _place()
        cps = []
        for w in range(n):
            hr = outs[w].shape[0] // 2
            mine = outs[w].at[pl.ds(c * hr, hr)]
            cps.append(pltpu.make_async_remote_copy(src_ref=mine, dst_ref=mine, send_sem=send_sems.at[w],
                                                    recv_sem=recv_sems.at[w], device_id=(x, y, 1 - c), device_id_type=MESH))
        for cp in cps:
            cp.start()
        for w, cp in enumerate(cps):
            cp.wait_send()
            hr = outs[w].shape[0] // 2
            theirs = outs[w].at[pl.ds((1 - c) * hr, hr)]
            pltpu.make_async_remote_copy(src_ref=theirs, dst_ref=theirs, send_sem=send_sems.at[w], recv_sem=recv_sems.at[w],
                                         device_id=(x, y, 1 - c), device_id_type=MESH).wait_recv()

    any_spec = pl.BlockSpec(memory_space=pl.ANY)
    return pl.pallas_call(
        body, name=name, in_specs=[any_spec] * n, out_specs=[any_spec] * n,
        out_shape=[jax.ShapeDtypeStruct(b.shape, b.dtype) for b in bufs],
        input_output_aliases={w: w for w in range(n)},
        scratch_shapes=[pltpu.SemaphoreType.DMA((n,))] * 2,
    )(*bufs)


def _pack(arrs):
    parts = []
    for a in arrs:
        flat = a.reshape(-1).astype(F32)
        n = flat.shape[0]
        padded = -(-n // 1024) * 1024
        parts.append(jnp.pad(flat, (0, padded - n)).reshape(padded // 128, 128))
    return jnp.concatenate(parts, axis=0)


def _unpack(buf, shapes):
    out, row = [], 0
    for shp in shapes:
        n = int(np.prod(shp))
        rows = -(-n // 1024) * 8
        out.append(buf[row:row + rows].reshape(-1)[:n].reshape(shp))
        row += rows
    return out


def _bias_epi(acc, b):
    return (acc + b,)


def local_step(x, target, W, P):
    T, D = x.shape
    g = {}
    plain = lambda acc: (acc,)

    (h1,) = mm_nn("pw1_fwd", x, W["pw1"], "col", _bias_epi, [F32], extras=[(P["pw1_b"], "row")])
    u, cpre, s = conv_fwd("conv_fwd", h1, P["dw_w"], P["dw_b"], P["cln_g"], P["cln_b"])
    (mix0,) = mm_nn("pw2_fwd", s, W["pw2"], "row", _bias_epi, [F32], extras=[(P["pw2_b"], "row")])
    ln = [None] * 4
    gam = [P["ln_mix_g"][0:1], P["ln_mlp_g"][0:1], P["ln_mix_g"][1:2], P["ln_mlp_g"][1:2]]
    bet = [P["ln_mix_b"][0:1], P["ln_mlp_b"][0:1], P["ln_mix_b"][1:2], P["ln_mlp_b"][1:2]]
    ln[0] = ln_fwd("ln0_fwd", mix0, x)(gam[0], bet[0])

    def mlp_fwd(tag, i_ln, w1, w2):
        xhat, rstd, xbf = ln[i_ln]
        (hid,) = mm_nn(tag + "_up", xbf, w1, "col", lambda acc: (jnp.square(jnp.maximum(acc, 0.0)),), [BF16])
        (mlp,) = mm_nn(tag + "_down", hid, w2, "row", plain, [F32])
        ln[i_ln + 1] = ln_fwd(tag + "_ln", mlp, xhat, gam[i_ln], bet[i_ln])(gam[i_ln + 1], bet[i_ln + 1])
        return hid

    hid0 = mlp_fwd("mlp0", 0, W["w1_0"], W["w2_0"])

    x2bf = ln[1][2]
    (kv,) = mm_nn("kv_fwd", x2bf, W["kv"], "col", plain, [BF16])
    (q,) = mm_nn("q_fwd", x2bf, W["wq"], "row", plain, [BF16])
    biases = [bias_expand("bias_d%d" % d, P["rel_bias"], d) for _, d in BRANCHES]
    outs, lses = [], []
    for (win, d), b in zip(BRANCHES, biases):
        assert win // d == BAND and (T // d) % BAND == 0
        o_b, l_b = attn_fwd("attn_fwd_d%d" % d, q, kv, b, d)
        outs.append(o_b)
        lses.append(l_b)
    o, obf, lse = attn_merge("attn_merge", outs, lses)
    (attn,) = mm_nn("wo_fwd", obf, W["wo"], "row", plain, [F32])
    ln[2] = ln_fwd("ln2_fwd", attn, ln[1][0], gam[1], bet[1])(gam[2], bet[2])
    hid1 = mlp_fwd("mlp1", 2, W["w1_1"], W["w2_1"])

    dr3, dr3bf, g["ln_mlp_g1"], g["ln_mlp_b1"], _, loss_sum = ln_bwd(
        "ln3_bwd", ln[3][0], ln[3][1], gam[3], target=target, beta=bet[3])

    def mlp_bwd(tag, i_ln, w1, w2, hid, dr, drbf):
        xbf = ln[i_ln][2]
        g[tag + "_w2"] = mm_tn(tag + "_dw2", hid, drbf, w2.shape, "row")
        (dp,) = mm_nt(tag + "_dhid", drbf, w2, "row",
                      lambda acc, h: (acc * (2.0 * jnp.sqrt(h.astype(F32))),), [BF16], extras=[(hid, "tile")])
        g[tag + "_w1"] = mm_tn(tag + "_dw1", xbf, dp, w1.shape, "col")
        (dx,) = mm_nt(tag + "_dx", dp, w1, "col", lambda acc, e: (acc + ALPHA * e,), [F32], extras=[(dr, "tile")])
        return dx

    dx3 = mlp_bwd("mlp1", 2, W["w1_1"], W["w2_1"], hid1, dr3, dr3bf)
    dr2, dr2bf, g["ln_mix_g1"], g["ln_mix_b1"], _ = ln_bwd("ln2_bwd", ln[2][0], ln[2][1], gam[2], dy=dx3)
    g["wo"] = mm_tn("wo_dw", obf, dr2bf, W["wo"].shape, "row")
    (do,) = mm_nt("wo_dx", dr2bf, W["wo"], "row", plain, [F32])
    dobf, dsum = attn_bwd_prep("attn_bwd_prep", do, o)
    dqs, dks, dvs, dsbs = [], [], [], []
    for (win, d), b in zip(BRANCHES, biases):
        dq_b, dk_b, dv_b, dsb = attn_bwd("attn_bwd_d%d" % d, q, kv, dobf, lse, dsum, b, d)
        dqs.append(dq_b)
        dks.append(dk_b)
        dvs.append(dv_b)
        dsbs.append(dsb)
    g["rel_bias"] = relbias_grad("relbias_grad", dsbs)[:, 0, :REL_BUCKETS].T
    dq, dkv = sum_branches("attn_bwd_sum", dqs, dks, dvs)
    g["wq"] = mm_tn("wq_dw", x2bf, dq, W["wq"].shape, "row")
    g["kv"] = mm_tn("kv_dw", x2bf, dkv, W["kv"].shape, "col")
    (dx2a,) = mm_nt("wq_dx", dq, W["wq"], "row", lambda acc, e: (acc + ALPHA * e,), [F32], extras=[(dr2, "tile")])
    (dx2,) = mm_nt("kv_dx", dkv, W["kv"], "col", lambda acc, e: (acc + e,), [F32], extras=[(dx2a, "tile")])

    dr1, dr1bf, g["ln_mlp_g0"], g["ln_mlp_b0"], _ = ln_bwd("ln1_bwd", ln[1][0], ln[1][1], gam[1], dy=dx2)
    dx1 = mlp_bwd("mlp0", 0, W["w1_0"], W["w2_0"], hid0, dr1, dr1bf)
    dr0, dr0bf, g["ln_mix_g0"], g["ln_mix_b0"], g["pw2_b"] = ln_bwd("ln0_bwd", ln[0][0], ln[0][1], gam[0], dy=dx1)

    g["pw2"] = mm_tn("pw2_dw", s, dr0bf, W["pw2"].shape, "row")
    (ds,) = mm_nt("pw2_dx", dr0bf, W["pw2"], "row", plain, [F32])
    dc, g["cln_g"], g["cln_b"], g["dw_b"] = conv_bwd_ln("conv_bwd_ln", ds, cpre, P["cln_g"], P["cln_b"])
    dh1, g["pw1_b"], g["dw_w"] = conv_bwd_taps("conv_bwd_taps", dc, u, h1, P["dw_w"])
    g["pw1"] = mm_tn("pw1_dw", x, dh1, W["pw1"].shape, "col")
    (dx,) = mm_nt("pw1_dx", dh1, W["pw1"], "col", lambda acc, e: (acc + ALPHA * e,), [F32], extras=[(dr0, "tile")])
    return loss_sum, dx, g


BIG = ("pw1", "pw2", "kv", "wq", "wo", "w1_0", "w1_1", "w2_0", "w2_1")


def kernel(x, conv_pw1_w, conv_pw1_b, conv_dw_w, conv_dw_b, conv_ln_g, conv_ln_b, conv_pw2_w, conv_pw2_b, w_kv, attn_wq, attn_wo, rel_bias, mlp_w1, mlp_w2, ln_mix_g, ln_mix_b, ln_mlp_g, ln_mlp_b, loss_target, m_conv_pw1_w, m_conv_pw1_b, m_conv_dw_w, m_conv_dw_b, m_conv_ln_g, m_conv_ln_b, m_conv_pw2_w, m_conv_pw2_b, m_w_kv, m_attn_wq, m_attn_wo, m_rel_bias, m_mlp_w1, m_mlp_w2, m_ln_mix_g, m_ln_mix_b, m_ln_mlp_g, m_ln_mlp_b, v_conv_pw1_w, v_conv_pw1_b, v_conv_dw_w, v_conv_dw_b, v_conv_ln_g, v_conv_ln_b, v_conv_pw2_w, v_conv_pw2_b, v_w_kv, v_attn_wq, v_attn_wo, v_rel_bias, v_mlp_w1, v_mlp_w2, v_ln_mix_g, v_ln_mix_b, v_ln_mlp_g, v_ln_mlp_b):
    _, T, D = x.shape
    xi, yi, ci = _place()
    chip = 2 * xi + yi
    core = jnp.reshape(ci, (1,)).astype(jnp.int32)
    chip_core = jnp.stack([chip, ci]).astype(jnp.int32)

    shard = {"pw1": conv_pw1_w[0], "pw2": conv_pw2_w[0], "kv": w_kv, "wq": attn_wq[0], "wo": attn_wo[0],
             "w1_0": mlp_w1[0], "w1_1": mlp_w1[1], "w2_0": mlp_w2[0], "w2_1": mlp_w2[1]}
    mom = {"pw1": m_conv_pw1_w[0], "pw2": m_conv_pw2_w[0], "kv": m_w_kv, "wq": m_attn_wq[0], "wo": m_attn_wo[0],
           "w1_0": m_mlp_w1[0], "w1_1": m_mlp_w1[1], "w2_0": m_mlp_w2[0], "w2_1": m_mlp_w2[1]}
    vel = {"pw1": v_conv_pw1_w[0], "pw2": v_conv_pw2_w[0], "kv": v_w_kv, "wq": v_attn_wq[0], "wo": v_attn_wo[0],
           "w1_0": v_mlp_w1[0], "w1_1": v_mlp_w1[1], "w2_0": v_mlp_w2[0], "w2_1": v_mlp_w2[1]}
    gathered = gather_weights("gather_weights", [shard[n].astype(BF16) for n in BIG])
    W = dict(zip(BIG, gathered))

    sharded_small = [conv_pw1_b, conv_dw_w[0], conv_dw_b, conv_ln_g, conv_ln_b, conv_pw2_b]
    sh_shapes = [a.shape for a in sharded_small]
    small_all = all_gather8("gather_small", _pack(sharded_small))
    per_chip = [_unpack(small_all[2 * j], sh_shapes) for j in range(N_CHIPS)]
    full = [jnp.concatenate([per_chip[j][i] for j in range(N_CHIPS)], axis=-1) for i in range(len(sharded_small))]
    P = dict(pw1_b=full[0], dw_w=full[1], dw_b=full[2], cln_g=full[3], cln_b=full[4], pw2_b=full[5],
             rel_bias=rel_bias, ln_mix_g=ln_mix_g, ln_mix_b=ln_mix_b, ln_mlp_g=ln_mlp_g, ln_mlp_b=ln_mlp_b)

    loss_sum, dx, g = local_step(x[0], loss_target[0], W, P)
    loss = (0.5 / D) * lax.psum(loss_sum[0, 0], ("x", "y", "c"))

    dws = [g[{"w1_0": "mlp0_w1", "w1_1": "mlp1_w1", "w2_0": "mlp0_w2", "w2_1": "mlp1_w2"}.get(n, n)] for n in BIG]
    to_sibling = [half_cast("rs_cast_" + n, dw, core) for n, dw in zip(BIG, dws)]
    from_sibling = swap_with_sibling("rs_pair_swap", to_sibling)
    pairs = [pair_sum("rs_pair_sum_" + n, dw, r, core) for n, dw, r in zip(BIG, dws, from_sibling)]
    from_chips = scatter_to_chips("rs_chip_scatter", [p[1] for p in pairs])
    halves = [chip_sum("rs_chip_sum_" + n, p[0], r, chip_core) for n, p, r in zip(BIG, pairs, from_chips)]
    grads_big = dict(zip(BIG, share_half_with_sibling("rs_half_share", halves)))
    upd_big = {n: adamw("adamw_" + n, shard[n], grads_big[n], mom[n], vel[n]) for n in BIG}

    small_names = ["pw1_b", "dw_w", "dw_b", "cln_g", "cln_b", "pw2_b", "rel_bias",
                   "ln_mix_g0", "ln_mix_g1", "ln_mix_b0", "ln_mix_b1", "ln_mlp_g0", "ln_mlp_g1", "ln_mlp_b0", "ln_mlp_b1"]
    small_grads = [g[n] for n in small_names]
    sg_shapes = [a.shape for a in small_grads]
    summed = sum_devices("small_grad_sum", all_gather8("gather_small_grads", _pack(small_grads)))
    sg = dict(zip(small_names, _unpack(summed, sg_shapes)))

    def my_cols(a, width):
        return lax.dynamic_slice_in_dim(a, chip * width, width, axis=a.ndim - 1)

    small_g = [my_cols(sg["pw1_b"], conv_pw1_b.shape[-1]),
               my_cols(sg["dw_w"], conv_dw_w.shape[-1])[None],
               my_cols(sg["dw_b"], conv_dw_b.shape[-1]), my_cols(sg["cln_g"], conv_ln_g.shape[-1]),
               my_cols(sg["cln_b"], conv_ln_b.shape[-1]), my_cols(sg["pw2_b"], conv_pw2_b.shape[-1]),
               sg["rel_bias"],
               jnp.concatenate([sg["ln_mix_g0"], sg["ln_mix_g1"]], axis=0),
               jnp.concatenate([sg["ln_mix_b0"], sg["ln_mix_b1"]], axis=0),
               jnp.concatenate([sg["ln_mlp_g0"], sg["ln_mlp_g1"]], axis=0),
               jnp.concatenate([sg["ln_mlp_b0"], sg["ln_mlp_b1"]], axis=0)]
    small_w = [conv_pw1_b, conv_dw_w, conv_dw_b, conv_ln_g, conv_ln_b, conv_pw2_b, rel_bias, ln_mix_g, ln_mix_b, ln_mlp_g, ln_mlp_b]
    small_m = [m_conv_pw1_b, m_conv_dw_w, m_conv_dw_b, m_conv_ln_g, m_conv_ln_b, m_conv_pw2_b, m_rel_bias, m_ln_mix_g, m_ln_mix_b, m_ln_mlp_g, m_ln_mlp_b]
    small_v = [v_conv_pw1_b, v_conv_dw_w, v_conv_dw_b, v_conv_ln_g, v_conv_ln_b, v_conv_pw2_b, v_rel_bias, v_ln_mix_g, v_ln_mix_b, v_ln_mlp_g, v_ln_mlp_b]
    sw_shapes = [a.shape for a in small_w]
    small_g = [a.reshape(s) for a, s in zip(small_g, sw_shapes)]
    upd_small = adamw("adamw_small", _pack(small_w), _pack(small_g), _pack(small_m), _pack(small_v))
    sd, snm, snv = (_unpack(b, sw_shapes) for b in upd_small)

    def big_out(tree):
        return dict(pw1=tree["pw1"][None], pw2=tree["pw2"][None], kv=tree["kv"], wq=tree["wq"][None], wo=tree["wo"][None],
                    w1=jnp.stack([tree["w1_0"], tree["w1_1"]]), w2=jnp.stack([tree["w2_0"], tree["w2_1"]]))

    def ordered(big, small):
        return [big["pw1"], small[0], small[1], small[2], small[3], small[4], big["pw2"], small[5], big["kv"], big["wq"],
                big["wo"], small[6], big["w1"], big["w2"], small[7], small[8], small[9], small[10]]

    grads = ordered(big_out(grads_big), small_g)
    deltas = ordered(big_out({n: upd_big[n][0] for n in BIG}), sd)
    new_m = ordered(big_out({n: upd_big[n][1] for n in BIG}), snm)
    new_v = ordered(big_out({n: upd_big[n][2] for n in BIG}), snv)
    return (loss, dx[None], *grads, *deltas, *new_m, *new_v)
```

```python
import functools
import math

import numpy as np
import jax
import jax.numpy as jnp
from jax import lax
from jax.experimental import pallas as pl
from jax.experimental.pallas import tpu as pltpu

F32 = jnp.float32
BF16 = jnp.bfloat16

HEAD_DIM = 128
BAND = 128
BRANCHES = ((128, 1), (512, 4), (2048, 16))
CONV_WIDTH = 31
CONV_HALO = 32
REL_BUCKETS = 32
REL_MAX_DIST = 2048
DEPTH = 2
ALPHA = (2 * DEPTH) ** 0.25
LN_EPS = 1e-5
ADAM_LR, ADAM_B1, ADAM_B2, ADAM_EPS, ADAM_WD, ADAM_STEP = 0.001, 0.9, 0.999, 1e-08, 0.01, 10

N_CHIPS = 4
N_DEV = 8
MESH = pl.DeviceIdType.MESH
VMEM_LIMIT_BYTES = 56 * 1024 * 1024
MM_TM, MM_TN, MM_TK = 1024, 1024, 2048
ROW_TILE = 256
CONV_TILE = 128
ATTN_CHUNK = 1024
NEG_BIG = -1e30


def _cparams(sem):
    return pltpu.CompilerParams(dimension_semantics=sem, vmem_limit_bytes=VMEM_LIMIT_BYTES)


def _sigmoid(x):
    return 1.0 / (1.0 + jnp.exp(-x))


def _wspec(wshape, axis, br, bc, rsel, csel):
    _, R, C = wshape
    if axis == "col":
        nb = C // bc
        assert nb * bc == C, (wshape, bc)
        return pl.BlockSpec((None, br, bc), lambda *g: (csel(*g) // nb, rsel(*g), csel(*g) % nb))
    nb = R // br
    assert nb * br == R, (wshape, br)
    return pl.BlockSpec((None, br, bc), lambda *g: (rsel(*g) // nb, rsel(*g) % nb, csel(*g)))


def _full_dims(wshape, axis):
    _, R, C = wshape
    return (R, N_CHIPS * C) if axis == "col" else (N_CHIPS * R, C)


def _mm_body(nk, n_extra, n_out, dims, epilogue):
    def body(*refs):
        a_ref, b_ref = refs[0], refs[1]
        extra = refs[2:2 + n_extra]
        outs = refs[2 + n_extra:2 + n_extra + n_out]
        part = lax.dot_general(a_ref[...].astype(BF16), b_ref[...].astype(BF16), (dims, ((), ())),
                               preferred_element_type=F32)
        if nk == 1:
            res = epilogue(part, *[e[...] for e in extra])
            for r, o in zip(res, outs):
                o[...] = r.astype(o.dtype)
            return
        acc_ref = refs[2 + n_extra + n_out]
        k = pl.program_id(2)

        @pl.when(k == 0)
        def _():
            acc_ref[...] = part

        @pl.when(k > 0)
        def _():
            acc_ref[...] += part

        @pl.when(k == nk - 1)
        def _():
            res = epilogue(acc_ref[...], *[e[...] for e in extra])
            for r, o in zip(res, outs):
                o[...] = r.astype(o.dtype)
    return body


def _extra_specs(extras, tm, tn):
    specs = []
    for arr, kind in extras:
        if kind == "tile":
            specs.append(pl.BlockSpec((tm, tn), lambda i, j, k: (i, j)))
        else:
            specs.append(pl.BlockSpec((1, tn), lambda i, j, k: (0, j)))
    return specs


def mm_nn(name, a, w, axis, epilogue, out_dtypes, extras=()):
    M, K = a.shape
    Kw, N = _full_dims(w.shape, axis)
    assert K == Kw
    tm, tn, tk = min(MM_TM, M), min(MM_TN, N), min(MM_TK, K)
    if axis == "col":
        tn = min(tn, w.shape[2])
    else:
        tk = min(tk, w.shape[1])
    nk = K // tk
    in_specs = [pl.BlockSpec((tm, tk), lambda i, j, k: (i, k)),
                _wspec(w.shape, axis, tk, tn, lambda i, j, k: k, lambda i, j, k: j)]
    in_specs += _extra_specs(extras, tm, tn)
    body = _mm_body(nk, len(extras), len(out_dtypes), ((1,), (0,)), epilogue)
    return pl.pallas_call(
        body, name=name, grid=(M // tm, N // tn, nk), in_specs=in_specs,
        out_specs=[pl.BlockSpec((tm, tn), lambda i, j, k: (i, j)) for _ in out_dtypes],
        out_shape=[jax.ShapeDtypeStruct((M, N), d) for d in out_dtypes],
        scratch_shapes=[pltpu.VMEM((tm, tn), F32)] if nk > 1 else [],
        compiler_params=_cparams(("parallel", "parallel", "arbitrary")),
    )(a, w, *[e for e, _ in extras])


def mm_nt(name, g, w, axis, epilogue, out_dtypes, extras=()):
    M, N = g.shape
    K, Nw = _full_dims(w.shape, axis)
    assert N == Nw
    tm, tn, tk = min(MM_TM, M), min(MM_TN, K), min(MM_TK, N)
    if axis == "col":
        tk = min(tk, w.shape[2])
    else:
        tn = min(tn, w.shape[1])
    nk = N // tk
    in_specs = [pl.BlockSpec((tm, tk), lambda i, j, k: (i, k)),
                _wspec(w.shape, axis, tn, tk, lambda i, j, k: j, lambda i, j, k: k)]
    in_specs += _extra_specs(extras, tm, tn)
    body = _mm_body(nk, len(extras), len(out_dtypes), ((1,), (1,)), epilogue)
    return pl.pallas_call(
        body, name=name, grid=(M // tm, K // tn, nk), in_specs=in_specs,
        out_specs=[pl.BlockSpec((tm, tn), lambda i, j, k: (i, j)) for _ in out_dtypes],
        out_shape=[jax.ShapeDtypeStruct((M, K), d) for d in out_dtypes],
        scratch_shapes=[pltpu.VMEM((tm, tn), F32)] if nk > 1 else [],
        compiler_params=_cparams(("parallel", "parallel", "arbitrary")),
    )(g, w, *[e for e, _ in extras])


def mm_tn(name, a, g, wshape, axis):
    M, K = a.shape
    Mg, N = g.shape
    assert M == Mg and (K, N) == _full_dims(wshape, axis)
    tm, tn, tk = min(MM_TM, K), min(MM_TN, N), min(MM_TK, M)
    if axis == "col":
        tn = min(tn, wshape[2])
    else:
        tm = min(tm, wshape[1])
    nk = M // tk
    body = _mm_body(nk, 0, 1, ((0,), (0,)), lambda acc: (acc,))
    return pl.pallas_call(
        body, name=name, grid=(K // tm, N // tn, nk),
        in_specs=[pl.BlockSpec((tk, tm), lambda i, j, k: (k, i)),
                  pl.BlockSpec((tk, tn), lambda i, j, k: (k, j))],
        out_specs=[_wspec(wshape, axis, tm, tn, lambda i, j, k: i, lambda i, j, k: j)],
        out_shape=[jax.ShapeDtypeStruct(wshape, F32)],
        scratch_shapes=[pltpu.VMEM((tm, tn), F32)] if nk > 1 else [],
        compiler_params=_cparams(("parallel", "parallel", "arbitrary")),
    )(a, g)[0]


def _row_spec(tr, width):
    return pl.BlockSpec((tr, width), lambda i: (i, 0))


def _vec_spec(width):
    return pl.BlockSpec((1, width), lambda i: (0, 0))


def _fold8(x):
    r, d = x.shape
    return jnp.sum(x.reshape(r // 8, 8, d), axis=0)


def ln_fwd(name, f, prev, prev_g=None, prev_b=None):
    T, D = f.shape
    tr = min(ROW_TILE, T)
    affine = prev_g is not None

    def body(*refs):
        if affine:
            f_ref, p_ref, pg_ref, pb_ref, g_ref, b_ref, xhat_ref, rstd_ref, xbf_ref = refs
            xprev = p_ref[...] * pg_ref[...] + pb_ref[...]
        else:
            f_ref, p_ref, g_ref, b_ref, xhat_ref, rstd_ref, xbf_ref = refs
            xprev = p_ref[...]
        r = ALPHA * xprev + f_ref[...]
        mu = jnp.mean(r, axis=-1, keepdims=True)
        cen = r - mu
        var = jnp.mean(cen * cen, axis=-1, keepdims=True)
        rstd = lax.rsqrt(var + LN_EPS)
        xhat = cen * rstd
        xhat_ref[...] = xhat
        rstd_ref[...] = rstd
        xbf_ref[...] = (xhat * g_ref[...] + b_ref[...]).astype(BF16)

    def call(g, b):
        ins = [f, prev] + ([prev_g, prev_b] if affine else []) + [g, b]
        specs = [_row_spec(tr, D), _row_spec(tr, D)] + ([_vec_spec(D)] * 2 if affine else []) + [_vec_spec(D)] * 2
        return pl.pallas_call(
            body, name=name, grid=(T // tr,), in_specs=specs,
            out_specs=[_row_spec(tr, D), _row_spec(tr, 1), _row_spec(tr, D)],
            out_shape=[jax.ShapeDtypeStruct((T, D), F32), jax.ShapeDtypeStruct((T, 1), F32),
                       jax.ShapeDtypeStruct((T, D), BF16)],
            compiler_params=_cparams(("parallel",)),
        )(*ins)
    return call


def ln_bwd(name, xhat, rstd, gamma, dy=None, target=None, beta=None):
    T, D = xhat.shape
    tr = min(ROW_TILE, T)
    nt = T // tr
    head = target is not None

    def body(*refs):
        if head:
            xhat_ref, rstd_ref, g_ref, tgt_ref, b_ref = refs[:5]
            outs = refs[5:]
        else:
            xhat_ref, rstd_ref, g_ref, dy_ref = refs[:4]
            outs = refs[4:]
        dr_ref, drbf_ref, dg_ref, db_ref, cs_ref = outs[:5]
        rest = outs[5:]
        if head:
            loss_ref, acc_ref = rest
        else:
            (acc_ref,) = rest
        i = pl.program_id(0)
        xhat_v = xhat_ref[...]
        gam = g_ref[...]
        if head:
            diff = xhat_v * gam + b_ref[...] - tgt_ref[...]
            dyv = diff * (1.0 / D)
        else:
            dyv = dy_ref[...]
        dxh = dyv * gam
        m1 = jnp.mean(dxh, axis=-1, keepdims=True)
        m2 = jnp.mean(dxh * xhat_v, axis=-1, keepdims=True)
        dr = rstd_ref[...] * (dxh - m1 - xhat_v * m2)
        dr_ref[...] = dr
        drbf_ref[...] = dr.astype(BF16)

        @pl.when(i == 0)
        def _():
            acc_ref[...] = jnp.zeros_like(acc_ref)

        acc_ref[0] += _fold8(dyv * xhat_v)
        acc_ref[1] += _fold8(dyv)
        acc_ref[2] += _fold8(dr)
        if head:
            acc_ref[3] += _fold8(diff * diff)

        @pl.when(i == nt - 1)
        def _():
            dg_ref[...] = jnp.sum(acc_ref[0], axis=0, keepdims=True)
            db_ref[...] = jnp.sum(acc_ref[1], axis=0, keepdims=True)
            cs_ref[...] = jnp.sum(acc_ref[2], axis=0, keepdims=True)
            if head:
                loss_ref[...] = jnp.sum(jnp.sum(acc_ref[3], axis=0, keepdims=True), axis=1, keepdims=True)

    ins = [xhat, rstd, gamma] + ([target, beta] if head else [dy])
    specs = [_row_spec(tr, D), _row_spec(tr, 1), _vec_spec(D)] + ([_row_spec(tr, D), _vec_spec(D)] if head else [_row_spec(tr, D)])
    out_specs = [_row_spec(tr, D), _row_spec(tr, D), _vec_spec(D), _vec_spec(D), _vec_spec(D)]
    out_shape = [jax.ShapeDtypeStruct((T, D), F32), jax.ShapeDtypeStruct((T, D), BF16)] + [jax.ShapeDtypeStruct((1, D), F32)] * 3
    if head:
        out_specs.append(pl.BlockSpec((1, 1), lambda i: (0, 0)))
        out_shape.append(jax.ShapeDtypeStruct((1, 1), F32))
    return pl.pallas_call(
        body, name=name, grid=(nt,), in_specs=specs, out_specs=out_specs, out_shape=out_shape,
        scratch_shapes=[pltpu.VMEM((4, 8, D), F32)],
        compiler_params=_cparams(("arbitrary",)),
    )(*ins)


CONV_ROWS, CONV_COLS = 64, 512


def _tap_chunks(tt, D):
    for r0 in range(0, tt, min(CONV_ROWS, tt)):
        for c0 in range(0, D, min(CONV_COLS, D)):
            yield r0, min(CONV_ROWS, tt), c0, min(CONV_COLS, D)


def conv_fwd(name, h1, dw, dwb, lng, lnb):
    T, D2 = h1.shape
    D = D2 // 2
    tt = min(CONV_TILE, T)
    hb = tt // CONV_HALO
    KW = dw.shape[0]
    lead = CONV_HALO - (KW - 1)

    def body(a_ref, g_ref, ah_ref, gh_ref, dw_ref, dwb_ref, lng_ref, lnb_ref, u_ref, c_ref, s_ref, ext_ref):
        i = pl.program_id(0)
        u = a_ref[...] * _sigmoid(g_ref[...])
        u_ref[...] = u
        uh = ah_ref[...] * _sigmoid(gh_ref[...])
        ext_ref[pl.ds(0, CONV_HALO), :] = jnp.where(i > 0, uh, 0.0)
        ext_ref[pl.ds(CONV_HALO, tt), :] = u
        for r0, nr, c0, nc in _tap_chunks(tt, D):
            acc = jnp.zeros((nr, nc), F32) + dwb_ref[:, pl.ds(c0, nc)]
            for k in range(KW):
                acc = acc + dw_ref[pl.ds(k, 1), pl.ds(c0, nc)] * ext_ref[pl.ds(r0 + lead + k, nr), pl.ds(c0, nc)]
            c_ref[pl.ds(r0, nr), pl.ds(c0, nc)] = acc
        c = c_ref[...]
        mu = jnp.mean(c, axis=-1, keepdims=True)
        cen = c - mu
        var = jnp.mean(cen * cen, axis=-1, keepdims=True)
        n = cen * lax.rsqrt(var + LN_EPS) * lng_ref[...] + lnb_ref[...]
        s_ref[...] = (n * _sigmoid(n)).astype(BF16)

    halo = lambda col: pl.BlockSpec((CONV_HALO, D), lambda i: (jnp.maximum(i * hb - 1, 0), col))
    return pl.pallas_call(
        body, name=name, grid=(T // tt,),
        in_specs=[pl.BlockSpec((tt, D), lambda i: (i, 0)), pl.BlockSpec((tt, D), lambda i: (i, 1)), halo(0), halo(1),
                  pl.BlockSpec((KW, D), lambda i: (0, 0)), _vec_spec(D), _vec_spec(D), _vec_spec(D)],
        out_specs=[_row_spec(tt, D)] * 3,
        out_shape=[jax.ShapeDtypeStruct((T, D), F32), jax.ShapeDtypeStruct((T, D), F32), jax.ShapeDtypeStruct((T, D), BF16)],
        scratch_shapes=[pltpu.VMEM((tt + CONV_HALO, D), F32)],
        compiler_params=_cparams(("parallel",)),
    )(h1, h1, h1, h1, dw, dwb, lng, lnb)


def conv_bwd_ln(name, ds, c, lng, lnb):
    T, D = c.shape
    tr = min(ROW_TILE, T)
    nt = T // tr

    def body(ds_ref, c_ref, g_ref, b_ref, dc_ref, dg_ref, db_ref, cs_ref, acc_ref):
        i = pl.program_id(0)
        cv = c_ref[...]
        mu = jnp.mean(cv, axis=-1, keepdims=True)
        cen = cv - mu
        var = jnp.mean(cen * cen, axis=-1, keepdims=True)
        rstd = lax.rsqrt(var + LN_EPS)
        chat = cen * rstd
        n = chat * g_ref[...] + b_ref[...]
        sg = _sigmoid(n)
        dn = ds_ref[...] * (sg * (1.0 + n * (1.0 - sg)))
        dxh = dn * g_ref[...]
        m1 = jnp.mean(dxh, axis=-1, keepdims=True)
        m2 = jnp.mean(dxh * chat, axis=-1, keepdims=True)
        dc = rstd * (dxh - m1 - chat * m2)
        dc_ref[...] = dc

        @pl.when(i == 0)
        def _():
            acc_ref[...] = jnp.zeros_like(acc_ref)

        acc_ref[0] += _fold8(dn * chat)
        acc_ref[1] += _fold8(dn)
        acc_ref[2] += _fold8(dc)

        @pl.when(i == nt - 1)
        def _():
            dg_ref[...] = jnp.sum(acc_ref[0], axis=0, keepdims=True)
            db_ref[...] = jnp.sum(acc_ref[1], axis=0, keepdims=True)
            cs_ref[...] = jnp.sum(acc_ref[2], axis=0, keepdims=True)

    return pl.pallas_call(
        body, name=name, grid=(nt,),
        in_specs=[_row_spec(tr, D), _row_spec(tr, D), _vec_spec(D), _vec_spec(D)],
        out_specs=[_row_spec(tr, D), _vec_spec(D), _vec_spec(D), _vec_spec(D)],
        out_shape=[jax.ShapeDtypeStruct((T, D), F32)] + [jax.ShapeDtypeStruct((1, D), F32)] * 3,
        scratch_shapes=[pltpu.VMEM((3, 8, D), F32)],
        compiler_params=_cparams(("arbitrary",)),
    )(ds, c, lng, lnb)


def conv_bwd_taps(name, dc, u, h1, dw):
    T, D = dc.shape
    tt = min(CONV_TILE, T)
    nt = T // tt
    hb = tt // CONV_HALO
    nhb = T // CONV_HALO
    KW = dw.shape[0]
    lead = CONV_HALO - (KW - 1)

    def body(dc_ref, dcn_ref, u_ref, uh_ref, a_ref, g_ref, dw_ref, dh1_ref, db1_ref, ddw_ref,
             edc_ref, eu_ref, du_ref, accw_ref, accb_ref):
        i = pl.program_id(0)

        @pl.when(i == 0)
        def _():
            accw_ref[...] = jnp.zeros_like(accw_ref)
            accb_ref[...] = jnp.zeros_like(accb_ref)

        edc_ref[pl.ds(0, tt), :] = dc_ref[...]
        edc_ref[pl.ds(tt, CONV_HALO), :] = jnp.where(i < nt - 1, dcn_ref[...], 0.0)
        eu_ref[pl.ds(0, CONV_HALO), :] = jnp.where(i > 0, uh_ref[...], 0.0)
        eu_ref[pl.ds(CONV_HALO, tt), :] = u_ref[...]
        for r0, nr, c0, nc in _tap_chunks(tt, D):
            dcv = dc_ref[pl.ds(r0, nr), pl.ds(c0, nc)]
            acc = jnp.zeros((nr, nc), F32)
            for k in range(KW):
                acc = acc + dw_ref[pl.ds(k, 1), pl.ds(c0, nc)] * edc_ref[pl.ds(r0 + (KW - 1) - k, nr), pl.ds(c0, nc)]
                accw_ref[k, :, pl.ds(c0, nc)] += _fold8(dcv * eu_ref[pl.ds(r0 + lead + k, nr), pl.ds(c0, nc)])
            du_ref[pl.ds(r0, nr), pl.ds(c0, nc)] = acc
        du = du_ref[...]
        sg = _sigmoid(g_ref[...])
        da = du * sg
        dg = du * a_ref[...] * sg * (1.0 - sg)
        dh1_ref[:, pl.ds(0, D)] = da.astype(BF16)
        dh1_ref[:, pl.ds(D, D)] = dg.astype(BF16)
        accb_ref[:, pl.ds(0, D)] += _fold8(da)
        accb_ref[:, pl.ds(D, D)] += _fold8(dg)

        @pl.when(i == nt - 1)
        def _():
            db1_ref[...] = jnp.sum(accb_ref[...], axis=0, keepdims=True)
            ddw_ref[...] = jnp.sum(accw_ref[...], axis=1)

    return pl.pallas_call(
        body, name=name, grid=(nt,),
        in_specs=[_row_spec(tt, D),
                  pl.BlockSpec((CONV_HALO, D), lambda i: (jnp.minimum((i + 1) * hb, nhb - 1), 0)),
                  _row_spec(tt, D),
                  pl.BlockSpec((CONV_HALO, D), lambda i: (jnp.maximum(i * hb - 1, 0), 0)),
                  pl.BlockSpec((tt, D), lambda i: (i, 0)), pl.BlockSpec((tt, D), lambda i: (i, 1)),
                  pl.BlockSpec((KW, D), lambda i: (0, 0))],
        out_specs=[_row_spec(tt, 2 * D), _vec_spec(2 * D), pl.BlockSpec((KW, D), lambda i: (0, 0))],
        out_shape=[jax.ShapeDtypeStruct((T, 2 * D), BF16), jax.ShapeDtypeStruct((1, 2 * D), F32),
                   jax.ShapeDtypeStruct((KW, D), F32)],
        scratch_shapes=[pltpu.VMEM((tt + CONV_HALO, D), F32), pltpu.VMEM((tt + CONV_HALO, D), F32),
                        pltpu.VMEM((tt, D), F32), pltpu.VMEM((KW, 8, D), F32), pltpu.VMEM((8, 2 * D), F32)],
        compiler_params=_cparams(("arbitrary",)),
    )(dc, dc, u, u, h1, h1, dw)


def _t5_bucket(dist):
    max_exact = REL_BUCKETS // 2
    large = max_exact + (np.log(np.maximum(dist, 1) / max_exact) / math.log(REL_MAX_DIST / max_exact)
                         * (REL_BUCKETS - max_exact)).astype(np.int32)
    large = np.minimum(large, REL_BUCKETS - 1)
    return np.where(dist < max_exact, dist, large).astype(np.int32)


def _bucket_table(dil):
    i = np.arange(BAND)[:, None]
    j = np.arange(2 * BAND)[None, :]
    delta = i - j + BAND
    return _t5_bucket(np.clip(delta, 0, None) * dil)


def bias_expand(name, rel_bias, dil):
    n_heads = rel_bias.shape[1]
    idx = jnp.asarray(_bucket_table(dil))

    def body(rel_ref, idx_ref, out_ref):
        h = pl.program_id(0)
        idxv = idx_ref[...]
        b = jnp.zeros((BAND, 2 * BAND), F32)
        for bk in range(REL_BUCKETS):
            b = jnp.where(idxv == bk, rel_ref[bk, h], b)
        out_ref[...] = b

    return pl.pallas_call(
        body, name=name, grid=(n_heads,),
        in_specs=[pl.BlockSpec(memory_space=pltpu.SMEM), pl.BlockSpec((BAND, 2 * BAND), lambda h: (0, 0))],
        out_specs=pl.BlockSpec((None, BAND, 2 * BAND), lambda h: (h, 0, 0)),
        out_shape=jax.ShapeDtypeStruct((n_heads, BAND, 2 * BAND), F32),
        compiler_params=_cparams(("arbitrary",)),
    )(rel_bias, idx)


def relbias_grad(name, dsb_list):
    n_heads = dsb_list[0].shape[0]
    idxs = [jnp.asarray(_bucket_table(d)) for _, d in BRANCHES]
    nb = len(BRANCHES)

    def body(*refs):
        ds_refs, idx_refs, out_ref = refs[:nb], refs[nb:2 * nb], refs[2 * nb]
        lane = lax.broadcasted_iota(jnp.int32, (1, 128), 1)
        row = jnp.zeros((1, 128), F32)
        for bk in range(REL_BUCKETS):
            tot = jnp.zeros((1, 1), F32)
            for ds_ref, idx_ref in zip(ds_refs, idx_refs):
                sel = jnp.where(idx_ref[...] == bk, ds_ref[...], 0.0)
                tot = tot + jnp.sum(jnp.sum(sel, axis=0, keepdims=True), axis=1, keepdims=True)
            row = jnp.where(lane == bk, tot, row)
        out_ref[...] = row

    return pl.pallas_call(
        body, name=name, grid=(n_heads,),
        in_specs=[pl.BlockSpec((None, BAND, 2 * BAND), lambda h: (h, 0, 0))] * nb
                 + [pl.BlockSpec((BAND, 2 * BAND), lambda h: (0, 0))] * nb,
        out_specs=pl.BlockSpec((None, 1, 128), lambda h: (h, 0, 0)),
        out_shape=jax.ShapeDtypeStruct((n_heads, 1, 128), F32),
        compiler_params=_cparams(("arbitrary",)),
    )(*dsb_list, *idxs)


def _band_mask():
    i = lax.broadcasted_iota(jnp.int32, (BAND, 2 * BAND), 0)
    j = lax.broadcasted_iota(jnp.int32, (BAND, 2 * BAND), 1)
    return (j >= i) & (j <= i + BAND), j


def _rep2(x):
    return jnp.concatenate([x, x], axis=1)


def _attn_views(T, dil):
    L = T // dil
    lc = min(ATTN_CHUNK, L)
    return L, lc, L // lc, lc // BAND


def attn_fwd(name, q, kv, bias, dil):
    T, D = q.shape
    n_heads = D // HEAD_DIM
    L, lc, nchunk, nsub = _attn_views(T, dil)
    scale = HEAD_DIM ** -0.5
    q3 = q.reshape(L, dil * D)
    kv3 = kv.reshape(L, dil * 2 * D)

    def body(q_ref, k_ref, v_ref, kp_ref, vp_ref, b_ref, o_ref, lse_ref, kext_ref, vext_ref):
        c = pl.program_id(2)
        kext_ref[pl.ds(0, BAND), :] = kp_ref[...]
        vext_ref[pl.ds(0, BAND), :] = vp_ref[...]
        kext_ref[pl.ds(BAND, lc), :] = k_ref[...]
        vext_ref[pl.ds(BAND, lc), :] = v_ref[...]
        band, jcol = _band_mask()
        bias_v = b_ref[...]

        def sub(a, carry):
            off = pl.multiple_of(a * BAND, BAND)
            qa = q_ref[pl.ds(off, BAND), :]
            kw = kext_ref[pl.ds(off, 2 * BAND), :]
            vw = vext_ref[pl.ds(off, 2 * BAND), :]
            s = lax.dot_general(qa, kw, (((1,), (1,)), ((), ())), preferred_element_type=F32) * scale + bias_v
            first = jnp.logical_and(c == 0, a == 0)
            valid = band & jnp.logical_or(jcol >= BAND, jnp.logical_not(first))
            s = jnp.where(valid, s, NEG_BIG)
            m = jnp.max(s, axis=-1, keepdims=True)
            p = jnp.exp(s - m)
            den = jnp.sum(p, axis=-1, keepdims=True)
            pv = lax.dot_general(p.astype(BF16), vw, (((1,), (0,)), ((), ())), preferred_element_type=F32)
            o_ref[pl.ds(off, BAND), :] = pv / den
            lse_ref[pl.ds(off, BAND), :] = jnp.broadcast_to(m + jnp.log(den), (BAND, HEAD_DIM))
            return carry

        lax.fori_loop(0, nsub, sub, 0)

    nsb = lc // BAND
    blk = lambda w, col0: pl.BlockSpec((lc, HEAD_DIM), lambda h, r, c: (c, r * w + col0 + h))
    halo = lambda w, col0: pl.BlockSpec((BAND, HEAD_DIM), lambda h, r, c: (jnp.maximum(c * nsb - 1, 0), r * w + col0 + h))
    nh = n_heads
    o3, lse3 = pl.pallas_call(
        body, name=name, grid=(n_heads, dil, nchunk),
        in_specs=[blk(nh, 0), blk(2 * nh, 0), blk(2 * nh, nh), halo(2 * nh, 0), halo(2 * nh, nh),
                  pl.BlockSpec((None, BAND, 2 * BAND), lambda h, r, c: (h, 0, 0))],
        out_specs=[blk(nh, 0), blk(nh, 0)],
        out_shape=[jax.ShapeDtypeStruct((L, dil * D), F32)] * 2,
        scratch_shapes=[pltpu.VMEM((lc + BAND, HEAD_DIM), BF16)] * 2,
        compiler_params=_cparams(("arbitrary", "arbitrary", "arbitrary")),
    )(q3, kv3, kv3, kv3, kv3, bias)
    return o3.reshape(T, D), lse3.reshape(T, D)


def attn_merge(name, outs, lses):
    T, D = outs[0].shape
    tr = min(ROW_TILE, T)
    nb = len(outs)

    def body(*refs):
        o_refs, l_refs = refs[:nb], refs[nb:2 * nb]
        o_ref, obf_ref, lse_ref = refs[2 * nb:]
        ls = [r[...] for r in l_refs]
        m = functools.reduce(jnp.maximum, ls)
        es = [jnp.exp(l - m) for l in ls]
        tot = functools.reduce(lambda x, y: x + y, es)
        o = functools.reduce(lambda x, y: x + y, [(e / tot) * r[...] for e, r in zip(es, o_refs)])
        o_ref[...] = o
        obf_ref[...] = o.astype(BF16)
        lse_ref[...] = m + jnp.log(tot)

    return pl.pallas_call(
        body, name=name, grid=(T // tr,), in_specs=[_row_spec(tr, D)] * (2 * nb),
        out_specs=[_row_spec(tr, D)] * 3,
        out_shape=[jax.ShapeDtypeStruct((T, D), F32), jax.ShapeDtypeStruct((T, D), BF16), jax.ShapeDtypeStruct((T, D), F32)],
        compiler_params=_cparams(("parallel",)),
    )(*outs, *lses)


def attn_bwd_prep(name, do, o):
    T, D = o.shape
    n_heads = D // HEAD_DIM
    tr = min(ROW_TILE, T)

    def body(do_ref, o_ref, dobf_ref, dsum_ref):
        dobf_ref[...] = do_ref[...].astype(BF16)
        for h in range(n_heads):
            cols = pl.ds(h * HEAD_DIM, HEAD_DIM)
            d = jnp.sum(do_ref[:, cols] * o_ref[:, cols], axis=-1, keepdims=True)
            dsum_ref[:, cols] = jnp.broadcast_to(d, (tr, HEAD_DIM))

    return pl.pallas_call(
        body, name=name, grid=(T // tr,), in_specs=[_row_spec(tr, D)] * 2, out_specs=[_row_spec(tr, D)] * 2,
        out_shape=[jax.ShapeDtypeStruct((T, D), BF16), jax.ShapeDtypeStruct((T, D), F32)],
        compiler_params=_cparams(("parallel",)),
    )(do, o)


def attn_bwd(name, q, kv, do, lse, dsum, bias, dil):
    T, D = q.shape
    n_heads = D // HEAD_DIM
    L, lc, nchunk, nsub = _attn_views(T, dil)
    scale = HEAD_DIM ** -0.5
    q3, do3, lse3, dsum3 = (t.reshape(L, dil * D) for t in (q, do, lse, dsum))
    kv3 = kv.reshape(L, dil * 2 * D)
    nt_dims = (((1,), (1,)), ((), ()))
    tn_dims = (((0,), (0,)), ((), ()))
    nn_dims = (((1,), (0,)), ((), ()))

    def body(q_ref, k_ref, v_ref, do_ref, lse_ref, ds_ref, kp_ref, vp_ref, qn_ref, don_ref, lsen_ref, dsn_ref, b_ref,
             dq_ref, dk_ref, dv_ref, dsb_ref, kext_ref, vext_ref, dkext_ref, dvext_ref):
        r = pl.program_id(1)
        c = pl.program_id(2)
        kext_ref[pl.ds(0, BAND), :] = kp_ref[...]
        vext_ref[pl.ds(0, BAND), :] = vp_ref[...]
        kext_ref[pl.ds(BAND, lc), :] = k_ref[...]
        vext_ref[pl.ds(BAND, lc), :] = v_ref[...]
        dkext_ref[...] = jnp.zeros_like(dkext_ref)
        dvext_ref[...] = jnp.zeros_like(dvext_ref)
        band, jcol = _band_mask()
        bias_v = b_ref[...]

        @pl.when(jnp.logical_and(r == 0, c == 0))
        def _():
            dsb_ref[...] = jnp.zeros_like(dsb_ref)

        def sub(a, carry):
            off = pl.multiple_of(a * BAND, BAND)
            qa = q_ref[pl.ds(off, BAND), :]
            doa = do_ref[pl.ds(off, BAND), :]
            kw = kext_ref[pl.ds(off, 2 * BAND), :]
            vw = vext_ref[pl.ds(off, 2 * BAND), :]
            s = lax.dot_general(qa, kw, nt_dims, preferred_element_type=F32) * scale + bias_v
            first = jnp.logical_and(c == 0, a == 0)
            valid = band & jnp.logical_or(jcol >= BAND, jnp.logical_not(first))
            p = jnp.where(valid, jnp.exp(s - _rep2(lse_ref[pl.ds(off, BAND), :])), 0.0)
            dp = lax.dot_general(doa, vw, nt_dims, preferred_element_type=F32)
            ds = p * (dp - _rep2(ds_ref[pl.ds(off, BAND), :]))
            dsb_ref[...] += ds
            dsb16 = ds.astype(BF16)
            dq_ref[pl.ds(off, BAND), :] = lax.dot_general(dsb16, kw, nn_dims, preferred_element_type=F32) * scale
            dkext_ref[pl.ds(off, 2 * BAND), :] += lax.dot_general(dsb16, qa, tn_dims, preferred_element_type=F32) * scale
            dvext_ref[pl.ds(off, 2 * BAND), :] += lax.dot_general(p.astype(BF16), doa, tn_dims, preferred_element_type=F32)
            return carry

        lax.fori_loop(0, nsub, sub, 0)

        @pl.when(c < nchunk - 1)
        def _():
            qn = qn_ref[...]
            don = don_ref[...]
            kl = kext_ref[pl.ds(lc, BAND), :]
            vl = vext_ref[pl.ds(lc, BAND), :]
            s = lax.dot_general(qn, kl, nt_dims, preferred_element_type=F32) * scale + bias_v[:, :BAND]
            p = jnp.where(band[:, :BAND], jnp.exp(s - lsen_ref[...]), 0.0)
            dp = lax.dot_general(don, vl, nt_dims, preferred_element_type=F32)
            ds = p * (dp - dsn_ref[...])
            dkext_ref[pl.ds(lc, BAND), :] += lax.dot_general(ds.astype(BF16), qn, tn_dims, preferred_element_type=F32) * scale
            dvext_ref[pl.ds(lc, BAND), :] += lax.dot_general(p.astype(BF16), don, tn_dims, preferred_element_type=F32)

        dk_ref[...] = dkext_ref[pl.ds(BAND, lc), :]
        dv_ref[...] = dvext_ref[pl.ds(BAND, lc), :]

    nsb = lc // BAND
    nblk = L // BAND
    nh = n_heads
    blk = lambda w, col0: pl.BlockSpec((lc, HEAD_DIM), lambda h, r, c: (c, r * w + col0 + h))
    prev = lambda w, col0: pl.BlockSpec((BAND, HEAD_DIM), lambda h, r, c: (jnp.maximum(c * nsb - 1, 0), r * w + col0 + h))
    nxt = lambda: pl.BlockSpec((BAND, HEAD_DIM), lambda h, r, c: (jnp.minimum((c + 1) * nsb, nblk - 1), r * nh + h))
    one = blk(nh, 0)
    dq3, dk3, dv3, dsb = pl.pallas_call(
        body, name=name, grid=(n_heads, dil, nchunk),
        in_specs=[one, blk(2 * nh, 0), blk(2 * nh, nh), one, one, one, prev(2 * nh, 0), prev(2 * nh, nh),
                  nxt(), nxt(), nxt(), nxt(), pl.BlockSpec((None, BAND, 2 * BAND), lambda h, r, c: (h, 0, 0))],
        out_specs=[one, one, one, pl.BlockSpec((None, BAND, 2 * BAND), lambda h, r, c: (h, 0, 0))],
        out_shape=[jax.ShapeDtypeStruct((L, dil * D), F32)] * 3 + [jax.ShapeDtypeStruct((n_heads, BAND, 2 * BAND), F32)],
        scratch_shapes=[pltpu.VMEM((lc + BAND, HEAD_DIM), BF16)] * 2 + [pltpu.VMEM((lc + BAND, HEAD_DIM), F32)] * 2,
        compiler_params=_cparams(("arbitrary", "arbitrary", "arbitrary")),
    )(q3, kv3, kv3, do3, lse3, dsum3, kv3, kv3, q3, do3, lse3, dsum3, bias)
    return dq3.reshape(T, D), dk3.reshape(T, D), dv3.reshape(T, D), dsb


def sum_branches(name, dqs, dks, dvs):
    T, D = dqs[0].shape
    tr = min(ROW_TILE, T)
    nb = len(dqs)

    def body(*refs):
        dq_refs, dk_refs, dv_refs = refs[:nb], refs[nb:2 * nb], refs[2 * nb:3 * nb]
        dq_ref, dkv_ref = refs[3 * nb:]
        add = lambda rs: functools.reduce(lambda x, y: x + y, [r[...] for r in rs])
        dq_ref[...] = add(dq_refs).astype(BF16)
        dkv_ref[:, pl.ds(0, D)] = add(dk_refs).astype(BF16)
        dkv_ref[:, pl.ds(D, D)] = add(dv_refs).astype(BF16)

    return pl.pallas_call(
        body, name=name, grid=(T // tr,), in_specs=[_row_spec(tr, D)] * (3 * nb),
        out_specs=[_row_spec(tr, D), _row_spec(tr, 2 * D)],
        out_shape=[jax.ShapeDtypeStruct((T, D), BF16), jax.ShapeDtypeStruct((T, 2 * D), BF16)],
        compiler_params=_cparams(("parallel",)),
    )(*dqs, *dks, *dvs)


def _divisor_tile(n, cap, mult):
    if n <= cap:
        return n
    t = cap - cap % mult
    while n % t:
        t -= mult
    return t


def _tile2(R, C):
    return _divisor_tile(R, 512, 8), _divisor_tile(C, 1024, 128)


def half_cast(name, dw, core):
    S, R, C = dw.shape
    hr = R // 2
    tr, tc = _tile2(hr, C)
    nrb = hr // tr

    def body(c_ref, x_ref, o_ref):
        o_ref[...] = x_ref[...].astype(BF16)

    return pl.pallas_call(
        body, name=name,
        grid_spec=pltpu.PrefetchScalarGridSpec(
            num_scalar_prefetch=1, grid=(S, nrb, C // tc),
            in_specs=[pl.BlockSpec((None, tr, tc), lambda s, i, j, c: (s, (1 - c[0]) * nrb + i, j))],
            out_specs=pl.BlockSpec((None, tr, tc), lambda s, i, j, c: (s, i, j))),
        out_shape=jax.ShapeDtypeStruct((S, hr, C), BF16),
        compiler_params=_cparams(("parallel", "parallel", "parallel")),
    )(core, dw)


def pair_sum(name, dw, recv, core):
    S, R, C = dw.shape
    hr = R // 2
    tr, tc = _tile2(hr, C)
    nrb = hr // tr

    def body(c_ref, x_ref, r_ref, p_ref, pbf_ref):
        p = x_ref[...] + r_ref[...].astype(F32)
        p_ref[...] = p
        pbf_ref[...] = p.astype(BF16)

    out = pl.BlockSpec((None, tr, tc), lambda s, i, j, c: (s, i, j))
    return pl.pallas_call(
        body, name=name,
        grid_spec=pltpu.PrefetchScalarGridSpec(
            num_scalar_prefetch=1, grid=(S, nrb, C // tc),
            in_specs=[pl.BlockSpec((None, tr, tc), lambda s, i, j, c: (s, c[0] * nrb + i, j)), out],
            out_specs=[out, out]),
        out_shape=[jax.ShapeDtypeStruct((S, hr, C), F32), jax.ShapeDtypeStruct((S, hr, C), BF16)],
        compiler_params=_cparams(("parallel", "parallel", "parallel")),
    )(core, dw, recv)


def chip_sum(name, p, recv, chip, core):
    S, hr, C = p.shape
    tr, tc = _tile2(hr, C)
    nrb = hr // tr

    def body(chip_ref, core_ref, p_ref, r_ref, o_ref):
        acc = p_ref[...]
        for t in range(N_CHIPS - 1):
            acc = acc + r_ref[t].astype(F32)
        o_ref[...] = acc

    return pl.pallas_call(
        body, name=name,
        grid_spec=pltpu.PrefetchScalarGridSpec(
            num_scalar_prefetch=2, grid=(nrb, C // tc),
            in_specs=[pl.BlockSpec((None, tr, tc), lambda i, j, s, c: (s[0], i, j)),
                      pl.BlockSpec((N_CHIPS - 1, tr, tc), lambda i, j, s, c: (0, i, j))],
            out_specs=pl.BlockSpec((tr, tc), lambda i, j, s, c: (c[0] * nrb + i, j))),
        out_shape=jax.ShapeDtypeStruct((2 * hr, C), F32),
        compiler_params=_cparams(("parallel", "parallel")),
    )(chip, core, p, recv)


def adamw(name, w, g, m, v):
    R, C = w.shape
    tr, tc = _tile2(R, C)
    c1 = 1.0 - ADAM_B1 ** ADAM_STEP
    c2 = 1.0 - ADAM_B2 ** ADAM_STEP

    def body(w_ref, g_ref, m_ref, v_ref, d_ref, nm_ref, nv_ref):
        gv = g_ref[...]
        nm = ADAM_B1 * m_ref[...] + (1.0 - ADAM_B1) * gv
        nv = ADAM_B2 * v_ref[...] + (1.0 - ADAM_B2) * (gv * gv)
        nm_ref[...] = nm
        nv_ref[...] = nv
        d_ref[...] = -ADAM_LR * ((nm / c1) / (jnp.sqrt(nv / c2) + ADAM_EPS) + ADAM_WD * w_ref[...])

    spec = pl.BlockSpec((tr, tc), lambda i, j: (i, j))
    return pl.pallas_call(
        body, name=name, grid=(R // tr, C // tc), in_specs=[spec] * 4, out_specs=[spec] * 3,
        out_shape=[jax.ShapeDtypeStruct((R, C), F32)] * 3,
        compiler_params=_cparams(("parallel", "parallel")),
    )(w, g, m, v)


def sum_devices(name, gathered):
    n, R, C = gathered.shape

    def body(x_ref, o_ref):
        acc = x_ref[0]
        for d in range(1, n):
            acc = acc + x_ref[d]
        o_ref[...] = acc

    return pl.pallas_call(
        body, name=name, in_specs=[pl.BlockSpec(memory_space=pltpu.VMEM)],
        out_specs=pl.BlockSpec(memory_space=pltpu.VMEM),
        out_shape=jax.ShapeDtypeStruct((R, C), F32),
    )(gathered)


def _place():
    x, y, c = lax.axis_index("x"), lax.axis_index("y"), lax.axis_index("c")
    return x, y, c


def _other_chips(x, y):
    return [(1 - x, y), (x, 1 - y), (1 - x, 1 - y)]


def all_gather8(name, block):
    R, C = block.shape

    def body(x_ref, out_ref, send_sems, recv_sems, local_sem):
        x, y, c = _place()
        me, sibling = (x, y, c), (x, y, 1 - c)
        chips = _other_chips(x, y)

        def rows(px, py, pc):
            return out_ref.at[4 * px + 2 * py + pc]

        def copy(k, blk, to, src=None):
            return pltpu.make_async_remote_copy(
                src_ref=rows(*blk) if src is None else src, dst_ref=rows(*blk),
                send_sem=send_sems.at[k], recv_sem=recv_sems.at[k], device_id=to, device_id_type=MESH)

        mine = pltpu.make_async_copy(x_ref, rows(*me), local_sem)
        mine.start()
        first = [copy(0, me, sibling, src=x_ref)]
        first += [copy(1 + j, me, (*chip, c), src=x_ref) for j, chip in enumerate(chips)]
        for cp in first:
            cp.start()
        passed = [copy(4 + j, (*chip, c), sibling) for j, chip in enumerate(chips)]
        for j, chip in enumerate(chips):
            copy(1 + j, (*chip, c), me).wait_recv()
            passed[j].start()
        copy(0, sibling, me).wait_recv()
        for j, chip in enumerate(chips):
            copy(4 + j, (*chip, 1 - c), me).wait_recv()
        for cp in first + passed:
            cp.wait_send()
        mine.wait()

    return pl.pallas_call(
        body, name=name, out_shape=jax.ShapeDtypeStruct((N_DEV, R, C), block.dtype),
        in_specs=[pl.BlockSpec(memory_space=pltpu.VMEM)], out_specs=pl.BlockSpec(memory_space=pltpu.VMEM),
        scratch_shapes=[pltpu.SemaphoreType.DMA((7,)), pltpu.SemaphoreType.DMA((7,)), pltpu.SemaphoreType.DMA],
    )(block)


def gather_weights(name, shards):
    n = len(shards)

    def body(*refs):
        ins, outs = refs[:n], refs[n:2 * n]
        ici_send, ici_recv, d2d_send, d2d_recv, local_sems = refs[2 * n:]
        x, y, c = _place()
        me_chip = 2 * x + y
        sibling = (x, y, 1 - c)
        chips = _other_chips(x, y)

        def half(ref, h):
            hr = ref.shape[0] // 2
            return ref.at[pl.ds(h * hr, hr)]

        local = [pltpu.make_async_copy(ins[w], outs[w].at[me_chip], local_sems.at[w]) for w in range(n)]
        for cp in local:
            cp.start()

        def ici(w, t, chip, src_chip_slot):
            return pltpu.make_async_remote_copy(
                src_ref=half(ins[w], c), dst_ref=half(outs[w].at[src_chip_slot], c),
                send_sem=ici_send.at[w * 3 + t], recv_sem=ici_recv.at[w * 3 + t],
                device_id=(*chip, c), device_id_type=MESH)

        def d2d(w, t, slot, h):
            return pltpu.make_async_remote_copy(
                src_ref=half(outs[w].at[slot], h), dst_ref=half(outs[w].at[slot], h),
                send_sem=d2d_send.at[w * 3 + t], recv_sem=d2d_recv.at[w * 3 + t],
                device_id=sibling, device_id_type=MESH)

        sends = [ici(w, t, chip, me_chip) for w in range(n) for t, chip in enumerate(chips)]
        for cp in sends:
            cp.start()
        fwd = []
        for w in range(n):
            for t, chip in enumerate(chips):
                slot = 2 * chip[0] + chip[1]
                ici(w, t, chip, slot).wait_recv()
                cp = d2d(w, t, slot, c)
                cp.start()
                fwd.append(cp)
        for w in range(n):
            for t, chip in enumerate(chips):
                d2d(w, t, 2 * chip[0] + chip[1], 1 - c).wait_recv()
        for cp in sends + fwd:
            cp.wait_send()
        for cp in local:
            cp.wait()

    any_spec = pl.BlockSpec(memory_space=pl.ANY)
    return pl.pallas_call(
        body, name=name, in_specs=[any_spec] * n, out_specs=[any_spec] * n,
        out_shape=[jax.ShapeDtypeStruct((N_CHIPS,) + s.shape, s.dtype) for s in shards],
        scratch_shapes=[pltpu.SemaphoreType.DMA((3 * n,))] * 4 + [pltpu.SemaphoreType.DMA((n,))],
    )(*shards)


def swap_with_sibling(name, bufs):
    n = len(bufs)

    def body(*refs):
        ins, outs, send_sems, recv_sems = refs[:n], refs[n:2 * n], refs[2 * n], refs[2 * n + 1]
        x, y, c = _place()
        cps = [pltpu.make_async_remote_copy(src_ref=ins[w], dst_ref=outs[w], send_sem=send_sems.at[w],
                                            recv_sem=recv_sems.at[w], device_id=(x, y, 1 - c), device_id_type=MESH)
               for w in range(n)]
        for cp in cps:
            cp.start()
        for cp in cps:
            cp.wait()

    any_spec = pl.BlockSpec(memory_space=pl.ANY)
    return pl.pallas_call(
        body, name=name, in_specs=[any_spec] * n, out_specs=[any_spec] * n,
        out_shape=[jax.ShapeDtypeStruct(b.shape, b.dtype) for b in bufs],
        scratch_shapes=[pltpu.SemaphoreType.DMA((n,))] * 2,
    )(*bufs)


def scatter_to_chips(name, bufs):
    n = len(bufs)

    def body(*refs):
        ins, outs, send_sems, recv_sems = refs[:n], refs[n:2 * n], refs[2 * n], refs[2 * n + 1]
        x, y, c = _place()
        chips = _other_chips(x, y)
        cps = [pltpu.make_async_remote_copy(src_ref=ins[w].at[2 * chip[0] + chip[1]], dst_ref=outs[w].at[t],
                                            send_sem=send_sems.at[w * 3 + t], recv_sem=recv_sems.at[w * 3 + t],
                                            device_id=(*chip, c), device_id_type=MESH)
               for w in range(n) for t, chip in enumerate(chips)]
        for cp in cps:
            cp.start()
        for cp in cps:
            cp.wait()

    any_spec = pl.BlockSpec(memory_space=pl.ANY)
    return pl.pallas_call(
        body, name=name, in_specs=[any_spec] * n, out_specs=[any_spec] * n,
        out_shape=[jax.ShapeDtypeStruct((N_CHIPS - 1,) + b.shape[1:], b.dtype) for b in bufs],
        scratch_shapes=[pltpu.SemaphoreType.DMA((3 * n,))] * 2,
    )(*bufs)


def share_half_with_sibling(name, bufs):
    n = len(bufs)

    def body(*refs):
        ins, outs, send_sems, recv_sems = refs[:n], refs[n:2 * n], refs[2 * n], refs[2 * n + 1]
        x, y, c = _place()
        cps = []
        for w in range(n):
            hr = outs[w].shape[0] // 2
            mine = outs[w].at[pl.ds(c * hr, hr)]
            cps.append(pltpu.make_async_remote_copy(src_ref=mine, dst_ref=mine, send_sem=send_sems.at[w],
                                                    recv_sem=recv_sems.at[w], device_id=(x, y, 1 - c), device_id_type=MESH))
        for cp in cps:
            cp.start()
        for w, cp in enumerate(cps):
            cp.wait_send()
            hr = outs[w].shape[0] // 2
            theirs = outs[w].at[pl.ds((1 - c) * hr, hr)]
            pltpu.make_async_remote_copy(src_ref=theirs, dst_ref=theirs, send_sem=send_sems.at[w], recv_sem=recv_sems.at[w],
                                         device_id=(x, y, 1 - c), device_id_type=MESH).wait_recv()

    any_spec = pl.BlockSpec(memory_space=pl.ANY)
    return pl.pallas_call(
        body, name=name, in_specs=[any_spec] * n, out_specs=[any_spec] * n,
        out_shape=[jax.ShapeDtypeStruct(b.shape, b.dtype) for b in bufs],
        input_output_aliases={w: w for w in range(n)},
        scratch_shapes=[pltpu.SemaphoreType.DMA((n,))] * 2,
    )(*bufs)


def _pack(arrs):
    parts = []
    for a in arrs:
        flat = a.reshape(-1).astype(F32)
        n = flat.shape[0]
        padded = -(-n // 1024) * 1024
        parts.append(jnp.pad(flat, (0, padded - n)).reshape(padded // 128, 128))
    return jnp.concatenate(parts, axis=0)


def _unpack(buf, shapes):
    out, row = [], 0
    for shp in shapes:
        n = int(np.prod(shp))
        rows = -(-n // 1024) * 8
        out.append(buf[row:row + rows].reshape(-1)[:n].reshape(shp))
        row += rows
    return out


def _bias_epi(acc, b):
    return (acc + b,)


def local_step(x, target, W, P):
    T, D = x.shape
    g = {}
    plain = lambda acc: (acc,)

    (h1,) = mm_nn("pw1_fwd", x, W["pw1"], "col", _bias_epi, [F32], extras=[(P["pw1_b"], "row")])
    u, cpre, s = conv_fwd("conv_fwd", h1, P["dw_w"], P["dw_b"], P["cln_g"], P["cln_b"])
    (mix0,) = mm_nn("pw2_fwd", s, W["pw2"], "row", _bias_epi, [F32], extras=[(P["pw2_b"], "row")])
    ln = [None] * 4
    gam = [P["ln_mix_g"][0:1], P["ln_mlp_g"][0:1], P["ln_mix_g"][1:2], P["ln_mlp_g"][1:2]]
    bet = [P["ln_mix_b"][0:1], P["ln_mlp_b"][0:1], P["ln_mix_b"][1:2], P["ln_mlp_b"][1:2]]
    ln[0] = ln_fwd("ln0_fwd", mix0, x)(gam[0], bet[0])

    def mlp_fwd(tag, i_ln, w1, w2):
        xhat, rstd, xbf = ln[i_ln]
        def up_epi(acc):
            r = jnp.maximum(acc, 0.0)
            return r * r, r

        hid, relu = mm_nn(tag + "_up", xbf, w1, "col", up_epi, [BF16, BF16])
        (mlp,) = mm_nn(tag + "_down", hid, w2, "row", plain, [F32])
        ln[i_ln + 1] = ln_fwd(tag + "_ln", mlp, xhat, gam[i_ln], bet[i_ln])(gam[i_ln + 1], bet[i_ln + 1])
        return hid, relu

    hid0 = mlp_fwd("mlp0", 0, W["w1_0"], W["w2_0"])

    x2bf = ln[1][2]
    (kv,) = mm_nn("kv_fwd", x2bf, W["kv"], "col", plain, [BF16])
    (q,) = mm_nn("q_fwd", x2bf, W["wq"], "row", plain, [BF16])
    biases = [bias_expand("bias_d%d" % d, P["rel_bias"], d) for _, d in BRANCHES]
    outs, lses = [], []
    for (win, d), b in zip(BRANCHES, biases):
        assert win // d == BAND and (T // d) % BAND == 0
        o_b, l_b = attn_fwd("attn_fwd_d%d" % d, q, kv, b, d)
        outs.append(o_b)
        lses.append(l_b)
    o, obf, lse = attn_merge("attn_merge", outs, lses)
    (attn,) = mm_nn("wo_fwd", obf, W["wo"], "row", plain, [F32])
    ln[2] = ln_fwd("ln2_fwd", attn, ln[1][0], gam[1], bet[1])(gam[2], bet[2])
    hid1 = mlp_fwd("mlp1", 2, W["w1_1"], W["w2_1"])

    dr3, dr3bf, g["ln_mlp_g1"], g["ln_mlp_b1"], _, loss_sum = ln_bwd(
        "ln3_bwd", ln[3][0], ln[3][1], gam[3], target=target, beta=bet[3])

    def mlp_bwd(tag, i_ln, w1, w2, hid_relu, dr, drbf):
        xbf = ln[i_ln][2]
        hid, relu = hid_relu
        g[tag + "_w2"] = mm_tn(tag + "_dw2", hid, drbf, w2.shape, "row")
        (dp,) = mm_nt(tag + "_dhid", drbf, w2, "row",
                      lambda acc, r: (acc * (2.0 * r.astype(F32)),), [BF16], extras=[(relu, "tile")])
        g[tag + "_w1"] = mm_tn(tag + "_dw1", xbf, dp, w1.shape, "col")
        (dx,) = mm_nt(tag + "_dx", dp, w1, "col", lambda acc, e: (acc + ALPHA * e,), [F32], extras=[(dr, "tile")])
        return dx

    dx3 = mlp_bwd("mlp1", 2, W["w1_1"], W["w2_1"], hid1, dr3, dr3bf)
    dr2, dr2bf, g["ln_mix_g1"], g["ln_mix_b1"], _ = ln_bwd("ln2_bwd", ln[2][0], ln[2][1], gam[2], dy=dx3)
    g["wo"] = mm_tn("wo_dw", obf, dr2bf, W["wo"].shape, "row")
    (do,) = mm_nt("wo_dx", dr2bf, W["wo"], "row", plain, [F32])
    dobf, dsum = attn_bwd_prep("attn_bwd_prep", do, o)
    dqs, dks, dvs, dsbs = [], [], [], []
    for (win, d), b in zip(BRANCHES, biases):
        dq_b, dk_b, dv_b, dsb = attn_bwd("attn_bwd_d%d" % d, q, kv, dobf, lse, dsum, b, d)
        dqs.append(dq_b)
        dks.append(dk_b)
        dvs.append(dv_b)
        dsbs.append(dsb)
    g["rel_bias"] = relbias_grad("relbias_grad", dsbs)[:, 0, :REL_BUCKETS].T
    dq, dkv = sum_branches("attn_bwd_sum", dqs, dks, dvs)
    g["wq"] = mm_tn("wq_dw", x2bf, dq, W["wq"].shape, "row")
    g["kv"] = mm_tn("kv_dw", x2bf, dkv, W["kv"].shape, "col")
    (dx2a,) = mm_nt("wq_dx", dq, W["wq"], "row", lambda acc, e: (acc + ALPHA * e,), [F32], extras=[(dr2, "tile")])
    (dx2,) = mm_nt("kv_dx", dkv, W["kv"], "col", lambda acc, e: (acc + e,), [F32], extras=[(dx2a, "tile")])

    dr1, dr1bf, g["ln_mlp_g0"], g["ln_mlp_b0"], _ = ln_bwd("ln1_bwd", ln[1][0], ln[1][1], gam[1], dy=dx2)
    dx1 = mlp_bwd("mlp0", 0, W["w1_0"], W["w2_0"], hid0, dr1, dr1bf)
    dr0, dr0bf, g["ln_mix_g0"], g["ln_mix_b0"], g["pw2_b"] = ln_bwd("ln0_bwd", ln[0][0], ln[0][1], gam[0], dy=dx1)

    g["pw2"] = mm_tn("pw2_dw", s, dr0bf, W["pw2"].shape, "row")
    (ds,) = mm_nt("pw2_dx", dr0bf, W["pw2"], "row", plain, [F32])
    dc, g["cln_g"], g["cln_b"], g["dw_b"] = conv_bwd_ln("conv_bwd_ln", ds, cpre, P["cln_g"], P["cln_b"])
    dh1, g["pw1_b"], g["dw_w"] = conv_bwd_taps("conv_bwd_taps", dc, u, h1, P["dw_w"])
    g["pw1"] = mm_tn("pw1_dw", x, dh1, W["pw1"].shape, "col")
    (dx,) = mm_nt("pw1_dx", dh1, W["pw1"], "col", lambda acc, e: (acc + ALPHA * e,), [F32], extras=[(dr0, "tile")])
    return loss_sum, dx, g


BIG = ("pw1", "pw2", "kv", "wq", "wo", "w1_0", "w1_1", "w2_0", "w2_1")


def kernel(x, conv_pw1_w, conv_pw1_b, conv_dw_w, conv_dw_b, conv_ln_g, conv_ln_b, conv_pw2_w, conv_pw2_b, w_kv, attn_wq, attn_wo, rel_bias, mlp_w1, mlp_w2, ln_mix_g, ln_mix_b, ln_mlp_g, ln_mlp_b, loss_target, m_conv_pw1_w, m_conv_pw1_b, m_conv_dw_w, m_conv_dw_b, m_conv_ln_g, m_conv_ln_b, m_conv_pw2_w, m_conv_pw2_b, m_w_kv, m_attn_wq, m_attn_wo, m_rel_bias, m_mlp_w1, m_mlp_w2, m_ln_mix_g, m_ln_mix_b, m_ln_mlp_g, m_ln_mlp_b, v_conv_pw1_w, v_conv_pw1_b, v_conv_dw_w, v_conv_dw_b, v_conv_ln_g, v_conv_ln_b, v_conv_pw2_w, v_conv_pw2_b, v_w_kv, v_attn_wq, v_attn_wo, v_rel_bias, v_mlp_w1, v_mlp_w2, v_ln_mix_g, v_ln_mix_b, v_ln_mlp_g, v_ln_mlp_b):
    _, T, D = x.shape
    xi, yi, ci = _place()
    chip = 2 * xi + yi
    core = jnp.reshape(ci, (1,)).astype(jnp.int32)
    chip1 = jnp.reshape(chip, (1,)).astype(jnp.int32)

    shard = {"pw1": conv_pw1_w[0], "pw2": conv_pw2_w[0], "kv": w_kv, "wq": attn_wq[0], "wo": attn_wo[0],
             "w1_0": mlp_w1[0], "w1_1": mlp_w1[1], "w2_0": mlp_w2[0], "w2_1": mlp_w2[1]}
    mom = {"pw1": m_conv_pw1_w[0], "pw2": m_conv_pw2_w[0], "kv": m_w_kv, "wq": m_attn_wq[0], "wo": m_attn_wo[0],
           "w1_0": m_mlp_w1[0], "w1_1": m_mlp_w1[1], "w2_0": m_mlp_w2[0], "w2_1": m_mlp_w2[1]}
    vel = {"pw1": v_conv_pw1_w[0], "pw2": v_conv_pw2_w[0], "kv": v_w_kv, "wq": v_attn_wq[0], "wo": v_attn_wo[0],
           "w1_0": v_mlp_w1[0], "w1_1": v_mlp_w1[1], "w2_0": v_mlp_w2[0], "w2_1": v_mlp_w2[1]}
    gathered = gather_weights("gather_weights", [shard[n].astype(BF16) for n in BIG])
    W = dict(zip(BIG, gathered))

    sharded_small = [conv_pw1_b, conv_dw_w[0], conv_dw_b, conv_ln_g, conv_ln_b, conv_pw2_b]
    sh_shapes = [a.shape for a in sharded_small]
    small_all = all_gather8("gather_small", _pack(sharded_small))
    per_chip = [_unpack(small_all[2 * j], sh_shapes) for j in range(N_CHIPS)]
    full = [jnp.concatenate([per_chip[j][i] for j in range(N_CHIPS)], axis=-1) for i in range(len(sharded_small))]
    P = dict(pw1_b=full[0], dw_w=full[1], dw_b=full[2], cln_g=full[3], cln_b=full[4], pw2_b=full[5],
             rel_bias=rel_bias, ln_mix_g=ln_mix_g, ln_mix_b=ln_mix_b, ln_mlp_g=ln_mlp_g, ln_mlp_b=ln_mlp_b)

    loss_sum, dx, g = local_step(x[0], loss_target[0], W, P)
    loss = (0.5 / D) * lax.psum(loss_sum[0, 0], ("x", "y", "c"))

    dws = [g[{"w1_0": "mlp0_w1", "w1_1": "mlp1_w1", "w2_0": "mlp0_w2", "w2_1": "mlp1_w2"}.get(n, n)] for n in BIG]
    to_sibling = [half_cast("rs_cast_" + n, dw, core) for n, dw in zip(BIG, dws)]
    from_sibling = swap_with_sibling("rs_pair_swap", to_sibling)
    pairs = [pair_sum("rs_pair_sum_" + n, dw, r, core) for n, dw, r in zip(BIG, dws, from_sibling)]
    from_chips = scatter_to_chips("rs_chip_scatter", [p[1] for p in pairs])
    halves = [chip_sum("rs_chip_sum_" + n, p[0], r, chip1, core) for n, p, r in zip(BIG, pairs, from_chips)]
    grads_big = dict(zip(BIG, share_half_with_sibling("rs_half_share", halves)))
    upd_big = {n: adamw("adamw_" + n, shard[n], grads_big[n], mom[n], vel[n]) for n in BIG}

    small_names = ["pw1_b", "dw_w", "dw_b", "cln_g", "cln_b", "pw2_b", "rel_bias",
                   "ln_mix_g0", "ln_mix_g1", "ln_mix_b0", "ln_mix_b1", "ln_mlp_g0", "ln_mlp_g1", "ln_mlp_b0", "ln_mlp_b1"]
    small_grads = [g[n] for n in small_names]
    sg_shapes = [a.shape for a in small_grads]
    summed = sum_devices("small_grad_sum", all_gather8("gather_small_grads", _pack(small_grads)))
    sg = dict(zip(small_names, _unpack(summed, sg_shapes)))

    def my_cols(a, width):
        return lax.dynamic_slice_in_dim(a, chip * width, width, axis=a.ndim - 1)

    small_g = [my_cols(sg["pw1_b"], conv_pw1_b.shape[-1]),
               my_cols(sg["dw_w"], conv_dw_w.shape[-1])[None],
               my_cols(sg["dw_b"], conv_dw_b.shape[-1]), my_cols(sg["cln_g"], conv_ln_g.shape[-1]),
               my_cols(sg["cln_b"], conv_ln_b.shape[-1]), my_cols(sg["pw2_b"], conv_pw2_b.shape[-1]),
               sg["rel_bias"],
               jnp.concatenate([sg["ln_mix_g0"], sg["ln_mix_g1"]], axis=0),
               jnp.concatenate([sg["ln_mix_b0"], sg["ln_mix_b1"]], axis=0),
               jnp.concatenate([sg["ln_mlp_g0"], sg["ln_mlp_g1"]], axis=0),
               jnp.concatenate([sg["ln_mlp_b0"], sg["ln_mlp_b1"]], axis=0)]
    small_w = [conv_pw1_b, conv_dw_w, conv_dw_b, conv_ln_g, conv_ln_b, conv_pw2_b, rel_bias, ln_mix_g, ln_mix_b, ln_mlp_g, ln_mlp_b]
    small_m = [m_conv_pw1_b, m_conv_dw_w, m_conv_dw_b, m_conv_ln_g, m_conv_ln_b, m_conv_pw2_b, m_rel_bias, m_ln_mix_g, m_ln_mix_b, m_ln_mlp_g, m_ln_mlp_b]
    small_v = [v_conv_pw1_b, v_conv_dw_w, v_conv_dw_b, v_conv_ln_g, v_conv_ln_b, v_conv_pw2_b, v_rel_bias, v_ln_mix_g, v_ln_mix_b, v_ln_mlp_g, v_ln_mlp_b]
    sw_shapes = [a.shape for a in small_w]
    small_g = [a.reshape(s) for a, s in zip(small_g, sw_shapes)]
    upd_small = adamw("adamw_small", _pack(small_w), _pack(small_g), _pack(small_m), _pack(small_v))
    sd, snm, snv = (_unpack(b, sw_shapes) for b in upd_small)

    def big_out(tree):
        return dict(pw1=tree["pw1"][None], pw2=tree["pw2"][None], kv=tree["kv"], wq=tree["wq"][None], wo=tree["wo"][None],
                    w1=jnp.stack([tree["w1_0"], tree["w1_1"]]), w2=jnp.stack([tree["w2_0"], tree["w2_1"]]))

    def ordered(big, small):
        return [big["pw1"], small[0], small[1], small[2], small[3], small[4], big["pw2"], small[5], big["kv"], big["wq"],
                big["wo"], small[6], big["w1"], big["w2"], small[7], small[8], small[9], small[10]]

    grads = ordered(big_out(grads_big), small_g)
    deltas = ordered(big_out({n: upd_big[n][0] for n in BIG}), sd)
    new_m = ordered(big_out({n: upd_big[n][1] for n in BIG}), snm)
    new_v = ordered(big_out({n: upd_big[n][2] for n in BIG}), snv)
    return (loss, dx[None], *grads, *deltas, *new_m, *new_v)
```

```python
import functools
import math

import numpy as np
import jax
import jax.numpy as jnp
from jax import lax
from jax.experimental import pallas as pl
from jax.experimental.pallas import tpu as pltpu

F32 = jnp.float32
BF16 = jnp.bfloat16

HEAD_DIM = 128
BAND = 128
BRANCHES = ((128, 1), (512, 4), (2048, 16))
CONV_WIDTH = 31
CONV_HALO = 32
REL_BUCKETS = 32
REL_MAX_DIST = 2048
DEPTH = 2
ALPHA = (2 * DEPTH) ** 0.25
LN_EPS = 1e-5
ADAM_LR, ADAM_B1, ADAM_B2, ADAM_EPS, ADAM_WD, ADAM_STEP = 0.001, 0.9, 0.999, 1e-08, 0.01, 10

N_CHIPS = 4
N_DEV = 8
MESH = pl.DeviceIdType.MESH
VMEM_LIMIT_BYTES = 56 * 1024 * 1024
MM_TM, MM_TN, MM_TK = 1024, 1024, 2048
ROW_TILE = 256
CONV_TILE = 128
ATTN_CHUNK = 1024
NEG_BIG = -1e30


def _cparams(sem):
    return pltpu.CompilerParams(dimension_semantics=sem, vmem_limit_bytes=VMEM_LIMIT_BYTES)


def _sigmoid(x):
    return 1.0 / (1.0 + jnp.exp(-x))


def _wspec(wshape, axis, br, bc, rsel, csel):
    _, R, C = wshape
    if axis == "col":
        nb = C // bc
        assert nb * bc == C, (wshape, bc)
        return pl.BlockSpec((None, br, bc), lambda *g: (csel(*g) // nb, rsel(*g), csel(*g) % nb))
    nb = R // br
    assert nb * br == R, (wshape, br)
    return pl.BlockSpec((None, br, bc), lambda *g: (rsel(*g) // nb, rsel(*g) % nb, csel(*g)))


def _full_dims(wshape, axis):
    _, R, C = wshape
    return (R, N_CHIPS * C) if axis == "col" else (N_CHIPS * R, C)


def _mm_body(nk, kinds, n_out, dims, epilogue):
    n_extra = len(kinds)

    def body(*refs):
        a_ref, b_ref = refs[0], refs[1]
        extra = [r for r, kind in zip(refs[2:2 + n_extra], kinds) if kind != "dep"]
        outs = refs[2 + n_extra:2 + n_extra + n_out]
        part = lax.dot_general(a_ref[...].astype(BF16), b_ref[...].astype(BF16), (dims, ((), ())),
                               preferred_element_type=F32)
        if nk == 1:
            res = epilogue(part, *[e[...] for e in extra])
            for r, o in zip(res, outs):
                o[...] = r.astype(o.dtype)
            return
        acc_ref = refs[2 + n_extra + n_out]
        k = pl.program_id(2)

        @pl.when(k == 0)
        def _():
            acc_ref[...] = part

        @pl.when(k > 0)
        def _():
            acc_ref[...] += part

        @pl.when(k == nk - 1)
        def _():
            res = epilogue(acc_ref[...], *[e[...] for e in extra])
            for r, o in zip(res, outs):
                o[...] = r.astype(o.dtype)
    return body


def _extra_specs(extras, tm, tn):
    specs = []
    for arr, kind in extras:
        if kind == "tile":
            specs.append(pl.BlockSpec((tm, tn), lambda i, j, k: (i, j)))
        elif kind == "dep":
            specs.append(pl.BlockSpec(arr.shape, lambda i, j, k: (0, 0)))
        else:
            specs.append(pl.BlockSpec((1, tn), lambda i, j, k: (0, j)))
    return specs


def mm_nn(name, a, w, axis, epilogue, out_dtypes, extras=()):
    M, K = a.shape
    Kw, N = _full_dims(w.shape, axis)
    assert K == Kw
    tm, tn, tk = min(MM_TM, M), min(MM_TN, N), min(MM_TK, K)
    if axis == "col":
        tn = min(tn, w.shape[2])
    else:
        tk = min(tk, w.shape[1])
    nk = K // tk
    in_specs = [pl.BlockSpec((tm, tk), lambda i, j, k: (i, k)),
                _wspec(w.shape, axis, tk, tn, lambda i, j, k: k, lambda i, j, k: j)]
    in_specs += _extra_specs(extras, tm, tn)
    body = _mm_body(nk, [kind for _, kind in extras], len(out_dtypes), ((1,), (0,)), epilogue)
    return pl.pallas_call(
        body, name=name, grid=(M // tm, N // tn, nk), in_specs=in_specs,
        out_specs=[pl.BlockSpec((tm, tn), lambda i, j, k: (i, j)) for _ in out_dtypes],
        out_shape=[jax.ShapeDtypeStruct((M, N), d) for d in out_dtypes],
        scratch_shapes=[pltpu.VMEM((tm, tn), F32)] if nk > 1 else [],
        compiler_params=_cparams(("parallel", "parallel", "arbitrary")),
    )(a, w, *[e for e, _ in extras])


def mm_nt(name, g, w, axis, epilogue, out_dtypes, extras=()):
    M, N = g.shape
    K, Nw = _full_dims(w.shape, axis)
    assert N == Nw
    tm, tn, tk = min(MM_TM, M), min(MM_TN, K), min(MM_TK, N)
    if axis == "col":
        tk = min(tk, w.shape[2])
    else:
        tn = min(tn, w.shape[1])
    nk = N // tk
    in_specs = [pl.BlockSpec((tm, tk), lambda i, j, k: (i, k)),
                _wspec(w.shape, axis, tn, tk, lambda i, j, k: j, lambda i, j, k: k)]
    in_specs += _extra_specs(extras, tm, tn)
    body = _mm_body(nk, [kind for _, kind in extras], len(out_dtypes), ((1,), (1,)), epilogue)
    return pl.pallas_call(
        body, name=name, grid=(M // tm, K // tn, nk), in_specs=in_specs,
        out_specs=[pl.BlockSpec((tm, tn), lambda i, j, k: (i, j)) for _ in out_dtypes],
        out_shape=[jax.ShapeDtypeStruct((M, K), d) for d in out_dtypes],
        scratch_shapes=[pltpu.VMEM((tm, tn), F32)] if nk > 1 else [],
        compiler_params=_cparams(("parallel", "parallel", "arbitrary")),
    )(g, w, *[e for e, _ in extras])


def mm_tn(name, a, g, wshape, axis, deps=()):
    M, K = a.shape
    Mg, N = g.shape
    assert M == Mg and (K, N) == _full_dims(wshape, axis)
    tm, tn, tk = min(MM_TM, K), min(MM_TN, N), min(MM_TK, M)
    if axis == "col":
        tn = min(tn, wshape[2])
    else:
        tm = min(tm, wshape[1])
    nk = M // tk
    body = _mm_body(nk, ["dep"] * len(deps), 1, ((0,), (0,)), lambda acc: (acc,))
    return pl.pallas_call(
        body, name=name, grid=(K // tm, N // tn, nk),
        in_specs=[pl.BlockSpec((tk, tm), lambda i, j, k: (k, i)),
                  pl.BlockSpec((tk, tn), lambda i, j, k: (k, j))] + _extra_specs([(d, "dep") for d in deps], tm, tn),
        out_specs=[_wspec(wshape, axis, tm, tn, lambda i, j, k: i, lambda i, j, k: j)],
        out_shape=[jax.ShapeDtypeStruct(wshape, F32)],
        scratch_shapes=[pltpu.VMEM((tm, tn), F32)] if nk > 1 else [],
        compiler_params=_cparams(("parallel", "parallel", "arbitrary")),
    )(a, g, *deps)[0]


def _row_spec(tr, width):
    return pl.BlockSpec((tr, width), lambda i: (i, 0))


def _vec_spec(width):
    return pl.BlockSpec((1, width), lambda i: (0, 0))


def _fold8(x):
    r, d = x.shape
    return jnp.sum(x.reshape(r // 8, 8, d), axis=0)


def ln_fwd(name, f, prev, prev_g=None, prev_b=None):
    T, D = f.shape
    tr = min(ROW_TILE, T)
    affine = prev_g is not None

    def body(*refs):
        if affine:
            f_ref, p_ref, pg_ref, pb_ref, g_ref, b_ref, xhat_ref, rstd_ref, xbf_ref = refs
            xprev = p_ref[...] * pg_ref[...] + pb_ref[...]
        else:
            f_ref, p_ref, g_ref, b_ref, xhat_ref, rstd_ref, xbf_ref = refs
            xprev = p_ref[...]
        r = ALPHA * xprev + f_ref[...]
        mu = jnp.mean(r, axis=-1, keepdims=True)
        cen = r - mu
        var = jnp.mean(cen * cen, axis=-1, keepdims=True)
        rstd = lax.rsqrt(var + LN_EPS)
        xhat = cen * rstd
        xhat_ref[...] = xhat
        rstd_ref[...] = rstd
        xbf_ref[...] = (xhat * g_ref[...] + b_ref[...]).astype(BF16)

    def call(g, b):
        ins = [f, prev] + ([prev_g, prev_b] if affine else []) + [g, b]
        specs = [_row_spec(tr, D), _row_spec(tr, D)] + ([_vec_spec(D)] * 2 if affine else []) + [_vec_spec(D)] * 2
        return pl.pallas_call(
            body, name=name, grid=(T // tr,), in_specs=specs,
            out_specs=[_row_spec(tr, D), _row_spec(tr, 1), _row_spec(tr, D)],
            out_shape=[jax.ShapeDtypeStruct((T, D), F32), jax.ShapeDtypeStruct((T, 1), F32),
                       jax.ShapeDtypeStruct((T, D), BF16)],
            compiler_params=_cparams(("parallel",)),
        )(*ins)
    return call


def ln_bwd(name, xhat, rstd, gamma, dy=None, target=None, beta=None):
    T, D = xhat.shape
    tr = min(ROW_TILE, T)
    nt = T // tr
    head = target is not None

    def body(*refs):
        if head:
            xhat_ref, rstd_ref, g_ref, tgt_ref, b_ref = refs[:5]
            outs = refs[5:]
        else:
            xhat_ref, rstd_ref, g_ref, dy_ref = refs[:4]
            outs = refs[4:]
        dr_ref, drbf_ref, dg_ref, db_ref, cs_ref = outs[:5]
        rest = outs[5:]
        if head:
            loss_ref, acc_ref = rest
        else:
            (acc_ref,) = rest
        i = pl.program_id(0)
        xhat_v = xhat_ref[...]
        gam = g_ref[...]
        if head:
            diff = xhat_v * gam + b_ref[...] - tgt_ref[...]
            dyv = diff * (1.0 / D)
        else:
            dyv = dy_ref[...]
        dxh = dyv * gam
        m1 = jnp.mean(dxh, axis=-1, keepdims=True)
        m2 = jnp.mean(dxh * xhat_v, axis=-1, keepdims=True)
        dr = rstd_ref[...] * (dxh - m1 - xhat_v * m2)
        dr_ref[...] = dr
        drbf_ref[...] = dr.astype(BF16)

        @pl.when(i == 0)
        def _():
            acc_ref[...] = jnp.zeros_like(acc_ref)

        acc_ref[0] += _fold8(dyv * xhat_v)
        acc_ref[1] += _fold8(dyv)
        acc_ref[2] += _fold8(dr)
        if head:
            acc_ref[3] += _fold8(diff * diff)

        @pl.when(i == nt - 1)
        def _():
            dg_ref[...] = jnp.sum(acc_ref[0], axis=0, keepdims=True)
            db_ref[...] = jnp.sum(acc_ref[1], axis=0, keepdims=True)
            cs_ref[...] = jnp.sum(acc_ref[2], axis=0, keepdims=True)
            if head:
                loss_ref[...] = jnp.sum(jnp.sum(acc_ref[3], axis=0, keepdims=True), axis=1, keepdims=True)

    ins = [xhat, rstd, gamma] + ([target, beta] if head else [dy])
    specs = [_row_spec(tr, D), _row_spec(tr, 1), _vec_spec(D)] + ([_row_spec(tr, D), _vec_spec(D)] if head else [_row_spec(tr, D)])
    out_specs = [_row_spec(tr, D), _row_spec(tr, D), _vec_spec(D), _vec_spec(D), _vec_spec(D)]
    out_shape = [jax.ShapeDtypeStruct((T, D), F32), jax.ShapeDtypeStruct((T, D), BF16)] + [jax.ShapeDtypeStruct((1, D), F32)] * 3
    if head:
        out_specs.append(pl.BlockSpec((1, 1), lambda i: (0, 0)))
        out_shape.append(jax.ShapeDtypeStruct((1, 1), F32))
    return pl.pallas_call(
        body, name=name, grid=(nt,), in_specs=specs, out_specs=out_specs, out_shape=out_shape,
        scratch_shapes=[pltpu.VMEM((4, 8, D), F32)],
        compiler_params=_cparams(("arbitrary",)),
    )(*ins)


CONV_ROWS, CONV_COLS = 64, 512


def _tap_chunks(tt, D):
    for r0 in range(0, tt, min(CONV_ROWS, tt)):
        for c0 in range(0, D, min(CONV_COLS, D)):
            yield r0, min(CONV_ROWS, tt), c0, min(CONV_COLS, D)


def conv_fwd(name, h1, dw, dwb, lng, lnb):
    T, D2 = h1.shape
    D = D2 // 2
    tt = min(CONV_TILE, T)
    hb = tt // CONV_HALO
    KW = dw.shape[0]
    lead = CONV_HALO - (KW - 1)

    def body(a_ref, g_ref, ah_ref, gh_ref, dw_ref, dwb_ref, lng_ref, lnb_ref, u_ref, c_ref, s_ref, ext_ref):
        i = pl.program_id(0)
        u = a_ref[...] * _sigmoid(g_ref[...])
        u_ref[...] = u
        uh = ah_ref[...] * _sigmoid(gh_ref[...])
        ext_ref[pl.ds(0, CONV_HALO), :] = jnp.where(i > 0, uh, 0.0)
        ext_ref[pl.ds(CONV_HALO, tt), :] = u
        for r0, nr, c0, nc in _tap_chunks(tt, D):
            acc = jnp.zeros((nr, nc), F32) + dwb_ref[:, pl.ds(c0, nc)]
            for k in range(KW):
                acc = acc + dw_ref[pl.ds(k, 1), pl.ds(c0, nc)] * ext_ref[pl.ds(r0 + lead + k, nr), pl.ds(c0, nc)]
            c_ref[pl.ds(r0, nr), pl.ds(c0, nc)] = acc
        c = c_ref[...]
        mu = jnp.mean(c, axis=-1, keepdims=True)
        cen = c - mu
        var = jnp.mean(cen * cen, axis=-1, keepdims=True)
        n = cen * lax.rsqrt(var + LN_EPS) * lng_ref[...] + lnb_ref[...]
        s_ref[...] = (n * _sigmoid(n)).astype(BF16)

    halo = lambda col: pl.BlockSpec((CONV_HALO, D), lambda i: (jnp.maximum(i * hb - 1, 0), col))
    return pl.pallas_call(
        body, name=name, grid=(T // tt,),
        in_specs=[pl.BlockSpec((tt, D), lambda i: (i, 0)), pl.BlockSpec((tt, D), lambda i: (i, 1)), halo(0), halo(1),
                  pl.BlockSpec((KW, D), lambda i: (0, 0)), _vec_spec(D), _vec_spec(D), _vec_spec(D)],
        out_specs=[_row_spec(tt, D)] * 3,
        out_shape=[jax.ShapeDtypeStruct((T, D), F32), jax.ShapeDtypeStruct((T, D), F32), jax.ShapeDtypeStruct((T, D), BF16)],
        scratch_shapes=[pltpu.VMEM((tt + CONV_HALO, D), F32)],
        compiler_params=_cparams(("parallel",)),
    )(h1, h1, h1, h1, dw, dwb, lng, lnb)


def conv_bwd_ln(name, ds, c, lng, lnb):
    T, D = c.shape
    tr = min(ROW_TILE, T)
    nt = T // tr

    def body(ds_ref, c_ref, g_ref, b_ref, dc_ref, dg_ref, db_ref, cs_ref, acc_ref):
        i = pl.program_id(0)
        cv = c_ref[...]
        mu = jnp.mean(cv, axis=-1, keepdims=True)
        cen = cv - mu
        var = jnp.mean(cen * cen, axis=-1, keepdims=True)
        rstd = lax.rsqrt(var + LN_EPS)
        chat = cen * rstd
        n = chat * g_ref[...] + b_ref[...]
        sg = _sigmoid(n)
        dn = ds_ref[...] * (sg * (1.0 + n * (1.0 - sg)))
        dxh = dn * g_ref[...]
        m1 = jnp.mean(dxh, axis=-1, keepdims=True)
        m2 = jnp.mean(dxh * chat, axis=-1, keepdims=True)
        dc = rstd * (dxh - m1 - chat * m2)
        dc_ref[...] = dc

        @pl.when(i == 0)
        def _():
            acc_ref[...] = jnp.zeros_like(acc_ref)

        acc_ref[0] += _fold8(dn * chat)
        acc_ref[1] += _fold8(dn)
        acc_ref[2] += _fold8(dc)

        @pl.when(i == nt - 1)
        def _():
            dg_ref[...] = jnp.sum(acc_ref[0], axis=0, keepdims=True)
            db_ref[...] = jnp.sum(acc_ref[1], axis=0, keepdims=True)
            cs_ref[...] = jnp.sum(acc_ref[2], axis=0, keepdims=True)

    return pl.pallas_call(
        body, name=name, grid=(nt,),
        in_specs=[_row_spec(tr, D), _row_spec(tr, D), _vec_spec(D), _vec_spec(D)],
        out_specs=[_row_spec(tr, D), _vec_spec(D), _vec_spec(D), _vec_spec(D)],
        out_shape=[jax.ShapeDtypeStruct((T, D), F32)] + [jax.ShapeDtypeStruct((1, D), F32)] * 3,
        scratch_shapes=[pltpu.VMEM((3, 8, D), F32)],
        compiler_params=_cparams(("arbitrary",)),
    )(ds, c, lng, lnb)


def conv_bwd_taps(name, dc, u, h1, dw):
    T, D = dc.shape
    tt = min(CONV_TILE, T)
    nt = T // tt
    hb = tt // CONV_HALO
    nhb = T // CONV_HALO
    KW = dw.shape[0]
    lead = CONV_HALO - (KW - 1)

    def body(dc_ref, dcn_ref, u_ref, uh_ref, a_ref, g_ref, dw_ref, dh1_ref, db1_ref, ddw_ref,
             edc_ref, eu_ref, du_ref, accw_ref, accb_ref):
        i = pl.program_id(0)

        @pl.when(i == 0)
        def _():
            accw_ref[...] = jnp.zeros_like(accw_ref)
            accb_ref[...] = jnp.zeros_like(accb_ref)

        edc_ref[pl.ds(0, tt), :] = dc_ref[...]
        edc_ref[pl.ds(tt, CONV_HALO), :] = jnp.where(i < nt - 1, dcn_ref[...], 0.0)
        eu_ref[pl.ds(0, CONV_HALO), :] = jnp.where(i > 0, uh_ref[...], 0.0)
        eu_ref[pl.ds(CONV_HALO, tt), :] = u_ref[...]
        for r0, nr, c0, nc in _tap_chunks(tt, D):
            dcv = dc_ref[pl.ds(r0, nr), pl.ds(c0, nc)]
            acc = jnp.zeros((nr, nc), F32)
            for k in range(KW):
                acc = acc + dw_ref[pl.ds(k, 1), pl.ds(c0, nc)] * edc_ref[pl.ds(r0 + (KW - 1) - k, nr), pl.ds(c0, nc)]
                accw_ref[k, :, pl.ds(c0, nc)] += _fold8(dcv * eu_ref[pl.ds(r0 + lead + k, nr), pl.ds(c0, nc)])
            du_ref[pl.ds(r0, nr), pl.ds(c0, nc)] = acc
        du = du_ref[...]
        sg = _sigmoid(g_ref[...])
        da = du * sg
        dg = du * a_ref[...] * sg * (1.0 - sg)
        dh1_ref[:, pl.ds(0, D)] = da.astype(BF16)
        dh1_ref[:, pl.ds(D, D)] = dg.astype(BF16)
        accb_ref[:, pl.ds(0, D)] += _fold8(da)
        accb_ref[:, pl.ds(D, D)] += _fold8(dg)

        @pl.when(i == nt - 1)
        def _():
            db1_ref[...] = jnp.sum(accb_ref[...], axis=0, keepdims=True)
            ddw_ref[...] = jnp.sum(accw_ref[...], axis=1)

    return pl.pallas_call(
        body, name=name, grid=(nt,),
        in_specs=[_row_spec(tt, D),
                  pl.BlockSpec((CONV_HALO, D), lambda i: (jnp.minimum((i + 1) * hb, nhb - 1), 0)),
                  _row_spec(tt, D),
                  pl.BlockSpec((CONV_HALO, D), lambda i: (jnp.maximum(i * hb - 1, 0), 0)),
                  pl.BlockSpec((tt, D), lambda i: (i, 0)), pl.BlockSpec((tt, D), lambda i: (i, 1)),
                  pl.BlockSpec((KW, D), lambda i: (0, 0))],
        out_specs=[_row_spec(tt, 2 * D), _vec_spec(2 * D), pl.BlockSpec((KW, D), lambda i: (0, 0))],
        out_shape=[jax.ShapeDtypeStruct((T, 2 * D), BF16), jax.ShapeDtypeStruct((1, 2 * D), F32),
                   jax.ShapeDtypeStruct((KW, D), F32)],
        scratch_shapes=[pltpu.VMEM((tt + CONV_HALO, D), F32), pltpu.VMEM((tt + CONV_HALO, D), F32),
                        pltpu.VMEM((tt, D), F32), pltpu.VMEM((KW, 8, D), F32), pltpu.VMEM((8, 2 * D), F32)],
        compiler_params=_cparams(("arbitrary",)),
    )(dc, dc, u, u, h1, h1, dw)


def _t5_bucket(dist):
    max_exact = REL_BUCKETS // 2
    large = max_exact + (np.log(np.maximum(dist, 1) / max_exact) / math.log(REL_MAX_DIST / max_exact)
                         * (REL_BUCKETS - max_exact)).astype(np.int32)
    large = np.minimum(large, REL_BUCKETS - 1)
    return np.where(dist < max_exact, dist, large).astype(np.int32)


def _bucket_table(dil):
    i = np.arange(BAND)[:, None]
    j = np.arange(2 * BAND)[None, :]
    delta = i - j + BAND
    return _t5_bucket(np.clip(delta, 0, None) * dil)


def bias_expand(name, rel_bias, dil):
    n_heads = rel_bias.shape[1]
    idx = jnp.asarray(_bucket_table(dil))

    def body(rel_ref, idx_ref, out_ref):
        h = pl.program_id(0)
        idxv = idx_ref[...]
        b = jnp.zeros((BAND, 2 * BAND), F32)
        for bk in range(REL_BUCKETS):
            b = jnp.where(idxv == bk, rel_ref[bk, h], b)
        out_ref[...] = b

    return pl.pallas_call(
        body, name=name, grid=(n_heads,),
        in_specs=[pl.BlockSpec(memory_space=pltpu.SMEM), pl.BlockSpec((BAND, 2 * BAND), lambda h: (0, 0))],
        out_specs=pl.BlockSpec((None, BAND, 2 * BAND), lambda h: (h, 0, 0)),
        out_shape=jax.ShapeDtypeStruct((n_heads, BAND, 2 * BAND), F32),
        compiler_params=_cparams(("arbitrary",)),
    )(rel_bias, idx)


def relbias_grad(name, dsb_list):
    n_heads = dsb_list[0].shape[0]
    idxs = [jnp.asarray(_bucket_table(d)) for _, d in BRANCHES]
    nb = len(BRANCHES)

    def body(*refs):
        ds_refs, idx_refs, out_ref = refs[:nb], refs[nb:2 * nb], refs[2 * nb]
        lane = lax.broadcasted_iota(jnp.int32, (1, 128), 1)
        row = jnp.zeros((1, 128), F32)
        for bk in range(REL_BUCKETS):
            tot = jnp.zeros((1, 1), F32)
            for ds_ref, idx_ref in zip(ds_refs, idx_refs):
                sel = jnp.where(idx_ref[...] == bk, ds_ref[...], 0.0)
                tot = tot + jnp.sum(jnp.sum(sel, axis=0, keepdims=True), axis=1, keepdims=True)
            row = jnp.where(lane == bk, tot, row)
        out_ref[...] = row

    return pl.pallas_call(
        body, name=name, grid=(n_heads,),
        in_specs=[pl.BlockSpec((None, BAND, 2 * BAND), lambda h: (h, 0, 0))] * nb
                 + [pl.BlockSpec((BAND, 2 * BAND), lambda h: (0, 0))] * nb,
        out_specs=pl.BlockSpec((None, 1, 128), lambda h: (h, 0, 0)),
        out_shape=jax.ShapeDtypeStruct((n_heads, 1, 128), F32),
        compiler_params=_cparams(("arbitrary",)),
    )(*dsb_list, *idxs)


def _band_mask():
    i = lax.broadcasted_iota(jnp.int32, (BAND, 2 * BAND), 0)
    j = lax.broadcasted_iota(jnp.int32, (BAND, 2 * BAND), 1)
    return (j >= i) & (j <= i + BAND), j


def _rep2(x):
    return jnp.concatenate([x, x], axis=1)


def _attn_views(T, dil):
    L = T // dil
    lc = min(ATTN_CHUNK, L)
    return L, lc, L // lc, lc // BAND


def attn_fwd(name, q, kv, bias, dil):
    T, D = q.shape
    n_heads = D // HEAD_DIM
    L, lc, nchunk, nsub = _attn_views(T, dil)
    scale = HEAD_DIM ** -0.5
    q3 = q.reshape(L, dil * D)
    kv3 = kv.reshape(L, dil * 2 * D)

    def body(q_ref, k_ref, v_ref, kp_ref, vp_ref, b_ref, o_ref, lse_ref, kext_ref, vext_ref):
        c = pl.program_id(2)
        kext_ref[pl.ds(0, BAND), :] = kp_ref[...]
        vext_ref[pl.ds(0, BAND), :] = vp_ref[...]
        kext_ref[pl.ds(BAND, lc), :] = k_ref[...]
        vext_ref[pl.ds(BAND, lc), :] = v_ref[...]
        band, jcol = _band_mask()
        bias_v = b_ref[...]

        def sub(a, carry):
            off = pl.multiple_of(a * BAND, BAND)
            qa = q_ref[pl.ds(off, BAND), :]
            kw = kext_ref[pl.ds(off, 2 * BAND), :]
            vw = vext_ref[pl.ds(off, 2 * BAND), :]
            s = lax.dot_general(qa, kw, (((1,), (1,)), ((), ())), preferred_element_type=F32) * scale + bias_v
            first = jnp.logical_and(c == 0, a == 0)
            valid = band & jnp.logical_or(jcol >= BAND, jnp.logical_not(first))
            s = jnp.where(valid, s, NEG_BIG)
            m = jnp.max(s, axis=-1, keepdims=True)
            p = jnp.exp(s - m)
            den = jnp.sum(p, axis=-1, keepdims=True)
            pv = lax.dot_general(p.astype(BF16), vw, (((1,), (0,)), ((), ())), preferred_element_type=F32)
            o_ref[pl.ds(off, BAND), :] = pv / den
            lse_ref[pl.ds(off, BAND), :] = jnp.broadcast_to(m + jnp.log(den), (BAND, HEAD_DIM))
            return carry

        lax.fori_loop(0, nsub, sub, 0, unroll=True)

    nsb = lc // BAND
    blk = lambda w, col0: pl.BlockSpec((lc, HEAD_DIM), lambda h, r, c: (c, r * w + col0 + h))
    halo = lambda w, col0: pl.BlockSpec((BAND, HEAD_DIM), lambda h, r, c: (jnp.maximum(c * nsb - 1, 0), r * w + col0 + h))
    nh = n_heads
    o3, lse3 = pl.pallas_call(
        body, name=name, grid=(n_heads, dil, nchunk),
        in_specs=[blk(nh, 0), blk(2 * nh, 0), blk(2 * nh, nh), halo(2 * nh, 0), halo(2 * nh, nh),
                  pl.BlockSpec((None, BAND, 2 * BAND), lambda h, r, c: (h, 0, 0))],
        out_specs=[blk(nh, 0), blk(nh, 0)],
        out_shape=[jax.ShapeDtypeStruct((L, dil * D), F32)] * 2,
        scratch_shapes=[pltpu.VMEM((lc + BAND, HEAD_DIM), BF16)] * 2,
        compiler_params=_cparams(("arbitrary", "arbitrary", "arbitrary")),
    )(q3, kv3, kv3, kv3, kv3, bias)
    return o3.reshape(T, D), lse3.reshape(T, D)


def attn_merge(name, outs, lses):
    T, D = outs[0].shape
    tr = min(ROW_TILE, T)
    nb = len(outs)

    def body(*refs):
        o_refs, l_refs = refs[:nb], refs[nb:2 * nb]
        o_ref, obf_ref, lse_ref = refs[2 * nb:]
        ls = [r[...] for r in l_refs]
        m = functools.reduce(jnp.maximum, ls)
        es = [jnp.exp(l - m) for l in ls]
        tot = functools.reduce(lambda x, y: x + y, es)
        o = functools.reduce(lambda x, y: x + y, [(e / tot) * r[...] for e, r in zip(es, o_refs)])
        o_ref[...] = o
        obf_ref[...] = o.astype(BF16)
        lse_ref[...] = m + jnp.log(tot)

    return pl.pallas_call(
        body, name=name, grid=(T // tr,), in_specs=[_row_spec(tr, D)] * (2 * nb),
        out_specs=[_row_spec(tr, D)] * 3,
        out_shape=[jax.ShapeDtypeStruct((T, D), F32), jax.ShapeDtypeStruct((T, D), BF16), jax.ShapeDtypeStruct((T, D), F32)],
        compiler_params=_cparams(("parallel",)),
    )(*outs, *lses)


def attn_bwd_prep(name, do, o):
    T, D = o.shape
    n_heads = D // HEAD_DIM
    tr = min(ROW_TILE, T)

    def body(do_ref, o_ref, dobf_ref, dsum_ref):
        dobf_ref[...] = do_ref[...].astype(BF16)
        for h in range(n_heads):
            cols = pl.ds(h * HEAD_DIM, HEAD_DIM)
            d = jnp.sum(do_ref[:, cols] * o_ref[:, cols], axis=-1, keepdims=True)
            dsum_ref[:, cols] = jnp.broadcast_to(d, (tr, HEAD_DIM))

    return pl.pallas_call(
        body, name=name, grid=(T // tr,), in_specs=[_row_spec(tr, D)] * 2, out_specs=[_row_spec(tr, D)] * 2,
        out_shape=[jax.ShapeDtypeStruct((T, D), BF16), jax.ShapeDtypeStruct((T, D), F32)],
        compiler_params=_cparams(("parallel",)),
    )(do, o)


def attn_bwd(name, q, kv, do, lse, dsum, bias, dil):
    T, D = q.shape
    n_heads = D // HEAD_DIM
    L, lc, nchunk, nsub = _attn_views(T, dil)
    scale = HEAD_DIM ** -0.5
    q3, do3, lse3, dsum3 = (t.reshape(L, dil * D) for t in (q, do, lse, dsum))
    kv3 = kv.reshape(L, dil * 2 * D)
    nt_dims = (((1,), (1,)), ((), ()))
    tn_dims = (((0,), (0,)), ((), ()))
    nn_dims = (((1,), (0,)), ((), ()))

    def body(q_ref, k_ref, v_ref, do_ref, lse_ref, ds_ref, kp_ref, vp_ref, qn_ref, don_ref, lsen_ref, dsn_ref, b_ref,
             dq_ref, dk_ref, dv_ref, dsb_ref, kext_ref, vext_ref, dkext_ref, dvext_ref):
        r = pl.program_id(1)
        c = pl.program_id(2)
        kext_ref[pl.ds(0, BAND), :] = kp_ref[...]
        vext_ref[pl.ds(0, BAND), :] = vp_ref[...]
        kext_ref[pl.ds(BAND, lc), :] = k_ref[...]
        vext_ref[pl.ds(BAND, lc), :] = v_ref[...]
        dkext_ref[...] = jnp.zeros_like(dkext_ref)
        dvext_ref[...] = jnp.zeros_like(dvext_ref)
        band, jcol = _band_mask()
        bias_v = b_ref[...]

        @pl.when(jnp.logical_and(r == 0, c == 0))
        def _():
            dsb_ref[...] = jnp.zeros_like(dsb_ref)

        def sub(a, carry):
            off = pl.multiple_of(a * BAND, BAND)
            qa = q_ref[pl.ds(off, BAND), :]
            doa = do_ref[pl.ds(off, BAND), :]
            kw = kext_ref[pl.ds(off, 2 * BAND), :]
            vw = vext_ref[pl.ds(off, 2 * BAND), :]
            s = lax.dot_general(qa, kw, nt_dims, preferred_element_type=F32) * scale + bias_v
            first = jnp.logical_and(c == 0, a == 0)
            valid = band & jnp.logical_or(jcol >= BAND, jnp.logical_not(first))
            p = jnp.where(valid, jnp.exp(s - _rep2(lse_ref[pl.ds(off, BAND), :])), 0.0)
            dp = lax.dot_general(doa, vw, nt_dims, preferred_element_type=F32)
            ds = p * (dp - _rep2(ds_ref[pl.ds(off, BAND), :]))
            dsb_ref[...] += ds
            dsb16 = ds.astype(BF16)
            dq_ref[pl.ds(off, BAND), :] = lax.dot_general(dsb16, kw, nn_dims, preferred_element_type=F32) * scale
            dkext_ref[pl.ds(off, 2 * BAND), :] += lax.dot_general(dsb16, qa, tn_dims, preferred_element_type=F32) * scale
            dvext_ref[pl.ds(off, 2 * BAND), :] += lax.dot_general(p.astype(BF16), doa, tn_dims, preferred_element_type=F32)
            return carry

        lax.fori_loop(0, nsub, sub, 0, unroll=True)

        @pl.when(c < nchunk - 1)
        def _():
            qn = qn_ref[...]
            don = don_ref[...]
            kl = kext_ref[pl.ds(lc, BAND), :]
            vl = vext_ref[pl.ds(lc, BAND), :]
            s = lax.dot_general(qn, kl, nt_dims, preferred_element_type=F32) * scale + bias_v[:, :BAND]
            p = jnp.where(band[:, :BAND], jnp.exp(s - lsen_ref[...]), 0.0)
            dp = lax.dot_general(don, vl, nt_dims, preferred_element_type=F32)
            ds = p * (dp - dsn_ref[...])
            dkext_ref[pl.ds(lc, BAND), :] += lax.dot_general(ds.astype(BF16), qn, tn_dims, preferred_element_type=F32) * scale
            dvext_ref[pl.ds(lc, BAND), :] += lax.dot_general(p.astype(BF16), don, tn_dims, preferred_element_type=F32)

        dk_ref[...] = dkext_ref[pl.ds(BAND, lc), :]
        dv_ref[...] = dvext_ref[pl.ds(BAND, lc), :]

    nsb = lc // BAND
    nblk = L // BAND
    nh = n_heads
    blk = lambda w, col0: pl.BlockSpec((lc, HEAD_DIM), lambda h, r, c: (c, r * w + col0 + h))
    prev = lambda w, col0: pl.BlockSpec((BAND, HEAD_DIM), lambda h, r, c: (jnp.maximum(c * nsb - 1, 0), r * w + col0 + h))
    nxt = lambda: pl.BlockSpec((BAND, HEAD_DIM), lambda h, r, c: (jnp.minimum((c + 1) * nsb, nblk - 1), r * nh + h))
    one = blk(nh, 0)
    dq3, dk3, dv3, dsb = pl.pallas_call(
        body, name=name, grid=(n_heads, dil, nchunk),
        in_specs=[one, blk(2 * nh, 0), blk(2 * nh, nh), one, one, one, prev(2 * nh, 0), prev(2 * nh, nh),
                  nxt(), nxt(), nxt(), nxt(), pl.BlockSpec((None, BAND, 2 * BAND), lambda h, r, c: (h, 0, 0))],
        out_specs=[one, one, one, pl.BlockSpec((None, BAND, 2 * BAND), lambda h, r, c: (h, 0, 0))],
        out_shape=[jax.ShapeDtypeStruct((L, dil * D), F32)] * 3 + [jax.ShapeDtypeStruct((n_heads, BAND, 2 * BAND), F32)],
        scratch_shapes=[pltpu.VMEM((lc + BAND, HEAD_DIM), BF16)] * 2 + [pltpu.VMEM((lc + BAND, HEAD_DIM), F32)] * 2,
        compiler_params=_cparams(("arbitrary", "arbitrary", "arbitrary")),
    )(q3, kv3, kv3, do3, lse3, dsum3, kv3, kv3, q3, do3, lse3, dsum3, bias)
    return dq3.reshape(T, D), dk3.reshape(T, D), dv3.reshape(T, D), dsb


def sum_branches(name, dqs, dks, dvs):
    T, D = dqs[0].shape
    tr = min(ROW_TILE, T)
    nb = len(dqs)

    def body(*refs):
        dq_refs, dk_refs, dv_refs = refs[:nb], refs[nb:2 * nb], refs[2 * nb:3 * nb]
        dq_ref, dkv_ref = refs[3 * nb:]
        add = lambda rs: functools.reduce(lambda x, y: x + y, [r[...] for r in rs])
        dq_ref[...] = add(dq_refs).astype(BF16)
        dkv_ref[:, pl.ds(0, D)] = add(dk_refs).astype(BF16)
        dkv_ref[:, pl.ds(D, D)] = add(dv_refs).astype(BF16)

    return pl.pallas_call(
        body, name=name, grid=(T // tr,), in_specs=[_row_spec(tr, D)] * (3 * nb),
        out_specs=[_row_spec(tr, D), _row_spec(tr, 2 * D)],
        out_shape=[jax.ShapeDtypeStruct((T, D), BF16), jax.ShapeDtypeStruct((T, 2 * D), BF16)],
        compiler_params=_cparams(("parallel",)),
    )(*dqs, *dks, *dvs)


def _divisor_tile(n, cap, mult):
    if n <= cap:
        return n
    t = cap - cap % mult
    while n % t:
        t -= mult
    return t


def _tile2(R, C):
    return _divisor_tile(R, 512, 8), _divisor_tile(C, 1024, 128)


def half_cast(name, dw, core):
    S, R, C = dw.shape
    hr = R // 2
    tr, tc = _tile2(hr, C)
    nrb = hr // tr

    def body(c_ref, x_ref, o_ref):
        o_ref[...] = x_ref[...].astype(BF16)

    return pl.pallas_call(
        body, name=name,
        grid_spec=pltpu.PrefetchScalarGridSpec(
            num_scalar_prefetch=1, grid=(S, nrb, C // tc),
            in_specs=[pl.BlockSpec((None, tr, tc), lambda s, i, j, c: (s, (1 - c[0]) * nrb + i, j))],
            out_specs=pl.BlockSpec((None, tr, tc), lambda s, i, j, c: (s, i, j))),
        out_shape=jax.ShapeDtypeStruct((S, hr, C), BF16),
        compiler_params=_cparams(("parallel", "parallel", "parallel")),
    )(core, dw)


def pair_sum(name, dw, recv, core):
    S, R, C = dw.shape
    hr = R // 2
    tr, tc = _tile2(hr, C)
    nrb = hr // tr

    def body(c_ref, x_ref, r_ref, p_ref, pbf_ref):
        p = x_ref[...] + r_ref[...].astype(F32)
        p_ref[...] = p
        pbf_ref[...] = p.astype(BF16)

    out = pl.BlockSpec((None, tr, tc), lambda s, i, j, c: (s, i, j))
    return pl.pallas_call(
        body, name=name,
        grid_spec=pltpu.PrefetchScalarGridSpec(
            num_scalar_prefetch=1, grid=(S, nrb, C // tc),
            in_specs=[pl.BlockSpec((None, tr, tc), lambda s, i, j, c: (s, c[0] * nrb + i, j)), out],
            out_specs=[out, out]),
        out_shape=[jax.ShapeDtypeStruct((S, hr, C), F32), jax.ShapeDtypeStruct((S, hr, C), BF16)],
        compiler_params=_cparams(("parallel", "parallel", "parallel")),
    )(core, dw, recv)


def chip_sum(name, p, recv, chip, core):
    S, hr, C = p.shape
    tr, tc = _tile2(hr, C)
    nrb = hr // tr

    def body(chip_ref, core_ref, p_ref, r_ref, o_ref):
        acc = p_ref[...]
        for t in range(N_CHIPS - 1):
            acc = acc + r_ref[t].astype(F32)
        o_ref[...] = acc

    return pl.pallas_call(
        body, name=name,
        grid_spec=pltpu.PrefetchScalarGridSpec(
            num_scalar_prefetch=2, grid=(nrb, C // tc),
            in_specs=[pl.BlockSpec((None, tr, tc), lambda i, j, s, c: (s[0], i, j)),
                      pl.BlockSpec((N_CHIPS - 1, tr, tc), lambda i, j, s, c: (0, i, j))],
            out_specs=pl.BlockSpec((tr, tc), lambda i, j, s, c: (c[0] * nrb + i, j))),
        out_shape=jax.ShapeDtypeStruct((2 * hr, C), F32),
        compiler_params=_cparams(("parallel", "parallel")),
    )(chip, core, p, recv)


def adamw(name, w, g, m, v):
    R, C = w.shape
    tr, tc = _tile2(R, C)
    c1 = 1.0 - ADAM_B1 ** ADAM_STEP
    c2 = 1.0 - ADAM_B2 ** ADAM_STEP

    def body(w_ref, g_ref, m_ref, v_ref, d_ref, nm_ref, nv_ref):
        gv = g_ref[...]
        nm = ADAM_B1 * m_ref[...] + (1.0 - ADAM_B1) * gv
        nv = ADAM_B2 * v_ref[...] + (1.0 - ADAM_B2) * (gv * gv)
        nm_ref[...] = nm
        nv_ref[...] = nv
        d_ref[...] = -ADAM_LR * ((nm / c1) / (jnp.sqrt(nv / c2) + ADAM_EPS) + ADAM_WD * w_ref[...])

    spec = pl.BlockSpec((tr, tc), lambda i, j: (i, j))
    return pl.pallas_call(
        body, name=name, grid=(R // tr, C // tc), in_specs=[spec] * 4, out_specs=[spec] * 3,
        out_shape=[jax.ShapeDtypeStruct((R, C), F32)] * 3,
        compiler_params=_cparams(("parallel", "parallel")),
    )(w, g, m, v)


def sum_devices(name, gathered):
    n, R, C = gathered.shape

    def body(x_ref, o_ref):
        acc = x_ref[0]
        for d in range(1, n):
            acc = acc + x_ref[d]
        o_ref[...] = acc

    return pl.pallas_call(
        body, name=name, in_specs=[pl.BlockSpec(memory_space=pltpu.VMEM)],
        out_specs=pl.BlockSpec(memory_space=pltpu.VMEM),
        out_shape=jax.ShapeDtypeStruct((R, C), F32),
    )(gathered)


def _place():
    x, y, c = lax.axis_index("x"), lax.axis_index("y"), lax.axis_index("c")
    return x, y, c


def _other_chips(x, y):
    return [(1 - x, y), (x, 1 - y), (1 - x, 1 - y)]


def all_gather8(name, block):
    R, C = block.shape

    def body(x_ref, out_ref, send_sems, recv_sems, local_sem):
        x, y, c = _place()
        me, sibling = (x, y, c), (x, y, 1 - c)
        chips = _other_chips(x, y)

        def rows(px, py, pc):
            return out_ref.at[4 * px + 2 * py + pc]

        def copy(k, blk, to, src=None):
            return pltpu.make_async_remote_copy(
                src_ref=rows(*blk) if src is None else src, dst_ref=rows(*blk),
                send_sem=send_sems.at[k], recv_sem=recv_sems.at[k], device_id=to, device_id_type=MESH)

        mine = pltpu.make_async_copy(x_ref, rows(*me), local_sem)
        mine.start()
        first = [copy(0, me, sibling, src=x_ref)]
        first += [copy(1 + j, me, (*chip, c), src=x_ref) for j, chip in enumerate(chips)]
        for cp in first:
            cp.start()
        passed = [copy(4 + j, (*chip, c), sibling) for j, chip in enumerate(chips)]
        for j, chip in enumerate(chips):
            copy(1 + j, (*chip, c), me).wait_recv()
            passed[j].start()
        copy(0, sibling, me).wait_recv()
        for j, chip in enumerate(chips):
            copy(4 + j, (*chip, 1 - c), me).wait_recv()
        for cp in first + passed:
            cp.wait_send()
        mine.wait()

    return pl.pallas_call(
        body, name=name, out_shape=jax.ShapeDtypeStruct((N_DEV, R, C), block.dtype),
        in_specs=[pl.BlockSpec(memory_space=pltpu.VMEM)], out_specs=pl.BlockSpec(memory_space=pltpu.VMEM),
        scratch_shapes=[pltpu.SemaphoreType.DMA((7,)), pltpu.SemaphoreType.DMA((7,)), pltpu.SemaphoreType.DMA],
    )(block)


_HBM = pl.BlockSpec(memory_space=pltpu.HBM)
_SEM = pl.BlockSpec(memory_space=pltpu.SEMAPHORE)
_DATAFLOW = pltpu.SideEffectType.DATAFLOW_SIDE_EFFECTING


def _in_hbm(a):
    return pltpu.with_memory_space_constraint(a, pltpu.HBM)


def split_start(name, srcs, lands, n_sem, plan):
    ns, nl = len(srcs), len(lands)

    def body(*refs):
        src, land = refs[:ns], refs[ns:ns + nl]
        send_sems, recv_sems = refs[ns + nl], refs[ns + nl + 1]
        token = refs[-1]
        outgoing, _ = plan(src, land, send_sems, recv_sems)
        for cp in outgoing:
            cp.start()
        token[...] = jnp.zeros_like(token)

    bufs = list(srcs) + list(lands)
    res = pl.pallas_call(
        body, name=name,
        out_shape=(pltpu.SemaphoreType.DMA((n_sem,)), pltpu.SemaphoreType.DMA((n_sem,)),
                   *[pltpu.HBM(b.shape, b.dtype) for b in bufs], jax.ShapeDtypeStruct((8, 128), F32)),
        in_specs=[_HBM] * (ns + nl),
        out_specs=(_SEM, _SEM, *[_HBM] * (ns + nl), pl.BlockSpec(memory_space=pltpu.VMEM)),
        input_output_aliases={i: 2 + i for i in range(ns + nl)},
        compiler_params=pltpu.CompilerParams(has_side_effects=_DATAFLOW),
    )(*[_in_hbm(b) for b in bufs])
    return res[0], res[1], list(res[2:2 + ns]), list(res[2 + ns:2 + ns + nl]), res[-1]


def split_wait(name, started, after, plan):
    send_sems, recv_sems, srcs, lands, _ = started
    ns, nl = len(srcs), len(lands)

    def body(*refs):
        src, land = refs[:ns], refs[ns:ns + nl]
        send, recv = refs[ns + nl], refs[ns + nl + 1]
        outgoing, incoming = plan(src, land, send, recv)
        for cp in outgoing:
            cp.wait_send()
        for cp in incoming:
            cp.wait_recv()

    bufs = list(srcs) + list(lands)
    res = pl.pallas_call(
        body, name=name,
        out_shape=tuple(pltpu.HBM(b.shape, b.dtype) for b in bufs),
        in_specs=[_HBM] * (ns + nl) + [_SEM, _SEM, pl.BlockSpec(memory_space=pl.ANY)],
        out_specs=tuple([_HBM] * (ns + nl)),
        input_output_aliases={i: i for i in range(ns + nl)},
        compiler_params=pltpu.CompilerParams(has_side_effects=_DATAFLOW),
    )(*bufs, send_sems, recv_sems, after)
    return list(res[ns:])


def _rcopy(src, dst, send_sems, ks, recv_sems, kr, device):
    return pltpu.make_async_remote_copy(src_ref=src, dst_ref=dst, send_sem=send_sems.at[ks], recv_sem=recv_sems.at[kr],
                                        device_id=device, device_id_type=MESH)


def _half_rows(ref, h):
    hr = ref.shape[0] // 2
    return ref.at[pl.ds(h * hr, hr)]


def _gather_plan(src, land, send_sems, recv_sems):
    x, y, c = _place()
    me_chip = 2 * x + y
    chips = _other_chips(x, y)
    outgoing, incoming = [], []
    for w, buf in enumerate(land):
        mine = _half_rows(buf.at[me_chip], c)
        for t, chip in enumerate(chips):
            slot = 2 * chip[0] + chip[1]
            for cc in range(2):
                outgoing.append(_rcopy(mine, mine, send_sems, 6 * w + 2 * t + cc, recv_sems, 6 * w + 2 * t + c, (*chip, cc)))
                theirs = _half_rows(buf.at[slot], cc)
                incoming.append(_rcopy(theirs, theirs, send_sems, 6 * w + 2 * t + cc, recv_sems, 6 * w + 2 * t + cc, (*chip, cc)))
    return outgoing, incoming


def _swap_plan(src, land, send_sems, recv_sems):
    x, y, c = _place()
    cp = _rcopy(src[0], land[0], send_sems, 0, recv_sems, 0, (x, y, 1 - c))
    return [cp], [cp]


def _scatter_plan(src, land, send_sems, recv_sems):
    x, y, c = _place()
    cps = [_rcopy(src[0].at[2 * chip[0] + chip[1]], land[0].at[t], send_sems, t, recv_sems, t, (*chip, c))
           for t, chip in enumerate(_other_chips(x, y))]
    return cps, cps


def _share_plan(src, land, send_sems, recv_sems):
    x, y, c = _place()
    mine, theirs = _half_rows(land[0], c), _half_rows(land[0], 1 - c)
    return ([_rcopy(mine, mine, send_sems, 0, recv_sems, 0, (x, y, 1 - c))],
            [_rcopy(theirs, theirs, send_sems, 0, recv_sems, 0, (x, y, 1 - c))])


def place_shard(name, shard, chip):
    R, C = shard.shape
    tr, tc = _tile2(R, C)

    def body(chip_ref, x_ref, o_ref):
        o_ref[...] = x_ref[...].astype(BF16)

    return pl.pallas_call(
        body, name=name,
        grid_spec=pltpu.PrefetchScalarGridSpec(
            num_scalar_prefetch=1, grid=(R // tr, C // tc),
            in_specs=[pl.BlockSpec((tr, tc), lambda i, j, s: (i, j))],
            out_specs=pl.BlockSpec((None, tr, tc), lambda i, j, s: (s[0], i, j))),
        out_shape=jax.ShapeDtypeStruct((N_CHIPS, R, C), BF16),
        compiler_params=_cparams(("parallel", "parallel")),
    )(chip, shard)


class GradExchange:
    SCATTER_TICKS = 2

    def __init__(self, chip1, core, shard, mom, vel):
        self.chip1, self.core, self.shard, self.mom, self.vel = chip1, core, shard, mom, vel
        self.inflight, self.tokens, self.results = [], [], {}

    def take_deps(self):
        deps, self.tokens = self.tokens, []
        return deps

    def _start(self, name, srcs, lands, n_sem, plan):
        started = split_start(name, srcs, lands, n_sem, plan)
        self.tokens.append(started[-1])
        return started

    def add(self, n, dw):
        S, R, C = dw.shape
        to_sibling = half_cast("rs_cast_" + n, dw, self.core)
        started = self._start("rs_swap_start_" + n, [to_sibling], [lax.empty((S, R // 2, C), BF16)], 1, _swap_plan)
        self.inflight.append(dict(n=n, dw=dw, stage=0, started=started, ticks=0))

    def tick(self, after):
        for it in self.inflight:
            n = it["n"]
            if it["stage"] == 0:
                (recv,) = split_wait("rs_swap_wait_" + n, it["started"], after, _swap_plan)
                p, pbf = pair_sum("rs_pair_sum_" + n, it["dw"], recv, self.core)
                S, hr, C = pbf.shape
                it.update(stage=1, p=p, ticks=0,
                          started=self._start("rs_scatter_start_" + n, [pbf], [lax.empty((N_CHIPS - 1, hr, C), BF16)], 3, _scatter_plan))
            elif it["stage"] == 1:
                it["ticks"] += 1
                if it["ticks"] >= self.SCATTER_TICKS:
                    (recv,) = split_wait("rs_scatter_wait_" + n, it["started"], after, _scatter_plan)
                    half = chip_sum("rs_chip_sum_" + n, it["p"], recv, self.chip1, self.core)
                    it.update(stage=2, started=self._start("rs_share_start_" + n, [], [half], 1, _share_plan))
            elif it["stage"] == 2:
                (grad,) = split_wait("rs_share_wait_" + n, it["started"], after, _share_plan)
                self.results[n] = (grad,) + tuple(adamw("adamw_" + n, self.shard[n], grad, self.mom[n], self.vel[n]))
                it["stage"] = 3
        self.inflight = [it for it in self.inflight if it["stage"] < 3]

    def flush(self, after):
        while self.inflight:
            self.tick(after)


def _pack(arrs):
    parts = []
    for a in arrs:
        flat = a.reshape(-1).astype(F32)
        n = flat.shape[0]
        padded = -(-n // 1024) * 1024
        parts.append(jnp.pad(flat, (0, padded - n)).reshape(padded // 128, 128))
    return jnp.concatenate(parts, axis=0)


def _unpack(buf, shapes):
    out, row = [], 0
    for shp in shapes:
        n = int(np.prod(shp))
        rows = -(-n // 1024) * 8
        out.append(buf[row:row + rows].reshape(-1)[:n].reshape(shp))
        row += rows
    return out


def _bias_epi(acc, b):
    return (acc + b,)


def local_step(x, target, W, P, ex, first_deps=()):
    T, D = x.shape
    g = {}
    plain = lambda acc: (acc,)

    (h1,) = mm_nn("pw1_fwd", x, W("pw1", x), "col", _bias_epi, [F32],
                  extras=[(P["pw1_b"], "row")] + [(d, "dep") for d in first_deps])
    u, cpre, s = conv_fwd("conv_fwd", h1, P["dw_w"], P["dw_b"], P["cln_g"], P["cln_b"])
    (mix0,) = mm_nn("pw2_fwd", s, W("pw2", s), "row", _bias_epi, [F32], extras=[(P["pw2_b"], "row")])
    ln = [None] * 4
    gam = [P["ln_mix_g"][0:1], P["ln_mlp_g"][0:1], P["ln_mix_g"][1:2], P["ln_mlp_g"][1:2]]
    bet = [P["ln_mix_b"][0:1], P["ln_mlp_b"][0:1], P["ln_mix_b"][1:2], P["ln_mlp_b"][1:2]]
    ln[0] = ln_fwd("ln0_fwd", mix0, x)(gam[0], bet[0])

    def mlp_fwd(tag, i_ln, n1, n2):
        xhat, rstd, xbf = ln[i_ln]

        def up_epi(acc):
            r = jnp.maximum(acc, 0.0)
            return r * r, r

        hid, relu = mm_nn(tag + "_up", xbf, W(n1, xbf), "col", up_epi, [BF16, BF16])
        (mlp,) = mm_nn(tag + "_down", hid, W(n2, hid), "row", plain, [F32])
        ln[i_ln + 1] = ln_fwd(tag + "_ln", mlp, xhat, gam[i_ln], bet[i_ln])(gam[i_ln + 1], bet[i_ln + 1])
        return hid, relu

    hid0 = mlp_fwd("mlp0", 0, "w1_0", "w2_0")

    x2bf = ln[1][2]
    (kv,) = mm_nn("kv_fwd", x2bf, W("kv", x2bf), "col", plain, [BF16])
    (q,) = mm_nn("q_fwd", x2bf, W("wq", kv), "row", plain, [BF16])
    biases = [bias_expand("bias_d%d" % d, P["rel_bias"], d) for _, d in BRANCHES]
    outs, lses = [], []
    for (win, d), b in zip(BRANCHES, biases):
        assert win // d == BAND and (T // d) % BAND == 0
        o_b, l_b = attn_fwd("attn_fwd_d%d" % d, q, kv, b, d)
        outs.append(o_b)
        lses.append(l_b)
    o, obf, lse = attn_merge("attn_merge", outs, lses)
    (attn,) = mm_nn("wo_fwd", obf, W("wo", obf), "row", plain, [F32])
    ln[2] = ln_fwd("ln2_fwd", attn, ln[1][0], gam[1], bet[1])(gam[2], bet[2])
    hid1 = mlp_fwd("mlp1", 2, "w1_1", "w2_1")

    dr3, dr3bf, g["ln_mlp_g1"], g["ln_mlp_b1"], _, loss_sum = ln_bwd(
        "ln3_bwd", ln[3][0], ln[3][1], gam[3], target=target, beta=bet[3])

    def dw_step(name, wname, a, cot, axis):
        dw = mm_tn(name, a, cot, W(wname, a).shape, axis, deps=ex.take_deps())
        ex.tick(dw)
        ex.add(wname, dw)

    def dx_step(name, cot, wname, axis, epilogue, out_dtype, extras):
        deps = [(d, "dep") for d in ex.take_deps()]
        (out,) = mm_nt(name, cot, W(wname, cot), axis, epilogue, [out_dtype], extras=list(extras) + deps)
        ex.tick(out)
        return out

    def mlp_bwd(tag, i_ln, n1, n2, hid_relu, dr, drbf):
        xbf = ln[i_ln][2]
        hid, relu = hid_relu
        dw_step(tag + "_dw2", n2, hid, drbf, "row")
        dp = dx_step(tag + "_dhid", drbf, n2, "row", lambda acc, r: (acc * (2.0 * r.astype(F32)),), BF16, [(relu, "tile")])
        dw_step(tag + "_dw1", n1, xbf, dp, "col")
        return dx_step(tag + "_dx", dp, n1, "col", lambda acc, e: (acc + ALPHA * e,), F32, [(dr, "tile")])

    dx3 = mlp_bwd("mlp1", 2, "w1_1", "w2_1", hid1, dr3, dr3bf)
    dr2, dr2bf, g["ln_mix_g1"], g["ln_mix_b1"], _ = ln_bwd("ln2_bwd", ln[2][0], ln[2][1], gam[2], dy=dx3)
    dw_step("wo_dw", "wo", obf, dr2bf, "row")
    do = dx_step("wo_dx", dr2bf, "wo", "row", plain, F32, [])
    dobf, dsum = attn_bwd_prep("attn_bwd_prep", do, o)
    dqs, dks, dvs, dsbs = [], [], [], []
    for (win, d), b in zip(BRANCHES, biases):
        dq_b, dk_b, dv_b, dsb = attn_bwd("attn_bwd_d%d" % d, q, kv, dobf, lse, dsum, b, d)
        dqs.append(dq_b)
        dks.append(dk_b)
        dvs.append(dv_b)
        dsbs.append(dsb)
    g["rel_bias"] = relbias_grad("relbias_grad", dsbs)[:, 0, :REL_BUCKETS].T
    dq, dkv = sum_branches("attn_bwd_sum", dqs, dks, dvs)
    dw_step("wq_dw", "wq", x2bf, dq, "row")
    dw_step("kv_dw", "kv", x2bf, dkv, "col")
    dx2a = dx_step("wq_dx", dq, "wq", "row", lambda acc, e: (acc + ALPHA * e,), F32, [(dr2, "tile")])
    dx2 = dx_step("kv_dx", dkv, "kv", "col", lambda acc, e: (acc + e,), F32, [(dx2a, "tile")])

    dr1, dr1bf, g["ln_mlp_g0"], g["ln_mlp_b0"], _ = ln_bwd("ln1_bwd", ln[1][0], ln[1][1], gam[1], dy=dx2)
    dx1 = mlp_bwd("mlp0", 0, "w1_0", "w2_0", hid0, dr1, dr1bf)
    dr0, dr0bf, g["ln_mix_g0"], g["ln_mix_b0"], g["pw2_b"] = ln_bwd("ln0_bwd", ln[0][0], ln[0][1], gam[0], dy=dx1)

    dw_step("pw2_dw", "pw2", s, dr0bf, "row")
    ds = dx_step("pw2_dx", dr0bf, "pw2", "row", plain, F32, [])
    dc, g["cln_g"], g["cln_b"], g["dw_b"] = conv_bwd_ln("conv_bwd_ln", ds, cpre, P["cln_g"], P["cln_b"])
    dh1, g["pw1_b"], g["dw_w"] = conv_bwd_taps("conv_bwd_taps", dc, u, h1, P["dw_w"])
    dw_step("pw1_dw", "pw1", x, dh1, "col")
    dx = dx_step("pw1_dx", dh1, "pw1", "col", lambda acc, e: (acc + ALPHA * e,), F32, [(dr0, "tile")])
    ex.flush(dx)
    return loss_sum, dx, g


BIG = ("pw1", "pw2", "w1_0", "w2_0", "kv", "wq", "wo", "w1_1", "w2_1")


def kernel(x, conv_pw1_w, conv_pw1_b, conv_dw_w, conv_dw_b, conv_ln_g, conv_ln_b, conv_pw2_w, conv_pw2_b, w_kv, attn_wq, attn_wo, rel_bias, mlp_w1, mlp_w2, ln_mix_g, ln_mix_b, ln_mlp_g, ln_mlp_b, loss_target, m_conv_pw1_w, m_conv_pw1_b, m_conv_dw_w, m_conv_dw_b, m_conv_ln_g, m_conv_ln_b, m_conv_pw2_w, m_conv_pw2_b, m_w_kv, m_attn_wq, m_attn_wo, m_rel_bias, m_mlp_w1, m_mlp_w2, m_ln_mix_g, m_ln_mix_b, m_ln_mlp_g, m_ln_mlp_b, v_conv_pw1_w, v_conv_pw1_b, v_conv_dw_w, v_conv_dw_b, v_conv_ln_g, v_conv_ln_b, v_conv_pw2_w, v_conv_pw2_b, v_w_kv, v_attn_wq, v_attn_wo, v_rel_bias, v_mlp_w1, v_mlp_w2, v_ln_mix_g, v_ln_mix_b, v_ln_mlp_g, v_ln_mlp_b):
    _, T, D = x.shape
    xi, yi, ci = _place()
    chip = 2 * xi + yi
    core = jnp.reshape(ci, (1,)).astype(jnp.int32)
    chip1 = jnp.reshape(chip, (1,)).astype(jnp.int32)

    def two_d(a):
        return a.reshape(a.shape[-2:])

    shard = {"pw1": two_d(conv_pw1_w), "pw2": two_d(conv_pw2_w), "kv": w_kv, "wq": two_d(attn_wq), "wo": two_d(attn_wo),
             "w1_0": mlp_w1[0], "w1_1": mlp_w1[1], "w2_0": mlp_w2[0], "w2_1": mlp_w2[1]}
    mom = {"pw1": two_d(m_conv_pw1_w), "pw2": two_d(m_conv_pw2_w), "kv": m_w_kv, "wq": two_d(m_attn_wq), "wo": two_d(m_attn_wo),
           "w1_0": m_mlp_w1[0], "w1_1": m_mlp_w1[1], "w2_0": m_mlp_w2[0], "w2_1": m_mlp_w2[1]}
    vel = {"pw1": two_d(v_conv_pw1_w), "pw2": two_d(v_conv_pw2_w), "kv": v_w_kv, "wq": two_d(v_attn_wq), "wo": two_d(v_attn_wo),
           "w1_0": v_mlp_w1[0], "w1_1": v_mlp_w1[1], "w2_0": v_mlp_w2[0], "w2_1": v_mlp_w2[1]}

    started = {n: split_start("gather_start_" + n, [], [place_shard("place_" + n, shard[n], chip1)], 6, _gather_plan)
               for n in BIG}
    gathered = {}

    def W(n, after):
        if n not in gathered:
            (gathered[n],) = split_wait("gather_wait_" + n, started[n], after, _gather_plan)
        return gathered[n]

    sharded_small = [conv_pw1_b, conv_dw_w[0], conv_dw_b, conv_ln_g, conv_ln_b, conv_pw2_b]
    sh_shapes = [a.shape for a in sharded_small]
    small_all = all_gather8("gather_small", _pack(sharded_small))
    per_chip = [_unpack(small_all[2 * j], sh_shapes) for j in range(N_CHIPS)]
    full = [jnp.concatenate([per_chip[j][i] for j in range(N_CHIPS)], axis=-1) for i in range(len(sharded_small))]
    P = dict(pw1_b=full[0], dw_w=full[1], dw_b=full[2], cln_g=full[3], cln_b=full[4], pw2_b=full[5],
             rel_bias=rel_bias, ln_mix_g=ln_mix_g, ln_mix_b=ln_mix_b, ln_mlp_g=ln_mlp_g, ln_mlp_b=ln_mlp_b)

    ex = GradExchange(chip1, core, shard, mom, vel)
    loss_sum, dx, g = local_step(x.reshape(T, D), loss_target.reshape(T, D), W, P, ex,
                                 first_deps=[started[n][-1] for n in BIG])
    loss = (0.5 / D) * lax.psum(loss_sum[0, 0], ("x", "y", "c"))
    grads_big = {n: ex.results[n][0] for n in BIG}

    small_names = ["pw1_b", "dw_w", "dw_b", "cln_g", "cln_b", "pw2_b", "rel_bias",
                   "ln_mix_g0", "ln_mix_g1", "ln_mix_b0", "ln_mix_b1", "ln_mlp_g0", "ln_mlp_g1", "ln_mlp_b0", "ln_mlp_b1"]
    small_grads = [g[n] for n in small_names]
    sg_shapes = [a.shape for a in small_grads]
    summed = sum_devices("small_grad_sum", all_gather8("gather_small_grads", _pack(small_grads)))
    sg = dict(zip(small_names, _unpack(summed, sg_shapes)))

    def my_cols(a, width):
        return lax.dynamic_slice_in_dim(a, chip * width, width, axis=a.ndim - 1)

    small_g = [my_cols(sg["pw1_b"], conv_pw1_b.shape[-1]),
               my_cols(sg["dw_w"], conv_dw_w.shape[-1])[None],
               my_cols(sg["dw_b"], conv_dw_b.shape[-1]), my_cols(sg["cln_g"], conv_ln_g.shape[-1]),
               my_cols(sg["cln_b"], conv_ln_b.shape[-1]), my_cols(sg["pw2_b"], conv_pw2_b.shape[-1]),
               sg["rel_bias"],
               jnp.concatenate([sg["ln_mix_g0"], sg["ln_mix_g1"]], axis=0),
               jnp.concatenate([sg["ln_mix_b0"], sg["ln_mix_b1"]], axis=0),
               jnp.concatenate([sg["ln_mlp_g0"], sg["ln_mlp_g1"]], axis=0),
               jnp.concatenate([sg["ln_mlp_b0"], sg["ln_mlp_b1"]], axis=0)]
    small_w = [conv_pw1_b, conv_dw_w, conv_dw_b, conv_ln_g, conv_ln_b, conv_pw2_b, rel_bias, ln_mix_g, ln_mix_b, ln_mlp_g, ln_mlp_b]
    small_m = [m_conv_pw1_b, m_conv_dw_w, m_conv_dw_b, m_conv_ln_g, m_conv_ln_b, m_conv_pw2_b, m_rel_bias, m_ln_mix_g, m_ln_mix_b, m_ln_mlp_g, m_ln_mlp_b]
    small_v = [v_conv_pw1_b, v_conv_dw_w, v_conv_dw_b, v_conv_ln_g, v_conv_ln_b, v_conv_pw2_b, v_rel_bias, v_ln_mix_g, v_ln_mix_b, v_ln_mlp_g, v_ln_mlp_b]
    sw_shapes = [a.shape for a in small_w]
    small_g = [a.reshape(s) for a, s in zip(small_g, sw_shapes)]
    upd_small = adamw("adamw_small", _pack(small_w), _pack(small_g), _pack(small_m), _pack(small_v))
    sd, snm, snv = (_unpack(b, sw_shapes) for b in upd_small)

    def big_out(tree):
        return dict(pw1=tree["pw1"][None], pw2=tree["pw2"][None], kv=tree["kv"], wq=tree["wq"][None], wo=tree["wo"][None],
                    w1=jnp.stack([tree["w1_0"], tree["w1_1"]]), w2=jnp.stack([tree["w2_0"], tree["w2_1"]]))

    def ordered(big, small):
        return [big["pw1"], small[0], small[1], small[2], small[3], small[4], big["pw2"], small[5], big["kv"], big["wq"],
                big["wo"], small[6], big["w1"], big["w2"], small[7], small[8], small[9], small[10]]

    grads = ordered(big_out(grads_big), small_g)
    deltas = ordered(big_out({n: ex.results[n][1] for n in BIG}), sd)
    new_m = ordered(big_out({n: ex.results[n][2] for n in BIG}), snm)
    new_v = ordered(big_out({n: ex.results[n][3] for n in BIG}), snv)
    return (loss, dx.reshape(1, T, D), *grads, *deltas, *new_m, *new_v)
```

```python
import functools
import math

import numpy as np
import jax
import jax.numpy as jnp
from jax import lax
from jax.experimental import pallas as pl
from jax.experimental.pallas import tpu as pltpu

F32 = jnp.float32
BF16 = jnp.bfloat16

HEAD_DIM = 128
BAND = 128
BRANCHES = ((128, 1), (512, 4), (2048, 16))
CONV_WIDTH = 31
CONV_HALO = 32
REL_BUCKETS = 32
REL_MAX_DIST = 2048
DEPTH = 2
ALPHA = (2 * DEPTH) ** 0.25
LN_EPS = 1e-5
ADAM_LR, ADAM_B1, ADAM_B2, ADAM_EPS, ADAM_WD, ADAM_STEP = 0.001, 0.9, 0.999, 1e-08, 0.01, 10

N_CHIPS = 4
N_DEV = 8
MESH = pl.DeviceIdType.MESH
VMEM_LIMIT_BYTES = 56 * 1024 * 1024
MM_TM, MM_TN, MM_TK = 1024, 1024, 2048
ROW_TILE = 256
CONV_TILE = 128
ATTN_CHUNK = 1024
NEG_BIG = -1e30


def _cparams(sem):
    return pltpu.CompilerParams(dimension_semantics=sem, vmem_limit_bytes=VMEM_LIMIT_BYTES)


def _sigmoid(x):
    return 1.0 / (1.0 + jnp.exp(-x))


def _wspec(wshape, axis, br, bc, rsel, csel):
    _, R, C = wshape
    if axis == "col":
        nb = C // bc
        assert nb * bc == C, (wshape, bc)
        return pl.BlockSpec((None, br, bc), lambda *g: (csel(*g) // nb, rsel(*g), csel(*g) % nb))
    nb = R // br
    assert nb * br == R, (wshape, br)
    return pl.BlockSpec((None, br, bc), lambda *g: (rsel(*g) // nb, rsel(*g) % nb, csel(*g)))


def _full_dims(wshape, axis):
    _, R, C = wshape
    return (R, N_CHIPS * C) if axis == "col" else (N_CHIPS * R, C)


def _mm_body(nk, kinds, n_out, dims, epilogue):
    n_extra = len(kinds)

    def body(*refs):
        a_ref, b_ref = refs[0], refs[1]
        extra = [r for r, kind in zip(refs[2:2 + n_extra], kinds) if kind != "dep"]
        outs = refs[2 + n_extra:2 + n_extra + n_out]
        part = lax.dot_general(a_ref[...].astype(BF16), b_ref[...].astype(BF16), (dims, ((), ())),
                               preferred_element_type=F32)
        if nk == 1:
            res = epilogue(part, *[e[...] for e in extra])
            for r, o in zip(res, outs):
                o[...] = r.astype(o.dtype)
            return
        acc_ref = refs[2 + n_extra + n_out]
        k = pl.program_id(2)

        @pl.when(k == 0)
        def _():
            acc_ref[...] = part

        @pl.when(k > 0)
        def _():
            acc_ref[...] += part

        @pl.when(k == nk - 1)
        def _():
            res = epilogue(acc_ref[...], *[e[...] for e in extra])
            for r, o in zip(res, outs):
                o[...] = r.astype(o.dtype)
    return body


def _extra_specs(extras, tm, tn):
    specs = []
    for arr, kind in extras:
        if kind == "tile":
            specs.append(pl.BlockSpec((tm, tn), lambda i, j, k: (i, j)))
        elif kind == "dep":
            specs.append(pl.BlockSpec(arr.shape, lambda i, j, k: (0, 0)))
        else:
            specs.append(pl.BlockSpec((1, tn), lambda i, j, k: (0, j)))
    return specs


def mm_nn(name, a, w, axis, epilogue, out_dtypes, extras=()):
    M, K = a.shape
    Kw, N = _full_dims(w.shape, axis)
    assert K == Kw
    tm, tn, tk = min(MM_TM, M), min(MM_TN, N), min(MM_TK, K)
    if axis == "col":
        tn = min(tn, w.shape[2])
    else:
        tk = min(tk, w.shape[1])
    nk = K // tk
    in_specs = [pl.BlockSpec((tm, tk), lambda i, j, k: (i, k)),
                _wspec(w.shape, axis, tk, tn, lambda i, j, k: k, lambda i, j, k: j)]
    in_specs += _extra_specs(extras, tm, tn)
    body = _mm_body(nk, [kind for _, kind in extras], len(out_dtypes), ((1,), (0,)), epilogue)
    return pl.pallas_call(
        body, name=name, grid=(M // tm, N // tn, nk), in_specs=in_specs,
        out_specs=[pl.BlockSpec((tm, tn), lambda i, j, k: (i, j)) for _ in out_dtypes],
        out_shape=[jax.ShapeDtypeStruct((M, N), d) for d in out_dtypes],
        scratch_shapes=[pltpu.VMEM((tm, tn), F32)] if nk > 1 else [],
        compiler_params=_cparams(("parallel", "parallel", "arbitrary")),
    )(a, w, *[e for e, _ in extras])


def mm_nt(name, g, w, axis, epilogue, out_dtypes, extras=()):
    M, N = g.shape
    K, Nw = _full_dims(w.shape, axis)
    assert N == Nw
    tm, tn, tk = min(MM_TM, M), min(MM_TN, K), min(MM_TK, N)
    if axis == "col":
        tk = min(tk, w.shape[2])
    else:
        tn = min(tn, w.shape[1])
    nk = N // tk
    in_specs = [pl.BlockSpec((tm, tk), lambda i, j, k: (i, k)),
                _wspec(w.shape, axis, tn, tk, lambda i, j, k: j, lambda i, j, k: k)]
    in_specs += _extra_specs(extras, tm, tn)
    body = _mm_body(nk, [kind for _, kind in extras], len(out_dtypes), ((1,), (1,)), epilogue)
    return pl.pallas_call(
        body, name=name, grid=(M // tm, K // tn, nk), in_specs=in_specs,
        out_specs=[pl.BlockSpec((tm, tn), lambda i, j, k: (i, j)) for _ in out_dtypes],
        out_shape=[jax.ShapeDtypeStruct((M, K), d) for d in out_dtypes],
        scratch_shapes=[pltpu.VMEM((tm, tn), F32)] if nk > 1 else [],
        compiler_params=_cparams(("parallel", "parallel", "arbitrary")),
    )(g, w, *[e for e, _ in extras])


def mm_tn(name, a, g, wshape, axis, deps=()):
    M, K = a.shape
    Mg, N = g.shape
    assert M == Mg and (K, N) == _full_dims(wshape, axis)
    tm, tn, tk = min(MM_TM, K), min(MM_TN, N), min(MM_TK, M)
    if axis == "col":
        tn = min(tn, wshape[2])
    else:
        tm = min(tm, wshape[1])
    nk = M // tk
    body = _mm_body(nk, ["dep"] * len(deps), 1, ((0,), (0,)), lambda acc: (acc,))
    return pl.pallas_call(
        body, name=name, grid=(K // tm, N // tn, nk),
        in_specs=[pl.BlockSpec((tk, tm), lambda i, j, k: (k, i)),
                  pl.BlockSpec((tk, tn), lambda i, j, k: (k, j))] + _extra_specs([(d, "dep") for d in deps], tm, tn),
        out_specs=[_wspec(wshape, axis, tm, tn, lambda i, j, k: i, lambda i, j, k: j)],
        out_shape=[jax.ShapeDtypeStruct(wshape, F32)],
        scratch_shapes=[pltpu.VMEM((tm, tn), F32)] if nk > 1 else [],
        compiler_params=_cparams(("parallel", "parallel", "arbitrary")),
    )(a, g, *deps)[0]


def _row_spec(tr, width):
    return pl.BlockSpec((tr, width), lambda i: (i, 0))


def _vec_spec(width):
    return pl.BlockSpec((1, width), lambda i: (0, 0))


def _fold8(x):
    r, d = x.shape
    return jnp.sum(x.reshape(r // 8, 8, d), axis=0)


def ln_fwd(name, f, prev, prev_g=None, prev_b=None):
    T, D = f.shape
    tr = min(ROW_TILE, T)
    affine = prev_g is not None

    def body(*refs):
        if affine:
            f_ref, p_ref, pg_ref, pb_ref, g_ref, b_ref, xhat_ref, rstd_ref, xbf_ref = refs
            xprev = p_ref[...] * pg_ref[...] + pb_ref[...]
        else:
            f_ref, p_ref, g_ref, b_ref, xhat_ref, rstd_ref, xbf_ref = refs
            xprev = p_ref[...]
        r = ALPHA * xprev + f_ref[...]
        mu = jnp.mean(r, axis=-1, keepdims=True)
        cen = r - mu
        var = jnp.mean(cen * cen, axis=-1, keepdims=True)
        rstd = lax.rsqrt(var + LN_EPS)
        xhat = cen * rstd
        xhat_ref[...] = xhat
        rstd_ref[...] = rstd
        xbf_ref[...] = (xhat * g_ref[...] + b_ref[...]).astype(BF16)

    def call(g, b):
        ins = [f, prev] + ([prev_g, prev_b] if affine else []) + [g, b]
        specs = [_row_spec(tr, D), _row_spec(tr, D)] + ([_vec_spec(D)] * 2 if affine else []) + [_vec_spec(D)] * 2
        return pl.pallas_call(
            body, name=name, grid=(T // tr,), in_specs=specs,
            out_specs=[_row_spec(tr, D), _row_spec(tr, 1), _row_spec(tr, D)],
            out_shape=[jax.ShapeDtypeStruct((T, D), F32), jax.ShapeDtypeStruct((T, 1), F32),
                       jax.ShapeDtypeStruct((T, D), BF16)],
            compiler_params=_cparams(("parallel",)),
        )(*ins)
    return call


def ln_bwd(name, xhat, rstd, gamma, dy=None, target=None, beta=None):
    T, D = xhat.shape
    tr = min(ROW_TILE, T)
    nt = T // tr
    head = target is not None

    def body(*refs):
        if head:
            xhat_ref, rstd_ref, g_ref, tgt_ref, b_ref = refs[:5]
            outs = refs[5:]
        else:
            xhat_ref, rstd_ref, g_ref, dy_ref = refs[:4]
            outs = refs[4:]
        dr_ref, drbf_ref, dg_ref, db_ref, cs_ref = outs[:5]
        rest = outs[5:]
        if head:
            loss_ref, acc_ref = rest
        else:
            (acc_ref,) = rest
        i = pl.program_id(0)
        xhat_v = xhat_ref[...]
        gam = g_ref[...]
        if head:
            diff = xhat_v * gam + b_ref[...] - tgt_ref[...]
            dyv = diff * (1.0 / D)
        else:
            dyv = dy_ref[...]
        dxh = dyv * gam
        m1 = jnp.mean(dxh, axis=-1, keepdims=True)
        m2 = jnp.mean(dxh * xhat_v, axis=-1, keepdims=True)
        dr = rstd_ref[...] * (dxh - m1 - xhat_v * m2)
        dr_ref[...] = dr
        drbf_ref[...] = dr.astype(BF16)

        @pl.when(i == 0)
        def _():
            acc_ref[...] = jnp.zeros_like(acc_ref)

        acc_ref[0] += _fold8(dyv * xhat_v)
        acc_ref[1] += _fold8(dyv)
        acc_ref[2] += _fold8(dr)
        if head:
            acc_ref[3] += _fold8(diff * diff)

        @pl.when(i == nt - 1)
        def _():
            dg_ref[...] = jnp.sum(acc_ref[0], axis=0, keepdims=True)
            db_ref[...] = jnp.sum(acc_ref[1], axis=0, keepdims=True)
            cs_ref[...] = jnp.sum(acc_ref[2], axis=0, keepdims=True)
            if head:
                loss_ref[...] = jnp.sum(jnp.sum(acc_ref[3], axis=0, keepdims=True), axis=1, keepdims=True)

    ins = [xhat, rstd, gamma] + ([target, beta] if head else [dy])
    specs = [_row_spec(tr, D), _row_spec(tr, 1), _vec_spec(D)] + ([_row_spec(tr, D), _vec_spec(D)] if head else [_row_spec(tr, D)])
    out_specs = [_row_spec(tr, D), _row_spec(tr, D), _vec_spec(D), _vec_spec(D), _vec_spec(D)]
    out_shape = [jax.ShapeDtypeStruct((T, D), F32), jax.ShapeDtypeStruct((T, D), BF16)] + [jax.ShapeDtypeStruct((1, D), F32)] * 3
    if head:
        out_specs.append(pl.BlockSpec((1, 1), lambda i: (0, 0)))
        out_shape.append(jax.ShapeDtypeStruct((1, 1), F32))
    return pl.pallas_call(
        body, name=name, grid=(nt,), in_specs=specs, out_specs=out_specs, out_shape=out_shape,
        scratch_shapes=[pltpu.VMEM((4, 8, D), F32)],
        compiler_params=_cparams(("arbitrary",)),
    )(*ins)


CONV_ROWS, CONV_COLS = 64, 512


def _tap_chunks(tt, D):
    for r0 in range(0, tt, min(CONV_ROWS, tt)):
        for c0 in range(0, D, min(CONV_COLS, D)):
            yield r0, min(CONV_ROWS, tt), c0, min(CONV_COLS, D)


SUBLANES = 8


def _shifted_copies(ext_ref, sh_ref):
    n = sh_ref.shape[1]
    for b in range(1, SUBLANES):
        sh_ref[b - 1] = ext_ref[pl.ds(b, n), :]


def _rows_at(ext_ref, sh_ref, off, nr, cols):
    a, b = divmod(off, SUBLANES)
    if b == 0:
        return ext_ref[pl.ds(off, nr), cols]
    return sh_ref[b - 1, pl.ds(a * SUBLANES, nr), cols]


def conv_fwd(name, h1, dw, dwb, lng, lnb):
    T, D2 = h1.shape
    D = D2 // 2
    tt = min(CONV_TILE, T)
    hb = tt // CONV_HALO
    KW = dw.shape[0]
    lead = CONV_HALO - (KW - 1)

    def body(a_ref, g_ref, ah_ref, gh_ref, dw_ref, dwb_ref, lng_ref, lnb_ref, u_ref, c_ref, s_ref, ext_ref, sh_ref):
        i = pl.program_id(0)
        u = a_ref[...] * _sigmoid(g_ref[...])
        u_ref[...] = u
        uh = ah_ref[...] * _sigmoid(gh_ref[...])
        ext_ref[pl.ds(0, CONV_HALO), :] = jnp.where(i > 0, uh, 0.0)
        ext_ref[pl.ds(CONV_HALO, tt), :] = u
        _shifted_copies(ext_ref, sh_ref)
        for r0, nr, c0, nc in _tap_chunks(tt, D):
            cols = pl.ds(c0, nc)
            acc = jnp.zeros((nr, nc), F32) + dwb_ref[:, cols]
            for k in range(KW):
                acc = acc + dw_ref[pl.ds(k, 1), cols] * _rows_at(ext_ref, sh_ref, r0 + lead + k, nr, cols)
            c_ref[pl.ds(r0, nr), cols] = acc
        c = c_ref[...]
        mu = jnp.mean(c, axis=-1, keepdims=True)
        cen = c - mu
        var = jnp.mean(cen * cen, axis=-1, keepdims=True)
        n = cen * lax.rsqrt(var + LN_EPS) * lng_ref[...] + lnb_ref[...]
        s_ref[...] = (n * _sigmoid(n)).astype(BF16)

    halo = lambda col: pl.BlockSpec((CONV_HALO, D), lambda i: (jnp.maximum(i * hb - 1, 0), col))
    return pl.pallas_call(
        body, name=name, grid=(T // tt,),
        in_specs=[pl.BlockSpec((tt, D), lambda i: (i, 0)), pl.BlockSpec((tt, D), lambda i: (i, 1)), halo(0), halo(1),
                  pl.BlockSpec((KW, D), lambda i: (0, 0)), _vec_spec(D), _vec_spec(D), _vec_spec(D)],
        out_specs=[_row_spec(tt, D)] * 3,
        out_shape=[jax.ShapeDtypeStruct((T, D), F32), jax.ShapeDtypeStruct((T, D), F32), jax.ShapeDtypeStruct((T, D), BF16)],
        scratch_shapes=[pltpu.VMEM((tt + CONV_HALO, D), F32),
                        pltpu.VMEM((SUBLANES - 1, tt + CONV_HALO - SUBLANES, D), F32)],
        compiler_params=_cparams(("parallel",)),
    )(h1, h1, h1, h1, dw, dwb, lng, lnb)


def conv_bwd_ln(name, ds, c, lng, lnb):
    T, D = c.shape
    tr = min(ROW_TILE, T)
    nt = T // tr

    def body(ds_ref, c_ref, g_ref, b_ref, dc_ref, dg_ref, db_ref, cs_ref, acc_ref):
        i = pl.program_id(0)
        cv = c_ref[...]
        mu = jnp.mean(cv, axis=-1, keepdims=True)
        cen = cv - mu
        var = jnp.mean(cen * cen, axis=-1, keepdims=True)
        rstd = lax.rsqrt(var + LN_EPS)
        chat = cen * rstd
        n = chat * g_ref[...] + b_ref[...]
        sg = _sigmoid(n)
        dn = ds_ref[...] * (sg * (1.0 + n * (1.0 - sg)))
        dxh = dn * g_ref[...]
        m1 = jnp.mean(dxh, axis=-1, keepdims=True)
        m2 = jnp.mean(dxh * chat, axis=-1, keepdims=True)
        dc = rstd * (dxh - m1 - chat * m2)
        dc_ref[...] = dc

        @pl.when(i == 0)
        def _():
            acc_ref[...] = jnp.zeros_like(acc_ref)

        acc_ref[0] += _fold8(dn * chat)
        acc_ref[1] += _fold8(dn)
        acc_ref[2] += _fold8(dc)

        @pl.when(i == nt - 1)
        def _():
            dg_ref[...] = jnp.sum(acc_ref[0], axis=0, keepdims=True)
            db_ref[...] = jnp.sum(acc_ref[1], axis=0, keepdims=True)
            cs_ref[...] = jnp.sum(acc_ref[2], axis=0, keepdims=True)

    return pl.pallas_call(
        body, name=name, grid=(nt,),
        in_specs=[_row_spec(tr, D), _row_spec(tr, D), _vec_spec(D), _vec_spec(D)],
        out_specs=[_row_spec(tr, D), _vec_spec(D), _vec_spec(D), _vec_spec(D)],
        out_shape=[jax.ShapeDtypeStruct((T, D), F32)] + [jax.ShapeDtypeStruct((1, D), F32)] * 3,
        scratch_shapes=[pltpu.VMEM((3, 8, D), F32)],
        compiler_params=_cparams(("arbitrary",)),
    )(ds, c, lng, lnb)


def conv_bwd_taps(name, dc, u, h1, dw):
    T, D = dc.shape
    tt = min(CONV_TILE, T)
    nt = T // tt
    hb = tt // CONV_HALO
    nhb = T // CONV_HALO
    KW = dw.shape[0]
    lead = CONV_HALO - (KW - 1)

    def body(dc_ref, dcn_ref, u_ref, uh_ref, a_ref, g_ref, dw_ref, dh1_ref, db1_ref, ddw_ref,
             edc_ref, eu_ref, du_ref, accw_ref, accb_ref, shdc_ref, shu_ref):
        i = pl.program_id(0)

        @pl.when(i == 0)
        def _():
            accw_ref[...] = jnp.zeros_like(accw_ref)
            accb_ref[...] = jnp.zeros_like(accb_ref)

        edc_ref[pl.ds(0, tt), :] = dc_ref[...]
        edc_ref[pl.ds(tt, CONV_HALO), :] = jnp.where(i < nt - 1, dcn_ref[...], 0.0)
        eu_ref[pl.ds(0, CONV_HALO), :] = jnp.where(i > 0, uh_ref[...], 0.0)
        eu_ref[pl.ds(CONV_HALO, tt), :] = u_ref[...]
        _shifted_copies(edc_ref, shdc_ref)
        _shifted_copies(eu_ref, shu_ref)
        for r0, nr, c0, nc in _tap_chunks(tt, D):
            cols = pl.ds(c0, nc)
            dcv = dc_ref[pl.ds(r0, nr), cols]
            acc = jnp.zeros((nr, nc), F32)
            for k in range(KW):
                acc = acc + dw_ref[pl.ds(k, 1), cols] * _rows_at(edc_ref, shdc_ref, r0 + (KW - 1) - k, nr, cols)
                accw_ref[k, :, cols] += _fold8(dcv * _rows_at(eu_ref, shu_ref, r0 + lead + k, nr, cols))
            du_ref[pl.ds(r0, nr), cols] = acc
        du = du_ref[...]
        sg = _sigmoid(g_ref[...])
        da = du * sg
        dg = du * a_ref[...] * sg * (1.0 - sg)
        dh1_ref[:, pl.ds(0, D)] = da.astype(BF16)
        dh1_ref[:, pl.ds(D, D)] = dg.astype(BF16)
        accb_ref[:, pl.ds(0, D)] += _fold8(da)
        accb_ref[:, pl.ds(D, D)] += _fold8(dg)

        @pl.when(i == nt - 1)
        def _():
            db1_ref[...] = jnp.sum(accb_ref[...], axis=0, keepdims=True)
            ddw_ref[...] = jnp.sum(accw_ref[...], axis=1)

    return pl.pallas_call(
        body, name=name, grid=(nt,),
        in_specs=[_row_spec(tt, D),
                  pl.BlockSpec((CONV_HALO, D), lambda i: (jnp.minimum((i + 1) * hb, nhb - 1), 0)),
                  _row_spec(tt, D),
                  pl.BlockSpec((CONV_HALO, D), lambda i: (jnp.maximum(i * hb - 1, 0), 0)),
                  pl.BlockSpec((tt, D), lambda i: (i, 0)), pl.BlockSpec((tt, D), lambda i: (i, 1)),
                  pl.BlockSpec((KW, D), lambda i: (0, 0))],
        out_specs=[_row_spec(tt, 2 * D), _vec_spec(2 * D), pl.BlockSpec((KW, D), lambda i: (0, 0))],
        out_shape=[jax.ShapeDtypeStruct((T, 2 * D), BF16), jax.ShapeDtypeStruct((1, 2 * D), F32),
                   jax.ShapeDtypeStruct((KW, D), F32)],
        scratch_shapes=[pltpu.VMEM((tt + CONV_HALO, D), F32), pltpu.VMEM((tt + CONV_HALO, D), F32),
                        pltpu.VMEM((tt, D), F32), pltpu.VMEM((KW, 8, D), F32), pltpu.VMEM((8, 2 * D), F32)]
                       + [pltpu.VMEM((SUBLANES - 1, tt + CONV_HALO - SUBLANES, D), F32)] * 2,
        compiler_params=_cparams(("arbitrary",)),
    )(dc, dc, u, u, h1, h1, dw)


def _t5_bucket(dist):
    max_exact = REL_BUCKETS // 2
    large = max_exact + (np.log(np.maximum(dist, 1) / max_exact) / math.log(REL_MAX_DIST / max_exact)
                         * (REL_BUCKETS - max_exact)).astype(np.int32)
    large = np.minimum(large, REL_BUCKETS - 1)
    return np.where(dist < max_exact, dist, large).astype(np.int32)


def _bucket_table(dil):
    i = np.arange(BAND)[:, None]
    j = np.arange(2 * BAND)[None, :]
    delta = i - j + BAND
    return _t5_bucket(np.clip(delta, 0, None) * dil)


def bias_expand(name, rel_bias, dil):
    n_heads = rel_bias.shape[1]
    idx = jnp.asarray(_bucket_table(dil))

    def body(rel_ref, idx_ref, out_ref):
        h = pl.program_id(0)
        idxv = idx_ref[...]
        b = jnp.zeros((BAND, 2 * BAND), F32)
        for bk in range(REL_BUCKETS):
            b = jnp.where(idxv == bk, rel_ref[bk, h], b)
        out_ref[...] = b

    return pl.pallas_call(
        body, name=name, grid=(n_heads,),
        in_specs=[pl.BlockSpec(memory_space=pltpu.SMEM), pl.BlockSpec((BAND, 2 * BAND), lambda h: (0, 0))],
        out_specs=pl.BlockSpec((None, BAND, 2 * BAND), lambda h: (h, 0, 0)),
        out_shape=jax.ShapeDtypeStruct((n_heads, BAND, 2 * BAND), F32),
        compiler_params=_cparams(("arbitrary",)),
    )(rel_bias, idx)


def relbias_grad(name, dsb_list):
    n_heads = dsb_list[0].shape[0]
    idxs = [jnp.asarray(_bucket_table(d)) for _, d in BRANCHES]
    nb = len(BRANCHES)

    def body(*refs):
        ds_refs, idx_refs, out_ref = refs[:nb], refs[nb:2 * nb], refs[2 * nb]
        lane = lax.broadcasted_iota(jnp.int32, (1, 128), 1)
        row = jnp.zeros((1, 128), F32)
        for bk in range(REL_BUCKETS):
            tot = jnp.zeros((1, 1), F32)
            for ds_ref, idx_ref in zip(ds_refs, idx_refs):
                sel = jnp.where(idx_ref[...] == bk, ds_ref[...], 0.0)
                tot = tot + jnp.sum(jnp.sum(sel, axis=0, keepdims=True), axis=1, keepdims=True)
            row = jnp.where(lane == bk, tot, row)
        out_ref[...] = row

    return pl.pallas_call(
        body, name=name, grid=(n_heads,),
        in_specs=[pl.BlockSpec((None, BAND, 2 * BAND), lambda h: (h, 0, 0))] * nb
                 + [pl.BlockSpec((BAND, 2 * BAND), lambda h: (0, 0))] * nb,
        out_specs=pl.BlockSpec((None, 1, 128), lambda h: (h, 0, 0)),
        out_shape=jax.ShapeDtypeStruct((n_heads, 1, 128), F32),
        compiler_params=_cparams(("arbitrary",)),
    )(*dsb_list, *idxs)


def _band_mask():
    i = lax.broadcasted_iota(jnp.int32, (BAND, 2 * BAND), 0)
    j = lax.broadcasted_iota(jnp.int32, (BAND, 2 * BAND), 1)
    return (j >= i) & (j <= i + BAND), j


def _rep2(x):
    return jnp.concatenate([x, x], axis=1)


def _attn_views(T, dil):
    L = T // dil
    lc = min(ATTN_CHUNK, L)
    return L, lc, L // lc, lc // BAND


def attn_fwd(name, q, kv, bias, dil):
    T, D = q.shape
    n_heads = D // HEAD_DIM
    L, lc, nchunk, nsub = _attn_views(T, dil)
    scale = HEAD_DIM ** -0.5
    q3 = q.reshape(L, dil * D)
    kv3 = kv.reshape(L, dil * 2 * D)

    def body(q_ref, k_ref, v_ref, kp_ref, vp_ref, b_ref, o_ref, lse_ref, kext_ref, vext_ref):
        c = pl.program_id(2)
        kext_ref[pl.ds(0, BAND), :] = kp_ref[...]
        vext_ref[pl.ds(0, BAND), :] = vp_ref[...]
        kext_ref[pl.ds(BAND, lc), :] = k_ref[...]
        vext_ref[pl.ds(BAND, lc), :] = v_ref[...]
        band, jcol = _band_mask()
        bias_v = b_ref[...]

        def sub(a, carry):
            off = pl.multiple_of(a * BAND, BAND)
            qa = q_ref[pl.ds(off, BAND), :]
            kw = kext_ref[pl.ds(off, 2 * BAND), :]
            vw = vext_ref[pl.ds(off, 2 * BAND), :]
            s = lax.dot_general(qa, kw, (((1,), (1,)), ((), ())), preferred_element_type=F32) * scale + bias_v
            first = jnp.logical_and(c == 0, a == 0)
            valid = band & jnp.logical_or(jcol >= BAND, jnp.logical_not(first))
            s = jnp.where(valid, s, NEG_BIG)
            m = jnp.max(s, axis=-1, keepdims=True)
            p = jnp.exp(s - m)
            den = jnp.sum(p, axis=-1, keepdims=True)
            pv = lax.dot_general(p.astype(BF16), vw, (((1,), (0,)), ((), ())), preferred_element_type=F32)
            o_ref[pl.ds(off, BAND), :] = pv / den
            lse_ref[pl.ds(off, BAND), :] = jnp.broadcast_to(m + jnp.log(den), (BAND, HEAD_DIM))
            return carry

        lax.fori_loop(0, nsub, sub, 0, unroll=True)

    nsb = lc // BAND
    blk = lambda w, col0: pl.BlockSpec((lc, HEAD_DIM), lambda h, r, c: (c, r * w + col0 + h))
    halo = lambda w, col0: pl.BlockSpec((BAND, HEAD_DIM), lambda h, r, c: (jnp.maximum(c * nsb - 1, 0), r * w + col0 + h))
    nh = n_heads
    o3, lse3 = pl.pallas_call(
        body, name=name, grid=(n_heads, dil, nchunk),
        in_specs=[blk(nh, 0), blk(2 * nh, 0), blk(2 * nh, nh), halo(2 * nh, 0), halo(2 * nh, nh),
                  pl.BlockSpec((None, BAND, 2 * BAND), lambda h, r, c: (h, 0, 0))],
        out_specs=[blk(nh, 0), blk(nh, 0)],
        out_shape=[jax.ShapeDtypeStruct((L, dil * D), F32)] * 2,
        scratch_shapes=[pltpu.VMEM((lc + BAND, HEAD_DIM), BF16)] * 2,
        compiler_params=_cparams(("arbitrary", "arbitrary", "arbitrary")),
    )(q3, kv3, kv3, kv3, kv3, bias)
    return o3.reshape(T, D), lse3.reshape(T, D)


def attn_merge(name, outs, lses):
    T, D = outs[0].shape
    tr = min(ROW_TILE, T)
    nb = len(outs)

    def body(*refs):
        o_refs, l_refs = refs[:nb], refs[nb:2 * nb]
        o_ref, obf_ref, lse_ref = refs[2 * nb:]
        ls = [r[...] for r in l_refs]
        m = functools.reduce(jnp.maximum, ls)
        es = [jnp.exp(l - m) for l in ls]
        tot = functools.reduce(lambda x, y: x + y, es)
        o = functools.reduce(lambda x, y: x + y, [(e / tot) * r[...] for e, r in zip(es, o_refs)])
        o_ref[...] = o
        obf_ref[...] = o.astype(BF16)
        lse_ref[...] = m + jnp.log(tot)

    return pl.pallas_call(
        body, name=name, grid=(T // tr,), in_specs=[_row_spec(tr, D)] * (2 * nb),
        out_specs=[_row_spec(tr, D)] * 3,
        out_shape=[jax.ShapeDtypeStruct((T, D), F32), jax.ShapeDtypeStruct((T, D), BF16), jax.ShapeDtypeStruct((T, D), F32)],
        compiler_params=_cparams(("parallel",)),
    )(*outs, *lses)


def attn_bwd_prep(name, do, o):
    T, D = o.shape
    n_heads = D // HEAD_DIM
    tr = min(ROW_TILE, T)

    def body(do_ref, o_ref, dobf_ref, dsum_ref):
        dobf_ref[...] = do_ref[...].astype(BF16)
        for h in range(n_heads):
            cols = pl.ds(h * HEAD_DIM, HEAD_DIM)
            d = jnp.sum(do_ref[:, cols] * o_ref[:, cols], axis=-1, keepdims=True)
            dsum_ref[:, cols] = jnp.broadcast_to(d, (tr, HEAD_DIM))

    return pl.pallas_call(
        body, name=name, grid=(T // tr,), in_specs=[_row_spec(tr, D)] * 2, out_specs=[_row_spec(tr, D)] * 2,
        out_shape=[jax.ShapeDtypeStruct((T, D), BF16), jax.ShapeDtypeStruct((T, D), F32)],
        compiler_params=_cparams(("parallel",)),
    )(do, o)


def attn_bwd(name, q, kv, do, lse, dsum, bias, dil):
    T, D = q.shape
    n_heads = D // HEAD_DIM
    L, lc, nchunk, nsub = _attn_views(T, dil)
    scale = HEAD_DIM ** -0.5
    q3, do3, lse3, dsum3 = (t.reshape(L, dil * D) for t in (q, do, lse, dsum))
    kv3 = kv.reshape(L, dil * 2 * D)
    nt_dims = (((1,), (1,)), ((), ()))
    tn_dims = (((0,), (0,)), ((), ()))
    nn_dims = (((1,), (0,)), ((), ()))

    def body(q_ref, k_ref, v_ref, do_ref, lse_ref, ds_ref, kp_ref, vp_ref, qn_ref, don_ref, lsen_ref, dsn_ref, b_ref,
             dq_ref, dk_ref, dv_ref, dsb_ref, kext_ref, vext_ref, dkext_ref, dvext_ref):
        r = pl.program_id(1)
        c = pl.program_id(2)
        kext_ref[pl.ds(0, BAND), :] = kp_ref[...]
        vext_ref[pl.ds(0, BAND), :] = vp_ref[...]
        kext_ref[pl.ds(BAND, lc), :] = k_ref[...]
        vext_ref[pl.ds(BAND, lc), :] = v_ref[...]
        dkext_ref[...] = jnp.zeros_like(dkext_ref)
        dvext_ref[...] = jnp.zeros_like(dvext_ref)
        band, jcol = _band_mask()
        bias_v = b_ref[...]

        @pl.when(jnp.logical_and(r == 0, c == 0))
        def _():
            dsb_ref[...] = jnp.zeros_like(dsb_ref)

        def sub(a, carry):
            off = pl.multiple_of(a * BAND, BAND)
            qa = q_ref[pl.ds(off, BAND), :]
            doa = do_ref[pl.ds(off, BAND), :]
            kw = kext_ref[pl.ds(off, 2 * BAND), :]
            vw = vext_ref[pl.ds(off, 2 * BAND), :]
            s = lax.dot_general(qa, kw, nt_dims, preferred_element_type=F32) * scale + bias_v
            first = jnp.logical_and(c == 0, a == 0)
            valid = band & jnp.logical_or(jcol >= BAND, jnp.logical_not(first))
            p = jnp.where(valid, jnp.exp(s - _rep2(lse_ref[pl.ds(off, BAND), :])), 0.0)
            dp = lax.dot_general(doa, vw, nt_dims, preferred_element_type=F32)
            ds = p * (dp - _rep2(ds_ref[pl.ds(off, BAND), :]))
            dsb_ref[...] += ds
            dsb16 = ds.astype(BF16)
            dq_ref[pl.ds(off, BAND), :] = lax.dot_general(dsb16, kw, nn_dims, preferred_element_type=F32) * scale
            dkext_ref[pl.ds(off, 2 * BAND), :] += lax.dot_general(dsb16, qa, tn_dims, preferred_element_type=F32) * scale
            dvext_ref[pl.ds(off, 2 * BAND), :] += lax.dot_general(p.astype(BF16), doa, tn_dims, preferred_element_type=F32)
            return carry

        lax.fori_loop(0, nsub, sub, 0, unroll=True)

        @pl.when(c < nchunk - 1)
        def _():
            qn = qn_ref[...]
            don = don_ref[...]
            kl = kext_ref[pl.ds(lc, BAND), :]
            vl = vext_ref[pl.ds(lc, BAND), :]
            s = lax.dot_general(qn, kl, nt_dims, preferred_element_type=F32) * scale + bias_v[:, :BAND]
            p = jnp.where(band[:, :BAND], jnp.exp(s - lsen_ref[...]), 0.0)
            dp = lax.dot_general(don, vl, nt_dims, preferred_element_type=F32)
            ds = p * (dp - dsn_ref[...])
            dkext_ref[pl.ds(lc, BAND), :] += lax.dot_general(ds.astype(BF16), qn, tn_dims, preferred_element_type=F32) * scale
            dvext_ref[pl.ds(lc, BAND), :] += lax.dot_general(p.astype(BF16), don, tn_dims, preferred_element_type=F32)

        dk_ref[...] = dkext_ref[pl.ds(BAND, lc), :]
        dv_ref[...] = dvext_ref[pl.ds(BAND, lc), :]

    nsb = lc // BAND
    nblk = L // BAND
    nh = n_heads
    blk = lambda w, col0: pl.BlockSpec((lc, HEAD_DIM), lambda h, r, c: (c, r * w + col0 + h))
    prev = lambda w, col0: pl.BlockSpec((BAND, HEAD_DIM), lambda h, r, c: (jnp.maximum(c * nsb - 1, 0), r * w + col0 + h))
    nxt = lambda: pl.BlockSpec((BAND, HEAD_DIM), lambda h, r, c: (jnp.minimum((c + 1) * nsb, nblk - 1), r * nh + h))
    one = blk(nh, 0)
    dq3, dk3, dv3, dsb = pl.pallas_call(
        body, name=name, grid=(n_heads, dil, nchunk),
        in_specs=[one, blk(2 * nh, 0), blk(2 * nh, nh), one, one, one, prev(2 * nh, 0), prev(2 * nh, nh),
                  nxt(), nxt(), nxt(), nxt(), pl.BlockSpec((None, BAND, 2 * BAND), lambda h, r, c: (h, 0, 0))],
        out_specs=[one, one, one, pl.BlockSpec((None, BAND, 2 * BAND), lambda h, r, c: (h, 0, 0))],
        out_shape=[jax.ShapeDtypeStruct((L, dil * D), F32)] * 3 + [jax.ShapeDtypeStruct((n_heads, BAND, 2 * BAND), F32)],
        scratch_shapes=[pltpu.VMEM((lc + BAND, HEAD_DIM), BF16)] * 2 + [pltpu.VMEM((lc + BAND, HEAD_DIM), F32)] * 2,
        compiler_params=_cparams(("arbitrary", "arbitrary", "arbitrary")),
    )(q3, kv3, kv3, do3, lse3, dsum3, kv3, kv3, q3, do3, lse3, dsum3, bias)
    return dq3.reshape(T, D), dk3.reshape(T, D), dv3.reshape(T, D), dsb


def sum_branches(name, dqs, dks, dvs):
    T, D = dqs[0].shape
    tr = min(ROW_TILE, T)
    nb = len(dqs)

    def body(*refs):
        dq_refs, dk_refs, dv_refs = refs[:nb], refs[nb:2 * nb], refs[2 * nb:3 * nb]
        dq_ref, dkv_ref = refs[3 * nb:]
        add = lambda rs: functools.reduce(lambda x, y: x + y, [r[...] for r in rs])
        dq_ref[...] = add(dq_refs).astype(BF16)
        dkv_ref[:, pl.ds(0, D)] = add(dk_refs).astype(BF16)
        dkv_ref[:, pl.ds(D, D)] = add(dv_refs).astype(BF16)

    return pl.pallas_call(
        body, name=name, grid=(T // tr,), in_specs=[_row_spec(tr, D)] * (3 * nb),
        out_specs=[_row_spec(tr, D), _row_spec(tr, 2 * D)],
        out_shape=[jax.ShapeDtypeStruct((T, D), BF16), jax.ShapeDtypeStruct((T, 2 * D), BF16)],
        compiler_params=_cparams(("parallel",)),
    )(*dqs, *dks, *dvs)


def _divisor_tile(n, cap, mult):
    if n <= cap:
        return n
    t = cap - cap % mult
    while n % t:
        t -= mult
    return t


def _tile2(R, C):
    return _divisor_tile(R, 512, 8), _divisor_tile(C, 1024, 128)


def half_cast(name, dw, core):
    S, R, C = dw.shape
    hr = R // 2
    tr, tc = _tile2(hr, C)
    nrb = hr // tr

    def body(c_ref, x_ref, o_ref):
        o_ref[...] = x_ref[...].astype(BF16)

    return pl.pallas_call(
        body, name=name,
        grid_spec=pltpu.PrefetchScalarGridSpec(
            num_scalar_prefetch=1, grid=(S, nrb, C // tc),
            in_specs=[pl.BlockSpec((None, tr, tc), lambda s, i, j, c: (s, (1 - c[0]) * nrb + i, j))],
            out_specs=pl.BlockSpec((None, tr, tc), lambda s, i, j, c: (s, i, j))),
        out_shape=jax.ShapeDtypeStruct((S, hr, C), BF16),
        compiler_params=_cparams(("parallel", "parallel", "parallel")),
    )(core, dw)


def pair_sum(name, dw, recv, core):
    S, R, C = dw.shape
    hr = R // 2
    tr, tc = _tile2(hr, C)
    nrb = hr // tr

    def body(c_ref, x_ref, r_ref, p_ref, pbf_ref):
        p = x_ref[...] + r_ref[...].astype(F32)
        p_ref[...] = p
        pbf_ref[...] = p.astype(BF16)

    out = pl.BlockSpec((None, tr, tc), lambda s, i, j, c: (s, i, j))
    return pl.pallas_call(
        body, name=name,
        grid_spec=pltpu.PrefetchScalarGridSpec(
            num_scalar_prefetch=1, grid=(S, nrb, C // tc),
            in_specs=[pl.BlockSpec((None, tr, tc), lambda s, i, j, c: (s, c[0] * nrb + i, j)), out],
            out_specs=[out, out]),
        out_shape=[jax.ShapeDtypeStruct((S, hr, C), F32), jax.ShapeDtypeStruct((S, hr, C), BF16)],
        compiler_params=_cparams(("parallel", "parallel", "parallel")),
    )(core, dw, recv)


def chip_sum(name, p, recv, chip, core):
    S, hr, C = p.shape
    tr, tc = _tile2(hr, C)
    nrb = hr // tr

    def body(chip_ref, core_ref, p_ref, r_ref, o_ref):
        acc = p_ref[...]
        for t in range(N_CHIPS - 1):
            acc = acc + r_ref[t].astype(F32)
        o_ref[...] = acc

    return pl.pallas_call(
        body, name=name,
        grid_spec=pltpu.PrefetchScalarGridSpec(
            num_scalar_prefetch=2, grid=(nrb, C // tc),
            in_specs=[pl.BlockSpec((None, tr, tc), lambda i, j, s, c: (s[0], i, j)),
                      pl.BlockSpec((N_CHIPS - 1, tr, tc), lambda i, j, s, c: (0, i, j))],
            out_specs=pl.BlockSpec((tr, tc), lambda i, j, s, c: (c[0] * nrb + i, j))),
        out_shape=jax.ShapeDtypeStruct((2 * hr, C), F32),
        compiler_params=_cparams(("parallel", "parallel")),
    )(chip, core, p, recv)


def adamw(name, w, g, m, v):
    R, C = w.shape
    tr, tc = _tile2(R, C)
    c1 = 1.0 - ADAM_B1 ** ADAM_STEP
    c2 = 1.0 - ADAM_B2 ** ADAM_STEP

    def body(w_ref, g_ref, m_ref, v_ref, d_ref, nm_ref, nv_ref):
        gv = g_ref[...]
        nm = ADAM_B1 * m_ref[...] + (1.0 - ADAM_B1) * gv
        nv = ADAM_B2 * v_ref[...] + (1.0 - ADAM_B2) * (gv * gv)
        nm_ref[...] = nm
        nv_ref[...] = nv
        d_ref[...] = -ADAM_LR * ((nm / c1) / (jnp.sqrt(nv / c2) + ADAM_EPS) + ADAM_WD * w_ref[...])

    spec = pl.BlockSpec((tr, tc), lambda i, j: (i, j))
    return pl.pallas_call(
        body, name=name, grid=(R // tr, C // tc), in_specs=[spec] * 4, out_specs=[spec] * 3,
        out_shape=[jax.ShapeDtypeStruct((R, C), F32)] * 3,
        compiler_params=_cparams(("parallel", "parallel")),
    )(w, g, m, v)


def sum_devices(name, gathered):
    n, R, C = gathered.shape

    def body(x_ref, o_ref):
        acc = x_ref[0]
        for d in range(1, n):
            acc = acc + x_ref[d]
        o_ref[...] = acc

    return pl.pallas_call(
        body, name=name, in_specs=[pl.BlockSpec(memory_space=pltpu.VMEM)],
        out_specs=pl.BlockSpec(memory_space=pltpu.VMEM),
        out_shape=jax.ShapeDtypeStruct((R, C), F32),
    )(gathered)


def _place():
    x, y, c = lax.axis_index("x"), lax.axis_index("y"), lax.axis_index("c")
    return x, y, c


def _other_chips(x, y):
    return [(1 - x, y), (x, 1 - y), (1 - x, 1 - y)]


def all_gather8(name, block):
    R, C = block.shape

    def body(x_ref, out_ref, send_sems, recv_sems, local_sem):
        x, y, c = _place()
        me, sibling = (x, y, c), (x, y, 1 - c)
        chips = _other_chips(x, y)

        def rows(px, py, pc):
            return out_ref.at[4 * px + 2 * py + pc]

        def copy(k, blk, to, src=None):
            return pltpu.make_async_remote_copy(
                src_ref=rows(*blk) if src is None else src, dst_ref=rows(*blk),
                send_sem=send_sems.at[k], recv_sem=recv_sems.at[k], device_id=to, device_id_type=MESH)

        mine = pltpu.make_async_copy(x_ref, rows(*me), local_sem)
        mine.start()
        first = [copy(0, me, sibling, src=x_ref)]
        first += [copy(1 + j, me, (*chip, c), src=x_ref) for j, chip in enumerate(chips)]
        for cp in first:
            cp.start()
        passed = [copy(4 + j, (*chip, c), sibling) for j, chip in enumerate(chips)]
        for j, chip in enumerate(chips):
            copy(1 + j, (*chip, c), me).wait_recv()
            passed[j].start()
        copy(0, sibling, me).wait_recv()
        for j, chip in enumerate(chips):
            copy(4 + j, (*chip, 1 - c), me).wait_recv()
        for cp in first + passed:
            cp.wait_send()
        mine.wait()

    return pl.pallas_call(
        body, name=name, out_shape=jax.ShapeDtypeStruct((N_DEV, R, C), block.dtype),
        in_specs=[pl.BlockSpec(memory_space=pltpu.VMEM)], out_specs=pl.BlockSpec(memory_space=pltpu.VMEM),
        scratch_shapes=[pltpu.SemaphoreType.DMA((7,)), pltpu.SemaphoreType.DMA((7,)), pltpu.SemaphoreType.DMA],
    )(block)


_HBM = pl.BlockSpec(memory_space=pltpu.HBM)
_SEM = pl.BlockSpec(memory_space=pltpu.SEMAPHORE)
_DATAFLOW = pltpu.SideEffectType.DATAFLOW_SIDE_EFFECTING


def _in_hbm(a):
    return pltpu.with_memory_space_constraint(a, pltpu.HBM)


def split_start(name, srcs, lands, n_sem, plan):
    ns, nl = len(srcs), len(lands)

    def body(*refs):
        src, land = refs[:ns], refs[ns:ns + nl]
        send_sems, recv_sems = refs[ns + nl], refs[ns + nl + 1]
        token = refs[-1]
        outgoing, _ = plan(src, land, send_sems, recv_sems)
        for cp in outgoing:
            cp.start()
        token[...] = jnp.zeros_like(token)

    bufs = list(srcs) + list(lands)
    res = pl.pallas_call(
        body, name=name,
        out_shape=(pltpu.SemaphoreType.DMA((n_sem,)), pltpu.SemaphoreType.DMA((n_sem,)),
                   *[pltpu.HBM(b.shape, b.dtype) for b in bufs], jax.ShapeDtypeStruct((8, 128), F32)),
        in_specs=[_HBM] * (ns + nl),
        out_specs=(_SEM, _SEM, *[_HBM] * (ns + nl), pl.BlockSpec(memory_space=pltpu.VMEM)),
        input_output_aliases={i: 2 + i for i in range(ns + nl)},
        compiler_params=pltpu.CompilerParams(has_side_effects=_DATAFLOW),
    )(*[_in_hbm(b) for b in bufs])
    return res[0], res[1], list(res[2:2 + ns]), list(res[2 + ns:2 + ns + nl]), res[-1]


def split_wait(name, started, after, plan):
    send_sems, recv_sems, srcs, lands, _ = started
    ns, nl = len(srcs), len(lands)

    def body(*refs):
        src, land = refs[:ns], refs[ns:ns + nl]
        send, recv = refs[ns + nl], refs[ns + nl + 1]
        outgoing, incoming = plan(src, land, send, recv)
        for cp in outgoing:
            cp.wait_send()
        for cp in incoming:
            cp.wait_recv()

    bufs = list(srcs) + list(lands)
    res = pl.pallas_call(
        body, name=name,
        out_shape=tuple(pltpu.HBM(b.shape, b.dtype) for b in bufs),
        in_specs=[_HBM] * (ns + nl) + [_SEM, _SEM, pl.BlockSpec(memory_space=pl.ANY)],
        out_specs=tuple([_HBM] * (ns + nl)),
        input_output_aliases={i: i for i in range(ns + nl)},
        compiler_params=pltpu.CompilerParams(has_side_effects=_DATAFLOW),
    )(*bufs, send_sems, recv_sems, after)
    return list(res[ns:])


def _rcopy(src, dst, send_sems, ks, recv_sems, kr, device):
    return pltpu.make_async_remote_copy(src_ref=src, dst_ref=dst, send_sem=send_sems.at[ks], recv_sem=recv_sems.at[kr],
                                        device_id=device, device_id_type=MESH)


def _half_rows(ref, h):
    hr = ref.shape[0] // 2
    return ref.at[pl.ds(h * hr, hr)]


def _gather_plan(src, land, send_sems, recv_sems):
    x, y, c = _place()
    me_chip = 2 * x + y
    chips = _other_chips(x, y)
    outgoing, incoming = [], []
    for w, buf in enumerate(land):
        mine = _half_rows(buf.at[me_chip], c)
        for t, chip in enumerate(chips):
            slot = 2 * chip[0] + chip[1]
            for cc in range(2):
                outgoing.append(_rcopy(mine, mine, send_sems, 6 * w + 2 * t + cc, recv_sems, 6 * w + 2 * t + c, (*chip, cc)))
                theirs = _half_rows(buf.at[slot], cc)
                incoming.append(_rcopy(theirs, theirs, send_sems, 6 * w + 2 * t + cc, recv_sems, 6 * w + 2 * t + cc, (*chip, cc)))
    return outgoing, incoming


def _swap_plan(src, land, send_sems, recv_sems):
    x, y, c = _place()
    cp = _rcopy(src[0], land[0], send_sems, 0, recv_sems, 0, (x, y, 1 - c))
    return [cp], [cp]


def _scatter_plan(src, land, send_sems, recv_sems):
    x, y, c = _place()
    cps = [_rcopy(src[0].at[2 * chip[0] + chip[1]], land[0].at[t], send_sems, t, recv_sems, t, (*chip, c))
           for t, chip in enumerate(_other_chips(x, y))]
    return cps, cps


def _share_plan(src, land, send_sems, recv_sems):
    x, y, c = _place()
    mine, theirs = _half_rows(land[0], c), _half_rows(land[0], 1 - c)
    return ([_rcopy(mine, mine, send_sems, 0, recv_sems, 0, (x, y, 1 - c))],
            [_rcopy(theirs, theirs, send_sems, 0, recv_sems, 0, (x, y, 1 - c))])


def place_shard(name, shard, chip, deps=()):
    R, C = shard.shape
    tr, tc = _tile2(R, C)

    def body(chip_ref, x_ref, *rest):
        rest[-1][...] = x_ref[...].astype(BF16)

    return pl.pallas_call(
        body, name=name,
        grid_spec=pltpu.PrefetchScalarGridSpec(
            num_scalar_prefetch=1, grid=(R // tr, C // tc),
            in_specs=[pl.BlockSpec((tr, tc), lambda i, j, s: (i, j))]
                     + [pl.BlockSpec(d.shape, lambda i, j, s: (0, 0)) for d in deps],
            out_specs=pl.BlockSpec((None, tr, tc), lambda i, j, s: (s[0], i, j))),
        out_shape=jax.ShapeDtypeStruct((N_CHIPS, R, C), BF16),
        compiler_params=_cparams(("parallel", "parallel")),
    )(chip, shard, *deps)


class GradExchange:
    SCATTER_TICKS = 2

    def __init__(self, chip1, core, shard, mom, vel):
        self.chip1, self.core, self.shard, self.mom, self.vel = chip1, core, shard, mom, vel
        self.inflight, self.tokens, self.results = [], [], {}

    def take_deps(self):
        deps, self.tokens = self.tokens, []
        return deps

    def _start(self, name, srcs, lands, n_sem, plan):
        started = split_start(name, srcs, lands, n_sem, plan)
        self.tokens.append(started[-1])
        return started

    def add(self, n, dw):
        S, R, C = dw.shape
        to_sibling = half_cast("rs_cast_" + n, dw, self.core)
        started = self._start("rs_swap_start_" + n, [to_sibling], [lax.empty((S, R // 2, C), BF16)], 1, _swap_plan)
        self.inflight.append(dict(n=n, dw=dw, stage=0, started=started, ticks=0))

    def tick(self, after):
        for it in self.inflight:
            n = it["n"]
            if it["stage"] == 0:
                (recv,) = split_wait("rs_swap_wait_" + n, it["started"], after, _swap_plan)
                p, pbf = pair_sum("rs_pair_sum_" + n, it["dw"], recv, self.core)
                S, hr, C = pbf.shape
                it.update(stage=1, p=p, ticks=0,
                          started=self._start("rs_scatter_start_" + n, [pbf], [lax.empty((N_CHIPS - 1, hr, C), BF16)], 3, _scatter_plan))
            elif it["stage"] == 1:
                it["ticks"] += 1
                if it["ticks"] >= self.SCATTER_TICKS:
                    (recv,) = split_wait("rs_scatter_wait_" + n, it["started"], after, _scatter_plan)
                    half = chip_sum("rs_chip_sum_" + n, it["p"], recv, self.chip1, self.core)
                    it.update(stage=2, started=self._start("rs_share_start_" + n, [], [half], 1, _share_plan))
            elif it["stage"] == 2:
                (grad,) = split_wait("rs_share_wait_" + n, it["started"], after, _share_plan)
                self.results[n] = (grad,) + tuple(adamw("adamw_" + n, self.shard[n], grad, self.mom[n], self.vel[n]))
                it["stage"] = 3
        self.inflight = [it for it in self.inflight if it["stage"] < 3]

    def flush(self, after):
        while self.inflight:
            self.tick(after)


def _pack(arrs):
    parts = []
    for a in arrs:
        flat = a.reshape(-1).astype(F32)
        n = flat.shape[0]
        padded = -(-n // 1024) * 1024
        parts.append(jnp.pad(flat, (0, padded - n)).reshape(padded // 128, 128))
    return jnp.concatenate(parts, axis=0)


def _unpack(buf, shapes):
    out, row = [], 0
    for shp in shapes:
        n = int(np.prod(shp))
        rows = -(-n // 1024) * 8
        out.append(buf[row:row + rows].reshape(-1)[:n].reshape(shp))
        row += rows
    return out


def _bias_epi(acc, b):
    return (acc + b,)


def local_step(x, target, W, P, ex, first_deps=()):
    T, D = x.shape
    g = {}
    plain = lambda acc: (acc,)

    (h1,) = mm_nn("pw1_fwd", x, W("pw1", x), "col", _bias_epi, [F32],
                  extras=[(P["pw1_b"], "row")] + [(d, "dep") for d in first_deps])
    u, cpre, s = conv_fwd("conv_fwd", h1, P["dw_w"], P["dw_b"], P["cln_g"], P["cln_b"])
    (mix0,) = mm_nn("pw2_fwd", s, W("pw2", s), "row", _bias_epi, [F32], extras=[(P["pw2_b"], "row")])
    ln = [None] * 4
    gam = [P["ln_mix_g"][0:1], P["ln_mlp_g"][0:1], P["ln_mix_g"][1:2], P["ln_mlp_g"][1:2]]
    bet = [P["ln_mix_b"][0:1], P["ln_mlp_b"][0:1], P["ln_mix_b"][1:2], P["ln_mlp_b"][1:2]]
    ln[0] = ln_fwd("ln0_fwd", mix0, x)(gam[0], bet[0])

    def mlp_fwd(tag, i_ln, n1, n2):
        xhat, rstd, xbf = ln[i_ln]

        def up_epi(acc):
            r = jnp.maximum(acc, 0.0)
            return r * r, r

        hid, relu = mm_nn(tag + "_up", xbf, W(n1, xbf), "col", up_epi, [BF16, BF16])
        (mlp,) = mm_nn(tag + "_down", hid, W(n2, hid), "row", plain, [F32])
        ln[i_ln + 1] = ln_fwd(tag + "_ln", mlp, xhat, gam[i_ln], bet[i_ln])(gam[i_ln + 1], bet[i_ln + 1])
        return hid, relu

    hid0 = mlp_fwd("mlp0", 0, "w1_0", "w2_0")

    x2bf = ln[1][2]
    (kv,) = mm_nn("kv_fwd", x2bf, W("kv", x2bf), "col", plain, [BF16])
    (q,) = mm_nn("q_fwd", x2bf, W("wq", kv), "row", plain, [BF16])
    biases = [bias_expand("bias_d%d" % d, P["rel_bias"], d) for _, d in BRANCHES]
    outs, lses = [], []
    for (win, d), b in zip(BRANCHES, biases):
        assert win // d == BAND and (T // d) % BAND == 0
        o_b, l_b = attn_fwd("attn_fwd_d%d" % d, q, kv, b, d)
        outs.append(o_b)
        lses.append(l_b)
    o, obf, lse = attn_merge("attn_merge", outs, lses)
    (attn,) = mm_nn("wo_fwd", obf, W("wo", obf), "row", plain, [F32])
    ln[2] = ln_fwd("ln2_fwd", attn, ln[1][0], gam[1], bet[1])(gam[2], bet[2])
    hid1 = mlp_fwd("mlp1", 2, "w1_1", "w2_1")

    dr3, dr3bf, g["ln_mlp_g1"], g["ln_mlp_b1"], _, loss_sum = ln_bwd(
        "ln3_bwd", ln[3][0], ln[3][1], gam[3], target=target, beta=bet[3])

    def dw_step(name, wname, a, cot, axis):
        dw = mm_tn(name, a, cot, W(wname, a).shape, axis, deps=ex.take_deps())
        ex.tick(dw)
        ex.add(wname, dw)

    def dx_step(name, cot, wname, axis, epilogue, out_dtype, extras):
        deps = [(d, "dep") for d in ex.take_deps()]
        (out,) = mm_nt(name, cot, W(wname, cot), axis, epilogue, [out_dtype], extras=list(extras) + deps)
        ex.tick(out)
        return out

    def mlp_bwd(tag, i_ln, n1, n2, hid_relu, dr, drbf):
        xbf = ln[i_ln][2]
        hid, relu = hid_relu
        dw_step(tag + "_dw2", n2, hid, drbf, "row")
        dp = dx_step(tag + "_dhid", drbf, n2, "row", lambda acc, r: (acc * (2.0 * r.astype(F32)),), BF16, [(relu, "tile")])
        dw_step(tag + "_dw1", n1, xbf, dp, "col")
        return dx_step(tag + "_dx", dp, n1, "col", lambda acc, e: (acc + ALPHA * e,), F32, [(dr, "tile")])

    dx3 = mlp_bwd("mlp1", 2, "w1_1", "w2_1", hid1, dr3, dr3bf)
    dr2, dr2bf, g["ln_mix_g1"], g["ln_mix_b1"], _ = ln_bwd("ln2_bwd", ln[2][0], ln[2][1], gam[2], dy=dx3)
    dw_step("wo_dw", "wo", obf, dr2bf, "row")
    do = dx_step("wo_dx", dr2bf, "wo", "row", plain, F32, [])
    dobf, dsum = attn_bwd_prep("attn_bwd_prep", do, o)
    dqs, dks, dvs, dsbs = [], [], [], []
    for (win, d), b in zip(BRANCHES, biases):
        dq_b, dk_b, dv_b, dsb = attn_bwd("attn_bwd_d%d" % d, q, kv, dobf, lse, dsum, b, d)
        dqs.append(dq_b)
        dks.append(dk_b)
        dvs.append(dv_b)
        dsbs.append(dsb)
    g["rel_bias"] = relbias_grad("relbias_grad", dsbs)[:, 0, :REL_BUCKETS].T
    dq, dkv = sum_branches("attn_bwd_sum", dqs, dks, dvs)
    dw_step("wq_dw", "wq", x2bf, dq, "row")
    dw_step("kv_dw", "kv", x2bf, dkv, "col")
    dx2a = dx_step("wq_dx", dq, "wq", "row", lambda acc, e: (acc + ALPHA * e,), F32, [(dr2, "tile")])
    dx2 = dx_step("kv_dx", dkv, "kv", "col", lambda acc, e: (acc + e,), F32, [(dx2a, "tile")])

    dr1, dr1bf, g["ln_mlp_g0"], g["ln_mlp_b0"], _ = ln_bwd("ln1_bwd", ln[1][0], ln[1][1], gam[1], dy=dx2)
    dx1 = mlp_bwd("mlp0", 0, "w1_0", "w2_0", hid0, dr1, dr1bf)
    dr0, dr0bf, g["ln_mix_g0"], g["ln_mix_b0"], g["pw2_b"] = ln_bwd("ln0_bwd", ln[0][0], ln[0][1], gam[0], dy=dx1)

    dw_step("pw2_dw", "pw2", s, dr0bf, "row")
    ds = dx_step("pw2_dx", dr0bf, "pw2", "row", plain, F32, [])
    dc, g["cln_g"], g["cln_b"], g["dw_b"] = conv_bwd_ln("conv_bwd_ln", ds, cpre, P["cln_g"], P["cln_b"])
    dh1, g["pw1_b"], g["dw_w"] = conv_bwd_taps("conv_bwd_taps", dc, u, h1, P["dw_w"])
    dw_step("pw1_dw", "pw1", x, dh1, "col")
    dx = dx_step("pw1_dx", dh1, "pw1", "col", lambda acc, e: (acc + ALPHA * e,), F32, [(dr0, "tile")])
    ex.flush(dx)
    return loss_sum, dx, g


BIG = ("pw1", "pw2", "w1_0", "w2_0", "kv", "wq", "wo", "w1_1", "w2_1")


def kernel(x, conv_pw1_w, conv_pw1_b, conv_dw_w, conv_dw_b, conv_ln_g, conv_ln_b, conv_pw2_w, conv_pw2_b, w_kv, attn_wq, attn_wo, rel_bias, mlp_w1, mlp_w2, ln_mix_g, ln_mix_b, ln_mlp_g, ln_mlp_b, loss_target, m_conv_pw1_w, m_conv_pw1_b, m_conv_dw_w, m_conv_dw_b, m_conv_ln_g, m_conv_ln_b, m_conv_pw2_w, m_conv_pw2_b, m_w_kv, m_attn_wq, m_attn_wo, m_rel_bias, m_mlp_w1, m_mlp_w2, m_ln_mix_g, m_ln_mix_b, m_ln_mlp_g, m_ln_mlp_b, v_conv_pw1_w, v_conv_pw1_b, v_conv_dw_w, v_conv_dw_b, v_conv_ln_g, v_conv_ln_b, v_conv_pw2_w, v_conv_pw2_b, v_w_kv, v_attn_wq, v_attn_wo, v_rel_bias, v_mlp_w1, v_mlp_w2, v_ln_mix_g, v_ln_mix_b, v_ln_mlp_g, v_ln_mlp_b):
    _, T, D = x.shape
    xi, yi, ci = _place()
    chip = 2 * xi + yi
    core = jnp.reshape(ci, (1,)).astype(jnp.int32)
    chip1 = jnp.reshape(chip, (1,)).astype(jnp.int32)

    def two_d(a):
        return a.reshape(a.shape[-2:])

    shard = {"pw1": two_d(conv_pw1_w), "pw2": two_d(conv_pw2_w), "kv": w_kv, "wq": two_d(attn_wq), "wo": two_d(attn_wo),
             "w1_0": mlp_w1[0], "w1_1": mlp_w1[1], "w2_0": mlp_w2[0], "w2_1": mlp_w2[1]}
    mom = {"pw1": two_d(m_conv_pw1_w), "pw2": two_d(m_conv_pw2_w), "kv": m_w_kv, "wq": two_d(m_attn_wq), "wo": two_d(m_attn_wo),
           "w1_0": m_mlp_w1[0], "w1_1": m_mlp_w1[1], "w2_0": m_mlp_w2[0], "w2_1": m_mlp_w2[1]}
    vel = {"pw1": two_d(v_conv_pw1_w), "pw2": two_d(v_conv_pw2_w), "kv": v_w_kv, "wq": two_d(v_attn_wq), "wo": two_d(v_attn_wo),
           "w1_0": v_mlp_w1[0], "w1_1": v_mlp_w1[1], "w2_0": v_mlp_w2[0], "w2_1": v_mlp_w2[1]}

    started = {}
    for n in BIG:
        deps = [started[prev][-1] for prev in list(started)[-1:]]
        started[n] = split_start("gather_start_" + n, [], [place_shard("place_" + n, shard[n], chip1, deps)], 6, _gather_plan)
    gathered = {}

    def W(n, after):
        if n not in gathered:
            (gathered[n],) = split_wait("gather_wait_" + n, started[n], after, _gather_plan)
        return gathered[n]

    sharded_small = [conv_pw1_b, conv_dw_w[0], conv_dw_b, conv_ln_g, conv_ln_b, conv_pw2_b]
    sh_shapes = [a.shape for a in sharded_small]
    small_all = all_gather8("gather_small", _pack(sharded_small))
    per_chip = [_unpack(small_all[2 * j], sh_shapes) for j in range(N_CHIPS)]
    full = [jnp.concatenate([per_chip[j][i] for j in range(N_CHIPS)], axis=-1) for i in range(len(sharded_small))]
    P = dict(pw1_b=full[0], dw_w=full[1], dw_b=full[2], cln_g=full[3], cln_b=full[4], pw2_b=full[5],
             rel_bias=rel_bias, ln_mix_g=ln_mix_g, ln_mix_b=ln_mix_b, ln_mlp_g=ln_mlp_g, ln_mlp_b=ln_mlp_b)

    ex = GradExchange(chip1, core, shard, mom, vel)
    loss_sum, dx, g = local_step(x.reshape(T, D), loss_target.reshape(T, D), W, P, ex,
                                 first_deps=[started[n][-1] for n in BIG])
    loss = (0.5 / D) * lax.psum(loss_sum[0, 0], ("x", "y", "c"))
    grads_big = {n: ex.results[n][0] for n in BIG}

    small_names = ["pw1_b", "dw_w", "dw_b", "cln_g", "cln_b", "pw2_b", "rel_bias",
                   "ln_mix_g0", "ln_mix_g1", "ln_mix_b0", "ln_mix_b1", "ln_mlp_g0", "ln_mlp_g1", "ln_mlp_b0", "ln_mlp_b1"]
    small_grads = [g[n] for n in small_names]
    sg_shapes = [a.shape for a in small_grads]
    summed = sum_devices("small_grad_sum", all_gather8("gather_small_grads", _pack(small_grads)))
    sg = dict(zip(small_names, _unpack(summed, sg_shapes)))

    def my_cols(a, width):
        return lax.dynamic_slice_in_dim(a, chip * width, width, axis=a.ndim - 1)

    small_g = [my_cols(sg["pw1_b"], conv_pw1_b.shape[-1]),
               my_cols(sg["dw_w"], conv_dw_w.shape[-1])[None],
               my_cols(sg["dw_b"], conv_dw_b.shape[-1]), my_cols(sg["cln_g"], conv_ln_g.shape[-1]),
               my_cols(sg["cln_b"], conv_ln_b.shape[-1]), my_cols(sg["pw2_b"], conv_pw2_b.shape[-1]),
               sg["rel_bias"],
               jnp.concatenate([sg["ln_mix_g0"], sg["ln_mix_g1"]], axis=0),
               jnp.concatenate([sg["ln_mix_b0"], sg["ln_mix_b1"]], axis=0),
               jnp.concatenate([sg["ln_mlp_g0"], sg["ln_mlp_g1"]], axis=0),
               jnp.concatenate([sg["ln_mlp_b0"], sg["ln_mlp_b1"]], axis=0)]
    small_w = [conv_pw1_b, conv_dw_w, conv_dw_b, conv_ln_g, conv_ln_b, conv_pw2_b, rel_bias, ln_mix_g, ln_mix_b, ln_mlp_g, ln_mlp_b]
    small_m = [m_conv_pw1_b, m_conv_dw_w, m_conv_dw_b, m_conv_ln_g, m_conv_ln_b, m_conv_pw2_b, m_rel_bias, m_ln_mix_g, m_ln_mix_b, m_ln_mlp_g, m_ln_mlp_b]
    small_v = [v_conv_pw1_b, v_conv_dw_w, v_conv_dw_b, v_conv_ln_g, v_conv_ln_b, v_conv_pw2_b, v_rel_bias, v_ln_mix_g, v_ln_mix_b, v_ln_mlp_g, v_ln_mlp_b]
    sw_shapes = [a.shape for a in small_w]
    small_g = [a.reshape(s) for a, s in zip(small_g, sw_shapes)]
    upd_small = adamw("adamw_small", _pack(small_w), _pack(small_g), _pack(small_m), _pack(small_v))
    sd, snm, snv = (_unpack(b, sw_shapes) for b in upd_small)

    def big_out(tree):
        return dict(pw1=tree["pw1"][None], pw2=tree["pw2"][None], kv=tree["kv"], wq=tree["wq"][None], wo=tree["wo"][None],
                    w1=jnp.stack([tree["w1_0"], tree["w1_1"]]), w2=jnp.stack([tree["w2_0"], tree["w2_1"]]))

    def ordered(big, small):
        return [big["pw1"], small[0], small[1], small[2], small[3], small[4], big["pw2"], small[5], big["kv"], big["wq"],
                big["wo"], small[6], big["w1"], big["w2"], small[7], small[8], small[9], small[10]]

    grads = ordered(big_out(grads_big), small_g)
    deltas = ordered(big_out({n: ex.results[n][1] for n in BIG}), sd)
    new_m = ordered(big_out({n: ex.results[n][2] for n in BIG}), snm)
    new_v = ordered(big_out({n: ex.results[n][3] for n in BIG}), snv)
    return (loss, dx.reshape(1, T, D), *grads, *deltas, *new_m, *new_v)
```

```python
import functools
import math

import numpy as np
import jax
import jax.numpy as jnp
from jax import lax
from jax.experimental import pallas as pl
from jax.experimental.pallas import tpu as pltpu

F32 = jnp.float32
BF16 = jnp.bfloat16

HEAD_DIM = 128
BAND = 128
BRANCHES = ((128, 1), (512, 4), (2048, 16))
CONV_WIDTH = 31
CONV_HALO = 32
REL_BUCKETS = 32
REL_MAX_DIST = 2048
DEPTH = 2
ALPHA = (2 * DEPTH) ** 0.25
LN_EPS = 1e-5
ADAM_LR, ADAM_B1, ADAM_B2, ADAM_EPS, ADAM_WD, ADAM_STEP = 0.001, 0.9, 0.999, 1e-08, 0.01, 10

N_CHIPS = 4
N_DEV = 8
MESH = pl.DeviceIdType.MESH
VMEM_LIMIT_BYTES = 56 * 1024 * 1024
MM_TM, MM_TN, MM_TK = 1024, 1024, 2048
ROW_TILE = 256
CONV_TILE = 128
NEG_BIG = -1e30


def _cparams(sem):
    return pltpu.CompilerParams(dimension_semantics=sem, vmem_limit_bytes=VMEM_LIMIT_BYTES)


def _sigmoid(x):
    return 1.0 / (1.0 + jnp.exp(-x))


def _wspec(wshape, axis, br, bc, rsel, csel):
    _, R, C = wshape
    if axis == "col":
        nb = C // bc
        assert nb * bc == C, (wshape, bc)
        return pl.BlockSpec((None, br, bc), lambda *g: (csel(*g) // nb, rsel(*g), csel(*g) % nb))
    nb = R // br
    assert nb * br == R, (wshape, br)
    return pl.BlockSpec((None, br, bc), lambda *g: (rsel(*g) // nb, rsel(*g) % nb, csel(*g)))


def _full_dims(wshape, axis):
    _, R, C = wshape
    return (R, N_CHIPS * C) if axis == "col" else (N_CHIPS * R, C)


def _mm_body(nk, kinds, n_out, dims, epilogue):
    n_extra = len(kinds)

    def body(*refs):
        a_ref, b_ref = refs[0], refs[1]
        extra = [r for r, kind in zip(refs[2:2 + n_extra], kinds) if kind != "dep"]
        outs = refs[2 + n_extra:2 + n_extra + n_out]
        part = lax.dot_general(a_ref[...].astype(BF16), b_ref[...].astype(BF16), (dims, ((), ())),
                               preferred_element_type=F32)
        if nk == 1:
            res = epilogue(part, *[e[...] for e in extra])
            for r, o in zip(res, outs):
                o[...] = r.astype(o.dtype)
            return
        acc_ref = refs[2 + n_extra + n_out]
        k = pl.program_id(2)

        @pl.when(k == 0)
        def _():
            acc_ref[...] = part

        @pl.when(k > 0)
        def _():
            acc_ref[...] += part

        @pl.when(k == nk - 1)
        def _():
            res = epilogue(acc_ref[...], *[e[...] for e in extra])
            for r, o in zip(res, outs):
                o[...] = r.astype(o.dtype)
    return body


def _extra_specs(extras, tm, tn):
    specs = []
    for arr, kind in extras:
        if kind == "tile":
            specs.append(pl.BlockSpec((tm, tn), lambda i, j, k: (i, j)))
        elif kind == "dep":
            specs.append(pl.BlockSpec(arr.shape, lambda i, j, k: (0, 0)))
        else:
            specs.append(pl.BlockSpec((1, tn), lambda i, j, k: (0, j)))
    return specs


def mm_nn(name, a, w, axis, epilogue, out_dtypes, extras=()):
    M, K = a.shape
    Kw, N = _full_dims(w.shape, axis)
    assert K == Kw
    tm, tn, tk = min(MM_TM, M), min(MM_TN, N), min(MM_TK, K)
    if axis == "col":
        tn = min(tn, w.shape[2])
    else:
        tk = min(tk, w.shape[1])
    nk = K // tk
    in_specs = [pl.BlockSpec((tm, tk), lambda i, j, k: (i, k)),
                _wspec(w.shape, axis, tk, tn, lambda i, j, k: k, lambda i, j, k: j)]
    in_specs += _extra_specs(extras, tm, tn)
    body = _mm_body(nk, [kind for _, kind in extras], len(out_dtypes), ((1,), (0,)), epilogue)
    return pl.pallas_call(
        body, name=name, grid=(M // tm, N // tn, nk), in_specs=in_specs,
        out_specs=[pl.BlockSpec((tm, tn), lambda i, j, k: (i, j)) for _ in out_dtypes],
        out_shape=[jax.ShapeDtypeStruct((M, N), d) for d in out_dtypes],
        scratch_shapes=[pltpu.VMEM((tm, tn), F32)] if nk > 1 else [],
        compiler_params=_cparams(("parallel", "parallel", "arbitrary")),
    )(a, w, *[e for e, _ in extras])


def mm_nt(name, g, w, axis, epilogue, out_dtypes, extras=()):
    M, N = g.shape
    K, Nw = _full_dims(w.shape, axis)
    assert N == Nw
    tm, tn, tk = min(MM_TM, M), min(MM_TN, K), min(MM_TK, N)
    if axis == "col":
        tk = min(tk, w.shape[2])
    else:
        tn = min(tn, w.shape[1])
    nk = N // tk
    in_specs = [pl.BlockSpec((tm, tk), lambda i, j, k: (i, k)),
                _wspec(w.shape, axis, tn, tk, lambda i, j, k: j, lambda i, j, k: k)]
    in_specs += _extra_specs(extras, tm, tn)
    body = _mm_body(nk, [kind for _, kind in extras], len(out_dtypes), ((1,), (1,)), epilogue)
    return pl.pallas_call(
        body, name=name, grid=(M // tm, K // tn, nk), in_specs=in_specs,
        out_specs=[pl.BlockSpec((tm, tn), lambda i, j, k: (i, j)) for _ in out_dtypes],
        out_shape=[jax.ShapeDtypeStruct((M, K), d) for d in out_dtypes],
        scratch_shapes=[pltpu.VMEM((tm, tn), F32)] if nk > 1 else [],
        compiler_params=_cparams(("parallel", "parallel", "arbitrary")),
    )(g, w, *[e for e, _ in extras])


def mm_tn(name, a, g, wshape, axis, deps=()):
    M, K = a.shape
    Mg, N = g.shape
    assert M == Mg and (K, N) == _full_dims(wshape, axis)
    tm, tn, tk = min(MM_TM, K), min(MM_TN, N), min(MM_TK, M)
    if axis == "col":
        tn = min(tn, wshape[2])
    else:
        tm = min(tm, wshape[1])
    nk = M // tk
    body = _mm_body(nk, ["dep"] * len(deps), 1, ((0,), (0,)), lambda acc: (acc,))
    return pl.pallas_call(
        body, name=name, grid=(K // tm, N // tn, nk),
        in_specs=[pl.BlockSpec((tk, tm), lambda i, j, k: (k, i)),
                  pl.BlockSpec((tk, tn), lambda i, j, k: (k, j))] + _extra_specs([(d, "dep") for d in deps], tm, tn),
        out_specs=[_wspec(wshape, axis, tm, tn, lambda i, j, k: i, lambda i, j, k: j)],
        out_shape=[jax.ShapeDtypeStruct(wshape, F32)],
        scratch_shapes=[pltpu.VMEM((tm, tn), F32)] if nk > 1 else [],
        compiler_params=_cparams(("parallel", "parallel", "arbitrary")),
    )(a, g, *deps)[0]


def _row_spec(tr, width):
    return pl.BlockSpec((tr, width), lambda i: (i, 0))


def _vec_spec(width):
    return pl.BlockSpec((1, width), lambda i: (0, 0))


def _fold8(x):
    r, d = x.shape
    return jnp.sum(x.reshape(r // 8, 8, d), axis=0)


def ln_fwd(name, f, prev, prev_g=None, prev_b=None):
    T, D = f.shape
    tr = min(ROW_TILE, T)
    affine = prev_g is not None

    def body(*refs):
        if affine:
            f_ref, p_ref, pg_ref, pb_ref, g_ref, b_ref, xhat_ref, rstd_ref, xbf_ref = refs
            xprev = p_ref[...] * pg_ref[...] + pb_ref[...]
        else:
            f_ref, p_ref, g_ref, b_ref, xhat_ref, rstd_ref, xbf_ref = refs
            xprev = p_ref[...]
        r = ALPHA * xprev + f_ref[...]
        mu = jnp.mean(r, axis=-1, keepdims=True)
        cen = r - mu
        var = jnp.mean(cen * cen, axis=-1, keepdims=True)
        rstd = lax.rsqrt(var + LN_EPS)
        xhat = cen * rstd
        xhat_ref[...] = xhat
        rstd_ref[...] = rstd
        xbf_ref[...] = (xhat * g_ref[...] + b_ref[...]).astype(BF16)

    def call(g, b):
        ins = [f, prev] + ([prev_g, prev_b] if affine else []) + [g, b]
        specs = [_row_spec(tr, D), _row_spec(tr, D)] + ([_vec_spec(D)] * 2 if affine else []) + [_vec_spec(D)] * 2
        return pl.pallas_call(
            body, name=name, grid=(T // tr,), in_specs=specs,
            out_specs=[_row_spec(tr, D), _row_spec(tr, 1), _row_spec(tr, D)],
            out_shape=[jax.ShapeDtypeStruct((T, D), F32), jax.ShapeDtypeStruct((T, 1), F32),
                       jax.ShapeDtypeStruct((T, D), BF16)],
            compiler_params=_cparams(("parallel",)),
        )(*ins)
    return call


def ln_bwd(name, xhat, rstd, gamma, dy=None, target=None, beta=None):
    T, D = xhat.shape
    tr = min(ROW_TILE, T)
    nt = T // tr
    head = target is not None

    def body(*refs):
        if head:
            xhat_ref, rstd_ref, g_ref, tgt_ref, b_ref = refs[:5]
            outs = refs[5:]
        else:
            xhat_ref, rstd_ref, g_ref, dy_ref = refs[:4]
            outs = refs[4:]
        dr_ref, drbf_ref, dg_ref, db_ref, cs_ref = outs[:5]
        rest = outs[5:]
        if head:
            loss_ref, acc_ref = rest
        else:
            (acc_ref,) = rest
        i = pl.program_id(0)
        xhat_v = xhat_ref[...]
        gam = g_ref[...]
        if head:
            diff = xhat_v * gam + b_ref[...] - tgt_ref[...]
            dyv = diff * (1.0 / D)
        else:
            dyv = dy_ref[...]
        dxh = dyv * gam
        m1 = jnp.mean(dxh, axis=-1, keepdims=True)
        m2 = jnp.mean(dxh * xhat_v, axis=-1, keepdims=True)
        dr = rstd_ref[...] * (dxh - m1 - xhat_v * m2)
        dr_ref[...] = dr
        drbf_ref[...] = dr.astype(BF16)

        @pl.when(i == 0)
        def _():
            acc_ref[...] = jnp.zeros_like(acc_ref)

        acc_ref[0] += _fold8(dyv * xhat_v)
        acc_ref[1] += _fold8(dyv)
        acc_ref[2] += _fold8(dr)
        if head:
            acc_ref[3] += _fold8(diff * diff)

        @pl.when(i == nt - 1)
        def _():
            dg_ref[...] = jnp.sum(acc_ref[0], axis=0, keepdims=True)
            db_ref[...] = jnp.sum(acc_ref[1], axis=0, keepdims=True)
            cs_ref[...] = jnp.sum(acc_ref[2], axis=0, keepdims=True)
            if head:
                loss_ref[...] = jnp.sum(jnp.sum(acc_ref[3], axis=0, keepdims=True), axis=1, keepdims=True)

    ins = [xhat, rstd, gamma] + ([target, beta] if head else [dy])
    specs = [_row_spec(tr, D), _row_spec(tr, 1), _vec_spec(D)] + ([_row_spec(tr, D), _vec_spec(D)] if head else [_row_spec(tr, D)])
    out_specs = [_row_spec(tr, D), _row_spec(tr, D), _vec_spec(D), _vec_spec(D), _vec_spec(D)]
    out_shape = [jax.ShapeDtypeStruct((T, D), F32), jax.ShapeDtypeStruct((T, D), BF16)] + [jax.ShapeDtypeStruct((1, D), F32)] * 3
    if head:
        out_specs.append(pl.BlockSpec((1, 1), lambda i: (0, 0)))
        out_shape.append(jax.ShapeDtypeStruct((1, 1), F32))
    return pl.pallas_call(
        body, name=name, grid=(nt,), in_specs=specs, out_specs=out_specs, out_shape=out_shape,
        scratch_shapes=[pltpu.VMEM((4, 8, D), F32)],
        compiler_params=_cparams(("arbitrary",)),
    )(*ins)


CONV_ROWS, CONV_COLS = 64, 512


def _tap_chunks(tt, D):
    for r0 in range(0, tt, min(CONV_ROWS, tt)):
        for c0 in range(0, D, min(CONV_COLS, D)):
            yield r0, min(CONV_ROWS, tt), c0, min(CONV_COLS, D)


SUBLANES = 8


def _shifted_copies(ext_ref, sh_ref):
    n = sh_ref.shape[1]
    for b in range(1, SUBLANES):
        sh_ref[b - 1] = ext_ref[pl.ds(b, n), :]


def _rows_at(ext_ref, sh_ref, off, nr, cols):
    a, b = divmod(off, SUBLANES)
    if b == 0:
        return ext_ref[pl.ds(off, nr), cols]
    return sh_ref[b - 1, pl.ds(a * SUBLANES, nr), cols]


def conv_fwd(name, h1, dw, dwb, lng, lnb):
    T, D2 = h1.shape
    D = D2 // 2
    tt = min(CONV_TILE, T)
    hb = tt // CONV_HALO
    KW = dw.shape[0]
    lead = CONV_HALO - (KW - 1)

    def body(a_ref, g_ref, ah_ref, gh_ref, dw_ref, dwb_ref, lng_ref, lnb_ref, u_ref, c_ref, s_ref, ext_ref, sh_ref):
        i = pl.program_id(0)
        u = a_ref[...] * _sigmoid(g_ref[...])
        u_ref[...] = u
        uh = ah_ref[...] * _sigmoid(gh_ref[...])
        ext_ref[pl.ds(0, CONV_HALO), :] = jnp.where(i > 0, uh, 0.0)
        ext_ref[pl.ds(CONV_HALO, tt), :] = u
        _shifted_copies(ext_ref, sh_ref)
        for r0, nr, c0, nc in _tap_chunks(tt, D):
            cols = pl.ds(c0, nc)
            acc = jnp.zeros((nr, nc), F32) + dwb_ref[:, cols]
            for k in range(KW):
                acc = acc + dw_ref[pl.ds(k, 1), cols] * _rows_at(ext_ref, sh_ref, r0 + lead + k, nr, cols)
            c_ref[pl.ds(r0, nr), cols] = acc
        c = c_ref[...]
        mu = jnp.mean(c, axis=-1, keepdims=True)
        cen = c - mu
        var = jnp.mean(cen * cen, axis=-1, keepdims=True)
        n = cen * lax.rsqrt(var + LN_EPS) * lng_ref[...] + lnb_ref[...]
        s_ref[...] = (n * _sigmoid(n)).astype(BF16)

    halo = lambda col: pl.BlockSpec((CONV_HALO, D), lambda i: (jnp.maximum(i * hb - 1, 0), col))
    return pl.pallas_call(
        body, name=name, grid=(T // tt,),
        in_specs=[pl.BlockSpec((tt, D), lambda i: (i, 0)), pl.BlockSpec((tt, D), lambda i: (i, 1)), halo(0), halo(1),
                  pl.BlockSpec((KW, D), lambda i: (0, 0)), _vec_spec(D), _vec_spec(D), _vec_spec(D)],
        out_specs=[_row_spec(tt, D)] * 3,
        out_shape=[jax.ShapeDtypeStruct((T, D), F32), jax.ShapeDtypeStruct((T, D), F32), jax.ShapeDtypeStruct((T, D), BF16)],
        scratch_shapes=[pltpu.VMEM((tt + CONV_HALO, D), F32),
                        pltpu.VMEM((SUBLANES - 1, tt + CONV_HALO - SUBLANES, D), F32)],
        compiler_params=_cparams(("parallel",)),
    )(h1, h1, h1, h1, dw, dwb, lng, lnb)


def conv_bwd_ln(name, ds, c, lng, lnb):
    T, D = c.shape
    tr = min(ROW_TILE, T)
    nt = T // tr

    def body(ds_ref, c_ref, g_ref, b_ref, dc_ref, dg_ref, db_ref, cs_ref, acc_ref):
        i = pl.program_id(0)
        cv = c_ref[...]
        mu = jnp.mean(cv, axis=-1, keepdims=True)
        cen = cv - mu
        var = jnp.mean(cen * cen, axis=-1, keepdims=True)
        rstd = lax.rsqrt(var + LN_EPS)
        chat = cen * rstd
        n = chat * g_ref[...] + b_ref[...]
        sg = _sigmoid(n)
        dn = ds_ref[...] * (sg * (1.0 + n * (1.0 - sg)))
        dxh = dn * g_ref[...]
        m1 = jnp.mean(dxh, axis=-1, keepdims=True)
        m2 = jnp.mean(dxh * chat, axis=-1, keepdims=True)
        dc = rstd * (dxh - m1 - chat * m2)
        dc_ref[...] = dc

        @pl.when(i == 0)
        def _():
            acc_ref[...] = jnp.zeros_like(acc_ref)

        acc_ref[0] += _fold8(dn * chat)
        acc_ref[1] += _fold8(dn)
        acc_ref[2] += _fold8(dc)

        @pl.when(i == nt - 1)
        def _():
            dg_ref[...] = jnp.sum(acc_ref[0], axis=0, keepdims=True)
            db_ref[...] = jnp.sum(acc_ref[1], axis=0, keepdims=True)
            cs_ref[...] = jnp.sum(acc_ref[2], axis=0, keepdims=True)

    return pl.pallas_call(
        body, name=name, grid=(nt,),
        in_specs=[_row_spec(tr, D), _row_spec(tr, D), _vec_spec(D), _vec_spec(D)],
        out_specs=[_row_spec(tr, D), _vec_spec(D), _vec_spec(D), _vec_spec(D)],
        out_shape=[jax.ShapeDtypeStruct((T, D), F32)] + [jax.ShapeDtypeStruct((1, D), F32)] * 3,
        scratch_shapes=[pltpu.VMEM((3, 8, D), F32)],
        compiler_params=_cparams(("arbitrary",)),
    )(ds, c, lng, lnb)


def conv_bwd_taps(name, dc, u, h1, dw):
    T, D = dc.shape
    tt = min(CONV_TILE, T)
    nt = T // tt
    hb = tt // CONV_HALO
    nhb = T // CONV_HALO
    KW = dw.shape[0]
    lead = CONV_HALO - (KW - 1)

    def body(dc_ref, dcn_ref, u_ref, uh_ref, a_ref, g_ref, dw_ref, dh1_ref, db1_ref, ddw_ref,
             edc_ref, eu_ref, du_ref, accw_ref, accb_ref, shdc_ref, shu_ref):
        i = pl.program_id(0)

        @pl.when(i == 0)
        def _():
            accw_ref[...] = jnp.zeros_like(accw_ref)
            accb_ref[...] = jnp.zeros_like(accb_ref)

        edc_ref[pl.ds(0, tt), :] = dc_ref[...]
        edc_ref[pl.ds(tt, CONV_HALO), :] = jnp.where(i < nt - 1, dcn_ref[...], 0.0)
        eu_ref[pl.ds(0, CONV_HALO), :] = jnp.where(i > 0, uh_ref[...], 0.0)
        eu_ref[pl.ds(CONV_HALO, tt), :] = u_ref[...]
        _shifted_copies(edc_ref, shdc_ref)
        _shifted_copies(eu_ref, shu_ref)
        for r0, nr, c0, nc in _tap_chunks(tt, D):
            cols = pl.ds(c0, nc)
            dcv = dc_ref[pl.ds(r0, nr), cols]
            acc = jnp.zeros((nr, nc), F32)
            for k in range(KW):
                acc = acc + dw_ref[pl.ds(k, 1), cols] * _rows_at(edc_ref, shdc_ref, r0 + (KW - 1) - k, nr, cols)
                accw_ref[k, :, cols] += _fold8(dcv * _rows_at(eu_ref, shu_ref, r0 + lead + k, nr, cols))
            du_ref[pl.ds(r0, nr), cols] = acc
        du = du_ref[...]
        sg = _sigmoid(g_ref[...])
        da = du * sg
        dg = du * a_ref[...] * sg * (1.0 - sg)
        dh1_ref[:, pl.ds(0, D)] = da.astype(BF16)
        dh1_ref[:, pl.ds(D, D)] = dg.astype(BF16)
        accb_ref[:, pl.ds(0, D)] += _fold8(da)
        accb_ref[:, pl.ds(D, D)] += _fold8(dg)

        @pl.when(i == nt - 1)
        def _():
            db1_ref[...] = jnp.sum(accb_ref[...], axis=0, keepdims=True)
            ddw_ref[...] = jnp.sum(accw_ref[...], axis=1)

    return pl.pallas_call(
        body, name=name, grid=(nt,),
        in_specs=[_row_spec(tt, D),
                  pl.BlockSpec((CONV_HALO, D), lambda i: (jnp.minimum((i + 1) * hb, nhb - 1), 0)),
                  _row_spec(tt, D),
                  pl.BlockSpec((CONV_HALO, D), lambda i: (jnp.maximum(i * hb - 1, 0), 0)),
                  pl.BlockSpec((tt, D), lambda i: (i, 0)), pl.BlockSpec((tt, D), lambda i: (i, 1)),
                  pl.BlockSpec((KW, D), lambda i: (0, 0))],
        out_specs=[_row_spec(tt, 2 * D), _vec_spec(2 * D), pl.BlockSpec((KW, D), lambda i: (0, 0))],
        out_shape=[jax.ShapeDtypeStruct((T, 2 * D), BF16), jax.ShapeDtypeStruct((1, 2 * D), F32),
                   jax.ShapeDtypeStruct((KW, D), F32)],
        scratch_shapes=[pltpu.VMEM((tt + CONV_HALO, D), F32), pltpu.VMEM((tt + CONV_HALO, D), F32),
                        pltpu.VMEM((tt, D), F32), pltpu.VMEM((KW, 8, D), F32), pltpu.VMEM((8, 2 * D), F32)]
                       + [pltpu.VMEM((SUBLANES - 1, tt + CONV_HALO - SUBLANES, D), F32)] * 2,
        compiler_params=_cparams(("arbitrary",)),
    )(dc, dc, u, u, h1, h1, dw)


def _t5_bucket(dist):
    max_exact = REL_BUCKETS // 2
    large = max_exact + (np.log(np.maximum(dist, 1) / max_exact) / math.log(REL_MAX_DIST / max_exact)
                         * (REL_BUCKETS - max_exact)).astype(np.int32)
    large = np.minimum(large, REL_BUCKETS - 1)
    return np.where(dist < max_exact, dist, large).astype(np.int32)


def _bucket_table(dil):
    i = np.arange(BAND)[:, None]
    j = np.arange(2 * BAND)[None, :]
    delta = i - j + BAND
    return _t5_bucket(np.clip(delta, 0, None) * dil)


def bias_expand(name, rel_bias, dil):
    n_heads = rel_bias.shape[1]
    idx = jnp.asarray(_bucket_table(dil))

    def body(rel_ref, idx_ref, out_ref):
        h = pl.program_id(0)
        idxv = idx_ref[...]
        b = jnp.zeros((BAND, 2 * BAND), F32)
        for bk in range(REL_BUCKETS):
            b = jnp.where(idxv == bk, rel_ref[bk, h], b)
        out_ref[...] = b

    return pl.pallas_call(
        body, name=name, grid=(n_heads,),
        in_specs=[pl.BlockSpec(memory_space=pltpu.SMEM), pl.BlockSpec((BAND, 2 * BAND), lambda h: (0, 0))],
        out_specs=pl.BlockSpec((None, BAND, 2 * BAND), lambda h: (h, 0, 0)),
        out_shape=jax.ShapeDtypeStruct((n_heads, BAND, 2 * BAND), F32),
        compiler_params=_cparams(("arbitrary",)),
    )(rel_bias, idx)


def relbias_grad(name, dsb_list):
    n_heads = dsb_list[0].shape[0]
    idxs = [jnp.asarray(_bucket_table(d)) for _, d in BRANCHES]
    nb = len(BRANCHES)

    def body(*refs):
        ds_refs, idx_refs, out_ref = refs[:nb], refs[nb:2 * nb], refs[2 * nb]
        lane = lax.broadcasted_iota(jnp.int32, (1, 128), 1)
        row = jnp.zeros((1, 128), F32)
        for bk in range(REL_BUCKETS):
            tot = jnp.zeros((1, 1), F32)
            for ds_ref, idx_ref in zip(ds_refs, idx_refs):
                sel = jnp.where(idx_ref[...] == bk, ds_ref[...], 0.0)
                tot = tot + jnp.sum(jnp.sum(sel, axis=0, keepdims=True), axis=1, keepdims=True)
            row = jnp.where(lane == bk, tot, row)
        out_ref[...] = row

    return pl.pallas_call(
        body, name=name, grid=(n_heads,),
        in_specs=[pl.BlockSpec((None, BAND, 2 * BAND), lambda h: (h, 0, 0))] * nb
                 + [pl.BlockSpec((BAND, 2 * BAND), lambda h: (0, 0))] * nb,
        out_specs=pl.BlockSpec((None, 1, 128), lambda h: (h, 0, 0)),
        out_shape=jax.ShapeDtypeStruct((n_heads, 1, 128), F32),
        compiler_params=_cparams(("arbitrary",)),
    )(*dsb_list, *idxs)


def _band_mask():
    i = lax.broadcasted_iota(jnp.int32, (BAND, 2 * BAND), 0)
    j = lax.broadcasted_iota(jnp.int32, (BAND, 2 * BAND), 1)
    return (j >= i) & (j <= i + BAND), j


def _rep2(x):
    return jnp.concatenate([x, x], axis=1)


ATTN_TOKENS = 2048
MERGE_ROWS = 256


def _rows(ref, start, n, dil):
    if dil == 1:
        return ref[pl.ds(start, n), :]
    return ref[pl.ds(start, n, stride=dil), :]


def _set_rows(ref, start, n, dil, val):
    if dil == 1:
        ref[pl.ds(start, n), :] = val
    else:
        ref[pl.ds(start, n, stride=dil), :] = val


def _attn_specs(ct, n_heads, n_chunks):
    cur = lambda col0: pl.BlockSpec((ct, HEAD_DIM), lambda h, c: (c, col0 + h))
    prev = lambda col0: pl.BlockSpec((ct, HEAD_DIM), lambda h, c: (jnp.maximum(c - 1, 0), col0 + h))
    nxt = lambda col0: pl.BlockSpec((ct, HEAD_DIM), lambda h, c: (jnp.minimum(c + 1, n_chunks - 1), col0 + h))
    bias = pl.BlockSpec((None, BAND, 2 * BAND), lambda h, c: (h, 0, 0))
    return cur, prev, nxt, bias


def _load_keys(kext_ref, vext_ref, k_ref, v_ref, kp_ref, vp_ref, r, dil, ct):
    lc = ct // dil
    kext_ref[pl.ds(0, BAND), :] = _rows(kp_ref, ct - BAND * dil + r, BAND, dil).astype(BF16)
    vext_ref[pl.ds(0, BAND), :] = _rows(vp_ref, ct - BAND * dil + r, BAND, dil).astype(BF16)
    kext_ref[pl.ds(BAND, lc), :] = _rows(k_ref, r, lc, dil).astype(BF16)
    vext_ref[pl.ds(BAND, lc), :] = _rows(v_ref, r, lc, dil).astype(BF16)


def _window_mask(band, jcol, a, c):
    if a > 0:
        return band
    return band & jnp.logical_or(jcol >= BAND, c > 0)


def attn_fwd(name, q, kv, biases):
    T, D = q.shape
    n_heads = D // HEAD_DIM
    ct = min(ATTN_TOKENS, T)
    n_chunks = T // ct
    nbr = len(BRANCHES)
    scale = HEAD_DIM ** -0.5
    nt_dims = (((1,), (1,)), ((), ()))
    nn_dims = (((1,), (0,)), ((), ()))

    def body(q_ref, k_ref, v_ref, kp_ref, vp_ref, *rest):
        b_refs = rest[:nbr]
        o_ref, obf_ref, lse_ref = rest[nbr:nbr + 3]
        kext_ref, vext_ref, acc_ref, m_ref, l_ref = rest[nbr + 3:]
        c = pl.program_id(1)
        band, jcol = _band_mask()
        for bi, (win, dil) in enumerate(BRANCHES):
            lc = ct // dil
            bias_v = b_refs[bi][...]

            def residue(r, carry, bi=bi, dil=dil, lc=lc, bias_v=bias_v):
                _load_keys(kext_ref, vext_ref, k_ref, v_ref, kp_ref, vp_ref, r, dil, ct)
                for a in range(lc // BAND):
                    tok = r + a * BAND * dil
                    qa = _rows(q_ref, tok, BAND, dil).astype(BF16)
                    kw = kext_ref[pl.ds(a * BAND, 2 * BAND), :]
                    vw = vext_ref[pl.ds(a * BAND, 2 * BAND), :]
                    s = lax.dot_general(qa, kw, nt_dims, preferred_element_type=F32) * scale + bias_v
                    s = jnp.where(_window_mask(band, jcol, a, c), s, NEG_BIG)
                    m = jnp.max(s, axis=-1, keepdims=True)
                    p = jnp.exp(s - m)
                    den = jnp.sum(p, axis=-1, keepdims=True)
                    pv = lax.dot_general(p.astype(BF16), vw, nn_dims, preferred_element_type=F32)
                    _set_rows(acc_ref.at[bi], tok, BAND, dil, pv)
                    _set_rows(m_ref.at[bi], tok, BAND, dil, jnp.broadcast_to(m, (BAND, HEAD_DIM)))
                    _set_rows(l_ref.at[bi], tok, BAND, dil, jnp.broadcast_to(den, (BAND, HEAD_DIM)))
                return carry

            if dil == 1:
                residue(0, 0)
            else:
                lax.fori_loop(0, dil, residue, 0)

        def merge(i, carry):
            rows = pl.ds(pl.multiple_of(i * MERGE_ROWS, MERGE_ROWS), MERGE_ROWS)
            ms = [m_ref[bi, rows, :] for bi in range(nbr)]
            m = functools.reduce(jnp.maximum, ms)
            ws = [jnp.exp(mb - m) for mb in ms]
            tot = functools.reduce(lambda x, y: x + y, [w * l_ref[bi, rows, :] for bi, w in enumerate(ws)])
            o = functools.reduce(lambda x, y: x + y, [w * acc_ref[bi, rows, :] for bi, w in enumerate(ws)]) / tot
            o_ref[rows, :] = o
            obf_ref[rows, :] = o.astype(BF16)
            lse_ref[rows, :] = m + jnp.log(tot)
            return carry

        lax.fori_loop(0, ct // min(MERGE_ROWS, ct), merge, 0)

    cur, prev, nxt, bias = _attn_specs(ct, n_heads, n_chunks)
    lmax = ct + BAND
    return pl.pallas_call(
        body, name=name, grid=(n_heads, n_chunks),
        in_specs=[cur(0), cur(0), cur(n_heads), prev(0), prev(n_heads)] + [bias] * nbr,
        out_specs=[cur(0)] * 3,
        out_shape=[jax.ShapeDtypeStruct((T, D), F32), jax.ShapeDtypeStruct((T, D), BF16), jax.ShapeDtypeStruct((T, D), F32)],
        scratch_shapes=[pltpu.VMEM((lmax, HEAD_DIM), BF16)] * 2 + [pltpu.VMEM((nbr, ct, HEAD_DIM), F32)] * 3,
        compiler_params=_cparams(("arbitrary", "arbitrary")),
    )(q, kv, kv, kv, kv, *biases)


def attn_bwd(name, q, kv, do, o, lse, biases):
    T, D = q.shape
    n_heads = D // HEAD_DIM
    ct = min(ATTN_TOKENS, T)
    n_chunks = T // ct
    nbr = len(BRANCHES)
    scale = HEAD_DIM ** -0.5
    nt_dims = (((1,), (1,)), ((), ()))
    tn_dims = (((0,), (0,)), ((), ()))
    nn_dims = (((1,), (0,)), ((), ()))
    mrows = min(MERGE_ROWS, ct)

    def body(q_ref, k_ref, v_ref, do_ref, o_ref, lse_ref, kp_ref, vp_ref, qn_ref, don_ref, on_ref, lsen_ref, *rest):
        b_refs = rest[:nbr]
        dq_ref, dk_ref, dv_ref = rest[nbr:nbr + 3]
        dsb_refs = rest[nbr + 3:2 * nbr + 3]
        kext_ref, vext_ref, dkext_ref, dvext_ref, dqa_ref, dka_ref, dva_ref, dsum_ref, dsumn_ref = rest[2 * nbr + 3:]
        c = pl.program_id(1)
        band, jcol = _band_mask()

        def prep(i, carry):
            rows = pl.ds(pl.multiple_of(i * mrows, mrows), mrows)
            dsum_ref[rows, :] = jnp.broadcast_to(jnp.sum(do_ref[rows, :] * o_ref[rows, :], axis=-1, keepdims=True), (mrows, HEAD_DIM))
            dsumn_ref[rows, :] = jnp.broadcast_to(jnp.sum(don_ref[rows, :] * on_ref[rows, :], axis=-1, keepdims=True), (mrows, HEAD_DIM))
            dqa_ref[rows, :] = jnp.zeros((mrows, HEAD_DIM), F32)
            dka_ref[rows, :] = jnp.zeros((mrows, HEAD_DIM), F32)
            dva_ref[rows, :] = jnp.zeros((mrows, HEAD_DIM), F32)
            return carry

        lax.fori_loop(0, ct // mrows, prep, 0)

        @pl.when(c == 0)
        def _():
            for r in dsb_refs:
                r[...] = jnp.zeros_like(r)

        for bi, (win, dil) in enumerate(BRANCHES):
            lc = ct // dil
            bias_v = b_refs[bi][...]
            dsb_ref = dsb_refs[bi]

            def residue(r, carry, dil=dil, lc=lc, bias_v=bias_v, dsb_ref=dsb_ref):
                _load_keys(kext_ref, vext_ref, k_ref, v_ref, kp_ref, vp_ref, r, dil, ct)
                dkext_ref[pl.ds(0, BAND + lc), :] = jnp.zeros((BAND + lc, HEAD_DIM), F32)
                dvext_ref[pl.ds(0, BAND + lc), :] = jnp.zeros((BAND + lc, HEAD_DIM), F32)
                for a in range(lc // BAND):
                    tok = r + a * BAND * dil
                    qa = _rows(q_ref, tok, BAND, dil).astype(BF16)
                    doa = _rows(do_ref, tok, BAND, dil).astype(BF16)
                    kw = kext_ref[pl.ds(a * BAND, 2 * BAND), :]
                    vw = vext_ref[pl.ds(a * BAND, 2 * BAND), :]
                    s = lax.dot_general(qa, kw, nt_dims, preferred_element_type=F32) * scale + bias_v
                    p = jnp.where(_window_mask(band, jcol, a, c), jnp.exp(s - _rep2(_rows(lse_ref, tok, BAND, dil))), 0.0)
                    dp = lax.dot_general(doa, vw, nt_dims, preferred_element_type=F32)
                    ds = p * (dp - _rep2(_rows(dsum_ref, tok, BAND, dil)))
                    dsb_ref[...] += ds
                    dsb16 = ds.astype(BF16)
                    dqw = lax.dot_general(dsb16, kw, nn_dims, preferred_element_type=F32) * scale
                    _set_rows(dqa_ref, tok, BAND, dil, _rows(dqa_ref, tok, BAND, dil) + dqw)
                    dkext_ref[pl.ds(a * BAND, 2 * BAND), :] += lax.dot_general(dsb16, qa, tn_dims, preferred_element_type=F32) * scale
                    dvext_ref[pl.ds(a * BAND, 2 * BAND), :] += lax.dot_general(p.astype(BF16), doa, tn_dims, preferred_element_type=F32)

                @pl.when(c < n_chunks - 1)
                def _():
                    qn = _rows(qn_ref, r, BAND, dil).astype(BF16)
                    don = _rows(don_ref, r, BAND, dil).astype(BF16)
                    kl = kext_ref[pl.ds(lc, BAND), :]
                    vl = vext_ref[pl.ds(lc, BAND), :]
                    s = lax.dot_general(qn, kl, nt_dims, preferred_element_type=F32) * scale + bias_v[:, :BAND]
                    p = jnp.where(band[:, :BAND], jnp.exp(s - _rows(lsen_ref, r, BAND, dil)), 0.0)
                    dp = lax.dot_general(don, vl, nt_dims, preferred_element_type=F32)
                    ds = p * (dp - _rows(dsumn_ref, r, BAND, dil))
                    dkext_ref[pl.ds(lc, BAND), :] += lax.dot_general(ds.astype(BF16), qn, tn_dims, preferred_element_type=F32) * scale
                    dvext_ref[pl.ds(lc, BAND), :] += lax.dot_general(p.astype(BF16), don, tn_dims, preferred_element_type=F32)

                _set_rows(dka_ref, r, lc, dil, _rows(dka_ref, r, lc, dil) + dkext_ref[pl.ds(BAND, lc), :])
                _set_rows(dva_ref, r, lc, dil, _rows(dva_ref, r, lc, dil) + dvext_ref[pl.ds(BAND, lc), :])
                return carry

            if dil == 1:
                residue(0, 0)
            else:
                lax.fori_loop(0, dil, residue, 0)

        dq_ref[...] = dqa_ref[...].astype(BF16)
        dk_ref[...] = dka_ref[...].astype(BF16)
        dv_ref[...] = dva_ref[...].astype(BF16)

    cur, prev, nxt, bias = _attn_specs(ct, n_heads, n_chunks)
    lmax = ct + BAND
    res = pl.pallas_call(
        body, name=name, grid=(n_heads, n_chunks),
        in_specs=[cur(0), cur(0), cur(n_heads), cur(0), cur(0), cur(0), prev(0), prev(n_heads), nxt(0), nxt(0), nxt(0), nxt(0)]
                 + [bias] * nbr,
        out_specs=[cur(0)] * 3 + [bias] * nbr,
        out_shape=[jax.ShapeDtypeStruct((T, D), BF16)] * 3 + [jax.ShapeDtypeStruct((n_heads, BAND, 2 * BAND), F32)] * nbr,
        scratch_shapes=[pltpu.VMEM((lmax, HEAD_DIM), BF16)] * 2 + [pltpu.VMEM((lmax, HEAD_DIM), F32)] * 2
                       + [pltpu.VMEM((ct, HEAD_DIM), F32)] * 5,
        compiler_params=_cparams(("arbitrary", "arbitrary")),
    )(q, kv, kv, do, o, lse, kv, kv, q, do, o, lse, *biases)
    return res[0], res[1], res[2], list(res[3:])


def _divisor_tile(n, cap, mult):
    if n <= cap:
        return n
    t = cap - cap % mult
    while n % t:
        t -= mult
    return t


def _tile2(R, C):
    return _divisor_tile(R, 512, 8), _divisor_tile(C, 1024, 128)


def half_cast(name, dw, core):
    S, R, C = dw.shape
    hr = R // 2
    tr, tc = _tile2(hr, C)
    nrb = hr // tr

    def body(c_ref, x_ref, o_ref):
        o_ref[...] = x_ref[...].astype(BF16)

    return pl.pallas_call(
        body, name=name,
        grid_spec=pltpu.PrefetchScalarGridSpec(
            num_scalar_prefetch=1, grid=(S, nrb, C // tc),
            in_specs=[pl.BlockSpec((None, tr, tc), lambda s, i, j, c: (s, (1 - c[0]) * nrb + i, j))],
            out_specs=pl.BlockSpec((None, tr, tc), lambda s, i, j, c: (s, i, j))),
        out_shape=jax.ShapeDtypeStruct((S, hr, C), BF16),
        compiler_params=_cparams(("parallel", "parallel", "parallel")),
    )(core, dw)


def pair_sum(name, dw, recv, core):
    S, R, C = dw.shape
    hr = R // 2
    tr, tc = _tile2(hr, C)
    nrb = hr // tr

    def body(c_ref, x_ref, r_ref, p_ref, pbf_ref):
        p = x_ref[...] + r_ref[...].astype(F32)
        p_ref[...] = p
        pbf_ref[...] = p.astype(BF16)

    out = pl.BlockSpec((None, tr, tc), lambda s, i, j, c: (s, i, j))
    return pl.pallas_call(
        body, name=name,
        grid_spec=pltpu.PrefetchScalarGridSpec(
            num_scalar_prefetch=1, grid=(S, nrb, C // tc),
            in_specs=[pl.BlockSpec((None, tr, tc), lambda s, i, j, c: (s, c[0] * nrb + i, j)), out],
            out_specs=[out, out]),
        out_shape=[jax.ShapeDtypeStruct((S, hr, C), F32), jax.ShapeDtypeStruct((S, hr, C), BF16)],
        compiler_params=_cparams(("parallel", "parallel", "parallel")),
    )(core, dw, recv)


def chip_sum(name, p, recv, chip, core):
    S, hr, C = p.shape
    tr, tc = _tile2(hr, C)
    nrb = hr // tr

    def body(chip_ref, core_ref, p_ref, r_ref, o_ref):
        acc = p_ref[...]
        for t in range(N_CHIPS - 1):
            acc = acc + r_ref[t].astype(F32)
        o_ref[...] = acc

    return pl.pallas_call(
        body, name=name,
        grid_spec=pltpu.PrefetchScalarGridSpec(
            num_scalar_prefetch=2, grid=(nrb, C // tc),
            in_specs=[pl.BlockSpec((None, tr, tc), lambda i, j, s, c: (s[0], i, j)),
                      pl.BlockSpec((N_CHIPS - 1, tr, tc), lambda i, j, s, c: (0, i, j))],
            out_specs=pl.BlockSpec((tr, tc), lambda i, j, s, c: (c[0] * nrb + i, j))),
        out_shape=jax.ShapeDtypeStruct((2 * hr, C), F32),
        compiler_params=_cparams(("parallel", "parallel")),
    )(chip, core, p, recv)


def adamw(name, w, g, m, v):
    R, C = w.shape
    tr, tc = _tile2(R, C)
    c1 = 1.0 - ADAM_B1 ** ADAM_STEP
    c2 = 1.0 - ADAM_B2 ** ADAM_STEP

    def body(w_ref, g_ref, m_ref, v_ref, d_ref, nm_ref, nv_ref):
        gv = g_ref[...]
        nm = ADAM_B1 * m_ref[...] + (1.0 - ADAM_B1) * gv
        nv = ADAM_B2 * v_ref[...] + (1.0 - ADAM_B2) * (gv * gv)
        nm_ref[...] = nm
        nv_ref[...] = nv
        d_ref[...] = -ADAM_LR * ((nm / c1) / (jnp.sqrt(nv / c2) + ADAM_EPS) + ADAM_WD * w_ref[...])

    spec = pl.BlockSpec((tr, tc), lambda i, j: (i, j))
    return pl.pallas_call(
        body, name=name, grid=(R // tr, C // tc), in_specs=[spec] * 4, out_specs=[spec] * 3,
        out_shape=[jax.ShapeDtypeStruct((R, C), F32)] * 3,
        compiler_params=_cparams(("parallel", "parallel")),
    )(w, g, m, v)


def sum_devices(name, gathered):
    n, R, C = gathered.shape

    def body(x_ref, o_ref):
        acc = x_ref[0]
        for d in range(1, n):
            acc = acc + x_ref[d]
        o_ref[...] = acc

    return pl.pallas_call(
        body, name=name, in_specs=[pl.BlockSpec(memory_space=pltpu.VMEM)],
        out_specs=pl.BlockSpec(memory_space=pltpu.VMEM),
        out_shape=jax.ShapeDtypeStruct((R, C), F32),
    )(gathered)


def _place():
    x, y, c = lax.axis_index("x"), lax.axis_index("y"), lax.axis_index("c")
    return x, y, c


def _other_chips(x, y):
    return [(1 - x, y), (x, 1 - y), (1 - x, 1 - y)]


def all_gather8(name, block):
    R, C = block.shape

    def body(x_ref, out_ref, send_sems, recv_sems, local_sem):
        x, y, c = _place()
        me, sibling = (x, y, c), (x, y, 1 - c)
        chips = _other_chips(x, y)

        def rows(px, py, pc):
            return out_ref.at[4 * px + 2 * py + pc]

        def copy(k, blk, to, src=None):
            return pltpu.make_async_remote_copy(
                src_ref=rows(*blk) if src is None else src, dst_ref=rows(*blk),
                send_sem=send_sems.at[k], recv_sem=recv_sems.at[k], device_id=to, device_id_type=MESH)

        mine = pltpu.make_async_copy(x_ref, rows(*me), local_sem)
        mine.start()
        first = [copy(0, me, sibling, src=x_ref)]
        first += [copy(1 + j, me, (*chip, c), src=x_ref) for j, chip in enumerate(chips)]
        for cp in first:
            cp.start()
        passed = [copy(4 + j, (*chip, c), sibling) for j, chip in enumerate(chips)]
        for j, chip in enumerate(chips):
            copy(1 + j, (*chip, c), me).wait_recv()
            passed[j].start()
        copy(0, sibling, me).wait_recv()
        for j, chip in enumerate(chips):
            copy(4 + j, (*chip, 1 - c), me).wait_recv()
        for cp in first + passed:
            cp.wait_send()
        mine.wait()

    return pl.pallas_call(
        body, name=name, out_shape=jax.ShapeDtypeStruct((N_DEV, R, C), block.dtype),
        in_specs=[pl.BlockSpec(memory_space=pltpu.VMEM)], out_specs=pl.BlockSpec(memory_space=pltpu.VMEM),
        scratch_shapes=[pltpu.SemaphoreType.DMA((7,)), pltpu.SemaphoreType.DMA((7,)), pltpu.SemaphoreType.DMA],
    )(block)


_HBM = pl.BlockSpec(memory_space=pltpu.HBM)
_SEM = pl.BlockSpec(memory_space=pltpu.SEMAPHORE)
_DATAFLOW = pltpu.SideEffectType.DATAFLOW_SIDE_EFFECTING


def _in_hbm(a):
    return pltpu.with_memory_space_constraint(a, pltpu.HBM)


def split_start(name, srcs, lands, n_sem, plan):
    ns, nl = len(srcs), len(lands)

    def body(*refs):
        src, land = refs[:ns], refs[ns:ns + nl]
        send_sems, recv_sems = refs[ns + nl], refs[ns + nl + 1]
        token = refs[-1]
        outgoing, _ = plan(src, land, send_sems, recv_sems)
        for cp in outgoing:
            cp.start()
        token[...] = jnp.zeros_like(token)

    bufs = list(srcs) + list(lands)
    res = pl.pallas_call(
        body, name=name,
        out_shape=(pltpu.SemaphoreType.DMA((n_sem,)), pltpu.SemaphoreType.DMA((n_sem,)),
                   *[pltpu.HBM(b.shape, b.dtype) for b in bufs], jax.ShapeDtypeStruct((8, 128), F32)),
        in_specs=[_HBM] * (ns + nl),
        out_specs=(_SEM, _SEM, *[_HBM] * (ns + nl), pl.BlockSpec(memory_space=pltpu.VMEM)),
        input_output_aliases={i: 2 + i for i in range(ns + nl)},
        compiler_params=pltpu.CompilerParams(has_side_effects=_DATAFLOW),
    )(*[_in_hbm(b) for b in bufs])
    return res[0], res[1], list(res[2:2 + ns]), list(res[2 + ns:2 + ns + nl]), res[-1]


def split_wait(name, started, after, plan):
    send_sems, recv_sems, srcs, lands, _ = started
    ns, nl = len(srcs), len(lands)

    def body(*refs):
        src, land = refs[:ns], refs[ns:ns + nl]
        send, recv = refs[ns + nl], refs[ns + nl + 1]
        outgoing, incoming = plan(src, land, send, recv)
        for cp in outgoing:
            cp.wait_send()
        for cp in incoming:
            cp.wait_recv()

    bufs = list(srcs) + list(lands)
    res = pl.pallas_call(
        body, name=name,
        out_shape=tuple(pltpu.HBM(b.shape, b.dtype) for b in bufs),
        in_specs=[_HBM] * (ns + nl) + [_SEM, _SEM, pl.BlockSpec(memory_space=pl.ANY)],
        out_specs=tuple([_HBM] * (ns + nl)),
        input_output_aliases={i: i for i in range(ns + nl)},
        compiler_params=pltpu.CompilerParams(has_side_effects=_DATAFLOW),
    )(*bufs, send_sems, recv_sems, after)
    return list(res[ns:])


def _rcopy(src, dst, send_sems, ks, recv_sems, kr, device):
    return pltpu.make_async_remote_copy(src_ref=src, dst_ref=dst, send_sem=send_sems.at[ks], recv_sem=recv_sems.at[kr],
                                        device_id=device, device_id_type=MESH)


def _half_rows(ref, h):
    hr = ref.shape[0] // 2
    return ref.at[pl.ds(h * hr, hr)]


def _gather_plan(src, land, send_sems, recv_sems):
    x, y, c = _place()
    me_chip = 2 * x + y
    chips = _other_chips(x, y)
    outgoing, incoming = [], []
    for w, buf in enumerate(land):
        mine = _half_rows(buf.at[me_chip], c)
        for t, chip in enumerate(chips):
            slot = 2 * chip[0] + chip[1]
            for cc in range(2):
                outgoing.append(_rcopy(mine, mine, send_sems, 6 * w + 2 * t + cc, recv_sems, 6 * w + 2 * t + c, (*chip, cc)))
                theirs = _half_rows(buf.at[slot], cc)
                incoming.append(_rcopy(theirs, theirs, send_sems, 6 * w + 2 * t + cc, recv_sems, 6 * w + 2 * t + cc, (*chip, cc)))
    return outgoing, incoming


def _swap_plan(src, land, send_sems, recv_sems):
    x, y, c = _place()
    cp = _rcopy(src[0], land[0], send_sems, 0, recv_sems, 0, (x, y, 1 - c))
    return [cp], [cp]


def _scatter_plan(src, land, send_sems, recv_sems):
    x, y, c = _place()
    cps = [_rcopy(src[0].at[2 * chip[0] + chip[1]], land[0].at[t], send_sems, t, recv_sems, t, (*chip, c))
           for t, chip in enumerate(_other_chips(x, y))]
    return cps, cps


def _share_plan(src, land, send_sems, recv_sems):
    x, y, c = _place()
    mine, theirs = _half_rows(land[0], c), _half_rows(land[0], 1 - c)
    return ([_rcopy(mine, mine, send_sems, 0, recv_sems, 0, (x, y, 1 - c))],
            [_rcopy(theirs, theirs, send_sems, 0, recv_sems, 0, (x, y, 1 - c))])


def place_shard(name, shard, chip, deps=()):
    R, C = shard.shape
    tr, tc = _tile2(R, C)

    def body(chip_ref, x_ref, *rest):
        rest[-1][...] = x_ref[...].astype(BF16)

    return pl.pallas_call(
        body, name=name,
        grid_spec=pltpu.PrefetchScalarGridSpec(
            num_scalar_prefetch=1, grid=(R // tr, C // tc),
            in_specs=[pl.BlockSpec((tr, tc), lambda i, j, s: (i, j))]
                     + [pl.BlockSpec(d.shape, lambda i, j, s: (0, 0)) for d in deps],
            out_specs=pl.BlockSpec((None, tr, tc), lambda i, j, s: (s[0], i, j))),
        out_shape=jax.ShapeDtypeStruct((N_CHIPS, R, C), BF16),
        compiler_params=_cparams(("parallel", "parallel")),
    )(chip, shard, *deps)


class GradExchange:
    SCATTER_TICKS = 2

    def __init__(self, chip1, core, shard, mom, vel):
        self.chip1, self.core, self.shard, self.mom, self.vel = chip1, core, shard, mom, vel
        self.inflight, self.tokens, self.results = [], [], {}

    def take_deps(self):
        deps, self.tokens = self.tokens, []
        return deps

    def _start(self, name, srcs, lands, n_sem, plan):
        started = split_start(name, srcs, lands, n_sem, plan)
        self.tokens.append(started[-1])
        return started

    def add(self, n, dw):
        S, R, C = dw.shape
        to_sibling = half_cast("rs_cast_" + n, dw, self.core)
        started = self._start("rs_swap_start_" + n, [to_sibling], [lax.empty((S, R // 2, C), BF16)], 1, _swap_plan)
        self.inflight.append(dict(n=n, dw=dw, stage=0, started=started, ticks=0))

    def tick(self, after):
        for it in self.inflight:
            n = it["n"]
            if it["stage"] == 0:
                (recv,) = split_wait("rs_swap_wait_" + n, it["started"], after, _swap_plan)
                p, pbf = pair_sum("rs_pair_sum_" + n, it["dw"], recv, self.core)
                S, hr, C = pbf.shape
                it.update(stage=1, p=p, ticks=0,
                          started=self._start("rs_scatter_start_" + n, [pbf], [lax.empty((N_CHIPS - 1, hr, C), BF16)], 3, _scatter_plan))
            elif it["stage"] == 1:
                it["ticks"] += 1
                if it["ticks"] >= self.SCATTER_TICKS:
                    (recv,) = split_wait("rs_scatter_wait_" + n, it["started"], after, _scatter_plan)
                    half = chip_sum("rs_chip_sum_" + n, it["p"], recv, self.chip1, self.core)
                    it.update(stage=2, started=self._start("rs_share_start_" + n, [], [half], 1, _share_plan))
            elif it["stage"] == 2:
                (grad,) = split_wait("rs_share_wait_" + n, it["started"], after, _share_plan)
                self.results[n] = (grad,) + tuple(adamw("adamw_" + n, self.shard[n], grad, self.mom[n], self.vel[n]))
                it["stage"] = 3
        self.inflight = [it for it in self.inflight if it["stage"] < 3]

    def flush(self, after):
        while self.inflight:
            self.tick(after)


def _pack(arrs):
    parts = []
    for a in arrs:
        flat = a.reshape(-1).astype(F32)
        n = flat.shape[0]
        padded = -(-n // 1024) * 1024
        parts.append(jnp.pad(flat, (0, padded - n)).reshape(padded // 128, 128))
    return jnp.concatenate(parts, axis=0)


def _unpack(buf, shapes):
    out, row = [], 0
    for shp in shapes:
        n = int(np.prod(shp))
        rows = -(-n // 1024) * 8
        out.append(buf[row:row + rows].reshape(-1)[:n].reshape(shp))
        row += rows
    return out


def _bias_epi(acc, b):
    return (acc + b,)


def local_step(x, target, W, P, ex, first_deps=()):
    T, D = x.shape
    g = {}
    plain = lambda acc: (acc,)

    (h1,) = mm_nn("pw1_fwd", x, W("pw1", x), "col", _bias_epi, [F32],
                  extras=[(P["pw1_b"], "row")] + [(d, "dep") for d in first_deps])
    u, cpre, s = conv_fwd("conv_fwd", h1, P["dw_w"], P["dw_b"], P["cln_g"], P["cln_b"])
    (mix0,) = mm_nn("pw2_fwd", s, W("pw2", s), "row", _bias_epi, [F32], extras=[(P["pw2_b"], "row")])
    ln = [None] * 4
    gam = [P["ln_mix_g"][0:1], P["ln_mlp_g"][0:1], P["ln_mix_g"][1:2], P["ln_mlp_g"][1:2]]
    bet = [P["ln_mix_b"][0:1], P["ln_mlp_b"][0:1], P["ln_mix_b"][1:2], P["ln_mlp_b"][1:2]]
    ln[0] = ln_fwd("ln0_fwd", mix0, x)(gam[0], bet[0])

    def mlp_fwd(tag, i_ln, n1, n2):
        xhat, rstd, xbf = ln[i_ln]

        def up_epi(acc):
            r = jnp.maximum(acc, 0.0)
            return r * r, r

        hid, relu = mm_nn(tag + "_up", xbf, W(n1, xbf), "col", up_epi, [BF16, BF16])
        (mlp,) = mm_nn(tag + "_down", hid, W(n2, hid), "row", plain, [F32])
        ln[i_ln + 1] = ln_fwd(tag + "_ln", mlp, xhat, gam[i_ln], bet[i_ln])(gam[i_ln + 1], bet[i_ln + 1])
        return hid, relu

    hid0 = mlp_fwd("mlp0", 0, "w1_0", "w2_0")

    x2bf = ln[1][2]
    (kv,) = mm_nn("kv_fwd", x2bf, W("kv", x2bf), "col", plain, [F32])
    (q,) = mm_nn("q_fwd", x2bf, W("wq", kv), "row", plain, [F32])
    biases = [bias_expand("bias_d%d" % d, P["rel_bias"], d) for _, d in BRANCHES]
    assert all(win // d == BAND and min(ATTN_TOKENS, T) % (BAND * d) == 0 for win, d in BRANCHES)
    o, obf, lse = attn_fwd("attn_fwd", q, kv, biases)
    (attn,) = mm_nn("wo_fwd", obf, W("wo", obf), "row", plain, [F32])
    ln[2] = ln_fwd("ln2_fwd", attn, ln[1][0], gam[1], bet[1])(gam[2], bet[2])
    hid1 = mlp_fwd("mlp1", 2, "w1_1", "w2_1")

    dr3, dr3bf, g["ln_mlp_g1"], g["ln_mlp_b1"], _, loss_sum = ln_bwd(
        "ln3_bwd", ln[3][0], ln[3][1], gam[3], target=target, beta=bet[3])

    def dw_step(name, wname, a, cot, axis):
        dw = mm_tn(name, a, cot, W(wname, a).shape, axis, deps=ex.take_deps())
        ex.tick(dw)
        ex.add(wname, dw)

    def dx_step(name, cot, wname, axis, epilogue, out_dtype, extras):
        deps = [(d, "dep") for d in ex.take_deps()]
        (out,) = mm_nt(name, cot, W(wname, cot), axis, epilogue, [out_dtype], extras=list(extras) + deps)
        ex.tick(out)
        return out

    def mlp_bwd(tag, i_ln, n1, n2, hid_relu, dr, drbf):
        xbf = ln[i_ln][2]
        hid, relu = hid_relu
        dw_step(tag + "_dw2", n2, hid, drbf, "row")
        dp = dx_step(tag + "_dhid", drbf, n2, "row", lambda acc, r: (acc * (2.0 * r.astype(F32)),), BF16, [(relu, "tile")])
        dw_step(tag + "_dw1", n1, xbf, dp, "col")
        return dx_step(tag + "_dx", dp, n1, "col", lambda acc, e: (acc + ALPHA * e,), F32, [(dr, "tile")])

    dx3 = mlp_bwd("mlp1", 2, "w1_1", "w2_1", hid1, dr3, dr3bf)
    dr2, dr2bf, g["ln_mix_g1"], g["ln_mix_b1"], _ = ln_bwd("ln2_bwd", ln[2][0], ln[2][1], gam[2], dy=dx3)
    dw_step("wo_dw", "wo", obf, dr2bf, "row")
    do = dx_step("wo_dx", dr2bf, "wo", "row", plain, F32, [])
    dq, dk, dv, dsbs = attn_bwd("attn_bwd", q, kv, do, o, lse, biases)
    g["rel_bias"] = relbias_grad("relbias_grad", dsbs)[:, 0, :REL_BUCKETS].T
    dkv = jnp.concatenate([dk, dv], axis=1)
    dw_step("wq_dw", "wq", x2bf, dq, "row")
    dw_step("kv_dw", "kv", x2bf, dkv, "col")
    dx2a = dx_step("wq_dx", dq, "wq", "row", lambda acc, e: (acc + ALPHA * e,), F32, [(dr2, "tile")])
    dx2 = dx_step("kv_dx", dkv, "kv", "col", lambda acc, e: (acc + e,), F32, [(dx2a, "tile")])

    dr1, dr1bf, g["ln_mlp_g0"], g["ln_mlp_b0"], _ = ln_bwd("ln1_bwd", ln[1][0], ln[1][1], gam[1], dy=dx2)
    dx1 = mlp_bwd("mlp0", 0, "w1_0", "w2_0", hid0, dr1, dr1bf)
    dr0, dr0bf, g["ln_mix_g0"], g["ln_mix_b0"], g["pw2_b"] = ln_bwd("ln0_bwd", ln[0][0], ln[0][1], gam[0], dy=dx1)

    dw_step("pw2_dw", "pw2", s, dr0bf, "row")
    ds = dx_step("pw2_dx", dr0bf, "pw2", "row", plain, F32, [])
    dc, g["cln_g"], g["cln_b"], g["dw_b"] = conv_bwd_ln("conv_bwd_ln", ds, cpre, P["cln_g"], P["cln_b"])
    dh1, g["pw1_b"], g["dw_w"] = conv_bwd_taps("conv_bwd_taps", dc, u, h1, P["dw_w"])
    dw_step("pw1_dw", "pw1", x, dh1, "col")
    dx = dx_step("pw1_dx", dh1, "pw1", "col", lambda acc, e: (acc + ALPHA * e,), F32, [(dr0, "tile")])
    ex.flush(dx)
    return loss_sum, dx, g


BIG = ("pw1", "pw2", "w1_0", "w2_0", "kv", "wq", "wo", "w1_1", "w2_1")


def kernel(x, conv_pw1_w, conv_pw1_b, conv_dw_w, conv_dw_b, conv_ln_g, conv_ln_b, conv_pw2_w, conv_pw2_b, w_kv, attn_wq, attn_wo, rel_bias, mlp_w1, mlp_w2, ln_mix_g, ln_mix_b, ln_mlp_g, ln_mlp_b, loss_target, m_conv_pw1_w, m_conv_pw1_b, m_conv_dw_w, m_conv_dw_b, m_conv_ln_g, m_conv_ln_b, m_conv_pw2_w, m_conv_pw2_b, m_w_kv, m_attn_wq, m_attn_wo, m_rel_bias, m_mlp_w1, m_mlp_w2, m_ln_mix_g, m_ln_mix_b, m_ln_mlp_g, m_ln_mlp_b, v_conv_pw1_w, v_conv_pw1_b, v_conv_dw_w, v_conv_dw_b, v_conv_ln_g, v_conv_ln_b, v_conv_pw2_w, v_conv_pw2_b, v_w_kv, v_attn_wq, v_attn_wo, v_rel_bias, v_mlp_w1, v_mlp_w2, v_ln_mix_g, v_ln_mix_b, v_ln_mlp_g, v_ln_mlp_b):
    _, T, D = x.shape
    xi, yi, ci = _place()
    chip = 2 * xi + yi
    core = jnp.reshape(ci, (1,)).astype(jnp.int32)
    chip1 = jnp.reshape(chip, (1,)).astype(jnp.int32)

    def two_d(a):
        return a.reshape(a.shape[-2:])

    shard = {"pw1": two_d(conv_pw1_w), "pw2": two_d(conv_pw2_w), "kv": w_kv, "wq": two_d(attn_wq), "wo": two_d(attn_wo),
             "w1_0": mlp_w1[0], "w1_1": mlp_w1[1], "w2_0": mlp_w2[0], "w2_1": mlp_w2[1]}
    mom = {"pw1": two_d(m_conv_pw1_w), "pw2": two_d(m_conv_pw2_w), "kv": m_w_kv, "wq": two_d(m_attn_wq), "wo": two_d(m_attn_wo),
           "w1_0": m_mlp_w1[0], "w1_1": m_mlp_w1[1], "w2_0": m_mlp_w2[0], "w2_1": m_mlp_w2[1]}
    vel = {"pw1": two_d(v_conv_pw1_w), "pw2": two_d(v_conv_pw2_w), "kv": v_w_kv, "wq": two_d(v_attn_wq), "wo": two_d(v_attn_wo),
           "w1_0": v_mlp_w1[0], "w1_1": v_mlp_w1[1], "w2_0": v_mlp_w2[0], "w2_1": v_mlp_w2[1]}

    started = {}
    for n in BIG:
        deps = [started[prev][-1] for prev in list(started)[-1:]]
        started[n] = split_start("gather_start_" + n, [], [place_shard("place_" + n, shard[n], chip1, deps)], 6, _gather_plan)
    gathered = {}

    def W(n, after):
        if n not in gathered:
            (gathered[n],) = split_wait("gather_wait_" + n, started[n], after, _gather_plan)
        return gathered[n]

    sharded_small = [conv_pw1_b, conv_dw_w[0], conv_dw_b, conv_ln_g, conv_ln_b, conv_pw2_b]
    sh_shapes = [a.shape for a in sharded_small]
    small_all = all_gather8("gather_small", _pack(sharded_small))
    per_chip = [_unpack(small_all[2 * j], sh_shapes) for j in range(N_CHIPS)]
    full = [jnp.concatenate([per_chip[j][i] for j in range(N_CHIPS)], axis=-1) for i in range(len(sharded_small))]
    P = dict(pw1_b=full[0], dw_w=full[1], dw_b=full[2], cln_g=full[3], cln_b=full[4], pw2_b=full[5],
             rel_bias=rel_bias, ln_mix_g=ln_mix_g, ln_mix_b=ln_mix_b, ln_mlp_g=ln_mlp_g, ln_mlp_b=ln_mlp_b)

    ex = GradExchange(chip1, core, shard, mom, vel)
    loss_sum, dx, g = local_step(x.reshape(T, D), loss_target.reshape(T, D), W, P, ex,
                                 first_deps=[started[n][-1] for n in BIG])
    loss = (0.5 / D) * lax.psum(loss_sum[0, 0], ("x", "y", "c"))
    grads_big = {n: ex.results[n][0] for n in BIG}

    small_names = ["pw1_b", "dw_w", "dw_b", "cln_g", "cln_b", "pw2_b", "rel_bias",
                   "ln_mix_g0", "ln_mix_g1", "ln_mix_b0", "ln_mix_b1", "ln_mlp_g0", "ln_mlp_g1", "ln_mlp_b0", "ln_mlp_b1"]
    small_grads = [g[n] for n in small_names]
    sg_shapes = [a.shape for a in small_grads]
    summed = sum_devices("small_grad_sum", all_gather8("gather_small_grads", _pack(small_grads)))
    sg = dict(zip(small_names, _unpack(summed, sg_shapes)))

    def my_cols(a, width):
        return lax.dynamic_slice_in_dim(a, chip * width, width, axis=a.ndim - 1)

    small_g = [my_cols(sg["pw1_b"], conv_pw1_b.shape[-1]),
               my_cols(sg["dw_w"], conv_dw_w.shape[-1])[None],
               my_cols(sg["dw_b"], conv_dw_b.shape[-1]), my_cols(sg["cln_g"], conv_ln_g.shape[-1]),
               my_cols(sg["cln_b"], conv_ln_b.shape[-1]), my_cols(sg["pw2_b"], conv_pw2_b.shape[-1]),
               sg["rel_bias"],
               jnp.concatenate([sg["ln_mix_g0"], sg["ln_mix_g1"]], axis=0),
               jnp.concatenate([sg["ln_mix_b0"], sg["ln_mix_b1"]], axis=0),
               jnp.concatenate([sg["ln_mlp_g0"], sg["ln_mlp_g1"]], axis=0),
               jnp.concatenate([sg["ln_mlp_b0"], sg["ln_mlp_b1"]], axis=0)]
    small_w = [conv_pw1_b, conv_dw_w, conv_dw_b, conv_ln_g, conv_ln_b, conv_pw2_b, rel_bias, ln_mix_g, ln_mix_b, ln_mlp_g, ln_mlp_b]
    small_m = [m_conv_pw1_b, m_conv_dw_w, m_conv_dw_b, m_conv_ln_g, m_conv_ln_b, m_conv_pw2_b, m_rel_bias, m_ln_mix_g, m_ln_mix_b, m_ln_mlp_g, m_ln_mlp_b]
    small_v = [v_conv_pw1_b, v_conv_dw_w, v_conv_dw_b, v_conv_ln_g, v_conv_ln_b, v_conv_pw2_b, v_rel_bias, v_ln_mix_g, v_ln_mix_b, v_ln_mlp_g, v_ln_mlp_b]
    sw_shapes = [a.shape for a in small_w]
    small_g = [a.reshape(s) for a, s in zip(small_g, sw_shapes)]
    upd_small = adamw("adamw_small", _pack(small_w), _pack(small_g), _pack(small_m), _pack(small_v))
    sd, snm, snv = (_unpack(b, sw_shapes) for b in upd_small)

    def big_out(tree):
        return dict(pw1=tree["pw1"][None], pw2=tree["pw2"][None], kv=tree["kv"], wq=tree["wq"][None], wo=tree["wo"][None],
                    w1=jnp.stack([tree["w1_0"], tree["w1_1"]]), w2=jnp.stack([tree["w2_0"], tree["w2_1"]]))

    def ordered(big, small):
        return [big["pw1"], small[0], small[1], small[2], small[3], small[4], big["pw2"], small[5], big["kv"], big["wq"],
                big["wo"], small[6], big["w1"], big["w2"], small[7], small[8], small[9], small[10]]

    grads = ordered(big_out(grads_big), small_g)
    deltas = ordered(big_out({n: ex.results[n][1] for n in BIG}), sd)
    new_m = ordered(big_out({n: ex.results[n][2] for n in BIG}), snm)
    new_v = ordered(big_out({n: ex.results[n][3] for n in BIG}), snv)
    return (loss, dx.reshape(1, T, D), *grads, *deltas, *new_m, *new_v)
```

```python
import functools
import math

import numpy as np
import jax
import jax.numpy as jnp
from jax import lax
from jax.experimental import pallas as pl
from jax.experimental.pallas import tpu as pltpu

F32 = jnp.float32
BF16 = jnp.bfloat16

HEAD_DIM = 128
BAND = 128
BRANCHES = ((128, 1), (512, 4), (2048, 16))
CONV_WIDTH = 31
CONV_HALO = 32
REL_BUCKETS = 32
REL_MAX_DIST = 2048
DEPTH = 2
ALPHA = (2 * DEPTH) ** 0.25
LN_EPS = 1e-5
ADAM_LR, ADAM_B1, ADAM_B2, ADAM_EPS, ADAM_WD, ADAM_STEP = 0.001, 0.9, 0.999, 1e-08, 0.01, 10

N_CHIPS = 4
N_DEV = 8
MESH = pl.DeviceIdType.MESH
VMEM_LIMIT_BYTES = 56 * 1024 * 1024
MM_TM, MM_TN, MM_TK = 1024, 1024, 2048
ROW_TILE = 256
CONV_TILE = 128
NEG_BIG = -1e30


def _cparams(sem):
    return pltpu.CompilerParams(dimension_semantics=sem, vmem_limit_bytes=VMEM_LIMIT_BYTES)


def _sigmoid(x):
    return 1.0 / (1.0 + jnp.exp(-x))


def _wspec(wshape, axis, br, bc, rsel, csel):
    _, R, C = wshape
    if axis == "col":
        if bc > C:
            assert bc % C == 0, (wshape, bc)
            return pl.BlockSpec((bc // C, br, C), lambda *g: (csel(*g), rsel(*g), 0))
        nb = C // bc
        assert nb * bc == C, (wshape, bc)
        return pl.BlockSpec((None, br, bc), lambda *g: (csel(*g) // nb, rsel(*g), csel(*g) % nb))
    if br > R:
        assert br % R == 0, (wshape, br)
        return pl.BlockSpec((br // R, R, bc), lambda *g: (rsel(*g), 0, csel(*g)))
    nb = R // br
    assert nb * br == R, (wshape, br)
    return pl.BlockSpec((None, br, bc), lambda *g: (rsel(*g) // nb, rsel(*g) % nb, csel(*g)))


def _join_shards(b, axis):
    if b.ndim == 2:
        return b
    if axis == "row":
        return b.reshape(b.shape[0] * b.shape[1], b.shape[2])
    return jnp.concatenate([b[s] for s in range(b.shape[0])], axis=1)


def _split_shards(r, shape, axis):
    if len(shape) == 2:
        return r
    if axis == "row":
        return r.reshape(shape)
    return jnp.stack([r[:, s * shape[2]:(s + 1) * shape[2]] for s in range(shape[0])])


def _full_dims(wshape, axis):
    _, R, C = wshape
    return (R, N_CHIPS * C) if axis == "col" else (N_CHIPS * R, C)


def _mm_body(nk, kinds, n_out, dims, epilogue, axis):
    n_extra = len(kinds)

    def body(*refs):
        a_ref, b_ref = refs[0], refs[1]
        extra = [r for r, kind in zip(refs[2:2 + n_extra], kinds) if kind != "dep"]
        outs = refs[2 + n_extra:2 + n_extra + n_out]
        part = lax.dot_general(a_ref[...].astype(BF16), _join_shards(b_ref[...], axis).astype(BF16), (dims, ((), ())),
                               preferred_element_type=F32)

        def write(res):
            for r, o in zip(res, outs):
                o[...] = _split_shards(r, o.shape, axis).astype(o.dtype)

        if nk == 1:
            write(epilogue(part, *[e[...] for e in extra]))
            return
        acc_ref = refs[2 + n_extra + n_out]
        k = pl.program_id(2)

        @pl.when(k == 0)
        def _():
            acc_ref[...] = part

        @pl.when(k > 0)
        def _():
            acc_ref[...] += part

        @pl.when(k == nk - 1)
        def _():
            write(epilogue(acc_ref[...], *[e[...] for e in extra]))
    return body


def _extra_specs(extras, tm, tn):
    specs = []
    for arr, kind in extras:
        if kind == "tile":
            specs.append(pl.BlockSpec((tm, tn), lambda i, j, k: (i, j)))
        elif kind == "dep":
            specs.append(pl.BlockSpec(arr.shape, lambda i, j, k: (0, 0)))
        else:
            specs.append(pl.BlockSpec((1, tn), lambda i, j, k: (0, j)))
    return specs


def mm_nn(name, a, w, axis, epilogue, out_dtypes, extras=()):
    M, K = a.shape
    Kw, N = _full_dims(w.shape, axis)
    assert K == Kw
    tm, tn, tk = min(MM_TM, M), min(MM_TN, N), min(MM_TK, K)
    nk = K // tk
    in_specs = [pl.BlockSpec((tm, tk), lambda i, j, k: (i, k)),
                _wspec(w.shape, axis, tk, tn, lambda i, j, k: k, lambda i, j, k: j)]
    in_specs += _extra_specs(extras, tm, tn)
    body = _mm_body(nk, [kind for _, kind in extras], len(out_dtypes), ((1,), (0,)), epilogue, axis)
    return pl.pallas_call(
        body, name=name, grid=(M // tm, N // tn, nk), in_specs=in_specs,
        out_specs=[pl.BlockSpec((tm, tn), lambda i, j, k: (i, j)) for _ in out_dtypes],
        out_shape=[jax.ShapeDtypeStruct((M, N), d) for d in out_dtypes],
        scratch_shapes=[pltpu.VMEM((tm, tn), F32)] if nk > 1 else [],
        compiler_params=_cparams(("parallel", "parallel", "arbitrary")),
    )(a, w, *[e for e, _ in extras])


def mm_nt(name, g, w, axis, epilogue, out_dtypes, extras=()):
    M, N = g.shape
    K, Nw = _full_dims(w.shape, axis)
    assert N == Nw
    tm, tn, tk = min(MM_TM, M), min(MM_TN, K), min(MM_TK, N)
    nk = N // tk
    in_specs = [pl.BlockSpec((tm, tk), lambda i, j, k: (i, k)),
                _wspec(w.shape, axis, tn, tk, lambda i, j, k: j, lambda i, j, k: k)]
    in_specs += _extra_specs(extras, tm, tn)
    body = _mm_body(nk, [kind for _, kind in extras], len(out_dtypes), ((1,), (1,)), epilogue, axis)
    return pl.pallas_call(
        body, name=name, grid=(M // tm, K // tn, nk), in_specs=in_specs,
        out_specs=[pl.BlockSpec((tm, tn), lambda i, j, k: (i, j)) for _ in out_dtypes],
        out_shape=[jax.ShapeDtypeStruct((M, K), d) for d in out_dtypes],
        scratch_shapes=[pltpu.VMEM((tm, tn), F32)] if nk > 1 else [],
        compiler_params=_cparams(("parallel", "parallel", "arbitrary")),
    )(g, w, *[e for e, _ in extras])


def mm_tn(name, a, g, wshape, axis, deps=()):
    M, K = a.shape
    Mg, N = g.shape
    assert M == Mg and (K, N) == _full_dims(wshape, axis)
    tm, tn, tk = min(MM_TM, K), min(MM_TN, N), min(MM_TK, M)
    nk = M // tk
    body = _mm_body(nk, ["dep"] * len(deps), 1, ((0,), (0,)), lambda acc: (acc,), axis)
    return pl.pallas_call(
        body, name=name, grid=(K // tm, N // tn, nk),
        in_specs=[pl.BlockSpec((tk, tm), lambda i, j, k: (k, i)),
                  pl.BlockSpec((tk, tn), lambda i, j, k: (k, j))] + _extra_specs([(d, "dep") for d in deps], tm, tn),
        out_specs=[_wspec(wshape, axis, tm, tn, lambda i, j, k: i, lambda i, j, k: j)],
        out_shape=[jax.ShapeDtypeStruct(wshape, F32)],
        scratch_shapes=[pltpu.VMEM((tm, tn), F32)] if nk > 1 else [],
        compiler_params=_cparams(("parallel", "parallel", "arbitrary")),
    )(a, g, *deps)[0]


def _row_spec(tr, width):
    return pl.BlockSpec((tr, width), lambda i: (i, 0))


def _vec_spec(width):
    return pl.BlockSpec((1, width), lambda i: (0, 0))


def _fold8(x):
    r, d = x.shape
    return jnp.sum(x.reshape(r // 8, 8, d), axis=0)


def ln_fwd(name, f, prev, prev_g=None, prev_b=None):
    T, D = f.shape
    tr = min(ROW_TILE, T)
    affine = prev_g is not None

    def body(*refs):
        if affine:
            f_ref, p_ref, pg_ref, pb_ref, g_ref, b_ref, xhat_ref, rstd_ref, xbf_ref = refs
            xprev = p_ref[...] * pg_ref[...] + pb_ref[...]
        else:
            f_ref, p_ref, g_ref, b_ref, xhat_ref, rstd_ref, xbf_ref = refs
            xprev = p_ref[...]
        r = ALPHA * xprev + f_ref[...]
        mu = jnp.mean(r, axis=-1, keepdims=True)
        cen = r - mu
        var = jnp.mean(cen * cen, axis=-1, keepdims=True)
        rstd = lax.rsqrt(var + LN_EPS)
        xhat = cen * rstd
        xhat_ref[...] = xhat
        rstd_ref[...] = rstd
        xbf_ref[...] = (xhat * g_ref[...] + b_ref[...]).astype(BF16)

    def call(g, b):
        ins = [f, prev] + ([prev_g, prev_b] if affine else []) + [g, b]
        specs = [_row_spec(tr, D), _row_spec(tr, D)] + ([_vec_spec(D)] * 2 if affine else []) + [_vec_spec(D)] * 2
        return pl.pallas_call(
            body, name=name, grid=(T // tr,), in_specs=specs,
            out_specs=[_row_spec(tr, D), _row_spec(tr, 1), _row_spec(tr, D)],
            out_shape=[jax.ShapeDtypeStruct((T, D), F32), jax.ShapeDtypeStruct((T, 1), F32),
                       jax.ShapeDtypeStruct((T, D), BF16)],
            compiler_params=_cparams(("parallel",)),
        )(*ins)
    return call


def ln_bwd(name, xhat, rstd, gamma, dy=None, target=None, beta=None):
    T, D = xhat.shape
    tr = min(ROW_TILE, T)
    nt = T // tr
    head = target is not None

    def body(*refs):
        if head:
            xhat_ref, rstd_ref, g_ref, tgt_ref, b_ref = refs[:5]
            outs = refs[5:]
        else:
            xhat_ref, rstd_ref, g_ref, dy_ref = refs[:4]
            outs = refs[4:]
        dr_ref, drbf_ref, dg_ref, db_ref, cs_ref = outs[:5]
        rest = outs[5:]
        if head:
            loss_ref, acc_ref = rest
        else:
            (acc_ref,) = rest
        i = pl.program_id(0)
        xhat_v = xhat_ref[...]
        gam = g_ref[...]
        if head:
            diff = xhat_v * gam + b_ref[...] - tgt_ref[...]
            dyv = diff * (1.0 / D)
        else:
            dyv = dy_ref[...]
        dxh = dyv * gam
        m1 = jnp.mean(dxh, axis=-1, keepdims=True)
        m2 = jnp.mean(dxh * xhat_v, axis=-1, keepdims=True)
        dr = rstd_ref[...] * (dxh - m1 - xhat_v * m2)
        dr_ref[...] = dr
        drbf_ref[...] = dr.astype(BF16)

        @pl.when(i == 0)
        def _():
            acc_ref[...] = jnp.zeros_like(acc_ref)

        acc_ref[0] += _fold8(dyv * xhat_v)
        acc_ref[1] += _fold8(dyv)
        acc_ref[2] += _fold8(dr)
        if head:
            acc_ref[3] += _fold8(diff * diff)

        @pl.when(i == nt - 1)
        def _():
            dg_ref[...] = jnp.sum(acc_ref[0], axis=0, keepdims=True)
            db_ref[...] = jnp.sum(acc_ref[1], axis=0, keepdims=True)
            cs_ref[...] = jnp.sum(acc_ref[2], axis=0, keepdims=True)
            if head:
                loss_ref[...] = jnp.sum(jnp.sum(acc_ref[3], axis=0, keepdims=True), axis=1, keepdims=True)

    ins = [xhat, rstd, gamma] + ([target, beta] if head else [dy])
    specs = [_row_spec(tr, D), _row_spec(tr, 1), _vec_spec(D)] + ([_row_spec(tr, D), _vec_spec(D)] if head else [_row_spec(tr, D)])
    out_specs = [_row_spec(tr, D), _row_spec(tr, D), _vec_spec(D), _vec_spec(D), _vec_spec(D)]
    out_shape = [jax.ShapeDtypeStruct((T, D), F32), jax.ShapeDtypeStruct((T, D), BF16)] + [jax.ShapeDtypeStruct((1, D), F32)] * 3
    if head:
        out_specs.append(pl.BlockSpec((1, 1), lambda i: (0, 0)))
        out_shape.append(jax.ShapeDtypeStruct((1, 1), F32))
    return pl.pallas_call(
        body, name=name, grid=(nt,), in_specs=specs, out_specs=out_specs, out_shape=out_shape,
        scratch_shapes=[pltpu.VMEM((4, 8, D), F32)],
        compiler_params=_cparams(("arbitrary",)),
    )(*ins)


CONV_ROWS, CONV_COLS = 64, 512


def _tap_chunks(tt, D):
    for r0 in range(0, tt, min(CONV_ROWS, tt)):
        for c0 in range(0, D, min(CONV_COLS, D)):
            yield r0, min(CONV_ROWS, tt), c0, min(CONV_COLS, D)


SUBLANES = 8


def _shifted_copies(ext_ref, sh_ref):
    n = sh_ref.shape[1]
    for b in range(1, SUBLANES):
        sh_ref[b - 1] = ext_ref[pl.ds(b, n), :]


def _rows_at(ext_ref, sh_ref, off, nr, cols):
    a, b = divmod(off, SUBLANES)
    if b == 0:
        return ext_ref[pl.ds(off, nr), cols]
    return sh_ref[b - 1, pl.ds(a * SUBLANES, nr), cols]


def conv_fwd(name, h1, dw, dwb, lng, lnb):
    T, D2 = h1.shape
    D = D2 // 2
    tt = min(CONV_TILE, T)
    hb = tt // CONV_HALO
    KW = dw.shape[0]
    lead = CONV_HALO - (KW - 1)

    def body(a_ref, g_ref, ah_ref, gh_ref, dw_ref, dwb_ref, lng_ref, lnb_ref, u_ref, c_ref, s_ref, ext_ref, sh_ref):
        i = pl.program_id(0)
        u = a_ref[...] * _sigmoid(g_ref[...])
        u_ref[...] = u
        uh = ah_ref[...] * _sigmoid(gh_ref[...])
        ext_ref[pl.ds(0, CONV_HALO), :] = jnp.where(i > 0, uh, 0.0)
        ext_ref[pl.ds(CONV_HALO, tt), :] = u
        _shifted_copies(ext_ref, sh_ref)
        for r0, nr, c0, nc in _tap_chunks(tt, D):
            cols = pl.ds(c0, nc)
            acc = jnp.zeros((nr, nc), F32) + dwb_ref[:, cols]
            for k in range(KW):
                acc = acc + dw_ref[pl.ds(k, 1), cols] * _rows_at(ext_ref, sh_ref, r0 + lead + k, nr, cols)
            c_ref[pl.ds(r0, nr), cols] = acc
        c = c_ref[...]
        mu = jnp.mean(c, axis=-1, keepdims=True)
        cen = c - mu
        var = jnp.mean(cen * cen, axis=-1, keepdims=True)
        n = cen * lax.rsqrt(var + LN_EPS) * lng_ref[...] + lnb_ref[...]
        s_ref[...] = (n * _sigmoid(n)).astype(BF16)

    halo = lambda col: pl.BlockSpec((CONV_HALO, D), lambda i: (jnp.maximum(i * hb - 1, 0), col))
    return pl.pallas_call(
        body, name=name, grid=(T // tt,),
        in_specs=[pl.BlockSpec((tt, D), lambda i: (i, 0)), pl.BlockSpec((tt, D), lambda i: (i, 1)), halo(0), halo(1),
                  pl.BlockSpec((KW, D), lambda i: (0, 0)), _vec_spec(D), _vec_spec(D), _vec_spec(D)],
        out_specs=[_row_spec(tt, D)] * 3,
        out_shape=[jax.ShapeDtypeStruct((T, D), F32), jax.ShapeDtypeStruct((T, D), F32), jax.ShapeDtypeStruct((T, D), BF16)],
        scratch_shapes=[pltpu.VMEM((tt + CONV_HALO, D), F32),
                        pltpu.VMEM((SUBLANES - 1, tt + CONV_HALO - SUBLANES, D), F32)],
        compiler_params=_cparams(("parallel",)),
    )(h1, h1, h1, h1, dw, dwb, lng, lnb)


def conv_bwd_ln(name, ds, c, lng, lnb):
    T, D = c.shape
    tr = min(ROW_TILE, T)
    nt = T // tr

    def body(ds_ref, c_ref, g_ref, b_ref, dc_ref, dg_ref, db_ref, cs_ref, acc_ref):
        i = pl.program_id(0)
        cv = c_ref[...]
        mu = jnp.mean(cv, axis=-1, keepdims=True)
        cen = cv - mu
        var = jnp.mean(cen * cen, axis=-1, keepdims=True)
        rstd = lax.rsqrt(var + LN_EPS)
        chat = cen * rstd
        n = chat * g_ref[...] + b_ref[...]
        sg = _sigmoid(n)
        dn = ds_ref[...] * (sg * (1.0 + n * (1.0 - sg)))
        dxh = dn * g_ref[...]
        m1 = jnp.mean(dxh, axis=-1, keepdims=True)
        m2 = jnp.mean(dxh * chat, axis=-1, keepdims=True)
        dc = rstd * (dxh - m1 - chat * m2)
        dc_ref[...] = dc

        @pl.when(i == 0)
        def _():
            acc_ref[...] = jnp.zeros_like(acc_ref)

        acc_ref[0] += _fold8(dn * chat)
        acc_ref[1] += _fold8(dn)
        acc_ref[2] += _fold8(dc)

        @pl.when(i == nt - 1)
        def _():
            dg_ref[...] = jnp.sum(acc_ref[0], axis=0, keepdims=True)
            db_ref[...] = jnp.sum(acc_ref[1], axis=0, keepdims=True)
            cs_ref[...] = jnp.sum(acc_ref[2], axis=0, keepdims=True)

    return pl.pallas_call(
        body, name=name, grid=(nt,),
        in_specs=[_row_spec(tr, D), _row_spec(tr, D), _vec_spec(D), _vec_spec(D)],
        out_specs=[_row_spec(tr, D), _vec_spec(D), _vec_spec(D), _vec_spec(D)],
        out_shape=[jax.ShapeDtypeStruct((T, D), F32)] + [jax.ShapeDtypeStruct((1, D), F32)] * 3,
        scratch_shapes=[pltpu.VMEM((3, 8, D), F32)],
        compiler_params=_cparams(("arbitrary",)),
    )(ds, c, lng, lnb)


def conv_bwd_taps(name, dc, u, h1, dw):
    T, D = dc.shape
    tt = min(CONV_TILE, T)
    nt = T // tt
    hb = tt // CONV_HALO
    nhb = T // CONV_HALO
    KW = dw.shape[0]
    lead = CONV_HALO - (KW - 1)

    def body(dc_ref, dcn_ref, u_ref, uh_ref, a_ref, g_ref, dw_ref, dh1_ref, db1_ref, ddw_ref,
             edc_ref, eu_ref, du_ref, accw_ref, accb_ref, shdc_ref, shu_ref):
        i = pl.program_id(0)

        @pl.when(i == 0)
        def _():
            accw_ref[...] = jnp.zeros_like(accw_ref)
            accb_ref[...] = jnp.zeros_like(accb_ref)

        edc_ref[pl.ds(0, tt), :] = dc_ref[...]
        edc_ref[pl.ds(tt, CONV_HALO), :] = jnp.where(i < nt - 1, dcn_ref[...], 0.0)
        eu_ref[pl.ds(0, CONV_HALO), :] = jnp.where(i > 0, uh_ref[...], 0.0)
        eu_ref[pl.ds(CONV_HALO, tt), :] = u_ref[...]
        _shifted_copies(edc_ref, shdc_ref)
        _shifted_copies(eu_ref, shu_ref)
        for r0, nr, c0, nc in _tap_chunks(tt, D):
            cols = pl.ds(c0, nc)
            dcv = dc_ref[pl.ds(r0, nr), cols]
            acc = jnp.zeros((nr, nc), F32)
            for k in range(KW):
                acc = acc + dw_ref[pl.ds(k, 1), cols] * _rows_at(edc_ref, shdc_ref, r0 + (KW - 1) - k, nr, cols)
                accw_ref[k, :, cols] += _fold8(dcv * _rows_at(eu_ref, shu_ref, r0 + lead + k, nr, cols))
            du_ref[pl.ds(r0, nr), cols] = acc
        du = du_ref[...]
        sg = _sigmoid(g_ref[...])
        da = du * sg
        dg = du * a_ref[...] * sg * (1.0 - sg)
        dh1_ref[:, pl.ds(0, D)] = da.astype(BF16)
        dh1_ref[:, pl.ds(D, D)] = dg.astype(BF16)
        accb_ref[:, pl.ds(0, D)] += _fold8(da)
        accb_ref[:, pl.ds(D, D)] += _fold8(dg)

        @pl.when(i == nt - 1)
        def _():
            db1_ref[...] = jnp.sum(accb_ref[...], axis=0, keepdims=True)
            ddw_ref[...] = jnp.sum(accw_ref[...], axis=1)

    return pl.pallas_call(
        body, name=name, grid=(nt,),
        in_specs=[_row_spec(tt, D),
                  pl.BlockSpec((CONV_HALO, D), lambda i: (jnp.minimum((i + 1) * hb, nhb - 1), 0)),
                  _row_spec(tt, D),
                  pl.BlockSpec((CONV_HALO, D), lambda i: (jnp.maximum(i * hb - 1, 0), 0)),
                  pl.BlockSpec((tt, D), lambda i: (i, 0)), pl.BlockSpec((tt, D), lambda i: (i, 1)),
                  pl.BlockSpec((KW, D), lambda i: (0, 0))],
        out_specs=[_row_spec(tt, 2 * D), _vec_spec(2 * D), pl.BlockSpec((KW, D), lambda i: (0, 0))],
        out_shape=[jax.ShapeDtypeStruct((T, 2 * D), BF16), jax.ShapeDtypeStruct((1, 2 * D), F32),
                   jax.ShapeDtypeStruct((KW, D), F32)],
        scratch_shapes=[pltpu.VMEM((tt + CONV_HALO, D), F32), pltpu.VMEM((tt + CONV_HALO, D), F32),
                        pltpu.VMEM((tt, D), F32), pltpu.VMEM((KW, 8, D), F32), pltpu.VMEM((8, 2 * D), F32)]
                       + [pltpu.VMEM((SUBLANES - 1, tt + CONV_HALO - SUBLANES, D), F32)] * 2,
        compiler_params=_cparams(("arbitrary",)),
    )(dc, dc, u, u, h1, h1, dw)


def _t5_bucket(dist):
    max_exact = REL_BUCKETS // 2
    large = max_exact + (np.log(np.maximum(dist, 1) / max_exact) / math.log(REL_MAX_DIST / max_exact)
                         * (REL_BUCKETS - max_exact)).astype(np.int32)
    large = np.minimum(large, REL_BUCKETS - 1)
    return np.where(dist < max_exact, dist, large).astype(np.int32)


def _bucket_table(dil):
    i = np.arange(BAND)[:, None]
    j = np.arange(2 * BAND)[None, :]
    delta = i - j + BAND
    return _t5_bucket(np.clip(delta, 0, None) * dil)


def bias_expand(name, rel_bias, dil):
    n_heads = rel_bias.shape[1]
    idx = jnp.asarray(_bucket_table(dil))

    def body(rel_ref, idx_ref, out_ref):
        h = pl.program_id(0)
        idxv = idx_ref[...]
        b = jnp.zeros((BAND, 2 * BAND), F32)
        for bk in range(REL_BUCKETS):
            b = jnp.where(idxv == bk, rel_ref[bk, h], b)
        out_ref[...] = b

    return pl.pallas_call(
        body, name=name, grid=(n_heads,),
        in_specs=[pl.BlockSpec(memory_space=pltpu.SMEM), pl.BlockSpec((BAND, 2 * BAND), lambda h: (0, 0))],
        out_specs=pl.BlockSpec((None, BAND, 2 * BAND), lambda h: (h, 0, 0)),
        out_shape=jax.ShapeDtypeStruct((n_heads, BAND, 2 * BAND), F32),
        compiler_params=_cparams(("arbitrary",)),
    )(rel_bias, idx)


def relbias_grad(name, dsb_list):
    n_heads = dsb_list[0].shape[0]
    idxs = [jnp.asarray(_bucket_table(d)) for _, d in BRANCHES]
    nb = len(BRANCHES)

    def body(*refs):
        ds_refs, idx_refs, out_ref = refs[:nb], refs[nb:2 * nb], refs[2 * nb]
        lane = lax.broadcasted_iota(jnp.int32, (1, 128), 1)
        row = jnp.zeros((1, 128), F32)
        for bk in range(REL_BUCKETS):
            tot = jnp.zeros((1, 1), F32)
            for ds_ref, idx_ref in zip(ds_refs, idx_refs):
                sel = jnp.where(idx_ref[...] == bk, ds_ref[...], 0.0)
                tot = tot + jnp.sum(jnp.sum(sel, axis=0, keepdims=True), axis=1, keepdims=True)
            row = jnp.where(lane == bk, tot, row)
        out_ref[...] = row

    return pl.pallas_call(
        body, name=name, grid=(n_heads,),
        in_specs=[pl.BlockSpec((None, BAND, 2 * BAND), lambda h: (h, 0, 0))] * nb
                 + [pl.BlockSpec((BAND, 2 * BAND), lambda h: (0, 0))] * nb,
        out_specs=pl.BlockSpec((None, 1, 128), lambda h: (h, 0, 0)),
        out_shape=jax.ShapeDtypeStruct((n_heads, 1, 128), F32),
        compiler_params=_cparams(("arbitrary",)),
    )(*dsb_list, *idxs)


def _band_mask():
    i = lax.broadcasted_iota(jnp.int32, (BAND, 2 * BAND), 0)
    j = lax.broadcasted_iota(jnp.int32, (BAND, 2 * BAND), 1)
    return (j >= i) & (j <= i + BAND), j


def _rep2(x):
    return jnp.concatenate([x, x], axis=1)


ATTN_TOKENS = 2048
MERGE_ROWS = 256


def _rows(ref, start, n, dil):
    if dil == 1:
        return ref[pl.ds(start, n), :]
    return ref[pl.ds(start, n, stride=dil), :]


def _set_rows(ref, start, n, dil, val):
    if dil == 1:
        ref[pl.ds(start, n), :] = val
    else:
        ref[pl.ds(start, n, stride=dil), :] = val


def _attn_specs(ct, n_heads, n_chunks):
    cur = lambda col0: pl.BlockSpec((ct, HEAD_DIM), lambda h, c: (c, col0 + h))
    prev = lambda col0: pl.BlockSpec((ct, HEAD_DIM), lambda h, c: (jnp.maximum(c - 1, 0), col0 + h))
    nxt = lambda col0: pl.BlockSpec((ct, HEAD_DIM), lambda h, c: (jnp.minimum(c + 1, n_chunks - 1), col0 + h))
    bias = pl.BlockSpec((None, BAND, 2 * BAND), lambda h, c: (h, 0, 0))
    return cur, prev, nxt, bias


def _load_keys(kext_ref, vext_ref, k_ref, v_ref, kp_ref, vp_ref, r, dil, ct):
    lc = ct // dil
    kext_ref[pl.ds(0, BAND), :] = _rows(kp_ref, ct - BAND * dil + r, BAND, dil).astype(BF16)
    vext_ref[pl.ds(0, BAND), :] = _rows(vp_ref, ct - BAND * dil + r, BAND, dil).astype(BF16)
    kext_ref[pl.ds(BAND, lc), :] = _rows(k_ref, r, lc, dil).astype(BF16)
    vext_ref[pl.ds(BAND, lc), :] = _rows(v_ref, r, lc, dil).astype(BF16)


def _window_mask(band, jcol, a, c):
    if a > 0:
        return band
    return band & jnp.logical_or(jcol >= BAND, c > 0)


def attn_fwd(name, q, kv, biases):
    T, D = q.shape
    n_heads = D // HEAD_DIM
    ct = min(ATTN_TOKENS, T)
    n_chunks = T // ct
    nbr = len(BRANCHES)
    scale = HEAD_DIM ** -0.5
    nt_dims = (((1,), (1,)), ((), ()))
    nn_dims = (((1,), (0,)), ((), ()))

    def body(q_ref, k_ref, v_ref, kp_ref, vp_ref, *rest):
        b_refs = rest[:nbr]
        o_ref, obf_ref, lse_ref = rest[nbr:nbr + 3]
        kext_ref, vext_ref, acc_ref, m_ref, l_ref = rest[nbr + 3:]
        c = pl.program_id(1)
        band, jcol = _band_mask()
        for bi, (win, dil) in enumerate(BRANCHES):
            lc = ct // dil
            bias_v = b_refs[bi][...]

            def residue(r, carry, bi=bi, dil=dil, lc=lc, bias_v=bias_v):
                _load_keys(kext_ref, vext_ref, k_ref, v_ref, kp_ref, vp_ref, r, dil, ct)
                for a in range(lc // BAND):
                    tok = r + a * BAND * dil
                    qa = _rows(q_ref, tok, BAND, dil).astype(BF16)
                    kw = kext_ref[pl.ds(a * BAND, 2 * BAND), :]
                    vw = vext_ref[pl.ds(a * BAND, 2 * BAND), :]
                    s = lax.dot_general(qa, kw, nt_dims, preferred_element_type=F32) * scale + bias_v
                    s = jnp.where(_window_mask(band, jcol, a, c), s, NEG_BIG)
                    m = jnp.max(s, axis=-1, keepdims=True)
                    p = jnp.exp(s - m)
                    den = jnp.sum(p, axis=-1, keepdims=True)
                    pv = lax.dot_general(p.astype(BF16), vw, nn_dims, preferred_element_type=F32)
                    _set_rows(acc_ref.at[bi], tok, BAND, dil, pv)
                    _set_rows(m_ref.at[bi], tok, BAND, dil, jnp.broadcast_to(m, (BAND, HEAD_DIM)))
                    _set_rows(l_ref.at[bi], tok, BAND, dil, jnp.broadcast_to(den, (BAND, HEAD_DIM)))
                return carry

            if dil == 1:
                residue(0, 0)
            else:
                lax.fori_loop(0, dil, residue, 0)

        def merge(i, carry):
            rows = pl.ds(pl.multiple_of(i * MERGE_ROWS, MERGE_ROWS), MERGE_ROWS)
            ms = [m_ref[bi, rows, :] for bi in range(nbr)]
            m = functools.reduce(jnp.maximum, ms)
            ws = [jnp.exp(mb - m) for mb in ms]
            tot = functools.reduce(lambda x, y: x + y, [w * l_ref[bi, rows, :] for bi, w in enumerate(ws)])
            o = functools.reduce(lambda x, y: x + y, [w * acc_ref[bi, rows, :] for bi, w in enumerate(ws)]) / tot
            o_ref[rows, :] = o
            obf_ref[rows, :] = o.astype(BF16)
            lse_ref[rows, :] = m + jnp.log(tot)
            return carry

        lax.fori_loop(0, ct // min(MERGE_ROWS, ct), merge, 0)

    cur, prev, nxt, bias = _attn_specs(ct, n_heads, n_chunks)
    lmax = ct + BAND
    return pl.pallas_call(
        body, name=name, grid=(n_heads, n_chunks),
        in_specs=[cur(0), cur(0), cur(n_heads), prev(0), prev(n_heads)] + [bias] * nbr,
        out_specs=[cur(0)] * 3,
        out_shape=[jax.ShapeDtypeStruct((T, D), F32), jax.ShapeDtypeStruct((T, D), BF16), jax.ShapeDtypeStruct((T, D), F32)],
        scratch_shapes=[pltpu.VMEM((lmax, HEAD_DIM), BF16)] * 2 + [pltpu.VMEM((nbr, ct, HEAD_DIM), F32)] * 3,
        compiler_params=_cparams(("arbitrary", "arbitrary")),
    )(q, kv, kv, kv, kv, *biases)


def attn_bwd(name, q, kv, do, o, lse, biases):
    T, D = q.shape
    n_heads = D // HEAD_DIM
    ct = min(ATTN_TOKENS, T)
    n_chunks = T // ct
    nbr = len(BRANCHES)
    scale = HEAD_DIM ** -0.5
    nt_dims = (((1,), (1,)), ((), ()))
    tn_dims = (((0,), (0,)), ((), ()))
    nn_dims = (((1,), (0,)), ((), ()))
    mrows = min(MERGE_ROWS, ct)

    def body(q_ref, k_ref, v_ref, do_ref, o_ref, lse_ref, kp_ref, vp_ref, qn_ref, don_ref, on_ref, lsen_ref, *rest):
        b_refs = rest[:nbr]
        dq_ref, dk_ref, dv_ref = rest[nbr:nbr + 3]
        dsb_refs = rest[nbr + 3:2 * nbr + 3]
        kext_ref, vext_ref, dkext_ref, dvext_ref, dqa_ref, dka_ref, dva_ref, dsum_ref, dsumn_ref = rest[2 * nbr + 3:]
        c = pl.program_id(1)
        band, jcol = _band_mask()

        def prep(i, carry):
            rows = pl.ds(pl.multiple_of(i * mrows, mrows), mrows)
            dsum_ref[rows, :] = jnp.broadcast_to(jnp.sum(do_ref[rows, :] * o_ref[rows, :], axis=-1, keepdims=True), (mrows, HEAD_DIM))
            dsumn_ref[rows, :] = jnp.broadcast_to(jnp.sum(don_ref[rows, :] * on_ref[rows, :], axis=-1, keepdims=True), (mrows, HEAD_DIM))
            dqa_ref[rows, :] = jnp.zeros((mrows, HEAD_DIM), F32)
            dka_ref[rows, :] = jnp.zeros((mrows, HEAD_DIM), F32)
            dva_ref[rows, :] = jnp.zeros((mrows, HEAD_DIM), F32)
            return carry

        lax.fori_loop(0, ct // mrows, prep, 0)

        @pl.when(c == 0)
        def _():
            for r in dsb_refs:
                r[...] = jnp.zeros_like(r)

        for bi, (win, dil) in enumerate(BRANCHES):
            lc = ct // dil
            bias_v = b_refs[bi][...]
            dsb_ref = dsb_refs[bi]

            def residue(r, carry, dil=dil, lc=lc, bias_v=bias_v, dsb_ref=dsb_ref):
                _load_keys(kext_ref, vext_ref, k_ref, v_ref, kp_ref, vp_ref, r, dil, ct)
                dkext_ref[pl.ds(0, BAND + lc), :] = jnp.zeros((BAND + lc, HEAD_DIM), F32)
                dvext_ref[pl.ds(0, BAND + lc), :] = jnp.zeros((BAND + lc, HEAD_DIM), F32)
                for a in range(lc // BAND):
                    tok = r + a * BAND * dil
                    qa = _rows(q_ref, tok, BAND, dil).astype(BF16)
                    doa = _rows(do_ref, tok, BAND, dil).astype(BF16)
                    kw = kext_ref[pl.ds(a * BAND, 2 * BAND), :]
                    vw = vext_ref[pl.ds(a * BAND, 2 * BAND), :]
                    s = lax.dot_general(qa, kw, nt_dims, preferred_element_type=F32) * scale + bias_v
                    p = jnp.where(_window_mask(band, jcol, a, c), jnp.exp(s - _rep2(_rows(lse_ref, tok, BAND, dil))), 0.0)
                    dp = lax.dot_general(doa, vw, nt_dims, preferred_element_type=F32)
                    ds = p * (dp - _rep2(_rows(dsum_ref, tok, BAND, dil)))
                    dsb_ref[...] += ds
                    dsb16 = ds.astype(BF16)
                    dqw = lax.dot_general(dsb16, kw, nn_dims, preferred_element_type=F32) * scale
                    _set_rows(dqa_ref, tok, BAND, dil, _rows(dqa_ref, tok, BAND, dil) + dqw)
                    dkext_ref[pl.ds(a * BAND, 2 * BAND), :] += lax.dot_general(dsb16, qa, tn_dims, preferred_element_type=F32) * scale
                    dvext_ref[pl.ds(a * BAND, 2 * BAND), :] += lax.dot_general(p.astype(BF16), doa, tn_dims, preferred_element_type=F32)

                @pl.when(c < n_chunks - 1)
                def _():
                    qn = _rows(qn_ref, r, BAND, dil).astype(BF16)
                    don = _rows(don_ref, r, BAND, dil).astype(BF16)
                    kl = kext_ref[pl.ds(lc, BAND), :]
                    vl = vext_ref[pl.ds(lc, BAND), :]
                    s = lax.dot_general(qn, kl, nt_dims, preferred_element_type=F32) * scale + bias_v[:, :BAND]
                    p = jnp.where(band[:, :BAND], jnp.exp(s - _rows(lsen_ref, r, BAND, dil)), 0.0)
                    dp = lax.dot_general(don, vl, nt_dims, preferred_element_type=F32)
                    ds = p * (dp - _rows(dsumn_ref, r, BAND, dil))
                    dkext_ref[pl.ds(lc, BAND), :] += lax.dot_general(ds.astype(BF16), qn, tn_dims, preferred_element_type=F32) * scale
                    dvext_ref[pl.ds(lc, BAND), :] += lax.dot_general(p.astype(BF16), don, tn_dims, preferred_element_type=F32)

                _set_rows(dka_ref, r, lc, dil, _rows(dka_ref, r, lc, dil) + dkext_ref[pl.ds(BAND, lc), :])
                _set_rows(dva_ref, r, lc, dil, _rows(dva_ref, r, lc, dil) + dvext_ref[pl.ds(BAND, lc), :])
                return carry

            if dil == 1:
                residue(0, 0)
            else:
                lax.fori_loop(0, dil, residue, 0)

        dq_ref[...] = dqa_ref[...].astype(BF16)
        dk_ref[...] = dka_ref[...].astype(BF16)
        dv_ref[...] = dva_ref[...].astype(BF16)

    cur, prev, nxt, bias = _attn_specs(ct, n_heads, n_chunks)
    lmax = ct + BAND
    res = pl.pallas_call(
        body, name=name, grid=(n_heads, n_chunks),
        in_specs=[cur(0), cur(0), cur(n_heads), cur(0), cur(0), cur(0), prev(0), prev(n_heads), nxt(0), nxt(0), nxt(0), nxt(0)]
                 + [bias] * nbr,
        out_specs=[cur(0)] * 3 + [bias] * nbr,
        out_shape=[jax.ShapeDtypeStruct((T, D), BF16)] * 3 + [jax.ShapeDtypeStruct((n_heads, BAND, 2 * BAND), F32)] * nbr,
        scratch_shapes=[pltpu.VMEM((lmax, HEAD_DIM), BF16)] * 2 + [pltpu.VMEM((lmax, HEAD_DIM), F32)] * 2
                       + [pltpu.VMEM((ct, HEAD_DIM), F32)] * 5,
        compiler_params=_cparams(("arbitrary", "arbitrary")),
    )(q, kv, kv, do, o, lse, kv, kv, q, do, o, lse, *biases)
    return res[0], res[1], res[2], list(res[3:])


def _divisor_tile(n, cap, mult):
    if n <= cap:
        return n
    t = cap - cap % mult
    while n % t:
        t -= mult
    return t


def _tile2(R, C):
    return _divisor_tile(R, 512, 8), _divisor_tile(C, 1024, 128)


def half_cast(name, dw, core):
    S, R, C = dw.shape
    hr = R // 2
    tr, tc = _tile2(hr, C)
    nrb = hr // tr

    def body(c_ref, x_ref, o_ref):
        o_ref[...] = x_ref[...].astype(BF16)

    return pl.pallas_call(
        body, name=name,
        grid_spec=pltpu.PrefetchScalarGridSpec(
            num_scalar_prefetch=1, grid=(S, nrb, C // tc),
            in_specs=[pl.BlockSpec((None, tr, tc), lambda s, i, j, c: (s, (1 - c[0]) * nrb + i, j))],
            out_specs=pl.BlockSpec((None, tr, tc), lambda s, i, j, c: (s, i, j))),
        out_shape=jax.ShapeDtypeStruct((S, hr, C), BF16),
        compiler_params=_cparams(("parallel", "parallel", "parallel")),
    )(core, dw)


def pair_sum(name, dw, recv, core):
    S, R, C = dw.shape
    hr = R // 2
    tr, tc = _tile2(hr, C)
    nrb = hr // tr

    def body(c_ref, x_ref, r_ref, p_ref, pbf_ref):
        p = x_ref[...] + r_ref[...].astype(F32)
        p_ref[...] = p
        pbf_ref[...] = p.astype(BF16)

    out = pl.BlockSpec((None, tr, tc), lambda s, i, j, c: (s, i, j))
    return pl.pallas_call(
        body, name=name,
        grid_spec=pltpu.PrefetchScalarGridSpec(
            num_scalar_prefetch=1, grid=(S, nrb, C // tc),
            in_specs=[pl.BlockSpec((None, tr, tc), lambda s, i, j, c: (s, c[0] * nrb + i, j)), out],
            out_specs=[out, out]),
        out_shape=[jax.ShapeDtypeStruct((S, hr, C), F32), jax.ShapeDtypeStruct((S, hr, C), BF16)],
        compiler_params=_cparams(("parallel", "parallel", "parallel")),
    )(core, dw, recv)


def chip_sum(name, p, recv, chip, core):
    S, hr, C = p.shape
    tr, tc = _tile2(hr, C)
    nrb = hr // tr

    def body(chip_ref, core_ref, p_ref, r_ref, o_ref):
        acc = p_ref[...]
        for t in range(N_CHIPS - 1):
            acc = acc + r_ref[t].astype(F32)
        o_ref[...] = acc

    return pl.pallas_call(
        body, name=name,
        grid_spec=pltpu.PrefetchScalarGridSpec(
            num_scalar_prefetch=2, grid=(nrb, C // tc),
            in_specs=[pl.BlockSpec((None, tr, tc), lambda i, j, s, c: (s[0], i, j)),
                      pl.BlockSpec((N_CHIPS - 1, tr, tc), lambda i, j, s, c: (0, i, j))],
            out_specs=pl.BlockSpec((tr, tc), lambda i, j, s, c: (c[0] * nrb + i, j))),
        out_shape=jax.ShapeDtypeStruct((2 * hr, C), F32),
        compiler_params=_cparams(("parallel", "parallel")),
    )(chip, core, p, recv)


def adamw(name, w, g, m, v):
    R, C = w.shape
    tr, tc = _tile2(R, C)
    c1 = 1.0 - ADAM_B1 ** ADAM_STEP
    c2 = 1.0 - ADAM_B2 ** ADAM_STEP

    def body(w_ref, g_ref, m_ref, v_ref, d_ref, nm_ref, nv_ref):
        gv = g_ref[...]
        nm = ADAM_B1 * m_ref[...] + (1.0 - ADAM_B1) * gv
        nv = ADAM_B2 * v_ref[...] + (1.0 - ADAM_B2) * (gv * gv)
        nm_ref[...] = nm
        nv_ref[...] = nv
        d_ref[...] = -ADAM_LR * ((nm / c1) / (jnp.sqrt(nv / c2) + ADAM_EPS) + ADAM_WD * w_ref[...])

    spec = pl.BlockSpec((tr, tc), lambda i, j: (i, j))
    return pl.pallas_call(
        body, name=name, grid=(R // tr, C // tc), in_specs=[spec] * 4, out_specs=[spec] * 3,
        out_shape=[jax.ShapeDtypeStruct((R, C), F32)] * 3,
        compiler_params=_cparams(("parallel", "parallel")),
    )(w, g, m, v)


def adamw_layers(name, w, g_layers, m, v):
    nl, R, C = w.shape
    tr, tc = _tile2(R, C)
    ni, nj = R // tr, C // tc
    c1 = 1.0 - ADAM_B1 ** ADAM_STEP
    c2 = 1.0 - ADAM_B2 ** ADAM_STEP

    def body(w_ref, *rest):
        g_refs = rest[:nl]
        m_ref, v_ref, g_ref, d_ref, nm_ref, nv_ref = rest[nl:]
        layer = pl.program_id(0)
        gv = g_refs[0][...]
        for l in range(1, nl):
            gv = jnp.where(layer == l, g_refs[l][...], gv)
        nm = ADAM_B1 * m_ref[...] + (1.0 - ADAM_B1) * gv
        nv = ADAM_B2 * v_ref[...] + (1.0 - ADAM_B2) * (gv * gv)
        g_ref[...] = gv
        nm_ref[...] = nm
        nv_ref[...] = nv
        d_ref[...] = -ADAM_LR * ((nm / c1) / (jnp.sqrt(nv / c2) + ADAM_EPS) + ADAM_WD * w_ref[...])

    def g_spec(l):
        def index(layer, i, j):
            return (jnp.where(layer == l, i, jnp.where(layer < l, 0, ni - 1)),
                    jnp.where(layer == l, j, jnp.where(layer < l, 0, nj - 1)))
        return pl.BlockSpec((tr, tc), index)

    spec = pl.BlockSpec((None, tr, tc), lambda layer, i, j: (layer, i, j))
    return pl.pallas_call(
        body, name=name, grid=(nl, ni, nj),
        in_specs=[spec] + [g_spec(l) for l in range(nl)] + [spec] * 2, out_specs=[spec] * 4,
        out_shape=[jax.ShapeDtypeStruct((nl, R, C), F32)] * 4,
        compiler_params=_cparams(("arbitrary", "arbitrary", "arbitrary")),
    )(w, *g_layers, m, v)


def sum_devices(name, gathered):
    n, R, C = gathered.shape

    def body(x_ref, o_ref):
        acc = x_ref[0]
        for d in range(1, n):
            acc = acc + x_ref[d]
        o_ref[...] = acc

    return pl.pallas_call(
        body, name=name, in_specs=[pl.BlockSpec(memory_space=pltpu.VMEM)],
        out_specs=pl.BlockSpec(memory_space=pltpu.VMEM),
        out_shape=jax.ShapeDtypeStruct((R, C), F32),
    )(gathered)


def _place():
    x, y, c = lax.axis_index("x"), lax.axis_index("y"), lax.axis_index("c")
    return x, y, c


def _other_chips(x, y):
    return [(1 - x, y), (x, 1 - y), (1 - x, 1 - y)]


def all_gather8(name, block):
    R, C = block.shape

    def body(x_ref, out_ref, send_sems, recv_sems, local_sem):
        x, y, c = _place()
        me, sibling = (x, y, c), (x, y, 1 - c)
        chips = _other_chips(x, y)

        def rows(px, py, pc):
            return out_ref.at[4 * px + 2 * py + pc]

        def copy(k, blk, to, src=None):
            return pltpu.make_async_remote_copy(
                src_ref=rows(*blk) if src is None else src, dst_ref=rows(*blk),
                send_sem=send_sems.at[k], recv_sem=recv_sems.at[k], device_id=to, device_id_type=MESH)

        mine = pltpu.make_async_copy(x_ref, rows(*me), local_sem)
        mine.start()
        first = [copy(0, me, sibling, src=x_ref)]
        first += [copy(1 + j, me, (*chip, c), src=x_ref) for j, chip in enumerate(chips)]
        for cp in first:
            cp.start()
        passed = [copy(4 + j, (*chip, c), sibling) for j, chip in enumerate(chips)]
        for j, chip in enumerate(chips):
            copy(1 + j, (*chip, c), me).wait_recv()
            passed[j].start()
        copy(0, sibling, me).wait_recv()
        for j, chip in enumerate(chips):
            copy(4 + j, (*chip, 1 - c), me).wait_recv()
        for cp in first + passed:
            cp.wait_send()
        mine.wait()

    return pl.pallas_call(
        body, name=name, out_shape=jax.ShapeDtypeStruct((N_DEV, R, C), block.dtype),
        in_specs=[pl.BlockSpec(memory_space=pltpu.VMEM)], out_specs=pl.BlockSpec(memory_space=pltpu.VMEM),
        scratch_shapes=[pltpu.SemaphoreType.DMA((7,)), pltpu.SemaphoreType.DMA((7,)), pltpu.SemaphoreType.DMA],
    )(block)


_HBM = pl.BlockSpec(memory_space=pltpu.HBM)
_SEM = pl.BlockSpec(memory_space=pltpu.SEMAPHORE)
_DATAFLOW = pltpu.SideEffectType.DATAFLOW_SIDE_EFFECTING


def _in_hbm(a):
    return pltpu.with_memory_space_constraint(a, pltpu.HBM)


def split_start(name, srcs, lands, n_sem, plan):
    ns, nl = len(srcs), len(lands)

    def body(*refs):
        src, land = refs[:ns], refs[ns:ns + nl]
        send_sems, recv_sems = refs[ns + nl], refs[ns + nl + 1]
        token = refs[-1]
        outgoing, _ = plan(src, land, send_sems, recv_sems)
        for cp in outgoing:
            cp.start()
        token[...] = jnp.zeros_like(token)

    bufs = list(srcs) + list(lands)
    res = pl.pallas_call(
        body, name=name,
        out_shape=(pltpu.SemaphoreType.DMA((n_sem,)), pltpu.SemaphoreType.DMA((n_sem,)),
                   *[pltpu.HBM(b.shape, b.dtype) for b in bufs], jax.ShapeDtypeStruct((8, 128), F32)),
        in_specs=[_HBM] * (ns + nl),
        out_specs=(_SEM, _SEM, *[_HBM] * (ns + nl), pl.BlockSpec(memory_space=pltpu.VMEM)),
        input_output_aliases={i: 2 + i for i in range(ns + nl)},
        compiler_params=pltpu.CompilerParams(has_side_effects=_DATAFLOW),
    )(*[_in_hbm(b) for b in bufs])
    return res[0], res[1], list(res[2:2 + ns]), list(res[2 + ns:2 + ns + nl]), res[-1]


def split_wait(name, started, after, plan):
    send_sems, recv_sems, srcs, lands, _ = started
    ns, nl = len(srcs), len(lands)

    def body(*refs):
        src, land = refs[:ns], refs[ns:ns + nl]
        send, recv = refs[ns + nl], refs[ns + nl + 1]
        outgoing, incoming = plan(src, land, send, recv)
        for cp in outgoing:
            cp.wait_send()
        for cp in incoming:
            cp.wait_recv()

    bufs = list(srcs) + list(lands)
    res = pl.pallas_call(
        body, name=name,
        out_shape=tuple(pltpu.HBM(b.shape, b.dtype) for b in bufs),
        in_specs=[_HBM] * (ns + nl) + [_SEM, _SEM, pl.BlockSpec(memory_space=pl.ANY)],
        out_specs=tuple([_HBM] * (ns + nl)),
        input_output_aliases={i: i for i in range(ns + nl)},
        compiler_params=pltpu.CompilerParams(has_side_effects=_DATAFLOW),
    )(*bufs, send_sems, recv_sems, after)
    return list(res[ns:])


def _rcopy(src, dst, send_sems, ks, recv_sems, kr, device):
    return pltpu.make_async_remote_copy(src_ref=src, dst_ref=dst, send_sem=send_sems.at[ks], recv_sem=recv_sems.at[kr],
                                        device_id=device, device_id_type=MESH)


def _half_rows(ref, h):
    hr = ref.shape[0] // 2
    return ref.at[pl.ds(h * hr, hr)]


def _gather_plan(src, land, send_sems, recv_sems):
    x, y, c = _place()
    me_chip = 2 * x + y
    chips = _other_chips(x, y)
    outgoing, incoming = [], []
    for w, buf in enumerate(land):
        mine = _half_rows(buf.at[me_chip], c)
        for t, chip in enumerate(chips):
            slot = 2 * chip[0] + chip[1]
            for cc in range(2):
                outgoing.append(_rcopy(mine, mine, send_sems, 6 * w + 2 * t + cc, recv_sems, 6 * w + 2 * t + c, (*chip, cc)))
                theirs = _half_rows(buf.at[slot], cc)
                incoming.append(_rcopy(theirs, theirs, send_sems, 6 * w + 2 * t + cc, recv_sems, 6 * w + 2 * t + cc, (*chip, cc)))
    return outgoing, incoming


def _swap_plan(src, land, send_sems, recv_sems):
    x, y, c = _place()
    cp = _rcopy(src[0], land[0], send_sems, 0, recv_sems, 0, (x, y, 1 - c))
    return [cp], [cp]


def _scatter_plan(src, land, send_sems, recv_sems):
    x, y, c = _place()
    cps = [_rcopy(src[0].at[2 * chip[0] + chip[1]], land[0].at[t], send_sems, t, recv_sems, t, (*chip, c))
           for t, chip in enumerate(_other_chips(x, y))]
    return cps, cps


def _share_plan(src, land, send_sems, recv_sems):
    x, y, c = _place()
    mine, theirs = _half_rows(land[0], c), _half_rows(land[0], 1 - c)
    return ([_rcopy(mine, mine, send_sems, 0, recv_sems, 0, (x, y, 1 - c))],
            [_rcopy(theirs, theirs, send_sems, 0, recv_sems, 0, (x, y, 1 - c))])


def place_shard(name, shard, chip, deps=(), layer=None):
    R, C = shard.shape[-2:]
    tr, tc = _tile2(R, C)

    def body(chip_ref, x_ref, *rest):
        rest[-1][...] = x_ref[...].astype(BF16)

    if layer is None:
        src = pl.BlockSpec((tr, tc), lambda i, j, s: (i, j))
    else:
        src = pl.BlockSpec((None, tr, tc), lambda i, j, s: (layer, i, j))
    return pl.pallas_call(
        body, name=name,
        grid_spec=pltpu.PrefetchScalarGridSpec(
            num_scalar_prefetch=1, grid=(R // tr, C // tc),
            in_specs=[src] + [pl.BlockSpec(d.shape, lambda i, j, s: (0, 0)) for d in deps],
            out_specs=pl.BlockSpec((None, tr, tc), lambda i, j, s: (s[0], i, j))),
        out_shape=jax.ShapeDtypeStruct((N_CHIPS, R, C), BF16),
        compiler_params=_cparams(("parallel", "parallel")),
    )(chip, shard, *deps)


class GradExchange:
    SCATTER_TICKS = 2

    def __init__(self, chip1, core, shard, mom, vel):
        self.chip1, self.core, self.shard, self.mom, self.vel = chip1, core, shard, mom, vel
        self.inflight, self.tokens, self.results = [], [], {}

    def take_deps(self):
        deps, self.tokens = self.tokens, []
        return deps

    def _start(self, name, srcs, lands, n_sem, plan):
        started = split_start(name, srcs, lands, n_sem, plan)
        self.tokens.append(started[-1])
        return started

    def add(self, n, dw):
        S, R, C = dw.shape
        to_sibling = half_cast("rs_cast_" + n, dw, self.core)
        started = self._start("rs_swap_start_" + n, [to_sibling], [lax.empty((S, R // 2, C), BF16)], 1, _swap_plan)
        self.inflight.append(dict(n=n, dw=dw, stage=0, started=started, ticks=0))

    def tick(self, after):
        for it in self.inflight:
            n = it["n"]
            if it["stage"] == 0:
                (recv,) = split_wait("rs_swap_wait_" + n, it["started"], after, _swap_plan)
                p, pbf = pair_sum("rs_pair_sum_" + n, it["dw"], recv, self.core)
                S, hr, C = pbf.shape
                it.update(stage=1, p=p, ticks=0,
                          started=self._start("rs_scatter_start_" + n, [pbf], [lax.empty((N_CHIPS - 1, hr, C), BF16)], 3, _scatter_plan))
            elif it["stage"] == 1:
                it["ticks"] += 1
                if it["ticks"] >= self.SCATTER_TICKS:
                    (recv,) = split_wait("rs_scatter_wait_" + n, it["started"], after, _scatter_plan)
                    half = chip_sum("rs_chip_sum_" + n, it["p"], recv, self.chip1, self.core)
                    it.update(stage=2, started=self._start("rs_share_start_" + n, [], [half], 1, _share_plan))
            elif it["stage"] == 2:
                (grad,) = split_wait("rs_share_wait_" + n, it["started"], after, _share_plan)
                if n in self.shard:
                    self.results[n] = (grad,) + tuple(adamw("adamw_" + n, self.shard[n], grad, self.mom[n], self.vel[n]))
                else:
                    self.results[n] = (grad,)
                it["stage"] = 3
        self.inflight = [it for it in self.inflight if it["stage"] < 3]

    def flush(self, after):
        while self.inflight:
            self.tick(after)


def _pack(arrs):
    parts = []
    for a in arrs:
        flat = a.reshape(-1).astype(F32)
        n = flat.shape[0]
        padded = -(-n // 1024) * 1024
        parts.append(jnp.pad(flat, (0, padded - n)).reshape(padded // 128, 128))
    return jnp.concatenate(parts, axis=0)


def _unpack(buf, shapes):
    out, row = [], 0
    for shp in shapes:
        n = int(np.prod(shp))
        rows = -(-n // 1024) * 8
        out.append(buf[row:row + rows].reshape(-1)[:n].reshape(shp))
        row += rows
    return out


def _bias_epi(acc, b):
    return (acc + b,)


def local_step(x, target, W, P, ex, first_deps=()):
    T, D = x.shape
    g = {}
    plain = lambda acc: (acc,)

    (h1,) = mm_nn("pw1_fwd", x, W("pw1", x), "col", _bias_epi, [F32],
                  extras=[(P["pw1_b"], "row")] + [(d, "dep") for d in first_deps])
    u, cpre, s = conv_fwd("conv_fwd", h1, P["dw_w"], P["dw_b"], P["cln_g"], P["cln_b"])
    (mix0,) = mm_nn("pw2_fwd", s, W("pw2", s), "row", _bias_epi, [F32], extras=[(P["pw2_b"], "row")])
    ln = [None] * 4
    gam = [P["ln_mix_g"][0:1], P["ln_mlp_g"][0:1], P["ln_mix_g"][1:2], P["ln_mlp_g"][1:2]]
    bet = [P["ln_mix_b"][0:1], P["ln_mlp_b"][0:1], P["ln_mix_b"][1:2], P["ln_mlp_b"][1:2]]
    ln[0] = ln_fwd("ln0_fwd", mix0, x)(gam[0], bet[0])

    def mlp_fwd(tag, i_ln, n1, n2):
        xhat, rstd, xbf = ln[i_ln]

        def up_epi(acc):
            r = jnp.maximum(acc, 0.0)
            return r * r, r

        hid, relu = mm_nn(tag + "_up", xbf, W(n1, xbf), "col", up_epi, [BF16, BF16])
        (mlp,) = mm_nn(tag + "_down", hid, W(n2, hid), "row", plain, [F32])
        ln[i_ln + 1] = ln_fwd(tag + "_ln", mlp, xhat, gam[i_ln], bet[i_ln])(gam[i_ln + 1], bet[i_ln + 1])
        return hid, relu

    hid0 = mlp_fwd("mlp0", 0, "w1_0", "w2_0")

    x2bf = ln[1][2]
    (kv,) = mm_nn("kv_fwd", x2bf, W("kv", x2bf), "col", plain, [F32])
    (q,) = mm_nn("q_fwd", x2bf, W("wq", kv), "row", plain, [F32])
    biases = [bias_expand("bias_d%d" % d, P["rel_bias"], d) for _, d in BRANCHES]
    assert all(win // d == BAND and min(ATTN_TOKENS, T) % (BAND * d) == 0 for win, d in BRANCHES)
    o, obf, lse = attn_fwd("attn_fwd", q, kv, biases)
    (attn,) = mm_nn("wo_fwd", obf, W("wo", obf), "row", plain, [F32])
    ln[2] = ln_fwd("ln2_fwd", attn, ln[1][0], gam[1], bet[1])(gam[2], bet[2])
    hid1 = mlp_fwd("mlp1", 2, "w1_1", "w2_1")

    dr3, dr3bf, g["ln_mlp_g1"], g["ln_mlp_b1"], _, loss_sum = ln_bwd(
        "ln3_bwd", ln[3][0], ln[3][1], gam[3], target=target, beta=bet[3])

    def dw_step(name, wname, a, cot, axis):
        dw = mm_tn(name, a, cot, W(wname, a).shape, axis, deps=ex.take_deps())
        ex.tick(dw)
        ex.add(wname, dw)

    def dx_step(name, cot, wname, axis, epilogue, out_dtype, extras):
        deps = [(d, "dep") for d in ex.take_deps()]
        (out,) = mm_nt(name, cot, W(wname, cot), axis, epilogue, [out_dtype], extras=list(extras) + deps)
        ex.tick(out)
        return out

    def mlp_bwd(tag, i_ln, n1, n2, hid_relu, dr, drbf):
        xbf = ln[i_ln][2]
        hid, relu = hid_relu
        dw_step(tag + "_dw2", n2, hid, drbf, "row")
        dp = dx_step(tag + "_dhid", drbf, n2, "row", lambda acc, r: (acc * (2.0 * r.astype(F32)),), BF16, [(relu, "tile")])
        dw_step(tag + "_dw1", n1, xbf, dp, "col")
        return dx_step(tag + "_dx", dp, n1, "col", lambda acc, e: (acc + ALPHA * e,), F32, [(dr, "tile")])

    dx3 = mlp_bwd("mlp1", 2, "w1_1", "w2_1", hid1, dr3, dr3bf)
    dr2, dr2bf, g["ln_mix_g1"], g["ln_mix_b1"], _ = ln_bwd("ln2_bwd", ln[2][0], ln[2][1], gam[2], dy=dx3)
    dw_step("wo_dw", "wo", obf, dr2bf, "row")
    do = dx_step("wo_dx", dr2bf, "wo", "row", plain, F32, [])
    dq, dk, dv, dsbs = attn_bwd("attn_bwd", q, kv, do, o, lse, biases)
    g["rel_bias"] = relbias_grad("relbias_grad", dsbs)[:, 0, :REL_BUCKETS].T
    dkv = jnp.concatenate([dk, dv], axis=1)
    dw_step("wq_dw", "wq", x2bf, dq, "row")
    dw_step("kv_dw", "kv", x2bf, dkv, "col")
    dx2a = dx_step("wq_dx", dq, "wq", "row", lambda acc, e: (acc + ALPHA * e,), F32, [(dr2, "tile")])
    dx2 = dx_step("kv_dx", dkv, "kv", "col", lambda acc, e: (acc + e,), F32, [(dx2a, "tile")])

    dr1, dr1bf, g["ln_mlp_g0"], g["ln_mlp_b0"], _ = ln_bwd("ln1_bwd", ln[1][0], ln[1][1], gam[1], dy=dx2)
    dx1 = mlp_bwd("mlp0", 0, "w1_0", "w2_0", hid0, dr1, dr1bf)
    dr0, dr0bf, g["ln_mix_g0"], g["ln_mix_b0"], g["pw2_b"] = ln_bwd("ln0_bwd", ln[0][0], ln[0][1], gam[0], dy=dx1)

    dw_step("pw2_dw", "pw2", s, dr0bf, "row")
    ds = dx_step("pw2_dx", dr0bf, "pw2", "row", plain, F32, [])
    dc, g["cln_g"], g["cln_b"], g["dw_b"] = conv_bwd_ln("conv_bwd_ln", ds, cpre, P["cln_g"], P["cln_b"])
    dh1, g["pw1_b"], g["dw_w"] = conv_bwd_taps("conv_bwd_taps", dc, u, h1, P["dw_w"])
    dw_step("pw1_dw", "pw1", x, dh1, "col")
    dx = dx_step("pw1_dx", dh1, "pw1", "col", lambda acc, e: (acc + ALPHA * e,), F32, [(dr0, "tile")])
    return loss_sum, dx, g


BIG = ("pw1", "pw2", "w1_0", "w2_0", "kv", "wq", "wo", "w1_1", "w2_1")


def kernel(x, conv_pw1_w, conv_pw1_b, conv_dw_w, conv_dw_b, conv_ln_g, conv_ln_b, conv_pw2_w, conv_pw2_b, w_kv, attn_wq, attn_wo, rel_bias, mlp_w1, mlp_w2, ln_mix_g, ln_mix_b, ln_mlp_g, ln_mlp_b, loss_target, m_conv_pw1_w, m_conv_pw1_b, m_conv_dw_w, m_conv_dw_b, m_conv_ln_g, m_conv_ln_b, m_conv_pw2_w, m_conv_pw2_b, m_w_kv, m_attn_wq, m_attn_wo, m_rel_bias, m_mlp_w1, m_mlp_w2, m_ln_mix_g, m_ln_mix_b, m_ln_mlp_g, m_ln_mlp_b, v_conv_pw1_w, v_conv_pw1_b, v_conv_dw_w, v_conv_dw_b, v_conv_ln_g, v_conv_ln_b, v_conv_pw2_w, v_conv_pw2_b, v_w_kv, v_attn_wq, v_attn_wo, v_rel_bias, v_mlp_w1, v_mlp_w2, v_ln_mix_g, v_ln_mix_b, v_ln_mlp_g, v_ln_mlp_b):
    _, T, D = x.shape
    xi, yi, ci = _place()
    chip = 2 * xi + yi
    core = jnp.reshape(ci, (1,)).astype(jnp.int32)
    chip1 = jnp.reshape(chip, (1,)).astype(jnp.int32)

    def two_d(a):
        return a.reshape(a.shape[-2:])

    shard = {"pw1": two_d(conv_pw1_w), "pw2": two_d(conv_pw2_w), "kv": w_kv, "wq": two_d(attn_wq), "wo": two_d(attn_wo)}
    mom = {"pw1": two_d(m_conv_pw1_w), "pw2": two_d(m_conv_pw2_w), "kv": m_w_kv, "wq": two_d(m_attn_wq), "wo": two_d(m_attn_wo)}
    vel = {"pw1": two_d(v_conv_pw1_w), "pw2": two_d(v_conv_pw2_w), "kv": v_w_kv, "wq": two_d(v_attn_wq), "wo": two_d(v_attn_wo)}
    stacked = {"w1_0": (mlp_w1, 0), "w1_1": (mlp_w1, 1), "w2_0": (mlp_w2, 0), "w2_1": (mlp_w2, 1)}

    started = {}
    for n in BIG:
        deps = [started[prev][-1] for prev in list(started)[-1:]]
        src, layer = stacked.get(n, (shard.get(n), None))
        started[n] = split_start("gather_start_" + n, [], [place_shard("place_" + n, src, chip1, deps, layer)], 6, _gather_plan)
    gathered = {}

    def W(n, after):
        if n not in gathered:
            (gathered[n],) = split_wait("gather_wait_" + n, started[n], after, _gather_plan)
        return gathered[n]

    sharded_small = [conv_pw1_b, conv_dw_w[0], conv_dw_b, conv_ln_g, conv_ln_b, conv_pw2_b]
    sh_shapes = [a.shape for a in sharded_small]
    small_all = all_gather8("gather_small", _pack(sharded_small))
    per_chip = [_unpack(small_all[2 * j], sh_shapes) for j in range(N_CHIPS)]
    full = [jnp.concatenate([per_chip[j][i] for j in range(N_CHIPS)], axis=-1) for i in range(len(sharded_small))]
    P = dict(pw1_b=full[0], dw_w=full[1], dw_b=full[2], cln_g=full[3], cln_b=full[4], pw2_b=full[5],
             rel_bias=rel_bias, ln_mix_g=ln_mix_g, ln_mix_b=ln_mix_b, ln_mlp_g=ln_mlp_g, ln_mlp_b=ln_mlp_b)

    ex = GradExchange(chip1, core, shard, mom, vel)
    loss_sum, dx, g = local_step(x.reshape(T, D), loss_target.reshape(T, D), W, P, ex,
                                 first_deps=[started[n][-1] for n in BIG])
    loss = (0.5 / D) * lax.psum(loss_sum[0, 0], ("x", "y", "c"))

    small_names = ["pw1_b", "dw_w", "dw_b", "cln_g", "cln_b", "pw2_b", "rel_bias",
                   "ln_mix_g0", "ln_mix_g1", "ln_mix_b0", "ln_mix_b1", "ln_mlp_g0", "ln_mlp_g1", "ln_mlp_b0", "ln_mlp_b1"]
    small_grads = [g[n] for n in small_names]
    sg_shapes = [a.shape for a in small_grads]
    summed = sum_devices("small_grad_sum", all_gather8("gather_small_grads", _pack(small_grads)))
    sg = dict(zip(small_names, _unpack(summed, sg_shapes)))

    def my_cols(a, width):
        return lax.dynamic_slice_in_dim(a, chip * width, width, axis=a.ndim - 1)

    small_g = [my_cols(sg["pw1_b"], conv_pw1_b.shape[-1]),
               my_cols(sg["dw_w"], conv_dw_w.shape[-1])[None],
               my_cols(sg["dw_b"], conv_dw_b.shape[-1]), my_cols(sg["cln_g"], conv_ln_g.shape[-1]),
               my_cols(sg["cln_b"], conv_ln_b.shape[-1]), my_cols(sg["pw2_b"], conv_pw2_b.shape[-1]),
               sg["rel_bias"],
               jnp.concatenate([sg["ln_mix_g0"], sg["ln_mix_g1"]], axis=0),
               jnp.concatenate([sg["ln_mix_b0"], sg["ln_mix_b1"]], axis=0),
               jnp.concatenate([sg["ln_mlp_g0"], sg["ln_mlp_g1"]], axis=0),
               jnp.concatenate([sg["ln_mlp_b0"], sg["ln_mlp_b1"]], axis=0)]
    small_w = [conv_pw1_b, conv_dw_w, conv_dw_b, conv_ln_g, conv_ln_b, conv_pw2_b, rel_bias, ln_mix_g, ln_mix_b, ln_mlp_g, ln_mlp_b]
    small_m = [m_conv_pw1_b, m_conv_dw_w, m_conv_dw_b, m_conv_ln_g, m_conv_ln_b, m_conv_pw2_b, m_rel_bias, m_ln_mix_g, m_ln_mix_b, m_ln_mlp_g, m_ln_mlp_b]
    small_v = [v_conv_pw1_b, v_conv_dw_w, v_conv_dw_b, v_conv_ln_g, v_conv_ln_b, v_conv_pw2_b, v_rel_bias, v_ln_mix_g, v_ln_mix_b, v_ln_mlp_g, v_ln_mlp_b]
    sw_shapes = [a.shape for a in small_w]
    small_g = [a.reshape(s) for a, s in zip(small_g, sw_shapes)]
    upd_small = adamw("adamw_small", _pack(small_w), _pack(small_g), _pack(small_m), _pack(small_v))
    sd, snm, snv = (_unpack(b, sw_shapes) for b in upd_small)

    ex.flush(upd_small[0])
    res_w1 = adamw_layers("adamw_w1", mlp_w1, [ex.results["w1_0"][0], ex.results["w1_1"][0]], m_mlp_w1, v_mlp_w1)
    res_w2 = adamw_layers("adamw_w2", mlp_w2, [ex.results["w2_0"][0], ex.results["w2_1"][0]], m_mlp_w2, v_mlp_w2)

    def big_out(k):
        one = {n: ex.results[n][k] for n in shard}
        return dict(pw1=one["pw1"][None], pw2=one["pw2"][None], kv=one["kv"], wq=one["wq"][None], wo=one["wo"][None],
                    w1=res_w1[k], w2=res_w2[k])

    def ordered(big, small):
        return [big["pw1"], small[0], small[1], small[2], small[3], small[4], big["pw2"], small[5], big["kv"], big["wq"],
                big["wo"], small[6], big["w1"], big["w2"], small[7], small[8], small[9], small[10]]

    grads = ordered(big_out(0), small_g)
    deltas = ordered(big_out(1), sd)
    new_m = ordered(big_out(2), snm)
    new_v = ordered(big_out(3), snv)
    return (loss, dx.reshape(1, T, D), *grads, *deltas, *new_m, *new_v)
```

```python
import functools
import math

import numpy as np
import jax
import jax.numpy as jnp
from jax import lax
from jax.experimental import pallas as pl
from jax.experimental.pallas import tpu as pltpu

F32 = jnp.float32
BF16 = jnp.bfloat16

HEAD_DIM = 128
BAND = 128
BRANCHES = ((128, 1), (512, 4), (2048, 16))
CONV_WIDTH = 31
CONV_HALO = 32
REL_BUCKETS = 32
REL_MAX_DIST = 2048
DEPTH = 2
ALPHA = (2 * DEPTH) ** 0.25
LN_EPS = 1e-5
ADAM_LR, ADAM_B1, ADAM_B2, ADAM_EPS, ADAM_WD, ADAM_STEP = 0.001, 0.9, 0.999, 1e-08, 0.01, 10

N_CHIPS = 4
N_DEV = 8
MESH = pl.DeviceIdType.MESH
VMEM_LIMIT_BYTES = 56 * 1024 * 1024
MM_TM, MM_TN, MM_TK = 1024, 1024, 2048
ROW_TILE = 256
CONV_TILE = 128
NEG_BIG = -1e30


def _cparams(sem):
    return pltpu.CompilerParams(dimension_semantics=sem, vmem_limit_bytes=VMEM_LIMIT_BYTES)


def _sigmoid(x):
    return 1.0 / (1.0 + jnp.exp(-x))


def _wspec(wshape, axis, br, bc, rsel, csel):
    _, R, C = wshape
    if axis == "col":
        if bc > C:
            assert bc % C == 0, (wshape, bc)
            return pl.BlockSpec((bc // C, br, C), lambda *g: (csel(*g), rsel(*g), 0))
        nb = C // bc
        assert nb * bc == C, (wshape, bc)
        return pl.BlockSpec((None, br, bc), lambda *g: (csel(*g) // nb, rsel(*g), csel(*g) % nb))
    if br > R:
        assert br % R == 0, (wshape, br)
        return pl.BlockSpec((br // R, R, bc), lambda *g: (rsel(*g), 0, csel(*g)))
    nb = R // br
    assert nb * br == R, (wshape, br)
    return pl.BlockSpec((None, br, bc), lambda *g: (rsel(*g) // nb, rsel(*g) % nb, csel(*g)))


def _join_shards(b, axis):
    if b.ndim == 2:
        return b
    if axis == "row":
        return b.reshape(b.shape[0] * b.shape[1], b.shape[2])
    return jnp.concatenate([b[s] for s in range(b.shape[0])], axis=1)


def _split_shards(r, shape, axis):
    if len(shape) == 2:
        return r
    if axis == "row":
        return r.reshape(shape)
    return jnp.stack([r[:, s * shape[2]:(s + 1) * shape[2]] for s in range(shape[0])])


def _full_dims(wshape, axis):
    _, R, C = wshape
    return (R, N_CHIPS * C) if axis == "col" else (N_CHIPS * R, C)


def _mm_body(nk, kinds, n_out, dims, epilogue, axis):
    n_extra = len(kinds)

    def body(*refs):
        a_ref, b_ref = refs[0], refs[1]
        extra = [r for r, kind in zip(refs[2:2 + n_extra], kinds) if kind != "dep"]
        outs = refs[2 + n_extra:2 + n_extra + n_out]
        part = lax.dot_general(a_ref[...].astype(BF16), _join_shards(b_ref[...], axis).astype(BF16), (dims, ((), ())),
                               preferred_element_type=F32)

        def write(res):
            for r, o in zip(res, outs):
                o[...] = _split_shards(r, o.shape, axis).astype(o.dtype)

        if nk == 1:
            write(epilogue(part, *[e[...] for e in extra]))
            return
        acc_ref = refs[2 + n_extra + n_out]
        k = pl.program_id(2)

        @pl.when(k == 0)
        def _():
            acc_ref[...] = part

        @pl.when(k > 0)
        def _():
            acc_ref[...] += part

        @pl.when(k == nk - 1)
        def _():
            write(epilogue(acc_ref[...], *[e[...] for e in extra]))
    return body


def _long_tk(a, k_dim):
    tk = min(MM_TK, k_dim)
    if a.dtype == BF16 and k_dim >= 4 * MM_TK:
        tk = 2 * MM_TK
    return tk


def _extra_specs(extras, tm, tn):
    specs = []
    for arr, kind in extras:
        if kind == "tile":
            specs.append(pl.BlockSpec((tm, tn), lambda i, j, k: (i, j)))
        elif kind == "dep":
            specs.append(pl.BlockSpec(arr.shape, lambda i, j, k: (0, 0)))
        else:
            specs.append(pl.BlockSpec((1, tn), lambda i, j, k: (0, j)))
    return specs


def mm_nn(name, a, w, axis, epilogue, out_dtypes, extras=()):
    M, K = a.shape
    Kw, N = _full_dims(w.shape, axis)
    assert K == Kw
    tm, tn, tk = min(MM_TM, M), min(MM_TN, N), _long_tk(a, K)
    nk = K // tk
    in_specs = [pl.BlockSpec((tm, tk), lambda i, j, k: (i, k)),
                _wspec(w.shape, axis, tk, tn, lambda i, j, k: k, lambda i, j, k: j)]
    in_specs += _extra_specs(extras, tm, tn)
    body = _mm_body(nk, [kind for _, kind in extras], len(out_dtypes), ((1,), (0,)), epilogue, axis)
    return pl.pallas_call(
        body, name=name, grid=(M // tm, N // tn, nk), in_specs=in_specs,
        out_specs=[pl.BlockSpec((tm, tn), lambda i, j, k: (i, j)) for _ in out_dtypes],
        out_shape=[jax.ShapeDtypeStruct((M, N), d) for d in out_dtypes],
        scratch_shapes=[pltpu.VMEM((tm, tn), F32)] if nk > 1 else [],
        compiler_params=_cparams(("parallel", "parallel", "arbitrary")),
    )(a, w, *[e for e, _ in extras])


def mm_nt(name, g, w, axis, epilogue, out_dtypes, extras=()):
    M, N = g.shape
    K, Nw = _full_dims(w.shape, axis)
    assert N == Nw
    tm, tn, tk = min(MM_TM, M), min(MM_TN, K), min(MM_TK, N)
    nk = N // tk
    in_specs = [pl.BlockSpec((tm, tk), lambda i, j, k: (i, k)),
                _wspec(w.shape, axis, tn, tk, lambda i, j, k: j, lambda i, j, k: k)]
    in_specs += _extra_specs(extras, tm, tn)
    body = _mm_body(nk, [kind for _, kind in extras], len(out_dtypes), ((1,), (1,)), epilogue, axis)
    return pl.pallas_call(
        body, name=name, grid=(M // tm, K // tn, nk), in_specs=in_specs,
        out_specs=[pl.BlockSpec((tm, tn), lambda i, j, k: (i, j)) for _ in out_dtypes],
        out_shape=[jax.ShapeDtypeStruct((M, K), d) for d in out_dtypes],
        scratch_shapes=[pltpu.VMEM((tm, tn), F32)] if nk > 1 else [],
        compiler_params=_cparams(("parallel", "parallel", "arbitrary")),
    )(g, w, *[e for e, _ in extras])


def mm_tn(name, a, g, wshape, axis, deps=()):
    M, K = a.shape
    Mg, N = g.shape
    assert M == Mg and (K, N) == _full_dims(wshape, axis)
    tm, tn, tk = min(MM_TM, K), min(MM_TN, N), _long_tk(a, M)
    nk = M // tk
    body = _mm_body(nk, ["dep"] * len(deps), 1, ((0,), (0,)), lambda acc: (acc,), axis)
    return pl.pallas_call(
        body, name=name, grid=(K // tm, N // tn, nk),
        in_specs=[pl.BlockSpec((tk, tm), lambda i, j, k: (k, i)),
                  pl.BlockSpec((tk, tn), lambda i, j, k: (k, j))] + _extra_specs([(d, "dep") for d in deps], tm, tn),
        out_specs=[_wspec(wshape, axis, tm, tn, lambda i, j, k: i, lambda i, j, k: j)],
        out_shape=[jax.ShapeDtypeStruct(wshape, F32)],
        scratch_shapes=[pltpu.VMEM((tm, tn), F32)] if nk > 1 else [],
        compiler_params=_cparams(("parallel", "parallel", "arbitrary")),
    )(a, g, *deps)[0]


def _row_spec(tr, width):
    return pl.BlockSpec((tr, width), lambda i: (i, 0))


def _vec_spec(width):
    return pl.BlockSpec((1, width), lambda i: (0, 0))


def _fold8(x):
    r, d = x.shape
    return jnp.sum(x.reshape(r // 8, 8, d), axis=0)


def ln_fwd(name, f, prev, prev_g=None, prev_b=None):
    T, D = f.shape
    tr = min(ROW_TILE, T)
    affine = prev_g is not None

    def body(*refs):
        if affine:
            f_ref, p_ref, pg_ref, pb_ref, g_ref, b_ref, xhat_ref, rstd_ref, xbf_ref = refs
            xprev = p_ref[...] * pg_ref[...] + pb_ref[...]
        else:
            f_ref, p_ref, g_ref, b_ref, xhat_ref, rstd_ref, xbf_ref = refs
            xprev = p_ref[...]
        r = ALPHA * xprev + f_ref[...]
        mu = jnp.mean(r, axis=-1, keepdims=True)
        cen = r - mu
        var = jnp.mean(cen * cen, axis=-1, keepdims=True)
        rstd = lax.rsqrt(var + LN_EPS)
        xhat = cen * rstd
        xhat_ref[...] = xhat
        rstd_ref[...] = rstd
        xbf_ref[...] = (xhat * g_ref[...] + b_ref[...]).astype(BF16)

    def call(g, b):
        ins = [f, prev] + ([prev_g, prev_b] if affine else []) + [g, b]
        specs = [_row_spec(tr, D), _row_spec(tr, D)] + ([_vec_spec(D)] * 2 if affine else []) + [_vec_spec(D)] * 2
        return pl.pallas_call(
            body, name=name, grid=(T // tr,), in_specs=specs,
            out_specs=[_row_spec(tr, D), _row_spec(tr, 1), _row_spec(tr, D)],
            out_shape=[jax.ShapeDtypeStruct((T, D), F32), jax.ShapeDtypeStruct((T, 1), F32),
                       jax.ShapeDtypeStruct((T, D), BF16)],
            compiler_params=_cparams(("parallel",)),
        )(*ins)
    return call


def ln_bwd(name, xhat, rstd, gamma, dy=None, target=None, beta=None):
    T, D = xhat.shape
    tr = min(ROW_TILE, T)
    nt = T // tr
    head = target is not None

    def body(*refs):
        if head:
            xhat_ref, rstd_ref, g_ref, tgt_ref, b_ref = refs[:5]
            outs = refs[5:]
        else:
            xhat_ref, rstd_ref, g_ref, dy_ref = refs[:4]
            outs = refs[4:]
        dr_ref, drbf_ref, dg_ref, db_ref, cs_ref = outs[:5]
        rest = outs[5:]
        if head:
            loss_ref, acc_ref = rest
        else:
            (acc_ref,) = rest
        i = pl.program_id(0)
        xhat_v = xhat_ref[...]
        gam = g_ref[...]
        if head:
            diff = xhat_v * gam + b_ref[...] - tgt_ref[...]
            dyv = diff * (1.0 / D)
        else:
            dyv = dy_ref[...]
        dxh = dyv * gam
        m1 = jnp.mean(dxh, axis=-1, keepdims=True)
        m2 = jnp.mean(dxh * xhat_v, axis=-1, keepdims=True)
        dr = rstd_ref[...] * (dxh - m1 - xhat_v * m2)
        dr_ref[...] = dr
        drbf_ref[...] = dr.astype(BF16)

        @pl.when(i == 0)
        def _():
            acc_ref[...] = jnp.zeros_like(acc_ref)

        acc_ref[0] += _fold8(dyv * xhat_v)
        acc_ref[1] += _fold8(dyv)
        acc_ref[2] += _fold8(dr)
        if head:
            acc_ref[3] += _fold8(diff * diff)

        @pl.when(i == nt - 1)
        def _():
            dg_ref[...] = jnp.sum(acc_ref[0], axis=0, keepdims=True)
            db_ref[...] = jnp.sum(acc_ref[1], axis=0, keepdims=True)
            cs_ref[...] = jnp.sum(acc_ref[2], axis=0, keepdims=True)
            if head:
                loss_ref[...] = jnp.sum(jnp.sum(acc_ref[3], axis=0, keepdims=True), axis=1, keepdims=True)

    ins = [xhat, rstd, gamma] + ([target, beta] if head else [dy])
    specs = [_row_spec(tr, D), _row_spec(tr, 1), _vec_spec(D)] + ([_row_spec(tr, D), _vec_spec(D)] if head else [_row_spec(tr, D)])
    out_specs = [_row_spec(tr, D), _row_spec(tr, D), _vec_spec(D), _vec_spec(D), _vec_spec(D)]
    out_shape = [jax.ShapeDtypeStruct((T, D), F32), jax.ShapeDtypeStruct((T, D), BF16)] + [jax.ShapeDtypeStruct((1, D), F32)] * 3
    if head:
        out_specs.append(pl.BlockSpec((1, 1), lambda i: (0, 0)))
        out_shape.append(jax.ShapeDtypeStruct((1, 1), F32))
    return pl.pallas_call(
        body, name=name, grid=(nt,), in_specs=specs, out_specs=out_specs, out_shape=out_shape,
        scratch_shapes=[pltpu.VMEM((4, 8, D), F32)],
        compiler_params=_cparams(("arbitrary",)),
    )(*ins)


CONV_ROWS, CONV_COLS = 64, 512


def _tap_chunks(tt, D):
    for r0 in range(0, tt, min(CONV_ROWS, tt)):
        for c0 in range(0, D, min(CONV_COLS, D)):
            yield r0, min(CONV_ROWS, tt), c0, min(CONV_COLS, D)


SUBLANES = 8


def _shifted_copies(ext_ref, sh_ref):
    n = sh_ref.shape[1]
    for b in range(1, SUBLANES):
        sh_ref[b - 1] = ext_ref[pl.ds(b, n), :]


def _rows_at(ext_ref, sh_ref, off, nr, cols):
    a, b = divmod(off, SUBLANES)
    if b == 0:
        return ext_ref[pl.ds(off, nr), cols]
    return sh_ref[b - 1, pl.ds(a * SUBLANES, nr), cols]


def conv_fwd(name, h1, dw, dwb, lng, lnb):
    T, D2 = h1.shape
    D = D2 // 2
    tt = min(CONV_TILE, T)
    hb = tt // CONV_HALO
    KW = dw.shape[0]
    lead = CONV_HALO - (KW - 1)

    def body(a_ref, g_ref, ah_ref, gh_ref, dw_ref, dwb_ref, lng_ref, lnb_ref, u_ref, c_ref, s_ref, ext_ref, sh_ref):
        i = pl.program_id(0)
        u = a_ref[...] * _sigmoid(g_ref[...])
        u_ref[...] = u
        uh = ah_ref[...] * _sigmoid(gh_ref[...])
        ext_ref[pl.ds(0, CONV_HALO), :] = jnp.where(i > 0, uh, 0.0)
        ext_ref[pl.ds(CONV_HALO, tt), :] = u
        _shifted_copies(ext_ref, sh_ref)
        for r0, nr, c0, nc in _tap_chunks(tt, D):
            cols = pl.ds(c0, nc)
            acc = jnp.zeros((nr, nc), F32) + dwb_ref[:, cols]
            for k in range(KW):
                acc = acc + dw_ref[pl.ds(k, 1), cols] * _rows_at(ext_ref, sh_ref, r0 + lead + k, nr, cols)
            c_ref[pl.ds(r0, nr), cols] = acc
        c = c_ref[...]
        mu = jnp.mean(c, axis=-1, keepdims=True)
        cen = c - mu
        var = jnp.mean(cen * cen, axis=-1, keepdims=True)
        n = cen * lax.rsqrt(var + LN_EPS) * lng_ref[...] + lnb_ref[...]
        s_ref[...] = (n * _sigmoid(n)).astype(BF16)

    halo = lambda col: pl.BlockSpec((CONV_HALO, D), lambda i: (jnp.maximum(i * hb - 1, 0), col))
    return pl.pallas_call(
        body, name=name, grid=(T // tt,),
        in_specs=[pl.BlockSpec((tt, D), lambda i: (i, 0)), pl.BlockSpec((tt, D), lambda i: (i, 1)), halo(0), halo(1),
                  pl.BlockSpec((KW, D), lambda i: (0, 0)), _vec_spec(D), _vec_spec(D), _vec_spec(D)],
        out_specs=[_row_spec(tt, D)] * 3,
        out_shape=[jax.ShapeDtypeStruct((T, D), F32), jax.ShapeDtypeStruct((T, D), F32), jax.ShapeDtypeStruct((T, D), BF16)],
        scratch_shapes=[pltpu.VMEM((tt + CONV_HALO, D), F32),
                        pltpu.VMEM((SUBLANES - 1, tt + CONV_HALO - SUBLANES, D), F32)],
        compiler_params=_cparams(("parallel",)),
    )(h1, h1, h1, h1, dw, dwb, lng, lnb)


def conv_bwd_ln(name, ds, c, lng, lnb):
    T, D = c.shape
    tr = min(ROW_TILE, T)
    nt = T // tr

    def body(ds_ref, c_ref, g_ref, b_ref, dc_ref, dg_ref, db_ref, cs_ref, acc_ref):
        i = pl.program_id(0)
        cv = c_ref[...]
        mu = jnp.mean(cv, axis=-1, keepdims=True)
        cen = cv - mu
        var = jnp.mean(cen * cen, axis=-1, keepdims=True)
        rstd = lax.rsqrt(var + LN_EPS)
        chat = cen * rstd
        n = chat * g_ref[...] + b_ref[...]
        sg = _sigmoid(n)
        dn = ds_ref[...] * (sg * (1.0 + n * (1.0 - sg)))
        dxh = dn * g_ref[...]
        m1 = jnp.mean(dxh, axis=-1, keepdims=True)
        m2 = jnp.mean(dxh * chat, axis=-1, keepdims=True)
        dc = rstd * (dxh - m1 - chat * m2)
        dc_ref[...] = dc

        @pl.when(i == 0)
        def _():
            acc_ref[...] = jnp.zeros_like(acc_ref)

        acc_ref[0] += _fold8(dn * chat)
        acc_ref[1] += _fold8(dn)
        acc_ref[2] += _fold8(dc)

        @pl.when(i == nt - 1)
        def _():
            dg_ref[...] = jnp.sum(acc_ref[0], axis=0, keepdims=True)
            db_ref[...] = jnp.sum(acc_ref[1], axis=0, keepdims=True)
            cs_ref[...] = jnp.sum(acc_ref[2], axis=0, keepdims=True)

    return pl.pallas_call(
        body, name=name, grid=(nt,),
        in_specs=[_row_spec(tr, D), _row_spec(tr, D), _vec_spec(D), _vec_spec(D)],
        out_specs=[_row_spec(tr, D), _vec_spec(D), _vec_spec(D), _vec_spec(D)],
        out_shape=[jax.ShapeDtypeStruct((T, D), F32)] + [jax.ShapeDtypeStruct((1, D), F32)] * 3,
        scratch_shapes=[pltpu.VMEM((3, 8, D), F32)],
        compiler_params=_cparams(("arbitrary",)),
    )(ds, c, lng, lnb)


def conv_bwd_taps(name, dc, u, h1, dw):
    T, D = dc.shape
    tt = min(CONV_TILE, T)
    nt = T // tt
    hb = tt // CONV_HALO
    nhb = T // CONV_HALO
    KW = dw.shape[0]
    lead = CONV_HALO - (KW - 1)

    def body(dc_ref, dcn_ref, u_ref, uh_ref, a_ref, g_ref, dw_ref, dh1_ref, db1_ref, ddw_ref,
             edc_ref, eu_ref, du_ref, accw_ref, accb_ref, shdc_ref, shu_ref):
        i = pl.program_id(0)

        @pl.when(i == 0)
        def _():
            accw_ref[...] = jnp.zeros_like(accw_ref)
            accb_ref[...] = jnp.zeros_like(accb_ref)

        edc_ref[pl.ds(0, tt), :] = dc_ref[...]
        edc_ref[pl.ds(tt, CONV_HALO), :] = jnp.where(i < nt - 1, dcn_ref[...], 0.0)
        eu_ref[pl.ds(0, CONV_HALO), :] = jnp.where(i > 0, uh_ref[...], 0.0)
        eu_ref[pl.ds(CONV_HALO, tt), :] = u_ref[...]
        _shifted_copies(edc_ref, shdc_ref)
        _shifted_copies(eu_ref, shu_ref)
        for r0, nr, c0, nc in _tap_chunks(tt, D):
            cols = pl.ds(c0, nc)
            dcv = dc_ref[pl.ds(r0, nr), cols]
            acc = jnp.zeros((nr, nc), F32)
            for k in range(KW):
                acc = acc + dw_ref[pl.ds(k, 1), cols] * _rows_at(edc_ref, shdc_ref, r0 + (KW - 1) - k, nr, cols)
                accw_ref[k, :, cols] += _fold8(dcv * _rows_at(eu_ref, shu_ref, r0 + lead + k, nr, cols))
            du_ref[pl.ds(r0, nr), cols] = acc
        du = du_ref[...]
        sg = _sigmoid(g_ref[...])
        da = du * sg
        dg = du * a_ref[...] * sg * (1.0 - sg)
        dh1_ref[:, pl.ds(0, D)] = da.astype(BF16)
        dh1_ref[:, pl.ds(D, D)] = dg.astype(BF16)
        accb_ref[:, pl.ds(0, D)] += _fold8(da)
        accb_ref[:, pl.ds(D, D)] += _fold8(dg)

        @pl.when(i == nt - 1)
        def _():
            db1_ref[...] = jnp.sum(accb_ref[...], axis=0, keepdims=True)
            ddw_ref[...] = jnp.sum(accw_ref[...], axis=1)

    return pl.pallas_call(
        body, name=name, grid=(nt,),
        in_specs=[_row_spec(tt, D),
                  pl.BlockSpec((CONV_HALO, D), lambda i: (jnp.minimum((i + 1) * hb, nhb - 1), 0)),
                  _row_spec(tt, D),
                  pl.BlockSpec((CONV_HALO, D), lambda i: (jnp.maximum(i * hb - 1, 0), 0)),
                  pl.BlockSpec((tt, D), lambda i: (i, 0)), pl.BlockSpec((tt, D), lambda i: (i, 1)),
                  pl.BlockSpec((KW, D), lambda i: (0, 0))],
        out_specs=[_row_spec(tt, 2 * D), _vec_spec(2 * D), pl.BlockSpec((KW, D), lambda i: (0, 0))],
        out_shape=[jax.ShapeDtypeStruct((T, 2 * D), BF16), jax.ShapeDtypeStruct((1, 2 * D), F32),
                   jax.ShapeDtypeStruct((KW, D), F32)],
        scratch_shapes=[pltpu.VMEM((tt + CONV_HALO, D), F32), pltpu.VMEM((tt + CONV_HALO, D), F32),
                        pltpu.VMEM((tt, D), F32), pltpu.VMEM((KW, 8, D), F32), pltpu.VMEM((8, 2 * D), F32)]
                       + [pltpu.VMEM((SUBLANES - 1, tt + CONV_HALO - SUBLANES, D), F32)] * 2,
        compiler_params=_cparams(("arbitrary",)),
    )(dc, dc, u, u, h1, h1, dw)


def _t5_bucket(dist):
    max_exact = REL_BUCKETS // 2
    large = max_exact + (np.log(np.maximum(dist, 1) / max_exact) / math.log(REL_MAX_DIST / max_exact)
                         * (REL_BUCKETS - max_exact)).astype(np.int32)
    large = np.minimum(large, REL_BUCKETS - 1)
    return np.where(dist < max_exact, dist, large).astype(np.int32)


def _bucket_table(dil):
    i = np.arange(BAND)[:, None]
    j = np.arange(2 * BAND)[None, :]
    delta = i - j + BAND
    return _t5_bucket(np.clip(delta, 0, None) * dil)


def bias_expand(name, rel_bias, dil):
    n_heads = rel_bias.shape[1]
    idx = jnp.asarray(_bucket_table(dil))

    def body(rel_ref, idx_ref, out_ref):
        h = pl.program_id(0)
        idxv = idx_ref[...]
        b = jnp.zeros((BAND, 2 * BAND), F32)
        for bk in range(REL_BUCKETS):
            b = jnp.where(idxv == bk, rel_ref[bk, h], b)
        out_ref[...] = b

    return pl.pallas_call(
        body, name=name, grid=(n_heads,),
        in_specs=[pl.BlockSpec(memory_space=pltpu.SMEM), pl.BlockSpec((BAND, 2 * BAND), lambda h: (0, 0))],
        out_specs=pl.BlockSpec((None, BAND, 2 * BAND), lambda h: (h, 0, 0)),
        out_shape=jax.ShapeDtypeStruct((n_heads, BAND, 2 * BAND), F32),
        compiler_params=_cparams(("arbitrary",)),
    )(rel_bias, idx)


def relbias_grad(name, dsb_list):
    n_heads = dsb_list[0].shape[0]
    idxs = [jnp.asarray(_bucket_table(d)) for _, d in BRANCHES]
    nb = len(BRANCHES)

    def body(*refs):
        ds_refs, idx_refs, out_ref = refs[:nb], refs[nb:2 * nb], refs[2 * nb]
        lane = lax.broadcasted_iota(jnp.int32, (1, 128), 1)
        row = jnp.zeros((1, 128), F32)
        for bk in range(REL_BUCKETS):
            tot = jnp.zeros((1, 1), F32)
            for ds_ref, idx_ref in zip(ds_refs, idx_refs):
                sel = jnp.where(idx_ref[...] == bk, ds_ref[...], 0.0)
                tot = tot + jnp.sum(jnp.sum(sel, axis=0, keepdims=True), axis=1, keepdims=True)
            row = jnp.where(lane == bk, tot, row)
        out_ref[...] = row

    return pl.pallas_call(
        body, name=name, grid=(n_heads,),
        in_specs=[pl.BlockSpec((None, BAND, 2 * BAND), lambda h: (h, 0, 0))] * nb
                 + [pl.BlockSpec((BAND, 2 * BAND), lambda h: (0, 0))] * nb,
        out_specs=pl.BlockSpec((None, 1, 128), lambda h: (h, 0, 0)),
        out_shape=jax.ShapeDtypeStruct((n_heads, 1, 128), F32),
        compiler_params=_cparams(("arbitrary",)),
    )(*dsb_list, *idxs)


def _band_mask():
    i = lax.broadcasted_iota(jnp.int32, (BAND, 2 * BAND), 0)
    j = lax.broadcasted_iota(jnp.int32, (BAND, 2 * BAND), 1)
    return (j >= i) & (j <= i + BAND), j


def _rep2(x):
    return jnp.concatenate([x, x], axis=1)


ATTN_TOKENS = 2048
MERGE_ROWS = 256


def _rows(ref, start, n, dil):
    if dil == 1:
        return ref[pl.ds(start, n), :]
    return ref[pl.ds(start, n, stride=dil), :]


def _set_rows(ref, start, n, dil, val):
    if dil == 1:
        ref[pl.ds(start, n), :] = val
    else:
        ref[pl.ds(start, n, stride=dil), :] = val


def _attn_specs(ct, n_heads, n_chunks):
    cur = lambda col0: pl.BlockSpec((ct, HEAD_DIM), lambda h, c: (c, col0 + h))
    prev = lambda col0: pl.BlockSpec((ct, HEAD_DIM), lambda h, c: (jnp.maximum(c - 1, 0), col0 + h))
    nxt = lambda col0: pl.BlockSpec((ct, HEAD_DIM), lambda h, c: (jnp.minimum(c + 1, n_chunks - 1), col0 + h))
    bias = pl.BlockSpec((None, BAND, 2 * BAND), lambda h, c: (h, 0, 0))
    return cur, prev, nxt, bias


def _load_keys(kext_ref, vext_ref, base, k_ref, v_ref, kp_ref, vp_ref, r, dil, ct):
    lc = ct // dil
    kext_ref[pl.ds(base, BAND), :] = _rows(kp_ref, ct - BAND * dil + r, BAND, dil).astype(BF16)
    vext_ref[pl.ds(base, BAND), :] = _rows(vp_ref, ct - BAND * dil + r, BAND, dil).astype(BF16)
    kext_ref[pl.ds(base + BAND, lc), :] = _rows(k_ref, r, lc, dil).astype(BF16)
    vext_ref[pl.ds(base + BAND, lc), :] = _rows(v_ref, r, lc, dil).astype(BF16)


ATTN_GROUP = 4


def _two_level(dil):
    if dil > ATTN_GROUP and dil % ATTN_GROUP == 0:
        return ATTN_GROUP, dil // ATTN_GROUP
    return 1, dil


def _slot_rows(ct):
    return max(ct + BAND, ATTN_GROUP * (ct // ATTN_GROUP + BAND))


def _window_mask(band, jcol, a, c):
    if a > 0:
        return band
    return band & jnp.logical_or(jcol >= BAND, c > 0)


def attn_fwd(name, q, kv, biases):
    T, D = q.shape
    n_heads = D // HEAD_DIM
    ct = min(ATTN_TOKENS, T)
    n_chunks = T // ct
    nbr = len(BRANCHES)
    scale = HEAD_DIM ** -0.5
    nt_dims = (((1,), (1,)), ((), ()))
    nn_dims = (((1,), (0,)), ((), ()))

    n_in = 5

    def body(*refs):
        ins = refs[:n_in]
        b_refs = refs[n_in:n_in + nbr]
        o_ref, obf_ref, lse_ref = refs[n_in + nbr:n_in + nbr + 3]
        kext_ref, vext_ref, acc_ref, m_ref, l_ref = refs[n_in + nbr + 3:n_in + nbr + 8]
        tmp_in = refs[n_in + nbr + 8:n_in + nbr + 8 + n_in]
        tmp_out = refs[n_in + nbr + 8 + n_in:]
        c = pl.program_id(1)
        band, jcol = _band_mask()

        def residue(src, dst, slot, r, dil, cte, bias_v):
            q_ref, k_ref, v_ref, kp_ref, vp_ref = src
            lc = cte // dil
            base = slot * (BAND + lc)
            _load_keys(kext_ref, vext_ref, base, k_ref, v_ref, kp_ref, vp_ref, r, dil, cte)
            for a in range(lc // BAND):
                tok = r + a * BAND * dil
                qa = _rows(q_ref, tok, BAND, dil).astype(BF16)
                kw = kext_ref[pl.ds(base + a * BAND, 2 * BAND), :]
                vw = vext_ref[pl.ds(base + a * BAND, 2 * BAND), :]
                s = lax.dot_general(qa, kw, nt_dims, preferred_element_type=F32) * scale + bias_v
                s = jnp.where(_window_mask(band, jcol, a, c), s, NEG_BIG)
                m = jnp.max(s, axis=-1, keepdims=True)
                p = jnp.exp(s - m)
                den = jnp.sum(p, axis=-1, keepdims=True)
                pv = lax.dot_general(p.astype(BF16), vw, nn_dims, preferred_element_type=F32)
                _set_rows(dst[0], tok, BAND, dil, pv)
                _set_rows(dst[1], tok, BAND, dil, jnp.broadcast_to(m, (BAND, HEAD_DIM)))
                _set_rows(dst[2], tok, BAND, dil, jnp.broadcast_to(den, (BAND, HEAD_DIM)))

        for bi, (win, dil) in enumerate(BRANCHES):
            bias_v = b_refs[bi][...]
            dst = (acc_ref.at[bi], m_ref.at[bi], l_ref.at[bi])
            outer, inner = _two_level(dil)
            if outer == 1:
                for r in range(dil):
                    residue(ins, dst, r % ATTN_GROUP, r, dil, ct, bias_v)
            else:
                cte = ct // outer

                def group(r1, carry, bias_v=bias_v, dst=dst, outer=outer, inner=inner, cte=cte):
                    for t_ref, x_ref in zip(tmp_in, ins):
                        t_ref[...] = _rows(x_ref, r1, cte, outer)
                    for r2 in range(inner):
                        residue(tmp_in, tmp_out, r2 % ATTN_GROUP, r2, inner, cte, bias_v)
                    for t_ref, d_ref in zip(tmp_out, dst):
                        _set_rows(d_ref, r1, cte, outer, t_ref[...])
                    return carry

                lax.fori_loop(0, outer, group, 0)

        def merge(i, carry):
            rows = pl.ds(pl.multiple_of(i * MERGE_ROWS, MERGE_ROWS), MERGE_ROWS)
            ms = [m_ref[bi, rows, :] for bi in range(nbr)]
            m = functools.reduce(jnp.maximum, ms)
            ws = [jnp.exp(mb - m) for mb in ms]
            tot = functools.reduce(lambda x, y: x + y, [w * l_ref[bi, rows, :] for bi, w in enumerate(ws)])
            o = functools.reduce(lambda x, y: x + y, [w * acc_ref[bi, rows, :] for bi, w in enumerate(ws)]) / tot
            o_ref[rows, :] = o
            obf_ref[rows, :] = o.astype(BF16)
            lse_ref[rows, :] = m + jnp.log(tot)
            return carry

        lax.fori_loop(0, ct // min(MERGE_ROWS, ct), merge, 0)

    cur, prev, nxt, bias = _attn_specs(ct, n_heads, n_chunks)
    small = (ct // ATTN_GROUP, HEAD_DIM)
    return pl.pallas_call(
        body, name=name, grid=(n_heads, n_chunks),
        in_specs=[cur(0), cur(0), cur(n_heads), prev(0), prev(n_heads)] + [bias] * nbr,
        out_specs=[cur(0)] * 3,
        out_shape=[jax.ShapeDtypeStruct((T, D), F32), jax.ShapeDtypeStruct((T, D), BF16), jax.ShapeDtypeStruct((T, D), F32)],
        scratch_shapes=[pltpu.VMEM((_slot_rows(ct), HEAD_DIM), BF16)] * 2 + [pltpu.VMEM((nbr, ct, HEAD_DIM), F32)] * 3
                       + [pltpu.VMEM(small, F32)] * (n_in + 3),
        compiler_params=_cparams(("arbitrary", "arbitrary")),
    )(q, kv, kv, kv, kv, *biases)


def attn_bwd(name, q, kv, do, o, lse, biases):
    T, D = q.shape
    n_heads = D // HEAD_DIM
    ct = min(ATTN_TOKENS, T)
    n_chunks = T // ct
    nbr = len(BRANCHES)
    scale = HEAD_DIM ** -0.5
    nt_dims = (((1,), (1,)), ((), ()))
    tn_dims = (((0,), (0,)), ((), ()))
    nn_dims = (((1,), (0,)), ((), ()))
    mrows = min(MERGE_ROWS, ct)
    n_src = 12

    def body(q_ref, k_ref, v_ref, do_ref, o_ref, lse_ref, kp_ref, vp_ref, qn_ref, don_ref, on_ref, lsen_ref, *rest):
        b_refs = rest[:nbr]
        dq_ref, dk_ref, dv_ref = rest[nbr:nbr + 3]
        dsb_refs = rest[nbr + 3:2 * nbr + 3]
        sc = rest[2 * nbr + 3:]
        kext_ref, vext_ref, dkext_ref, dvext_ref, dqa_ref, dka_ref, dva_ref, dsum_ref, dsumn_ref = sc[:9]
        tmp_in = sc[9:9 + n_src]
        tmp_acc = sc[9 + n_src:]
        c = pl.program_id(1)
        band, jcol = _band_mask()

        def prep(i, carry):
            rows = pl.ds(pl.multiple_of(i * mrows, mrows), mrows)
            dsum_ref[rows, :] = jnp.broadcast_to(jnp.sum(do_ref[rows, :] * o_ref[rows, :], axis=-1, keepdims=True), (mrows, HEAD_DIM))
            dsumn_ref[rows, :] = jnp.broadcast_to(jnp.sum(don_ref[rows, :] * on_ref[rows, :], axis=-1, keepdims=True), (mrows, HEAD_DIM))
            dqa_ref[rows, :] = jnp.zeros((mrows, HEAD_DIM), F32)
            dka_ref[rows, :] = jnp.zeros((mrows, HEAD_DIM), F32)
            dva_ref[rows, :] = jnp.zeros((mrows, HEAD_DIM), F32)
            return carry

        lax.fori_loop(0, ct // mrows, prep, 0)

        @pl.when(c == 0)
        def _():
            for r in dsb_refs:
                r[...] = jnp.zeros_like(r)

        def residue(src, acc, slot, r, dil, cte, bias_v, dsb_ref):
            sq, sk, sv, sdo, slse, sdsum, skp, svp, sqn, sdon, slsen, sdsumn = src
            adq, adk, adv = acc
            lc = cte // dil
            base = slot * (BAND + lc)
            _load_keys(kext_ref, vext_ref, base, sk, sv, skp, svp, r, dil, cte)
            dkext_ref[pl.ds(base, BAND + lc), :] = jnp.zeros((BAND + lc, HEAD_DIM), F32)
            dvext_ref[pl.ds(base, BAND + lc), :] = jnp.zeros((BAND + lc, HEAD_DIM), F32)
            for a in range(lc // BAND):
                tok = r + a * BAND * dil
                qa = _rows(sq, tok, BAND, dil).astype(BF16)
                doa = _rows(sdo, tok, BAND, dil).astype(BF16)
                kw = kext_ref[pl.ds(base + a * BAND, 2 * BAND), :]
                vw = vext_ref[pl.ds(base + a * BAND, 2 * BAND), :]
                s = lax.dot_general(qa, kw, nt_dims, preferred_element_type=F32) * scale + bias_v
                p = jnp.where(_window_mask(band, jcol, a, c), jnp.exp(s - _rep2(_rows(slse, tok, BAND, dil))), 0.0)
                dp = lax.dot_general(doa, vw, nt_dims, preferred_element_type=F32)
                ds = p * (dp - _rep2(_rows(sdsum, tok, BAND, dil)))
                dsb_ref[...] += ds
                dsb16 = ds.astype(BF16)
                dqw = lax.dot_general(dsb16, kw, nn_dims, preferred_element_type=F32) * scale
                _set_rows(adq, tok, BAND, dil, _rows(adq, tok, BAND, dil) + dqw)
                dkext_ref[pl.ds(base + a * BAND, 2 * BAND), :] += lax.dot_general(dsb16, qa, tn_dims, preferred_element_type=F32) * scale
                dvext_ref[pl.ds(base + a * BAND, 2 * BAND), :] += lax.dot_general(p.astype(BF16), doa, tn_dims, preferred_element_type=F32)

            @pl.when(c < n_chunks - 1)
            def _():
                qn = _rows(sqn, r, BAND, dil).astype(BF16)
                don = _rows(sdon, r, BAND, dil).astype(BF16)
                kl = kext_ref[pl.ds(base + lc, BAND), :]
                vl = vext_ref[pl.ds(base + lc, BAND), :]
                s = lax.dot_general(qn, kl, nt_dims, preferred_element_type=F32) * scale + bias_v[:, :BAND]
                p = jnp.where(band[:, :BAND], jnp.exp(s - _rows(slsen, r, BAND, dil)), 0.0)
                dp = lax.dot_general(don, vl, nt_dims, preferred_element_type=F32)
                ds = p * (dp - _rows(sdsumn, r, BAND, dil))
                dkext_ref[pl.ds(base + lc, BAND), :] += lax.dot_general(ds.astype(BF16), qn, tn_dims, preferred_element_type=F32) * scale
                dvext_ref[pl.ds(base + lc, BAND), :] += lax.dot_general(p.astype(BF16), don, tn_dims, preferred_element_type=F32)

            _set_rows(adk, r, lc, dil, _rows(adk, r, lc, dil) + dkext_ref[pl.ds(base + BAND, lc), :])
            _set_rows(adv, r, lc, dil, _rows(adv, r, lc, dil) + dvext_ref[pl.ds(base + BAND, lc), :])

        full_src = (q_ref, k_ref, v_ref, do_ref, lse_ref, dsum_ref, kp_ref, vp_ref, qn_ref, don_ref, lsen_ref, dsumn_ref)
        full_acc = (dqa_ref, dka_ref, dva_ref)
        for bi, (win, dil) in enumerate(BRANCHES):
            bias_v = b_refs[bi][...]
            outer, inner = _two_level(dil)
            if outer == 1:
                for r in range(dil):
                    residue(full_src, full_acc, r % ATTN_GROUP, r, dil, ct, bias_v, dsb_refs[bi])
            else:
                cte = ct // outer

                def group(r1, carry, bias_v=bias_v, dsb_ref=dsb_refs[bi], outer=outer, inner=inner, cte=cte):
                    for t_ref, x_ref in zip(tmp_in, full_src):
                        t_ref[...] = _rows(x_ref, r1, cte, outer)
                    for t_ref in tmp_acc:
                        t_ref[...] = jnp.zeros_like(t_ref)
                    for r2 in range(inner):
                        residue(tmp_in, tmp_acc, r2 % ATTN_GROUP, r2, inner, cte, bias_v, dsb_ref)
                    for t_ref, a_ref in zip(tmp_acc, full_acc):
                        _set_rows(a_ref, r1, cte, outer, _rows(a_ref, r1, cte, outer) + t_ref[...])
                    return carry

                lax.fori_loop(0, outer, group, 0)

        dq_ref[...] = dqa_ref[...].astype(BF16)
        dk_ref[...] = dka_ref[...].astype(BF16)
        dv_ref[...] = dva_ref[...].astype(BF16)

    cur, prev, nxt, bias = _attn_specs(ct, n_heads, n_chunks)
    small = (ct // ATTN_GROUP, HEAD_DIM)
    res = pl.pallas_call(
        body, name=name, grid=(n_heads, n_chunks),
        in_specs=[cur(0), cur(0), cur(n_heads), cur(0), cur(0), cur(0), prev(0), prev(n_heads), nxt(0), nxt(0), nxt(0), nxt(0)]
                 + [bias] * nbr,
        out_specs=[cur(0)] * 3 + [bias] * nbr,
        out_shape=[jax.ShapeDtypeStruct((T, D), BF16)] * 3 + [jax.ShapeDtypeStruct((n_heads, BAND, 2 * BAND), F32)] * nbr,
        scratch_shapes=[pltpu.VMEM((_slot_rows(ct), HEAD_DIM), BF16)] * 2 + [pltpu.VMEM((_slot_rows(ct), HEAD_DIM), F32)] * 2
                       + [pltpu.VMEM((ct, HEAD_DIM), F32)] * 5 + [pltpu.VMEM(small, F32)] * (n_src + 3),
        compiler_params=_cparams(("arbitrary", "arbitrary")),
    )(q, kv, kv, do, o, lse, kv, kv, q, do, o, lse, *biases)
    return res[0], res[1], res[2], list(res[3:])


def _divisor_tile(n, cap, mult):
    if n <= cap:
        return n
    t = cap - cap % mult
    while n % t:
        t -= mult
    return t


def _tile2(R, C):
    return _divisor_tile(R, 512, 8), _divisor_tile(C, 1024, 128)


def half_cast(name, dw, core):
    S, R, C = dw.shape
    hr = R // 2
    tr, tc = _tile2(hr, C)
    nrb = hr // tr

    def body(c_ref, x_ref, o_ref):
        o_ref[...] = x_ref[...].astype(BF16)

    return pl.pallas_call(
        body, name=name,
        grid_spec=pltpu.PrefetchScalarGridSpec(
            num_scalar_prefetch=1, grid=(S, nrb, C // tc),
            in_specs=[pl.BlockSpec((None, tr, tc), lambda s, i, j, c: (s, (1 - c[0]) * nrb + i, j))],
            out_specs=pl.BlockSpec((None, tr, tc), lambda s, i, j, c: (s, i, j))),
        out_shape=jax.ShapeDtypeStruct((S, hr, C), BF16),
        compiler_params=_cparams(("parallel", "parallel", "parallel")),
    )(core, dw)


def pair_sum(name, dw, recv, core):
    S, R, C = dw.shape
    hr = R // 2
    tr, tc = _tile2(hr, C)
    nrb = hr // tr

    def body(c_ref, x_ref, r_ref, p_ref, pbf_ref):
        p = x_ref[...] + r_ref[...].astype(F32)
        p_ref[...] = p
        pbf_ref[...] = p.astype(BF16)

    out = pl.BlockSpec((None, tr, tc), lambda s, i, j, c: (s, i, j))
    return pl.pallas_call(
        body, name=name,
        grid_spec=pltpu.PrefetchScalarGridSpec(
            num_scalar_prefetch=1, grid=(S, nrb, C // tc),
            in_specs=[pl.BlockSpec((None, tr, tc), lambda s, i, j, c: (s, c[0] * nrb + i, j)), out],
            out_specs=[out, out]),
        out_shape=[jax.ShapeDtypeStruct((S, hr, C), F32), jax.ShapeDtypeStruct((S, hr, C), BF16)],
        compiler_params=_cparams(("parallel", "parallel", "parallel")),
    )(core, dw, recv)


def chip_sum(name, p, recv, chip, core):
    S, hr, C = p.shape
    tr, tc = _tile2(hr, C)
    nrb = hr // tr

    def body(chip_ref, core_ref, p_ref, r_ref, o_ref):
        acc = p_ref[...]
        for t in range(N_CHIPS - 1):
            acc = acc + r_ref[t].astype(F32)
        o_ref[...] = acc

    return pl.pallas_call(
        body, name=name,
        grid_spec=pltpu.PrefetchScalarGridSpec(
            num_scalar_prefetch=2, grid=(nrb, C // tc),
            in_specs=[pl.BlockSpec((None, tr, tc), lambda i, j, s, c: (s[0], i, j)),
                      pl.BlockSpec((N_CHIPS - 1, tr, tc), lambda i, j, s, c: (0, i, j))],
            out_specs=pl.BlockSpec((tr, tc), lambda i, j, s, c: (c[0] * nrb + i, j))),
        out_shape=jax.ShapeDtypeStruct((2 * hr, C), F32),
        compiler_params=_cparams(("parallel", "parallel")),
    )(chip, core, p, recv)


def adamw(name, w, g, m, v):
    R, C = w.shape
    tr, tc = _tile2(R, C)
    c1 = 1.0 - ADAM_B1 ** ADAM_STEP
    c2 = 1.0 - ADAM_B2 ** ADAM_STEP

    def body(w_ref, g_ref, m_ref, v_ref, d_ref, nm_ref, nv_ref):
        gv = g_ref[...]
        nm = ADAM_B1 * m_ref[...] + (1.0 - ADAM_B1) * gv
        nv = ADAM_B2 * v_ref[...] + (1.0 - ADAM_B2) * (gv * gv)
        nm_ref[...] = nm
        nv_ref[...] = nv
        d_ref[...] = -ADAM_LR * ((nm / c1) / (jnp.sqrt(nv / c2) + ADAM_EPS) + ADAM_WD * w_ref[...])

    spec = pl.BlockSpec((tr, tc), lambda i, j: (i, j))
    return pl.pallas_call(
        body, name=name, grid=(R // tr, C // tc), in_specs=[spec] * 4, out_specs=[spec] * 3,
        out_shape=[jax.ShapeDtypeStruct((R, C), F32)] * 3,
        compiler_params=_cparams(("parallel", "parallel")),
    )(w, g, m, v)


def adamw_layers(name, w, g_layers, m, v):
    nl, R, C = w.shape
    tr, tc = _tile2(R, C)
    ni, nj = R // tr, C // tc
    c1 = 1.0 - ADAM_B1 ** ADAM_STEP
    c2 = 1.0 - ADAM_B2 ** ADAM_STEP

    def body(w_ref, *rest):
        g_refs = rest[:nl]
        m_ref, v_ref, g_ref, d_ref, nm_ref, nv_ref = rest[nl:]
        layer = pl.program_id(0)
        gv = g_refs[0][...]
        for l in range(1, nl):
            gv = jnp.where(layer == l, g_refs[l][...], gv)
        nm = ADAM_B1 * m_ref[...] + (1.0 - ADAM_B1) * gv
        nv = ADAM_B2 * v_ref[...] + (1.0 - ADAM_B2) * (gv * gv)
        g_ref[...] = gv
        nm_ref[...] = nm
        nv_ref[...] = nv
        d_ref[...] = -ADAM_LR * ((nm / c1) / (jnp.sqrt(nv / c2) + ADAM_EPS) + ADAM_WD * w_ref[...])

    def g_spec(l):
        def index(layer, i, j):
            return (jnp.where(layer == l, i, jnp.where(layer < l, 0, ni - 1)),
                    jnp.where(layer == l, j, jnp.where(layer < l, 0, nj - 1)))
        return pl.BlockSpec((tr, tc), index)

    spec = pl.BlockSpec((None, tr, tc), lambda layer, i, j: (layer, i, j))
    return pl.pallas_call(
        body, name=name, grid=(nl, ni, nj),
        in_specs=[spec] + [g_spec(l) for l in range(nl)] + [spec] * 2, out_specs=[spec] * 4,
        out_shape=[jax.ShapeDtypeStruct((nl, R, C), F32)] * 4,
        compiler_params=_cparams(("arbitrary", "arbitrary", "arbitrary")),
    )(w, *g_layers, m, v)


def sum_devices(name, gathered):
    n, R, C = gathered.shape

    def body(x_ref, o_ref):
        acc = x_ref[0]
        for d in range(1, n):
            acc = acc + x_ref[d]
        o_ref[...] = acc

    return pl.pallas_call(
        body, name=name, in_specs=[pl.BlockSpec(memory_space=pltpu.VMEM)],
        out_specs=pl.BlockSpec(memory_space=pltpu.VMEM),
        out_shape=jax.ShapeDtypeStruct((R, C), F32),
    )(gathered)


def _place():
    x, y, c = lax.axis_index("x"), lax.axis_index("y"), lax.axis_index("c")
    return x, y, c


def _other_chips(x, y):
    return [(1 - x, y), (x, 1 - y), (1 - x, 1 - y)]


def all_gather8(name, block):
    R, C = block.shape

    def body(x_ref, out_ref, send_sems, recv_sems, local_sem):
        x, y, c = _place()
        me, sibling = (x, y, c), (x, y, 1 - c)
        chips = _other_chips(x, y)

        def rows(px, py, pc):
            return out_ref.at[4 * px + 2 * py + pc]

        def copy(k, blk, to, src=None):
            return pltpu.make_async_remote_copy(
                src_ref=rows(*blk) if src is None else src, dst_ref=rows(*blk),
                send_sem=send_sems.at[k], recv_sem=recv_sems.at[k], device_id=to, device_id_type=MESH)

        mine = pltpu.make_async_copy(x_ref, rows(*me), local_sem)
        mine.start()
        first = [copy(0, me, sibling, src=x_ref)]
        first += [copy(1 + j, me, (*chip, c), src=x_ref) for j, chip in enumerate(chips)]
        for cp in first:
            cp.start()
        passed = [copy(4 + j, (*chip, c), sibling) for j, chip in enumerate(chips)]
        for j, chip in enumerate(chips):
            copy(1 + j, (*chip, c), me).wait_recv()
            passed[j].start()
        copy(0, sibling, me).wait_recv()
        for j, chip in enumerate(chips):
            copy(4 + j, (*chip, 1 - c), me).wait_recv()
        for cp in first + passed:
            cp.wait_send()
        mine.wait()

    return pl.pallas_call(
        body, name=name, out_shape=jax.ShapeDtypeStruct((N_DEV, R, C), block.dtype),
        in_specs=[pl.BlockSpec(memory_space=pltpu.VMEM)], out_specs=pl.BlockSpec(memory_space=pltpu.VMEM),
        scratch_shapes=[pltpu.SemaphoreType.DMA((7,)), pltpu.SemaphoreType.DMA((7,)), pltpu.SemaphoreType.DMA],
    )(block)


_HBM = pl.BlockSpec(memory_space=pltpu.HBM)
_SEM = pl.BlockSpec(memory_space=pltpu.SEMAPHORE)
_DATAFLOW = pltpu.SideEffectType.DATAFLOW_SIDE_EFFECTING


def _in_hbm(a):
    return pltpu.with_memory_space_constraint(a, pltpu.HBM)


def split_start(name, srcs, lands, n_sem, plan):
    ns, nl = len(srcs), len(lands)

    def body(*refs):
        src, land = refs[:ns], refs[ns:ns + nl]
        send_sems, recv_sems = refs[ns + nl], refs[ns + nl + 1]
        token = refs[-1]
        outgoing, _ = plan(src, land, send_sems, recv_sems)
        for cp in outgoing:
            cp.start()
        token[...] = jnp.zeros_like(token)

    bufs = list(srcs) + list(lands)
    res = pl.pallas_call(
        body, name=name,
        out_shape=(pltpu.SemaphoreType.DMA((n_sem,)), pltpu.SemaphoreType.DMA((n_sem,)),
                   *[pltpu.HBM(b.shape, b.dtype) for b in bufs], jax.ShapeDtypeStruct((8, 128), F32)),
        in_specs=[_HBM] * (ns + nl),
        out_specs=(_SEM, _SEM, *[_HBM] * (ns + nl), pl.BlockSpec(memory_space=pltpu.VMEM)),
        input_output_aliases={i: 2 + i for i in range(ns + nl)},
        compiler_params=pltpu.CompilerParams(has_side_effects=_DATAFLOW),
    )(*[_in_hbm(b) for b in bufs])
    return res[0], res[1], list(res[2:2 + ns]), list(res[2 + ns:2 + ns + nl]), res[-1]


def split_wait(name, started, after, plan):
    send_sems, recv_sems, srcs, lands, _ = started
    ns, nl = len(srcs), len(lands)

    def body(*refs):
        src, land = refs[:ns], refs[ns:ns + nl]
        send, recv = refs[ns + nl], refs[ns + nl + 1]
        outgoing, incoming = plan(src, land, send, recv)
        for cp in outgoing:
            cp.wait_send()
        for cp in incoming:
            cp.wait_recv()

    bufs = list(srcs) + list(lands)
    res = pl.pallas_call(
        body, name=name,
        out_shape=tuple(pltpu.HBM(b.shape, b.dtype) for b in bufs),
        in_specs=[_HBM] * (ns + nl) + [_SEM, _SEM, pl.BlockSpec(memory_space=pl.ANY)],
        out_specs=tuple([_HBM] * (ns + nl)),
        input_output_aliases={i: i for i in range(ns + nl)},
        compiler_params=pltpu.CompilerParams(has_side_effects=_DATAFLOW),
    )(*bufs, send_sems, recv_sems, after)
    return list(res[ns:])


def _rcopy(src, dst, send_sems, ks, recv_sems, kr, device):
    return pltpu.make_async_remote_copy(src_ref=src, dst_ref=dst, send_sem=send_sems.at[ks], recv_sem=recv_sems.at[kr],
                                        device_id=device, device_id_type=MESH)


def _half_rows(ref, h):
    hr = ref.shape[0] // 2
    return ref.at[pl.ds(h * hr, hr)]


def _gather_plan(src, land, send_sems, recv_sems):
    x, y, c = _place()
    me_chip = 2 * x + y
    chips = _other_chips(x, y)
    outgoing, incoming = [], []
    for w, buf in enumerate(land):
        mine = _half_rows(buf.at[me_chip], c)
        for t, chip in enumerate(chips):
            slot = 2 * chip[0] + chip[1]
            for cc in range(2):
                outgoing.append(_rcopy(mine, mine, send_sems, 6 * w + 2 * t + cc, recv_sems, 6 * w + 2 * t + c, (*chip, cc)))
                theirs = _half_rows(buf.at[slot], cc)
                incoming.append(_rcopy(theirs, theirs, send_sems, 6 * w + 2 * t + cc, recv_sems, 6 * w + 2 * t + cc, (*chip, cc)))
    return outgoing, incoming


def _swap_plan(src, land, send_sems, recv_sems):
    x, y, c = _place()
    cp = _rcopy(src[0], land[0], send_sems, 0, recv_sems, 0, (x, y, 1 - c))
    return [cp], [cp]


def _scatter_plan(src, land, send_sems, recv_sems):
    x, y, c = _place()
    cps = [_rcopy(src[0].at[2 * chip[0] + chip[1]], land[0].at[t], send_sems, t, recv_sems, t, (*chip, c))
           for t, chip in enumerate(_other_chips(x, y))]
    return cps, cps


def _share_plan(src, land, send_sems, recv_sems):
    x, y, c = _place()
    mine, theirs = _half_rows(land[0], c), _half_rows(land[0], 1 - c)
    return ([_rcopy(mine, mine, send_sems, 0, recv_sems, 0, (x, y, 1 - c))],
            [_rcopy(theirs, theirs, send_sems, 0, recv_sems, 0, (x, y, 1 - c))])


def place_shard(name, shard, chip, deps=(), layer=None):
    R, C = shard.shape[-2:]
    tr, tc = _tile2(R, C)

    def body(chip_ref, x_ref, *rest):
        rest[-1][...] = x_ref[...].astype(BF16)

    if layer is None:
        src = pl.BlockSpec((tr, tc), lambda i, j, s: (i, j))
    else:
        src = pl.BlockSpec((None, tr, tc), lambda i, j, s: (layer, i, j))
    return pl.pallas_call(
        body, name=name,
        grid_spec=pltpu.PrefetchScalarGridSpec(
            num_scalar_prefetch=1, grid=(R // tr, C // tc),
            in_specs=[src] + [pl.BlockSpec(d.shape, lambda i, j, s: (0, 0)) for d in deps],
            out_specs=pl.BlockSpec((None, tr, tc), lambda i, j, s: (s[0], i, j))),
        out_shape=jax.ShapeDtypeStruct((N_CHIPS, R, C), BF16),
        compiler_params=_cparams(("parallel", "parallel")),
    )(chip, shard, *deps)


class GradExchange:
    SCATTER_TICKS = 2

    def __init__(self, chip1, core, shard, mom, vel):
        self.chip1, self.core, self.shard, self.mom, self.vel = chip1, core, shard, mom, vel
        self.inflight, self.tokens, self.results = [], [], {}

    def take_deps(self):
        deps, self.tokens = self.tokens, []
        return deps

    def _start(self, name, srcs, lands, n_sem, plan):
        started = split_start(name, srcs, lands, n_sem, plan)
        self.tokens.append(started[-1])
        return started

    def add(self, n, dw):
        S, R, C = dw.shape
        to_sibling = half_cast("rs_cast_" + n, dw, self.core)
        started = self._start("rs_swap_start_" + n, [to_sibling], [lax.empty((S, R // 2, C), BF16)], 1, _swap_plan)
        self.inflight.append(dict(n=n, dw=dw, stage=0, started=started, ticks=0))

    def tick(self, after):
        for it in self.inflight:
            n = it["n"]
            if it["stage"] == 0:
                (recv,) = split_wait("rs_swap_wait_" + n, it["started"], after, _swap_plan)
                p, pbf = pair_sum("rs_pair_sum_" + n, it["dw"], recv, self.core)
                S, hr, C = pbf.shape
                it.update(stage=1, p=p, ticks=0,
                          started=self._start("rs_scatter_start_" + n, [pbf], [lax.empty((N_CHIPS - 1, hr, C), BF16)], 3, _scatter_plan))
            elif it["stage"] == 1:
                it["ticks"] += 1
                if it["ticks"] >= self.SCATTER_TICKS:
                    (recv,) = split_wait("rs_scatter_wait_" + n, it["started"], after, _scatter_plan)
                    half = chip_sum("rs_chip_sum_" + n, it["p"], recv, self.chip1, self.core)
                    it.update(stage=2, started=self._start("rs_share_start_" + n, [], [half], 1, _share_plan))
            elif it["stage"] == 2:
                (grad,) = split_wait("rs_share_wait_" + n, it["started"], after, _share_plan)
                if n in self.shard:
                    self.results[n] = (grad,) + tuple(adamw("adamw_" + n, self.shard[n], grad, self.mom[n], self.vel[n]))
                else:
                    self.results[n] = (grad,)
                it["stage"] = 3
        self.inflight = [it for it in self.inflight if it["stage"] < 3]

    def flush(self, after):
        while self.inflight:
            self.tick(after)


def _pack(arrs):
    parts = []
    for a in arrs:
        flat = a.reshape(-1).astype(F32)
        n = flat.shape[0]
        padded = -(-n // 1024) * 1024
        parts.append(jnp.pad(flat, (0, padded - n)).reshape(padded // 128, 128))
    return jnp.concatenate(parts, axis=0)


def _unpack(buf, shapes):
    out, row = [], 0
    for shp in shapes:
        n = int(np.prod(shp))
        rows = -(-n // 1024) * 8
        out.append(buf[row:row + rows].reshape(-1)[:n].reshape(shp))
        row += rows
    return out


def _bias_epi(acc, b):
    return (acc + b,)


def local_step(x, target, W, P, ex, first_deps=()):
    T, D = x.shape
    g = {}
    plain = lambda acc: (acc,)

    (h1,) = mm_nn("pw1_fwd", x, W("pw1", x), "col", _bias_epi, [F32],
                  extras=[(P["pw1_b"], "row")] + [(d, "dep") for d in first_deps])
    u, cpre, s = conv_fwd("conv_fwd", h1, P["dw_w"], P["dw_b"], P["cln_g"], P["cln_b"])
    (mix0,) = mm_nn("pw2_fwd", s, W("pw2", s), "row", _bias_epi, [F32], extras=[(P["pw2_b"], "row")])
    ln = [None] * 4
    gam = [P["ln_mix_g"][0:1], P["ln_mlp_g"][0:1], P["ln_mix_g"][1:2], P["ln_mlp_g"][1:2]]
    bet = [P["ln_mix_b"][0:1], P["ln_mlp_b"][0:1], P["ln_mix_b"][1:2], P["ln_mlp_b"][1:2]]
    ln[0] = ln_fwd("ln0_fwd", mix0, x)(gam[0], bet[0])

    def mlp_fwd(tag, i_ln, n1, n2):
        xhat, rstd, xbf = ln[i_ln]

        def up_epi(acc):
            r = jnp.maximum(acc, 0.0)
            return r * r, r

        hid, relu = mm_nn(tag + "_up", xbf, W(n1, xbf), "col", up_epi, [BF16, BF16])
        (mlp,) = mm_nn(tag + "_down", hid, W(n2, hid), "row", plain, [F32])
        ln[i_ln + 1] = ln_fwd(tag + "_ln", mlp, xhat, gam[i_ln], bet[i_ln])(gam[i_ln + 1], bet[i_ln + 1])
        return hid, relu

    hid0 = mlp_fwd("mlp0", 0, "w1_0", "w2_0")

    x2bf = ln[1][2]
    (kv,) = mm_nn("kv_fwd", x2bf, W("kv", x2bf), "col", plain, [F32])
    (q,) = mm_nn("q_fwd", x2bf, W("wq", kv), "row", plain, [F32])
    biases = [bias_expand("bias_d%d" % d, P["rel_bias"], d) for _, d in BRANCHES]
    assert all(win // d == BAND and min(ATTN_TOKENS, T) % (BAND * d) == 0 for win, d in BRANCHES)
    o, obf, lse = attn_fwd("attn_fwd", q, kv, biases)
    (attn,) = mm_nn("wo_fwd", obf, W("wo", obf), "row", plain, [F32])
    ln[2] = ln_fwd("ln2_fwd", attn, ln[1][0], gam[1], bet[1])(gam[2], bet[2])
    hid1 = mlp_fwd("mlp1", 2, "w1_1", "w2_1")

    dr3, dr3bf, g["ln_mlp_g1"], g["ln_mlp_b1"], _, loss_sum = ln_bwd(
        "ln3_bwd", ln[3][0], ln[3][1], gam[3], target=target, beta=bet[3])

    def dw_step(name, wname, a, cot, axis):
        dw = mm_tn(name, a, cot, W(wname, a).shape, axis, deps=ex.take_deps())
        ex.tick(dw)
        ex.add(wname, dw)

    def dx_step(name, cot, wname, axis, epilogue, out_dtype, extras):
        deps = [(d, "dep") for d in ex.take_deps()]
        (out,) = mm_nt(name, cot, W(wname, cot), axis, epilogue, [out_dtype], extras=list(extras) + deps)
        ex.tick(out)
        return out

    def mlp_bwd(tag, i_ln, n1, n2, hid_relu, dr, drbf):
        xbf = ln[i_ln][2]
        hid, relu = hid_relu
        dw_step(tag + "_dw2", n2, hid, drbf, "row")
        dp = dx_step(tag + "_dhid", drbf, n2, "row", lambda acc, r: (acc * (2.0 * r.astype(F32)),), BF16, [(relu, "tile")])
        dw_step(tag + "_dw1", n1, xbf, dp, "col")
        return dx_step(tag + "_dx", dp, n1, "col", lambda acc, e: (acc + ALPHA * e,), F32, [(dr, "tile")])

    dx3 = mlp_bwd("mlp1", 2, "w1_1", "w2_1", hid1, dr3, dr3bf)
    dr2, dr2bf, g["ln_mix_g1"], g["ln_mix_b1"], _ = ln_bwd("ln2_bwd", ln[2][0], ln[2][1], gam[2], dy=dx3)
    dw_step("wo_dw", "wo", obf, dr2bf, "row")
    do = dx_step("wo_dx", dr2bf, "wo", "row", plain, F32, [])
    dq, dk, dv, dsbs = attn_bwd("attn_bwd", q, kv, do, o, lse, biases)
    g["rel_bias"] = relbias_grad("relbias_grad", dsbs)[:, 0, :REL_BUCKETS].T
    dkv = jnp.concatenate([dk, dv], axis=1)
    dw_step("wq_dw", "wq", x2bf, dq, "row")
    dw_step("kv_dw", "kv", x2bf, dkv, "col")
    dx2a = dx_step("wq_dx", dq, "wq", "row", lambda acc, e: (acc + ALPHA * e,), F32, [(dr2, "tile")])
    dx2 = dx_step("kv_dx", dkv, "kv", "col", lambda acc, e: (acc + e,), F32, [(dx2a, "tile")])

    dr1, dr1bf, g["ln_mlp_g0"], g["ln_mlp_b0"], _ = ln_bwd("ln1_bwd", ln[1][0], ln[1][1], gam[1], dy=dx2)
    dx1 = mlp_bwd("mlp0", 0, "w1_0", "w2_0", hid0, dr1, dr1bf)
    dr0, dr0bf, g["ln_mix_g0"], g["ln_mix_b0"], g["pw2_b"] = ln_bwd("ln0_bwd", ln[0][0], ln[0][1], gam[0], dy=dx1)

    dw_step("pw2_dw", "pw2", s, dr0bf, "row")
    ds = dx_step("pw2_dx", dr0bf, "pw2", "row", plain, F32, [])
    dc, g["cln_g"], g["cln_b"], g["dw_b"] = conv_bwd_ln("conv_bwd_ln", ds, cpre, P["cln_g"], P["cln_b"])
    dh1, g["pw1_b"], g["dw_w"] = conv_bwd_taps("conv_bwd_taps", dc, u, h1, P["dw_w"])
    dw_step("pw1_dw", "pw1", x, dh1, "col")
    dx = dx_step("pw1_dx", dh1, "pw1", "col", lambda acc, e: (acc + ALPHA * e,), F32, [(dr0, "tile")])
    return loss_sum, dx, g


BIG = ("pw1", "pw2", "w1_0", "w2_0", "kv", "wq", "wo", "w1_1", "w2_1")


def kernel(x, conv_pw1_w, conv_pw1_b, conv_dw_w, conv_dw_b, conv_ln_g, conv_ln_b, conv_pw2_w, conv_pw2_b, w_kv, attn_wq, attn_wo, rel_bias, mlp_w1, mlp_w2, ln_mix_g, ln_mix_b, ln_mlp_g, ln_mlp_b, loss_target, m_conv_pw1_w, m_conv_pw1_b, m_conv_dw_w, m_conv_dw_b, m_conv_ln_g, m_conv_ln_b, m_conv_pw2_w, m_conv_pw2_b, m_w_kv, m_attn_wq, m_attn_wo, m_rel_bias, m_mlp_w1, m_mlp_w2, m_ln_mix_g, m_ln_mix_b, m_ln_mlp_g, m_ln_mlp_b, v_conv_pw1_w, v_conv_pw1_b, v_conv_dw_w, v_conv_dw_b, v_conv_ln_g, v_conv_ln_b, v_conv_pw2_w, v_conv_pw2_b, v_w_kv, v_attn_wq, v_attn_wo, v_rel_bias, v_mlp_w1, v_mlp_w2, v_ln_mix_g, v_ln_mix_b, v_ln_mlp_g, v_ln_mlp_b):
    _, T, D = x.shape
    xi, yi, ci = _place()
    chip = 2 * xi + yi
    core = jnp.reshape(ci, (1,)).astype(jnp.int32)
    chip1 = jnp.reshape(chip, (1,)).astype(jnp.int32)

    def two_d(a):
        return a.reshape(a.shape[-2:])

    shard = {"pw1": two_d(conv_pw1_w), "pw2": two_d(conv_pw2_w), "kv": w_kv, "wq": two_d(attn_wq), "wo": two_d(attn_wo)}
    mom = {"pw1": two_d(m_conv_pw1_w), "pw2": two_d(m_conv_pw2_w), "kv": m_w_kv, "wq": two_d(m_attn_wq), "wo": two_d(m_attn_wo)}
    vel = {"pw1": two_d(v_conv_pw1_w), "pw2": two_d(v_conv_pw2_w), "kv": v_w_kv, "wq": two_d(v_attn_wq), "wo": two_d(v_attn_wo)}
    stacked = {"w1_0": (mlp_w1, 0), "w1_1": (mlp_w1, 1), "w2_0": (mlp_w2, 0), "w2_1": (mlp_w2, 1)}

    started = {}
    for n in BIG:
        deps = [started[prev][-1] for prev in list(started)[-1:]]
        src, layer = stacked.get(n, (shard.get(n), None))
        started[n] = split_start("gather_start_" + n, [], [place_shard("place_" + n, src, chip1, deps, layer)], 6, _gather_plan)
    gathered = {}

    def W(n, after):
        if n not in gathered:
            (gathered[n],) = split_wait("gather_wait_" + n, started[n], after, _gather_plan)
        return gathered[n]

    sharded_small = [conv_pw1_b, conv_dw_w[0], conv_dw_b, conv_ln_g, conv_ln_b, conv_pw2_b]
    sh_shapes = [a.shape for a in sharded_small]
    small_all = all_gather8("gather_small", _pack(sharded_small))
    per_chip = [_unpack(small_all[2 * j], sh_shapes) for j in range(N_CHIPS)]
    full = [jnp.concatenate([per_chip[j][i] for j in range(N_CHIPS)], axis=-1) for i in range(len(sharded_small))]
    P = dict(pw1_b=full[0], dw_w=full[1], dw_b=full[2], cln_g=full[3], cln_b=full[4], pw2_b=full[5],
             rel_bias=rel_bias, ln_mix_g=ln_mix_g, ln_mix_b=ln_mix_b, ln_mlp_g=ln_mlp_g, ln_mlp_b=ln_mlp_b)

    ex = GradExchange(chip1, core, shard, mom, vel)
    loss_sum, dx, g = local_step(x.reshape(T, D), loss_target.reshape(T, D), W, P, ex,
                                 first_deps=[started[n][-1] for n in BIG])
    loss = (0.5 / D) * lax.psum(loss_sum[0, 0], ("x", "y", "c"))

    small_names = ["pw1_b", "dw_w", "dw_b", "cln_g", "cln_b", "pw2_b", "rel_bias",
                   "ln_mix_g0", "ln_mix_g1", "ln_mix_b0", "ln_mix_b1", "ln_mlp_g0", "ln_mlp_g1", "ln_mlp_b0", "ln_mlp_b1"]
    small_grads = [g[n] for n in small_names]
    sg_shapes = [a.shape for a in small_grads]
    summed = sum_devices("small_grad_sum", all_gather8("gather_small_grads", _pack(small_grads)))
    sg = dict(zip(small_names, _unpack(summed, sg_shapes)))

    def my_cols(a, width):
        return lax.dynamic_slice_in_dim(a, chip * width, width, axis=a.ndim - 1)

    small_g = [my_cols(sg["pw1_b"], conv_pw1_b.shape[-1]),
               my_cols(sg["dw_w"], conv_dw_w.shape[-1])[None],
               my_cols(sg["dw_b"], conv_dw_b.shape[-1]), my_cols(sg["cln_g"], conv_ln_g.shape[-1]),
               my_cols(sg["cln_b"], conv_ln_b.shape[-1]), my_cols(sg["pw2_b"], conv_pw2_b.shape[-1]),
               sg["rel_bias"],
               jnp.concatenate([sg["ln_mix_g0"], sg["ln_mix_g1"]], axis=0),
               jnp.concatenate([sg["ln_mix_b0"], sg["ln_mix_b1"]], axis=0),
               jnp.concatenate([sg["ln_mlp_g0"], sg["ln_mlp_g1"]], axis=0),
               jnp.concatenate([sg["ln_mlp_b0"], sg["ln_mlp_b1"]], axis=0)]
    small_w = [conv_pw1_b, conv_dw_w, conv_dw_b, conv_ln_g, conv_ln_b, conv_pw2_b, rel_bias, ln_mix_g, ln_mix_b, ln_mlp_g, ln_mlp_b]
    small_m = [m_conv_pw1_b, m_conv_dw_w, m_conv_dw_b, m_conv_ln_g, m_conv_ln_b, m_conv_pw2_b, m_rel_bias, m_ln_mix_g, m_ln_mix_b, m_ln_mlp_g, m_ln_mlp_b]
    small_v = [v_conv_pw1_b, v_conv_dw_w, v_conv_dw_b, v_conv_ln_g, v_conv_ln_b, v_conv_pw2_b, v_rel_bias, v_ln_mix_g, v_ln_mix_b, v_ln_mlp_g, v_ln_mlp_b]
    sw_shapes = [a.shape for a in small_w]
    small_g = [a.reshape(s) for a, s in zip(small_g, sw_shapes)]
    upd_small = adamw("adamw_small", _pack(small_w), _pack(small_g), _pack(small_m), _pack(small_v))
    sd, snm, snv = (_unpack(b, sw_shapes) for b in upd_small)

    ex.flush(upd_small[0])
    res_w1 = adamw_layers("adamw_w1", mlp_w1, [ex.results["w1_0"][0], ex.results["w1_1"][0]], m_mlp_w1, v_mlp_w1)
    res_w2 = adamw_layers("adamw_w2", mlp_w2, [ex.results["w2_0"][0], ex.results["w2_1"][0]], m_mlp_w2, v_mlp_w2)

    def big_out(k):
        one = {n: ex.results[n][k] for n in shard}
        return dict(pw1=one["pw1"][None], pw2=one["pw2"][None], kv=one["kv"], wq=one["wq"][None], wo=one["wo"][None],
                    w1=res_w1[k], w2=res_w2[k])

    def ordered(big, small):
        return [big["pw1"], small[0], small[1], small[2], small[3], small[4], big["pw2"], small[5], big["kv"], big["wq"],
                big["wo"], small[6], big["w1"], big["w2"], small[7], small[8], small[9], small[10]]

    grads = ordered(big_out(0), small_g)
    deltas = ordered(big_out(1), sd)
    new_m = ordered(big_out(2), snm)
    new_v = ordered(big_out(3), snv)
    return (loss, dx.reshape(1, T, D), *grads, *deltas, *new_m, *new_v)
```

```python
import functools
import math

import numpy as np
import jax
import jax.numpy as jnp
from jax import lax
from jax.experimental import pallas as pl
from jax.experimental.pallas import tpu as pltpu

F32 = jnp.float32
BF16 = jnp.bfloat16

HEAD_DIM = 128
BAND = 128
BRANCHES = ((128, 1), (512, 4), (2048, 16))
CONV_WIDTH = 31
CONV_HALO = 32
REL_BUCKETS = 32
REL_MAX_DIST = 2048
DEPTH = 2
ALPHA = (2 * DEPTH) ** 0.25
LN_EPS = 1e-5
ADAM_LR, ADAM_B1, ADAM_B2, ADAM_EPS, ADAM_WD, ADAM_STEP = 0.001, 0.9, 0.999, 1e-08, 0.01, 10

N_CHIPS = 4
N_DEV = 8
MESH = pl.DeviceIdType.MESH
VMEM_LIMIT_BYTES = 56 * 1024 * 1024
MM_TM, MM_TN, MM_TK = 1024, 1024, 2048
ROW_TILE = 256
CONV_TILE = 128
NEG_BIG = -1e30


def _cparams(sem):
    return pltpu.CompilerParams(dimension_semantics=sem, vmem_limit_bytes=VMEM_LIMIT_BYTES)


def _sigmoid(x):
    return 1.0 / (1.0 + jnp.exp(-x))


def _wspec(wshape, axis, br, bc, rsel, csel):
    _, R, C = wshape
    if axis == "col":
        if bc > C:
            assert bc % C == 0, (wshape, bc)
            return pl.BlockSpec((bc // C, br, C), lambda *g: (csel(*g), rsel(*g), 0))
        nb = C // bc
        assert nb * bc == C, (wshape, bc)
        return pl.BlockSpec((None, br, bc), lambda *g: (csel(*g) // nb, rsel(*g), csel(*g) % nb))
    if br > R:
        assert br % R == 0, (wshape, br)
        return pl.BlockSpec((br // R, R, bc), lambda *g: (rsel(*g), 0, csel(*g)))
    nb = R // br
    assert nb * br == R, (wshape, br)
    return pl.BlockSpec((None, br, bc), lambda *g: (rsel(*g) // nb, rsel(*g) % nb, csel(*g)))


def _join_shards(b, axis):
    if b.ndim == 2:
        return b
    if axis == "row":
        return b.reshape(b.shape[0] * b.shape[1], b.shape[2])
    return jnp.concatenate([b[s] for s in range(b.shape[0])], axis=1)


def _split_shards(r, shape, axis):
    if len(shape) == 2:
        return r
    if axis == "row":
        return r.reshape(shape)
    return jnp.stack([r[:, s * shape[2]:(s + 1) * shape[2]] for s in range(shape[0])])


def _full_dims(wshape, axis):
    _, R, C = wshape
    return (R, N_CHIPS * C) if axis == "col" else (N_CHIPS * R, C)


def _mm_body(nk, kinds, n_out, dims, epilogue, axis):
    n_extra = len(kinds)

    def body(*refs):
        a_ref, b_ref = refs[0], refs[1]
        extra = [r for r, kind in zip(refs[2:2 + n_extra], kinds) if kind != "dep"]
        outs = refs[2 + n_extra:2 + n_extra + n_out]
        part = lax.dot_general(a_ref[...].astype(BF16), _join_shards(b_ref[...], axis).astype(BF16), (dims, ((), ())),
                               preferred_element_type=F32)

        def write(res):
            for r, o in zip(res, outs):
                o[...] = _split_shards(r, o.shape, axis).astype(o.dtype)

        if nk == 1:
            write(epilogue(part, *[e[...] for e in extra]))
            return
        acc_ref = refs[2 + n_extra + n_out]
        k = pl.program_id(2)

        @pl.when(k == 0)
        def _():
            acc_ref[...] = part

        @pl.when(k > 0)
        def _():
            acc_ref[...] += part

        @pl.when(k == nk - 1)
        def _():
            write(epilogue(acc_ref[...], *[e[...] for e in extra]))
    return body


def _long_tk(a, k_dim):
    tk = min(MM_TK, k_dim)
    if a.dtype == BF16 and k_dim >= 4 * MM_TK:
        tk = 2 * MM_TK
    return tk


def _extra_specs(extras, tm, tn):
    specs = []
    for arr, kind in extras:
        if kind == "tile":
            specs.append(pl.BlockSpec((tm, tn), lambda i, j, k: (i, j)))
        elif kind == "dep":
            specs.append(pl.BlockSpec(arr.shape, lambda i, j, k: (0, 0)))
        else:
            specs.append(pl.BlockSpec((1, tn), lambda i, j, k: (0, j)))
    return specs


def mm_nn(name, a, w, axis, epilogue, out_dtypes, extras=()):
    M, K = a.shape
    Kw, N = _full_dims(w.shape, axis)
    assert K == Kw
    tm, tn, tk = min(MM_TM, M), min(MM_TN, N), _long_tk(a, K)
    nk = K // tk
    in_specs = [pl.BlockSpec((tm, tk), lambda i, j, k: (i, k)),
                _wspec(w.shape, axis, tk, tn, lambda i, j, k: k, lambda i, j, k: j)]
    in_specs += _extra_specs(extras, tm, tn)
    body = _mm_body(nk, [kind for _, kind in extras], len(out_dtypes), ((1,), (0,)), epilogue, axis)
    return pl.pallas_call(
        body, name=name, grid=(M // tm, N // tn, nk), in_specs=in_specs,
        out_specs=[pl.BlockSpec((tm, tn), lambda i, j, k: (i, j)) for _ in out_dtypes],
        out_shape=[jax.ShapeDtypeStruct((M, N), d) for d in out_dtypes],
        scratch_shapes=[pltpu.VMEM((tm, tn), F32)] if nk > 1 else [],
        compiler_params=_cparams(("parallel", "parallel", "arbitrary")),
    )(a, w, *[e for e, _ in extras])


def mm_nt(name, g, w, axis, epilogue, out_dtypes, extras=()):
    M, N = g.shape
    K, Nw = _full_dims(w.shape, axis)
    assert N == Nw
    tm, tn, tk = min(MM_TM, M), min(MM_TN, K), min(MM_TK, N)
    nk = N // tk
    in_specs = [pl.BlockSpec((tm, tk), lambda i, j, k: (i, k)),
                _wspec(w.shape, axis, tn, tk, lambda i, j, k: j, lambda i, j, k: k)]
    in_specs += _extra_specs(extras, tm, tn)
    body = _mm_body(nk, [kind for _, kind in extras], len(out_dtypes), ((1,), (1,)), epilogue, axis)
    return pl.pallas_call(
        body, name=name, grid=(M // tm, K // tn, nk), in_specs=in_specs,
        out_specs=[pl.BlockSpec((tm, tn), lambda i, j, k: (i, j)) for _ in out_dtypes],
        out_shape=[jax.ShapeDtypeStruct((M, K), d) for d in out_dtypes],
        scratch_shapes=[pltpu.VMEM((tm, tn), F32)] if nk > 1 else [],
        compiler_params=_cparams(("parallel", "parallel", "arbitrary")),
    )(g, w, *[e for e, _ in extras])


def mm_tn(name, a, g, wshape, axis, deps=()):
    M, K = a.shape
    Mg, N = g.shape
    assert M == Mg and (K, N) == _full_dims(wshape, axis)
    tm, tn, tk = min(MM_TM, K), min(MM_TN, N), _long_tk(a, M)
    nk = M // tk
    body = _mm_body(nk, ["dep"] * len(deps), 1, ((0,), (0,)), lambda acc: (acc,), axis)
    return pl.pallas_call(
        body, name=name, grid=(K // tm, N // tn, nk),
        in_specs=[pl.BlockSpec((tk, tm), lambda i, j, k: (k, i)),
                  pl.BlockSpec((tk, tn), lambda i, j, k: (k, j))] + _extra_specs([(d, "dep") for d in deps], tm, tn),
        out_specs=[_wspec(wshape, axis, tm, tn, lambda i, j, k: i, lambda i, j, k: j)],
        out_shape=[jax.ShapeDtypeStruct(wshape, F32)],
        scratch_shapes=[pltpu.VMEM((tm, tn), F32)] if nk > 1 else [],
        compiler_params=_cparams(("parallel", "parallel", "arbitrary")),
    )(a, g, *deps)[0]


def _row_spec(tr, width):
    return pl.BlockSpec((tr, width), lambda i: (i, 0))


def _vec_spec(width):
    return pl.BlockSpec((1, width), lambda i: (0, 0))


def _fold8(x):
    r, d = x.shape
    return jnp.sum(x.reshape(r // 8, 8, d), axis=0)


def ln_fwd(name, f, prev, prev_g=None, prev_b=None):
    T, D = f.shape
    tr = min(ROW_TILE, T)
    affine = prev_g is not None

    def body(*refs):
        if affine:
            f_ref, p_ref, pg_ref, pb_ref, g_ref, b_ref, xhat_ref, rstd_ref, xbf_ref = refs
            xprev = p_ref[...] * pg_ref[...] + pb_ref[...]
        else:
            f_ref, p_ref, g_ref, b_ref, xhat_ref, rstd_ref, xbf_ref = refs
            xprev = p_ref[...]
        r = ALPHA * xprev + f_ref[...]
        mu = jnp.mean(r, axis=-1, keepdims=True)
        cen = r - mu
        var = jnp.mean(cen * cen, axis=-1, keepdims=True)
        rstd = lax.rsqrt(var + LN_EPS)
        xhat = cen * rstd
        xhat_ref[...] = xhat
        rstd_ref[...] = rstd
        xbf_ref[...] = (xhat * g_ref[...] + b_ref[...]).astype(BF16)

    def call(g, b):
        ins = [f, prev] + ([prev_g, prev_b] if affine else []) + [g, b]
        specs = [_row_spec(tr, D), _row_spec(tr, D)] + ([_vec_spec(D)] * 2 if affine else []) + [_vec_spec(D)] * 2
        return pl.pallas_call(
            body, name=name, grid=(T // tr,), in_specs=specs,
            out_specs=[_row_spec(tr, D), _row_spec(tr, 1), _row_spec(tr, D)],
            out_shape=[jax.ShapeDtypeStruct((T, D), F32), jax.ShapeDtypeStruct((T, 1), F32),
                       jax.ShapeDtypeStruct((T, D), BF16)],
            compiler_params=_cparams(("parallel",)),
        )(*ins)
    return call


def ln_bwd(name, xhat, rstd, gamma, dy=None, target=None, beta=None):
    T, D = xhat.shape
    tr = min(ROW_TILE, T)
    nt = T // tr
    head = target is not None

    def body(*refs):
        if head:
            xhat_ref, rstd_ref, g_ref, tgt_ref, b_ref = refs[:5]
            outs = refs[5:]
        else:
            xhat_ref, rstd_ref, g_ref, dy_ref = refs[:4]
            outs = refs[4:]
        dr_ref, drbf_ref, dg_ref, db_ref, cs_ref = outs[:5]
        rest = outs[5:]
        if head:
            loss_ref, acc_ref = rest
        else:
            (acc_ref,) = rest
        i = pl.program_id(0)
        xhat_v = xhat_ref[...]
        gam = g_ref[...]
        if head:
            diff = xhat_v * gam + b_ref[...] - tgt_ref[...]
            dyv = diff * (1.0 / D)
        else:
            dyv = dy_ref[...]
        dxh = dyv * gam
        m1 = jnp.mean(dxh, axis=-1, keepdims=True)
        m2 = jnp.mean(dxh * xhat_v, axis=-1, keepdims=True)
        dr = rstd_ref[...] * (dxh - m1 - xhat_v * m2)
        dr_ref[...] = dr
        drbf_ref[...] = dr.astype(BF16)

        @pl.when(i == 0)
        def _():
            acc_ref[...] = jnp.zeros_like(acc_ref)

        acc_ref[0] += _fold8(dyv * xhat_v)
        acc_ref[1] += _fold8(dyv)
        acc_ref[2] += _fold8(dr)
        if head:
            acc_ref[3] += _fold8(diff * diff)

        @pl.when(i == nt - 1)
        def _():
            dg_ref[...] = jnp.sum(acc_ref[0], axis=0, keepdims=True)
            db_ref[...] = jnp.sum(acc_ref[1], axis=0, keepdims=True)
            cs_ref[...] = jnp.sum(acc_ref[2], axis=0, keepdims=True)
            if head:
                loss_ref[...] = jnp.sum(jnp.sum(acc_ref[3], axis=0, keepdims=True), axis=1, keepdims=True)

    ins = [xhat, rstd, gamma] + ([target, beta] if head else [dy])
    specs = [_row_spec(tr, D), _row_spec(tr, 1), _vec_spec(D)] + ([_row_spec(tr, D), _vec_spec(D)] if head else [_row_spec(tr, D)])
    out_specs = [_row_spec(tr, D), _row_spec(tr, D), _vec_spec(D), _vec_spec(D), _vec_spec(D)]
    out_shape = [jax.ShapeDtypeStruct((T, D), F32), jax.ShapeDtypeStruct((T, D), BF16)] + [jax.ShapeDtypeStruct((1, D), F32)] * 3
    if head:
        out_specs.append(pl.BlockSpec((1, 1), lambda i: (0, 0)))
        out_shape.append(jax.ShapeDtypeStruct((1, 1), F32))
    return pl.pallas_call(
        body, name=name, grid=(nt,), in_specs=specs, out_specs=out_specs, out_shape=out_shape,
        scratch_shapes=[pltpu.VMEM((4, 8, D), F32)],
        compiler_params=_cparams(("arbitrary",)),
    )(*ins)


CONV_ROWS, CONV_COLS = 64, 512


def _tap_chunks(tt, D):
    for r0 in range(0, tt, min(CONV_ROWS, tt)):
        for c0 in range(0, D, min(CONV_COLS, D)):
            yield r0, min(CONV_ROWS, tt), c0, min(CONV_COLS, D)


SUBLANES = 8


def _shifted_copies(ext_ref, sh_ref):
    n = sh_ref.shape[1]
    for b in range(1, SUBLANES):
        sh_ref[b - 1] = ext_ref[pl.ds(b, n), :]


def _rows_at(ext_ref, sh_ref, off, nr, cols):
    a, b = divmod(off, SUBLANES)
    if b == 0:
        return ext_ref[pl.ds(off, nr), cols]
    return sh_ref[b - 1, pl.ds(a * SUBLANES, nr), cols]


def conv_fwd(name, h1, dw, dwb, lng, lnb):
    T, D2 = h1.shape
    D = D2 // 2
    tt = min(CONV_TILE, T)
    hb = tt // CONV_HALO
    KW = dw.shape[0]
    lead = CONV_HALO - (KW - 1)

    def body(a_ref, g_ref, ah_ref, gh_ref, dw_ref, dwb_ref, lng_ref, lnb_ref, u_ref, c_ref, s_ref, ext_ref, sh_ref):
        i = pl.program_id(0)
        u = a_ref[...] * _sigmoid(g_ref[...])
        u_ref[...] = u
        uh = ah_ref[...] * _sigmoid(gh_ref[...])
        ext_ref[pl.ds(0, CONV_HALO), :] = jnp.where(i > 0, uh, 0.0)
        ext_ref[pl.ds(CONV_HALO, tt), :] = u
        _shifted_copies(ext_ref, sh_ref)
        for r0, nr, c0, nc in _tap_chunks(tt, D):
            cols = pl.ds(c0, nc)
            acc = jnp.zeros((nr, nc), F32) + dwb_ref[:, cols]
            for k in range(KW):
                acc = acc + dw_ref[pl.ds(k, 1), cols] * _rows_at(ext_ref, sh_ref, r0 + lead + k, nr, cols)
            c_ref[pl.ds(r0, nr), cols] = acc
        c = c_ref[...]
        mu = jnp.mean(c, axis=-1, keepdims=True)
        cen = c - mu
        var = jnp.mean(cen * cen, axis=-1, keepdims=True)
        n = cen * lax.rsqrt(var + LN_EPS) * lng_ref[...] + lnb_ref[...]
        s_ref[...] = (n * _sigmoid(n)).astype(BF16)

    halo = lambda col: pl.BlockSpec((CONV_HALO, D), lambda i: (jnp.maximum(i * hb - 1, 0), col))
    return pl.pallas_call(
        body, name=name, grid=(T // tt,),
        in_specs=[pl.BlockSpec((tt, D), lambda i: (i, 0)), pl.BlockSpec((tt, D), lambda i: (i, 1)), halo(0), halo(1),
                  pl.BlockSpec((KW, D), lambda i: (0, 0)), _vec_spec(D), _vec_spec(D), _vec_spec(D)],
        out_specs=[_row_spec(tt, D)] * 3,
        out_shape=[jax.ShapeDtypeStruct((T, D), F32), jax.ShapeDtypeStruct((T, D), F32), jax.ShapeDtypeStruct((T, D), BF16)],
        scratch_shapes=[pltpu.VMEM((tt + CONV_HALO, D), F32),
                        pltpu.VMEM((SUBLANES - 1, tt + CONV_HALO - SUBLANES, D), F32)],
        compiler_params=_cparams(("parallel",)),
    )(h1, h1, h1, h1, dw, dwb, lng, lnb)


def conv_bwd_ln(name, ds, c, lng, lnb):
    T, D = c.shape
    tr = min(ROW_TILE, T)
    nt = T // tr

    def body(ds_ref, c_ref, g_ref, b_ref, dc_ref, dg_ref, db_ref, cs_ref, acc_ref):
        i = pl.program_id(0)
        cv = c_ref[...]
        mu = jnp.mean(cv, axis=-1, keepdims=True)
        cen = cv - mu
        var = jnp.mean(cen * cen, axis=-1, keepdims=True)
        rstd = lax.rsqrt(var + LN_EPS)
        chat = cen * rstd
        n = chat * g_ref[...] + b_ref[...]
        sg = _sigmoid(n)
        dn = ds_ref[...] * (sg * (1.0 + n * (1.0 - sg)))
        dxh = dn * g_ref[...]
        m1 = jnp.mean(dxh, axis=-1, keepdims=True)
        m2 = jnp.mean(dxh * chat, axis=-1, keepdims=True)
        dc = rstd * (dxh - m1 - chat * m2)
        dc_ref[...] = dc

        @pl.when(i == 0)
        def _():
            acc_ref[...] = jnp.zeros_like(acc_ref)

        acc_ref[0] += _fold8(dn * chat)
        acc_ref[1] += _fold8(dn)
        acc_ref[2] += _fold8(dc)

        @pl.when(i == nt - 1)
        def _():
            dg_ref[...] = jnp.sum(acc_ref[0], axis=0, keepdims=True)
            db_ref[...] = jnp.sum(acc_ref[1], axis=0, keepdims=True)
            cs_ref[...] = jnp.sum(acc_ref[2], axis=0, keepdims=True)

    return pl.pallas_call(
        body, name=name, grid=(nt,),
        in_specs=[_row_spec(tr, D), _row_spec(tr, D), _vec_spec(D), _vec_spec(D)],
        out_specs=[_row_spec(tr, D), _vec_spec(D), _vec_spec(D), _vec_spec(D)],
        out_shape=[jax.ShapeDtypeStruct((T, D), F32)] + [jax.ShapeDtypeStruct((1, D), F32)] * 3,
        scratch_shapes=[pltpu.VMEM((3, 8, D), F32)],
        compiler_params=_cparams(("arbitrary",)),
    )(ds, c, lng, lnb)


def conv_bwd_taps(name, dc, u, h1, dw):
    T, D = dc.shape
    tt = min(CONV_TILE, T)
    nt = T // tt
    hb = tt // CONV_HALO
    nhb = T // CONV_HALO
    KW = dw.shape[0]
    lead = CONV_HALO - (KW - 1)

    def body(dc_ref, dcn_ref, u_ref, uh_ref, a_ref, g_ref, dw_ref, dh1_ref, db1_ref, ddw_ref,
             edc_ref, eu_ref, du_ref, accw_ref, accb_ref, shdc_ref, shu_ref):
        i = pl.program_id(0)

        @pl.when(i == 0)
        def _():
            accw_ref[...] = jnp.zeros_like(accw_ref)
            accb_ref[...] = jnp.zeros_like(accb_ref)

        edc_ref[pl.ds(0, tt), :] = dc_ref[...]
        edc_ref[pl.ds(tt, CONV_HALO), :] = jnp.where(i < nt - 1, dcn_ref[...], 0.0)
        eu_ref[pl.ds(0, CONV_HALO), :] = jnp.where(i > 0, uh_ref[...], 0.0)
        eu_ref[pl.ds(CONV_HALO, tt), :] = u_ref[...]
        _shifted_copies(edc_ref, shdc_ref)
        _shifted_copies(eu_ref, shu_ref)
        for r0, nr, c0, nc in _tap_chunks(tt, D):
            cols = pl.ds(c0, nc)
            dcv = dc_ref[pl.ds(r0, nr), cols]
            acc = jnp.zeros((nr, nc), F32)
            for k in range(KW):
                acc = acc + dw_ref[pl.ds(k, 1), cols] * _rows_at(edc_ref, shdc_ref, r0 + (KW - 1) - k, nr, cols)
                accw_ref[k, :, cols] += _fold8(dcv * _rows_at(eu_ref, shu_ref, r0 + lead + k, nr, cols))
            du_ref[pl.ds(r0, nr), cols] = acc
        du = du_ref[...]
        sg = _sigmoid(g_ref[...])
        da = du * sg
        dg = du * a_ref[...] * sg * (1.0 - sg)
        dh1_ref[:, pl.ds(0, D)] = da.astype(BF16)
        dh1_ref[:, pl.ds(D, D)] = dg.astype(BF16)
        accb_ref[:, pl.ds(0, D)] += _fold8(da)
        accb_ref[:, pl.ds(D, D)] += _fold8(dg)

        @pl.when(i == nt - 1)
        def _():
            db1_ref[...] = jnp.sum(accb_ref[...], axis=0, keepdims=True)
            ddw_ref[...] = jnp.sum(accw_ref[...], axis=1)

    return pl.pallas_call(
        body, name=name, grid=(nt,),
        in_specs=[_row_spec(tt, D),
                  pl.BlockSpec((CONV_HALO, D), lambda i: (jnp.minimum((i + 1) * hb, nhb - 1), 0)),
                  _row_spec(tt, D),
                  pl.BlockSpec((CONV_HALO, D), lambda i: (jnp.maximum(i * hb - 1, 0), 0)),
                  pl.BlockSpec((tt, D), lambda i: (i, 0)), pl.BlockSpec((tt, D), lambda i: (i, 1)),
                  pl.BlockSpec((KW, D), lambda i: (0, 0))],
        out_specs=[_row_spec(tt, 2 * D), _vec_spec(2 * D), pl.BlockSpec((KW, D), lambda i: (0, 0))],
        out_shape=[jax.ShapeDtypeStruct((T, 2 * D), BF16), jax.ShapeDtypeStruct((1, 2 * D), F32),
                   jax.ShapeDtypeStruct((KW, D), F32)],
        scratch_shapes=[pltpu.VMEM((tt + CONV_HALO, D), F32), pltpu.VMEM((tt + CONV_HALO, D), F32),
                        pltpu.VMEM((tt, D), F32), pltpu.VMEM((KW, 8, D), F32), pltpu.VMEM((8, 2 * D), F32)]
                       + [pltpu.VMEM((SUBLANES - 1, tt + CONV_HALO - SUBLANES, D), F32)] * 2,
        compiler_params=_cparams(("arbitrary",)),
    )(dc, dc, u, u, h1, h1, dw)


def _t5_bucket(dist):
    max_exact = REL_BUCKETS // 2
    large = max_exact + (np.log(np.maximum(dist, 1) / max_exact) / math.log(REL_MAX_DIST / max_exact)
                         * (REL_BUCKETS - max_exact)).astype(np.int32)
    large = np.minimum(large, REL_BUCKETS - 1)
    return np.where(dist < max_exact, dist, large).astype(np.int32)


def _bucket_table(dil):
    i = np.arange(BAND)[:, None]
    j = np.arange(2 * BAND)[None, :]
    delta = i - j + BAND
    return _t5_bucket(np.clip(delta, 0, None) * dil)


def bias_expand(name, rel_bias, dil):
    n_heads = rel_bias.shape[1]
    idx = jnp.asarray(_bucket_table(dil))

    def body(rel_ref, idx_ref, out_ref):
        h = pl.program_id(0)
        idxv = idx_ref[...]
        b = jnp.zeros((BAND, 2 * BAND), F32)
        for bk in range(REL_BUCKETS):
            b = jnp.where(idxv == bk, rel_ref[bk, h], b)
        out_ref[...] = b

    return pl.pallas_call(
        body, name=name, grid=(n_heads,),
        in_specs=[pl.BlockSpec(memory_space=pltpu.SMEM), pl.BlockSpec((BAND, 2 * BAND), lambda h: (0, 0))],
        out_specs=pl.BlockSpec((None, BAND, 2 * BAND), lambda h: (h, 0, 0)),
        out_shape=jax.ShapeDtypeStruct((n_heads, BAND, 2 * BAND), F32),
        compiler_params=_cparams(("arbitrary",)),
    )(rel_bias, idx)


def relbias_grad(name, dsb_list):
    n_heads = dsb_list[0].shape[0]
    idxs = [jnp.asarray(_bucket_table(d)) for _, d in BRANCHES]
    nb = len(BRANCHES)

    def body(*refs):
        ds_refs, idx_refs, out_ref = refs[:nb], refs[nb:2 * nb], refs[2 * nb]
        lane = lax.broadcasted_iota(jnp.int32, (1, 128), 1)
        row = jnp.zeros((1, 128), F32)
        for bk in range(REL_BUCKETS):
            tot = jnp.zeros((1, 1), F32)
            for ds_ref, idx_ref in zip(ds_refs, idx_refs):
                sel = jnp.where(idx_ref[...] == bk, ds_ref[...], 0.0)
                tot = tot + jnp.sum(jnp.sum(sel, axis=0, keepdims=True), axis=1, keepdims=True)
            row = jnp.where(lane == bk, tot, row)
        out_ref[...] = row

    return pl.pallas_call(
        body, name=name, grid=(n_heads,),
        in_specs=[pl.BlockSpec((None, BAND, 2 * BAND), lambda h: (h, 0, 0))] * nb
                 + [pl.BlockSpec((BAND, 2 * BAND), lambda h: (0, 0))] * nb,
        out_specs=pl.BlockSpec((None, 1, 128), lambda h: (h, 0, 0)),
        out_shape=jax.ShapeDtypeStruct((n_heads, 1, 128), F32),
        compiler_params=_cparams(("arbitrary",)),
    )(*dsb_list, *idxs)


def _band_mask():
    i = lax.broadcasted_iota(jnp.int32, (BAND, 2 * BAND), 0)
    j = lax.broadcasted_iota(jnp.int32, (BAND, 2 * BAND), 1)
    return (j >= i) & (j <= i + BAND), j


def _rep2(x):
    return jnp.concatenate([x, x], axis=1)


ATTN_TOKENS = 2048
MERGE_ROWS = 256


def _rows(ref, start, n, dil):
    if dil == 1:
        return ref[pl.ds(start, n), :]
    return ref[pl.ds(start, n, stride=dil), :]


def _set_rows(ref, start, n, dil, val):
    if dil == 1:
        ref[pl.ds(start, n), :] = val
    else:
        ref[pl.ds(start, n, stride=dil), :] = val


def _attn_specs(ct, n_heads, chunk_of):
    cur = lambda col0: pl.BlockSpec((ct, HEAD_DIM), lambda h, s: (chunk_of(s), col0 + h))
    prev = lambda col0: pl.BlockSpec((ct, HEAD_DIM), lambda h, s: (jnp.maximum(chunk_of(s) - 1, 0), col0 + h))
    bias = pl.BlockSpec((None, BAND, 2 * BAND), lambda h, s: (h, 0, 0))
    return cur, prev, bias


def _load_keys(kext_ref, vext_ref, base, k_ref, v_ref, kp_ref, vp_ref, r, dil, ct):
    lc = ct // dil
    kext_ref[pl.ds(base, BAND), :] = _rows(kp_ref, ct - BAND * dil + r, BAND, dil).astype(BF16)
    vext_ref[pl.ds(base, BAND), :] = _rows(vp_ref, ct - BAND * dil + r, BAND, dil).astype(BF16)
    kext_ref[pl.ds(base + BAND, lc), :] = _rows(k_ref, r, lc, dil).astype(BF16)
    vext_ref[pl.ds(base + BAND, lc), :] = _rows(v_ref, r, lc, dil).astype(BF16)


ATTN_GROUP = 4


def _two_level(dil):
    if dil > ATTN_GROUP and dil % ATTN_GROUP == 0:
        return ATTN_GROUP, dil // ATTN_GROUP
    return 1, dil


def _slot_rows(ct):
    return max(ct + BAND, ATTN_GROUP * (ct // ATTN_GROUP + BAND))


def _window_mask(band, jcol, a, c):
    if a > 0:
        return band
    return band & jnp.logical_or(jcol >= BAND, c > 0)


def attn_fwd(name, q, kv, biases):
    T, D = q.shape
    n_heads = D // HEAD_DIM
    ct = min(ATTN_TOKENS, T)
    n_chunks = T // ct
    nbr = len(BRANCHES)
    scale = HEAD_DIM ** -0.5
    nt_dims = (((1,), (1,)), ((), ()))
    nn_dims = (((1,), (0,)), ((), ()))

    n_in = 5

    def body(*refs):
        ins = refs[:n_in]
        b_refs = refs[n_in:n_in + nbr]
        o_ref, obf_ref, lse_ref = refs[n_in + nbr:n_in + nbr + 3]
        kext_ref, vext_ref, acc_ref, m_ref, l_ref = refs[n_in + nbr + 3:n_in + nbr + 8]
        tmp_in = refs[n_in + nbr + 8:n_in + nbr + 8 + n_in]
        tmp_out = refs[n_in + nbr + 8 + n_in:]
        c = pl.program_id(1)
        band, jcol = _band_mask()

        def residue(src, dst, slot, r, dil, cte, bias_v):
            q_ref, k_ref, v_ref, kp_ref, vp_ref = src
            lc = cte // dil
            base = slot * (BAND + lc)
            _load_keys(kext_ref, vext_ref, base, k_ref, v_ref, kp_ref, vp_ref, r, dil, cte)
            for a in range(lc // BAND):
                tok = r + a * BAND * dil
                qa = _rows(q_ref, tok, BAND, dil).astype(BF16)
                kw = kext_ref[pl.ds(base + a * BAND, 2 * BAND), :]
                vw = vext_ref[pl.ds(base + a * BAND, 2 * BAND), :]
                s = lax.dot_general(qa, kw, nt_dims, preferred_element_type=F32) * scale + bias_v
                s = jnp.where(_window_mask(band, jcol, a, c), s, NEG_BIG)
                m = jnp.max(s, axis=-1, keepdims=True)
                p = jnp.exp(s - m)
                den = jnp.sum(p, axis=-1, keepdims=True)
                pv = lax.dot_general(p.astype(BF16), vw, nn_dims, preferred_element_type=F32)
                _set_rows(dst[0], tok, BAND, dil, pv)
                _set_rows(dst[1], tok, BAND, dil, jnp.broadcast_to(m, (BAND, HEAD_DIM)))
                _set_rows(dst[2], tok, BAND, dil, jnp.broadcast_to(den, (BAND, HEAD_DIM)))

        for bi, (win, dil) in enumerate(BRANCHES):
            bias_v = b_refs[bi][...]
            dst = (acc_ref.at[bi], m_ref.at[bi], l_ref.at[bi])
            outer, inner = _two_level(dil)
            if outer == 1:
                for r in range(dil):
                    residue(ins, dst, r % ATTN_GROUP, r, dil, ct, bias_v)
            else:
                cte = ct // outer

                def group(r1, carry, bias_v=bias_v, dst=dst, outer=outer, inner=inner, cte=cte):
                    for t_ref, x_ref in zip(tmp_in, ins):
                        t_ref[...] = _rows(x_ref, r1, cte, outer)
                    for r2 in range(inner):
                        residue(tmp_in, tmp_out, r2 % ATTN_GROUP, r2, inner, cte, bias_v)
                    for t_ref, d_ref in zip(tmp_out, dst):
                        _set_rows(d_ref, r1, cte, outer, t_ref[...])
                    return carry

                lax.fori_loop(0, outer, group, 0)

        def merge(i, carry):
            rows = pl.ds(pl.multiple_of(i * MERGE_ROWS, MERGE_ROWS), MERGE_ROWS)
            ms = [m_ref[bi, rows, :] for bi in range(nbr)]
            m = functools.reduce(jnp.maximum, ms)
            ws = [jnp.exp(mb - m) for mb in ms]
            tot = functools.reduce(lambda x, y: x + y, [w * l_ref[bi, rows, :] for bi, w in enumerate(ws)])
            o = functools.reduce(lambda x, y: x + y, [w * acc_ref[bi, rows, :] for bi, w in enumerate(ws)]) / tot
            o_ref[rows, :] = o
            obf_ref[rows, :] = o.astype(BF16)
            lse_ref[rows, :] = m + jnp.log(tot)
            return carry

        lax.fori_loop(0, ct // min(MERGE_ROWS, ct), merge, 0)

    cur, prev, bias = _attn_specs(ct, n_heads, lambda s: s)
    small = (ct // ATTN_GROUP, HEAD_DIM)
    return pl.pallas_call(
        body, name=name, grid=(n_heads, n_chunks),
        in_specs=[cur(0), cur(0), cur(n_heads), prev(0), prev(n_heads)] + [bias] * nbr,
        out_specs=[cur(0)] * 3,
        out_shape=[jax.ShapeDtypeStruct((T, D), F32), jax.ShapeDtypeStruct((T, D), BF16), jax.ShapeDtypeStruct((T, D), F32)],
        scratch_shapes=[pltpu.VMEM((_slot_rows(ct), HEAD_DIM), BF16)] * 2 + [pltpu.VMEM((nbr, ct, HEAD_DIM), F32)] * 3
                       + [pltpu.VMEM(small, F32)] * (n_in + 3),
        compiler_params=_cparams(("arbitrary", "arbitrary")),
    )(q, kv, kv, kv, kv, *biases)


def attn_bwd(name, q, kv, do, o, lse, biases):
    T, D = q.shape
    n_heads = D // HEAD_DIM
    ct = min(ATTN_TOKENS, T)
    n_chunks = T // ct
    nbr = len(BRANCHES)
    scale = HEAD_DIM ** -0.5
    nt_dims = (((1,), (1,)), ((), ()))
    tn_dims = (((0,), (0,)), ((), ()))
    nn_dims = (((1,), (0,)), ((), ()))
    mrows = min(MERGE_ROWS, ct)
    n_src = 8
    n_acc = 5

    def body(q_ref, k_ref, v_ref, do_ref, o_ref, lse_ref, kp_ref, vp_ref, *rest):
        b_refs = rest[:nbr]
        dq_ref, dk_ref, dv_ref = rest[nbr:nbr + 3]
        dsb_refs = rest[nbr + 3:2 * nbr + 3]
        sc = rest[2 * nbr + 3:]
        kext_ref, vext_ref, dkext_ref, dvext_ref, dqa_ref, dka_ref, dva_ref, dsum_ref, ck_ref, cv_ref = sc[:10]
        tmp_in = sc[10:10 + n_src]
        tmp_acc = sc[10 + n_src:]
        step = pl.program_id(1)
        c = n_chunks - 1 - step
        band, jcol = _band_mask()

        @pl.when(step == 0)
        def _():
            ck_ref[...] = jnp.zeros_like(ck_ref)
            cv_ref[...] = jnp.zeros_like(cv_ref)
            for r in dsb_refs:
                r[...] = jnp.zeros_like(r)

        def prep(i, carry):
            rows = pl.ds(pl.multiple_of(i * mrows, mrows), mrows)
            dsum_ref[rows, :] = jnp.broadcast_to(jnp.sum(do_ref[rows, :] * o_ref[rows, :], axis=-1, keepdims=True), (mrows, HEAD_DIM))
            dqa_ref[rows, :] = jnp.zeros((mrows, HEAD_DIM), F32)
            dka_ref[rows, :] = ck_ref[rows, :]
            dva_ref[rows, :] = cv_ref[rows, :]
            ck_ref[rows, :] = jnp.zeros((mrows, HEAD_DIM), F32)
            cv_ref[rows, :] = jnp.zeros((mrows, HEAD_DIM), F32)
            return carry

        lax.fori_loop(0, ct // mrows, prep, 0)

        def residue(src, acc, slot, r, dil, cte, bias_v, dsb_ref):
            sq, sk, sv, sdo, slse, sdsum, skp, svp = src
            adq, adk, adv, ack, acv = acc
            lc = cte // dil
            base = slot * (BAND + lc)
            _load_keys(kext_ref, vext_ref, base, sk, sv, skp, svp, r, dil, cte)
            dkext_ref[pl.ds(base, BAND + lc), :] = jnp.zeros((BAND + lc, HEAD_DIM), F32)
            dvext_ref[pl.ds(base, BAND + lc), :] = jnp.zeros((BAND + lc, HEAD_DIM), F32)
            for a in range(lc // BAND):
                tok = r + a * BAND * dil
                qa = _rows(sq, tok, BAND, dil).astype(BF16)
                doa = _rows(sdo, tok, BAND, dil).astype(BF16)
                kw = kext_ref[pl.ds(base + a * BAND, 2 * BAND), :]
                vw = vext_ref[pl.ds(base + a * BAND, 2 * BAND), :]
                s = lax.dot_general(qa, kw, nt_dims, preferred_element_type=F32) * scale + bias_v
                p = jnp.where(_window_mask(band, jcol, a, c), jnp.exp(s - _rep2(_rows(slse, tok, BAND, dil))), 0.0)
                dp = lax.dot_general(doa, vw, nt_dims, preferred_element_type=F32)
                ds = p * (dp - _rep2(_rows(sdsum, tok, BAND, dil)))
                dsb_ref[...] += ds
                dsb16 = ds.astype(BF16)
                dqw = lax.dot_general(dsb16, kw, nn_dims, preferred_element_type=F32) * scale
                _set_rows(adq, tok, BAND, dil, _rows(adq, tok, BAND, dil) + dqw)
                dkext_ref[pl.ds(base + a * BAND, 2 * BAND), :] += lax.dot_general(dsb16, qa, tn_dims, preferred_element_type=F32) * scale
                dvext_ref[pl.ds(base + a * BAND, 2 * BAND), :] += lax.dot_general(p.astype(BF16), doa, tn_dims, preferred_element_type=F32)

            _set_rows(adk, r, lc, dil, _rows(adk, r, lc, dil) + dkext_ref[pl.ds(base + BAND, lc), :])
            _set_rows(adv, r, lc, dil, _rows(adv, r, lc, dil) + dvext_ref[pl.ds(base + BAND, lc), :])
            last = cte - BAND * dil + r
            _set_rows(ack, last, BAND, dil, _rows(ack, last, BAND, dil) + dkext_ref[pl.ds(base, BAND), :])
            _set_rows(acv, last, BAND, dil, _rows(acv, last, BAND, dil) + dvext_ref[pl.ds(base, BAND), :])

        full_src = (q_ref, k_ref, v_ref, do_ref, lse_ref, dsum_ref, kp_ref, vp_ref)
        full_acc = (dqa_ref, dka_ref, dva_ref, ck_ref, cv_ref)
        for bi, (win, dil) in enumerate(BRANCHES):
            bias_v = b_refs[bi][...]
            outer, inner = _two_level(dil)
            if outer == 1:
                for r in range(dil):
                    residue(full_src, full_acc, r % ATTN_GROUP, r, dil, ct, bias_v, dsb_refs[bi])
            else:
                cte = ct // outer

                def group(r1, carry, bias_v=bias_v, dsb_ref=dsb_refs[bi], outer=outer, inner=inner, cte=cte):
                    for t_ref, x_ref in zip(tmp_in, full_src):
                        t_ref[...] = _rows(x_ref, r1, cte, outer)
                    for t_ref in tmp_acc:
                        t_ref[...] = jnp.zeros_like(t_ref)
                    for r2 in range(inner):
                        residue(tmp_in, tmp_acc, r2 % ATTN_GROUP, r2, inner, cte, bias_v, dsb_ref)
                    for t_ref, a_ref in zip(tmp_acc, full_acc):
                        _set_rows(a_ref, r1, cte, outer, _rows(a_ref, r1, cte, outer) + t_ref[...])
                    return carry

                lax.fori_loop(0, outer, group, 0)

        dq_ref[...] = dqa_ref[...].astype(BF16)
        dk_ref[...] = dka_ref[...].astype(BF16)
        dv_ref[...] = dva_ref[...].astype(BF16)

    cur, prev, bias = _attn_specs(ct, n_heads, lambda s: n_chunks - 1 - s)
    small = (ct // ATTN_GROUP, HEAD_DIM)
    res = pl.pallas_call(
        body, name=name, grid=(n_heads, n_chunks),
        in_specs=[cur(0), cur(0), cur(n_heads), cur(0), cur(0), cur(0), prev(0), prev(n_heads)] + [bias] * nbr,
        out_specs=[cur(0)] * 3 + [bias] * nbr,
        out_shape=[jax.ShapeDtypeStruct((T, D), BF16)] * 3 + [jax.ShapeDtypeStruct((n_heads, BAND, 2 * BAND), F32)] * nbr,
        scratch_shapes=[pltpu.VMEM((_slot_rows(ct), HEAD_DIM), BF16)] * 2 + [pltpu.VMEM((_slot_rows(ct), HEAD_DIM), F32)] * 2
                       + [pltpu.VMEM((ct, HEAD_DIM), F32)] * 6 + [pltpu.VMEM(small, F32)] * (n_src + n_acc),
        compiler_params=_cparams(("arbitrary", "arbitrary")),
    )(q, kv, kv, do, o, lse, kv, kv, *biases)
    return res[0], res[1], res[2], list(res[3:])


def _divisor_tile(n, cap, mult):
    if n <= cap:
        return n
    t = cap - cap % mult
    while n % t:
        t -= mult
    return t


def _tile2(R, C):
    return _divisor_tile(R, 512, 8), _divisor_tile(C, 1024, 128)


def half_cast(name, dw, core):
    S, R, C = dw.shape
    hr = R // 2
    tr, tc = _tile2(hr, C)
    nrb = hr // tr

    def body(c_ref, x_ref, o_ref):
        o_ref[...] = x_ref[...].astype(BF16)

    return pl.pallas_call(
        body, name=name,
        grid_spec=pltpu.PrefetchScalarGridSpec(
            num_scalar_prefetch=1, grid=(S, nrb, C // tc),
            in_specs=[pl.BlockSpec((None, tr, tc), lambda s, i, j, c: (s, (1 - c[0]) * nrb + i, j))],
            out_specs=pl.BlockSpec((None, tr, tc), lambda s, i, j, c: (s, i, j))),
        out_shape=jax.ShapeDtypeStruct((S, hr, C), BF16),
        compiler_params=_cparams(("parallel", "parallel", "parallel")),
    )(core, dw)


def pair_sum(name, dw, recv, core):
    S, R, C = dw.shape
    hr = R // 2
    tr, tc = _tile2(hr, C)
    nrb = hr // tr

    def body(c_ref, x_ref, r_ref, p_ref, pbf_ref):
        p = x_ref[...] + r_ref[...].astype(F32)
        p_ref[...] = p
        pbf_ref[...] = p.astype(BF16)

    out = pl.BlockSpec((None, tr, tc), lambda s, i, j, c: (s, i, j))
    return pl.pallas_call(
        body, name=name,
        grid_spec=pltpu.PrefetchScalarGridSpec(
            num_scalar_prefetch=1, grid=(S, nrb, C // tc),
            in_specs=[pl.BlockSpec((None, tr, tc), lambda s, i, j, c: (s, c[0] * nrb + i, j)), out],
            out_specs=[out, out]),
        out_shape=[jax.ShapeDtypeStruct((S, hr, C), F32), jax.ShapeDtypeStruct((S, hr, C), BF16)],
        compiler_params=_cparams(("parallel", "parallel", "parallel")),
    )(core, dw, recv)


def chip_sum(name, p, recv, chip, core):
    S, hr, C = p.shape
    tr, tc = _tile2(hr, C)
    nrb = hr // tr

    def body(chip_ref, core_ref, p_ref, r_ref, o_ref):
        acc = p_ref[...]
        for t in range(N_CHIPS - 1):
            acc = acc + r_ref[t].astype(F32)
        o_ref[...] = acc

    return pl.pallas_call(
        body, name=name,
        grid_spec=pltpu.PrefetchScalarGridSpec(
            num_scalar_prefetch=2, grid=(nrb, C // tc),
            in_specs=[pl.BlockSpec((None, tr, tc), lambda i, j, s, c: (s[0], i, j)),
                      pl.BlockSpec((N_CHIPS - 1, tr, tc), lambda i, j, s, c: (0, i, j))],
            out_specs=pl.BlockSpec((tr, tc), lambda i, j, s, c: (c[0] * nrb + i, j))),
        out_shape=jax.ShapeDtypeStruct((2 * hr, C), F32),
        compiler_params=_cparams(("parallel", "parallel")),
    )(chip, core, p, recv)


def adamw(name, w, g, m, v):
    R, C = w.shape
    tr, tc = _tile2(R, C)
    c1 = 1.0 - ADAM_B1 ** ADAM_STEP
    c2 = 1.0 - ADAM_B2 ** ADAM_STEP

    def body(w_ref, g_ref, m_ref, v_ref, d_ref, nm_ref, nv_ref):
        gv = g_ref[...]
        nm = ADAM_B1 * m_ref[...] + (1.0 - ADAM_B1) * gv
        nv = ADAM_B2 * v_ref[...] + (1.0 - ADAM_B2) * (gv * gv)
        nm_ref[...] = nm
        nv_ref[...] = nv
        d_ref[...] = -ADAM_LR * ((nm / c1) / (jnp.sqrt(nv / c2) + ADAM_EPS) + ADAM_WD * w_ref[...])

    spec = pl.BlockSpec((tr, tc), lambda i, j: (i, j))
    return pl.pallas_call(
        body, name=name, grid=(R // tr, C // tc), in_specs=[spec] * 4, out_specs=[spec] * 3,
        out_shape=[jax.ShapeDtypeStruct((R, C), F32)] * 3,
        compiler_params=_cparams(("parallel", "parallel")),
    )(w, g, m, v)


def adamw_layers(name, w, g_layers, m, v):
    nl, R, C = w.shape
    tr, tc = _tile2(R, C)
    ni, nj = R // tr, C // tc
    c1 = 1.0 - ADAM_B1 ** ADAM_STEP
    c2 = 1.0 - ADAM_B2 ** ADAM_STEP

    def body(w_ref, *rest):
        g_refs = rest[:nl]
        m_ref, v_ref, g_ref, d_ref, nm_ref, nv_ref = rest[nl:]
        layer = pl.program_id(0)
        gv = g_refs[0][...]
        for l in range(1, nl):
            gv = jnp.where(layer == l, g_refs[l][...], gv)
        nm = ADAM_B1 * m_ref[...] + (1.0 - ADAM_B1) * gv
        nv = ADAM_B2 * v_ref[...] + (1.0 - ADAM_B2) * (gv * gv)
        g_ref[...] = gv
        nm_ref[...] = nm
        nv_ref[...] = nv
        d_ref[...] = -ADAM_LR * ((nm / c1) / (jnp.sqrt(nv / c2) + ADAM_EPS) + ADAM_WD * w_ref[...])

    def g_spec(l):
        def index(layer, i, j):
            return (jnp.where(layer == l, i, jnp.where(layer < l, 0, ni - 1)),
                    jnp.where(layer == l, j, jnp.where(layer < l, 0, nj - 1)))
        return pl.BlockSpec((tr, tc), index)

    spec = pl.BlockSpec((None, tr, tc), lambda layer, i, j: (layer, i, j))
    return pl.pallas_call(
        body, name=name, grid=(nl, ni, nj),
        in_specs=[spec] + [g_spec(l) for l in range(nl)] + [spec] * 2, out_specs=[spec] * 4,
        out_shape=[jax.ShapeDtypeStruct((nl, R, C), F32)] * 4,
        compiler_params=_cparams(("arbitrary", "arbitrary", "arbitrary")),
    )(w, *g_layers, m, v)


def sum_devices(name, gathered):
    n, R, C = gathered.shape

    def body(x_ref, o_ref):
        acc = x_ref[0]
        for d in range(1, n):
            acc = acc + x_ref[d]
        o_ref[...] = acc

    return pl.pallas_call(
        body, name=name, in_specs=[pl.BlockSpec(memory_space=pltpu.VMEM)],
        out_specs=pl.BlockSpec(memory_space=pltpu.VMEM),
        out_shape=jax.ShapeDtypeStruct((R, C), F32),
    )(gathered)


def _place():
    x, y, c = lax.axis_index("x"), lax.axis_index("y"), lax.axis_index("c")
    return x, y, c


def _other_chips(x, y):
    return [(1 - x, y), (x, 1 - y), (1 - x, 1 - y)]


def all_gather8(name, block):
    R, C = block.shape

    def body(x_ref, out_ref, send_sems, recv_sems, local_sem):
        x, y, c = _place()
        me, sibling = (x, y, c), (x, y, 1 - c)
        chips = _other_chips(x, y)

        def rows(px, py, pc):
            return out_ref.at[4 * px + 2 * py + pc]

        def copy(k, blk, to, src=None):
            return pltpu.make_async_remote_copy(
                src_ref=rows(*blk) if src is None else src, dst_ref=rows(*blk),
                send_sem=send_sems.at[k], recv_sem=recv_sems.at[k], device_id=to, device_id_type=MESH)

        mine = pltpu.make_async_copy(x_ref, rows(*me), local_sem)
        mine.start()
        first = [copy(0, me, sibling, src=x_ref)]
        first += [copy(1 + j, me, (*chip, c), src=x_ref) for j, chip in enumerate(chips)]
        for cp in first:
            cp.start()
        passed = [copy(4 + j, (*chip, c), sibling) for j, chip in enumerate(chips)]
        for j, chip in enumerate(chips):
            copy(1 + j, (*chip, c), me).wait_recv()
            passed[j].start()
        copy(0, sibling, me).wait_recv()
        for j, chip in enumerate(chips):
            copy(4 + j, (*chip, 1 - c), me).wait_recv()
        for cp in first + passed:
            cp.wait_send()
        mine.wait()

    return pl.pallas_call(
        body, name=name, out_shape=jax.ShapeDtypeStruct((N_DEV, R, C), block.dtype),
        in_specs=[pl.BlockSpec(memory_space=pltpu.VMEM)], out_specs=pl.BlockSpec(memory_space=pltpu.VMEM),
        scratch_shapes=[pltpu.SemaphoreType.DMA((7,)), pltpu.SemaphoreType.DMA((7,)), pltpu.SemaphoreType.DMA],
    )(block)


_HBM = pl.BlockSpec(memory_space=pltpu.HBM)
_SEM = pl.BlockSpec(memory_space=pltpu.SEMAPHORE)
_DATAFLOW = pltpu.SideEffectType.DATAFLOW_SIDE_EFFECTING


def _in_hbm(a):
    return pltpu.with_memory_space_constraint(a, pltpu.HBM)


def split_start(name, srcs, lands, n_sem, plan):
    ns, nl = len(srcs), len(lands)

    def body(*refs):
        src, land = refs[:ns], refs[ns:ns + nl]
        send_sems, recv_sems = refs[ns + nl], refs[ns + nl + 1]
        token = refs[-1]
        outgoing, _ = plan(src, land, send_sems, recv_sems)
        for cp in outgoing:
            cp.start()
        token[...] = jnp.zeros_like(token)

    bufs = list(srcs) + list(lands)
    res = pl.pallas_call(
        body, name=name,
        out_shape=(pltpu.SemaphoreType.DMA((n_sem,)), pltpu.SemaphoreType.DMA((n_sem,)),
                   *[pltpu.HBM(b.shape, b.dtype) for b in bufs], jax.ShapeDtypeStruct((8, 128), F32)),
        in_specs=[_HBM] * (ns + nl),
        out_specs=(_SEM, _SEM, *[_HBM] * (ns + nl), pl.BlockSpec(memory_space=pltpu.VMEM)),
        input_output_aliases={i: 2 + i for i in range(ns + nl)},
        compiler_params=pltpu.CompilerParams(has_side_effects=_DATAFLOW),
    )(*[_in_hbm(b) for b in bufs])
    return res[0], res[1], list(res[2:2 + ns]), list(res[2 + ns:2 + ns + nl]), res[-1]


def split_wait(name, started, after, plan):
    send_sems, recv_sems, srcs, lands, _ = started
    ns, nl = len(srcs), len(lands)

    def body(*refs):
        src, land = refs[:ns], refs[ns:ns + nl]
        send, recv = refs[ns + nl], refs[ns + nl + 1]
        outgoing, incoming = plan(src, land, send, recv)
        for cp in outgoing:
            cp.wait_send()
        for cp in incoming:
            cp.wait_recv()

    bufs = list(srcs) + list(lands)
    res = pl.pallas_call(
        body, name=name,
        out_shape=tuple(pltpu.HBM(b.shape, b.dtype) for b in bufs),
        in_specs=[_HBM] * (ns + nl) + [_SEM, _SEM, pl.BlockSpec(memory_space=pl.ANY)],
        out_specs=tuple([_HBM] * (ns + nl)),
        input_output_aliases={i: i for i in range(ns + nl)},
        compiler_params=pltpu.CompilerParams(has_side_effects=_DATAFLOW),
    )(*bufs, send_sems, recv_sems, after)
    return list(res[ns:])


def _rcopy(src, dst, send_sems, ks, recv_sems, kr, device):
    return pltpu.make_async_remote_copy(src_ref=src, dst_ref=dst, send_sem=send_sems.at[ks], recv_sem=recv_sems.at[kr],
                                        device_id=device, device_id_type=MESH)


def _half_rows(ref, h):
    hr = ref.shape[0] // 2
    return ref.at[pl.ds(h * hr, hr)]


def _gather_plan(src, land, send_sems, recv_sems):
    x, y, c = _place()
    me_chip = 2 * x + y
    chips = _other_chips(x, y)
    outgoing, incoming = [], []
    for w, buf in enumerate(land):
        mine = _half_rows(buf.at[me_chip], c)
        for t, chip in enumerate(chips):
            slot = 2 * chip[0] + chip[1]
            for cc in range(2):
                outgoing.append(_rcopy(mine, mine, send_sems, 6 * w + 2 * t + cc, recv_sems, 6 * w + 2 * t + c, (*chip, cc)))
                theirs = _half_rows(buf.at[slot], cc)
                incoming.append(_rcopy(theirs, theirs, send_sems, 6 * w + 2 * t + cc, recv_sems, 6 * w + 2 * t + cc, (*chip, cc)))
    return outgoing, incoming


def _swap_plan(src, land, send_sems, recv_sems):
    x, y, c = _place()
    cp = _rcopy(src[0], land[0], send_sems, 0, recv_sems, 0, (x, y, 1 - c))
    return [cp], [cp]


def _scatter_plan(src, land, send_sems, recv_sems):
    x, y, c = _place()
    cps = [_rcopy(src[0].at[2 * chip[0] + chip[1]], land[0].at[t], send_sems, t, recv_sems, t, (*chip, c))
           for t, chip in enumerate(_other_chips(x, y))]
    return cps, cps


def _share_plan(src, land, send_sems, recv_sems):
    x, y, c = _place()
    mine, theirs = _half_rows(land[0], c), _half_rows(land[0], 1 - c)
    return ([_rcopy(mine, mine, send_sems, 0, recv_sems, 0, (x, y, 1 - c))],
            [_rcopy(theirs, theirs, send_sems, 0, recv_sems, 0, (x, y, 1 - c))])


def place_shard(name, shard, chip, deps=(), layer=None):
    R, C = shard.shape[-2:]
    tr, tc = _tile2(R, C)

    def body(chip_ref, x_ref, *rest):
        rest[-1][...] = x_ref[...].astype(BF16)

    if layer is None:
        src = pl.BlockSpec((tr, tc), lambda i, j, s: (i, j))
    else:
        src = pl.BlockSpec((None, tr, tc), lambda i, j, s: (layer, i, j))
    return pl.pallas_call(
        body, name=name,
        grid_spec=pltpu.PrefetchScalarGridSpec(
            num_scalar_prefetch=1, grid=(R // tr, C // tc),
            in_specs=[src] + [pl.BlockSpec(d.shape, lambda i, j, s: (0, 0)) for d in deps],
            out_specs=pl.BlockSpec((None, tr, tc), lambda i, j, s: (s[0], i, j))),
        out_shape=jax.ShapeDtypeStruct((N_CHIPS, R, C), BF16),
        compiler_params=_cparams(("parallel", "parallel")),
    )(chip, shard, *deps)


class GradExchange:
    SCATTER_TICKS = 2

    def __init__(self, chip1, core, shard, mom, vel):
        self.chip1, self.core, self.shard, self.mom, self.vel = chip1, core, shard, mom, vel
        self.inflight, self.tokens, self.results = [], [], {}

    def take_deps(self):
        deps, self.tokens = self.tokens, []
        return deps

    def _start(self, name, srcs, lands, n_sem, plan):
        started = split_start(name, srcs, lands, n_sem, plan)
        self.tokens.append(started[-1])
        return started

    def add(self, n, dw):
        S, R, C = dw.shape
        to_sibling = half_cast("rs_cast_" + n, dw, self.core)
        started = self._start("rs_swap_start_" + n, [to_sibling], [lax.empty((S, R // 2, C), BF16)], 1, _swap_plan)
        self.inflight.append(dict(n=n, dw=dw, stage=0, started=started, ticks=0))

    def tick(self, after):
        for it in self.inflight:
            n = it["n"]
            if it["stage"] == 0:
                (recv,) = split_wait("rs_swap_wait_" + n, it["started"], after, _swap_plan)
                p, pbf = pair_sum("rs_pair_sum_" + n, it["dw"], recv, self.core)
                S, hr, C = pbf.shape
                it.update(stage=1, p=p, ticks=0,
                          started=self._start("rs_scatter_start_" + n, [pbf], [lax.empty((N_CHIPS - 1, hr, C), BF16)], 3, _scatter_plan))
            elif it["stage"] == 1:
                it["ticks"] += 1
                if it["ticks"] >= self.SCATTER_TICKS:
                    (recv,) = split_wait("rs_scatter_wait_" + n, it["started"], after, _scatter_plan)
                    half = chip_sum("rs_chip_sum_" + n, it["p"], recv, self.chip1, self.core)
                    it.update(stage=2, started=self._start("rs_share_start_" + n, [], [half], 1, _share_plan))
            elif it["stage"] == 2:
                (grad,) = split_wait("rs_share_wait_" + n, it["started"], after, _share_plan)
                if n in self.shard:
                    self.results[n] = (grad,) + tuple(adamw("adamw_" + n, self.shard[n], grad, self.mom[n], self.vel[n]))
                else:
                    self.results[n] = (grad,)
                it["stage"] = 3
        self.inflight = [it for it in self.inflight if it["stage"] < 3]

    def flush(self, after):
        while self.inflight:
            self.tick(after)


def _pack(arrs):
    parts = []
    for a in arrs:
        flat = a.reshape(-1).astype(F32)
        n = flat.shape[0]
        padded = -(-n // 1024) * 1024
        parts.append(jnp.pad(flat, (0, padded - n)).reshape(padded // 128, 128))
    return jnp.concatenate(parts, axis=0)


def _unpack(buf, shapes):
    out, row = [], 0
    for shp in shapes:
        n = int(np.prod(shp))
        rows = -(-n // 1024) * 8
        out.append(buf[row:row + rows].reshape(-1)[:n].reshape(shp))
        row += rows
    return out


def _bias_epi(acc, b):
    return (acc + b,)


def local_step(x, target, W, P, ex, first_deps=()):
    T, D = x.shape
    g = {}
    plain = lambda acc: (acc,)

    (h1,) = mm_nn("pw1_fwd", x, W("pw1", x), "col", _bias_epi, [F32],
                  extras=[(P["pw1_b"], "row")] + [(d, "dep") for d in first_deps])
    u, cpre, s = conv_fwd("conv_fwd", h1, P["dw_w"], P["dw_b"], P["cln_g"], P["cln_b"])
    (mix0,) = mm_nn("pw2_fwd", s, W("pw2", s), "row", _bias_epi, [F32], extras=[(P["pw2_b"], "row")])
    ln = [None] * 4
    gam = [P["ln_mix_g"][0:1], P["ln_mlp_g"][0:1], P["ln_mix_g"][1:2], P["ln_mlp_g"][1:2]]
    bet = [P["ln_mix_b"][0:1], P["ln_mlp_b"][0:1], P["ln_mix_b"][1:2], P["ln_mlp_b"][1:2]]
    ln[0] = ln_fwd("ln0_fwd", mix0, x)(gam[0], bet[0])

    def mlp_fwd(tag, i_ln, n1, n2):
        xhat, rstd, xbf = ln[i_ln]

        def up_epi(acc):
            r = jnp.maximum(acc, 0.0)
            return r * r, r

        hid, relu = mm_nn(tag + "_up", xbf, W(n1, xbf), "col", up_epi, [BF16, BF16])
        (mlp,) = mm_nn(tag + "_down", hid, W(n2, hid), "row", plain, [F32])
        ln[i_ln + 1] = ln_fwd(tag + "_ln", mlp, xhat, gam[i_ln], bet[i_ln])(gam[i_ln + 1], bet[i_ln + 1])
        return hid, relu

    hid0 = mlp_fwd("mlp0", 0, "w1_0", "w2_0")

    x2bf = ln[1][2]
    (kv,) = mm_nn("kv_fwd", x2bf, W("kv", x2bf), "col", plain, [F32])
    (q,) = mm_nn("q_fwd", x2bf, W("wq", kv), "row", plain, [F32])
    biases = [bias_expand("bias_d%d" % d, P["rel_bias"], d) for _, d in BRANCHES]
    assert all(win // d == BAND and min(ATTN_TOKENS, T) % (BAND * d) == 0 for win, d in BRANCHES)
    o, obf, lse = attn_fwd("attn_fwd", q, kv, biases)
    (attn,) = mm_nn("wo_fwd", obf, W("wo", obf), "row", plain, [F32])
    ln[2] = ln_fwd("ln2_fwd", attn, ln[1][0], gam[1], bet[1])(gam[2], bet[2])
    hid1 = mlp_fwd("mlp1", 2, "w1_1", "w2_1")

    dr3, dr3bf, g["ln_mlp_g1"], g["ln_mlp_b1"], _, loss_sum = ln_bwd(
        "ln3_bwd", ln[3][0], ln[3][1], gam[3], target=target, beta=bet[3])

    def dw_step(name, wname, a, cot, axis):
        dw = mm_tn(name, a, cot, W(wname, a).shape, axis, deps=ex.take_deps())
        ex.tick(dw)
        ex.add(wname, dw)

    def dx_step(name, cot, wname, axis, epilogue, out_dtype, extras):
        deps = [(d, "dep") for d in ex.take_deps()]
        (out,) = mm_nt(name, cot, W(wname, cot), axis, epilogue, [out_dtype], extras=list(extras) + deps)
        ex.tick(out)
        return out

    def mlp_bwd(tag, i_ln, n1, n2, hid_relu, dr, drbf):
        xbf = ln[i_ln][2]
        hid, relu = hid_relu
        dw_step(tag + "_dw2", n2, hid, drbf, "row")
        dp = dx_step(tag + "_dhid", drbf, n2, "row", lambda acc, r: (acc * (2.0 * r.astype(F32)),), BF16, [(relu, "tile")])
        dw_step(tag + "_dw1", n1, xbf, dp, "col")
        return dx_step(tag + "_dx", dp, n1, "col", lambda acc, e: (acc + ALPHA * e,), F32, [(dr, "tile")])

    dx3 = mlp_bwd("mlp1", 2, "w1_1", "w2_1", hid1, dr3, dr3bf)
    dr2, dr2bf, g["ln_mix_g1"], g["ln_mix_b1"], _ = ln_bwd("ln2_bwd", ln[2][0], ln[2][1], gam[2], dy=dx3)
    dw_step("wo_dw", "wo", obf, dr2bf, "row")
    do = dx_step("wo_dx", dr2bf, "wo", "row", plain, F32, [])
    dq, dk, dv, dsbs = attn_bwd("attn_bwd", q, kv, do, o, lse, biases)
    g["rel_bias"] = relbias_grad("relbias_grad", dsbs)[:, 0, :REL_BUCKETS].T
    dkv = jnp.concatenate([dk, dv], axis=1)
    dw_step("wq_dw", "wq", x2bf, dq, "row")
    dw_step("kv_dw", "kv", x2bf, dkv, "col")
    dx2a = dx_step("wq_dx", dq, "wq", "row", lambda acc, e: (acc + ALPHA * e,), F32, [(dr2, "tile")])
    dx2 = dx_step("kv_dx", dkv, "kv", "col", lambda acc, e: (acc + e,), F32, [(dx2a, "tile")])

    dr1, dr1bf, g["ln_mlp_g0"], g["ln_mlp_b0"], _ = ln_bwd("ln1_bwd", ln[1][0], ln[1][1], gam[1], dy=dx2)
    dx1 = mlp_bwd("mlp0", 0, "w1_0", "w2_0", hid0, dr1, dr1bf)
    dr0, dr0bf, g["ln_mix_g0"], g["ln_mix_b0"], g["pw2_b"] = ln_bwd("ln0_bwd", ln[0][0], ln[0][1], gam[0], dy=dx1)

    dw_step("pw2_dw", "pw2", s, dr0bf, "row")
    ds = dx_step("pw2_dx", dr0bf, "pw2", "row", plain, F32, [])
    dc, g["cln_g"], g["cln_b"], g["dw_b"] = conv_bwd_ln("conv_bwd_ln", ds, cpre, P["cln_g"], P["cln_b"])
    dh1, g["pw1_b"], g["dw_w"] = conv_bwd_taps("conv_bwd_taps", dc, u, h1, P["dw_w"])
    dw_step("pw1_dw", "pw1", x, dh1, "col")
    dx = dx_step("pw1_dx", dh1, "pw1", "col", lambda acc, e: (acc + ALPHA * e,), F32, [(dr0, "tile")])
    return loss_sum, dx, g


BIG = ("pw1", "pw2", "w1_0", "w2_0", "kv", "wq", "wo", "w1_1", "w2_1")


def kernel(x, conv_pw1_w, conv_pw1_b, conv_dw_w, conv_dw_b, conv_ln_g, conv_ln_b, conv_pw2_w, conv_pw2_b, w_kv, attn_wq, attn_wo, rel_bias, mlp_w1, mlp_w2, ln_mix_g, ln_mix_b, ln_mlp_g, ln_mlp_b, loss_target, m_conv_pw1_w, m_conv_pw1_b, m_conv_dw_w, m_conv_dw_b, m_conv_ln_g, m_conv_ln_b, m_conv_pw2_w, m_conv_pw2_b, m_w_kv, m_attn_wq, m_attn_wo, m_rel_bias, m_mlp_w1, m_mlp_w2, m_ln_mix_g, m_ln_mix_b, m_ln_mlp_g, m_ln_mlp_b, v_conv_pw1_w, v_conv_pw1_b, v_conv_dw_w, v_conv_dw_b, v_conv_ln_g, v_conv_ln_b, v_conv_pw2_w, v_conv_pw2_b, v_w_kv, v_attn_wq, v_attn_wo, v_rel_bias, v_mlp_w1, v_mlp_w2, v_ln_mix_g, v_ln_mix_b, v_ln_mlp_g, v_ln_mlp_b):
    _, T, D = x.shape
    xi, yi, ci = _place()
    chip = 2 * xi + yi
    core = jnp.reshape(ci, (1,)).astype(jnp.int32)
    chip1 = jnp.reshape(chip, (1,)).astype(jnp.int32)

    def two_d(a):
        return a.reshape(a.shape[-2:])

    shard = {"pw1": two_d(conv_pw1_w), "pw2": two_d(conv_pw2_w), "kv": w_kv, "wq": two_d(attn_wq), "wo": two_d(attn_wo)}
    mom = {"pw1": two_d(m_conv_pw1_w), "pw2": two_d(m_conv_pw2_w), "kv": m_w_kv, "wq": two_d(m_attn_wq), "wo": two_d(m_attn_wo)}
    vel = {"pw1": two_d(v_conv_pw1_w), "pw2": two_d(v_conv_pw2_w), "kv": v_w_kv, "wq": two_d(v_attn_wq), "wo": two_d(v_attn_wo)}
    stacked = {"w1_0": (mlp_w1, 0), "w1_1": (mlp_w1, 1), "w2_0": (mlp_w2, 0), "w2_1": (mlp_w2, 1)}

    started = {}
    for n in BIG:
        deps = [started[prev][-1] for prev in list(started)[-1:]]
        src, layer = stacked.get(n, (shard.get(n), None))
        started[n] = split_start("gather_start_" + n, [], [place_shard("place_" + n, src, chip1, deps, layer)], 6, _gather_plan)
    gathered = {}

    def W(n, after):
        if n not in gathered:
            (gathered[n],) = split_wait("gather_wait_" + n, started[n], after, _gather_plan)
        return gathered[n]

    sharded_small = [conv_pw1_b, conv_dw_w[0], conv_dw_b, conv_ln_g, conv_ln_b, conv_pw2_b]
    sh_shapes = [a.shape for a in sharded_small]
    small_all = all_gather8("gather_small", _pack(sharded_small))
    per_chip = [_unpack(small_all[2 * j], sh_shapes) for j in range(N_CHIPS)]
    full = [jnp.concatenate([per_chip[j][i] for j in range(N_CHIPS)], axis=-1) for i in range(len(sharded_small))]
    P = dict(pw1_b=full[0], dw_w=full[1], dw_b=full[2], cln_g=full[3], cln_b=full[4], pw2_b=full[5],
             rel_bias=rel_bias, ln_mix_g=ln_mix_g, ln_mix_b=ln_mix_b, ln_mlp_g=ln_mlp_g, ln_mlp_b=ln_mlp_b)

    ex = GradExchange(chip1, core, shard, mom, vel)
    loss_sum, dx, g = local_step(x.reshape(T, D), loss_target.reshape(T, D), W, P, ex,
                                 first_deps=[started[n][-1] for n in BIG])
    loss = (0.5 / D) * lax.psum(loss_sum[0, 0], ("x", "y", "c"))

    small_names = ["pw1_b", "dw_w", "dw_b", "cln_g", "cln_b", "pw2_b", "rel_bias",
                   "ln_mix_g0", "ln_mix_g1", "ln_mix_b0", "ln_mix_b1", "ln_mlp_g0", "ln_mlp_g1", "ln_mlp_b0", "ln_mlp_b1"]
    small_grads = [g[n] for n in small_names]
    sg_shapes = [a.shape for a in small_grads]
    summed = sum_devices("small_grad_sum", all_gather8("gather_small_grads", _pack(small_grads)))
    sg = dict(zip(small_names, _unpack(summed, sg_shapes)))

    def my_cols(a, width):
        return lax.dynamic_slice_in_dim(a, chip * width, width, axis=a.ndim - 1)

    small_g = [my_cols(sg["pw1_b"], conv_pw1_b.shape[-1]),
               my_cols(sg["dw_w"], conv_dw_w.shape[-1])[None],
               my_cols(sg["dw_b"], conv_dw_b.shape[-1]), my_cols(sg["cln_g"], conv_ln_g.shape[-1]),
               my_cols(sg["cln_b"], conv_ln_b.shape[-1]), my_cols(sg["pw2_b"], conv_pw2_b.shape[-1]),
               sg["rel_bias"],
               jnp.concatenate([sg["ln_mix_g0"], sg["ln_mix_g1"]], axis=0),
               jnp.concatenate([sg["ln_mix_b0"], sg["ln_mix_b1"]], axis=0),
               jnp.concatenate([sg["ln_mlp_g0"], sg["ln_mlp_g1"]], axis=0),
               jnp.concatenate([sg["ln_mlp_b0"], sg["ln_mlp_b1"]], axis=0)]
    small_w = [conv_pw1_b, conv_dw_w, conv_dw_b, conv_ln_g, conv_ln_b, conv_pw2_b, rel_bias, ln_mix_g, ln_mix_b, ln_mlp_g, ln_mlp_b]
    small_m = [m_conv_pw1_b, m_conv_dw_w, m_conv_dw_b, m_conv_ln_g, m_conv_ln_b, m_conv_pw2_b, m_rel_bias, m_ln_mix_g, m_ln_mix_b, m_ln_mlp_g, m_ln_mlp_b]
    small_v = [v_conv_pw1_b, v_conv_dw_w, v_conv_dw_b, v_conv_ln_g, v_conv_ln_b, v_conv_pw2_b, v_rel_bias, v_ln_mix_g, v_ln_mix_b, v_ln_mlp_g, v_ln_mlp_b]
    sw_shapes = [a.shape for a in small_w]
    small_g = [a.reshape(s) for a, s in zip(small_g, sw_shapes)]
    upd_small = adamw("adamw_small", _pack(small_w), _pack(small_g), _pack(small_m), _pack(small_v))
    sd, snm, snv = (_unpack(b, sw_shapes) for b in upd_small)

    ex.flush(upd_small[0])
    res_w1 = adamw_layers("adamw_w1", mlp_w1, [ex.results["w1_0"][0], ex.results["w1_1"][0]], m_mlp_w1, v_mlp_w1)
    res_w2 = adamw_layers("adamw_w2", mlp_w2, [ex.results["w2_0"][0], ex.results["w2_1"][0]], m_mlp_w2, v_mlp_w2)

    def big_out(k):
        one = {n: ex.results[n][k] for n in shard}
        return dict(pw1=one["pw1"][None], pw2=one["pw2"][None], kv=one["kv"], wq=one["wq"][None], wo=one["wo"][None],
                    w1=res_w1[k], w2=res_w2[k])

    def ordered(big, small):
        return [big["pw1"], small[0], small[1], small[2], small[3], small[4], big["pw2"], small[5], big["kv"], big["wq"],
                big["wo"], small[6], big["w1"], big["w2"], small[7], small[8], small[9], small[10]]

    grads = ordered(big_out(0), small_g)
    deltas = ordered(big_out(1), sd)
    new_m = ordered(big_out(2), snm)
    new_v = ordered(big_out(3), snv)
    return (loss, dx.reshape(1, T, D), *grads, *deltas, *new_m, *new_v)
```

```python
import functools
import math

import numpy as np
import jax
import jax.numpy as jnp
from jax import lax
from jax.experimental import pallas as pl
from jax.experimental.pallas import tpu as pltpu

F32 = jnp.float32
BF16 = jnp.bfloat16

HEAD_DIM = 128
BAND = 128
BRANCHES = ((128, 1), (512, 4), (2048, 16))
CONV_WIDTH = 31
CONV_HALO = 32
REL_BUCKETS = 32
REL_MAX_DIST = 2048
DEPTH = 2
ALPHA = (2 * DEPTH) ** 0.25
LN_EPS = 1e-5
ADAM_LR, ADAM_B1, ADAM_B2, ADAM_EPS, ADAM_WD, ADAM_STEP = 0.001, 0.9, 0.999, 1e-08, 0.01, 10

N_CHIPS = 4
N_DEV = 8
MESH = pl.DeviceIdType.MESH
VMEM_LIMIT_BYTES = 56 * 1024 * 1024
MM_TM, MM_TN, MM_TK = 1024, 1024, 2048
ROW_TILE = 256
CONV_TILE = 128
NEG_BIG = -1e30


def _cparams(sem):
    return pltpu.CompilerParams(dimension_semantics=sem, vmem_limit_bytes=VMEM_LIMIT_BYTES)


def _sigmoid(x):
    return 1.0 / (1.0 + jnp.exp(-x))


def _wspec(wshape, axis, br, bc, rsel, csel):
    _, R, C = wshape
    if axis == "col":
        if bc > C:
            assert bc % C == 0, (wshape, bc)
            return pl.BlockSpec((bc // C, br, C), lambda *g: (csel(*g), rsel(*g), 0))
        nb = C // bc
        assert nb * bc == C, (wshape, bc)
        return pl.BlockSpec((None, br, bc), lambda *g: (csel(*g) // nb, rsel(*g), csel(*g) % nb))
    if br > R:
        assert br % R == 0, (wshape, br)
        return pl.BlockSpec((br // R, R, bc), lambda *g: (rsel(*g), 0, csel(*g)))
    nb = R // br
    assert nb * br == R, (wshape, br)
    return pl.BlockSpec((None, br, bc), lambda *g: (rsel(*g) // nb, rsel(*g) % nb, csel(*g)))


def _join_shards(b, axis):
    if b.ndim == 2:
        return b
    if axis == "row":
        return b.reshape(b.shape[0] * b.shape[1], b.shape[2])
    return jnp.concatenate([b[s] for s in range(b.shape[0])], axis=1)


def _split_shards(r, shape, axis):
    if len(shape) == 2:
        return r
    if axis == "row":
        return r.reshape(shape)
    return jnp.stack([r[:, s * shape[2]:(s + 1) * shape[2]] for s in range(shape[0])])


def _full_dims(wshape, axis):
    _, R, C = wshape
    return (R, N_CHIPS * C) if axis == "col" else (N_CHIPS * R, C)


def _mm_body(nk, kinds, n_out, dims, epilogue, axis):
    n_extra = len(kinds)

    def body(*refs):
        a_ref, b_ref = refs[0], refs[1]
        extra = [r for r, kind in zip(refs[2:2 + n_extra], kinds) if kind != "dep"]
        outs = refs[2 + n_extra:2 + n_extra + n_out]
        part = lax.dot_general(a_ref[...].astype(BF16), _join_shards(b_ref[...], axis).astype(BF16), (dims, ((), ())),
                               preferred_element_type=F32)

        def write(res):
            for r, o in zip(res, outs):
                o[...] = _split_shards(r, o.shape, axis).astype(o.dtype)

        if nk == 1:
            write(epilogue(part, *[e[...] for e in extra]))
            return
        acc_ref = refs[2 + n_extra + n_out]
        k = pl.program_id(2)

        @pl.when(k == 0)
        def _():
            acc_ref[...] = part

        @pl.when(k > 0)
        def _():
            acc_ref[...] += part

        @pl.when(k == nk - 1)
        def _():
            write(epilogue(acc_ref[...], *[e[...] for e in extra]))
    return body


def _long_tk(a, k_dim):
    tk = min(MM_TK, k_dim)
    if a.dtype == BF16 and k_dim >= 4 * MM_TK:
        tk = 2 * MM_TK
    return tk


def _extra_specs(extras, tm, tn):
    specs = []
    for arr, kind in extras:
        if kind == "tile":
            specs.append(pl.BlockSpec((tm, tn), lambda i, j, k: (i, j)))
        elif kind == "dep":
            specs.append(pl.BlockSpec(arr.shape, lambda i, j, k: (0, 0)))
        else:
            specs.append(pl.BlockSpec((1, tn), lambda i, j, k: (0, j)))
    return specs


def mm_nn(name, a, w, axis, epilogue, out_dtypes, extras=()):
    M, K = a.shape
    Kw, N = _full_dims(w.shape, axis)
    assert K == Kw
    tm, tn, tk = min(MM_TM, M), min(MM_TN, N), _long_tk(a, K)
    nk = K // tk
    in_specs = [pl.BlockSpec((tm, tk), lambda i, j, k: (i, k)),
                _wspec(w.shape, axis, tk, tn, lambda i, j, k: k, lambda i, j, k: j)]
    in_specs += _extra_specs(extras, tm, tn)
    body = _mm_body(nk, [kind for _, kind in extras], len(out_dtypes), ((1,), (0,)), epilogue, axis)
    return pl.pallas_call(
        body, name=name, grid=(M // tm, N // tn, nk), in_specs=in_specs,
        out_specs=[pl.BlockSpec((tm, tn), lambda i, j, k: (i, j)) for _ in out_dtypes],
        out_shape=[jax.ShapeDtypeStruct((M, N), d) for d in out_dtypes],
        scratch_shapes=[pltpu.VMEM((tm, tn), F32)] if nk > 1 else [],
        compiler_params=_cparams(("parallel", "parallel", "arbitrary")),
    )(a, w, *[e for e, _ in extras])


def mm_nt(name, g, w, axis, epilogue, out_dtypes, extras=()):
    M, N = g.shape
    K, Nw = _full_dims(w.shape, axis)
    assert N == Nw
    tm, tn, tk = min(MM_TM, M), min(MM_TN, K), min(MM_TK, N)
    nk = N // tk
    in_specs = [pl.BlockSpec((tm, tk), lambda i, j, k: (i, k)),
                _wspec(w.shape, axis, tn, tk, lambda i, j, k: j, lambda i, j, k: k)]
    in_specs += _extra_specs(extras, tm, tn)
    body = _mm_body(nk, [kind for _, kind in extras], len(out_dtypes), ((1,), (1,)), epilogue, axis)
    return pl.pallas_call(
        body, name=name, grid=(M // tm, K // tn, nk), in_specs=in_specs,
        out_specs=[pl.BlockSpec((tm, tn), lambda i, j, k: (i, j)) for _ in out_dtypes],
        out_shape=[jax.ShapeDtypeStruct((M, K), d) for d in out_dtypes],
        scratch_shapes=[pltpu.VMEM((tm, tn), F32)] if nk > 1 else [],
        compiler_params=_cparams(("parallel", "parallel", "arbitrary")),
    )(g, w, *[e for e, _ in extras])


def mm_tn(name, a, g, wshape, axis, deps=()):
    M, K = a.shape
    Mg, N = g.shape
    assert M == Mg and (K, N) == _full_dims(wshape, axis)
    tm, tn, tk = min(MM_TM, K), min(MM_TN, N), _long_tk(a, M)
    nk = M // tk
    body = _mm_body(nk, ["dep"] * len(deps), 1, ((0,), (0,)), lambda acc: (acc,), axis)
    return pl.pallas_call(
        body, name=name, grid=(K // tm, N // tn, nk),
        in_specs=[pl.BlockSpec((tk, tm), lambda i, j, k: (k, i)),
                  pl.BlockSpec((tk, tn), lambda i, j, k: (k, j))] + _extra_specs([(d, "dep") for d in deps], tm, tn),
        out_specs=[_wspec(wshape, axis, tm, tn, lambda i, j, k: i, lambda i, j, k: j)],
        out_shape=[jax.ShapeDtypeStruct(wshape, F32)],
        scratch_shapes=[pltpu.VMEM((tm, tn), F32)] if nk > 1 else [],
        compiler_params=_cparams(("parallel", "parallel", "arbitrary")),
    )(a, g, *deps)[0]


def _row_spec(tr, width):
    return pl.BlockSpec((tr, width), lambda i: (i, 0))


def _vec_spec(width):
    return pl.BlockSpec((1, width), lambda i: (0, 0))


def _fold8(x):
    r, d = x.shape
    return jnp.sum(x.reshape(r // 8, 8, d), axis=0)


def ln_fwd(name, f, prev, prev_g=None, prev_b=None):
    T, D = f.shape
    tr = min(ROW_TILE, T)
    affine = prev_g is not None

    def body(*refs):
        if affine:
            f_ref, p_ref, pg_ref, pb_ref, g_ref, b_ref, xhat_ref, rstd_ref, xbf_ref = refs
            xprev = p_ref[...] * pg_ref[...] + pb_ref[...]
        else:
            f_ref, p_ref, g_ref, b_ref, xhat_ref, rstd_ref, xbf_ref = refs
            xprev = p_ref[...]
        r = ALPHA * xprev + f_ref[...]
        mu = jnp.mean(r, axis=-1, keepdims=True)
        cen = r - mu
        var = jnp.mean(cen * cen, axis=-1, keepdims=True)
        rstd = lax.rsqrt(var + LN_EPS)
        xhat = cen * rstd
        xhat_ref[...] = xhat
        rstd_ref[...] = rstd
        xbf_ref[...] = (xhat * g_ref[...] + b_ref[...]).astype(BF16)

    def call(g, b):
        ins = [f, prev] + ([prev_g, prev_b] if affine else []) + [g, b]
        specs = [_row_spec(tr, D), _row_spec(tr, D)] + ([_vec_spec(D)] * 2 if affine else []) + [_vec_spec(D)] * 2
        return pl.pallas_call(
            body, name=name, grid=(T // tr,), in_specs=specs,
            out_specs=[_row_spec(tr, D), _row_spec(tr, 1), _row_spec(tr, D)],
            out_shape=[jax.ShapeDtypeStruct((T, D), F32), jax.ShapeDtypeStruct((T, 1), F32),
                       jax.ShapeDtypeStruct((T, D), BF16)],
            compiler_params=_cparams(("parallel",)),
        )(*ins)
    return call


def ln_bwd(name, xhat, rstd, gamma, dy=None, target=None, beta=None):
    T, D = xhat.shape
    tr = min(ROW_TILE, T)
    nt = T // tr
    head = target is not None

    def body(*refs):
        if head:
            xhat_ref, rstd_ref, g_ref, tgt_ref, b_ref = refs[:5]
            outs = refs[5:]
        else:
            xhat_ref, rstd_ref, g_ref, dy_ref = refs[:4]
            outs = refs[4:]
        dr_ref, drbf_ref, dg_ref, db_ref, cs_ref = outs[:5]
        rest = outs[5:]
        if head:
            loss_ref, acc_ref = rest
        else:
            (acc_ref,) = rest
        i = pl.program_id(0)
        xhat_v = xhat_ref[...]
        gam = g_ref[...]
        if head:
            diff = xhat_v * gam + b_ref[...] - tgt_ref[...]
            dyv = diff * (1.0 / D)
        else:
            dyv = dy_ref[...]
        dxh = dyv * gam
        m1 = jnp.mean(dxh, axis=-1, keepdims=True)
        m2 = jnp.mean(dxh * xhat_v, axis=-1, keepdims=True)
        dr = rstd_ref[...] * (dxh - m1 - xhat_v * m2)
        dr_ref[...] = dr
        drbf_ref[...] = dr.astype(BF16)

        @pl.when(i == 0)
        def _():
            acc_ref[...] = jnp.zeros_like(acc_ref)

        acc_ref[0] += _fold8(dyv * xhat_v)
        acc_ref[1] += _fold8(dyv)
        acc_ref[2] += _fold8(dr)
        if head:
            acc_ref[3] += _fold8(diff * diff)

        @pl.when(i == nt - 1)
        def _():
            dg_ref[...] = jnp.sum(acc_ref[0], axis=0, keepdims=True)
            db_ref[...] = jnp.sum(acc_ref[1], axis=0, keepdims=True)
            cs_ref[...] = jnp.sum(acc_ref[2], axis=0, keepdims=True)
            if head:
                loss_ref[...] = jnp.sum(jnp.sum(acc_ref[3], axis=0, keepdims=True), axis=1, keepdims=True)

    ins = [xhat, rstd, gamma] + ([target, beta] if head else [dy])
    specs = [_row_spec(tr, D), _row_spec(tr, 1), _vec_spec(D)] + ([_row_spec(tr, D), _vec_spec(D)] if head else [_row_spec(tr, D)])
    out_specs = [_row_spec(tr, D), _row_spec(tr, D), _vec_spec(D), _vec_spec(D), _vec_spec(D)]
    out_shape = [jax.ShapeDtypeStruct((T, D), F32), jax.ShapeDtypeStruct((T, D), BF16)] + [jax.ShapeDtypeStruct((1, D), F32)] * 3
    if head:
        out_specs.append(pl.BlockSpec((1, 1), lambda i: (0, 0)))
        out_shape.append(jax.ShapeDtypeStruct((1, 1), F32))
    return pl.pallas_call(
        body, name=name, grid=(nt,), in_specs=specs, out_specs=out_specs, out_shape=out_shape,
        scratch_shapes=[pltpu.VMEM((4, 8, D), F32)],
        compiler_params=_cparams(("arbitrary",)),
    )(*ins)


CONV_ROWS, CONV_COLS = 64, 512
CONV_COLS_BWD = 256


def _tap_chunks(tt, D, cols=CONV_COLS):
    for r0 in range(0, tt, min(CONV_ROWS, tt)):
        for c0 in range(0, D, min(cols, D)):
            yield r0, min(CONV_ROWS, tt), c0, min(cols, D)


SUBLANES = 8


def _shifted_copies(ext_ref, sh_ref):
    n = sh_ref.shape[1]
    zero = jnp.minimum(pl.program_id(0), 0)
    for b in range(1, SUBLANES):
        sh_ref[zero + (b - 1)] = ext_ref[pl.ds(b, n), :]


def _rows_at(ext_ref, sh_ref, off, nr, cols):
    a, b = divmod(off, SUBLANES)
    if b == 0:
        return ext_ref[pl.ds(off, nr), cols]
    return sh_ref[b - 1, pl.ds(a * SUBLANES, nr), cols]


def conv_fwd(name, h1, dw, dwb, lng, lnb):
    T, D2 = h1.shape
    D = D2 // 2
    tt = min(CONV_TILE, T)
    hb = tt // CONV_HALO
    KW = dw.shape[0]
    lead = CONV_HALO - (KW - 1)

    def body(a_ref, g_ref, ah_ref, gh_ref, dw_ref, dwb_ref, lng_ref, lnb_ref, u_ref, c_ref, s_ref, ext_ref, sh_ref):
        i = pl.program_id(0)
        u = a_ref[...] * _sigmoid(g_ref[...])
        u_ref[...] = u
        uh = ah_ref[...] * _sigmoid(gh_ref[...])
        ext_ref[pl.ds(0, CONV_HALO), :] = jnp.where(i > 0, uh, 0.0)
        ext_ref[pl.ds(CONV_HALO, tt), :] = u
        _shifted_copies(ext_ref, sh_ref)
        for r0, nr, c0, nc in _tap_chunks(tt, D):
            cols = pl.ds(c0, nc)
            acc = jnp.zeros((nr, nc), F32) + dwb_ref[:, cols]
            for k in range(KW):
                acc = acc + dw_ref[pl.ds(k, 1), cols] * _rows_at(ext_ref, sh_ref, r0 + lead + k, nr, cols)
            c_ref[pl.ds(r0, nr), cols] = acc
        c = c_ref[...]
        mu = jnp.mean(c, axis=-1, keepdims=True)
        cen = c - mu
        var = jnp.mean(cen * cen, axis=-1, keepdims=True)
        n = cen * lax.rsqrt(var + LN_EPS) * lng_ref[...] + lnb_ref[...]
        s_ref[...] = (n * _sigmoid(n)).astype(BF16)

    halo = lambda col: pl.BlockSpec((CONV_HALO, D), lambda i: (jnp.maximum(i * hb - 1, 0), col))
    return pl.pallas_call(
        body, name=name, grid=(T // tt,),
        in_specs=[pl.BlockSpec((tt, D), lambda i: (i, 0)), pl.BlockSpec((tt, D), lambda i: (i, 1)), halo(0), halo(1),
                  pl.BlockSpec((KW, D), lambda i: (0, 0)), _vec_spec(D), _vec_spec(D), _vec_spec(D)],
        out_specs=[_row_spec(tt, D)] * 3,
        out_shape=[jax.ShapeDtypeStruct((T, D), F32), jax.ShapeDtypeStruct((T, D), F32), jax.ShapeDtypeStruct((T, D), BF16)],
        scratch_shapes=[pltpu.VMEM((tt + CONV_HALO, D), F32),
                        pltpu.VMEM((SUBLANES - 1, tt + CONV_HALO - SUBLANES, D), F32)],
        compiler_params=_cparams(("parallel",)),
    )(h1, h1, h1, h1, dw, dwb, lng, lnb)


def conv_bwd_ln(name, ds, c, lng, lnb):
    T, D = c.shape
    tr = min(ROW_TILE, T)
    nt = T // tr

    def body(ds_ref, c_ref, g_ref, b_ref, dc_ref, dg_ref, db_ref, cs_ref, acc_ref):
        i = pl.program_id(0)
        cv = c_ref[...]
        mu = jnp.mean(cv, axis=-1, keepdims=True)
        cen = cv - mu
        var = jnp.mean(cen * cen, axis=-1, keepdims=True)
        rstd = lax.rsqrt(var + LN_EPS)
        chat = cen * rstd
        n = chat * g_ref[...] + b_ref[...]
        sg = _sigmoid(n)
        dn = ds_ref[...] * (sg * (1.0 + n * (1.0 - sg)))
        dxh = dn * g_ref[...]
        m1 = jnp.mean(dxh, axis=-1, keepdims=True)
        m2 = jnp.mean(dxh * chat, axis=-1, keepdims=True)
        dc = rstd * (dxh - m1 - chat * m2)
        dc_ref[...] = dc

        @pl.when(i == 0)
        def _():
            acc_ref[...] = jnp.zeros_like(acc_ref)

        acc_ref[0] += _fold8(dn * chat)
        acc_ref[1] += _fold8(dn)
        acc_ref[2] += _fold8(dc)

        @pl.when(i == nt - 1)
        def _():
            dg_ref[...] = jnp.sum(acc_ref[0], axis=0, keepdims=True)
            db_ref[...] = jnp.sum(acc_ref[1], axis=0, keepdims=True)
            cs_ref[...] = jnp.sum(acc_ref[2], axis=0, keepdims=True)

    return pl.pallas_call(
        body, name=name, grid=(nt,),
        in_specs=[_row_spec(tr, D), _row_spec(tr, D), _vec_spec(D), _vec_spec(D)],
        out_specs=[_row_spec(tr, D), _vec_spec(D), _vec_spec(D), _vec_spec(D)],
        out_shape=[jax.ShapeDtypeStruct((T, D), F32)] + [jax.ShapeDtypeStruct((1, D), F32)] * 3,
        scratch_shapes=[pltpu.VMEM((3, 8, D), F32)],
        compiler_params=_cparams(("arbitrary",)),
    )(ds, c, lng, lnb)


def conv_bwd_taps(name, dc, u, h1, dw):
    T, D = dc.shape
    tt = min(CONV_TILE, T)
    nt = T // tt
    hb = tt // CONV_HALO
    nhb = T // CONV_HALO
    KW = dw.shape[0]
    lead = CONV_HALO - (KW - 1)

    def body(dc_ref, dcn_ref, u_ref, uh_ref, a_ref, g_ref, dw_ref, dh1_ref, db1_ref, ddw_ref,
             edc_ref, eu_ref, du_ref, accw_ref, accb_ref, shdc_ref, shu_ref):
        i = pl.program_id(0)

        @pl.when(i == 0)
        def _():
            accw_ref[...] = jnp.zeros_like(accw_ref)
            accb_ref[...] = jnp.zeros_like(accb_ref)

        edc_ref[pl.ds(0, tt), :] = dc_ref[...]
        edc_ref[pl.ds(tt, CONV_HALO), :] = jnp.where(i < nt - 1, dcn_ref[...], 0.0)
        eu_ref[pl.ds(0, CONV_HALO), :] = jnp.where(i > 0, uh_ref[...], 0.0)
        eu_ref[pl.ds(CONV_HALO, tt), :] = u_ref[...]
        _shifted_copies(edc_ref, shdc_ref)
        _shifted_copies(eu_ref, shu_ref)
        for r0, nr, c0, nc in _tap_chunks(tt, D, CONV_COLS_BWD):
            cols = pl.ds(c0, nc)
            dcv = dc_ref[pl.ds(r0, nr), cols]
            acc = jnp.zeros((nr, nc), F32)
            for k in range(KW):
                acc = acc + dw_ref[pl.ds(k, 1), cols] * _rows_at(edc_ref, shdc_ref, r0 + (KW - 1) - k, nr, cols)
                accw_ref[k, :, cols] += _fold8(dcv * _rows_at(eu_ref, shu_ref, r0 + lead + k, nr, cols))
            du_ref[pl.ds(r0, nr), cols] = acc
        du = du_ref[...]
        sg = _sigmoid(g_ref[...])
        da = du * sg
        dg = du * a_ref[...] * sg * (1.0 - sg)
        dh1_ref[:, pl.ds(0, D)] = da.astype(BF16)
        dh1_ref[:, pl.ds(D, D)] = dg.astype(BF16)
        accb_ref[:, pl.ds(0, D)] += _fold8(da)
        accb_ref[:, pl.ds(D, D)] += _fold8(dg)

        @pl.when(i == nt - 1)
        def _():
            db1_ref[...] = jnp.sum(accb_ref[...], axis=0, keepdims=True)
            ddw_ref[...] = jnp.sum(accw_ref[...], axis=1)

    return pl.pallas_call(
        body, name=name, grid=(nt,),
        in_specs=[_row_spec(tt, D),
                  pl.BlockSpec((CONV_HALO, D), lambda i: (jnp.minimum((i + 1) * hb, nhb - 1), 0)),
                  _row_spec(tt, D),
                  pl.BlockSpec((CONV_HALO, D), lambda i: (jnp.maximum(i * hb - 1, 0), 0)),
                  pl.BlockSpec((tt, D), lambda i: (i, 0)), pl.BlockSpec((tt, D), lambda i: (i, 1)),
                  pl.BlockSpec((KW, D), lambda i: (0, 0))],
        out_specs=[_row_spec(tt, 2 * D), _vec_spec(2 * D), pl.BlockSpec((KW, D), lambda i: (0, 0))],
        out_shape=[jax.ShapeDtypeStruct((T, 2 * D), BF16), jax.ShapeDtypeStruct((1, 2 * D), F32),
                   jax.ShapeDtypeStruct((KW, D), F32)],
        scratch_shapes=[pltpu.VMEM((tt + CONV_HALO, D), F32), pltpu.VMEM((tt + CONV_HALO, D), F32),
                        pltpu.VMEM((tt, D), F32), pltpu.VMEM((KW, 8, D), F32), pltpu.VMEM((8, 2 * D), F32)]
                       + [pltpu.VMEM((SUBLANES - 1, tt + CONV_HALO - SUBLANES, D), F32)] * 2,
        compiler_params=_cparams(("arbitrary",)),
    )(dc, dc, u, u, h1, h1, dw)


def _t5_bucket(dist):
    max_exact = REL_BUCKETS // 2
    large = max_exact + (np.log(np.maximum(dist, 1) / max_exact) / math.log(REL_MAX_DIST / max_exact)
                         * (REL_BUCKETS - max_exact)).astype(np.int32)
    large = np.minimum(large, REL_BUCKETS - 1)
    return np.where(dist < max_exact, dist, large).astype(np.int32)


def _bucket_table(dil):
    i = np.arange(BAND)[:, None]
    j = np.arange(2 * BAND)[None, :]
    delta = i - j + BAND
    return _t5_bucket(np.clip(delta, 0, None) * dil)


def bias_expand(name, rel_bias, dil):
    n_heads = rel_bias.shape[1]
    idx = jnp.asarray(_bucket_table(dil))

    def body(rel_ref, idx_ref, out_ref):
        h = pl.program_id(0)
        idxv = idx_ref[...]
        b = jnp.zeros((BAND, 2 * BAND), F32)
        for bk in range(REL_BUCKETS):
            b = jnp.where(idxv == bk, rel_ref[bk, h], b)
        out_ref[...] = b

    return pl.pallas_call(
        body, name=name, grid=(n_heads,),
        in_specs=[pl.BlockSpec(memory_space=pltpu.SMEM), pl.BlockSpec((BAND, 2 * BAND), lambda h: (0, 0))],
        out_specs=pl.BlockSpec((None, BAND, 2 * BAND), lambda h: (h, 0, 0)),
        out_shape=jax.ShapeDtypeStruct((n_heads, BAND, 2 * BAND), F32),
        compiler_params=_cparams(("arbitrary",)),
    )(rel_bias, idx)


def relbias_grad(name, dsb_list):
    n_heads = dsb_list[0].shape[0]
    idxs = [jnp.asarray(_bucket_table(d)) for _, d in BRANCHES]
    nb = len(BRANCHES)

    def body(*refs):
        ds_refs, idx_refs, out_ref = refs[:nb], refs[nb:2 * nb], refs[2 * nb]
        lane = lax.broadcasted_iota(jnp.int32, (1, 128), 1)
        row = jnp.zeros((1, 128), F32)
        for bk in range(REL_BUCKETS):
            tot = jnp.zeros((1, 1), F32)
            for ds_ref, idx_ref in zip(ds_refs, idx_refs):
                sel = jnp.where(idx_ref[...] == bk, ds_ref[...], 0.0)
                tot = tot + jnp.sum(jnp.sum(sel, axis=0, keepdims=True), axis=1, keepdims=True)
            row = jnp.where(lane == bk, tot, row)
        out_ref[...] = row

    return pl.pallas_call(
        body, name=name, grid=(n_heads,),
        in_specs=[pl.BlockSpec((None, BAND, 2 * BAND), lambda h: (h, 0, 0))] * nb
                 + [pl.BlockSpec((BAND, 2 * BAND), lambda h: (0, 0))] * nb,
        out_specs=pl.BlockSpec((None, 1, 128), lambda h: (h, 0, 0)),
        out_shape=jax.ShapeDtypeStruct((n_heads, 1, 128), F32),
        compiler_params=_cparams(("arbitrary",)),
    )(*dsb_list, *idxs)


def _band_mask():
    i = lax.broadcasted_iota(jnp.int32, (BAND, 2 * BAND), 0)
    j = lax.broadcasted_iota(jnp.int32, (BAND, 2 * BAND), 1)
    return (j >= i) & (j <= i + BAND), j


def _rep2(x):
    return jnp.concatenate([x, x], axis=1)


ATTN_TOKENS = 2048
MERGE_ROWS = 256


def _rows(ref, start, n, dil):
    if dil == 1:
        return ref[pl.ds(start, n), :]
    return ref[pl.ds(start, n, stride=dil), :]


def _set_rows(ref, start, n, dil, val):
    if dil == 1:
        ref[pl.ds(start, n), :] = val
    else:
        ref[pl.ds(start, n, stride=dil), :] = val


def _attn_specs(ct, n_heads, chunk_of):
    cur = lambda col0: pl.BlockSpec((ct, HEAD_DIM), lambda h, s: (chunk_of(s), col0 + h))
    prev = lambda col0: pl.BlockSpec((ct, HEAD_DIM), lambda h, s: (jnp.maximum(chunk_of(s) - 1, 0), col0 + h))
    bias = pl.BlockSpec((None, BAND, 2 * BAND), lambda h, s: (h, 0, 0))
    return cur, prev, bias


def _load_keys(kext_ref, vext_ref, base, k_ref, v_ref, kp_ref, vp_ref, r, dil, ct):
    lc = ct // dil
    kext_ref[pl.ds(base, BAND), :] = _rows(kp_ref, ct - BAND * dil + r, BAND, dil).astype(BF16)
    vext_ref[pl.ds(base, BAND), :] = _rows(vp_ref, ct - BAND * dil + r, BAND, dil).astype(BF16)
    kext_ref[pl.ds(base + BAND, lc), :] = _rows(k_ref, r, lc, dil).astype(BF16)
    vext_ref[pl.ds(base + BAND, lc), :] = _rows(v_ref, r, lc, dil).astype(BF16)


ATTN_GROUP = 4


def _two_level(dil):
    if dil > ATTN_GROUP and dil % ATTN_GROUP == 0:
        return ATTN_GROUP, dil // ATTN_GROUP
    return 1, dil


def _slot_rows(ct):
    return max(ct + BAND, ATTN_GROUP * (ct // ATTN_GROUP + BAND))


def _window_mask(band, jcol, a, c):
    if a > 0:
        return band
    return band & jnp.logical_or(jcol >= BAND, c > 0)


def attn_fwd(name, q, kv, biases):
    T, D = q.shape
    n_heads = D // HEAD_DIM
    ct = min(ATTN_TOKENS, T)
    n_chunks = T // ct
    nbr = len(BRANCHES)
    scale = HEAD_DIM ** -0.5
    nt_dims = (((1,), (1,)), ((), ()))
    nn_dims = (((1,), (0,)), ((), ()))

    n_in = 5

    def body(*refs):
        ins = refs[:n_in]
        b_refs = refs[n_in:n_in + nbr]
        o_ref, obf_ref, lse_ref = refs[n_in + nbr:n_in + nbr + 3]
        kext_ref, vext_ref, acc_ref, m_ref, l_ref = refs[n_in + nbr + 3:n_in + nbr + 8]
        tmp_in = refs[n_in + nbr + 8:n_in + nbr + 8 + n_in]
        tmp_out = refs[n_in + nbr + 8 + n_in:]
        c = pl.program_id(1)
        band, jcol = _band_mask()

        def residue(src, dst, slot, r, dil, cte, bias_v):
            q_ref, k_ref, v_ref, kp_ref, vp_ref = src
            lc = cte // dil
            base = slot * (BAND + lc)
            _load_keys(kext_ref, vext_ref, base, k_ref, v_ref, kp_ref, vp_ref, r, dil, cte)
            for a in range(lc // BAND):
                tok = r + a * BAND * dil
                qa = _rows(q_ref, tok, BAND, dil).astype(BF16)
                kw = kext_ref[pl.ds(base + a * BAND, 2 * BAND), :]
                vw = vext_ref[pl.ds(base + a * BAND, 2 * BAND), :]
                s = lax.dot_general(qa, kw, nt_dims, preferred_element_type=F32) * scale + bias_v
                s = jnp.where(_window_mask(band, jcol, a, c), s, NEG_BIG)
                m = jnp.max(s, axis=-1, keepdims=True)
                p = jnp.exp(s - m)
                den = jnp.sum(p, axis=-1, keepdims=True)
                pv = lax.dot_general(p.astype(BF16), vw, nn_dims, preferred_element_type=F32)
                _set_rows(dst[0], tok, BAND, dil, pv)
                _set_rows(dst[1], tok, BAND, dil, jnp.broadcast_to(m, (BAND, HEAD_DIM)))
                _set_rows(dst[2], tok, BAND, dil, jnp.broadcast_to(den, (BAND, HEAD_DIM)))

        for bi, (win, dil) in enumerate(BRANCHES):
            bias_v = b_refs[bi][...]
            dst = (acc_ref.at[bi], m_ref.at[bi], l_ref.at[bi])
            outer, inner = _two_level(dil)
            if outer == 1:
                for r in range(dil):
                    residue(ins, dst, r % ATTN_GROUP, r, dil, ct, bias_v)
            else:
                cte = ct // outer

                def group(r1, carry, bias_v=bias_v, dst=dst, outer=outer, inner=inner, cte=cte):
                    for t_ref, x_ref in zip(tmp_in, ins):
                        t_ref[...] = _rows(x_ref, r1, cte, outer)
                    for r2 in range(inner):
                        residue(tmp_in, tmp_out, r2 % ATTN_GROUP, r2, inner, cte, bias_v)
                    for t_ref, d_ref in zip(tmp_out, dst):
                        _set_rows(d_ref, r1, cte, outer, t_ref[...])
                    return carry

                lax.fori_loop(0, outer, group, 0)

        def merge(i, carry):
            rows = pl.ds(pl.multiple_of(i * MERGE_ROWS, MERGE_ROWS), MERGE_ROWS)
            ms = [m_ref[bi, rows, :] for bi in range(nbr)]
            m = functools.reduce(jnp.maximum, ms)
            ws = [jnp.exp(mb - m) for mb in ms]
            tot = functools.reduce(lambda x, y: x + y, [w * l_ref[bi, rows, :] for bi, w in enumerate(ws)])
            o = functools.reduce(lambda x, y: x + y, [w * acc_ref[bi, rows, :] for bi, w in enumerate(ws)]) / tot
            o_ref[rows, :] = o
            obf_ref[rows, :] = o.astype(BF16)
            lse_ref[rows, :] = m + jnp.log(tot)
            return carry

        lax.fori_loop(0, ct // min(MERGE_ROWS, ct), merge, 0)

    cur, prev, bias = _attn_specs(ct, n_heads, lambda s: s)
    small = (ct // ATTN_GROUP, HEAD_DIM)
    return pl.pallas_call(
        body, name=name, grid=(n_heads, n_chunks),
        in_specs=[cur(0), cur(0), cur(n_heads), prev(0), prev(n_heads)] + [bias] * nbr,
        out_specs=[cur(0)] * 3,
        out_shape=[jax.ShapeDtypeStruct((T, D), F32), jax.ShapeDtypeStruct((T, D), BF16), jax.ShapeDtypeStruct((T, D), F32)],
        scratch_shapes=[pltpu.VMEM((_slot_rows(ct), HEAD_DIM), BF16)] * 2 + [pltpu.VMEM((nbr, ct, HEAD_DIM), F32)] * 3
                       + [pltpu.VMEM(small, F32)] * (n_in + 3),
        compiler_params=_cparams(("arbitrary", "arbitrary")),
    )(q, kv, kv, kv, kv, *biases)


def attn_bwd(name, q, kv, do, o, lse, biases):
    T, D = q.shape
    n_heads = D // HEAD_DIM
    ct = min(ATTN_TOKENS, T)
    n_chunks = T // ct
    nbr = len(BRANCHES)
    scale = HEAD_DIM ** -0.5
    nt_dims = (((1,), (1,)), ((), ()))
    tn_dims = (((0,), (0,)), ((), ()))
    nn_dims = (((1,), (0,)), ((), ()))
    mrows = min(MERGE_ROWS, ct)
    n_src = 8
    n_acc = 5

    def body(q_ref, k_ref, v_ref, do_ref, o_ref, lse_ref, kp_ref, vp_ref, *rest):
        b_refs = rest[:nbr]
        dq_ref, dk_ref, dv_ref = rest[nbr:nbr + 3]
        dsb_refs = rest[nbr + 3:2 * nbr + 3]
        sc = rest[2 * nbr + 3:]
        kext_ref, vext_ref, dkext_ref, dvext_ref, dqa_ref, dka_ref, dva_ref, dsum_ref, ck_ref, cv_ref = sc[:10]
        tmp_in = sc[10:10 + n_src]
        tmp_acc = sc[10 + n_src:10 + n_src + n_acc]
        dsacc_ref = sc[10 + n_src + n_acc]
        step = pl.program_id(1)
        c = n_chunks - 1 - step
        band, jcol = _band_mask()

        @pl.when(step == 0)
        def _():
            ck_ref[...] = jnp.zeros_like(ck_ref)
            cv_ref[...] = jnp.zeros_like(cv_ref)
            for r in dsb_refs:
                r[...] = jnp.zeros_like(r)

        def prep(i, carry):
            rows = pl.ds(pl.multiple_of(i * mrows, mrows), mrows)
            dsum_ref[rows, :] = jnp.broadcast_to(jnp.sum(do_ref[rows, :] * o_ref[rows, :], axis=-1, keepdims=True), (mrows, HEAD_DIM))
            dqa_ref[rows, :] = jnp.zeros((mrows, HEAD_DIM), F32)
            dka_ref[rows, :] = ck_ref[rows, :]
            dva_ref[rows, :] = cv_ref[rows, :]
            ck_ref[rows, :] = jnp.zeros((mrows, HEAD_DIM), F32)
            cv_ref[rows, :] = jnp.zeros((mrows, HEAD_DIM), F32)
            return carry

        lax.fori_loop(0, ct // mrows, prep, 0)

        def residue(src, acc, slot, r, dil, cte, bias_v):
            sq, sk, sv, sdo, slse, sdsum, skp, svp = src
            adq, adk, adv, ack, acv = acc
            lc = cte // dil
            base = slot * (BAND + lc)
            _load_keys(kext_ref, vext_ref, base, sk, sv, skp, svp, r, dil, cte)
            dkext_ref[pl.ds(base, BAND + lc), :] = jnp.zeros((BAND + lc, HEAD_DIM), F32)
            dvext_ref[pl.ds(base, BAND + lc), :] = jnp.zeros((BAND + lc, HEAD_DIM), F32)
            for a in range(lc // BAND):
                tok = r + a * BAND * dil
                qa = _rows(sq, tok, BAND, dil).astype(BF16)
                doa = _rows(sdo, tok, BAND, dil).astype(BF16)
                kw = kext_ref[pl.ds(base + a * BAND, 2 * BAND), :]
                vw = vext_ref[pl.ds(base + a * BAND, 2 * BAND), :]
                s = lax.dot_general(qa, kw, nt_dims, preferred_element_type=F32) * scale + bias_v
                p = jnp.where(_window_mask(band, jcol, a, c), jnp.exp(s - _rep2(_rows(slse, tok, BAND, dil))), 0.0)
                dp = lax.dot_general(doa, vw, nt_dims, preferred_element_type=F32)
                ds = p * (dp - _rep2(_rows(sdsum, tok, BAND, dil)))
                dsacc_ref[slot] += ds
                dsb16 = ds.astype(BF16)
                dqw = lax.dot_general(dsb16, kw, nn_dims, preferred_element_type=F32) * scale
                _set_rows(adq, tok, BAND, dil, _rows(adq, tok, BAND, dil) + dqw)
                dkext_ref[pl.ds(base + a * BAND, 2 * BAND), :] += lax.dot_general(dsb16, qa, tn_dims, preferred_element_type=F32) * scale
                dvext_ref[pl.ds(base + a * BAND, 2 * BAND), :] += lax.dot_general(p.astype(BF16), doa, tn_dims, preferred_element_type=F32)

            _set_rows(adk, r, lc, dil, _rows(adk, r, lc, dil) + dkext_ref[pl.ds(base + BAND, lc), :])
            _set_rows(adv, r, lc, dil, _rows(adv, r, lc, dil) + dvext_ref[pl.ds(base + BAND, lc), :])
            last = cte - BAND * dil + r
            _set_rows(ack, last, BAND, dil, _rows(ack, last, BAND, dil) + dkext_ref[pl.ds(base, BAND), :])
            _set_rows(acv, last, BAND, dil, _rows(acv, last, BAND, dil) + dvext_ref[pl.ds(base, BAND), :])

        full_src = (q_ref, k_ref, v_ref, do_ref, lse_ref, dsum_ref, kp_ref, vp_ref)
        full_acc = (dqa_ref, dka_ref, dva_ref, ck_ref, cv_ref)
        for bi, (win, dil) in enumerate(BRANCHES):
            bias_v = b_refs[bi][...]
            dsacc_ref[...] = jnp.zeros_like(dsacc_ref)
            outer, inner = _two_level(dil)
            if outer == 1:
                for r in range(dil):
                    residue(full_src, full_acc, r % ATTN_GROUP, r, dil, ct, bias_v)
            else:
                cte = ct // outer

                def group(r1, carry, bias_v=bias_v, outer=outer, inner=inner, cte=cte):
                    for t_ref, x_ref in zip(tmp_in, full_src):
                        t_ref[...] = _rows(x_ref, r1, cte, outer)
                    for t_ref in tmp_acc:
                        t_ref[...] = jnp.zeros_like(t_ref)
                    for r2 in range(inner):
                        residue(tmp_in, tmp_acc, r2 % ATTN_GROUP, r2, inner, cte, bias_v)
                    for t_ref, a_ref in zip(tmp_acc, full_acc):
                        _set_rows(a_ref, r1, cte, outer, _rows(a_ref, r1, cte, outer) + t_ref[...])
                    return carry

                lax.fori_loop(0, outer, group, 0)
            dsb_refs[bi][...] += functools.reduce(lambda x, y: x + y, [dsacc_ref[s] for s in range(ATTN_GROUP)])

        dq_ref[...] = dqa_ref[...].astype(BF16)
        dk_ref[...] = dka_ref[...].astype(BF16)
        dv_ref[...] = dva_ref[...].astype(BF16)

    cur, prev, bias = _attn_specs(ct, n_heads, lambda s: n_chunks - 1 - s)
    small = (ct // ATTN_GROUP, HEAD_DIM)
    res = pl.pallas_call(
        body, name=name, grid=(n_heads, n_chunks),
        in_specs=[cur(0), cur(0), cur(n_heads), cur(0), cur(0), cur(0), prev(0), prev(n_heads)] + [bias] * nbr,
        out_specs=[cur(0)] * 3 + [bias] * nbr,
        out_shape=[jax.ShapeDtypeStruct((T, D), BF16)] * 3 + [jax.ShapeDtypeStruct((n_heads, BAND, 2 * BAND), F32)] * nbr,
        scratch_shapes=[pltpu.VMEM((_slot_rows(ct), HEAD_DIM), BF16)] * 2 + [pltpu.VMEM((_slot_rows(ct), HEAD_DIM), F32)] * 2
                       + [pltpu.VMEM((ct, HEAD_DIM), F32)] * 6 + [pltpu.VMEM(small, F32)] * (n_src + n_acc)
                       + [pltpu.VMEM((ATTN_GROUP, BAND, 2 * BAND), F32)],
        compiler_params=_cparams(("arbitrary", "arbitrary")),
    )(q, kv, kv, do, o, lse, kv, kv, *biases)
    return res[0], res[1], res[2], list(res[3:])


def _divisor_tile(n, cap, mult):
    if n <= cap:
        return n
    t = cap - cap % mult
    while n % t:
        t -= mult
    return t


def _tile2(R, C):
    return _divisor_tile(R, 512, 8), _divisor_tile(C, 1024, 128)


def half_cast(name, dw, core):
    S, R, C = dw.shape
    hr = R // 2
    tr, tc = _tile2(hr, C)
    nrb = hr // tr

    def body(c_ref, x_ref, o_ref):
        o_ref[...] = x_ref[...].astype(BF16)

    return pl.pallas_call(
        body, name=name,
        grid_spec=pltpu.PrefetchScalarGridSpec(
            num_scalar_prefetch=1, grid=(S, nrb, C // tc),
            in_specs=[pl.BlockSpec((None, tr, tc), lambda s, i, j, c: (s, (1 - c[0]) * nrb + i, j))],
            out_specs=pl.BlockSpec((None, tr, tc), lambda s, i, j, c: (s, i, j))),
        out_shape=jax.ShapeDtypeStruct((S, hr, C), BF16),
        compiler_params=_cparams(("parallel", "parallel", "parallel")),
    )(core, dw)


def pair_sum(name, dw, recv, core):
    S, R, C = dw.shape
    hr = R // 2
    tr, tc = _tile2(hr, C)
    nrb = hr // tr

    def body(c_ref, x_ref, r_ref, p_ref, pbf_ref):
        p = x_ref[...] + r_ref[...].astype(F32)
        p_ref[...] = p
        pbf_ref[...] = p.astype(BF16)

    out = pl.BlockSpec((None, tr, tc), lambda s, i, j, c: (s, i, j))
    return pl.pallas_call(
        body, name=name,
        grid_spec=pltpu.PrefetchScalarGridSpec(
            num_scalar_prefetch=1, grid=(S, nrb, C // tc),
            in_specs=[pl.BlockSpec((None, tr, tc), lambda s, i, j, c: (s, c[0] * nrb + i, j)), out],
            out_specs=[out, out]),
        out_shape=[jax.ShapeDtypeStruct((S, hr, C), F32), jax.ShapeDtypeStruct((S, hr, C), BF16)],
        compiler_params=_cparams(("parallel", "parallel", "parallel")),
    )(core, dw, recv)


def chip_sum(name, p, recv, chip, core):
    S, hr, C = p.shape
    tr, tc = _tile2(hr, C)
    nrb = hr // tr

    def body(chip_ref, core_ref, p_ref, r_ref, o_ref):
        acc = p_ref[...]
        for t in range(N_CHIPS - 1):
            acc = acc + r_ref[t].astype(F32)
        o_ref[...] = acc

    return pl.pallas_call(
        body, name=name,
        grid_spec=pltpu.PrefetchScalarGridSpec(
            num_scalar_prefetch=2, grid=(nrb, C // tc),
            in_specs=[pl.BlockSpec((None, tr, tc), lambda i, j, s, c: (s[0], i, j)),
                      pl.BlockSpec((N_CHIPS - 1, tr, tc), lambda i, j, s, c: (0, i, j))],
            out_specs=pl.BlockSpec((tr, tc), lambda i, j, s, c: (c[0] * nrb + i, j))),
        out_shape=jax.ShapeDtypeStruct((2 * hr, C), F32),
        compiler_params=_cparams(("parallel", "parallel")),
    )(chip, core, p, recv)


def adamw(name, w, g, m, v):
    R, C = w.shape
    tr, tc = _tile2(R, C)
    c1 = 1.0 - ADAM_B1 ** ADAM_STEP
    c2 = 1.0 - ADAM_B2 ** ADAM_STEP

    def body(w_ref, g_ref, m_ref, v_ref, d_ref, nm_ref, nv_ref):
        gv = g_ref[...]
        nm = ADAM_B1 * m_ref[...] + (1.0 - ADAM_B1) * gv
        nv = ADAM_B2 * v_ref[...] + (1.0 - ADAM_B2) * (gv * gv)
        nm_ref[...] = nm
        nv_ref[...] = nv
        d_ref[...] = -ADAM_LR * ((nm / c1) / (jnp.sqrt(nv / c2) + ADAM_EPS) + ADAM_WD * w_ref[...])

    spec = pl.BlockSpec((tr, tc), lambda i, j: (i, j))
    return pl.pallas_call(
        body, name=name, grid=(R // tr, C // tc), in_specs=[spec] * 4, out_specs=[spec] * 3,
        out_shape=[jax.ShapeDtypeStruct((R, C), F32)] * 3,
        compiler_params=_cparams(("parallel", "parallel")),
    )(w, g, m, v)


def adamw_layers(name, w, g_layers, m, v):
    nl, R, C = w.shape
    tr, tc = _tile2(R, C)
    ni, nj = R // tr, C // tc
    c1 = 1.0 - ADAM_B1 ** ADAM_STEP
    c2 = 1.0 - ADAM_B2 ** ADAM_STEP

    def body(w_ref, *rest):
        g_refs = rest[:nl]
        m_ref, v_ref, g_ref, d_ref, nm_ref, nv_ref = rest[nl:]
        layer = pl.program_id(0)
        gv = g_refs[0][...]
        for l in range(1, nl):
            gv = jnp.where(layer == l, g_refs[l][...], gv)
        nm = ADAM_B1 * m_ref[...] + (1.0 - ADAM_B1) * gv
        nv = ADAM_B2 * v_ref[...] + (1.0 - ADAM_B2) * (gv * gv)
        g_ref[...] = gv
        nm_ref[...] = nm
        nv_ref[...] = nv
        d_ref[...] = -ADAM_LR * ((nm / c1) / (jnp.sqrt(nv / c2) + ADAM_EPS) + ADAM_WD * w_ref[...])

    def g_spec(l):
        def index(layer, i, j):
            return (jnp.where(layer == l, i, jnp.where(layer < l, 0, ni - 1)),
                    jnp.where(layer == l, j, jnp.where(layer < l, 0, nj - 1)))
        return pl.BlockSpec((tr, tc), index)

    spec = pl.BlockSpec((None, tr, tc), lambda layer, i, j: (layer, i, j))
    return pl.pallas_call(
        body, name=name, grid=(nl, ni, nj),
        in_specs=[spec] + [g_spec(l) for l in range(nl)] + [spec] * 2, out_specs=[spec] * 4,
        out_shape=[jax.ShapeDtypeStruct((nl, R, C), F32)] * 4,
        compiler_params=_cparams(("arbitrary", "arbitrary", "arbitrary")),
    )(w, *g_layers, m, v)


def sum_devices(name, gathered):
    n, R, C = gathered.shape

    def body(x_ref, o_ref):
        acc = x_ref[0]
        for d in range(1, n):
            acc = acc + x_ref[d]
        o_ref[...] = acc

    return pl.pallas_call(
        body, name=name, in_specs=[pl.BlockSpec(memory_space=pltpu.VMEM)],
        out_specs=pl.BlockSpec(memory_space=pltpu.VMEM),
        out_shape=jax.ShapeDtypeStruct((R, C), F32),
    )(gathered)


def _place():
    x, y, c = lax.axis_index("x"), lax.axis_index("y"), lax.axis_index("c")
    return x, y, c


def _other_chips(x, y):
    return [(1 - x, y), (x, 1 - y), (1 - x, 1 - y)]


def all_gather8(name, block):
    R, C = block.shape

    def body(x_ref, out_ref, send_sems, recv_sems, local_sem):
        x, y, c = _place()
        me, sibling = (x, y, c), (x, y, 1 - c)
        chips = _other_chips(x, y)

        def rows(px, py, pc):
            return out_ref.at[4 * px + 2 * py + pc]

        def copy(k, blk, to, src=None):
            return pltpu.make_async_remote_copy(
                src_ref=rows(*blk) if src is None else src, dst_ref=rows(*blk),
                send_sem=send_sems.at[k], recv_sem=recv_sems.at[k], device_id=to, device_id_type=MESH)

        mine = pltpu.make_async_copy(x_ref, rows(*me), local_sem)
        mine.start()
        first = [copy(0, me, sibling, src=x_ref)]
        first += [copy(1 + j, me, (*chip, c), src=x_ref) for j, chip in enumerate(chips)]
        for cp in first:
            cp.start()
        passed = [copy(4 + j, (*chip, c), sibling) for j, chip in enumerate(chips)]
        for j, chip in enumerate(chips):
            copy(1 + j, (*chip, c), me).wait_recv()
            passed[j].start()
        copy(0, sibling, me).wait_recv()
        for j, chip in enumerate(chips):
            copy(4 + j, (*chip, 1 - c), me).wait_recv()
        for cp in first + passed:
            cp.wait_send()
        mine.wait()

    return pl.pallas_call(
        body, name=name, out_shape=jax.ShapeDtypeStruct((N_DEV, R, C), block.dtype),
        in_specs=[pl.BlockSpec(memory_space=pltpu.VMEM)], out_specs=pl.BlockSpec(memory_space=pltpu.VMEM),
        scratch_shapes=[pltpu.SemaphoreType.DMA((7,)), pltpu.SemaphoreType.DMA((7,)), pltpu.SemaphoreType.DMA],
    )(block)


_HBM = pl.BlockSpec(memory_space=pltpu.HBM)
_SEM = pl.BlockSpec(memory_space=pltpu.SEMAPHORE)
_DATAFLOW = pltpu.SideEffectType.DATAFLOW_SIDE_EFFECTING


def _in_hbm(a):
    return pltpu.with_memory_space_constraint(a, pltpu.HBM)


def split_start(name, srcs, lands, n_sem, plan):
    ns, nl = len(srcs), len(lands)

    def body(*refs):
        src, land = refs[:ns], refs[ns:ns + nl]
        send_sems, recv_sems = refs[ns + nl], refs[ns + nl + 1]
        token = refs[-1]
        outgoing, _ = plan(src, land, send_sems, recv_sems)
        for cp in outgoing:
            cp.start()
        token[...] = jnp.zeros_like(token)

    bufs = list(srcs) + list(lands)
    res = pl.pallas_call(
        body, name=name,
        out_shape=(pltpu.SemaphoreType.DMA((n_sem,)), pltpu.SemaphoreType.DMA((n_sem,)),
                   *[pltpu.HBM(b.shape, b.dtype) for b in bufs], jax.ShapeDtypeStruct((8, 128), F32)),
        in_specs=[_HBM] * (ns + nl),
        out_specs=(_SEM, _SEM, *[_HBM] * (ns + nl), pl.BlockSpec(memory_space=pltpu.VMEM)),
        input_output_aliases={i: 2 + i for i in range(ns + nl)},
        compiler_params=pltpu.CompilerParams(has_side_effects=_DATAFLOW),
    )(*[_in_hbm(b) for b in bufs])
    return res[0], res[1], list(res[2:2 + ns]), list(res[2 + ns:2 + ns + nl]), res[-1]


def split_wait(name, started, after, plan):
    send_sems, recv_sems, srcs, lands, _ = started
    ns, nl = len(srcs), len(lands)

    def body(*refs):
        src, land = refs[:ns], refs[ns:ns + nl]
        send, recv = refs[ns + nl], refs[ns + nl + 1]
        outgoing, incoming = plan(src, land, send, recv)
        for cp in outgoing:
            cp.wait_send()
        for cp in incoming:
            cp.wait_recv()

    bufs = list(srcs) + list(lands)
    res = pl.pallas_call(
        body, name=name,
        out_shape=tuple(pltpu.HBM(b.shape, b.dtype) for b in bufs),
        in_specs=[_HBM] * (ns + nl) + [_SEM, _SEM, pl.BlockSpec(memory_space=pl.ANY)],
        out_specs=tuple([_HBM] * (ns + nl)),
        input_output_aliases={i: i for i in range(ns + nl)},
        compiler_params=pltpu.CompilerParams(has_side_effects=_DATAFLOW),
    )(*bufs, send_sems, recv_sems, after)
    return list(res[ns:])


def _rcopy(src, dst, send_sems, ks, recv_sems, kr, device):
    return pltpu.make_async_remote_copy(src_ref=src, dst_ref=dst, send_sem=send_sems.at[ks], recv_sem=recv_sems.at[kr],
                                        device_id=device, device_id_type=MESH)


def _half_rows(ref, h):
    hr = ref.shape[0] // 2
    return ref.at[pl.ds(h * hr, hr)]


def _gather_plan(src, land, send_sems, recv_sems):
    x, y, c = _place()
    me_chip = 2 * x + y
    chips = _other_chips(x, y)
    outgoing, incoming = [], []
    for w, buf in enumerate(land):
        mine = _half_rows(buf.at[me_chip], c)
        for t, chip in enumerate(chips):
            slot = 2 * chip[0] + chip[1]
            for cc in range(2):
                outgoing.append(_rcopy(mine, mine, send_sems, 6 * w + 2 * t + cc, recv_sems, 6 * w + 2 * t + c, (*chip, cc)))
                theirs = _half_rows(buf.at[slot], cc)
                incoming.append(_rcopy(theirs, theirs, send_sems, 6 * w + 2 * t + cc, recv_sems, 6 * w + 2 * t + cc, (*chip, cc)))
    return outgoing, incoming


def _swap_plan(src, land, send_sems, recv_sems):
    x, y, c = _place()
    cp = _rcopy(src[0], land[0], send_sems, 0, recv_sems, 0, (x, y, 1 - c))
    return [cp], [cp]


def _scatter_plan(src, land, send_sems, recv_sems):
    x, y, c = _place()
    cps = [_rcopy(src[0].at[2 * chip[0] + chip[1]], land[0].at[t], send_sems, t, recv_sems, t, (*chip, c))
           for t, chip in enumerate(_other_chips(x, y))]
    return cps, cps


def _share_plan(src, land, send_sems, recv_sems):
    x, y, c = _place()
    mine, theirs = _half_rows(land[0], c), _half_rows(land[0], 1 - c)
    return ([_rcopy(mine, mine, send_sems, 0, recv_sems, 0, (x, y, 1 - c))],
            [_rcopy(theirs, theirs, send_sems, 0, recv_sems, 0, (x, y, 1 - c))])


def place_shard(name, shard, chip, deps=(), layer=None):
    R, C = shard.shape[-2:]
    tr, tc = _tile2(R, C)

    def body(chip_ref, x_ref, *rest):
        rest[-1][...] = x_ref[...].astype(BF16)

    if layer is None:
        src = pl.BlockSpec((tr, tc), lambda i, j, s: (i, j))
    else:
        src = pl.BlockSpec((None, tr, tc), lambda i, j, s: (layer, i, j))
    return pl.pallas_call(
        body, name=name,
        grid_spec=pltpu.PrefetchScalarGridSpec(
            num_scalar_prefetch=1, grid=(R // tr, C // tc),
            in_specs=[src] + [pl.BlockSpec(d.shape, lambda i, j, s: (0, 0)) for d in deps],
            out_specs=pl.BlockSpec((None, tr, tc), lambda i, j, s: (s[0], i, j))),
        out_shape=jax.ShapeDtypeStruct((N_CHIPS, R, C), BF16),
        compiler_params=_cparams(("parallel", "parallel")),
    )(chip, shard, *deps)


class GradExchange:
    SCATTER_TICKS = 2

    def __init__(self, chip1, core, shard, mom, vel):
        self.chip1, self.core, self.shard, self.mom, self.vel = chip1, core, shard, mom, vel
        self.inflight, self.tokens, self.results = [], [], {}

    def take_deps(self):
        deps, self.tokens = self.tokens, []
        return deps

    def _start(self, name, srcs, lands, n_sem, plan):
        started = split_start(name, srcs, lands, n_sem, plan)
        self.tokens.append(started[-1])
        return started

    def add(self, n, dw):
        S, R, C = dw.shape
        to_sibling = half_cast("rs_cast_" + n, dw, self.core)
        started = self._start("rs_swap_start_" + n, [to_sibling], [lax.empty((S, R // 2, C), BF16)], 1, _swap_plan)
        self.inflight.append(dict(n=n, dw=dw, stage=0, started=started, ticks=0))

    def tick(self, after):
        for it in self.inflight:
            n = it["n"]
            if it["stage"] == 0:
                (recv,) = split_wait("rs_swap_wait_" + n, it["started"], after, _swap_plan)
                p, pbf = pair_sum("rs_pair_sum_" + n, it["dw"], recv, self.core)
                S, hr, C = pbf.shape
                it.update(stage=1, p=p, ticks=0,
                          started=self._start("rs_scatter_start_" + n, [pbf], [lax.empty((N_CHIPS - 1, hr, C), BF16)], 3, _scatter_plan))
            elif it["stage"] == 1:
                it["ticks"] += 1
                if it["ticks"] >= self.SCATTER_TICKS:
                    (recv,) = split_wait("rs_scatter_wait_" + n, it["started"], after, _scatter_plan)
                    half = chip_sum("rs_chip_sum_" + n, it["p"], recv, self.chip1, self.core)
                    it.update(stage=2, started=self._start("rs_share_start_" + n, [], [half], 1, _share_plan))
            elif it["stage"] == 2:
                (grad,) = split_wait("rs_share_wait_" + n, it["started"], after, _share_plan)
                if n in self.shard:
                    self.results[n] = (grad,) + tuple(adamw("adamw_" + n, self.shard[n], grad, self.mom[n], self.vel[n]))
                else:
                    self.results[n] = (grad,)
                it["stage"] = 3
        self.inflight = [it for it in self.inflight if it["stage"] < 3]

    def flush(self, after):
        while self.inflight:
            self.tick(after)


def _pack(arrs):
    parts = []
    for a in arrs:
        flat = a.reshape(-1).astype(F32)
        n = flat.shape[0]
        padded = -(-n // 1024) * 1024
        parts.append(jnp.pad(flat, (0, padded - n)).reshape(padded // 128, 128))
    return jnp.concatenate(parts, axis=0)


def _unpack(buf, shapes):
    out, row = [], 0
    for shp in shapes:
        n = int(np.prod(shp))
        rows = -(-n // 1024) * 8
        out.append(buf[row:row + rows].reshape(-1)[:n].reshape(shp))
        row += rows
    return out


def _bias_epi(acc, b):
    return (acc + b,)


def local_step(x, target, W, P, ex, first_deps=()):
    T, D = x.shape
    g = {}
    plain = lambda acc: (acc,)

    (h1,) = mm_nn("pw1_fwd", x, W("pw1", x), "col", _bias_epi, [F32],
                  extras=[(P["pw1_b"], "row")] + [(d, "dep") for d in first_deps])
    u, cpre, s = conv_fwd("conv_fwd", h1, P["dw_w"], P["dw_b"], P["cln_g"], P["cln_b"])
    (mix0,) = mm_nn("pw2_fwd", s, W("pw2", s), "row", _bias_epi, [F32], extras=[(P["pw2_b"], "row")])
    ln = [None] * 4
    gam = [P["ln_mix_g"][0:1], P["ln_mlp_g"][0:1], P["ln_mix_g"][1:2], P["ln_mlp_g"][1:2]]
    bet = [P["ln_mix_b"][0:1], P["ln_mlp_b"][0:1], P["ln_mix_b"][1:2], P["ln_mlp_b"][1:2]]
    ln[0] = ln_fwd("ln0_fwd", mix0, x)(gam[0], bet[0])

    def mlp_fwd(tag, i_ln, n1, n2):
        xhat, rstd, xbf = ln[i_ln]

        def up_epi(acc):
            r = jnp.maximum(acc, 0.0)
            return r * r, r

        hid, relu = mm_nn(tag + "_up", xbf, W(n1, xbf), "col", up_epi, [BF16, BF16])
        (mlp,) = mm_nn(tag + "_down", hid, W(n2, hid), "row", plain, [F32])
        ln[i_ln + 1] = ln_fwd(tag + "_ln", mlp, xhat, gam[i_ln], bet[i_ln])(gam[i_ln + 1], bet[i_ln + 1])
        return hid, relu

    hid0 = mlp_fwd("mlp0", 0, "w1_0", "w2_0")

    x2bf = ln[1][2]
    (kv,) = mm_nn("kv_fwd", x2bf, W("kv", x2bf), "col", plain, [F32])
    (q,) = mm_nn("q_fwd", x2bf, W("wq", kv), "row", plain, [F32])
    biases = [bias_expand("bias_d%d" % d, P["rel_bias"], d) for _, d in BRANCHES]
    assert all(win // d == BAND and min(ATTN_TOKENS, T) % (BAND * d) == 0 for win, d in BRANCHES)
    o, obf, lse = attn_fwd("attn_fwd", q, kv, biases)
    (attn,) = mm_nn("wo_fwd", obf, W("wo", obf), "row", plain, [F32])
    ln[2] = ln_fwd("ln2_fwd", attn, ln[1][0], gam[1], bet[1])(gam[2], bet[2])
    hid1 = mlp_fwd("mlp1", 2, "w1_1", "w2_1")

    dr3, dr3bf, g["ln_mlp_g1"], g["ln_mlp_b1"], _, loss_sum = ln_bwd(
        "ln3_bwd", ln[3][0], ln[3][1], gam[3], target=target, beta=bet[3])

    def dw_step(name, wname, a, cot, axis):
        dw = mm_tn(name, a, cot, W(wname, a).shape, axis, deps=ex.take_deps())
        ex.tick(dw)
        ex.add(wname, dw)

    def dx_step(name, cot, wname, axis, epilogue, out_dtype, extras):
        deps = [(d, "dep") for d in ex.take_deps()]
        (out,) = mm_nt(name, cot, W(wname, cot), axis, epilogue, [out_dtype], extras=list(extras) + deps)
        ex.tick(out)
        return out

    def mlp_bwd(tag, i_ln, n1, n2, hid_relu, dr, drbf):
        xbf = ln[i_ln][2]
        hid, relu = hid_relu
        dw_step(tag + "_dw2", n2, hid, drbf, "row")
        dp = dx_step(tag + "_dhid", drbf, n2, "row", lambda acc, r: (acc * (2.0 * r.astype(F32)),), BF16, [(relu, "tile")])
        dw_step(tag + "_dw1", n1, xbf, dp, "col")
        return dx_step(tag + "_dx", dp, n1, "col", lambda acc, e: (acc + ALPHA * e,), F32, [(dr, "tile")])

    dx3 = mlp_bwd("mlp1", 2, "w1_1", "w2_1", hid1, dr3, dr3bf)
    dr2, dr2bf, g["ln_mix_g1"], g["ln_mix_b1"], _ = ln_bwd("ln2_bwd", ln[2][0], ln[2][1], gam[2], dy=dx3)
    dw_step("wo_dw", "wo", obf, dr2bf, "row")
    do = dx_step("wo_dx", dr2bf, "wo", "row", plain, F32, [])
    dq, dk, dv, dsbs = attn_bwd("attn_bwd", q, kv, do, o, lse, biases)
    g["rel_bias"] = relbias_grad("relbias_grad", dsbs)[:, 0, :REL_BUCKETS].T
    dkv = jnp.concatenate([dk, dv], axis=1)
    dw_step("wq_dw", "wq", x2bf, dq, "row")
    dw_step("kv_dw", "kv", x2bf, dkv, "col")
    dx2a = dx_step("wq_dx", dq, "wq", "row", lambda acc, e: (acc + ALPHA * e,), F32, [(dr2, "tile")])
    dx2 = dx_step("kv_dx", dkv, "kv", "col", lambda acc, e: (acc + e,), F32, [(dx2a, "tile")])

    dr1, dr1bf, g["ln_mlp_g0"], g["ln_mlp_b0"], _ = ln_bwd("ln1_bwd", ln[1][0], ln[1][1], gam[1], dy=dx2)
    dx1 = mlp_bwd("mlp0", 0, "w1_0", "w2_0", hid0, dr1, dr1bf)
    dr0, dr0bf, g["ln_mix_g0"], g["ln_mix_b0"], g["pw2_b"] = ln_bwd("ln0_bwd", ln[0][0], ln[0][1], gam[0], dy=dx1)

    dw_step("pw2_dw", "pw2", s, dr0bf, "row")
    ds = dx_step("pw2_dx", dr0bf, "pw2", "row", plain, F32, [])
    dc, g["cln_g"], g["cln_b"], g["dw_b"] = conv_bwd_ln("conv_bwd_ln", ds, cpre, P["cln_g"], P["cln_b"])
    dh1, g["pw1_b"], g["dw_w"] = conv_bwd_taps("conv_bwd_taps", dc, u, h1, P["dw_w"])
    dw_step("pw1_dw", "pw1", x, dh1, "col")
    dx = dx_step("pw1_dx", dh1, "pw1", "col", lambda acc, e: (acc + ALPHA * e,), F32, [(dr0, "tile")])
    return loss_sum, dx, g


BIG = ("pw1", "pw2", "w1_0", "w2_0", "kv", "wq", "wo", "w1_1", "w2_1")


def kernel(x, conv_pw1_w, conv_pw1_b, conv_dw_w, conv_dw_b, conv_ln_g, conv_ln_b, conv_pw2_w, conv_pw2_b, w_kv, attn_wq, attn_wo, rel_bias, mlp_w1, mlp_w2, ln_mix_g, ln_mix_b, ln_mlp_g, ln_mlp_b, loss_target, m_conv_pw1_w, m_conv_pw1_b, m_conv_dw_w, m_conv_dw_b, m_conv_ln_g, m_conv_ln_b, m_conv_pw2_w, m_conv_pw2_b, m_w_kv, m_attn_wq, m_attn_wo, m_rel_bias, m_mlp_w1, m_mlp_w2, m_ln_mix_g, m_ln_mix_b, m_ln_mlp_g, m_ln_mlp_b, v_conv_pw1_w, v_conv_pw1_b, v_conv_dw_w, v_conv_dw_b, v_conv_ln_g, v_conv_ln_b, v_conv_pw2_w, v_conv_pw2_b, v_w_kv, v_attn_wq, v_attn_wo, v_rel_bias, v_mlp_w1, v_mlp_w2, v_ln_mix_g, v_ln_mix_b, v_ln_mlp_g, v_ln_mlp_b):
    _, T, D = x.shape
    xi, yi, ci = _place()
    chip = 2 * xi + yi
    core = jnp.reshape(ci, (1,)).astype(jnp.int32)
    chip1 = jnp.reshape(chip, (1,)).astype(jnp.int32)

    def two_d(a):
        return a.reshape(a.shape[-2:])

    shard = {"pw1": two_d(conv_pw1_w), "pw2": two_d(conv_pw2_w), "kv": w_kv, "wq": two_d(attn_wq), "wo": two_d(attn_wo)}
    mom = {"pw1": two_d(m_conv_pw1_w), "pw2": two_d(m_conv_pw2_w), "kv": m_w_kv, "wq": two_d(m_attn_wq), "wo": two_d(m_attn_wo)}
    vel = {"pw1": two_d(v_conv_pw1_w), "pw2": two_d(v_conv_pw2_w), "kv": v_w_kv, "wq": two_d(v_attn_wq), "wo": two_d(v_attn_wo)}
    stacked = {"w1_0": (mlp_w1, 0), "w1_1": (mlp_w1, 1), "w2_0": (mlp_w2, 0), "w2_1": (mlp_w2, 1)}

    started = {}
    for n in BIG:
        deps = [started[prev][-1] for prev in list(started)[-1:]]
        src, layer = stacked.get(n, (shard.get(n), None))
        started[n] = split_start("gather_start_" + n, [], [place_shard("place_" + n, src, chip1, deps, layer)], 6, _gather_plan)
    gathered = {}

    def W(n, after):
        if n not in gathered:
            (gathered[n],) = split_wait("gather_wait_" + n, started[n], after, _gather_plan)
        return gathered[n]

    sharded_small = [conv_pw1_b, conv_dw_w[0], conv_dw_b, conv_ln_g, conv_ln_b, conv_pw2_b]
    sh_shapes = [a.shape for a in sharded_small]
    small_all = all_gather8("gather_small", _pack(sharded_small))
    per_chip = [_unpack(small_all[2 * j], sh_shapes) for j in range(N_CHIPS)]
    full = [jnp.concatenate([per_chip[j][i] for j in range(N_CHIPS)], axis=-1) for i in range(len(sharded_small))]
    P = dict(pw1_b=full[0], dw_w=full[1], dw_b=full[2], cln_g=full[3], cln_b=full[4], pw2_b=full[5],
             rel_bias=rel_bias, ln_mix_g=ln_mix_g, ln_mix_b=ln_mix_b, ln_mlp_g=ln_mlp_g, ln_mlp_b=ln_mlp_b)

    ex = GradExchange(chip1, core, shard, mom, vel)
    loss_sum, dx, g = local_step(x.reshape(T, D), loss_target.reshape(T, D), W, P, ex,
                                 first_deps=[started[n][-1] for n in BIG])
    loss = (0.5 / D) * lax.psum(loss_sum[0, 0], ("x", "y", "c"))

    small_names = ["pw1_b", "dw_w", "dw_b", "cln_g", "cln_b", "pw2_b", "rel_bias",
                   "ln_mix_g0", "ln_mix_g1", "ln_mix_b0", "ln_mix_b1", "ln_mlp_g0", "ln_mlp_g1", "ln_mlp_b0", "ln_mlp_b1"]
    small_grads = [g[n] for n in small_names]
    sg_shapes = [a.shape for a in small_grads]
    summed = sum_devices("small_grad_sum", all_gather8("gather_small_grads", _pack(small_grads)))
    sg = dict(zip(small_names, _unpack(summed, sg_shapes)))

    def my_cols(a, width):
        return lax.dynamic_slice_in_dim(a, chip * width, width, axis=a.ndim - 1)

    small_g = [my_cols(sg["pw1_b"], conv_pw1_b.shape[-1]),
               my_cols(sg["dw_w"], conv_dw_w.shape[-1])[None],
               my_cols(sg["dw_b"], conv_dw_b.shape[-1]), my_cols(sg["cln_g"], conv_ln_g.shape[-1]),
               my_cols(sg["cln_b"], conv_ln_b.shape[-1]), my_cols(sg["pw2_b"], conv_pw2_b.shape[-1]),
               sg["rel_bias"],
               jnp.concatenate([sg["ln_mix_g0"], sg["ln_mix_g1"]], axis=0),
               jnp.concatenate([sg["ln_mix_b0"], sg["ln_mix_b1"]], axis=0),
               jnp.concatenate([sg["ln_mlp_g0"], sg["ln_mlp_g1"]], axis=0),
               jnp.concatenate([sg["ln_mlp_b0"], sg["ln_mlp_b1"]], axis=0)]
    small_w = [conv_pw1_b, conv_dw_w, conv_dw_b, conv_ln_g, conv_ln_b, conv_pw2_b, rel_bias, ln_mix_g, ln_mix_b, ln_mlp_g, ln_mlp_b]
    small_m = [m_conv_pw1_b, m_conv_dw_w, m_conv_dw_b, m_conv_ln_g, m_conv_ln_b, m_conv_pw2_b, m_rel_bias, m_ln_mix_g, m_ln_mix_b, m_ln_mlp_g, m_ln_mlp_b]
    small_v = [v_conv_pw1_b, v_conv_dw_w, v_conv_dw_b, v_conv_ln_g, v_conv_ln_b, v_conv_pw2_b, v_rel_bias, v_ln_mix_g, v_ln_mix_b, v_ln_mlp_g, v_ln_mlp_b]
    sw_shapes = [a.shape for a in small_w]
    small_g = [a.reshape(s) for a, s in zip(small_g, sw_shapes)]
    upd_small = adamw("adamw_small", _pack(small_w), _pack(small_g), _pack(small_m), _pack(small_v))
    sd, snm, snv = (_unpack(b, sw_shapes) for b in upd_small)

    ex.flush(upd_small[0])
    res_w1 = adamw_layers("adamw_w1", mlp_w1, [ex.results["w1_0"][0], ex.results["w1_1"][0]], m_mlp_w1, v_mlp_w1)
    res_w2 = adamw_layers("adamw_w2", mlp_w2, [ex.results["w2_0"][0], ex.results["w2_1"][0]], m_mlp_w2, v_mlp_w2)

    def big_out(k):
        one = {n: ex.results[n][k] for n in shard}
        return dict(pw1=one["pw1"][None], pw2=one["pw2"][None], kv=one["kv"], wq=one["wq"][None], wo=one["wo"][None],
                    w1=res_w1[k], w2=res_w2[k])

    def ordered(big, small):
        return [big["pw1"], small[0], small[1], small[2], small[3], small[4], big["pw2"], small[5], big["kv"], big["wq"],
                big["wo"], small[6], big["w1"], big["w2"], small[7], small[8], small[9], small[10]]

    grads = ordered(big_out(0), small_g)
    deltas = ordered(big_out(1), sd)
    new_m = ordered(big_out(2), snm)
    new_v = ordered(big_out(3), snv)
    return (loss, dx.reshape(1, T, D), *grads, *deltas, *new_m, *new_v)
```

```python
import functools
import math

import numpy as np
import jax
import jax.numpy as jnp
from jax import lax
from jax.experimental import pallas as pl
from jax.experimental.pallas import tpu as pltpu

F32 = jnp.float32
BF16 = jnp.bfloat16

HEAD_DIM = 128
BAND = 128
BRANCHES = ((128, 1), (512, 4), (2048, 16))
CONV_WIDTH = 31
CONV_HALO = 32
REL_BUCKETS = 32
REL_MAX_DIST = 2048
DEPTH = 2
ALPHA = (2 * DEPTH) ** 0.25
LN_EPS = 1e-5
ADAM_LR, ADAM_B1, ADAM_B2, ADAM_EPS, ADAM_WD, ADAM_STEP = 0.001, 0.9, 0.999, 1e-08, 0.01, 10

N_CHIPS = 4
N_DEV = 8
MESH = pl.DeviceIdType.MESH
VMEM_LIMIT_BYTES = 56 * 1024 * 1024
MM_TM, MM_TN, MM_TK = 1024, 1024, 2048
ROW_TILE = 256
CONV_TILE = 128
NEG_BIG = -1e30


def _cparams(sem):
    return pltpu.CompilerParams(dimension_semantics=sem, vmem_limit_bytes=VMEM_LIMIT_BYTES)


def _sigmoid(x):
    return 1.0 / (1.0 + jnp.exp(-x))


def _wspec(wshape, axis, br, bc, rsel, csel):
    _, R, C = wshape
    if axis == "col":
        if bc > C:
            assert bc % C == 0, (wshape, bc)
            return pl.BlockSpec((bc // C, br, C), lambda *g: (csel(*g), rsel(*g), 0))
        nb = C // bc
        assert nb * bc == C, (wshape, bc)
        return pl.BlockSpec((None, br, bc), lambda *g: (csel(*g) // nb, rsel(*g), csel(*g) % nb))
    if br > R:
        assert br % R == 0, (wshape, br)
        return pl.BlockSpec((br // R, R, bc), lambda *g: (rsel(*g), 0, csel(*g)))
    nb = R // br
    assert nb * br == R, (wshape, br)
    return pl.BlockSpec((None, br, bc), lambda *g: (rsel(*g) // nb, rsel(*g) % nb, csel(*g)))


def _join_shards(b, axis):
    if b.ndim == 2:
        return b
    if axis == "row":
        return b.reshape(b.shape[0] * b.shape[1], b.shape[2])
    return jnp.concatenate([b[s] for s in range(b.shape[0])], axis=1)


def _split_shards(r, shape, axis):
    if len(shape) == 2:
        return r
    if axis == "row":
        return r.reshape(shape)
    return jnp.stack([r[:, s * shape[2]:(s + 1) * shape[2]] for s in range(shape[0])])


def _full_dims(wshape, axis):
    _, R, C = wshape
    return (R, N_CHIPS * C) if axis == "col" else (N_CHIPS * R, C)


def _mm_body(nk, kinds, n_out, dims, epilogue, axis):
    n_extra = len(kinds)

    def body(*refs):
        a_ref, b_ref = refs[0], refs[1]
        extra = [r for r, kind in zip(refs[2:2 + n_extra], kinds) if kind != "dep"]
        outs = refs[2 + n_extra:2 + n_extra + n_out]
        part = lax.dot_general(a_ref[...].astype(BF16), _join_shards(b_ref[...], axis).astype(BF16), (dims, ((), ())),
                               preferred_element_type=F32)

        def write(res):
            for r, o in zip(res, outs):
                o[...] = _split_shards(r, o.shape, axis).astype(o.dtype)

        if nk == 1:
            write(epilogue(part, *[e[...] for e in extra]))
            return
        acc_ref = refs[2 + n_extra + n_out]
        k = pl.program_id(2)

        @pl.when(k == 0)
        def _():
            acc_ref[...] = part

        @pl.when(k > 0)
        def _():
            acc_ref[...] += part

        @pl.when(k == nk - 1)
        def _():
            write(epilogue(acc_ref[...], *[e[...] for e in extra]))
    return body


def _long_tk(a, k_dim):
    tk = min(MM_TK, k_dim)
    if a.dtype == BF16 and k_dim >= 4 * MM_TK:
        tk = 2 * MM_TK
    return tk


def _extra_specs(extras, tm, tn):
    specs = []
    for arr, kind in extras:
        if kind == "tile":
            specs.append(pl.BlockSpec((tm, tn), lambda i, j, k: (i, j)))
        elif kind == "dep":
            specs.append(pl.BlockSpec(arr.shape, lambda i, j, k: (0, 0)))
        else:
            specs.append(pl.BlockSpec((1, tn), lambda i, j, k: (0, j)))
    return specs


def mm_nn(name, a, w, axis, epilogue, out_dtypes, extras=()):
    M, K = a.shape
    Kw, N = _full_dims(w.shape, axis)
    assert K == Kw
    tm, tn, tk = min(MM_TM, M), min(MM_TN, N), _long_tk(a, K)
    nk = K // tk
    in_specs = [pl.BlockSpec((tm, tk), lambda i, j, k: (i, k)),
                _wspec(w.shape, axis, tk, tn, lambda i, j, k: k, lambda i, j, k: j)]
    in_specs += _extra_specs(extras, tm, tn)
    body = _mm_body(nk, [kind for _, kind in extras], len(out_dtypes), ((1,), (0,)), epilogue, axis)
    return pl.pallas_call(
        body, name=name, grid=(M // tm, N // tn, nk), in_specs=in_specs,
        out_specs=[pl.BlockSpec((tm, tn), lambda i, j, k: (i, j)) for _ in out_dtypes],
        out_shape=[jax.ShapeDtypeStruct((M, N), d) for d in out_dtypes],
        scratch_shapes=[pltpu.VMEM((tm, tn), F32)] if nk > 1 else [],
        compiler_params=_cparams(("parallel", "parallel", "arbitrary")),
    )(a, w, *[e for e, _ in extras])


def mm_nt(name, g, w, axis, epilogue, out_dtypes, extras=()):
    M, N = g.shape
    K, Nw = _full_dims(w.shape, axis)
    assert N == Nw
    tm, tn, tk = min(MM_TM, M), min(MM_TN, K), min(MM_TK, N)
    nk = N // tk
    in_specs = [pl.BlockSpec((tm, tk), lambda i, j, k: (i, k)),
                _wspec(w.shape, axis, tn, tk, lambda i, j, k: j, lambda i, j, k: k)]
    in_specs += _extra_specs(extras, tm, tn)
    body = _mm_body(nk, [kind for _, kind in extras], len(out_dtypes), ((1,), (1,)), epilogue, axis)
    return pl.pallas_call(
        body, name=name, grid=(M // tm, K // tn, nk), in_specs=in_specs,
        out_specs=[pl.BlockSpec((tm, tn), lambda i, j, k: (i, j)) for _ in out_dtypes],
        out_shape=[jax.ShapeDtypeStruct((M, K), d) for d in out_dtypes],
        scratch_shapes=[pltpu.VMEM((tm, tn), F32)] if nk > 1 else [],
        compiler_params=_cparams(("parallel", "parallel", "arbitrary")),
    )(g, w, *[e for e, _ in extras])


def mm_tn(name, a, g, wshape, axis, deps=()):
    M, K = a.shape
    Mg, N = g.shape
    assert M == Mg and (K, N) == _full_dims(wshape, axis)
    tm, tn, tk = min(MM_TM, K), min(MM_TN, N), _long_tk(a, M)
    nk = M // tk
    body = _mm_body(nk, ["dep"] * len(deps), 1, ((0,), (0,)), lambda acc: (acc,), axis)
    return pl.pallas_call(
        body, name=name, grid=(K // tm, N // tn, nk),
        in_specs=[pl.BlockSpec((tk, tm), lambda i, j, k: (k, i)),
                  pl.BlockSpec((tk, tn), lambda i, j, k: (k, j))] + _extra_specs([(d, "dep") for d in deps], tm, tn),
        out_specs=[_wspec(wshape, axis, tm, tn, lambda i, j, k: i, lambda i, j, k: j)],
        out_shape=[jax.ShapeDtypeStruct(wshape, F32)],
        scratch_shapes=[pltpu.VMEM((tm, tn), F32)] if nk > 1 else [],
        compiler_params=_cparams(("parallel", "parallel", "arbitrary")),
    )(a, g, *deps)[0]


def _row_spec(tr, width):
    return pl.BlockSpec((tr, width), lambda i: (i, 0))


def _vec_spec(width):
    return pl.BlockSpec((1, width), lambda i: (0, 0))


def _fold8(x):
    r, d = x.shape
    return jnp.sum(x.reshape(r // 8, 8, d), axis=0)


def ln_fwd(name, f, prev, prev_g=None, prev_b=None):
    T, D = f.shape
    tr = min(ROW_TILE, T)
    affine = prev_g is not None

    def body(*refs):
        if affine:
            f_ref, p_ref, pg_ref, pb_ref, g_ref, b_ref, xhat_ref, rstd_ref, xbf_ref = refs
            xprev = p_ref[...] * pg_ref[...] + pb_ref[...]
        else:
            f_ref, p_ref, g_ref, b_ref, xhat_ref, rstd_ref, xbf_ref = refs
            xprev = p_ref[...]
        r = ALPHA * xprev + f_ref[...]
        mu = jnp.mean(r, axis=-1, keepdims=True)
        cen = r - mu
        var = jnp.mean(cen * cen, axis=-1, keepdims=True)
        rstd = lax.rsqrt(var + LN_EPS)
        xhat = cen * rstd
        xhat_ref[...] = xhat
        rstd_ref[...] = rstd
        xbf_ref[...] = (xhat * g_ref[...] + b_ref[...]).astype(BF16)

    def call(g, b):
        ins = [f, prev] + ([prev_g, prev_b] if affine else []) + [g, b]
        specs = [_row_spec(tr, D), _row_spec(tr, D)] + ([_vec_spec(D)] * 2 if affine else []) + [_vec_spec(D)] * 2
        return pl.pallas_call(
            body, name=name, grid=(T // tr,), in_specs=specs,
            out_specs=[_row_spec(tr, D), _row_spec(tr, 1), _row_spec(tr, D)],
            out_shape=[jax.ShapeDtypeStruct((T, D), F32), jax.ShapeDtypeStruct((T, 1), F32),
                       jax.ShapeDtypeStruct((T, D), BF16)],
            compiler_params=_cparams(("parallel",)),
        )(*ins)
    return call


def ln_bwd(name, xhat, rstd, gamma, dy=None, target=None, beta=None):
    T, D = xhat.shape
    tr = min(ROW_TILE, T)
    nt = T // tr
    head = target is not None

    def body(*refs):
        if head:
            xhat_ref, rstd_ref, g_ref, tgt_ref, b_ref = refs[:5]
            outs = refs[5:]
        else:
            xhat_ref, rstd_ref, g_ref, dy_ref = refs[:4]
            outs = refs[4:]
        dr_ref, drbf_ref, dg_ref, db_ref, cs_ref = outs[:5]
        rest = outs[5:]
        if head:
            loss_ref, acc_ref = rest
        else:
            (acc_ref,) = rest
        i = pl.program_id(0)
        xhat_v = xhat_ref[...]
        gam = g_ref[...]
        if head:
            diff = xhat_v * gam + b_ref[...] - tgt_ref[...]
            dyv = diff * (1.0 / D)
        else:
            dyv = dy_ref[...]
        dxh = dyv * gam
        m1 = jnp.mean(dxh, axis=-1, keepdims=True)
        m2 = jnp.mean(dxh * xhat_v, axis=-1, keepdims=True)
        dr = rstd_ref[...] * (dxh - m1 - xhat_v * m2)
        dr_ref[...] = dr
        drbf_ref[...] = dr.astype(BF16)

        @pl.when(i == 0)
        def _():
            acc_ref[...] = jnp.zeros_like(acc_ref)

        acc_ref[0] += _fold8(dyv * xhat_v)
        acc_ref[1] += _fold8(dyv)
        acc_ref[2] += _fold8(dr)
        if head:
            acc_ref[3] += _fold8(diff * diff)

        @pl.when(i == nt - 1)
        def _():
            dg_ref[...] = jnp.sum(acc_ref[0], axis=0, keepdims=True)
            db_ref[...] = jnp.sum(acc_ref[1], axis=0, keepdims=True)
            cs_ref[...] = jnp.sum(acc_ref[2], axis=0, keepdims=True)
            if head:
                loss_ref[...] = jnp.sum(jnp.sum(acc_ref[3], axis=0, keepdims=True), axis=1, keepdims=True)

    ins = [xhat, rstd, gamma] + ([target, beta] if head else [dy])
    specs = [_row_spec(tr, D), _row_spec(tr, 1), _vec_spec(D)] + ([_row_spec(tr, D), _vec_spec(D)] if head else [_row_spec(tr, D)])
    out_specs = [_row_spec(tr, D), _row_spec(tr, D), _vec_spec(D), _vec_spec(D), _vec_spec(D)]
    out_shape = [jax.ShapeDtypeStruct((T, D), F32), jax.ShapeDtypeStruct((T, D), BF16)] + [jax.ShapeDtypeStruct((1, D), F32)] * 3
    if head:
        out_specs.append(pl.BlockSpec((1, 1), lambda i: (0, 0)))
        out_shape.append(jax.ShapeDtypeStruct((1, 1), F32))
    return pl.pallas_call(
        body, name=name, grid=(nt,), in_specs=specs, out_specs=out_specs, out_shape=out_shape,
        scratch_shapes=[pltpu.VMEM((4, 8, D), F32)],
        compiler_params=_cparams(("arbitrary",)),
    )(*ins)


CONV_ROWS, CONV_COLS = 64, 512
CONV_COLS_BWD = 256


def _tap_chunks(tt, D, cols=CONV_COLS):
    for r0 in range(0, tt, min(CONV_ROWS, tt)):
        for c0 in range(0, D, min(cols, D)):
            yield r0, min(CONV_ROWS, tt), c0, min(cols, D)


SUBLANES = 8


def _shifted_copies(ext_ref, sh_ref):
    n = sh_ref.shape[1]
    zero = jnp.minimum(pl.program_id(0), 0)
    for b in range(1, SUBLANES):
        sh_ref[zero + (b - 1)] = ext_ref[pl.ds(b, n), :]


def _rows_at(ext_ref, sh_ref, off, nr, cols):
    a, b = divmod(off, SUBLANES)
    if b == 0:
        return ext_ref[pl.ds(off, nr), cols]
    return sh_ref[b - 1, pl.ds(a * SUBLANES, nr), cols]


def conv_fwd(name, h1, dw, dwb, lng, lnb):
    T, D2 = h1.shape
    D = D2 // 2
    tt = min(CONV_TILE, T)
    hb = tt // CONV_HALO
    KW = dw.shape[0]
    lead = CONV_HALO - (KW - 1)

    def body(a_ref, g_ref, ah_ref, gh_ref, dw_ref, dwb_ref, lng_ref, lnb_ref, u_ref, c_ref, s_ref, ext_ref, sh_ref):
        i = pl.program_id(0)
        u = a_ref[...] * _sigmoid(g_ref[...])
        u_ref[...] = u
        uh = ah_ref[...] * _sigmoid(gh_ref[...])
        ext_ref[pl.ds(0, CONV_HALO), :] = jnp.where(i > 0, uh, 0.0)
        ext_ref[pl.ds(CONV_HALO, tt), :] = u
        _shifted_copies(ext_ref, sh_ref)
        for r0, nr, c0, nc in _tap_chunks(tt, D):
            cols = pl.ds(c0, nc)
            acc = jnp.zeros((nr, nc), F32) + dwb_ref[:, cols]
            for k in range(KW):
                acc = acc + dw_ref[pl.ds(k, 1), cols] * _rows_at(ext_ref, sh_ref, r0 + lead + k, nr, cols)
            c_ref[pl.ds(r0, nr), cols] = acc
        c = c_ref[...]
        mu = jnp.mean(c, axis=-1, keepdims=True)
        cen = c - mu
        var = jnp.mean(cen * cen, axis=-1, keepdims=True)
        n = cen * lax.rsqrt(var + LN_EPS) * lng_ref[...] + lnb_ref[...]
        s_ref[...] = (n * _sigmoid(n)).astype(BF16)

    halo = lambda col: pl.BlockSpec((CONV_HALO, D), lambda i: (jnp.maximum(i * hb - 1, 0), col))
    return pl.pallas_call(
        body, name=name, grid=(T // tt,),
        in_specs=[pl.BlockSpec((tt, D), lambda i: (i, 0)), pl.BlockSpec((tt, D), lambda i: (i, 1)), halo(0), halo(1),
                  pl.BlockSpec((KW, D), lambda i: (0, 0)), _vec_spec(D), _vec_spec(D), _vec_spec(D)],
        out_specs=[_row_spec(tt, D)] * 3,
        out_shape=[jax.ShapeDtypeStruct((T, D), F32), jax.ShapeDtypeStruct((T, D), F32), jax.ShapeDtypeStruct((T, D), BF16)],
        scratch_shapes=[pltpu.VMEM((tt + CONV_HALO, D), F32),
                        pltpu.VMEM((SUBLANES - 1, tt + CONV_HALO - SUBLANES, D), F32)],
        compiler_params=_cparams(("parallel",)),
    )(h1, h1, h1, h1, dw, dwb, lng, lnb)


def conv_bwd_ln(name, ds, c, lng, lnb):
    T, D = c.shape
    tr = min(ROW_TILE, T)
    nt = T // tr

    def body(ds_ref, c_ref, g_ref, b_ref, dc_ref, dg_ref, db_ref, cs_ref, acc_ref):
        i = pl.program_id(0)
        cv = c_ref[...]
        mu = jnp.mean(cv, axis=-1, keepdims=True)
        cen = cv - mu
        var = jnp.mean(cen * cen, axis=-1, keepdims=True)
        rstd = lax.rsqrt(var + LN_EPS)
        chat = cen * rstd
        n = chat * g_ref[...] + b_ref[...]
        sg = _sigmoid(n)
        dn = ds_ref[...] * (sg * (1.0 + n * (1.0 - sg)))
        dxh = dn * g_ref[...]
        m1 = jnp.mean(dxh, axis=-1, keepdims=True)
        m2 = jnp.mean(dxh * chat, axis=-1, keepdims=True)
        dc = rstd * (dxh - m1 - chat * m2)
        dc_ref[...] = dc

        @pl.when(i == 0)
        def _():
            acc_ref[...] = jnp.zeros_like(acc_ref)

        acc_ref[0] += _fold8(dn * chat)
        acc_ref[1] += _fold8(dn)
        acc_ref[2] += _fold8(dc)

        @pl.when(i == nt - 1)
        def _():
            dg_ref[...] = jnp.sum(acc_ref[0], axis=0, keepdims=True)
            db_ref[...] = jnp.sum(acc_ref[1], axis=0, keepdims=True)
            cs_ref[...] = jnp.sum(acc_ref[2], axis=0, keepdims=True)

    return pl.pallas_call(
        body, name=name, grid=(nt,),
        in_specs=[_row_spec(tr, D), _row_spec(tr, D), _vec_spec(D), _vec_spec(D)],
        out_specs=[_row_spec(tr, D), _vec_spec(D), _vec_spec(D), _vec_spec(D)],
        out_shape=[jax.ShapeDtypeStruct((T, D), F32)] + [jax.ShapeDtypeStruct((1, D), F32)] * 3,
        scratch_shapes=[pltpu.VMEM((3, 8, D), F32)],
        compiler_params=_cparams(("arbitrary",)),
    )(ds, c, lng, lnb)


def conv_bwd_taps(name, dc, u, h1, dw):
    T, D = dc.shape
    tt = min(CONV_TILE, T)
    nt = T // tt
    hb = tt // CONV_HALO
    nhb = T // CONV_HALO
    KW = dw.shape[0]
    lead = CONV_HALO - (KW - 1)

    def body(dc_ref, dcn_ref, u_ref, uh_ref, a_ref, g_ref, dw_ref, dh1_ref, db1_ref, ddw_ref,
             edc_ref, eu_ref, du_ref, accw_ref, accb_ref, shdc_ref, shu_ref):
        i = pl.program_id(0)

        @pl.when(i == 0)
        def _():
            accw_ref[...] = jnp.zeros_like(accw_ref)
            accb_ref[...] = jnp.zeros_like(accb_ref)

        edc_ref[pl.ds(0, tt), :] = dc_ref[...]
        edc_ref[pl.ds(tt, CONV_HALO), :] = jnp.where(i < nt - 1, dcn_ref[...], 0.0)
        eu_ref[pl.ds(0, CONV_HALO), :] = jnp.where(i > 0, uh_ref[...], 0.0)
        eu_ref[pl.ds(CONV_HALO, tt), :] = u_ref[...]
        _shifted_copies(edc_ref, shdc_ref)
        _shifted_copies(eu_ref, shu_ref)
        for r0, nr, c0, nc in _tap_chunks(tt, D, CONV_COLS_BWD):
            cols = pl.ds(c0, nc)
            dcv = dc_ref[pl.ds(r0, nr), cols]
            acc = jnp.zeros((nr, nc), F32)
            for k in range(KW):
                acc = acc + dw_ref[pl.ds(k, 1), cols] * _rows_at(edc_ref, shdc_ref, r0 + (KW - 1) - k, nr, cols)
                accw_ref[k, :, cols] += _fold8(dcv * _rows_at(eu_ref, shu_ref, r0 + lead + k, nr, cols))
            du_ref[pl.ds(r0, nr), cols] = acc
        du = du_ref[...]
        sg = _sigmoid(g_ref[...])
        da = du * sg
        dg = du * a_ref[...] * sg * (1.0 - sg)
        dh1_ref[:, pl.ds(0, D)] = da.astype(BF16)
        dh1_ref[:, pl.ds(D, D)] = dg.astype(BF16)
        accb_ref[:, pl.ds(0, D)] += _fold8(da)
        accb_ref[:, pl.ds(D, D)] += _fold8(dg)

        @pl.when(i == nt - 1)
        def _():
            db1_ref[...] = jnp.sum(accb_ref[...], axis=0, keepdims=True)
            ddw_ref[...] = jnp.sum(accw_ref[...], axis=1)

    return pl.pallas_call(
        body, name=name, grid=(nt,),
        in_specs=[_row_spec(tt, D),
                  pl.BlockSpec((CONV_HALO, D), lambda i: (jnp.minimum((i + 1) * hb, nhb - 1), 0)),
                  _row_spec(tt, D),
                  pl.BlockSpec((CONV_HALO, D), lambda i: (jnp.maximum(i * hb - 1, 0), 0)),
                  pl.BlockSpec((tt, D), lambda i: (i, 0)), pl.BlockSpec((tt, D), lambda i: (i, 1)),
                  pl.BlockSpec((KW, D), lambda i: (0, 0))],
        out_specs=[_row_spec(tt, 2 * D), _vec_spec(2 * D), pl.BlockSpec((KW, D), lambda i: (0, 0))],
        out_shape=[jax.ShapeDtypeStruct((T, 2 * D), BF16), jax.ShapeDtypeStruct((1, 2 * D), F32),
                   jax.ShapeDtypeStruct((KW, D), F32)],
        scratch_shapes=[pltpu.VMEM((tt + CONV_HALO, D), F32), pltpu.VMEM((tt + CONV_HALO, D), F32),
                        pltpu.VMEM((tt, D), F32), pltpu.VMEM((KW, 8, D), F32), pltpu.VMEM((8, 2 * D), F32)]
                       + [pltpu.VMEM((SUBLANES - 1, tt + CONV_HALO - SUBLANES, D), F32)] * 2,
        compiler_params=_cparams(("arbitrary",)),
    )(dc, dc, u, u, h1, h1, dw)


def _t5_bucket(dist):
    max_exact = REL_BUCKETS // 2
    large = max_exact + (np.log(np.maximum(dist, 1) / max_exact) / math.log(REL_MAX_DIST / max_exact)
                         * (REL_BUCKETS - max_exact)).astype(np.int32)
    large = np.minimum(large, REL_BUCKETS - 1)
    return np.where(dist < max_exact, dist, large).astype(np.int32)


def _bucket_table(dil):
    i = np.arange(BAND)[:, None]
    j = np.arange(2 * BAND)[None, :]
    delta = i - j + BAND
    return _t5_bucket(np.clip(delta, 0, None) * dil)


def bias_expand(name, rel_bias, dil):
    n_heads = rel_bias.shape[1]
    idx = jnp.asarray(_bucket_table(dil))

    def body(rel_ref, idx_ref, out_ref):
        h = pl.program_id(0)
        idxv = idx_ref[...]
        b = jnp.zeros((BAND, 2 * BAND), F32)
        for bk in range(REL_BUCKETS):
            b = jnp.where(idxv == bk, rel_ref[bk, h], b)
        out_ref[...] = b

    return pl.pallas_call(
        body, name=name, grid=(n_heads,),
        in_specs=[pl.BlockSpec(memory_space=pltpu.SMEM), pl.BlockSpec((BAND, 2 * BAND), lambda h: (0, 0))],
        out_specs=pl.BlockSpec((None, BAND, 2 * BAND), lambda h: (h, 0, 0)),
        out_shape=jax.ShapeDtypeStruct((n_heads, BAND, 2 * BAND), F32),
        compiler_params=_cparams(("arbitrary",)),
    )(rel_bias, idx)


def relbias_grad(name, dsb_list):
    n_heads = dsb_list[0].shape[0]
    idxs = [jnp.asarray(_bucket_table(d)) for _, d in BRANCHES]
    nb = len(BRANCHES)

    def body(*refs):
        ds_refs, idx_refs, out_ref = refs[:nb], refs[nb:2 * nb], refs[2 * nb]
        lane = lax.broadcasted_iota(jnp.int32, (1, 128), 1)
        row = jnp.zeros((1, 128), F32)
        for bk in range(REL_BUCKETS):
            tot = jnp.zeros((1, 1), F32)
            for ds_ref, idx_ref in zip(ds_refs, idx_refs):
                sel = jnp.where(idx_ref[...] == bk, ds_ref[...], 0.0)
                tot = tot + jnp.sum(jnp.sum(sel, axis=0, keepdims=True), axis=1, keepdims=True)
            row = jnp.where(lane == bk, tot, row)
        out_ref[...] = row

    return pl.pallas_call(
        body, name=name, grid=(n_heads,),
        in_specs=[pl.BlockSpec((None, BAND, 2 * BAND), lambda h: (h, 0, 0))] * nb
                 + [pl.BlockSpec((BAND, 2 * BAND), lambda h: (0, 0))] * nb,
        out_specs=pl.BlockSpec((None, 1, 128), lambda h: (h, 0, 0)),
        out_shape=jax.ShapeDtypeStruct((n_heads, 1, 128), F32),
        compiler_params=_cparams(("arbitrary",)),
    )(*dsb_list, *idxs)


def _band_mask():
    i = lax.broadcasted_iota(jnp.int32, (BAND, 2 * BAND), 0)
    j = lax.broadcasted_iota(jnp.int32, (BAND, 2 * BAND), 1)
    return (j >= i) & (j <= i + BAND), j


def _rep2(x):
    return jnp.concatenate([x, x], axis=1)


ATTN_TOKENS = 2048
MERGE_ROWS = 256


def _rows(ref, start, n, dil):
    if dil == 1:
        return ref[pl.ds(start, n), :]
    return ref[pl.ds(start, n, stride=dil), :]


def _set_rows(ref, start, n, dil, val):
    if dil == 1:
        ref[pl.ds(start, n), :] = val
    else:
        ref[pl.ds(start, n, stride=dil), :] = val


def _attn_specs(ct, n_heads, chunk_of):
    cur = lambda col0: pl.BlockSpec((ct, HEAD_DIM), lambda h, s: (chunk_of(s), col0 + h))
    prev = lambda col0: pl.BlockSpec((ct, HEAD_DIM), lambda h, s: (jnp.maximum(chunk_of(s) - 1, 0), col0 + h))
    bias = pl.BlockSpec((None, BAND, 2 * BAND), lambda h, s: (h, 0, 0))
    return cur, prev, bias


def _load_keys(kext_ref, vext_ref, base, k_ref, v_ref, kp_ref, vp_ref, r, dil, ct):
    lc = ct // dil
    kext_ref[pl.ds(base, BAND), :] = _rows(kp_ref, ct - BAND * dil + r, BAND, dil).astype(BF16)
    vext_ref[pl.ds(base, BAND), :] = _rows(vp_ref, ct - BAND * dil + r, BAND, dil).astype(BF16)
    kext_ref[pl.ds(base + BAND, lc), :] = _rows(k_ref, r, lc, dil).astype(BF16)
    vext_ref[pl.ds(base + BAND, lc), :] = _rows(v_ref, r, lc, dil).astype(BF16)


ATTN_GROUP = 4


def _two_level(dil):
    if dil > ATTN_GROUP and dil % ATTN_GROUP == 0:
        return ATTN_GROUP, dil // ATTN_GROUP
    return 1, dil


def _slot_rows(ct):
    return max(ct + BAND, ATTN_GROUP * (ct // ATTN_GROUP + BAND))


def _window_mask(band, jcol, a, c):
    if a > 0:
        return band
    return band & jnp.logical_or(jcol >= BAND, c > 0)


def attn_fwd(name, q, kv, biases):
    T, D = q.shape
    n_heads = D // HEAD_DIM
    ct = min(ATTN_TOKENS, T)
    n_chunks = T // ct
    nbr = len(BRANCHES)
    scale = HEAD_DIM ** -0.5
    nt_dims = (((1,), (1,)), ((), ()))
    nn_dims = (((1,), (0,)), ((), ()))

    n_in = 5

    def body(*refs):
        ins = refs[:n_in]
        b_refs = refs[n_in:n_in + nbr]
        o_ref, obf_ref, lse_ref = refs[n_in + nbr:n_in + nbr + 3]
        kext_ref, vext_ref, acc_ref, m_ref, l_ref = refs[n_in + nbr + 3:n_in + nbr + 8]
        tmp_in = refs[n_in + nbr + 8:n_in + nbr + 8 + n_in]
        tmp_out = refs[n_in + nbr + 8 + n_in:]
        c = pl.program_id(1)
        band, jcol = _band_mask()

        def residue(src, dst, slot, r, dil, cte, bias_v):
            q_ref, k_ref, v_ref, kp_ref, vp_ref = src
            lc = cte // dil
            base = slot * (BAND + lc)
            _load_keys(kext_ref, vext_ref, base, k_ref, v_ref, kp_ref, vp_ref, r, dil, cte)
            for a in range(lc // BAND):
                tok = r + a * BAND * dil
                qa = _rows(q_ref, tok, BAND, dil).astype(BF16)
                kw = kext_ref[pl.ds(base + a * BAND, 2 * BAND), :]
                vw = vext_ref[pl.ds(base + a * BAND, 2 * BAND), :]
                s = lax.dot_general(qa, kw, nt_dims, preferred_element_type=F32) * scale + bias_v
                s = jnp.where(_window_mask(band, jcol, a, c), s, NEG_BIG)
                m = jnp.max(s, axis=-1, keepdims=True)
                p = jnp.exp(s - m)
                den = jnp.sum(p, axis=-1, keepdims=True)
                pv = lax.dot_general(p.astype(BF16), vw, nn_dims, preferred_element_type=F32)
                _set_rows(dst[0], tok, BAND, dil, pv)
                _set_rows(dst[1], tok, BAND, dil, jnp.broadcast_to(m, (BAND, HEAD_DIM)))
                _set_rows(dst[2], tok, BAND, dil, jnp.broadcast_to(den, (BAND, HEAD_DIM)))

        for bi, (win, dil) in enumerate(BRANCHES):
            bias_v = b_refs[bi][...]
            dst = (acc_ref.at[bi], m_ref.at[bi], l_ref.at[bi])
            outer, inner = _two_level(dil)
            if outer == 1:
                for r in range(dil):
                    residue(ins, dst, r % ATTN_GROUP, r, dil, ct, bias_v)
            else:
                cte = ct // outer

                def group(r1, carry, bias_v=bias_v, dst=dst, outer=outer, inner=inner, cte=cte):
                    for t_ref, x_ref in zip(tmp_in, ins):
                        t_ref[...] = _rows(x_ref, r1, cte, outer)
                    for r2 in range(inner):
                        residue(tmp_in, tmp_out, r2 % ATTN_GROUP, r2, inner, cte, bias_v)
                    for t_ref, d_ref in zip(tmp_out, dst):
                        _set_rows(d_ref, r1, cte, outer, t_ref[...])
                    return carry

                lax.fori_loop(0, outer, group, 0)

        def merge(i, carry):
            rows = pl.ds(pl.multiple_of(i * MERGE_ROWS, MERGE_ROWS), MERGE_ROWS)
            ms = [m_ref[bi, rows, :] for bi in range(nbr)]
            m = functools.reduce(jnp.maximum, ms)
            ws = [jnp.exp(mb - m) for mb in ms]
            tot = functools.reduce(lambda x, y: x + y, [w * l_ref[bi, rows, :] for bi, w in enumerate(ws)])
            o = functools.reduce(lambda x, y: x + y, [w * acc_ref[bi, rows, :] for bi, w in enumerate(ws)]) / tot
            o_ref[rows, :] = o
            obf_ref[rows, :] = o.astype(BF16)
            lse_ref[rows, :] = m + jnp.log(tot)
            return carry

        lax.fori_loop(0, ct // min(MERGE_ROWS, ct), merge, 0)

    cur, prev, bias = _attn_specs(ct, n_heads, lambda s: s)
    small = (ct // ATTN_GROUP, HEAD_DIM)
    return pl.pallas_call(
        body, name=name, grid=(n_heads, n_chunks),
        in_specs=[cur(0), cur(0), cur(n_heads), prev(0), prev(n_heads)] + [bias] * nbr,
        out_specs=[cur(0)] * 3,
        out_shape=[jax.ShapeDtypeStruct((T, D), F32), jax.ShapeDtypeStruct((T, D), BF16), jax.ShapeDtypeStruct((T, D), F32)],
        scratch_shapes=[pltpu.VMEM((_slot_rows(ct), HEAD_DIM), BF16)] * 2 + [pltpu.VMEM((nbr, ct, HEAD_DIM), F32)] * 3
                       + [pltpu.VMEM(small, F32)] * (n_in + 3),
        compiler_params=_cparams(("arbitrary", "arbitrary")),
    )(q, kv, kv, kv, kv, *biases)


def attn_bwd(name, q, kv, do, o, lse, biases):
    T, D = q.shape
    n_heads = D // HEAD_DIM
    ct = min(ATTN_TOKENS, T)
    n_chunks = T // ct
    nbr = len(BRANCHES)
    scale = HEAD_DIM ** -0.5
    nt_dims = (((1,), (1,)), ((), ()))
    tn_dims = (((0,), (0,)), ((), ()))
    nn_dims = (((1,), (0,)), ((), ()))
    mrows = min(MERGE_ROWS, ct)
    n_src = 8
    n_acc = 5

    def body(q_ref, k_ref, v_ref, do_ref, o_ref, lse_ref, kp_ref, vp_ref, *rest):
        b_refs = rest[:nbr]
        dq_ref, dk_ref, dv_ref = rest[nbr:nbr + 3]
        dsb_refs = rest[nbr + 3:2 * nbr + 3]
        sc = rest[2 * nbr + 3:]
        kext_ref, vext_ref, dkext_ref, dvext_ref, dqa_ref, dka_ref, dva_ref, dsum_ref, ck_ref, cv_ref = sc[:10]
        tmp_in = sc[10:10 + n_src]
        tmp_acc = sc[10 + n_src:10 + n_src + n_acc]
        dsacc_ref = sc[10 + n_src + n_acc]
        step = pl.program_id(1)
        c = n_chunks - 1 - step
        band, jcol = _band_mask()

        @pl.when(step == 0)
        def _():
            ck_ref[...] = jnp.zeros_like(ck_ref)
            cv_ref[...] = jnp.zeros_like(cv_ref)
            for r in dsb_refs:
                r[...] = jnp.zeros_like(r)

        def prep(i, carry):
            rows = pl.ds(pl.multiple_of(i * mrows, mrows), mrows)
            dsum_ref[rows, :] = jnp.broadcast_to(jnp.sum(do_ref[rows, :] * o_ref[rows, :], axis=-1, keepdims=True), (mrows, HEAD_DIM))
            dqa_ref[rows, :] = jnp.zeros((mrows, HEAD_DIM), F32)
            dka_ref[rows, :] = ck_ref[rows, :]
            dva_ref[rows, :] = cv_ref[rows, :]
            ck_ref[rows, :] = jnp.zeros((mrows, HEAD_DIM), F32)
            cv_ref[rows, :] = jnp.zeros((mrows, HEAD_DIM), F32)
            return carry

        lax.fori_loop(0, ct // mrows, prep, 0)

        def residue(src, acc, slot, r, dil, cte, bias_v):
            sq, sk, sv, sdo, slse, sdsum, skp, svp = src
            adq, adk, adv, ack, acv = acc
            lc = cte // dil
            base = slot * (BAND + lc)
            _load_keys(kext_ref, vext_ref, base, sk, sv, skp, svp, r, dil, cte)
            dkext_ref[pl.ds(base, BAND + lc), :] = jnp.zeros((BAND + lc, HEAD_DIM), F32)
            dvext_ref[pl.ds(base, BAND + lc), :] = jnp.zeros((BAND + lc, HEAD_DIM), F32)
            for a in range(lc // BAND):
                tok = r + a * BAND * dil
                qa = _rows(sq, tok, BAND, dil).astype(BF16)
                doa = _rows(sdo, tok, BAND, dil).astype(BF16)
                kw = kext_ref[pl.ds(base + a * BAND, 2 * BAND), :]
                vw = vext_ref[pl.ds(base + a * BAND, 2 * BAND), :]
                s = lax.dot_general(qa, kw, nt_dims, preferred_element_type=F32) * scale + bias_v
                p = jnp.where(_window_mask(band, jcol, a, c), jnp.exp(s - _rep2(_rows(slse, tok, BAND, dil))), 0.0)
                dp = lax.dot_general(doa, vw, nt_dims, preferred_element_type=F32)
                ds = p * (dp - _rep2(_rows(sdsum, tok, BAND, dil)))
                dsacc_ref[slot] += ds
                dsb16 = ds.astype(BF16)
                dqw = lax.dot_general(dsb16, kw, nn_dims, preferred_element_type=F32) * scale
                _set_rows(adq, tok, BAND, dil, _rows(adq, tok, BAND, dil) + dqw)
                dkext_ref[pl.ds(base + a * BAND, 2 * BAND), :] += lax.dot_general(dsb16, qa, tn_dims, preferred_element_type=F32) * scale
                dvext_ref[pl.ds(base + a * BAND, 2 * BAND), :] += lax.dot_general(p.astype(BF16), doa, tn_dims, preferred_element_type=F32)

            _set_rows(adk, r, lc, dil, _rows(adk, r, lc, dil) + dkext_ref[pl.ds(base + BAND, lc), :])
            _set_rows(adv, r, lc, dil, _rows(adv, r, lc, dil) + dvext_ref[pl.ds(base + BAND, lc), :])
            last = cte - BAND * dil + r
            _set_rows(ack, last, BAND, dil, _rows(ack, last, BAND, dil) + dkext_ref[pl.ds(base, BAND), :])
            _set_rows(acv, last, BAND, dil, _rows(acv, last, BAND, dil) + dvext_ref[pl.ds(base, BAND), :])

        full_src = (q_ref, k_ref, v_ref, do_ref, lse_ref, dsum_ref, kp_ref, vp_ref)
        full_acc = (dqa_ref, dka_ref, dva_ref, ck_ref, cv_ref)
        for bi, (win, dil) in enumerate(BRANCHES):
            bias_v = b_refs[bi][...]
            dsacc_ref[...] = jnp.zeros_like(dsacc_ref)
            outer, inner = _two_level(dil)
            if outer == 1:
                for r in range(dil):
                    residue(full_src, full_acc, r % ATTN_GROUP, r, dil, ct, bias_v)
            else:
                cte = ct // outer

                def group(r1, carry, bias_v=bias_v, outer=outer, inner=inner, cte=cte):
                    for t_ref, x_ref in zip(tmp_in, full_src):
                        t_ref[...] = _rows(x_ref, r1, cte, outer)
                    for t_ref in tmp_acc:
                        t_ref[...] = jnp.zeros_like(t_ref)
                    for r2 in range(inner):
                        residue(tmp_in, tmp_acc, r2 % ATTN_GROUP, r2, inner, cte, bias_v)
                    for t_ref, a_ref in zip(tmp_acc, full_acc):
                        _set_rows(a_ref, r1, cte, outer, _rows(a_ref, r1, cte, outer) + t_ref[...])
                    return carry

                lax.fori_loop(0, outer, group, 0)
            dsb_refs[bi][...] += functools.reduce(lambda x, y: x + y, [dsacc_ref[s] for s in range(ATTN_GROUP)])

        dq_ref[...] = dqa_ref[...].astype(BF16)
        dk_ref[...] = dka_ref[...].astype(BF16)
        dv_ref[...] = dva_ref[...].astype(BF16)

    cur, prev, bias = _attn_specs(ct, n_heads, lambda s: n_chunks - 1 - s)
    small = (ct // ATTN_GROUP, HEAD_DIM)
    res = pl.pallas_call(
        body, name=name, grid=(n_heads, n_chunks),
        in_specs=[cur(0), cur(0), cur(n_heads), cur(0), cur(0), cur(0), prev(0), prev(n_heads)] + [bias] * nbr,
        out_specs=[cur(0)] * 3 + [bias] * nbr,
        out_shape=[jax.ShapeDtypeStruct((T, D), BF16)] * 3 + [jax.ShapeDtypeStruct((n_heads, BAND, 2 * BAND), F32)] * nbr,
        scratch_shapes=[pltpu.VMEM((_slot_rows(ct), HEAD_DIM), BF16)] * 2 + [pltpu.VMEM((_slot_rows(ct), HEAD_DIM), F32)] * 2
                       + [pltpu.VMEM((ct, HEAD_DIM), F32)] * 6 + [pltpu.VMEM(small, F32)] * (n_src + n_acc)
                       + [pltpu.VMEM((ATTN_GROUP, BAND, 2 * BAND), F32)],
        compiler_params=_cparams(("arbitrary", "arbitrary")),
    )(q, kv, kv, do, o, lse, kv, kv, *biases)
    return res[0], res[1], res[2], list(res[3:])


def _divisor_tile(n, cap, mult):
    if n <= cap:
        return n
    t = cap - cap % mult
    while n % t:
        t -= mult
    return t


def _tile2(R, C):
    return _divisor_tile(R, 512, 8), _divisor_tile(C, 1024, 128)


def half_cast(name, dw, core):
    S, R, C = dw.shape
    hr = R // 2
    tr, tc = _tile2(hr, C)
    nrb = hr // tr

    def body(c_ref, x_ref, o_ref):
        o_ref[...] = x_ref[...].astype(BF16)

    return pl.pallas_call(
        body, name=name,
        grid_spec=pltpu.PrefetchScalarGridSpec(
            num_scalar_prefetch=1, grid=(S, nrb, C // tc),
            in_specs=[pl.BlockSpec((None, tr, tc), lambda s, i, j, c: (s, (1 - c[0]) * nrb + i, j))],
            out_specs=pl.BlockSpec((None, tr, tc), lambda s, i, j, c: (s, i, j))),
        out_shape=jax.ShapeDtypeStruct((S, hr, C), BF16),
        compiler_params=_cparams(("parallel", "parallel", "parallel")),
    )(core, dw)


def pair_sum(name, dw, recv, core):
    S, R, C = dw.shape
    hr = R // 2
    tr, tc = _tile2(hr, C)
    nrb = hr // tr

    def body(c_ref, x_ref, r_ref, p_ref, pbf_ref):
        p = x_ref[...] + r_ref[...].astype(F32)
        p_ref[...] = p
        pbf_ref[...] = p.astype(BF16)

    out = pl.BlockSpec((None, tr, tc), lambda s, i, j, c: (s, i, j))
    return pl.pallas_call(
        body, name=name,
        grid_spec=pltpu.PrefetchScalarGridSpec(
            num_scalar_prefetch=1, grid=(S, nrb, C // tc),
            in_specs=[pl.BlockSpec((None, tr, tc), lambda s, i, j, c: (s, c[0] * nrb + i, j)), out],
            out_specs=[out, out]),
        out_shape=[jax.ShapeDtypeStruct((S, hr, C), F32), jax.ShapeDtypeStruct((S, hr, C), BF16)],
        compiler_params=_cparams(("parallel", "parallel", "parallel")),
    )(core, dw, recv)


def chip_sum(name, p, recv, chip, core):
    S, hr, C = p.shape
    tr, tc = _tile2(hr, C)
    nrb = hr // tr

    def body(chip_ref, core_ref, p_ref, r_ref, o_ref):
        acc = p_ref[...]
        for t in range(N_CHIPS - 1):
            acc = acc + r_ref[t].astype(F32)
        o_ref[...] = acc

    return pl.pallas_call(
        body, name=name,
        grid_spec=pltpu.PrefetchScalarGridSpec(
            num_scalar_prefetch=2, grid=(nrb, C // tc),
            in_specs=[pl.BlockSpec((None, tr, tc), lambda i, j, s, c: (s[0], i, j)),
                      pl.BlockSpec((N_CHIPS - 1, tr, tc), lambda i, j, s, c: (0, i, j))],
            out_specs=pl.BlockSpec((tr, tc), lambda i, j, s, c: (c[0] * nrb + i, j))),
        out_shape=jax.ShapeDtypeStruct((2 * hr, C), F32),
        compiler_params=_cparams(("parallel", "parallel")),
    )(chip, core, p, recv)


def adamw(name, w, g, m, v):
    R, C = w.shape
    tr, tc = _tile2(R, C)
    c1 = 1.0 - ADAM_B1 ** ADAM_STEP
    c2 = 1.0 - ADAM_B2 ** ADAM_STEP

    def body(w_ref, g_ref, m_ref, v_ref, d_ref, nm_ref, nv_ref):
        gv = g_ref[...]
        nm = ADAM_B1 * m_ref[...] + (1.0 - ADAM_B1) * gv
        nv = ADAM_B2 * v_ref[...] + (1.0 - ADAM_B2) * (gv * gv)
        nm_ref[...] = nm
        nv_ref[...] = nv
        d_ref[...] = -ADAM_LR * ((nm / c1) / (jnp.sqrt(nv / c2) + ADAM_EPS) + ADAM_WD * w_ref[...])

    spec = pl.BlockSpec((tr, tc), lambda i, j: (i, j))
    return pl.pallas_call(
        body, name=name, grid=(R // tr, C // tc), in_specs=[spec] * 4, out_specs=[spec] * 3,
        out_shape=[jax.ShapeDtypeStruct((R, C), F32)] * 3,
        compiler_params=_cparams(("parallel", "parallel")),
    )(w, g, m, v)


def adamw_layers(name, w, g_layers, m, v):
    nl, R, C = w.shape
    tr, tc = _tile2(R, C)
    ni, nj = R // tr, C // tc
    c1 = 1.0 - ADAM_B1 ** ADAM_STEP
    c2 = 1.0 - ADAM_B2 ** ADAM_STEP

    def body(w_ref, *rest):
        g_refs = rest[:nl]
        m_ref, v_ref, g_ref, d_ref, nm_ref, nv_ref = rest[nl:]
        layer = pl.program_id(0)
        gv = g_refs[0][...]
        for l in range(1, nl):
            gv = jnp.where(layer == l, g_refs[l][...], gv)
        nm = ADAM_B1 * m_ref[...] + (1.0 - ADAM_B1) * gv
        nv = ADAM_B2 * v_ref[...] + (1.0 - ADAM_B2) * (gv * gv)
        g_ref[...] = gv
        nm_ref[...] = nm
        nv_ref[...] = nv
        d_ref[...] = -ADAM_LR * ((nm / c1) / (jnp.sqrt(nv / c2) + ADAM_EPS) + ADAM_WD * w_ref[...])

    def g_spec(l):
        def index(layer, i, j):
            return (jnp.where(layer == l, i, jnp.where(layer < l, 0, ni - 1)),
                    jnp.where(layer == l, j, jnp.where(layer < l, 0, nj - 1)))
        return pl.BlockSpec((tr, tc), index)

    spec = pl.BlockSpec((None, tr, tc), lambda layer, i, j: (layer, i, j))
    return pl.pallas_call(
        body, name=name, grid=(nl, ni, nj),
        in_specs=[spec] + [g_spec(l) for l in range(nl)] + [spec] * 2, out_specs=[spec] * 4,
        out_shape=[jax.ShapeDtypeStruct((nl, R, C), F32)] * 4,
        compiler_params=_cparams(("arbitrary", "arbitrary", "arbitrary")),
    )(w, *g_layers, m, v)


def sum_devices(name, gathered):
    n, R, C = gathered.shape

    def body(x_ref, o_ref):
        acc = x_ref[0]
        for d in range(1, n):
            acc = acc + x_ref[d]
        o_ref[...] = acc

    return pl.pallas_call(
        body, name=name, in_specs=[pl.BlockSpec(memory_space=pltpu.VMEM)],
        out_specs=pl.BlockSpec(memory_space=pltpu.VMEM),
        out_shape=jax.ShapeDtypeStruct((R, C), F32),
    )(gathered)


def _place():
    x, y, c = lax.axis_index("x"), lax.axis_index("y"), lax.axis_index("c")
    return x, y, c


def _other_chips(x, y):
    return [(1 - x, y), (x, 1 - y), (1 - x, 1 - y)]


def all_gather8(name, block):
    R, C = block.shape

    def body(x_ref, out_ref, send_sems, recv_sems, local_sem):
        x, y, c = _place()
        me, sibling = (x, y, c), (x, y, 1 - c)
        chips = _other_chips(x, y)

        def rows(px, py, pc):
            return out_ref.at[4 * px + 2 * py + pc]

        def copy(k, blk, to, src=None):
            return pltpu.make_async_remote_copy(
                src_ref=rows(*blk) if src is None else src, dst_ref=rows(*blk),
                send_sem=send_sems.at[k], recv_sem=recv_sems.at[k], device_id=to, device_id_type=MESH)

        mine = pltpu.make_async_copy(x_ref, rows(*me), local_sem)
        mine.start()
        first = [copy(0, me, sibling, src=x_ref)]
        first += [copy(1 + j, me, (*chip, c), src=x_ref) for j, chip in enumerate(chips)]
        for cp in first:
            cp.start()
        passed = [copy(4 + j, (*chip, c), sibling) for j, chip in enumerate(chips)]
        for j, chip in enumerate(chips):
            copy(1 + j, (*chip, c), me).wait_recv()
            passed[j].start()
        copy(0, sibling, me).wait_recv()
        for j, chip in enumerate(chips):
            copy(4 + j, (*chip, 1 - c), me).wait_recv()
        for cp in first + passed:
            cp.wait_send()
        mine.wait()

    return pl.pallas_call(
        body, name=name, out_shape=jax.ShapeDtypeStruct((N_DEV, R, C), block.dtype),
        in_specs=[pl.BlockSpec(memory_space=pltpu.VMEM)], out_specs=pl.BlockSpec(memory_space=pltpu.VMEM),
        scratch_shapes=[pltpu.SemaphoreType.DMA((7,)), pltpu.SemaphoreType.DMA((7,)), pltpu.SemaphoreType.DMA],
    )(block)


_HBM = pl.BlockSpec(memory_space=pltpu.HBM)
_SEM = pl.BlockSpec(memory_space=pltpu.SEMAPHORE)
_DATAFLOW = pltpu.SideEffectType.DATAFLOW_SIDE_EFFECTING


def _in_hbm(a):
    return pltpu.with_memory_space_constraint(a, pltpu.HBM)


def split_start(name, srcs, lands, n_sem, plan):
    ns, nl = len(srcs), len(lands)

    def body(*refs):
        src, land = refs[:ns], refs[ns:ns + nl]
        send_sems, recv_sems = refs[ns + nl], refs[ns + nl + 1]
        token = refs[-1]
        outgoing, _ = plan(src, land, send_sems, recv_sems)
        for cp in outgoing:
            cp.start()
        token[...] = jnp.zeros_like(token)

    bufs = list(srcs) + list(lands)
    res = pl.pallas_call(
        body, name=name,
        out_shape=(pltpu.SemaphoreType.DMA((n_sem,)), pltpu.SemaphoreType.DMA((n_sem,)),
                   *[pltpu.HBM(b.shape, b.dtype) for b in bufs], jax.ShapeDtypeStruct((8, 128), F32)),
        in_specs=[_HBM] * (ns + nl),
        out_specs=(_SEM, _SEM, *[_HBM] * (ns + nl), pl.BlockSpec(memory_space=pltpu.VMEM)),
        input_output_aliases={i: 2 + i for i in range(ns + nl)},
        compiler_params=pltpu.CompilerParams(has_side_effects=_DATAFLOW),
    )(*[_in_hbm(b) for b in bufs])
    return res[0], res[1], list(res[2:2 + ns]), list(res[2 + ns:2 + ns + nl]), res[-1]


def split_wait(name, started, after, plan):
    send_sems, recv_sems, srcs, lands, _ = started
    ns, nl = len(srcs), len(lands)

    def body(*refs):
        src, land = refs[:ns], refs[ns:ns + nl]
        send, recv = refs[ns + nl], refs[ns + nl + 1]
        outgoing, incoming = plan(src, land, send, recv)
        for cp in outgoing:
            cp.wait_send()
        for cp in incoming:
            cp.wait_recv()

    bufs = list(srcs) + list(lands)
    res = pl.pallas_call(
        body, name=name,
        out_shape=tuple(pltpu.HBM(b.shape, b.dtype) for b in bufs),
        in_specs=[_HBM] * (ns + nl) + [_SEM, _SEM, pl.BlockSpec(memory_space=pl.ANY)],
        out_specs=tuple([_HBM] * (ns + nl)),
        input_output_aliases={i: i for i in range(ns + nl)},
        compiler_params=pltpu.CompilerParams(has_side_effects=_DATAFLOW),
    )(*bufs, send_sems, recv_sems, after)
    return list(res[ns:])


def _rcopy(src, dst, send_sems, ks, recv_sems, kr, device):
    return pltpu.make_async_remote_copy(src_ref=src, dst_ref=dst, send_sem=send_sems.at[ks], recv_sem=recv_sems.at[kr],
                                        device_id=device, device_id_type=MESH)


def _half_rows(ref, h):
    hr = ref.shape[0] // 2
    return ref.at[pl.ds(h * hr, hr)]


def _gather_plan(src, land, send_sems, recv_sems):
    x, y, c = _place()
    me_chip = 2 * x + y
    chips = _other_chips(x, y)
    outgoing, incoming = [], []
    for w, buf in enumerate(land):
        mine = _half_rows(buf.at[me_chip], c)
        for t, chip in enumerate(chips):
            slot = 2 * chip[0] + chip[1]
            for cc in range(2):
                outgoing.append(_rcopy(mine, mine, send_sems, 6 * w + 2 * t + cc, recv_sems, 6 * w + 2 * t + c, (*chip, cc)))
                theirs = _half_rows(buf.at[slot], cc)
                incoming.append(_rcopy(theirs, theirs, send_sems, 6 * w + 2 * t + cc, recv_sems, 6 * w + 2 * t + cc, (*chip, cc)))
    return outgoing, incoming


def _gather_half_plan(src, land, send_sems, recv_sems):
    x, y, c = _place()
    me_chip = 2 * x + y
    mine = _half_rows(land[0].at[me_chip], c)
    outgoing, incoming = [], []
    for t, chip in enumerate(_other_chips(x, y)):
        outgoing.append(_rcopy(mine, mine, send_sems, t, recv_sems, t, (*chip, c)))
        theirs = _half_rows(land[0].at[2 * chip[0] + chip[1]], c)
        incoming.append(_rcopy(theirs, theirs, send_sems, t, recv_sems, t, (*chip, c)))
    return outgoing, incoming


def _forward_halves_plan(src, land, send_sems, recv_sems):
    x, y, c = _place()
    outgoing, incoming = [], []
    for t, chip in enumerate(_other_chips(x, y)):
        slot = land[0].at[2 * chip[0] + chip[1]]
        got, missing = _half_rows(slot, c), _half_rows(slot, 1 - c)
        outgoing.append(_rcopy(got, got, send_sems, t, recv_sems, t, (x, y, 1 - c)))
        incoming.append(_rcopy(missing, missing, send_sems, t, recv_sems, t, (x, y, 1 - c)))
    return outgoing, incoming


def _all_to_all_plan(src, land, send_sems, recv_sems):
    x, y, c = _place()
    mine = land[0].at[4 * x + 2 * y + c]
    outgoing, incoming = [], []
    for k in range(1, N_DEV):
        fx, fy, fc = (k >> 2) & 1, (k >> 1) & 1, k & 1
        px, py, pc = (1 - x if fx else x), (1 - y if fy else y), (1 - c if fc else c)
        outgoing.append(_rcopy(mine, mine, send_sems, k - 1, recv_sems, k - 1, (px, py, pc)))
        theirs = land[0].at[4 * px + 2 * py + pc]
        incoming.append(_rcopy(theirs, theirs, send_sems, k - 1, recv_sems, k - 1, (px, py, pc)))
    return outgoing, incoming


def place_block(name, block, slot, n_slots):
    R, C = block.shape

    def body(slot_ref, x_ref, o_ref):
        o_ref[...] = x_ref[...]

    return pl.pallas_call(
        body, name=name,
        grid_spec=pltpu.PrefetchScalarGridSpec(
            num_scalar_prefetch=1, grid=(1,),
            in_specs=[pl.BlockSpec((R, C), lambda i, s: (0, 0))],
            out_specs=pl.BlockSpec((None, R, C), lambda i, s: (s[0], 0, 0))),
        out_shape=jax.ShapeDtypeStruct((n_slots, R, C), block.dtype),
        compiler_params=_cparams(("arbitrary",)),
    )(slot, block)


def _swap_plan(src, land, send_sems, recv_sems):
    x, y, c = _place()
    cp = _rcopy(src[0], land[0], send_sems, 0, recv_sems, 0, (x, y, 1 - c))
    return [cp], [cp]


def _scatter_plan(src, land, send_sems, recv_sems):
    x, y, c = _place()
    cps = [_rcopy(src[0].at[2 * chip[0] + chip[1]], land[0].at[t], send_sems, t, recv_sems, t, (*chip, c))
           for t, chip in enumerate(_other_chips(x, y))]
    return cps, cps


def _share_plan(src, land, send_sems, recv_sems):
    x, y, c = _place()
    mine, theirs = _half_rows(land[0], c), _half_rows(land[0], 1 - c)
    return ([_rcopy(mine, mine, send_sems, 0, recv_sems, 0, (x, y, 1 - c))],
            [_rcopy(theirs, theirs, send_sems, 0, recv_sems, 0, (x, y, 1 - c))])


def place_shard(name, shard, chip, deps=(), layer=None):
    R, C = shard.shape[-2:]
    tr, tc = _tile2(R, C)

    def body(chip_ref, x_ref, *rest):
        rest[-1][...] = x_ref[...].astype(BF16)

    if layer is None:
        src = pl.BlockSpec((tr, tc), lambda i, j, s: (i, j))
    else:
        src = pl.BlockSpec((None, tr, tc), lambda i, j, s: (layer, i, j))
    return pl.pallas_call(
        body, name=name,
        grid_spec=pltpu.PrefetchScalarGridSpec(
            num_scalar_prefetch=1, grid=(R // tr, C // tc),
            in_specs=[src] + [pl.BlockSpec(d.shape, lambda i, j, s: (0, 0)) for d in deps],
            out_specs=pl.BlockSpec((None, tr, tc), lambda i, j, s: (s[0], i, j))),
        out_shape=jax.ShapeDtypeStruct((N_CHIPS, R, C), BF16),
        compiler_params=_cparams(("parallel", "parallel")),
    )(chip, shard, *deps)


class GradExchange:
    SCATTER_TICKS = 2

    def __init__(self, chip1, core, shard, mom, vel):
        self.chip1, self.core, self.shard, self.mom, self.vel = chip1, core, shard, mom, vel
        self.inflight, self.tokens, self.results = [], [], {}

    def take_deps(self):
        deps, self.tokens = self.tokens, []
        return deps

    def _start(self, name, srcs, lands, n_sem, plan):
        started = split_start(name, srcs, lands, n_sem, plan)
        self.tokens.append(started[-1])
        return started

    def add(self, n, dw):
        S, R, C = dw.shape
        to_sibling = half_cast("rs_cast_" + n, dw, self.core)
        started = self._start("rs_swap_start_" + n, [to_sibling], [lax.empty((S, R // 2, C), BF16)], 1, _swap_plan)
        self.inflight.append(dict(n=n, dw=dw, stage=0, started=started, ticks=0))

    def tick(self, after):
        for it in self.inflight:
            n = it["n"]
            if it["stage"] == 0:
                (recv,) = split_wait("rs_swap_wait_" + n, it["started"], after, _swap_plan)
                p, pbf = pair_sum("rs_pair_sum_" + n, it["dw"], recv, self.core)
                S, hr, C = pbf.shape
                it.update(stage=1, p=p, ticks=0,
                          started=self._start("rs_scatter_start_" + n, [pbf], [lax.empty((N_CHIPS - 1, hr, C), BF16)], 3, _scatter_plan))
            elif it["stage"] == 1:
                it["ticks"] += 1
                if it["ticks"] >= self.SCATTER_TICKS:
                    (recv,) = split_wait("rs_scatter_wait_" + n, it["started"], after, _scatter_plan)
                    half = chip_sum("rs_chip_sum_" + n, it["p"], recv, self.chip1, self.core)
                    it.update(stage=2, started=self._start("rs_share_start_" + n, [], [half], 1, _share_plan))
            elif it["stage"] == 2:
                (grad,) = split_wait("rs_share_wait_" + n, it["started"], after, _share_plan)
                if n in self.shard:
                    self.results[n] = (grad,) + tuple(adamw("adamw_" + n, self.shard[n], grad, self.mom[n], self.vel[n]))
                else:
                    self.results[n] = (grad,)
                it["stage"] = 3
        self.inflight = [it for it in self.inflight if it["stage"] < 3]

    def flush(self, after):
        while self.inflight:
            self.tick(after)


def _pack(arrs):
    parts = []
    for a in arrs:
        flat = a.reshape(-1).astype(F32)
        n = flat.shape[0]
        padded = -(-n // 1024) * 1024
        parts.append(jnp.pad(flat, (0, padded - n)).reshape(padded // 128, 128))
    return jnp.concatenate(parts, axis=0)


def _unpack(buf, shapes):
    out, row = [], 0
    for shp in shapes:
        n = int(np.prod(shp))
        rows = -(-n // 1024) * 8
        out.append(buf[row:row + rows].reshape(-1)[:n].reshape(shp))
        row += rows
    return out


def _bias_epi(acc, b):
    return (acc + b,)


def local_step(x, target, W, P, ex, first_deps=(), on_small=None):
    T, D = x.shape
    g = {}
    plain = lambda acc: (acc,)

    (h1,) = mm_nn("pw1_fwd", x, W("pw1", x), "col", _bias_epi, [F32],
                  extras=[(P["pw1_b"], "row")] + [(d, "dep") for d in first_deps])
    u, cpre, s = conv_fwd("conv_fwd", h1, P["dw_w"], P["dw_b"], P["cln_g"], P["cln_b"])
    (mix0,) = mm_nn("pw2_fwd", s, W("pw2", s), "row", _bias_epi, [F32], extras=[(P["pw2_b"], "row")])
    ln = [None] * 4
    gam = [P["ln_mix_g"][0:1], P["ln_mlp_g"][0:1], P["ln_mix_g"][1:2], P["ln_mlp_g"][1:2]]
    bet = [P["ln_mix_b"][0:1], P["ln_mlp_b"][0:1], P["ln_mix_b"][1:2], P["ln_mlp_b"][1:2]]
    ln[0] = ln_fwd("ln0_fwd", mix0, x)(gam[0], bet[0])

    def mlp_fwd(tag, i_ln, n1, n2):
        xhat, rstd, xbf = ln[i_ln]

        def up_epi(acc):
            r = jnp.maximum(acc, 0.0)
            return r * r, r

        hid, relu = mm_nn(tag + "_up", xbf, W(n1, xbf), "col", up_epi, [BF16, BF16])
        (mlp,) = mm_nn(tag + "_down", hid, W(n2, hid), "row", plain, [F32])
        ln[i_ln + 1] = ln_fwd(tag + "_ln", mlp, xhat, gam[i_ln], bet[i_ln])(gam[i_ln + 1], bet[i_ln + 1])
        return hid, relu

    hid0 = mlp_fwd("mlp0", 0, "w1_0", "w2_0")

    x2bf = ln[1][2]
    (kv,) = mm_nn("kv_fwd", x2bf, W("kv", x2bf), "col", plain, [F32])
    (q,) = mm_nn("q_fwd", x2bf, W("wq", kv), "row", plain, [F32])
    biases = [bias_expand("bias_d%d" % d, P["rel_bias"], d) for _, d in BRANCHES]
    assert all(win // d == BAND and min(ATTN_TOKENS, T) % (BAND * d) == 0 for win, d in BRANCHES)
    o, obf, lse = attn_fwd("attn_fwd", q, kv, biases)
    (attn,) = mm_nn("wo_fwd", obf, W("wo", obf), "row", plain, [F32])
    ln[2] = ln_fwd("ln2_fwd", attn, ln[1][0], gam[1], bet[1])(gam[2], bet[2])
    hid1 = mlp_fwd("mlp1", 2, "w1_1", "w2_1")

    dr3, dr3bf, g["ln_mlp_g1"], g["ln_mlp_b1"], _, loss_sum = ln_bwd(
        "ln3_bwd", ln[3][0], ln[3][1], gam[3], target=target, beta=bet[3])

    def dw_step(name, wname, a, cot, axis):
        dw = mm_tn(name, a, cot, W(wname, a).shape, axis, deps=ex.take_deps())
        ex.tick(dw)
        ex.add(wname, dw)

    def dx_step(name, cot, wname, axis, epilogue, out_dtype, extras):
        deps = [(d, "dep") for d in ex.take_deps()]
        (out,) = mm_nt(name, cot, W(wname, cot), axis, epilogue, [out_dtype], extras=list(extras) + deps)
        ex.tick(out)
        return out

    def mlp_bwd(tag, i_ln, n1, n2, hid_relu, dr, drbf):
        xbf = ln[i_ln][2]
        hid, relu = hid_relu
        dw_step(tag + "_dw2", n2, hid, drbf, "row")
        dp = dx_step(tag + "_dhid", drbf, n2, "row", lambda acc, r: (acc * (2.0 * r.astype(F32)),), BF16, [(relu, "tile")])
        dw_step(tag + "_dw1", n1, xbf, dp, "col")
        return dx_step(tag + "_dx", dp, n1, "col", lambda acc, e: (acc + ALPHA * e,), F32, [(dr, "tile")])

    dx3 = mlp_bwd("mlp1", 2, "w1_1", "w2_1", hid1, dr3, dr3bf)
    dr2, dr2bf, g["ln_mix_g1"], g["ln_mix_b1"], _ = ln_bwd("ln2_bwd", ln[2][0], ln[2][1], gam[2], dy=dx3)
    dw_step("wo_dw", "wo", obf, dr2bf, "row")
    do = dx_step("wo_dx", dr2bf, "wo", "row", plain, F32, [])
    dq, dk, dv, dsbs = attn_bwd("attn_bwd", q, kv, do, o, lse, biases)
    g["rel_bias"] = relbias_grad("relbias_grad", dsbs)[:, 0, :REL_BUCKETS].T
    dkv = jnp.concatenate([dk, dv], axis=1)
    dw_step("wq_dw", "wq", x2bf, dq, "row")
    dw_step("kv_dw", "kv", x2bf, dkv, "col")
    dx2a = dx_step("wq_dx", dq, "wq", "row", lambda acc, e: (acc + ALPHA * e,), F32, [(dr2, "tile")])
    dx2 = dx_step("kv_dx", dkv, "kv", "col", lambda acc, e: (acc + e,), F32, [(dx2a, "tile")])

    dr1, dr1bf, g["ln_mlp_g0"], g["ln_mlp_b0"], _ = ln_bwd("ln1_bwd", ln[1][0], ln[1][1], gam[1], dy=dx2)
    dx1 = mlp_bwd("mlp0", 0, "w1_0", "w2_0", hid0, dr1, dr1bf)
    dr0, dr0bf, g["ln_mix_g0"], g["ln_mix_b0"], g["pw2_b"] = ln_bwd("ln0_bwd", ln[0][0], ln[0][1], gam[0], dy=dx1)

    dw_step("pw2_dw", "pw2", s, dr0bf, "row")
    ds = dx_step("pw2_dx", dr0bf, "pw2", "row", plain, F32, [])
    dc, g["cln_g"], g["cln_b"], g["dw_b"] = conv_bwd_ln("conv_bwd_ln", ds, cpre, P["cln_g"], P["cln_b"])
    dh1, g["pw1_b"], g["dw_w"] = conv_bwd_taps("conv_bwd_taps", dc, u, h1, P["dw_w"])
    if on_small is not None:
        on_small(g)
    dw_step("pw1_dw", "pw1", x, dh1, "col")
    dx = dx_step("pw1_dx", dh1, "pw1", "col", lambda acc, e: (acc + ALPHA * e,), F32, [(dr0, "tile")])
    return loss_sum, dx, g


BIG = ("pw1", "pw2", "w1_0", "w2_0", "kv", "wq", "wo", "w1_1", "w2_1")


def kernel(x, conv_pw1_w, conv_pw1_b, conv_dw_w, conv_dw_b, conv_ln_g, conv_ln_b, conv_pw2_w, conv_pw2_b, w_kv, attn_wq, attn_wo, rel_bias, mlp_w1, mlp_w2, ln_mix_g, ln_mix_b, ln_mlp_g, ln_mlp_b, loss_target, m_conv_pw1_w, m_conv_pw1_b, m_conv_dw_w, m_conv_dw_b, m_conv_ln_g, m_conv_ln_b, m_conv_pw2_w, m_conv_pw2_b, m_w_kv, m_attn_wq, m_attn_wo, m_rel_bias, m_mlp_w1, m_mlp_w2, m_ln_mix_g, m_ln_mix_b, m_ln_mlp_g, m_ln_mlp_b, v_conv_pw1_w, v_conv_pw1_b, v_conv_dw_w, v_conv_dw_b, v_conv_ln_g, v_conv_ln_b, v_conv_pw2_w, v_conv_pw2_b, v_w_kv, v_attn_wq, v_attn_wo, v_rel_bias, v_mlp_w1, v_mlp_w2, v_ln_mix_g, v_ln_mix_b, v_ln_mlp_g, v_ln_mlp_b):
    _, T, D = x.shape
    xi, yi, ci = _place()
    chip = 2 * xi + yi
    core = jnp.reshape(ci, (1,)).astype(jnp.int32)
    chip1 = jnp.reshape(chip, (1,)).astype(jnp.int32)

    def two_d(a):
        return a.reshape(a.shape[-2:])

    shard = {"pw1": two_d(conv_pw1_w), "pw2": two_d(conv_pw2_w), "kv": w_kv, "wq": two_d(attn_wq), "wo": two_d(attn_wo)}
    mom = {"pw1": two_d(m_conv_pw1_w), "pw2": two_d(m_conv_pw2_w), "kv": m_w_kv, "wq": two_d(m_attn_wq), "wo": two_d(m_attn_wo)}
    vel = {"pw1": two_d(v_conv_pw1_w), "pw2": two_d(v_conv_pw2_w), "kv": v_w_kv, "wq": two_d(v_attn_wq), "wo": two_d(v_attn_wo)}
    stacked = {"w1_0": (mlp_w1, 0), "w1_1": (mlp_w1, 1), "w2_0": (mlp_w2, 0), "w2_1": (mlp_w2, 1)}

    started = {}
    for n in BIG:
        deps = [started[prev][-1] for prev in list(started)[-1:]]
        src, layer = stacked.get(n, (shard.get(n), None))
        land = place_shard("place_" + n, src, chip1, deps, layer)
        if n == BIG[0]:
            started[n] = split_start("gather_start_" + n, [], [land], 3, _gather_half_plan)
        else:
            started[n] = split_start("gather_start_" + n, [], [land], 6, _gather_plan)
    gathered = {}

    def W(n, after):
        if n not in gathered:
            if n == BIG[0]:
                lands = split_wait("gather_wait_" + n, started[n], after, _gather_half_plan)
                passed = split_start("gather_pass_start_" + n, [], lands, 3, _forward_halves_plan)
                (gathered[n],) = split_wait("gather_pass_wait_" + n, passed, passed[-1], _forward_halves_plan)
            else:
                (gathered[n],) = split_wait("gather_wait_" + n, started[n], after, _gather_plan)
        return gathered[n]

    sharded_small = [conv_pw1_b, conv_dw_w[0], conv_dw_b, conv_ln_g, conv_ln_b, conv_pw2_b]
    sh_shapes = [a.shape for a in sharded_small]
    small_all = all_gather8("gather_small", _pack(sharded_small))
    per_chip = [_unpack(small_all[2 * j], sh_shapes) for j in range(N_CHIPS)]
    full = [jnp.concatenate([per_chip[j][i] for j in range(N_CHIPS)], axis=-1) for i in range(len(sharded_small))]
    P = dict(pw1_b=full[0], dw_w=full[1], dw_b=full[2], cln_g=full[3], cln_b=full[4], pw2_b=full[5],
             rel_bias=rel_bias, ln_mix_g=ln_mix_g, ln_mix_b=ln_mix_b, ln_mlp_g=ln_mlp_g, ln_mlp_b=ln_mlp_b)

    ex = GradExchange(chip1, core, shard, mom, vel)

    small_names = ["pw1_b", "dw_w", "dw_b", "cln_g", "cln_b", "pw2_b", "rel_bias",
                   "ln_mix_g0", "ln_mix_g1", "ln_mix_b0", "ln_mix_b1", "ln_mlp_g0", "ln_mlp_g1", "ln_mlp_b0", "ln_mlp_b1"]
    small = {}

    def on_small(g):
        grads = [g[n] for n in small_names]
        small["shapes"] = [a.shape for a in grads]
        device1 = jnp.reshape(4 * xi + 2 * yi + ci, (1,)).astype(jnp.int32)
        land = place_block("place_small_grads", _pack(grads), device1, N_DEV)
        small["started"] = split_start("small_grads_start", [], [land], N_DEV - 1, _all_to_all_plan)
        ex.tokens.append(small["started"][-1])

    loss_sum, dx, g = local_step(x.reshape(T, D), loss_target.reshape(T, D), W, P, ex,
                                 first_deps=[started[n][-1] for n in BIG], on_small=on_small)
    loss = (0.5 / D) * lax.psum(loss_sum[0, 0], ("x", "y", "c"))
    (all_small,) = split_wait("small_grads_wait", small["started"], dx, _all_to_all_plan)
    summed = sum_devices("small_grad_sum", all_small)
    sg = dict(zip(small_names, _unpack(summed, small["shapes"])))

    def my_cols(a, width):
        return lax.dynamic_slice_in_dim(a, chip * width, width, axis=a.ndim - 1)

    small_g = [my_cols(sg["pw1_b"], conv_pw1_b.shape[-1]),
               my_cols(sg["dw_w"], conv_dw_w.shape[-1])[None],
               my_cols(sg["dw_b"], conv_dw_b.shape[-1]), my_cols(sg["cln_g"], conv_ln_g.shape[-1]),
               my_cols(sg["cln_b"], conv_ln_b.shape[-1]), my_cols(sg["pw2_b"], conv_pw2_b.shape[-1]),
               sg["rel_bias"],
               jnp.concatenate([sg["ln_mix_g0"], sg["ln_mix_g1"]], axis=0),
               jnp.concatenate([sg["ln_mix_b0"], sg["ln_mix_b1"]], axis=0),
               jnp.concatenate([sg["ln_mlp_g0"], sg["ln_mlp_g1"]], axis=0),
               jnp.concatenate([sg["ln_mlp_b0"], sg["ln_mlp_b1"]], axis=0)]
    small_w = [conv_pw1_b, conv_dw_w, conv_dw_b, conv_ln_g, conv_ln_b, conv_pw2_b, rel_bias, ln_mix_g, ln_mix_b, ln_mlp_g, ln_mlp_b]
    small_m = [m_conv_pw1_b, m_conv_dw_w, m_conv_dw_b, m_conv_ln_g, m_conv_ln_b, m_conv_pw2_b, m_rel_bias, m_ln_mix_g, m_ln_mix_b, m_ln_mlp_g, m_ln_mlp_b]
    small_v = [v_conv_pw1_b, v_conv_dw_w, v_conv_dw_b, v_conv_ln_g, v_conv_ln_b, v_conv_pw2_b, v_rel_bias, v_ln_mix_g, v_ln_mix_b, v_ln_mlp_g, v_ln_mlp_b]
    sw_shapes = [a.shape for a in small_w]
    small_g = [a.reshape(s) for a, s in zip(small_g, sw_shapes)]
    upd_small = adamw("adamw_small", _pack(small_w), _pack(small_g), _pack(small_m), _pack(small_v))
    sd, snm, snv = (_unpack(b, sw_shapes) for b in upd_small)

    ex.flush(upd_small[0])
    res_w1 = adamw_layers("adamw_w1", mlp_w1, [ex.results["w1_0"][0], ex.results["w1_1"][0]], m_mlp_w1, v_mlp_w1)
    res_w2 = adamw_layers("adamw_w2", mlp_w2, [ex.results["w2_0"][0], ex.results["w2_1"][0]], m_mlp_w2, v_mlp_w2)

    def big_out(k):
        one = {n: ex.results[n][k] for n in shard}
        return dict(pw1=one["pw1"][None], pw2=one["pw2"][None], kv=one["kv"], wq=one["wq"][None], wo=one["wo"][None],
                    w1=res_w1[k], w2=res_w2[k])

    def ordered(big, small):
        return [big["pw1"], small[0], small[1], small[2], small[3], small[4], big["pw2"], small[5], big["kv"], big["wq"],
                big["wo"], small[6], big["w1"], big["w2"], small[7], small[8], small[9], small[10]]

    grads = ordered(big_out(0), small_g)
    deltas = ordered(big_out(1), sd)
    new_m = ordered(big_out(2), snm)
    new_v = ordered(big_out(3), snv)
    return (loss, dx.reshape(1, T, D), *grads, *deltas, *new_m, *new_v)
```

```python
import functools
import math

import numpy as np
import jax
import jax.numpy as jnp
from jax import lax
from jax.experimental import pallas as pl
from jax.experimental.pallas import tpu as pltpu

F32 = jnp.float32
BF16 = jnp.bfloat16

HEAD_DIM = 128
BAND = 128
BRANCHES = ((128, 1), (512, 4), (2048, 16))
CONV_WIDTH = 31
CONV_HALO = 32
REL_BUCKETS = 32
REL_MAX_DIST = 2048
DEPTH = 2
ALPHA = (2 * DEPTH) ** 0.25
LN_EPS = 1e-5
ADAM_LR, ADAM_B1, ADAM_B2, ADAM_EPS, ADAM_WD, ADAM_STEP = 0.001, 0.9, 0.999, 1e-08, 0.01, 10

N_CHIPS = 4
N_DEV = 8
MESH = pl.DeviceIdType.MESH
VMEM_LIMIT_BYTES = 56 * 1024 * 1024
MM_TM, MM_TN, MM_TK = 1024, 1024, 2048
ROW_TILE = 256
CONV_TILE = 128
NEG_BIG = -1e30


def _cparams(sem):
    return pltpu.CompilerParams(dimension_semantics=sem, vmem_limit_bytes=VMEM_LIMIT_BYTES)


def _sigmoid(x):
    return 1.0 / (1.0 + jnp.exp(-x))


def _wspec(wshape, axis, br, bc, rsel, csel):
    _, R, C = wshape
    if axis == "col":
        if bc > C:
            assert bc % C == 0, (wshape, bc)
            return pl.BlockSpec((bc // C, br, C), lambda *g: (csel(*g), rsel(*g), 0))
        nb = C // bc
        assert nb * bc == C, (wshape, bc)
        return pl.BlockSpec((None, br, bc), lambda *g: (csel(*g) // nb, rsel(*g), csel(*g) % nb))
    if br > R:
        assert br % R == 0, (wshape, br)
        return pl.BlockSpec((br // R, R, bc), lambda *g: (rsel(*g), 0, csel(*g)))
    nb = R // br
    assert nb * br == R, (wshape, br)
    return pl.BlockSpec((None, br, bc), lambda *g: (rsel(*g) // nb, rsel(*g) % nb, csel(*g)))


def _join_shards(b, axis):
    if b.ndim == 2:
        return b
    if axis == "row":
        return b.reshape(b.shape[0] * b.shape[1], b.shape[2])
    return jnp.concatenate([b[s] for s in range(b.shape[0])], axis=1)


def _split_shards(r, shape, axis):
    if len(shape) == 2:
        return r
    if axis == "row":
        return r.reshape(shape)
    return jnp.stack([r[:, s * shape[2]:(s + 1) * shape[2]] for s in range(shape[0])])


def _full_dims(wshape, axis):
    _, R, C = wshape
    return (R, N_CHIPS * C) if axis == "col" else (N_CHIPS * R, C)


def _mm_body(nk, kinds, n_out, dims, epilogue, axis):
    n_extra = len(kinds)

    def body(*refs):
        a_ref, b_ref = refs[0], refs[1]
        extra = [r for r, kind in zip(refs[2:2 + n_extra], kinds) if kind != "dep"]
        outs = refs[2 + n_extra:2 + n_extra + n_out]
        part = lax.dot_general(a_ref[...].astype(BF16), _join_shards(b_ref[...], axis).astype(BF16), (dims, ((), ())),
                               preferred_element_type=F32)

        def write(res):
            for r, o in zip(res, outs):
                o[...] = _split_shards(r, o.shape, axis).astype(o.dtype)

        if nk == 1:
            write(epilogue(part, *[e[...] for e in extra]))
            return
        acc_ref = refs[2 + n_extra + n_out]
        k = pl.program_id(2)

        @pl.when(k == 0)
        def _():
            acc_ref[...] = part

        @pl.when(k > 0)
        def _():
            acc_ref[...] += part

        @pl.when(k == nk - 1)
        def _():
            write(epilogue(acc_ref[...], *[e[...] for e in extra]))
    return body


def _long_tk(a, k_dim):
    tk = min(MM_TK, k_dim)
    if a.dtype == BF16 and k_dim >= 4 * MM_TK:
        tk = 2 * MM_TK
    return tk


def _extra_specs(extras, tm, tn):
    specs = []
    for arr, kind in extras:
        if kind == "tile":
            specs.append(pl.BlockSpec((tm, tn), lambda i, j, k: (i, j)))
        elif kind == "dep":
            specs.append(pl.BlockSpec(arr.shape, lambda i, j, k: (0, 0)))
        else:
            specs.append(pl.BlockSpec((1, tn), lambda i, j, k: (0, j)))
    return specs


def mm_nn(name, a, w, axis, epilogue, out_dtypes, extras=()):
    M, K = a.shape
    Kw, N = _full_dims(w.shape, axis)
    assert K == Kw
    tm, tn, tk = min(MM_TM, M), min(MM_TN, N), _long_tk(a, K)
    nk = K // tk
    in_specs = [pl.BlockSpec((tm, tk), lambda i, j, k: (i, k)),
                _wspec(w.shape, axis, tk, tn, lambda i, j, k: k, lambda i, j, k: j)]
    in_specs += _extra_specs(extras, tm, tn)
    body = _mm_body(nk, [kind for _, kind in extras], len(out_dtypes), ((1,), (0,)), epilogue, axis)
    return pl.pallas_call(
        body, name=name, grid=(M // tm, N // tn, nk), in_specs=in_specs,
        out_specs=[pl.BlockSpec((tm, tn), lambda i, j, k: (i, j)) for _ in out_dtypes],
        out_shape=[jax.ShapeDtypeStruct((M, N), d) for d in out_dtypes],
        scratch_shapes=[pltpu.VMEM((tm, tn), F32)] if nk > 1 else [],
        compiler_params=_cparams(("parallel", "parallel", "arbitrary")),
    )(a, w, *[e for e, _ in extras])


def mm_nt(name, g, w, axis, epilogue, out_dtypes, extras=()):
    M, N = g.shape
    K, Nw = _full_dims(w.shape, axis)
    assert N == Nw
    tm, tn, tk = min(MM_TM, M), min(MM_TN, K), min(MM_TK, N)
    nk = N // tk
    in_specs = [pl.BlockSpec((tm, tk), lambda i, j, k: (i, k)),
                _wspec(w.shape, axis, tn, tk, lambda i, j, k: j, lambda i, j, k: k)]
    in_specs += _extra_specs(extras, tm, tn)
    body = _mm_body(nk, [kind for _, kind in extras], len(out_dtypes), ((1,), (1,)), epilogue, axis)
    return pl.pallas_call(
        body, name=name, grid=(M // tm, K // tn, nk), in_specs=in_specs,
        out_specs=[pl.BlockSpec((tm, tn), lambda i, j, k: (i, j)) for _ in out_dtypes],
        out_shape=[jax.ShapeDtypeStruct((M, K), d) for d in out_dtypes],
        scratch_shapes=[pltpu.VMEM((tm, tn), F32)] if nk > 1 else [],
        compiler_params=_cparams(("parallel", "parallel", "arbitrary")),
    )(g, w, *[e for e, _ in extras])


def mm_tn(name, a, g, wshape, axis, deps=()):
    M, K = a.shape
    Mg, N = g.shape
    assert M == Mg and (K, N) == _full_dims(wshape, axis)
    tm, tn, tk = min(MM_TM, K), min(MM_TN, N), _long_tk(a, M)
    nk = M // tk
    body = _mm_body(nk, ["dep"] * len(deps), 1, ((0,), (0,)), lambda acc: (acc,), axis)
    return pl.pallas_call(
        body, name=name, grid=(K // tm, N // tn, nk),
        in_specs=[pl.BlockSpec((tk, tm), lambda i, j, k: (k, i)),
                  pl.BlockSpec((tk, tn), lambda i, j, k: (k, j))] + _extra_specs([(d, "dep") for d in deps], tm, tn),
        out_specs=[_wspec(wshape, axis, tm, tn, lambda i, j, k: i, lambda i, j, k: j)],
        out_shape=[jax.ShapeDtypeStruct(wshape, F32)],
        scratch_shapes=[pltpu.VMEM((tm, tn), F32)] if nk > 1 else [],
        compiler_params=_cparams(("parallel", "parallel", "arbitrary")),
    )(a, g, *deps)[0]


def _row_spec(tr, width):
    return pl.BlockSpec((tr, width), lambda i: (i, 0))


def _vec_spec(width):
    return pl.BlockSpec((1, width), lambda i: (0, 0))


def _fold8(x):
    r, d = x.shape
    return jnp.sum(x.reshape(r // 8, 8, d), axis=0)


def ln_fwd(name, f, prev, prev_g=None, prev_b=None):
    T, D = f.shape
    tr = min(ROW_TILE, T)
    affine = prev_g is not None

    def body(*refs):
        if affine:
            f_ref, p_ref, pg_ref, pb_ref, g_ref, b_ref, xhat_ref, rstd_ref, xbf_ref = refs
            xprev = p_ref[...] * pg_ref[...] + pb_ref[...]
        else:
            f_ref, p_ref, g_ref, b_ref, xhat_ref, rstd_ref, xbf_ref = refs
            xprev = p_ref[...]
        r = ALPHA * xprev + f_ref[...]
        mu = jnp.mean(r, axis=-1, keepdims=True)
        cen = r - mu
        var = jnp.mean(cen * cen, axis=-1, keepdims=True)
        rstd = lax.rsqrt(var + LN_EPS)
        xhat = cen * rstd
        xhat_ref[...] = xhat
        rstd_ref[...] = rstd
        xbf_ref[...] = (xhat * g_ref[...] + b_ref[...]).astype(BF16)

    def call(g, b):
        ins = [f, prev] + ([prev_g, prev_b] if affine else []) + [g, b]
        specs = [_row_spec(tr, D), _row_spec(tr, D)] + ([_vec_spec(D)] * 2 if affine else []) + [_vec_spec(D)] * 2
        return pl.pallas_call(
            body, name=name, grid=(T // tr,), in_specs=specs,
            out_specs=[_row_spec(tr, D), _row_spec(tr, 1), _row_spec(tr, D)],
            out_shape=[jax.ShapeDtypeStruct((T, D), F32), jax.ShapeDtypeStruct((T, 1), F32),
                       jax.ShapeDtypeStruct((T, D), BF16)],
            compiler_params=_cparams(("parallel",)),
        )(*ins)
    return call


def ln_bwd(name, xhat, rstd, gamma, dy=None, target=None, beta=None):
    T, D = xhat.shape
    tr = min(ROW_TILE, T)
    nt = T // tr
    head = target is not None

    def body(*refs):
        if head:
            xhat_ref, rstd_ref, g_ref, tgt_ref, b_ref = refs[:5]
            outs = refs[5:]
        else:
            xhat_ref, rstd_ref, g_ref, dy_ref = refs[:4]
            outs = refs[4:]
        dr_ref, drbf_ref, dg_ref, db_ref, cs_ref = outs[:5]
        rest = outs[5:]
        if head:
            loss_ref, acc_ref = rest
        else:
            (acc_ref,) = rest
        i = pl.program_id(0)
        xhat_v = xhat_ref[...]
        gam = g_ref[...]
        if head:
            diff = xhat_v * gam + b_ref[...] - tgt_ref[...]
            dyv = diff * (1.0 / D)
        else:
            dyv = dy_ref[...]
        dxh = dyv * gam
        m1 = jnp.mean(dxh, axis=-1, keepdims=True)
        m2 = jnp.mean(dxh * xhat_v, axis=-1, keepdims=True)
        dr = rstd_ref[...] * (dxh - m1 - xhat_v * m2)
        dr_ref[...] = dr
        drbf_ref[...] = dr.astype(BF16)

        @pl.when(i == 0)
        def _():
            acc_ref[...] = jnp.zeros_like(acc_ref)

        acc_ref[0] += _fold8(dyv * xhat_v)
        acc_ref[1] += _fold8(dyv)
        acc_ref[2] += _fold8(dr)
        if head:
            acc_ref[3] += _fold8(diff * diff)

        @pl.when(i == nt - 1)
        def _():
            dg_ref[...] = jnp.sum(acc_ref[0], axis=0, keepdims=True)
            db_ref[...] = jnp.sum(acc_ref[1], axis=0, keepdims=True)
            cs_ref[...] = jnp.sum(acc_ref[2], axis=0, keepdims=True)
            if head:
                loss_ref[...] = jnp.sum(jnp.sum(acc_ref[3], axis=0, keepdims=True), axis=1, keepdims=True)

    ins = [xhat, rstd, gamma] + ([target, beta] if head else [dy])
    specs = [_row_spec(tr, D), _row_spec(tr, 1), _vec_spec(D)] + ([_row_spec(tr, D), _vec_spec(D)] if head else [_row_spec(tr, D)])
    out_specs = [_row_spec(tr, D), _row_spec(tr, D), _vec_spec(D), _vec_spec(D), _vec_spec(D)]
    out_shape = [jax.ShapeDtypeStruct((T, D), F32), jax.ShapeDtypeStruct((T, D), BF16)] + [jax.ShapeDtypeStruct((1, D), F32)] * 3
    if head:
        out_specs.append(pl.BlockSpec((1, 1), lambda i: (0, 0)))
        out_shape.append(jax.ShapeDtypeStruct((1, 1), F32))
    return pl.pallas_call(
        body, name=name, grid=(nt,), in_specs=specs, out_specs=out_specs, out_shape=out_shape,
        scratch_shapes=[pltpu.VMEM((4, 8, D), F32)],
        compiler_params=_cparams(("arbitrary",)),
    )(*ins)


CONV_ROWS, CONV_COLS = 64, 512
CONV_COLS_BWD = 256


def _tap_chunks(tt, D, cols=CONV_COLS):
    for r0 in range(0, tt, min(CONV_ROWS, tt)):
        for c0 in range(0, D, min(cols, D)):
            yield r0, min(CONV_ROWS, tt), c0, min(cols, D)


SUBLANES = 8


def _shifted_copies(ext_ref, sh_ref):
    n = sh_ref.shape[1]
    zero = jnp.minimum(pl.program_id(0), 0)
    for b in range(1, SUBLANES):
        sh_ref[zero + (b - 1)] = ext_ref[pl.ds(b, n), :]


def _rows_at(ext_ref, sh_ref, off, nr, cols):
    a, b = divmod(off, SUBLANES)
    if b == 0:
        return ext_ref[pl.ds(off, nr), cols]
    return sh_ref[b - 1, pl.ds(a * SUBLANES, nr), cols]


def conv_fwd(name, h1, dw, dwb, lng, lnb):
    T, D2 = h1.shape
    D = D2 // 2
    tt = min(CONV_TILE, T)
    hb = tt // CONV_HALO
    KW = dw.shape[0]
    lead = CONV_HALO - (KW - 1)

    def body(a_ref, g_ref, ah_ref, gh_ref, dw_ref, dwb_ref, lng_ref, lnb_ref, u_ref, c_ref, s_ref, ext_ref, sh_ref):
        i = pl.program_id(0)
        u = a_ref[...] * _sigmoid(g_ref[...])
        u_ref[...] = u
        uh = ah_ref[...] * _sigmoid(gh_ref[...])
        ext_ref[pl.ds(0, CONV_HALO), :] = jnp.where(i > 0, uh, 0.0)
        ext_ref[pl.ds(CONV_HALO, tt), :] = u
        _shifted_copies(ext_ref, sh_ref)
        for r0, nr, c0, nc in _tap_chunks(tt, D):
            cols = pl.ds(c0, nc)
            acc = jnp.zeros((nr, nc), F32) + dwb_ref[:, cols]
            for k in range(KW):
                acc = acc + dw_ref[pl.ds(k, 1), cols] * _rows_at(ext_ref, sh_ref, r0 + lead + k, nr, cols)
            c_ref[pl.ds(r0, nr), cols] = acc
        c = c_ref[...]
        mu = jnp.mean(c, axis=-1, keepdims=True)
        cen = c - mu
        var = jnp.mean(cen * cen, axis=-1, keepdims=True)
        n = cen * lax.rsqrt(var + LN_EPS) * lng_ref[...] + lnb_ref[...]
        s_ref[...] = (n * _sigmoid(n)).astype(BF16)

    halo = lambda col: pl.BlockSpec((CONV_HALO, D), lambda i: (jnp.maximum(i * hb - 1, 0), col))
    return pl.pallas_call(
        body, name=name, grid=(T // tt,),
        in_specs=[pl.BlockSpec((tt, D), lambda i: (i, 0)), pl.BlockSpec((tt, D), lambda i: (i, 1)), halo(0), halo(1),
                  pl.BlockSpec((KW, D), lambda i: (0, 0)), _vec_spec(D), _vec_spec(D), _vec_spec(D)],
        out_specs=[_row_spec(tt, D)] * 3,
        out_shape=[jax.ShapeDtypeStruct((T, D), F32), jax.ShapeDtypeStruct((T, D), F32), jax.ShapeDtypeStruct((T, D), BF16)],
        scratch_shapes=[pltpu.VMEM((tt + CONV_HALO, D), F32),
                        pltpu.VMEM((SUBLANES - 1, tt + CONV_HALO - SUBLANES, D), F32)],
        compiler_params=_cparams(("parallel",)),
    )(h1, h1, h1, h1, dw, dwb, lng, lnb)


def conv_bwd_ln(name, ds, c, lng, lnb):
    T, D = c.shape
    tr = min(ROW_TILE, T)
    nt = T // tr

    def body(ds_ref, c_ref, g_ref, b_ref, dc_ref, dg_ref, db_ref, cs_ref, acc_ref):
        i = pl.program_id(0)
        cv = c_ref[...]
        mu = jnp.mean(cv, axis=-1, keepdims=True)
        cen = cv - mu
        var = jnp.mean(cen * cen, axis=-1, keepdims=True)
        rstd = lax.rsqrt(var + LN_EPS)
        chat = cen * rstd
        n = chat * g_ref[...] + b_ref[...]
        sg = _sigmoid(n)
        dn = ds_ref[...] * (sg * (1.0 + n * (1.0 - sg)))
        dxh = dn * g_ref[...]
        m1 = jnp.mean(dxh, axis=-1, keepdims=True)
        m2 = jnp.mean(dxh * chat, axis=-1, keepdims=True)
        dc = rstd * (dxh - m1 - chat * m2)
        dc_ref[...] = dc

        @pl.when(i == 0)
        def _():
            acc_ref[...] = jnp.zeros_like(acc_ref)

        acc_ref[0] += _fold8(dn * chat)
        acc_ref[1] += _fold8(dn)
        acc_ref[2] += _fold8(dc)

        @pl.when(i == nt - 1)
        def _():
            dg_ref[...] = jnp.sum(acc_ref[0], axis=0, keepdims=True)
            db_ref[...] = jnp.sum(acc_ref[1], axis=0, keepdims=True)
            cs_ref[...] = jnp.sum(acc_ref[2], axis=0, keepdims=True)

    return pl.pallas_call(
        body, name=name, grid=(nt,),
        in_specs=[_row_spec(tr, D), _row_spec(tr, D), _vec_spec(D), _vec_spec(D)],
        out_specs=[_row_spec(tr, D), _vec_spec(D), _vec_spec(D), _vec_spec(D)],
        out_shape=[jax.ShapeDtypeStruct((T, D), F32)] + [jax.ShapeDtypeStruct((1, D), F32)] * 3,
        scratch_shapes=[pltpu.VMEM((3, 8, D), F32)],
        compiler_params=_cparams(("arbitrary",)),
    )(ds, c, lng, lnb)


def conv_bwd_taps(name, dc, u, h1, dw):
    T, D = dc.shape
    tt = min(CONV_TILE, T)
    nt = T // tt
    hb = tt // CONV_HALO
    nhb = T // CONV_HALO
    KW = dw.shape[0]
    lead = CONV_HALO - (KW - 1)

    def body(dc_ref, dcn_ref, u_ref, uh_ref, a_ref, g_ref, dw_ref, dh1_ref, db1_ref, ddw_ref,
             edc_ref, eu_ref, du_ref, accw_ref, accb_ref, shdc_ref, shu_ref):
        i = pl.program_id(0)

        @pl.when(i == 0)
        def _():
            accw_ref[...] = jnp.zeros_like(accw_ref)
            accb_ref[...] = jnp.zeros_like(accb_ref)

        edc_ref[pl.ds(0, tt), :] = dc_ref[...]
        edc_ref[pl.ds(tt, CONV_HALO), :] = jnp.where(i < nt - 1, dcn_ref[...], 0.0)
        eu_ref[pl.ds(0, CONV_HALO), :] = jnp.where(i > 0, uh_ref[...], 0.0)
        eu_ref[pl.ds(CONV_HALO, tt), :] = u_ref[...]
        _shifted_copies(edc_ref, shdc_ref)
        _shifted_copies(eu_ref, shu_ref)
        for r0, nr, c0, nc in _tap_chunks(tt, D, CONV_COLS_BWD):
            cols = pl.ds(c0, nc)
            dcv = dc_ref[pl.ds(r0, nr), cols]
            acc = jnp.zeros((nr, nc), F32)
            for k in range(KW):
                acc = acc + dw_ref[pl.ds(k, 1), cols] * _rows_at(edc_ref, shdc_ref, r0 + (KW - 1) - k, nr, cols)
                accw_ref[k, :, cols] += _fold8(dcv * _rows_at(eu_ref, shu_ref, r0 + lead + k, nr, cols))
            du_ref[pl.ds(r0, nr), cols] = acc
        du = du_ref[...]
        sg = _sigmoid(g_ref[...])
        da = du * sg
        dg = du * a_ref[...] * sg * (1.0 - sg)
        dh1_ref[:, pl.ds(0, D)] = da.astype(BF16)
        dh1_ref[:, pl.ds(D, D)] = dg.astype(BF16)
        accb_ref[:, pl.ds(0, D)] += _fold8(da)
        accb_ref[:, pl.ds(D, D)] += _fold8(dg)

        @pl.when(i == nt - 1)
        def _():
            db1_ref[...] = jnp.sum(accb_ref[...], axis=0, keepdims=True)
            ddw_ref[...] = jnp.sum(accw_ref[...], axis=1)

    return pl.pallas_call(
        body, name=name, grid=(nt,),
        in_specs=[_row_spec(tt, D),
                  pl.BlockSpec((CONV_HALO, D), lambda i: (jnp.minimum((i + 1) * hb, nhb - 1), 0)),
                  _row_spec(tt, D),
                  pl.BlockSpec((CONV_HALO, D), lambda i: (jnp.maximum(i * hb - 1, 0), 0)),
                  pl.BlockSpec((tt, D), lambda i: (i, 0)), pl.BlockSpec((tt, D), lambda i: (i, 1)),
                  pl.BlockSpec((KW, D), lambda i: (0, 0))],
        out_specs=[_row_spec(tt, 2 * D), _vec_spec(2 * D), pl.BlockSpec((KW, D), lambda i: (0, 0))],
        out_shape=[jax.ShapeDtypeStruct((T, 2 * D), BF16), jax.ShapeDtypeStruct((1, 2 * D), F32),
                   jax.ShapeDtypeStruct((KW, D), F32)],
        scratch_shapes=[pltpu.VMEM((tt + CONV_HALO, D), F32), pltpu.VMEM((tt + CONV_HALO, D), F32),
                        pltpu.VMEM((tt, D), F32), pltpu.VMEM((KW, 8, D), F32), pltpu.VMEM((8, 2 * D), F32)]
                       + [pltpu.VMEM((SUBLANES - 1, tt + CONV_HALO - SUBLANES, D), F32)] * 2,
        compiler_params=_cparams(("arbitrary",)),
    )(dc, dc, u, u, h1, h1, dw)


def _t5_bucket(dist):
    max_exact = REL_BUCKETS // 2
    large = max_exact + (np.log(np.maximum(dist, 1) / max_exact) / math.log(REL_MAX_DIST / max_exact)
                         * (REL_BUCKETS - max_exact)).astype(np.int32)
    large = np.minimum(large, REL_BUCKETS - 1)
    return np.where(dist < max_exact, dist, large).astype(np.int32)


def _bucket_table(dil):
    i = np.arange(BAND)[:, None]
    j = np.arange(2 * BAND)[None, :]
    delta = i - j + BAND
    return _t5_bucket(np.clip(delta, 0, None) * dil)


def bias_expand(name, rel_bias, dil):
    n_heads = rel_bias.shape[1]
    idx = jnp.asarray(_bucket_table(dil))

    def body(rel_ref, idx_ref, out_ref):
        h = pl.program_id(0)
        idxv = idx_ref[...]
        b = jnp.zeros((BAND, 2 * BAND), F32)
        for bk in range(REL_BUCKETS):
            b = jnp.where(idxv == bk, rel_ref[bk, h], b)
        out_ref[...] = b

    return pl.pallas_call(
        body, name=name, grid=(n_heads,),
        in_specs=[pl.BlockSpec(memory_space=pltpu.SMEM), pl.BlockSpec((BAND, 2 * BAND), lambda h: (0, 0))],
        out_specs=pl.BlockSpec((None, BAND, 2 * BAND), lambda h: (h, 0, 0)),
        out_shape=jax.ShapeDtypeStruct((n_heads, BAND, 2 * BAND), F32),
        compiler_params=_cparams(("arbitrary",)),
    )(rel_bias, idx)


def relbias_grad(name, dsb_list):
    n_heads = dsb_list[0].shape[0]
    idxs = [jnp.asarray(_bucket_table(d)) for _, d in BRANCHES]
    nb = len(BRANCHES)

    def body(*refs):
        ds_refs, idx_refs, out_ref = refs[:nb], refs[nb:2 * nb], refs[2 * nb]
        lane = lax.broadcasted_iota(jnp.int32, (1, 128), 1)
        row = jnp.zeros((1, 128), F32)
        for bk in range(REL_BUCKETS):
            tot = jnp.zeros((1, 1), F32)
            for ds_ref, idx_ref in zip(ds_refs, idx_refs):
                sel = jnp.where(idx_ref[...] == bk, ds_ref[...], 0.0)
                tot = tot + jnp.sum(jnp.sum(sel, axis=0, keepdims=True), axis=1, keepdims=True)
            row = jnp.where(lane == bk, tot, row)
        out_ref[...] = row

    return pl.pallas_call(
        body, name=name, grid=(n_heads,),
        in_specs=[pl.BlockSpec((None, BAND, 2 * BAND), lambda h: (h, 0, 0))] * nb
                 + [pl.BlockSpec((BAND, 2 * BAND), lambda h: (0, 0))] * nb,
        out_specs=pl.BlockSpec((None, 1, 128), lambda h: (h, 0, 0)),
        out_shape=jax.ShapeDtypeStruct((n_heads, 1, 128), F32),
        compiler_params=_cparams(("arbitrary",)),
    )(*dsb_list, *idxs)


def _band_mask():
    i = lax.broadcasted_iota(jnp.int32, (BAND, 2 * BAND), 0)
    j = lax.broadcasted_iota(jnp.int32, (BAND, 2 * BAND), 1)
    return (j >= i) & (j <= i + BAND), j


def _rep2(x):
    return jnp.concatenate([x, x], axis=1)


ATTN_TOKENS = 2048
MERGE_ROWS = 256


def _rows(ref, start, n, dil):
    if dil == 1:
        return ref[pl.ds(start, n), :]
    return ref[pl.ds(start, n, stride=dil), :]


def _set_rows(ref, start, n, dil, val):
    if dil == 1:
        ref[pl.ds(start, n), :] = val
    else:
        ref[pl.ds(start, n, stride=dil), :] = val


def _attn_specs(ct, n_heads, chunk_of):
    cur = lambda col0: pl.BlockSpec((ct, HEAD_DIM), lambda h, s: (chunk_of(s), col0 + h))
    prev = lambda col0: pl.BlockSpec((ct, HEAD_DIM), lambda h, s: (jnp.maximum(chunk_of(s) - 1, 0), col0 + h))
    bias = pl.BlockSpec((None, BAND, 2 * BAND), lambda h, s: (h, 0, 0))
    return cur, prev, bias


def _load_keys(kext_ref, vext_ref, base, k_ref, v_ref, kp_ref, vp_ref, r, dil, ct):
    lc = ct // dil
    kext_ref[pl.ds(base, BAND), :] = _rows(kp_ref, ct - BAND * dil + r, BAND, dil).astype(BF16)
    vext_ref[pl.ds(base, BAND), :] = _rows(vp_ref, ct - BAND * dil + r, BAND, dil).astype(BF16)
    kext_ref[pl.ds(base + BAND, lc), :] = _rows(k_ref, r, lc, dil).astype(BF16)
    vext_ref[pl.ds(base + BAND, lc), :] = _rows(v_ref, r, lc, dil).astype(BF16)


ATTN_GROUP = 4


def _two_level(dil):
    if dil > ATTN_GROUP and dil % ATTN_GROUP == 0:
        return ATTN_GROUP, dil // ATTN_GROUP
    return 1, dil


def _slot_rows(ct):
    return max(ct + BAND, ATTN_GROUP * (ct // ATTN_GROUP + BAND))


def _window_mask(band, jcol, a, c):
    if a > 0:
        return band
    return band & jnp.logical_or(jcol >= BAND, c > 0)


def attn_fwd(name, q, kv, biases):
    T, D = q.shape
    n_heads = D // HEAD_DIM
    ct = min(ATTN_TOKENS, T)
    n_chunks = T // ct
    nbr = len(BRANCHES)
    scale = HEAD_DIM ** -0.5
    nt_dims = (((1,), (1,)), ((), ()))
    nn_dims = (((1,), (0,)), ((), ()))

    n_in = 5

    def body(*refs):
        ins = refs[:n_in]
        b_refs = refs[n_in:n_in + nbr]
        o_ref, obf_ref, lse_ref = refs[n_in + nbr:n_in + nbr + 3]
        kext_ref, vext_ref, acc_ref, m_ref, l_ref = refs[n_in + nbr + 3:n_in + nbr + 8]
        tmp_in = refs[n_in + nbr + 8:n_in + nbr + 8 + n_in]
        tmp_out = refs[n_in + nbr + 8 + n_in:]
        c = pl.program_id(1)
        band, jcol = _band_mask()

        def residue(src, dst, slot, r, dil, cte, bias_v):
            q_ref, k_ref, v_ref, kp_ref, vp_ref = src
            lc = cte // dil
            base = slot * (BAND + lc)
            _load_keys(kext_ref, vext_ref, base, k_ref, v_ref, kp_ref, vp_ref, r, dil, cte)
            for a in range(lc // BAND):
                tok = r + a * BAND * dil
                qa = _rows(q_ref, tok, BAND, dil).astype(BF16)
                kw = kext_ref[pl.ds(base + a * BAND, 2 * BAND), :]
                vw = vext_ref[pl.ds(base + a * BAND, 2 * BAND), :]
                s = lax.dot_general(qa, kw, nt_dims, preferred_element_type=F32) * scale + bias_v
                s = jnp.where(_window_mask(band, jcol, a, c), s, NEG_BIG)
                m = jnp.max(s, axis=-1, keepdims=True)
                p = jnp.exp(s - m)
                den = jnp.sum(p, axis=-1, keepdims=True)
                pv = lax.dot_general(p.astype(BF16), vw, nn_dims, preferred_element_type=F32)
                _set_rows(dst[0], tok, BAND, dil, pv)
                _set_rows(dst[1], tok, BAND, dil, jnp.broadcast_to(m, (BAND, HEAD_DIM)))
                _set_rows(dst[2], tok, BAND, dil, jnp.broadcast_to(den, (BAND, HEAD_DIM)))

        for bi, (win, dil) in enumerate(BRANCHES):
            bias_v = b_refs[bi][...]
            dst = (acc_ref.at[bi], m_ref.at[bi], l_ref.at[bi])
            outer, inner = _two_level(dil)
            if outer == 1:
                for r in range(dil):
                    residue(ins, dst, r % ATTN_GROUP, r, dil, ct, bias_v)
            else:
                cte = ct // outer

                def group(r1, carry, bias_v=bias_v, dst=dst, outer=outer, inner=inner, cte=cte):
                    for t_ref, x_ref in zip(tmp_in, ins):
                        t_ref[...] = _rows(x_ref, r1, cte, outer)
                    for r2 in range(inner):
                        residue(tmp_in, tmp_out, r2 % ATTN_GROUP, r2, inner, cte, bias_v)
                    for t_ref, d_ref in zip(tmp_out, dst):
                        _set_rows(d_ref, r1, cte, outer, t_ref[...])
                    return carry

                lax.fori_loop(0, outer, group, 0)

        def merge(i, carry):
            rows = pl.ds(pl.multiple_of(i * MERGE_ROWS, MERGE_ROWS), MERGE_ROWS)
            ms = [m_ref[bi, rows, :] for bi in range(nbr)]
            m = functools.reduce(jnp.maximum, ms)
            ws = [jnp.exp(mb - m) for mb in ms]
            tot = functools.reduce(lambda x, y: x + y, [w * l_ref[bi, rows, :] for bi, w in enumerate(ws)])
            o = functools.reduce(lambda x, y: x + y, [w * acc_ref[bi, rows, :] for bi, w in enumerate(ws)]) / tot
            o_ref[rows, :] = o
            obf_ref[rows, :] = o.astype(BF16)
            lse_ref[rows, :] = m + jnp.log(tot)
            return carry

        lax.fori_loop(0, ct // min(MERGE_ROWS, ct), merge, 0)

    cur, prev, bias = _attn_specs(ct, n_heads, lambda s: s)
    small = (ct // ATTN_GROUP, HEAD_DIM)
    return pl.pallas_call(
        body, name=name, grid=(n_heads, n_chunks),
        in_specs=[cur(0), cur(0), cur(n_heads), prev(0), prev(n_heads)] + [bias] * nbr,
        out_specs=[cur(0)] * 3,
        out_shape=[jax.ShapeDtypeStruct((T, D), F32), jax.ShapeDtypeStruct((T, D), BF16), jax.ShapeDtypeStruct((T, D), F32)],
        scratch_shapes=[pltpu.VMEM((_slot_rows(ct), HEAD_DIM), BF16)] * 2 + [pltpu.VMEM((nbr, ct, HEAD_DIM), F32)] * 3
                       + [pltpu.VMEM(small, F32)] * (n_in + 3),
        compiler_params=_cparams(("arbitrary", "arbitrary")),
    )(q, kv, kv, kv, kv, *biases)


def attn_bwd(name, q, kv, do, o, lse, biases):
    T, D = q.shape
    n_heads = D // HEAD_DIM
    ct = min(ATTN_TOKENS, T)
    n_chunks = T // ct
    nbr = len(BRANCHES)
    scale = HEAD_DIM ** -0.5
    nt_dims = (((1,), (1,)), ((), ()))
    tn_dims = (((0,), (0,)), ((), ()))
    nn_dims = (((1,), (0,)), ((), ()))
    mrows = min(MERGE_ROWS, ct)
    n_src = 8
    n_acc = 5

    def body(q_ref, k_ref, v_ref, do_ref, o_ref, lse_ref, kp_ref, vp_ref, *rest):
        b_refs = rest[:nbr]
        dq_ref, dk_ref, dv_ref = rest[nbr:nbr + 3]
        dsb_refs = rest[nbr + 3:2 * nbr + 3]
        sc = rest[2 * nbr + 3:]
        kext_ref, vext_ref, dkext_ref, dvext_ref, dqa_ref, dka_ref, dva_ref, dsum_ref, ck_ref, cv_ref = sc[:10]
        tmp_in = sc[10:10 + n_src]
        tmp_acc = sc[10 + n_src:10 + n_src + n_acc]
        dsacc_ref = sc[10 + n_src + n_acc]
        step = pl.program_id(1)
        c = n_chunks - 1 - step
        band, jcol = _band_mask()

        @pl.when(step == 0)
        def _():
            ck_ref[...] = jnp.zeros_like(ck_ref)
            cv_ref[...] = jnp.zeros_like(cv_ref)
            for r in dsb_refs:
                r[...] = jnp.zeros_like(r)

        def prep(i, carry):
            rows = pl.ds(pl.multiple_of(i * mrows, mrows), mrows)
            dsum_ref[rows, :] = jnp.broadcast_to(jnp.sum(do_ref[rows, :] * o_ref[rows, :], axis=-1, keepdims=True), (mrows, HEAD_DIM))
            dqa_ref[rows, :] = jnp.zeros((mrows, HEAD_DIM), F32)
            dka_ref[rows, :] = ck_ref[rows, :]
            dva_ref[rows, :] = cv_ref[rows, :]
            ck_ref[rows, :] = jnp.zeros((mrows, HEAD_DIM), F32)
            cv_ref[rows, :] = jnp.zeros((mrows, HEAD_DIM), F32)
            return carry

        lax.fori_loop(0, ct // mrows, prep, 0)

        def residue(src, acc, slot, r, dil, cte, bias_v):
            sq, sk, sv, sdo, slse, sdsum, skp, svp = src
            adq, adk, adv, ack, acv = acc
            lc = cte // dil
            base = slot * (BAND + lc)
            _load_keys(kext_ref, vext_ref, base, sk, sv, skp, svp, r, dil, cte)
            dkext_ref[pl.ds(base, BAND + lc), :] = jnp.zeros((BAND + lc, HEAD_DIM), F32)
            dvext_ref[pl.ds(base, BAND + lc), :] = jnp.zeros((BAND + lc, HEAD_DIM), F32)
            for a in range(lc // BAND):
                tok = r + a * BAND * dil
                qa = _rows(sq, tok, BAND, dil).astype(BF16)
                doa = _rows(sdo, tok, BAND, dil).astype(BF16)
                kw = kext_ref[pl.ds(base + a * BAND, 2 * BAND), :]
                vw = vext_ref[pl.ds(base + a * BAND, 2 * BAND), :]
                s = lax.dot_general(qa, kw, nt_dims, preferred_element_type=F32) * scale + bias_v
                p = jnp.where(_window_mask(band, jcol, a, c), jnp.exp(s - _rep2(_rows(slse, tok, BAND, dil))), 0.0)
                dp = lax.dot_general(doa, vw, nt_dims, preferred_element_type=F32)
                ds = p * (dp - _rep2(_rows(sdsum, tok, BAND, dil)))
                dsacc_ref[slot] += ds
                dsb16 = ds.astype(BF16)
                dqw = lax.dot_general(dsb16, kw, nn_dims, preferred_element_type=F32) * scale
                _set_rows(adq, tok, BAND, dil, _rows(adq, tok, BAND, dil) + dqw)
                dkext_ref[pl.ds(base + a * BAND, 2 * BAND), :] += lax.dot_general(dsb16, qa, tn_dims, preferred_element_type=F32) * scale
                dvext_ref[pl.ds(base + a * BAND, 2 * BAND), :] += lax.dot_general(p.astype(BF16), doa, tn_dims, preferred_element_type=F32)

            _set_rows(adk, r, lc, dil, _rows(adk, r, lc, dil) + dkext_ref[pl.ds(base + BAND, lc), :])
            _set_rows(adv, r, lc, dil, _rows(adv, r, lc, dil) + dvext_ref[pl.ds(base + BAND, lc), :])
            last = cte - BAND * dil + r
            _set_rows(ack, last, BAND, dil, _rows(ack, last, BAND, dil) + dkext_ref[pl.ds(base, BAND), :])
            _set_rows(acv, last, BAND, dil, _rows(acv, last, BAND, dil) + dvext_ref[pl.ds(base, BAND), :])

        full_src = (q_ref, k_ref, v_ref, do_ref, lse_ref, dsum_ref, kp_ref, vp_ref)
        full_acc = (dqa_ref, dka_ref, dva_ref, ck_ref, cv_ref)
        for bi, (win, dil) in enumerate(BRANCHES):
            bias_v = b_refs[bi][...]
            dsacc_ref[...] = jnp.zeros_like(dsacc_ref)
            outer, inner = _two_level(dil)
            if outer == 1:
                for r in range(dil):
                    residue(full_src, full_acc, r % ATTN_GROUP, r, dil, ct, bias_v)
            else:
                cte = ct // outer

                def group(r1, carry, bias_v=bias_v, outer=outer, inner=inner, cte=cte):
                    for t_ref, x_ref in zip(tmp_in, full_src):
                        t_ref[...] = _rows(x_ref, r1, cte, outer)
                    for t_ref in tmp_acc:
                        t_ref[...] = jnp.zeros_like(t_ref)
                    for r2 in range(inner):
                        residue(tmp_in, tmp_acc, r2 % ATTN_GROUP, r2, inner, cte, bias_v)
                    for t_ref, a_ref in zip(tmp_acc, full_acc):
                        _set_rows(a_ref, r1, cte, outer, _rows(a_ref, r1, cte, outer) + t_ref[...])
                    return carry

                lax.fori_loop(0, outer, group, 0)
            dsb_refs[bi][...] += functools.reduce(lambda x, y: x + y, [dsacc_ref[s] for s in range(ATTN_GROUP)])

        dq_ref[...] = dqa_ref[...].astype(BF16)
        dk_ref[...] = dka_ref[...].astype(BF16)
        dv_ref[...] = dva_ref[...].astype(BF16)

    cur, prev, bias = _attn_specs(ct, n_heads, lambda s: n_chunks - 1 - s)
    small = (ct // ATTN_GROUP, HEAD_DIM)
    res = pl.pallas_call(
        body, name=name, grid=(n_heads, n_chunks),
        in_specs=[cur(0), cur(0), cur(n_heads), cur(0), cur(0), cur(0), prev(0), prev(n_heads)] + [bias] * nbr,
        out_specs=[cur(0)] * 3 + [bias] * nbr,
        out_shape=[jax.ShapeDtypeStruct((T, D), BF16)] * 3 + [jax.ShapeDtypeStruct((n_heads, BAND, 2 * BAND), F32)] * nbr,
        scratch_shapes=[pltpu.VMEM((_slot_rows(ct), HEAD_DIM), BF16)] * 2 + [pltpu.VMEM((_slot_rows(ct), HEAD_DIM), F32)] * 2
                       + [pltpu.VMEM((ct, HEAD_DIM), F32)] * 6 + [pltpu.VMEM(small, F32)] * (n_src + n_acc)
                       + [pltpu.VMEM((ATTN_GROUP, BAND, 2 * BAND), F32)],
        compiler_params=_cparams(("arbitrary", "arbitrary")),
    )(q, kv, kv, do, o, lse, kv, kv, *biases)
    return res[0], res[1], res[2], list(res[3:])


def _divisor_tile(n, cap, mult):
    if n <= cap:
        return n
    t = cap - cap % mult
    while n % t:
        t -= mult
    return t


def _tile2(R, C):
    return _divisor_tile(R, 512, 8), _divisor_tile(C, 1024, 128)


def half_cast(name, dw, core):
    S, R, C = dw.shape
    hr = R // 2
    tr, tc = _tile2(hr, C)
    nrb = hr // tr

    def body(c_ref, x_ref, o_ref):
        o_ref[...] = x_ref[...].astype(BF16)

    return pl.pallas_call(
        body, name=name,
        grid_spec=pltpu.PrefetchScalarGridSpec(
            num_scalar_prefetch=1, grid=(S, nrb, C // tc),
            in_specs=[pl.BlockSpec((None, tr, tc), lambda s, i, j, c: (s, (1 - c[0]) * nrb + i, j))],
            out_specs=pl.BlockSpec((None, tr, tc), lambda s, i, j, c: (s, i, j))),
        out_shape=jax.ShapeDtypeStruct((S, hr, C), BF16),
        compiler_params=_cparams(("parallel", "parallel", "parallel")),
    )(core, dw)


def pair_sum(name, dw, recv, core):
    S, R, C = dw.shape
    hr = R // 2
    tr, tc = _tile2(hr, C)
    nrb = hr // tr

    def body(c_ref, x_ref, r_ref, p_ref, pbf_ref):
        p = x_ref[...] + r_ref[...].astype(F32)
        p_ref[...] = p
        pbf_ref[...] = p.astype(BF16)

    out = pl.BlockSpec((None, tr, tc), lambda s, i, j, c: (s, i, j))
    return pl.pallas_call(
        body, name=name,
        grid_spec=pltpu.PrefetchScalarGridSpec(
            num_scalar_prefetch=1, grid=(S, nrb, C // tc),
            in_specs=[pl.BlockSpec((None, tr, tc), lambda s, i, j, c: (s, c[0] * nrb + i, j)), out],
            out_specs=[out, out]),
        out_shape=[jax.ShapeDtypeStruct((S, hr, C), F32), jax.ShapeDtypeStruct((S, hr, C), BF16)],
        compiler_params=_cparams(("parallel", "parallel", "parallel")),
    )(core, dw, recv)


def chip_sum(name, p, recv, chip, core):
    S, hr, C = p.shape
    tr, tc = _tile2(hr, C)
    nrb = hr // tr

    def body(chip_ref, core_ref, p_ref, r_ref, o_ref):
        acc = p_ref[...]
        for t in range(N_CHIPS - 1):
            acc = acc + r_ref[t].astype(F32)
        o_ref[...] = acc

    return pl.pallas_call(
        body, name=name,
        grid_spec=pltpu.PrefetchScalarGridSpec(
            num_scalar_prefetch=2, grid=(nrb, C // tc),
            in_specs=[pl.BlockSpec((None, tr, tc), lambda i, j, s, c: (s[0], i, j)),
                      pl.BlockSpec((N_CHIPS - 1, tr, tc), lambda i, j, s, c: (0, i, j))],
            out_specs=pl.BlockSpec((tr, tc), lambda i, j, s, c: (c[0] * nrb + i, j))),
        out_shape=jax.ShapeDtypeStruct((2 * hr, C), F32),
        compiler_params=_cparams(("parallel", "parallel")),
    )(chip, core, p, recv)


def adamw(name, w, g, m, v):
    R, C = w.shape
    tr, tc = _tile2(R, C)
    c1 = 1.0 - ADAM_B1 ** ADAM_STEP
    c2 = 1.0 - ADAM_B2 ** ADAM_STEP

    def body(w_ref, g_ref, m_ref, v_ref, d_ref, nm_ref, nv_ref):
        gv = g_ref[...]
        nm = ADAM_B1 * m_ref[...] + (1.0 - ADAM_B1) * gv
        nv = ADAM_B2 * v_ref[...] + (1.0 - ADAM_B2) * (gv * gv)
        nm_ref[...] = nm
        nv_ref[...] = nv
        d_ref[...] = -ADAM_LR * ((nm / c1) / (jnp.sqrt(nv / c2) + ADAM_EPS) + ADAM_WD * w_ref[...])

    spec = pl.BlockSpec((tr, tc), lambda i, j: (i, j))
    return pl.pallas_call(
        body, name=name, grid=(R // tr, C // tc), in_specs=[spec] * 4, out_specs=[spec] * 3,
        out_shape=[jax.ShapeDtypeStruct((R, C), F32)] * 3,
        compiler_params=_cparams(("parallel", "parallel")),
    )(w, g, m, v)


def adamw_layers(name, w, g_layers, m, v):
    nl, R, C = w.shape
    tr, tc = _tile2(R, C)
    ni, nj = R // tr, C // tc
    c1 = 1.0 - ADAM_B1 ** ADAM_STEP
    c2 = 1.0 - ADAM_B2 ** ADAM_STEP

    def body(w_ref, *rest):
        g_refs = rest[:nl]
        m_ref, v_ref, g_ref, d_ref, nm_ref, nv_ref = rest[nl:]
        layer = pl.program_id(0)
        gv = g_refs[0][...]
        for l in range(1, nl):
            gv = jnp.where(layer == l, g_refs[l][...], gv)
        nm = ADAM_B1 * m_ref[...] + (1.0 - ADAM_B1) * gv
        nv = ADAM_B2 * v_ref[...] + (1.0 - ADAM_B2) * (gv * gv)
        g_ref[...] = gv
        nm_ref[...] = nm
        nv_ref[...] = nv
        d_ref[...] = -ADAM_LR * ((nm / c1) / (jnp.sqrt(nv / c2) + ADAM_EPS) + ADAM_WD * w_ref[...])

    def g_spec(l):
        def index(layer, i, j):
            return (jnp.where(layer == l, i, jnp.where(layer < l, 0, ni - 1)),
                    jnp.where(layer == l, j, jnp.where(layer < l, 0, nj - 1)))
        return pl.BlockSpec((tr, tc), index)

    spec = pl.BlockSpec((None, tr, tc), lambda layer, i, j: (layer, i, j))
    return pl.pallas_call(
        body, name=name, grid=(nl, ni, nj),
        in_specs=[spec] + [g_spec(l) for l in range(nl)] + [spec] * 2, out_specs=[spec] * 4,
        out_shape=[jax.ShapeDtypeStruct((nl, R, C), F32)] * 4,
        compiler_params=_cparams(("arbitrary", "arbitrary", "arbitrary")),
    )(w, *g_layers, m, v)


def sum_devices(name, gathered):
    n, R, C = gathered.shape

    def body(x_ref, o_ref):
        acc = x_ref[0]
        for d in range(1, n):
            acc = acc + x_ref[d]
        o_ref[...] = acc

    return pl.pallas_call(
        body, name=name, in_specs=[pl.BlockSpec(memory_space=pltpu.VMEM)],
        out_specs=pl.BlockSpec(memory_space=pltpu.VMEM),
        out_shape=jax.ShapeDtypeStruct((R, C), F32),
    )(gathered)


def _place():
    x, y, c = lax.axis_index("x"), lax.axis_index("y"), lax.axis_index("c")
    return x, y, c


def _other_chips(x, y):
    return [(1 - x, y), (x, 1 - y), (1 - x, 1 - y)]


def all_gather8(name, block):
    R, C = block.shape

    def body(x_ref, out_ref, send_sems, recv_sems, local_sem):
        x, y, c = _place()
        me, sibling = (x, y, c), (x, y, 1 - c)
        chips = _other_chips(x, y)

        def rows(px, py, pc):
            return out_ref.at[4 * px + 2 * py + pc]

        def copy(k, blk, to, src=None):
            return pltpu.make_async_remote_copy(
                src_ref=rows(*blk) if src is None else src, dst_ref=rows(*blk),
                send_sem=send_sems.at[k], recv_sem=recv_sems.at[k], device_id=to, device_id_type=MESH)

        mine = pltpu.make_async_copy(x_ref, rows(*me), local_sem)
        mine.start()
        first = [copy(0, me, sibling, src=x_ref)]
        first += [copy(1 + j, me, (*chip, c), src=x_ref) for j, chip in enumerate(chips)]
        for cp in first:
            cp.start()
        passed = [copy(4 + j, (*chip, c), sibling) for j, chip in enumerate(chips)]
        for j, chip in enumerate(chips):
            copy(1 + j, (*chip, c), me).wait_recv()
            passed[j].start()
        copy(0, sibling, me).wait_recv()
        for j, chip in enumerate(chips):
            copy(4 + j, (*chip, 1 - c), me).wait_recv()
        for cp in first + passed:
            cp.wait_send()
        mine.wait()

    return pl.pallas_call(
        body, name=name, out_shape=jax.ShapeDtypeStruct((N_DEV, R, C), block.dtype),
        in_specs=[pl.BlockSpec(memory_space=pltpu.VMEM)], out_specs=pl.BlockSpec(memory_space=pltpu.VMEM),
        scratch_shapes=[pltpu.SemaphoreType.DMA((7,)), pltpu.SemaphoreType.DMA((7,)), pltpu.SemaphoreType.DMA],
    )(block)


_HBM = pl.BlockSpec(memory_space=pltpu.HBM)
_SEM = pl.BlockSpec(memory_space=pltpu.SEMAPHORE)
_DATAFLOW = pltpu.SideEffectType.DATAFLOW_SIDE_EFFECTING


def _in_hbm(a):
    return pltpu.with_memory_space_constraint(a, pltpu.HBM)


def split_start(name, srcs, lands, n_sem, plan):
    ns, nl = len(srcs), len(lands)

    def body(*refs):
        src, land = refs[:ns], refs[ns:ns + nl]
        send_sems, recv_sems = refs[ns + nl], refs[ns + nl + 1]
        token = refs[-1]
        outgoing, _ = plan(src, land, send_sems, recv_sems)
        for cp in outgoing:
            cp.start()
        token[...] = jnp.zeros_like(token)

    bufs = list(srcs) + list(lands)
    res = pl.pallas_call(
        body, name=name,
        out_shape=(pltpu.SemaphoreType.DMA((n_sem,)), pltpu.SemaphoreType.DMA((n_sem,)),
                   *[pltpu.HBM(b.shape, b.dtype) for b in bufs], jax.ShapeDtypeStruct((8, 128), F32)),
        in_specs=[_HBM] * (ns + nl),
        out_specs=(_SEM, _SEM, *[_HBM] * (ns + nl), pl.BlockSpec(memory_space=pltpu.VMEM)),
        input_output_aliases={i: 2 + i for i in range(ns + nl)},
        compiler_params=pltpu.CompilerParams(has_side_effects=_DATAFLOW),
    )(*[_in_hbm(b) for b in bufs])
    return res[0], res[1], list(res[2:2 + ns]), list(res[2 + ns:2 + ns + nl]), res[-1]


def split_wait(name, started, after, plan):
    send_sems, recv_sems, srcs, lands, _ = started
    ns, nl = len(srcs), len(lands)

    def body(*refs):
        src, land = refs[:ns], refs[ns:ns + nl]
        send, recv = refs[ns + nl], refs[ns + nl + 1]
        outgoing, incoming = plan(src, land, send, recv)
        for cp in outgoing:
            cp.wait_send()
        for cp in incoming:
            cp.wait_recv()

    bufs = list(srcs) + list(lands)
    res = pl.pallas_call(
        body, name=name,
        out_shape=tuple(pltpu.HBM(b.shape, b.dtype) for b in bufs),
        in_specs=[_HBM] * (ns + nl) + [_SEM, _SEM, pl.BlockSpec(memory_space=pl.ANY)],
        out_specs=tuple([_HBM] * (ns + nl)),
        input_output_aliases={i: i for i in range(ns + nl)},
        compiler_params=pltpu.CompilerParams(has_side_effects=_DATAFLOW),
    )(*bufs, send_sems, recv_sems, after)
    return list(res[ns:])


def _rcopy(src, dst, send_sems, ks, recv_sems, kr, device):
    return pltpu.make_async_remote_copy(src_ref=src, dst_ref=dst, send_sem=send_sems.at[ks], recv_sem=recv_sems.at[kr],
                                        device_id=device, device_id_type=MESH)


def _half_rows(ref, h):
    hr = ref.shape[0] // 2
    return ref.at[pl.ds(h * hr, hr)]


def _gather_plan(src, land, send_sems, recv_sems):
    x, y, c = _place()
    me_chip = 2 * x + y
    chips = _other_chips(x, y)
    outgoing, incoming = [], []
    for w, buf in enumerate(land):
        mine = _half_rows(buf.at[me_chip], c)
        for t, chip in enumerate(chips):
            slot = 2 * chip[0] + chip[1]
            for cc in range(2):
                outgoing.append(_rcopy(mine, mine, send_sems, 6 * w + 2 * t + cc, recv_sems, 6 * w + 2 * t + c, (*chip, cc)))
                theirs = _half_rows(buf.at[slot], cc)
                incoming.append(_rcopy(theirs, theirs, send_sems, 6 * w + 2 * t + cc, recv_sems, 6 * w + 2 * t + cc, (*chip, cc)))
    return outgoing, incoming


def _gather_half_plan(src, land, send_sems, recv_sems):
    x, y, c = _place()
    me_chip = 2 * x + y
    mine = _half_rows(land[0].at[me_chip], c)
    outgoing, incoming = [], []
    for t, chip in enumerate(_other_chips(x, y)):
        outgoing.append(_rcopy(mine, mine, send_sems, t, recv_sems, t, (*chip, c)))
        theirs = _half_rows(land[0].at[2 * chip[0] + chip[1]], c)
        incoming.append(_rcopy(theirs, theirs, send_sems, t, recv_sems, t, (*chip, c)))
    return outgoing, incoming


def _forward_halves_plan(src, land, send_sems, recv_sems):
    x, y, c = _place()
    outgoing, incoming = [], []
    for t, chip in enumerate(_other_chips(x, y)):
        slot = land[0].at[2 * chip[0] + chip[1]]
        got, missing = _half_rows(slot, c), _half_rows(slot, 1 - c)
        outgoing.append(_rcopy(got, got, send_sems, t, recv_sems, t, (x, y, 1 - c)))
        incoming.append(_rcopy(missing, missing, send_sems, t, recv_sems, t, (x, y, 1 - c)))
    return outgoing, incoming


def _all_to_all_plan(src, land, send_sems, recv_sems):
    x, y, c = _place()
    mine = land[0].at[4 * x + 2 * y + c]
    outgoing, incoming = [], []
    for k in range(1, N_DEV):
        fx, fy, fc = (k >> 2) & 1, (k >> 1) & 1, k & 1
        px, py, pc = (1 - x if fx else x), (1 - y if fy else y), (1 - c if fc else c)
        outgoing.append(_rcopy(mine, mine, send_sems, k - 1, recv_sems, k - 1, (px, py, pc)))
        theirs = land[0].at[4 * px + 2 * py + pc]
        incoming.append(_rcopy(theirs, theirs, send_sems, k - 1, recv_sems, k - 1, (px, py, pc)))
    return outgoing, incoming


def place_block(name, block, slot, n_slots):
    R, C = block.shape

    def body(slot_ref, x_ref, o_ref):
        o_ref[...] = x_ref[...]

    return pl.pallas_call(
        body, name=name,
        grid_spec=pltpu.PrefetchScalarGridSpec(
            num_scalar_prefetch=1, grid=(1,),
            in_specs=[pl.BlockSpec((R, C), lambda i, s: (0, 0))],
            out_specs=pl.BlockSpec((None, R, C), lambda i, s: (s[0], 0, 0))),
        out_shape=jax.ShapeDtypeStruct((n_slots, R, C), block.dtype),
        compiler_params=_cparams(("arbitrary",)),
    )(slot, block)


def _swap_plan(src, land, send_sems, recv_sems):
    x, y, c = _place()
    cp = _rcopy(src[0], land[0], send_sems, 0, recv_sems, 0, (x, y, 1 - c))
    return [cp], [cp]


def _scatter_plan(src, land, send_sems, recv_sems):
    x, y, c = _place()
    cps = [_rcopy(src[0].at[2 * chip[0] + chip[1]], land[0].at[t], send_sems, t, recv_sems, t, (*chip, c))
           for t, chip in enumerate(_other_chips(x, y))]
    return cps, cps


def _share_plan(src, land, send_sems, recv_sems):
    x, y, c = _place()
    mine, theirs = _half_rows(land[0], c), _half_rows(land[0], 1 - c)
    return ([_rcopy(mine, mine, send_sems, 0, recv_sems, 0, (x, y, 1 - c))],
            [_rcopy(theirs, theirs, send_sems, 0, recv_sems, 0, (x, y, 1 - c))])


def place_shard(name, shard, chip, deps=(), layer=None):
    R, C = shard.shape[-2:]
    tr, tc = _tile2(R, C)

    def body(chip_ref, x_ref, *rest):
        rest[-1][...] = x_ref[...].astype(BF16)

    if layer is None:
        src = pl.BlockSpec((tr, tc), lambda i, j, s: (i, j))
    else:
        src = pl.BlockSpec((None, tr, tc), lambda i, j, s: (layer, i, j))
    return pl.pallas_call(
        body, name=name,
        grid_spec=pltpu.PrefetchScalarGridSpec(
            num_scalar_prefetch=1, grid=(R // tr, C // tc),
            in_specs=[src] + [pl.BlockSpec(d.shape, lambda i, j, s: (0, 0)) for d in deps],
            out_specs=pl.BlockSpec((None, tr, tc), lambda i, j, s: (s[0], i, j))),
        out_shape=jax.ShapeDtypeStruct((N_CHIPS, R, C), BF16),
        compiler_params=_cparams(("parallel", "parallel")),
    )(chip, shard, *deps)


class GradExchange:
    SCATTER_TICKS = 2

    def __init__(self, chip1, core, shard, mom, vel):
        self.chip1, self.core, self.shard, self.mom, self.vel = chip1, core, shard, mom, vel
        self.inflight, self.tokens, self.results = [], [], {}

    def take_deps(self):
        deps, self.tokens = self.tokens, []
        return deps

    def _start(self, name, srcs, lands, n_sem, plan):
        started = split_start(name, srcs, lands, n_sem, plan)
        self.tokens.append(started[-1])
        return started

    def add(self, n, dw):
        S, R, C = dw.shape
        to_sibling = half_cast("rs_cast_" + n, dw, self.core)
        started = self._start("rs_swap_start_" + n, [to_sibling], [lax.empty((S, R // 2, C), BF16)], 1, _swap_plan)
        self.inflight.append(dict(n=n, dw=dw, stage=0, started=started, ticks=0))

    def tick(self, after):
        for it in self.inflight:
            n = it["n"]
            if it["stage"] == 0:
                (recv,) = split_wait("rs_swap_wait_" + n, it["started"], after, _swap_plan)
                p, pbf = pair_sum("rs_pair_sum_" + n, it["dw"], recv, self.core)
                S, hr, C = pbf.shape
                it.update(stage=1, p=p, ticks=0,
                          started=self._start("rs_scatter_start_" + n, [pbf], [lax.empty((N_CHIPS - 1, hr, C), BF16)], 3, _scatter_plan))
            elif it["stage"] == 1:
                it["ticks"] += 1
                if it["ticks"] >= self.SCATTER_TICKS:
                    (recv,) = split_wait("rs_scatter_wait_" + n, it["started"], after, _scatter_plan)
                    half = chip_sum("rs_chip_sum_" + n, it["p"], recv, self.chip1, self.core)
                    it.update(stage=2, started=self._start("rs_share_start_" + n, [], [half], 1, _share_plan))
            elif it["stage"] == 2:
                (grad,) = split_wait("rs_share_wait_" + n, it["started"], after, _share_plan)
                if n in self.shard:
                    self.results[n] = (grad,) + tuple(adamw("adamw_" + n, self.shard[n], grad, self.mom[n], self.vel[n]))
                else:
                    self.results[n] = (grad,)
                it["stage"] = 3
        self.inflight = [it for it in self.inflight if it["stage"] < 3]

    def flush(self, after):
        while self.inflight:
            self.tick(after)


def _pack(arrs):
    parts = []
    for a in arrs:
        flat = a.reshape(-1).astype(F32)
        n = flat.shape[0]
        padded = -(-n // 1024) * 1024
        parts.append(jnp.pad(flat, (0, padded - n)).reshape(padded // 128, 128))
    return jnp.concatenate(parts, axis=0)


def _unpack(buf, shapes):
    out, row = [], 0
    for shp in shapes:
        n = int(np.prod(shp))
        rows = -(-n // 1024) * 8
        out.append(buf[row:row + rows].reshape(-1)[:n].reshape(shp))
        row += rows
    return out


def _bias_epi(acc, b):
    return (acc + b,)


def local_step(x, target, W, P, ex, first_deps=(), on_small=None):
    T, D = x.shape
    g = {}
    plain = lambda acc: (acc,)

    (h1,) = mm_nn("pw1_fwd", x, W("pw1", x), "col", _bias_epi, [F32],
                  extras=[(P["pw1_b"], "row")] + [(d, "dep") for d in first_deps])
    u, cpre, s = conv_fwd("conv_fwd", h1, P["dw_w"], P["dw_b"], P["cln_g"], P["cln_b"])
    (mix0,) = mm_nn("pw2_fwd", s, W("pw2", s), "row", _bias_epi, [F32], extras=[(P["pw2_b"], "row")])
    ln = [None] * 4
    gam = [P["ln_mix_g"][0:1], P["ln_mlp_g"][0:1], P["ln_mix_g"][1:2], P["ln_mlp_g"][1:2]]
    bet = [P["ln_mix_b"][0:1], P["ln_mlp_b"][0:1], P["ln_mix_b"][1:2], P["ln_mlp_b"][1:2]]
    ln[0] = ln_fwd("ln0_fwd", mix0, x)(gam[0], bet[0])

    def mlp_fwd(tag, i_ln, n1, n2):
        xhat, rstd, xbf = ln[i_ln]

        def up_epi(acc):
            r = jnp.maximum(acc, 0.0)
            return r * r, r

        hid, relu = mm_nn(tag + "_up", xbf, W(n1, xbf), "col", up_epi, [BF16, BF16])
        (mlp,) = mm_nn(tag + "_down", hid, W(n2, hid), "row", plain, [F32])
        ln[i_ln + 1] = ln_fwd(tag + "_ln", mlp, xhat, gam[i_ln], bet[i_ln])(gam[i_ln + 1], bet[i_ln + 1])
        return hid, relu

    hid0 = mlp_fwd("mlp0", 0, "w1_0", "w2_0")

    x2bf = ln[1][2]
    (kv,) = mm_nn("kv_fwd", x2bf, W("kv", x2bf), "col", plain, [F32])
    (q,) = mm_nn("q_fwd", x2bf, W("wq", kv), "row", plain, [F32])
    biases = [bias_expand("bias_d%d" % d, P["rel_bias"], d) for _, d in BRANCHES]
    assert all(win // d == BAND and min(ATTN_TOKENS, T) % (BAND * d) == 0 for win, d in BRANCHES)
    o, obf, lse = attn_fwd("attn_fwd", q, kv, biases)
    (attn,) = mm_nn("wo_fwd", obf, W("wo", obf), "row", plain, [F32])
    ln[2] = ln_fwd("ln2_fwd", attn, ln[1][0], gam[1], bet[1])(gam[2], bet[2])
    hid1 = mlp_fwd("mlp1", 2, "w1_1", "w2_1")

    dr3, dr3bf, g["ln_mlp_g1"], g["ln_mlp_b1"], _, loss_sum = ln_bwd(
        "ln3_bwd", ln[3][0], ln[3][1], gam[3], target=target, beta=bet[3])

    def dw_step(name, wname, a, cot, axis):
        dw = mm_tn(name, a, cot, W(wname, a).shape, axis, deps=ex.take_deps())
        ex.tick(dw)
        ex.add(wname, dw)

    def dx_step(name, cot, wname, axis, epilogue, out_dtype, extras):
        deps = [(d, "dep") for d in ex.take_deps()]
        (out,) = mm_nt(name, cot, W(wname, cot), axis, epilogue, [out_dtype], extras=list(extras) + deps)
        ex.tick(out)
        return out

    def mlp_bwd(tag, i_ln, n1, n2, hid_relu, dr, drbf):
        xbf = ln[i_ln][2]
        hid, relu = hid_relu
        dw_step(tag + "_dw2", n2, hid, drbf, "row")
        dp = dx_step(tag + "_dhid", drbf, n2, "row", lambda acc, r: (acc * (2.0 * r.astype(F32)),), BF16, [(relu, "tile")])
        dw_step(tag + "_dw1", n1, xbf, dp, "col")
        return dx_step(tag + "_dx", dp, n1, "col", lambda acc, e: (acc + ALPHA * e,), F32, [(dr, "tile")])

    dx3 = mlp_bwd("mlp1", 2, "w1_1", "w2_1", hid1, dr3, dr3bf)
    dr2, dr2bf, g["ln_mix_g1"], g["ln_mix_b1"], _ = ln_bwd("ln2_bwd", ln[2][0], ln[2][1], gam[2], dy=dx3)
    dw_step("wo_dw", "wo", obf, dr2bf, "row")
    do = dx_step("wo_dx", dr2bf, "wo", "row", plain, F32, [])
    dq, dk, dv, dsbs = attn_bwd("attn_bwd", q, kv, do, o, lse, biases)
    g["rel_bias"] = relbias_grad("relbias_grad", dsbs)[:, 0, :REL_BUCKETS].T
    dkv = jnp.concatenate([dk, dv], axis=1)
    dw_step("wq_dw", "wq", x2bf, dq, "row")
    dw_step("kv_dw", "kv", x2bf, dkv, "col")
    dx2a = dx_step("wq_dx", dq, "wq", "row", lambda acc, e: (acc + ALPHA * e,), F32, [(dr2, "tile")])
    dx2 = dx_step("kv_dx", dkv, "kv", "col", lambda acc, e: (acc + e,), F32, [(dx2a, "tile")])

    dr1, dr1bf, g["ln_mlp_g0"], g["ln_mlp_b0"], _ = ln_bwd("ln1_bwd", ln[1][0], ln[1][1], gam[1], dy=dx2)
    dx1 = mlp_bwd("mlp0", 0, "w1_0", "w2_0", hid0, dr1, dr1bf)
    dr0, dr0bf, g["ln_mix_g0"], g["ln_mix_b0"], g["pw2_b"] = ln_bwd("ln0_bwd", ln[0][0], ln[0][1], gam[0], dy=dx1)

    dw_step("pw2_dw", "pw2", s, dr0bf, "row")
    ds = dx_step("pw2_dx", dr0bf, "pw2", "row", plain, F32, [])
    dc, g["cln_g"], g["cln_b"], g["dw_b"] = conv_bwd_ln("conv_bwd_ln", ds, cpre, P["cln_g"], P["cln_b"])
    dh1, g["pw1_b"], g["dw_w"] = conv_bwd_taps("conv_bwd_taps", dc, u, h1, P["dw_w"])
    if on_small is not None:
        on_small(g)
    dw_step("pw1_dw", "pw1", x, dh1, "col")
    dx = dx_step("pw1_dx", dh1, "pw1", "col", lambda acc, e: (acc + ALPHA * e,), F32, [(dr0, "tile")])
    return loss_sum, dx, g


BIG = ("pw1", "pw2", "w1_0", "w2_0", "kv", "wq", "wo", "w1_1", "w2_1")


def kernel(x, conv_pw1_w, conv_pw1_b, conv_dw_w, conv_dw_b, conv_ln_g, conv_ln_b, conv_pw2_w, conv_pw2_b, w_kv, attn_wq, attn_wo, rel_bias, mlp_w1, mlp_w2, ln_mix_g, ln_mix_b, ln_mlp_g, ln_mlp_b, loss_target, m_conv_pw1_w, m_conv_pw1_b, m_conv_dw_w, m_conv_dw_b, m_conv_ln_g, m_conv_ln_b, m_conv_pw2_w, m_conv_pw2_b, m_w_kv, m_attn_wq, m_attn_wo, m_rel_bias, m_mlp_w1, m_mlp_w2, m_ln_mix_g, m_ln_mix_b, m_ln_mlp_g, m_ln_mlp_b, v_conv_pw1_w, v_conv_pw1_b, v_conv_dw_w, v_conv_dw_b, v_conv_ln_g, v_conv_ln_b, v_conv_pw2_w, v_conv_pw2_b, v_w_kv, v_attn_wq, v_attn_wo, v_rel_bias, v_mlp_w1, v_mlp_w2, v_ln_mix_g, v_ln_mix_b, v_ln_mlp_g, v_ln_mlp_b):
    _, T, D = x.shape
    xi, yi, ci = _place()
    chip = 2 * xi + yi
    core = jnp.reshape(ci, (1,)).astype(jnp.int32)
    chip1 = jnp.reshape(chip, (1,)).astype(jnp.int32)

    def two_d(a):
        return a.reshape(a.shape[-2:])

    shard = {"pw1": two_d(conv_pw1_w), "pw2": two_d(conv_pw2_w), "kv": w_kv, "wq": two_d(attn_wq), "wo": two_d(attn_wo)}
    mom = {"pw1": two_d(m_conv_pw1_w), "pw2": two_d(m_conv_pw2_w), "kv": m_w_kv, "wq": two_d(m_attn_wq), "wo": two_d(m_attn_wo)}
    vel = {"pw1": two_d(v_conv_pw1_w), "pw2": two_d(v_conv_pw2_w), "kv": v_w_kv, "wq": two_d(v_attn_wq), "wo": two_d(v_attn_wo)}
    stacked = {"w1_0": (mlp_w1, 0), "w1_1": (mlp_w1, 1), "w2_0": (mlp_w2, 0), "w2_1": (mlp_w2, 1)}

    started = {}
    for n in BIG:
        deps = [started[prev][-1] for prev in list(started)[-1:]]
        src, layer = stacked.get(n, (shard.get(n), None))
        land = place_shard("place_" + n, src, chip1, deps, layer)
        if n == BIG[0]:
            started[n] = split_start("gather_start_" + n, [], [land], 3, _gather_half_plan)
        else:
            started[n] = split_start("gather_start_" + n, [], [land], 6, _gather_plan)
    gathered = {}

    def W(n, after):
        if n not in gathered:
            if n == BIG[0]:
                lands = split_wait("gather_wait_" + n, started[n], after, _gather_half_plan)
                passed = split_start("gather_pass_start_" + n, [], lands, 3, _forward_halves_plan)
                (gathered[n],) = split_wait("gather_pass_wait_" + n, passed, passed[-1], _forward_halves_plan)
            else:
                (gathered[n],) = split_wait("gather_wait_" + n, started[n], after, _gather_plan)
        return gathered[n]

    sharded_small = [conv_pw1_b, conv_dw_w[0], conv_dw_b, conv_ln_g, conv_ln_b, conv_pw2_b]
    sh_shapes = [a.shape for a in sharded_small]
    small_all = all_gather8("gather_small", _pack(sharded_small))
    per_chip = [_unpack(small_all[2 * j], sh_shapes) for j in range(N_CHIPS)]
    full = [jnp.concatenate([per_chip[j][i] for j in range(N_CHIPS)], axis=-1) for i in range(len(sharded_small))]
    P = dict(pw1_b=full[0], dw_w=full[1], dw_b=full[2], cln_g=full[3], cln_b=full[4], pw2_b=full[5],
             rel_bias=rel_bias, ln_mix_g=ln_mix_g, ln_mix_b=ln_mix_b, ln_mlp_g=ln_mlp_g, ln_mlp_b=ln_mlp_b)

    ex = GradExchange(chip1, core, shard, mom, vel)

    small_names = ["pw1_b", "dw_w", "dw_b", "cln_g", "cln_b", "pw2_b", "rel_bias",
                   "ln_mix_g0", "ln_mix_g1", "ln_mix_b0", "ln_mix_b1", "ln_mlp_g0", "ln_mlp_g1", "ln_mlp_b0", "ln_mlp_b1"]
    small = {}

    def on_small(g):
        grads = [g[n] for n in small_names]
        small["shapes"] = [a.shape for a in grads]
        device1 = jnp.reshape(4 * xi + 2 * yi + ci, (1,)).astype(jnp.int32)
        land = place_block("place_small_grads", _pack(grads), device1, N_DEV)
        small["started"] = split_start("small_grads_start", [], [land], N_DEV - 1, _all_to_all_plan)
        ex.tokens.append(small["started"][-1])

    loss_sum, dx, g = local_step(x.reshape(T, D), loss_target.reshape(T, D), W, P, ex,
                                 first_deps=[started[n][-1] for n in BIG], on_small=on_small)
    loss = (0.5 / D) * lax.psum(loss_sum[0, 0], ("x", "y", "c"))
    (all_small,) = split_wait("small_grads_wait", small["started"], dx, _all_to_all_plan)
    summed = sum_devices("small_grad_sum", all_small)
    sg = dict(zip(small_names, _unpack(summed, small["shapes"])))

    def my_cols(a, width):
        return lax.dynamic_slice_in_dim(a, chip * width, width, axis=a.ndim - 1)

    small_g = [my_cols(sg["pw1_b"], conv_pw1_b.shape[-1]),
               my_cols(sg["dw_w"], conv_dw_w.shape[-1])[None],
               my_cols(sg["dw_b"], conv_dw_b.shape[-1]), my_cols(sg["cln_g"], conv_ln_g.shape[-1]),
               my_cols(sg["cln_b"], conv_ln_b.shape[-1]), my_cols(sg["pw2_b"], conv_pw2_b.shape[-1]),
               sg["rel_bias"],
               jnp.concatenate([sg["ln_mix_g0"], sg["ln_mix_g1"]], axis=0),
               jnp.concatenate([sg["ln_mix_b0"], sg["ln_mix_b1"]], axis=0),
               jnp.concatenate([sg["ln_mlp_g0"], sg["ln_mlp_g1"]], axis=0),
               jnp.concatenate([sg["ln_mlp_b0"], sg["ln_mlp_b1"]], axis=0)]
    small_w = [conv_pw1_b, conv_dw_w, conv_dw_b, conv_ln_g, conv_ln_b, conv_pw2_b, rel_bias, ln_mix_g, ln_mix_b, ln_mlp_g, ln_mlp_b]
    small_m = [m_conv_pw1_b, m_conv_dw_w, m_conv_dw_b, m_conv_ln_g, m_conv_ln_b, m_conv_pw2_b, m_rel_bias, m_ln_mix_g, m_ln_mix_b, m_ln_mlp_g, m_ln_mlp_b]
    small_v = [v_conv_pw1_b, v_conv_dw_w, v_conv_dw_b, v_conv_ln_g, v_conv_ln_b, v_conv_pw2_b, v_rel_bias, v_ln_mix_g, v_ln_mix_b, v_ln_mlp_g, v_ln_mlp_b]
    sw_shapes = [a.shape for a in small_w]
    small_g = [a.reshape(s) for a, s in zip(small_g, sw_shapes)]
    upd_small = adamw("adamw_small", _pack(small_w), _pack(small_g), _pack(small_m), _pack(small_v))
    sd, snm, snv = (_unpack(b, sw_shapes) for b in upd_small)

    res_w1 = adamw_layers("adamw_w1", mlp_w1, [ex.results["w1_0"][0], ex.results["w1_1"][0]], m_mlp_w1, v_mlp_w1)
    res_w2 = adamw_layers("adamw_w2", mlp_w2, [ex.results["w2_0"][0], ex.results["w2_1"][0]], m_mlp_w2, v_mlp_w2)
    ex.flush(res_w2[1])

    def big_out(k):
        one = {n: ex.results[n][k] for n in shard}
        return dict(pw1=one["pw1"][None], pw2=one["pw2"][None], kv=one["kv"], wq=one["wq"][None], wo=one["wo"][None],
                    w1=res_w1[k], w2=res_w2[k])

    def ordered(big, small):
        return [big["pw1"], small[0], small[1], small[2], small[3], small[4], big["pw2"], small[5], big["kv"], big["wq"],
                big["wo"], small[6], big["w1"], big["w2"], small[7], small[8], small[9], small[10]]

    grads = ordered(big_out(0), small_g)
    deltas = ordered(big_out(1), sd)
    new_m = ordered(big_out(2), snm)
    new_v = ordered(big_out(3), snv)
    return (loss, dx.reshape(1, T, D), *grads, *deltas, *new_m, *new_v)
```

```python
import functools
import math

import numpy as np
import jax
import jax.numpy as jnp
from jax import lax
from jax.experimental import pallas as pl
from jax.experimental.pallas import tpu as pltpu

F32 = jnp.float32
BF16 = jnp.bfloat16

HEAD_DIM = 128
BAND = 128
BRANCHES = ((128, 1), (512, 4), (2048, 16))
CONV_WIDTH = 31
CONV_HALO = 32
REL_BUCKETS = 32
REL_MAX_DIST = 2048
DEPTH = 2
ALPHA = (2 * DEPTH) ** 0.25
LN_EPS = 1e-5
ADAM_LR, ADAM_B1, ADAM_B2, ADAM_EPS, ADAM_WD, ADAM_STEP = 0.001, 0.9, 0.999, 1e-08, 0.01, 10

N_CHIPS = 4
N_DEV = 8
MESH = pl.DeviceIdType.MESH
VMEM_LIMIT_BYTES = 56 * 1024 * 1024
MM_TM, MM_TN, MM_TK = 1024, 1024, 2048
ROW_TILE = 256
CONV_TILE = 128
NEG_BIG = -1e30


def _cparams(sem):
    return pltpu.CompilerParams(dimension_semantics=sem, vmem_limit_bytes=VMEM_LIMIT_BYTES)


def _sigmoid(x):
    return 1.0 / (1.0 + jnp.exp(-x))


def _wspec(wshape, axis, br, bc, rsel, csel):
    _, R, C = wshape
    if axis == "col":
        if bc > C:
            assert bc % C == 0, (wshape, bc)
            return pl.BlockSpec((bc // C, br, C), lambda *g: (csel(*g), rsel(*g), 0))
        nb = C // bc
        assert nb * bc == C, (wshape, bc)
        return pl.BlockSpec((None, br, bc), lambda *g: (csel(*g) // nb, rsel(*g), csel(*g) % nb))
    if br > R:
        assert br % R == 0, (wshape, br)
        return pl.BlockSpec((br // R, R, bc), lambda *g: (rsel(*g), 0, csel(*g)))
    nb = R // br
    assert nb * br == R, (wshape, br)
    return pl.BlockSpec((None, br, bc), lambda *g: (rsel(*g) // nb, rsel(*g) % nb, csel(*g)))


def _join_shards(b, axis):
    if b.ndim == 2:
        return b
    if axis == "row":
        return b.reshape(b.shape[0] * b.shape[1], b.shape[2])
    return jnp.concatenate([b[s] for s in range(b.shape[0])], axis=1)


def _split_shards(r, shape, axis):
    if len(shape) == 2:
        return r
    if axis == "row":
        return r.reshape(shape)
    return jnp.stack([r[:, s * shape[2]:(s + 1) * shape[2]] for s in range(shape[0])])


def _full_dims(wshape, axis):
    _, R, C = wshape
    return (R, N_CHIPS * C) if axis == "col" else (N_CHIPS * R, C)


def _mm_body(nk, kinds, n_out, dims, epilogue, axis):
    n_extra = len(kinds)

    def body(*refs):
        a_ref, b_ref = refs[0], refs[1]
        extra = [r for r, kind in zip(refs[2:2 + n_extra], kinds) if kind != "dep"]
        outs = refs[2 + n_extra:2 + n_extra + n_out]
        part = lax.dot_general(a_ref[...].astype(BF16), _join_shards(b_ref[...], axis).astype(BF16), (dims, ((), ())),
                               preferred_element_type=F32)

        def write(res):
            for r, o in zip(res, outs):
                o[...] = _split_shards(r, o.shape, axis).astype(o.dtype)

        if nk == 1:
            write(epilogue(part, *[e[...] for e in extra]))
            return
        acc_ref = refs[2 + n_extra + n_out]
        k = pl.program_id(2)

        @pl.when(k == 0)
        def _():
            acc_ref[...] = part

        @pl.when(k > 0)
        def _():
            acc_ref[...] += part

        @pl.when(k == nk - 1)
        def _():
            write(epilogue(acc_ref[...], *[e[...] for e in extra]))
    return body


def _long_tk(a, k_dim):
    tk = min(MM_TK, k_dim)
    if a.dtype == BF16 and k_dim >= 4 * MM_TK:
        tk = 2 * MM_TK
    return tk


def _extra_specs(extras, tm, tn):
    specs = []
    for arr, kind in extras:
        if kind == "tile":
            specs.append(pl.BlockSpec((tm, tn), lambda i, j, k: (i, j)))
        elif kind == "dep":
            specs.append(pl.BlockSpec(arr.shape, lambda i, j, k: (0, 0)))
        else:
            specs.append(pl.BlockSpec((1, tn), lambda i, j, k: (0, j)))
    return specs


def mm_nn(name, a, w, axis, epilogue, out_dtypes, extras=()):
    M, K = a.shape
    Kw, N = _full_dims(w.shape, axis)
    assert K == Kw
    tm, tn, tk = min(MM_TM, M), min(MM_TN, N), _long_tk(a, K)
    nk = K // tk
    in_specs = [pl.BlockSpec((tm, tk), lambda i, j, k: (i, k)),
                _wspec(w.shape, axis, tk, tn, lambda i, j, k: k, lambda i, j, k: j)]
    in_specs += _extra_specs(extras, tm, tn)
    body = _mm_body(nk, [kind for _, kind in extras], len(out_dtypes), ((1,), (0,)), epilogue, axis)
    return pl.pallas_call(
        body, name=name, grid=(M // tm, N // tn, nk), in_specs=in_specs,
        out_specs=[pl.BlockSpec((tm, tn), lambda i, j, k: (i, j)) for _ in out_dtypes],
        out_shape=[jax.ShapeDtypeStruct((M, N), d) for d in out_dtypes],
        scratch_shapes=[pltpu.VMEM((tm, tn), F32)] if nk > 1 else [],
        compiler_params=_cparams(("parallel", "parallel", "arbitrary")),
    )(a, w, *[e for e, _ in extras])


def mm_nt(name, g, w, axis, epilogue, out_dtypes, extras=()):
    M, N = g.shape
    K, Nw = _full_dims(w.shape, axis)
    assert N == Nw
    tm, tn, tk = min(MM_TM, M), min(MM_TN, K), min(MM_TK, N)
    nk = N // tk
    in_specs = [pl.BlockSpec((tm, tk), lambda i, j, k: (i, k)),
                _wspec(w.shape, axis, tn, tk, lambda i, j, k: j, lambda i, j, k: k)]
    in_specs += _extra_specs(extras, tm, tn)
    body = _mm_body(nk, [kind for _, kind in extras], len(out_dtypes), ((1,), (1,)), epilogue, axis)
    return pl.pallas_call(
        body, name=name, grid=(M // tm, K // tn, nk), in_specs=in_specs,
        out_specs=[pl.BlockSpec((tm, tn), lambda i, j, k: (i, j)) for _ in out_dtypes],
        out_shape=[jax.ShapeDtypeStruct((M, K), d) for d in out_dtypes],
        scratch_shapes=[pltpu.VMEM((tm, tn), F32)] if nk > 1 else [],
        compiler_params=_cparams(("parallel", "parallel", "arbitrary")),
    )(g, w, *[e for e, _ in extras])


def mm_tn(name, a, g, wshape, axis, deps=()):
    M, K = a.shape
    Mg, N = g.shape
    assert M == Mg and (K, N) == _full_dims(wshape, axis)
    tm, tn, tk = min(MM_TM, K), min(MM_TN, N), _long_tk(a, M)
    nk = M // tk
    body = _mm_body(nk, ["dep"] * len(deps), 1, ((0,), (0,)), lambda acc: (acc,), axis)
    return pl.pallas_call(
        body, name=name, grid=(K // tm, N // tn, nk),
        in_specs=[pl.BlockSpec((tk, tm), lambda i, j, k: (k, i)),
                  pl.BlockSpec((tk, tn), lambda i, j, k: (k, j))] + _extra_specs([(d, "dep") for d in deps], tm, tn),
        out_specs=[_wspec(wshape, axis, tm, tn, lambda i, j, k: i, lambda i, j, k: j)],
        out_shape=[jax.ShapeDtypeStruct(wshape, F32)],
        scratch_shapes=[pltpu.VMEM((tm, tn), F32)] if nk > 1 else [],
        compiler_params=_cparams(("parallel", "parallel", "arbitrary")),
    )(a, g, *deps)[0]


def _row_spec(tr, width):
    return pl.BlockSpec((tr, width), lambda i: (i, 0))


def _vec_spec(width):
    return pl.BlockSpec((1, width), lambda i: (0, 0))


def _fold8(x):
    r, d = x.shape
    return jnp.sum(x.reshape(r // 8, 8, d), axis=0)


def ln_fwd(name, f, prev, prev_g=None, prev_b=None):
    T, D = f.shape
    tr = min(ROW_TILE, T)
    affine = prev_g is not None

    def body(*refs):
        if affine:
            f_ref, p_ref, pg_ref, pb_ref, g_ref, b_ref, xhat_ref, rstd_ref, xbf_ref = refs
            xprev = p_ref[...] * pg_ref[...] + pb_ref[...]
        else:
            f_ref, p_ref, g_ref, b_ref, xhat_ref, rstd_ref, xbf_ref = refs
            xprev = p_ref[...]
        r = ALPHA * xprev + f_ref[...]
        mu = jnp.mean(r, axis=-1, keepdims=True)
        cen = r - mu
        var = jnp.mean(cen * cen, axis=-1, keepdims=True)
        rstd = lax.rsqrt(var + LN_EPS)
        xhat = cen * rstd
        xhat_ref[...] = xhat
        rstd_ref[...] = rstd
        xbf_ref[...] = (xhat * g_ref[...] + b_ref[...]).astype(BF16)

    def call(g, b):
        ins = [f, prev] + ([prev_g, prev_b] if affine else []) + [g, b]
        specs = [_row_spec(tr, D), _row_spec(tr, D)] + ([_vec_spec(D)] * 2 if affine else []) + [_vec_spec(D)] * 2
        return pl.pallas_call(
            body, name=name, grid=(T // tr,), in_specs=specs,
            out_specs=[_row_spec(tr, D), _row_spec(tr, 1), _row_spec(tr, D)],
            out_shape=[jax.ShapeDtypeStruct((T, D), F32), jax.ShapeDtypeStruct((T, 1), F32),
                       jax.ShapeDtypeStruct((T, D), BF16)],
            compiler_params=_cparams(("parallel",)),
        )(*ins)
    return call


def ln_bwd(name, xhat, rstd, gamma, dy=None, target=None, beta=None):
    T, D = xhat.shape
    tr = min(ROW_TILE, T)
    nt = T // tr
    head = target is not None

    def body(*refs):
        if head:
            xhat_ref, rstd_ref, g_ref, tgt_ref, b_ref = refs[:5]
            outs = refs[5:]
        else:
            xhat_ref, rstd_ref, g_ref, dy_ref = refs[:4]
            outs = refs[4:]
        dr_ref, drbf_ref, dg_ref, db_ref, cs_ref = outs[:5]
        rest = outs[5:]
        if head:
            loss_ref, acc_ref = rest
        else:
            (acc_ref,) = rest
        i = pl.program_id(0)
        xhat_v = xhat_ref[...]
        gam = g_ref[...]
        if head:
            diff = xhat_v * gam + b_ref[...] - tgt_ref[...]
            dyv = diff * (1.0 / D)
        else:
            dyv = dy_ref[...]
        dxh = dyv * gam
        m1 = jnp.mean(dxh, axis=-1, keepdims=True)
        m2 = jnp.mean(dxh * xhat_v, axis=-1, keepdims=True)
        dr = rstd_ref[...] * (dxh - m1 - xhat_v * m2)
        dr_ref[...] = dr
        drbf_ref[...] = dr.astype(BF16)

        @pl.when(i == 0)
        def _():
            acc_ref[...] = jnp.zeros_like(acc_ref)

        acc_ref[0] += _fold8(dyv * xhat_v)
        acc_ref[1] += _fold8(dyv)
        acc_ref[2] += _fold8(dr)
        if head:
            acc_ref[3] += _fold8(diff * diff)

        @pl.when(i == nt - 1)
        def _():
            dg_ref[...] = jnp.sum(acc_ref[0], axis=0, keepdims=True)
            db_ref[...] = jnp.sum(acc_ref[1], axis=0, keepdims=True)
            cs_ref[...] = jnp.sum(acc_ref[2], axis=0, keepdims=True)
            if head:
                loss_ref[...] = jnp.sum(jnp.sum(acc_ref[3], axis=0, keepdims=True), axis=1, keepdims=True)

    ins = [xhat, rstd, gamma] + ([target, beta] if head else [dy])
    specs = [_row_spec(tr, D), _row_spec(tr, 1), _vec_spec(D)] + ([_row_spec(tr, D), _vec_spec(D)] if head else [_row_spec(tr, D)])
    out_specs = [_row_spec(tr, D), _row_spec(tr, D), _vec_spec(D), _vec_spec(D), _vec_spec(D)]
    out_shape = [jax.ShapeDtypeStruct((T, D), F32), jax.ShapeDtypeStruct((T, D), BF16)] + [jax.ShapeDtypeStruct((1, D), F32)] * 3
    if head:
        out_specs.append(pl.BlockSpec((1, 1), lambda i: (0, 0)))
        out_shape.append(jax.ShapeDtypeStruct((1, 1), F32))
    return pl.pallas_call(
        body, name=name, grid=(nt,), in_specs=specs, out_specs=out_specs, out_shape=out_shape,
        scratch_shapes=[pltpu.VMEM((4, 8, D), F32)],
        compiler_params=_cparams(("arbitrary",)),
    )(*ins)


CONV_ROWS, CONV_COLS = 64, 512
CONV_COLS_BWD = 256


def _tap_chunks(tt, D, cols=CONV_COLS):
    for r0 in range(0, tt, min(CONV_ROWS, tt)):
        for c0 in range(0, D, min(cols, D)):
            yield r0, min(CONV_ROWS, tt), c0, min(cols, D)


SUBLANES = 8


def _shifted_copies(ext_ref, sh_ref):
    n = sh_ref.shape[1]
    zero = jnp.minimum(pl.program_id(0), 0)
    for b in range(1, SUBLANES):
        sh_ref[zero + (b - 1)] = ext_ref[pl.ds(b, n), :]


def _rows_at(ext_ref, sh_ref, off, nr, cols):
    a, b = divmod(off, SUBLANES)
    if b == 0:
        return ext_ref[pl.ds(off, nr), cols]
    return sh_ref[b - 1, pl.ds(a * SUBLANES, nr), cols]


def conv_fwd(name, h1, dw, dwb, lng, lnb):
    T, D2 = h1.shape
    D = D2 // 2
    tt = min(CONV_TILE, T)
    hb = tt // CONV_HALO
    KW = dw.shape[0]
    lead = CONV_HALO - (KW - 1)

    def body(a_ref, g_ref, ah_ref, gh_ref, dw_ref, dwb_ref, lng_ref, lnb_ref, u_ref, c_ref, s_ref, ext_ref, sh_ref):
        i = pl.program_id(0)
        u = a_ref[...] * _sigmoid(g_ref[...])
        u_ref[...] = u
        uh = ah_ref[...] * _sigmoid(gh_ref[...])
        ext_ref[pl.ds(0, CONV_HALO), :] = jnp.where(i > 0, uh, 0.0)
        ext_ref[pl.ds(CONV_HALO, tt), :] = u
        _shifted_copies(ext_ref, sh_ref)
        for r0, nr, c0, nc in _tap_chunks(tt, D):
            cols = pl.ds(c0, nc)
            acc = jnp.zeros((nr, nc), F32) + dwb_ref[:, cols]
            for k in range(KW):
                acc = acc + dw_ref[pl.ds(k, 1), cols] * _rows_at(ext_ref, sh_ref, r0 + lead + k, nr, cols)
            c_ref[pl.ds(r0, nr), cols] = acc
        c = c_ref[...]
        mu = jnp.mean(c, axis=-1, keepdims=True)
        cen = c - mu
        var = jnp.mean(cen * cen, axis=-1, keepdims=True)
        n = cen * lax.rsqrt(var + LN_EPS) * lng_ref[...] + lnb_ref[...]
        s_ref[...] = (n * _sigmoid(n)).astype(BF16)

    halo = lambda col: pl.BlockSpec((CONV_HALO, D), lambda i: (jnp.maximum(i * hb - 1, 0), col))
    return pl.pallas_call(
        body, name=name, grid=(T // tt,),
        in_specs=[pl.BlockSpec((tt, D), lambda i: (i, 0)), pl.BlockSpec((tt, D), lambda i: (i, 1)), halo(0), halo(1),
                  pl.BlockSpec((KW, D), lambda i: (0, 0)), _vec_spec(D), _vec_spec(D), _vec_spec(D)],
        out_specs=[_row_spec(tt, D)] * 3,
        out_shape=[jax.ShapeDtypeStruct((T, D), F32), jax.ShapeDtypeStruct((T, D), F32), jax.ShapeDtypeStruct((T, D), BF16)],
        scratch_shapes=[pltpu.VMEM((tt + CONV_HALO, D), F32),
                        pltpu.VMEM((SUBLANES - 1, tt + CONV_HALO - SUBLANES, D), F32)],
        compiler_params=_cparams(("parallel",)),
    )(h1, h1, h1, h1, dw, dwb, lng, lnb)


def conv_bwd_ln(name, ds, c, lng, lnb):
    T, D = c.shape
    tr = min(ROW_TILE, T)
    nt = T // tr

    def body(ds_ref, c_ref, g_ref, b_ref, dc_ref, dg_ref, db_ref, cs_ref, acc_ref):
        i = pl.program_id(0)
        cv = c_ref[...]
        mu = jnp.mean(cv, axis=-1, keepdims=True)
        cen = cv - mu
        var = jnp.mean(cen * cen, axis=-1, keepdims=True)
        rstd = lax.rsqrt(var + LN_EPS)
        chat = cen * rstd
        n = chat * g_ref[...] + b_ref[...]
        sg = _sigmoid(n)
        dn = ds_ref[...] * (sg * (1.0 + n * (1.0 - sg)))
        dxh = dn * g_ref[...]
        m1 = jnp.mean(dxh, axis=-1, keepdims=True)
        m2 = jnp.mean(dxh * chat, axis=-1, keepdims=True)
        dc = rstd * (dxh - m1 - chat * m2)
        dc_ref[...] = dc

        @pl.when(i == 0)
        def _():
            acc_ref[...] = jnp.zeros_like(acc_ref)

        acc_ref[0] += _fold8(dn * chat)
        acc_ref[1] += _fold8(dn)
        acc_ref[2] += _fold8(dc)

        @pl.when(i == nt - 1)
        def _():
            dg_ref[...] = jnp.sum(acc_ref[0], axis=0, keepdims=True)
            db_ref[...] = jnp.sum(acc_ref[1], axis=0, keepdims=True)
            cs_ref[...] = jnp.sum(acc_ref[2], axis=0, keepdims=True)

    return pl.pallas_call(
        body, name=name, grid=(nt,),
        in_specs=[_row_spec(tr, D), _row_spec(tr, D), _vec_spec(D), _vec_spec(D)],
        out_specs=[_row_spec(tr, D), _vec_spec(D), _vec_spec(D), _vec_spec(D)],
        out_shape=[jax.ShapeDtypeStruct((T, D), F32)] + [jax.ShapeDtypeStruct((1, D), F32)] * 3,
        scratch_shapes=[pltpu.VMEM((3, 8, D), F32)],
        compiler_params=_cparams(("arbitrary",)),
    )(ds, c, lng, lnb)


def conv_bwd_taps(name, dc, u, h1, dw):
    T, D = dc.shape
    tt = min(CONV_TILE, T)
    nt = T // tt
    hb = tt // CONV_HALO
    nhb = T // CONV_HALO
    KW = dw.shape[0]
    lead = CONV_HALO - (KW - 1)

    def body(dc_ref, dcn_ref, u_ref, uh_ref, a_ref, g_ref, dw_ref, dh1_ref, db1_ref, ddw_ref,
             edc_ref, eu_ref, du_ref, accw_ref, accb_ref, shdc_ref, shu_ref):
        i = pl.program_id(0)

        @pl.when(i == 0)
        def _():
            accw_ref[...] = jnp.zeros_like(accw_ref)
            accb_ref[...] = jnp.zeros_like(accb_ref)

        edc_ref[pl.ds(0, tt), :] = dc_ref[...]
        edc_ref[pl.ds(tt, CONV_HALO), :] = jnp.where(i < nt - 1, dcn_ref[...], 0.0)
        eu_ref[pl.ds(0, CONV_HALO), :] = jnp.where(i > 0, uh_ref[...], 0.0)
        eu_ref[pl.ds(CONV_HALO, tt), :] = u_ref[...]
        _shifted_copies(edc_ref, shdc_ref)
        _shifted_copies(eu_ref, shu_ref)
        for r0, nr, c0, nc in _tap_chunks(tt, D, CONV_COLS_BWD):
            cols = pl.ds(c0, nc)
            dcv = dc_ref[pl.ds(r0, nr), cols]
            acc = jnp.zeros((nr, nc), F32)
            for k in range(KW):
                acc = acc + dw_ref[pl.ds(k, 1), cols] * _rows_at(edc_ref, shdc_ref, r0 + (KW - 1) - k, nr, cols)
                accw_ref[k, :, cols] += _fold8(dcv * _rows_at(eu_ref, shu_ref, r0 + lead + k, nr, cols))
            du_ref[pl.ds(r0, nr), cols] = acc
        du = du_ref[...]
        sg = _sigmoid(g_ref[...])
        da = du * sg
        dg = du * a_ref[...] * sg * (1.0 - sg)
        dh1_ref[:, pl.ds(0, D)] = da.astype(BF16)
        dh1_ref[:, pl.ds(D, D)] = dg.astype(BF16)
        accb_ref[:, pl.ds(0, D)] += _fold8(da)
        accb_ref[:, pl.ds(D, D)] += _fold8(dg)

        @pl.when(i == nt - 1)
        def _():
            db1_ref[...] = jnp.sum(accb_ref[...], axis=0, keepdims=True)
            ddw_ref[...] = jnp.sum(accw_ref[...], axis=1)

    return pl.pallas_call(
        body, name=name, grid=(nt,),
        in_specs=[_row_spec(tt, D),
                  pl.BlockSpec((CONV_HALO, D), lambda i: (jnp.minimum((i + 1) * hb, nhb - 1), 0)),
                  _row_spec(tt, D),
                  pl.BlockSpec((CONV_HALO, D), lambda i: (jnp.maximum(i * hb - 1, 0), 0)),
                  pl.BlockSpec((tt, D), lambda i: (i, 0)), pl.BlockSpec((tt, D), lambda i: (i, 1)),
                  pl.BlockSpec((KW, D), lambda i: (0, 0))],
        out_specs=[_row_spec(tt, 2 * D), _vec_spec(2 * D), pl.BlockSpec((KW, D), lambda i: (0, 0))],
        out_shape=[jax.ShapeDtypeStruct((T, 2 * D), BF16), jax.ShapeDtypeStruct((1, 2 * D), F32),
                   jax.ShapeDtypeStruct((KW, D), F32)],
        scratch_shapes=[pltpu.VMEM((tt + CONV_HALO, D), F32), pltpu.VMEM((tt + CONV_HALO, D), F32),
                        pltpu.VMEM((tt, D), F32), pltpu.VMEM((KW, 8, D), F32), pltpu.VMEM((8, 2 * D), F32)]
                       + [pltpu.VMEM((SUBLANES - 1, tt + CONV_HALO - SUBLANES, D), F32)] * 2,
        compiler_params=_cparams(("arbitrary",)),
    )(dc, dc, u, u, h1, h1, dw)


def _t5_bucket(dist):
    max_exact = REL_BUCKETS // 2
    large = max_exact + (np.log(np.maximum(dist, 1) / max_exact) / math.log(REL_MAX_DIST / max_exact)
                         * (REL_BUCKETS - max_exact)).astype(np.int32)
    large = np.minimum(large, REL_BUCKETS - 1)
    return np.where(dist < max_exact, dist, large).astype(np.int32)


def _bucket_table(dil):
    i = np.arange(BAND)[:, None]
    j = np.arange(2 * BAND)[None, :]
    delta = i - j + BAND
    return _t5_bucket(np.clip(delta, 0, None) * dil)


def bias_expand(name, rel_bias, dil):
    n_heads = rel_bias.shape[1]
    idx = jnp.asarray(_bucket_table(dil))

    def body(rel_ref, idx_ref, out_ref):
        h = pl.program_id(0)
        idxv = idx_ref[...]
        b = jnp.zeros((BAND, 2 * BAND), F32)
        for bk in range(REL_BUCKETS):
            b = jnp.where(idxv == bk, rel_ref[bk, h], b)
        out_ref[...] = b

    return pl.pallas_call(
        body, name=name, grid=(n_heads,),
        in_specs=[pl.BlockSpec(memory_space=pltpu.SMEM), pl.BlockSpec((BAND, 2 * BAND), lambda h: (0, 0))],
        out_specs=pl.BlockSpec((None, BAND, 2 * BAND), lambda h: (h, 0, 0)),
        out_shape=jax.ShapeDtypeStruct((n_heads, BAND, 2 * BAND), F32),
        compiler_params=_cparams(("arbitrary",)),
    )(rel_bias, idx)


def relbias_grad(name, dsb_list):
    n_heads = dsb_list[0].shape[0]
    idxs = [jnp.asarray(_bucket_table(d)) for _, d in BRANCHES]
    nb = len(BRANCHES)

    def body(*refs):
        ds_refs, idx_refs, out_ref = refs[:nb], refs[nb:2 * nb], refs[2 * nb]
        lane = lax.broadcasted_iota(jnp.int32, (1, 128), 1)
        row = jnp.zeros((1, 128), F32)
        for bk in range(REL_BUCKETS):
            tot = jnp.zeros((1, 1), F32)
            for ds_ref, idx_ref in zip(ds_refs, idx_refs):
                sel = jnp.where(idx_ref[...] == bk, ds_ref[...], 0.0)
                tot = tot + jnp.sum(jnp.sum(sel, axis=0, keepdims=True), axis=1, keepdims=True)
            row = jnp.where(lane == bk, tot, row)
        out_ref[...] = row

    return pl.pallas_call(
        body, name=name, grid=(n_heads,),
        in_specs=[pl.BlockSpec((None, BAND, 2 * BAND), lambda h: (h, 0, 0))] * nb
                 + [pl.BlockSpec((BAND, 2 * BAND), lambda h: (0, 0))] * nb,
        out_specs=pl.BlockSpec((None, 1, 128), lambda h: (h, 0, 0)),
        out_shape=jax.ShapeDtypeStruct((n_heads, 1, 128), F32),
        compiler_params=_cparams(("arbitrary",)),
    )(*dsb_list, *idxs)


def _band_mask():
    i = lax.broadcasted_iota(jnp.int32, (BAND, 2 * BAND), 0)
    j = lax.broadcasted_iota(jnp.int32, (BAND, 2 * BAND), 1)
    return (j >= i) & (j <= i + BAND), j


def _rep2(x):
    return jnp.concatenate([x, x], axis=1)


ATTN_TOKENS = 2048
MERGE_ROWS = 256


def _rows(ref, start, n, dil):
    if dil == 1:
        return ref[pl.ds(start, n), :]
    return ref[pl.ds(start, n, stride=dil), :]


def _set_rows(ref, start, n, dil, val):
    if dil == 1:
        ref[pl.ds(start, n), :] = val
    else:
        ref[pl.ds(start, n, stride=dil), :] = val


def _attn_specs(ct, n_heads, chunk_of):
    cur = lambda col0: pl.BlockSpec((ct, HEAD_DIM), lambda h, s: (chunk_of(s), col0 + h))
    prev = lambda col0: pl.BlockSpec((ct, HEAD_DIM), lambda h, s: (jnp.maximum(chunk_of(s) - 1, 0), col0 + h))
    bias = pl.BlockSpec((None, BAND, 2 * BAND), lambda h, s: (h, 0, 0))
    return cur, prev, bias


def _load_keys(kext_ref, vext_ref, base, k_ref, v_ref, kp_ref, vp_ref, r, dil, ct):
    lc = ct // dil
    kext_ref[pl.ds(base, BAND), :] = _rows(kp_ref, ct - BAND * dil + r, BAND, dil).astype(BF16)
    vext_ref[pl.ds(base, BAND), :] = _rows(vp_ref, ct - BAND * dil + r, BAND, dil).astype(BF16)
    kext_ref[pl.ds(base + BAND, lc), :] = _rows(k_ref, r, lc, dil).astype(BF16)
    vext_ref[pl.ds(base + BAND, lc), :] = _rows(v_ref, r, lc, dil).astype(BF16)


ATTN_GROUP = 4


def _two_level(dil):
    if dil > ATTN_GROUP and dil % ATTN_GROUP == 0:
        return ATTN_GROUP, dil // ATTN_GROUP
    return 1, dil


def _slot_rows(ct):
    return max(ct + BAND, ATTN_GROUP * (ct // ATTN_GROUP + BAND))


def _window_mask(band, jcol, a, c):
    if a > 0:
        return band
    return band & jnp.logical_or(jcol >= BAND, c > 0)


def attn_fwd(name, q, kv, biases):
    T, D = q.shape
    n_heads = D // HEAD_DIM
    ct = min(ATTN_TOKENS, T)
    n_chunks = T // ct
    nbr = len(BRANCHES)
    scale = HEAD_DIM ** -0.5
    nt_dims = (((1,), (1,)), ((), ()))
    nn_dims = (((1,), (0,)), ((), ()))

    n_in = 5

    def body(*refs):
        ins = refs[:n_in]
        b_refs = refs[n_in:n_in + nbr]
        o_ref, obf_ref, lse_ref = refs[n_in + nbr:n_in + nbr + 3]
        kext_ref, vext_ref, acc_ref, m_ref, l_ref = refs[n_in + nbr + 3:n_in + nbr + 8]
        tmp_in = refs[n_in + nbr + 8:n_in + nbr + 8 + n_in]
        tmp_out = refs[n_in + nbr + 8 + n_in:]
        c = pl.program_id(1)
        band, jcol = _band_mask()

        def residue(src, dst, slot, r, dil, cte, bias_v):
            q_ref, k_ref, v_ref, kp_ref, vp_ref = src
            lc = cte // dil
            base = slot * (BAND + lc)
            _load_keys(kext_ref, vext_ref, base, k_ref, v_ref, kp_ref, vp_ref, r, dil, cte)
            for a in range(lc // BAND):
                tok = r + a * BAND * dil
                qa = _rows(q_ref, tok, BAND, dil).astype(BF16)
                kw = kext_ref[pl.ds(base + a * BAND, 2 * BAND), :]
                vw = vext_ref[pl.ds(base + a * BAND, 2 * BAND), :]
                s = lax.dot_general(qa, kw, nt_dims, preferred_element_type=F32) * scale + bias_v
                s = jnp.where(_window_mask(band, jcol, a, c), s, NEG_BIG)
                m = jnp.max(s, axis=-1, keepdims=True)
                p = jnp.exp(s - m)
                den = jnp.sum(p, axis=-1, keepdims=True)
                pv = lax.dot_general(p.astype(BF16), vw, nn_dims, preferred_element_type=F32)
                _set_rows(dst[0], tok, BAND, dil, pv)
                _set_rows(dst[1], tok, BAND, dil, jnp.broadcast_to(m, (BAND, HEAD_DIM)))
                _set_rows(dst[2], tok, BAND, dil, jnp.broadcast_to(den, (BAND, HEAD_DIM)))

        for bi, (win, dil) in enumerate(BRANCHES):
            bias_v = b_refs[bi][...]
            dst = (acc_ref.at[bi], m_ref.at[bi], l_ref.at[bi])
            outer, inner = _two_level(dil)
            if outer == 1:
                for r in range(dil):
                    residue(ins, dst, r % ATTN_GROUP, r, dil, ct, bias_v)
            else:
                cte = ct // outer

                def group(r1, carry, bias_v=bias_v, dst=dst, outer=outer, inner=inner, cte=cte):
                    for t_ref, x_ref in zip(tmp_in, ins):
                        t_ref[...] = _rows(x_ref, r1, cte, outer)
                    for r2 in range(inner):
                        residue(tmp_in, tmp_out, r2 % ATTN_GROUP, r2, inner, cte, bias_v)
                    for t_ref, d_ref in zip(tmp_out, dst):
                        _set_rows(d_ref, r1, cte, outer, t_ref[...])
                    return carry

                lax.fori_loop(0, outer, group, 0)

        def merge(i, carry):
            rows = pl.ds(pl.multiple_of(i * MERGE_ROWS, MERGE_ROWS), MERGE_ROWS)
            ms = [m_ref[bi, rows, :] for bi in range(nbr)]
            m = functools.reduce(jnp.maximum, ms)
            ws = [jnp.exp(mb - m) for mb in ms]
            tot = functools.reduce(lambda x, y: x + y, [w * l_ref[bi, rows, :] for bi, w in enumerate(ws)])
            o = functools.reduce(lambda x, y: x + y, [w * acc_ref[bi, rows, :] for bi, w in enumerate(ws)]) / tot
            o_ref[rows, :] = o
            obf_ref[rows, :] = o.astype(BF16)
            lse_ref[rows, :] = m + jnp.log(tot)
            return carry

        lax.fori_loop(0, ct // min(MERGE_ROWS, ct), merge, 0)

    cur, prev, bias = _attn_specs(ct, n_heads, lambda s: s)
    small = (ct // ATTN_GROUP, HEAD_DIM)
    return pl.pallas_call(
        body, name=name, grid=(n_heads, n_chunks),
        in_specs=[cur(0), cur(0), cur(n_heads), prev(0), prev(n_heads)] + [bias] * nbr,
        out_specs=[cur(0)] * 3,
        out_shape=[jax.ShapeDtypeStruct((T, D), F32), jax.ShapeDtypeStruct((T, D), BF16), jax.ShapeDtypeStruct((T, D), F32)],
        scratch_shapes=[pltpu.VMEM((_slot_rows(ct), HEAD_DIM), BF16)] * 2 + [pltpu.VMEM((nbr, ct, HEAD_DIM), F32)] * 3
                       + [pltpu.VMEM(small, F32)] * (n_in + 3),
        compiler_params=_cparams(("arbitrary", "arbitrary")),
    )(q, kv, kv, kv, kv, *biases)


def attn_bwd(name, q, kv, do, o, lse, biases):
    T, D = q.shape
    n_heads = D // HEAD_DIM
    ct = min(ATTN_TOKENS, T)
    n_chunks = T // ct
    nbr = len(BRANCHES)
    scale = HEAD_DIM ** -0.5
    nt_dims = (((1,), (1,)), ((), ()))
    tn_dims = (((0,), (0,)), ((), ()))
    nn_dims = (((1,), (0,)), ((), ()))
    mrows = min(MERGE_ROWS, ct)
    n_src = 8
    n_acc = 5

    def body(q_ref, k_ref, v_ref, do_ref, o_ref, lse_ref, kp_ref, vp_ref, *rest):
        b_refs = rest[:nbr]
        dq_ref, dk_ref, dv_ref = rest[nbr:nbr + 3]
        dsb_refs = rest[nbr + 3:2 * nbr + 3]
        sc = rest[2 * nbr + 3:]
        kext_ref, vext_ref, dkext_ref, dvext_ref, dqa_ref, dka_ref, dva_ref, dsum_ref, ck_ref, cv_ref = sc[:10]
        tmp_in = sc[10:10 + n_src]
        tmp_acc = sc[10 + n_src:10 + n_src + n_acc]
        dsacc_ref = sc[10 + n_src + n_acc]
        step = pl.program_id(1)
        c = n_chunks - 1 - step
        band, jcol = _band_mask()

        @pl.when(step == 0)
        def _():
            ck_ref[...] = jnp.zeros_like(ck_ref)
            cv_ref[...] = jnp.zeros_like(cv_ref)
            for r in dsb_refs:
                r[...] = jnp.zeros_like(r)

        def prep(i, carry):
            rows = pl.ds(pl.multiple_of(i * mrows, mrows), mrows)
            dsum_ref[rows, :] = jnp.broadcast_to(jnp.sum(do_ref[rows, :] * o_ref[rows, :], axis=-1, keepdims=True), (mrows, HEAD_DIM))
            dqa_ref[rows, :] = jnp.zeros((mrows, HEAD_DIM), F32)
            dka_ref[rows, :] = ck_ref[rows, :]
            dva_ref[rows, :] = cv_ref[rows, :]
            ck_ref[rows, :] = jnp.zeros((mrows, HEAD_DIM), F32)
            cv_ref[rows, :] = jnp.zeros((mrows, HEAD_DIM), F32)
            return carry

        lax.fori_loop(0, ct // mrows, prep, 0)

        def residue(src, acc, slot, r, dil, cte, bias_v):
            sq, sk, sv, sdo, slse, sdsum, skp, svp = src
            adq, adk, adv, ack, acv = acc
            lc = cte // dil
            base = slot * (BAND + lc)
            _load_keys(kext_ref, vext_ref, base, sk, sv, skp, svp, r, dil, cte)
            dkext_ref[pl.ds(base, BAND + lc), :] = jnp.zeros((BAND + lc, HEAD_DIM), F32)
            dvext_ref[pl.ds(base, BAND + lc), :] = jnp.zeros((BAND + lc, HEAD_DIM), F32)
            for a in range(lc // BAND):
                tok = r + a * BAND * dil
                qa = _rows(sq, tok, BAND, dil).astype(BF16)
                doa = _rows(sdo, tok, BAND, dil).astype(BF16)
                kw = kext_ref[pl.ds(base + a * BAND, 2 * BAND), :]
                vw = vext_ref[pl.ds(base + a * BAND, 2 * BAND), :]
                s = lax.dot_general(qa, kw, nt_dims, preferred_element_type=F32) * scale + bias_v
                p = jnp.where(_window_mask(band, jcol, a, c), jnp.exp(s - _rep2(_rows(slse, tok, BAND, dil))), 0.0)
                dp = lax.dot_general(doa, vw, nt_dims, preferred_element_type=F32)
                ds = p * (dp - _rep2(_rows(sdsum, tok, BAND, dil)))
                dsacc_ref[slot] += ds
                dsb16 = ds.astype(BF16)
                dqw = lax.dot_general(dsb16, kw, nn_dims, preferred_element_type=F32) * scale
                _set_rows(adq, tok, BAND, dil, _rows(adq, tok, BAND, dil) + dqw)
                dkext_ref[pl.ds(base + a * BAND, 2 * BAND), :] += lax.dot_general(dsb16, qa, tn_dims, preferred_element_type=F32) * scale
                dvext_ref[pl.ds(base + a * BAND, 2 * BAND), :] += lax.dot_general(p.astype(BF16), doa, tn_dims, preferred_element_type=F32)

            _set_rows(adk, r, lc, dil, _rows(adk, r, lc, dil) + dkext_ref[pl.ds(base + BAND, lc), :])
            _set_rows(adv, r, lc, dil, _rows(adv, r, lc, dil) + dvext_ref[pl.ds(base + BAND, lc), :])
            last = cte - BAND * dil + r
            _set_rows(ack, last, BAND, dil, _rows(ack, last, BAND, dil) + dkext_ref[pl.ds(base, BAND), :])
            _set_rows(acv, last, BAND, dil, _rows(acv, last, BAND, dil) + dvext_ref[pl.ds(base, BAND), :])

        full_src = (q_ref, k_ref, v_ref, do_ref, lse_ref, dsum_ref, kp_ref, vp_ref)
        full_acc = (dqa_ref, dka_ref, dva_ref, ck_ref, cv_ref)
        for bi, (win, dil) in enumerate(BRANCHES):
            bias_v = b_refs[bi][...]
            dsacc_ref[...] = jnp.zeros_like(dsacc_ref)
            outer, inner = _two_level(dil)
            if outer == 1:
                for r in range(dil):
                    residue(full_src, full_acc, r % ATTN_GROUP, r, dil, ct, bias_v)
            else:
                cte = ct // outer

                def group(r1, carry, bias_v=bias_v, outer=outer, inner=inner, cte=cte):
                    for t_ref, x_ref in zip(tmp_in, full_src):
                        t_ref[...] = _rows(x_ref, r1, cte, outer)
                    for t_ref in tmp_acc:
                        t_ref[...] = jnp.zeros_like(t_ref)
                    for r2 in range(inner):
                        residue(tmp_in, tmp_acc, r2 % ATTN_GROUP, r2, inner, cte, bias_v)
                    for t_ref, a_ref in zip(tmp_acc, full_acc):
                        _set_rows(a_ref, r1, cte, outer, _rows(a_ref, r1, cte, outer) + t_ref[...])
                    return carry

                lax.fori_loop(0, outer, group, 0)
            dsb_refs[bi][...] += functools.reduce(lambda x, y: x + y, [dsacc_ref[s] for s in range(ATTN_GROUP)])

        dq_ref[...] = dqa_ref[...].astype(BF16)
        dk_ref[...] = dka_ref[...].astype(BF16)
        dv_ref[...] = dva_ref[...].astype(BF16)

    cur, prev, bias = _attn_specs(ct, n_heads, lambda s: n_chunks - 1 - s)
    small = (ct // ATTN_GROUP, HEAD_DIM)
    res = pl.pallas_call(
        body, name=name, grid=(n_heads, n_chunks),
        in_specs=[cur(0), cur(0), cur(n_heads), cur(0), cur(0), cur(0), prev(0), prev(n_heads)] + [bias] * nbr,
        out_specs=[cur(0)] * 3 + [bias] * nbr,
        out_shape=[jax.ShapeDtypeStruct((T, D), BF16)] * 3 + [jax.ShapeDtypeStruct((n_heads, BAND, 2 * BAND), F32)] * nbr,
        scratch_shapes=[pltpu.VMEM((_slot_rows(ct), HEAD_DIM), BF16)] * 2 + [pltpu.VMEM((_slot_rows(ct), HEAD_DIM), F32)] * 2
                       + [pltpu.VMEM((ct, HEAD_DIM), F32)] * 6 + [pltpu.VMEM(small, F32)] * (n_src + n_acc)
                       + [pltpu.VMEM((ATTN_GROUP, BAND, 2 * BAND), F32)],
        compiler_params=_cparams(("arbitrary", "arbitrary")),
    )(q, kv, kv, do, o, lse, kv, kv, *biases)
    return res[0], res[1], res[2], list(res[3:])


def _divisor_tile(n, cap, mult):
    if n <= cap:
        return n
    t = cap - cap % mult
    while n % t:
        t -= mult
    return t


def _tile2(R, C):
    return _divisor_tile(R, 512, 8), _divisor_tile(C, 1024, 128)


def half_cast(name, dw, core):
    S, R, C = dw.shape
    hr = R // 2
    tr, tc = _tile2(hr, C)
    nrb = hr // tr

    def body(c_ref, x_ref, o_ref):
        o_ref[...] = x_ref[...].astype(BF16)

    return pl.pallas_call(
        body, name=name,
        grid_spec=pltpu.PrefetchScalarGridSpec(
            num_scalar_prefetch=1, grid=(S, nrb, C // tc),
            in_specs=[pl.BlockSpec((None, tr, tc), lambda s, i, j, c: (s, (1 - c[0]) * nrb + i, j))],
            out_specs=pl.BlockSpec((None, tr, tc), lambda s, i, j, c: (s, i, j))),
        out_shape=jax.ShapeDtypeStruct((S, hr, C), BF16),
        compiler_params=_cparams(("parallel", "parallel", "parallel")),
    )(core, dw)


def pair_sum(name, dw, recv, core):
    S, R, C = dw.shape
    hr = R // 2
    tr, tc = _tile2(hr, C)
    nrb = hr // tr

    def body(c_ref, x_ref, r_ref, p_ref, pbf_ref):
        p = x_ref[...] + r_ref[...].astype(F32)
        p_ref[...] = p
        pbf_ref[...] = p.astype(BF16)

    out = pl.BlockSpec((None, tr, tc), lambda s, i, j, c: (s, i, j))
    return pl.pallas_call(
        body, name=name,
        grid_spec=pltpu.PrefetchScalarGridSpec(
            num_scalar_prefetch=1, grid=(S, nrb, C // tc),
            in_specs=[pl.BlockSpec((None, tr, tc), lambda s, i, j, c: (s, c[0] * nrb + i, j)), out],
            out_specs=[out, out]),
        out_shape=[jax.ShapeDtypeStruct((S, hr, C), F32), jax.ShapeDtypeStruct((S, hr, C), BF16)],
        compiler_params=_cparams(("parallel", "parallel", "parallel")),
    )(core, dw, recv)


def chip_sum(name, p, recv, chip, core):
    S, hr, C = p.shape
    tr, tc = _tile2(hr, C)
    nrb = hr // tr

    def body(chip_ref, core_ref, p_ref, r_ref, o_ref):
        acc = p_ref[...]
        for t in range(N_CHIPS - 1):
            acc = acc + r_ref[t].astype(F32)
        o_ref[...] = acc

    return pl.pallas_call(
        body, name=name,
        grid_spec=pltpu.PrefetchScalarGridSpec(
            num_scalar_prefetch=2, grid=(nrb, C // tc),
            in_specs=[pl.BlockSpec((None, tr, tc), lambda i, j, s, c: (s[0], i, j)),
                      pl.BlockSpec((N_CHIPS - 1, tr, tc), lambda i, j, s, c: (0, i, j))],
            out_specs=pl.BlockSpec((tr, tc), lambda i, j, s, c: (c[0] * nrb + i, j))),
        out_shape=jax.ShapeDtypeStruct((2 * hr, C), F32),
        compiler_params=_cparams(("parallel", "parallel")),
    )(chip, core, p, recv)


def adamw(name, w, g, m, v):
    R, C = w.shape
    tr, tc = _tile2(R, C)
    c1 = 1.0 - ADAM_B1 ** ADAM_STEP
    c2 = 1.0 - ADAM_B2 ** ADAM_STEP

    def body(w_ref, g_ref, m_ref, v_ref, d_ref, nm_ref, nv_ref):
        gv = g_ref[...]
        nm = ADAM_B1 * m_ref[...] + (1.0 - ADAM_B1) * gv
        nv = ADAM_B2 * v_ref[...] + (1.0 - ADAM_B2) * (gv * gv)
        nm_ref[...] = nm
        nv_ref[...] = nv
        d_ref[...] = -ADAM_LR * ((nm / c1) / (jnp.sqrt(nv / c2) + ADAM_EPS) + ADAM_WD * w_ref[...])

    spec = pl.BlockSpec((tr, tc), lambda i, j: (i, j))
    return pl.pallas_call(
        body, name=name, grid=(R // tr, C // tc), in_specs=[spec] * 4, out_specs=[spec] * 3,
        out_shape=[jax.ShapeDtypeStruct((R, C), F32)] * 3,
        compiler_params=_cparams(("parallel", "parallel")),
    )(w, g, m, v)


def adamw_layers(name, w, g_layers, m, v):
    nl, R, C = w.shape
    tr, tc = _tile2(R, C)
    ni, nj = R // tr, C // tc
    c1 = 1.0 - ADAM_B1 ** ADAM_STEP
    c2 = 1.0 - ADAM_B2 ** ADAM_STEP

    def body(w_ref, *rest):
        g_refs = rest[:nl]
        m_ref, v_ref, g_ref, d_ref, nm_ref, nv_ref = rest[nl:]
        layer = pl.program_id(0)
        gv = g_refs[0][...]
        for l in range(1, nl):
            gv = jnp.where(layer == l, g_refs[l][...], gv)
        nm = ADAM_B1 * m_ref[...] + (1.0 - ADAM_B1) * gv
        nv = ADAM_B2 * v_ref[...] + (1.0 - ADAM_B2) * (gv * gv)
        g_ref[...] = gv
        nm_ref[...] = nm
        nv_ref[...] = nv
        d_ref[...] = -ADAM_LR * ((nm / c1) / (jnp.sqrt(nv / c2) + ADAM_EPS) + ADAM_WD * w_ref[...])

    def g_spec(l):
        def index(layer, i, j):
            return (jnp.where(layer == l, i, jnp.where(layer < l, 0, ni - 1)),
                    jnp.where(layer == l, j, jnp.where(layer < l, 0, nj - 1)))
        return pl.BlockSpec((tr, tc), index)

    spec = pl.BlockSpec((None, tr, tc), lambda layer, i, j: (layer, i, j))
    return pl.pallas_call(
        body, name=name, grid=(nl, ni, nj),
        in_specs=[spec] + [g_spec(l) for l in range(nl)] + [spec] * 2, out_specs=[spec] * 4,
        out_shape=[jax.ShapeDtypeStruct((nl, R, C), F32)] * 4,
        compiler_params=_cparams(("arbitrary", "arbitrary", "arbitrary")),
    )(w, *g_layers, m, v)


def sum_devices(name, gathered):
    n, R, C = gathered.shape

    def body(x_ref, o_ref):
        acc = x_ref[0]
        for d in range(1, n):
            acc = acc + x_ref[d]
        o_ref[...] = acc

    return pl.pallas_call(
        body, name=name, in_specs=[pl.BlockSpec(memory_space=pltpu.VMEM)],
        out_specs=pl.BlockSpec(memory_space=pltpu.VMEM),
        out_shape=jax.ShapeDtypeStruct((R, C), F32),
    )(gathered)


def _place():
    x, y, c = lax.axis_index("x"), lax.axis_index("y"), lax.axis_index("c")
    return x, y, c


def _other_chips(x, y):
    return [(1 - x, y), (x, 1 - y), (1 - x, 1 - y)]


def all_gather8(name, block):
    R, C = block.shape

    def body(x_ref, out_ref, send_sems, recv_sems, local_sem):
        x, y, c = _place()
        me, sibling = (x, y, c), (x, y, 1 - c)
        chips = _other_chips(x, y)

        def rows(px, py, pc):
            return out_ref.at[4 * px + 2 * py + pc]

        def copy(k, blk, to, src=None):
            return pltpu.make_async_remote_copy(
                src_ref=rows(*blk) if src is None else src, dst_ref=rows(*blk),
                send_sem=send_sems.at[k], recv_sem=recv_sems.at[k], device_id=to, device_id_type=MESH)

        mine = pltpu.make_async_copy(x_ref, rows(*me), local_sem)
        mine.start()
        first = [copy(0, me, sibling, src=x_ref)]
        first += [copy(1 + j, me, (*chip, c), src=x_ref) for j, chip in enumerate(chips)]
        for cp in first:
            cp.start()
        passed = [copy(4 + j, (*chip, c), sibling) for j, chip in enumerate(chips)]
        for j, chip in enumerate(chips):
            copy(1 + j, (*chip, c), me).wait_recv()
            passed[j].start()
        copy(0, sibling, me).wait_recv()
        for j, chip in enumerate(chips):
            copy(4 + j, (*chip, 1 - c), me).wait_recv()
        for cp in first + passed:
            cp.wait_send()
        mine.wait()

    return pl.pallas_call(
        body, name=name, out_shape=jax.ShapeDtypeStruct((N_DEV, R, C), block.dtype),
        in_specs=[pl.BlockSpec(memory_space=pltpu.VMEM)], out_specs=pl.BlockSpec(memory_space=pltpu.VMEM),
        scratch_shapes=[pltpu.SemaphoreType.DMA((7,)), pltpu.SemaphoreType.DMA((7,)), pltpu.SemaphoreType.DMA],
    )(block)


_HBM = pl.BlockSpec(memory_space=pltpu.HBM)
_SEM = pl.BlockSpec(memory_space=pltpu.SEMAPHORE)
_DATAFLOW = pltpu.SideEffectType.DATAFLOW_SIDE_EFFECTING


def _in_hbm(a):
    return pltpu.with_memory_space_constraint(a, pltpu.HBM)


def split_start(name, srcs, lands, n_sem, plan):
    ns, nl = len(srcs), len(lands)

    def body(*refs):
        src, land = refs[:ns], refs[ns:ns + nl]
        send_sems, recv_sems = refs[ns + nl], refs[ns + nl + 1]
        token = refs[-1]
        outgoing, _ = plan(src, land, send_sems, recv_sems)
        for cp in outgoing:
            cp.start()
        token[...] = jnp.zeros_like(token)

    bufs = list(srcs) + list(lands)
    res = pl.pallas_call(
        body, name=name,
        out_shape=(pltpu.SemaphoreType.DMA((n_sem,)), pltpu.SemaphoreType.DMA((n_sem,)),
                   *[pltpu.HBM(b.shape, b.dtype) for b in bufs], jax.ShapeDtypeStruct((8, 128), F32)),
        in_specs=[_HBM] * (ns + nl),
        out_specs=(_SEM, _SEM, *[_HBM] * (ns + nl), pl.BlockSpec(memory_space=pltpu.VMEM)),
        input_output_aliases={i: 2 + i for i in range(ns + nl)},
        compiler_params=pltpu.CompilerParams(has_side_effects=_DATAFLOW),
    )(*[_in_hbm(b) for b in bufs])
    return res[0], res[1], list(res[2:2 + ns]), list(res[2 + ns:2 + ns + nl]), res[-1]


def split_wait(name, started, after, plan):
    send_sems, recv_sems, srcs, lands, _ = started
    ns, nl = len(srcs), len(lands)

    def body(*refs):
        src, land = refs[:ns], refs[ns:ns + nl]
        send, recv = refs[ns + nl], refs[ns + nl + 1]
        outgoing, incoming = plan(src, land, send, recv)
        for cp in outgoing:
            cp.wait_send()
        for cp in incoming:
            cp.wait_recv()

    bufs = list(srcs) + list(lands)
    res = pl.pallas_call(
        body, name=name,
        out_shape=tuple(pltpu.HBM(b.shape, b.dtype) for b in bufs),
        in_specs=[_HBM] * (ns + nl) + [_SEM, _SEM, pl.BlockSpec(memory_space=pl.ANY)],
        out_specs=tuple([_HBM] * (ns + nl)),
        input_output_aliases={i: i for i in range(ns + nl)},
        compiler_params=pltpu.CompilerParams(has_side_effects=_DATAFLOW),
    )(*bufs, send_sems, recv_sems, after)
    return list(res[ns:])


def _rcopy(src, dst, send_sems, ks, recv_sems, kr, device):
    return pltpu.make_async_remote_copy(src_ref=src, dst_ref=dst, send_sem=send_sems.at[ks], recv_sem=recv_sems.at[kr],
                                        device_id=device, device_id_type=MESH)


def _half_rows(ref, h):
    hr = ref.shape[0] // 2
    return ref.at[pl.ds(h * hr, hr)]


def _gather_plan(src, land, send_sems, recv_sems):
    x, y, c = _place()
    me_chip = 2 * x + y
    chips = _other_chips(x, y)
    outgoing, incoming = [], []
    for w, buf in enumerate(land):
        mine = _half_rows(buf.at[me_chip], c)
        for t, chip in enumerate(chips):
            slot = 2 * chip[0] + chip[1]
            for cc in range(2):
                outgoing.append(_rcopy(mine, mine, send_sems, 6 * w + 2 * t + cc, recv_sems, 6 * w + 2 * t + c, (*chip, cc)))
                theirs = _half_rows(buf.at[slot], cc)
                incoming.append(_rcopy(theirs, theirs, send_sems, 6 * w + 2 * t + cc, recv_sems, 6 * w + 2 * t + cc, (*chip, cc)))
    return outgoing, incoming


def _gather_half_plan(src, land, send_sems, recv_sems):
    x, y, c = _place()
    me_chip = 2 * x + y
    mine = _half_rows(land[0].at[me_chip], c)
    outgoing, incoming = [], []
    for t, chip in enumerate(_other_chips(x, y)):
        outgoing.append(_rcopy(mine, mine, send_sems, t, recv_sems, t, (*chip, c)))
        theirs = _half_rows(land[0].at[2 * chip[0] + chip[1]], c)
        incoming.append(_rcopy(theirs, theirs, send_sems, t, recv_sems, t, (*chip, c)))
    return outgoing, incoming


def _forward_halves_plan(src, land, send_sems, recv_sems):
    x, y, c = _place()
    outgoing, incoming = [], []
    for t, chip in enumerate(_other_chips(x, y)):
        slot = land[0].at[2 * chip[0] + chip[1]]
        got, missing = _half_rows(slot, c), _half_rows(slot, 1 - c)
        outgoing.append(_rcopy(got, got, send_sems, t, recv_sems, t, (x, y, 1 - c)))
        incoming.append(_rcopy(missing, missing, send_sems, t, recv_sems, t, (x, y, 1 - c)))
    return outgoing, incoming


def _all_to_all_plan(src, land, send_sems, recv_sems):
    x, y, c = _place()
    mine = land[0].at[4 * x + 2 * y + c]
    outgoing, incoming = [], []
    for k in range(1, N_DEV):
        fx, fy, fc = (k >> 2) & 1, (k >> 1) & 1, k & 1
        px, py, pc = (1 - x if fx else x), (1 - y if fy else y), (1 - c if fc else c)
        outgoing.append(_rcopy(mine, mine, send_sems, k - 1, recv_sems, k - 1, (px, py, pc)))
        theirs = land[0].at[4 * px + 2 * py + pc]
        incoming.append(_rcopy(theirs, theirs, send_sems, k - 1, recv_sems, k - 1, (px, py, pc)))
    return outgoing, incoming


def place_block(name, block, slot, n_slots):
    R, C = block.shape

    def body(slot_ref, x_ref, o_ref):
        o_ref[...] = x_ref[...]

    return pl.pallas_call(
        body, name=name,
        grid_spec=pltpu.PrefetchScalarGridSpec(
            num_scalar_prefetch=1, grid=(1,),
            in_specs=[pl.BlockSpec((R, C), lambda i, s: (0, 0))],
            out_specs=pl.BlockSpec((None, R, C), lambda i, s: (s[0], 0, 0))),
        out_shape=jax.ShapeDtypeStruct((n_slots, R, C), block.dtype),
        compiler_params=_cparams(("arbitrary",)),
    )(slot, block)


def _swap_plan(src, land, send_sems, recv_sems):
    x, y, c = _place()
    cp = _rcopy(src[0], land[0], send_sems, 0, recv_sems, 0, (x, y, 1 - c))
    return [cp], [cp]


def _scatter_plan(src, land, send_sems, recv_sems):
    x, y, c = _place()
    cps = [_rcopy(src[0].at[2 * chip[0] + chip[1]], land[0].at[t], send_sems, t, recv_sems, t, (*chip, c))
           for t, chip in enumerate(_other_chips(x, y))]
    return cps, cps


def _share_plan(src, land, send_sems, recv_sems):
    x, y, c = _place()
    mine, theirs = _half_rows(land[0], c), _half_rows(land[0], 1 - c)
    return ([_rcopy(mine, mine, send_sems, 0, recv_sems, 0, (x, y, 1 - c))],
            [_rcopy(theirs, theirs, send_sems, 0, recv_sems, 0, (x, y, 1 - c))])


def place_shard(name, shard, chip, deps=(), layer=None):
    R, C = shard.shape[-2:]
    tr, tc = _tile2(R, C)

    def body(chip_ref, x_ref, *rest):
        rest[-1][...] = x_ref[...].astype(BF16)

    if layer is None:
        src = pl.BlockSpec((tr, tc), lambda i, j, s: (i, j))
    else:
        src = pl.BlockSpec((None, tr, tc), lambda i, j, s: (layer, i, j))
    return pl.pallas_call(
        body, name=name,
        grid_spec=pltpu.PrefetchScalarGridSpec(
            num_scalar_prefetch=1, grid=(R // tr, C // tc),
            in_specs=[src] + [pl.BlockSpec(d.shape, lambda i, j, s: (0, 0)) for d in deps],
            out_specs=pl.BlockSpec((None, tr, tc), lambda i, j, s: (s[0], i, j))),
        out_shape=jax.ShapeDtypeStruct((N_CHIPS, R, C), BF16),
        compiler_params=_cparams(("parallel", "parallel")),
    )(chip, shard, *deps)


class GradExchange:
    SCATTER_TICKS = 2

    def __init__(self, chip1, core, shard, mom, vel):
        self.chip1, self.core, self.shard, self.mom, self.vel = chip1, core, shard, mom, vel
        self.inflight, self.tokens, self.results = [], [], {}

    def take_deps(self):
        deps, self.tokens = self.tokens, []
        return deps

    def _start(self, name, srcs, lands, n_sem, plan):
        started = split_start(name, srcs, lands, n_sem, plan)
        self.tokens.append(started[-1])
        return started

    def add(self, n, dw):
        S, R, C = dw.shape
        to_sibling = half_cast("rs_cast_" + n, dw, self.core)
        started = self._start("rs_swap_start_" + n, [to_sibling], [lax.empty((S, R // 2, C), BF16)], 1, _swap_plan)
        self.inflight.append(dict(n=n, dw=dw, stage=0, started=started, ticks=0))

    def tick(self, after):
        for it in self.inflight:
            n = it["n"]
            if it["stage"] == 0:
                (recv,) = split_wait("rs_swap_wait_" + n, it["started"], after, _swap_plan)
                p, pbf = pair_sum("rs_pair_sum_" + n, it["dw"], recv, self.core)
                S, hr, C = pbf.shape
                it.update(stage=1, p=p, ticks=0,
                          started=self._start("rs_scatter_start_" + n, [pbf], [lax.empty((N_CHIPS - 1, hr, C), BF16)], 3, _scatter_plan))
            elif it["stage"] == 1:
                it["ticks"] += 1
                if it["ticks"] >= self.SCATTER_TICKS:
                    (recv,) = split_wait("rs_scatter_wait_" + n, it["started"], after, _scatter_plan)
                    half = chip_sum("rs_chip_sum_" + n, it["p"], recv, self.chip1, self.core)
                    it.update(stage=2, started=self._start("rs_share_start_" + n, [], [half], 1, _share_plan))
            elif it["stage"] == 2:
                (grad,) = split_wait("rs_share_wait_" + n, it["started"], after, _share_plan)
                if n in self.shard:
                    self.results[n] = (grad,) + tuple(adamw("adamw_" + n, self.shard[n], grad, self.mom[n], self.vel[n]))
                else:
                    self.results[n] = (grad,)
                it["stage"] = 3
        self.inflight = [it for it in self.inflight if it["stage"] < 3]

    def flush(self, after):
        while self.inflight:
            self.tick(after)


def _pack(arrs):
    parts = []
    for a in arrs:
        flat = a.reshape(-1).astype(F32)
        n = flat.shape[0]
        padded = -(-n // 1024) * 1024
        parts.append(jnp.pad(flat, (0, padded - n)).reshape(padded // 128, 128))
    return jnp.concatenate(parts, axis=0)


def _unpack(buf, shapes):
    out, row = [], 0
    for shp in shapes:
        n = int(np.prod(shp))
        rows = -(-n // 1024) * 8
        out.append(buf[row:row + rows].reshape(-1)[:n].reshape(shp))
        row += rows
    return out


def _bias_epi(acc, b):
    return (acc + b,)


def local_step(x, target, W, P, ex, first_deps=(), on_small=None):
    T, D = x.shape
    g = {}
    plain = lambda acc: (acc,)

    (h1,) = mm_nn("pw1_fwd", x, W("pw1", x), "col", _bias_epi, [F32],
                  extras=[(P["pw1_b"], "row")] + [(d, "dep") for d in first_deps])
    u, cpre, s = conv_fwd("conv_fwd", h1, P["dw_w"], P["dw_b"], P["cln_g"], P["cln_b"])
    (mix0,) = mm_nn("pw2_fwd", s, W("pw2", s), "row", _bias_epi, [F32], extras=[(P["pw2_b"], "row")])
    ln = [None] * 4
    gam = [P["ln_mix_g"][0:1], P["ln_mlp_g"][0:1], P["ln_mix_g"][1:2], P["ln_mlp_g"][1:2]]
    bet = [P["ln_mix_b"][0:1], P["ln_mlp_b"][0:1], P["ln_mix_b"][1:2], P["ln_mlp_b"][1:2]]
    ln[0] = ln_fwd("ln0_fwd", mix0, x)(gam[0], bet[0])

    def mlp_fwd(tag, i_ln, n1, n2):
        xhat, rstd, xbf = ln[i_ln]

        def up_epi(acc):
            r = jnp.maximum(acc, 0.0)
            return r * r, r

        hid, relu = mm_nn(tag + "_up", xbf, W(n1, xbf), "col", up_epi, [BF16, BF16])
        (mlp,) = mm_nn(tag + "_down", hid, W(n2, hid), "row", plain, [F32])
        ln[i_ln + 1] = ln_fwd(tag + "_ln", mlp, xhat, gam[i_ln], bet[i_ln])(gam[i_ln + 1], bet[i_ln + 1])
        return hid, relu

    hid0 = mlp_fwd("mlp0", 0, "w1_0", "w2_0")

    x2bf = ln[1][2]
    (kv,) = mm_nn("kv_fwd", x2bf, W("kv", x2bf), "col", plain, [F32])
    (q,) = mm_nn("q_fwd", x2bf, W("wq", kv), "row", plain, [F32])
    biases = [bias_expand("bias_d%d" % d, P["rel_bias"], d) for _, d in BRANCHES]
    assert all(win // d == BAND and min(ATTN_TOKENS, T) % (BAND * d) == 0 for win, d in BRANCHES)
    o, obf, lse = attn_fwd("attn_fwd", q, kv, biases)
    (attn,) = mm_nn("wo_fwd", obf, W("wo", obf), "row", plain, [F32])
    ln[2] = ln_fwd("ln2_fwd", attn, ln[1][0], gam[1], bet[1])(gam[2], bet[2])
    hid1 = mlp_fwd("mlp1", 2, "w1_1", "w2_1")

    dr3, dr3bf, g["ln_mlp_g1"], g["ln_mlp_b1"], _, loss_sum = ln_bwd(
        "ln3_bwd", ln[3][0], ln[3][1], gam[3], target=target, beta=bet[3])

    def dw_step(name, wname, a, cot, axis):
        dw = mm_tn(name, a, cot, W(wname, a).shape, axis, deps=ex.take_deps())
        ex.tick(dw)
        ex.add(wname, dw)

    def dx_step(name, cot, wname, axis, epilogue, out_dtype, extras):
        deps = [(d, "dep") for d in ex.take_deps()]
        (out,) = mm_nt(name, cot, W(wname, cot), axis, epilogue, [out_dtype], extras=list(extras) + deps)
        ex.tick(out)
        return out

    def mlp_bwd(tag, i_ln, n1, n2, hid_relu, dr, drbf):
        xbf = ln[i_ln][2]
        hid, relu = hid_relu
        dw_step(tag + "_dw2", n2, hid, drbf, "row")
        dp = dx_step(tag + "_dhid", drbf, n2, "row", lambda acc, r: (acc * (2.0 * r.astype(F32)),), BF16, [(relu, "tile")])
        dw_step(tag + "_dw1", n1, xbf, dp, "col")
        return dx_step(tag + "_dx", dp, n1, "col", lambda acc, e: (acc + ALPHA * e,), F32, [(dr, "tile")])

    dx3 = mlp_bwd("mlp1", 2, "w1_1", "w2_1", hid1, dr3, dr3bf)
    dr2, dr2bf, g["ln_mix_g1"], g["ln_mix_b1"], _ = ln_bwd("ln2_bwd", ln[2][0], ln[2][1], gam[2], dy=dx3)
    dw_step("wo_dw", "wo", obf, dr2bf, "row")
    do = dx_step("wo_dx", dr2bf, "wo", "row", plain, F32, [])
    dq, dk, dv, dsbs = attn_bwd("attn_bwd", q, kv, do, o, lse, biases)
    g["rel_bias"] = relbias_grad("relbias_grad", dsbs)[:, 0, :REL_BUCKETS].T
    dkv = jnp.concatenate([dk, dv], axis=1)
    dw_step("wq_dw", "wq", x2bf, dq, "row")
    dw_step("kv_dw", "kv", x2bf, dkv, "col")
    dx2a = dx_step("wq_dx", dq, "wq", "row", lambda acc, e: (acc + ALPHA * e,), F32, [(dr2, "tile")])
    dx2 = dx_step("kv_dx", dkv, "kv", "col", lambda acc, e: (acc + e,), F32, [(dx2a, "tile")])

    dr1, dr1bf, g["ln_mlp_g0"], g["ln_mlp_b0"], _ = ln_bwd("ln1_bwd", ln[1][0], ln[1][1], gam[1], dy=dx2)
    dx1 = mlp_bwd("mlp0", 0, "w1_0", "w2_0", hid0, dr1, dr1bf)
    dr0, dr0bf, g["ln_mix_g0"], g["ln_mix_b0"], g["pw2_b"] = ln_bwd("ln0_bwd", ln[0][0], ln[0][1], gam[0], dy=dx1)

    dw_step("pw2_dw", "pw2", s, dr0bf, "row")
    ds = dx_step("pw2_dx", dr0bf, "pw2", "row", plain, F32, [])
    dc, g["cln_g"], g["cln_b"], g["dw_b"] = conv_bwd_ln("conv_bwd_ln", ds, cpre, P["cln_g"], P["cln_b"])
    dh1, g["pw1_b"], g["dw_w"] = conv_bwd_taps("conv_bwd_taps", dc, u, h1, P["dw_w"])
    if on_small is not None:
        on_small(g)
    dw_step("pw1_dw", "pw1", x, dh1, "col")
    dx = dx_step("pw1_dx", dh1, "pw1", "col", lambda acc, e: (acc + ALPHA * e,), F32, [(dr0, "tile")])
    return loss_sum, dx, g


BIG = ("pw1", "pw2", "w1_0", "w2_0", "kv", "wq", "wo", "w1_1", "w2_1")


def kernel(x, conv_pw1_w, conv_pw1_b, conv_dw_w, conv_dw_b, conv_ln_g, conv_ln_b, conv_pw2_w, conv_pw2_b, w_kv, attn_wq, attn_wo, rel_bias, mlp_w1, mlp_w2, ln_mix_g, ln_mix_b, ln_mlp_g, ln_mlp_b, loss_target, m_conv_pw1_w, m_conv_pw1_b, m_conv_dw_w, m_conv_dw_b, m_conv_ln_g, m_conv_ln_b, m_conv_pw2_w, m_conv_pw2_b, m_w_kv, m_attn_wq, m_attn_wo, m_rel_bias, m_mlp_w1, m_mlp_w2, m_ln_mix_g, m_ln_mix_b, m_ln_mlp_g, m_ln_mlp_b, v_conv_pw1_w, v_conv_pw1_b, v_conv_dw_w, v_conv_dw_b, v_conv_ln_g, v_conv_ln_b, v_conv_pw2_w, v_conv_pw2_b, v_w_kv, v_attn_wq, v_attn_wo, v_rel_bias, v_mlp_w1, v_mlp_w2, v_ln_mix_g, v_ln_mix_b, v_ln_mlp_g, v_ln_mlp_b):
    _, T, D = x.shape
    xi, yi, ci = _place()
    chip = 2 * xi + yi
    core = jnp.reshape(ci, (1,)).astype(jnp.int32)
    chip1 = jnp.reshape(chip, (1,)).astype(jnp.int32)

    def two_d(a):
        return a.reshape(a.shape[-2:])

    shard = {"pw1": two_d(conv_pw1_w), "pw2": two_d(conv_pw2_w), "kv": w_kv, "wq": two_d(attn_wq), "wo": two_d(attn_wo)}
    mom = {"pw1": two_d(m_conv_pw1_w), "pw2": two_d(m_conv_pw2_w), "kv": m_w_kv, "wq": two_d(m_attn_wq), "wo": two_d(m_attn_wo)}
    vel = {"pw1": two_d(v_conv_pw1_w), "pw2": two_d(v_conv_pw2_w), "kv": v_w_kv, "wq": two_d(v_attn_wq), "wo": two_d(v_attn_wo)}
    stacked = {"w1_0": (mlp_w1, 0), "w1_1": (mlp_w1, 1), "w2_0": (mlp_w2, 0), "w2_1": (mlp_w2, 1)}

    started = {}
    for n in BIG:
        deps = [started[prev][-1] for prev in list(started)[-1:]]
        src, layer = stacked.get(n, (shard.get(n), None))
        land = place_shard("place_" + n, src, chip1, deps, layer)
        if n == BIG[0]:
            started[n] = split_start("gather_start_" + n, [], [land], 3, _gather_half_plan)
        else:
            started[n] = split_start("gather_start_" + n, [], [land], 6, _gather_plan)
    gathered = {}

    def W(n, after):
        if n not in gathered:
            if n == BIG[0]:
                lands = split_wait("gather_wait_" + n, started[n], started[BIG[-1]][-1], _gather_half_plan)
                passed = split_start("gather_pass_start_" + n, [], lands, 3, _forward_halves_plan)
                (gathered[n],) = split_wait("gather_pass_wait_" + n, passed, passed[-1], _forward_halves_plan)
            else:
                (gathered[n],) = split_wait("gather_wait_" + n, started[n], after, _gather_plan)
        return gathered[n]

    sharded_small = [conv_pw1_b, conv_dw_w[0], conv_dw_b, conv_ln_g, conv_ln_b, conv_pw2_b]
    sh_shapes = [a.shape for a in sharded_small]
    small_all = all_gather8("gather_small", _pack(sharded_small))
    per_chip = [_unpack(small_all[2 * j], sh_shapes) for j in range(N_CHIPS)]
    full = [jnp.concatenate([per_chip[j][i] for j in range(N_CHIPS)], axis=-1) for i in range(len(sharded_small))]
    P = dict(pw1_b=full[0], dw_w=full[1], dw_b=full[2], cln_g=full[3], cln_b=full[4], pw2_b=full[5],
             rel_bias=rel_bias, ln_mix_g=ln_mix_g, ln_mix_b=ln_mix_b, ln_mlp_g=ln_mlp_g, ln_mlp_b=ln_mlp_b)

    ex = GradExchange(chip1, core, shard, mom, vel)

    small_names = ["pw1_b", "dw_w", "dw_b", "cln_g", "cln_b", "pw2_b", "rel_bias",
                   "ln_mix_g0", "ln_mix_g1", "ln_mix_b0", "ln_mix_b1", "ln_mlp_g0", "ln_mlp_g1", "ln_mlp_b0", "ln_mlp_b1"]
    small = {}

    def on_small(g):
        grads = [g[n] for n in small_names]
        small["shapes"] = [a.shape for a in grads]
        device1 = jnp.reshape(4 * xi + 2 * yi + ci, (1,)).astype(jnp.int32)
        land = place_block("place_small_grads", _pack(grads), device1, N_DEV)
        small["started"] = split_start("small_grads_start", [], [land], N_DEV - 1, _all_to_all_plan)
        ex.tokens.append(small["started"][-1])

    loss_sum, dx, g = local_step(x.reshape(T, D), loss_target.reshape(T, D), W, P, ex,
                                 first_deps=[started[n][-1] for n in BIG], on_small=on_small)
    loss = (0.5 / D) * lax.psum(loss_sum[0, 0], ("x", "y", "c"))
    (all_small,) = split_wait("small_grads_wait", small["started"], dx, _all_to_all_plan)
    summed = sum_devices("small_grad_sum", all_small)
    sg = dict(zip(small_names, _unpack(summed, small["shapes"])))

    def my_cols(a, width):
        return lax.dynamic_slice_in_dim(a, chip * width, width, axis=a.ndim - 1)

    small_g = [my_cols(sg["pw1_b"], conv_pw1_b.shape[-1]),
               my_cols(sg["dw_w"], conv_dw_w.shape[-1])[None],
               my_cols(sg["dw_b"], conv_dw_b.shape[-1]), my_cols(sg["cln_g"], conv_ln_g.shape[-1]),
               my_cols(sg["cln_b"], conv_ln_b.shape[-1]), my_cols(sg["pw2_b"], conv_pw2_b.shape[-1]),
               sg["rel_bias"],
               jnp.concatenate([sg["ln_mix_g0"], sg["ln_mix_g1"]], axis=0),
               jnp.concatenate([sg["ln_mix_b0"], sg["ln_mix_b1"]], axis=0),
               jnp.concatenate([sg["ln_mlp_g0"], sg["ln_mlp_g1"]], axis=0),
               jnp.concatenate([sg["ln_mlp_b0"], sg["ln_mlp_b1"]], axis=0)]
    small_w = [conv_pw1_b, conv_dw_w, conv_dw_b, conv_ln_g, conv_ln_b, conv_pw2_b, rel_bias, ln_mix_g, ln_mix_b, ln_mlp_g, ln_mlp_b]
    small_m = [m_conv_pw1_b, m_conv_dw_w, m_conv_dw_b, m_conv_ln_g, m_conv_ln_b, m_conv_pw2_b, m_rel_bias, m_ln_mix_g, m_ln_mix_b, m_ln_mlp_g, m_ln_mlp_b]
    small_v = [v_conv_pw1_b, v_conv_dw_w, v_conv_dw_b, v_conv_ln_g, v_conv_ln_b, v_conv_pw2_b, v_rel_bias, v_ln_mix_g, v_ln_mix_b, v_ln_mlp_g, v_ln_mlp_b]
    sw_shapes = [a.shape for a in small_w]
    small_g = [a.reshape(s) for a, s in zip(small_g, sw_shapes)]
    upd_small = adamw("adamw_small", _pack(small_w), _pack(small_g), _pack(small_m), _pack(small_v))
    sd, snm, snv = (_unpack(b, sw_shapes) for b in upd_small)

    res_w1 = adamw_layers("adamw_w1", mlp_w1, [ex.results["w1_0"][0], ex.results["w1_1"][0]], m_mlp_w1, v_mlp_w1)
    res_w2 = adamw_layers("adamw_w2", mlp_w2, [ex.results["w2_0"][0], ex.results["w2_1"][0]], m_mlp_w2, v_mlp_w2)
    ex.flush(res_w2[1])

    def big_out(k):
        one = {n: ex.results[n][k] for n in shard}
        return dict(pw1=one["pw1"][None], pw2=one["pw2"][None], kv=one["kv"], wq=one["wq"][None], wo=one["wo"][None],
                    w1=res_w1[k], w2=res_w2[k])

    def ordered(big, small):
        return [big["pw1"], small[0], small[1], small[2], small[3], small[4], big["pw2"], small[5], big["kv"], big["wq"],
                big["wo"], small[6], big["w1"], big["w2"], small[7], small[8], small[9], small[10]]

    grads = ordered(big_out(0), small_g)
    deltas = ordered(big_out(1), sd)
    new_m = ordered(big_out(2), snm)
    new_v = ordered(big_out(3), snv)
    return (loss, dx.reshape(1, T, D), *grads, *deltas, *new_m, *new_v)
```

```python
import functools
import math

import numpy as np
import jax
import jax.numpy as jnp
from jax import lax
from jax.experimental import pallas as pl
from jax.experimental.pallas import tpu as pltpu

F32 = jnp.float32
BF16 = jnp.bfloat16

HEAD_DIM = 128
BAND = 128
BRANCHES = ((128, 1), (512, 4), (2048, 16))
CONV_WIDTH = 31
CONV_HALO = 32
REL_BUCKETS = 32
REL_MAX_DIST = 2048
DEPTH = 2
ALPHA = (2 * DEPTH) ** 0.25
LN_EPS = 1e-5
ADAM_LR, ADAM_B1, ADAM_B2, ADAM_EPS, ADAM_WD, ADAM_STEP = 0.001, 0.9, 0.999, 1e-08, 0.01, 10

N_CHIPS = 4
N_DEV = 8
MESH = pl.DeviceIdType.MESH
VMEM_LIMIT_BYTES = 56 * 1024 * 1024
MM_TM, MM_TN, MM_TK = 1024, 1024, 2048
ROW_TILE = 256
CONV_TILE = 128
NEG_BIG = -1e30


def _cparams(sem):
    return pltpu.CompilerParams(dimension_semantics=sem, vmem_limit_bytes=VMEM_LIMIT_BYTES)


def _sigmoid(x):
    return 1.0 / (1.0 + jnp.exp(-x))


def _wspec(wshape, axis, br, bc, rsel, csel):
    _, R, C = wshape
    if axis == "col":
        if bc > C:
            assert bc % C == 0, (wshape, bc)
            return pl.BlockSpec((bc // C, br, C), lambda *g: (csel(*g), rsel(*g), 0))
        nb = C // bc
        assert nb * bc == C, (wshape, bc)
        return pl.BlockSpec((None, br, bc), lambda *g: (csel(*g) // nb, rsel(*g), csel(*g) % nb))
    if br > R:
        assert br % R == 0, (wshape, br)
        return pl.BlockSpec((br // R, R, bc), lambda *g: (rsel(*g), 0, csel(*g)))
    nb = R // br
    assert nb * br == R, (wshape, br)
    return pl.BlockSpec((None, br, bc), lambda *g: (rsel(*g) // nb, rsel(*g) % nb, csel(*g)))


def _join_shards(b, axis):
    if b.ndim == 2:
        return b
    if axis == "row":
        return b.reshape(b.shape[0] * b.shape[1], b.shape[2])
    return jnp.concatenate([b[s] for s in range(b.shape[0])], axis=1)


def _split_shards(r, shape, axis):
    if len(shape) == 2:
        return r
    if axis == "row":
        return r.reshape(shape)
    return jnp.stack([r[:, s * shape[2]:(s + 1) * shape[2]] for s in range(shape[0])])


def _full_dims(wshape, axis):
    _, R, C = wshape
    return (R, N_CHIPS * C) if axis == "col" else (N_CHIPS * R, C)


def _mm_body(nk, kinds, n_out, dims, epilogue, axis):
    n_extra = len(kinds)

    def body(*refs):
        a_ref, b_ref = refs[0], refs[1]
        extra = [r for r, kind in zip(refs[2:2 + n_extra], kinds) if kind != "dep"]
        outs = refs[2 + n_extra:2 + n_extra + n_out]
        part = lax.dot_general(a_ref[...].astype(BF16), _join_shards(b_ref[...], axis).astype(BF16), (dims, ((), ())),
                               preferred_element_type=F32)

        def write(res):
            for r, o in zip(res, outs):
                o[...] = _split_shards(r, o.shape, axis).astype(o.dtype)

        if nk == 1:
            write(epilogue(part, *[e[...] for e in extra]))
            return
        acc_ref = refs[2 + n_extra + n_out]
        k = pl.program_id(2)

        @pl.when(k == 0)
        def _():
            acc_ref[...] = part

        @pl.when(k > 0)
        def _():
            acc_ref[...] += part

        @pl.when(k == nk - 1)
        def _():
            write(epilogue(acc_ref[...], *[e[...] for e in extra]))
    return body


def _long_tk(a, k_dim):
    tk = min(MM_TK, k_dim)
    if a.dtype == BF16 and k_dim >= 4 * MM_TK:
        tk = 2 * MM_TK
    return tk


def _extra_specs(extras, tm, tn):
    specs = []
    for arr, kind in extras:
        if kind == "tile":
            specs.append(pl.BlockSpec((tm, tn), lambda i, j, k: (i, j)))
        elif kind == "dep":
            specs.append(pl.BlockSpec(arr.shape, lambda i, j, k: (0, 0)))
        else:
            specs.append(pl.BlockSpec((1, tn), lambda i, j, k: (0, j)))
    return specs


def mm_nn(name, a, w, axis, epilogue, out_dtypes, extras=()):
    M, K = a.shape
    Kw, N = _full_dims(w.shape, axis)
    assert K == Kw
    tm, tn, tk = min(MM_TM, M), min(MM_TN, N), _long_tk(a, K)
    nk = K // tk
    in_specs = [pl.BlockSpec((tm, tk), lambda i, j, k: (i, k)),
                _wspec(w.shape, axis, tk, tn, lambda i, j, k: k, lambda i, j, k: j)]
    in_specs += _extra_specs(extras, tm, tn)
    body = _mm_body(nk, [kind for _, kind in extras], len(out_dtypes), ((1,), (0,)), epilogue, axis)
    return pl.pallas_call(
        body, name=name, grid=(M // tm, N // tn, nk), in_specs=in_specs,
        out_specs=[pl.BlockSpec((tm, tn), lambda i, j, k: (i, j)) for _ in out_dtypes],
        out_shape=[jax.ShapeDtypeStruct((M, N), d) for d in out_dtypes],
        scratch_shapes=[pltpu.VMEM((tm, tn), F32)] if nk > 1 else [],
        compiler_params=_cparams(("parallel", "parallel", "arbitrary")),
    )(a, w, *[e for e, _ in extras])


def mm_nt(name, g, w, axis, epilogue, out_dtypes, extras=()):
    M, N = g.shape
    K, Nw = _full_dims(w.shape, axis)
    assert N == Nw
    tm, tn, tk = min(MM_TM, M), min(MM_TN, K), min(MM_TK, N)
    nk = N // tk
    in_specs = [pl.BlockSpec((tm, tk), lambda i, j, k: (i, k)),
                _wspec(w.shape, axis, tn, tk, lambda i, j, k: j, lambda i, j, k: k)]
    in_specs += _extra_specs(extras, tm, tn)
    body = _mm_body(nk, [kind for _, kind in extras], len(out_dtypes), ((1,), (1,)), epilogue, axis)
    return pl.pallas_call(
        body, name=name, grid=(M // tm, K // tn, nk), in_specs=in_specs,
        out_specs=[pl.BlockSpec((tm, tn), lambda i, j, k: (i, j)) for _ in out_dtypes],
        out_shape=[jax.ShapeDtypeStruct((M, K), d) for d in out_dtypes],
        scratch_shapes=[pltpu.VMEM((tm, tn), F32)] if nk > 1 else [],
        compiler_params=_cparams(("parallel", "parallel", "arbitrary")),
    )(g, w, *[e for e, _ in extras])


def mm_tn(name, a, g, wshape, axis, deps=()):
    M, K = a.shape
    Mg, N = g.shape
    assert M == Mg and (K, N) == _full_dims(wshape, axis)
    tm, tn, tk = min(MM_TM, K), min(MM_TN, N), _long_tk(a, M)
    nk = M // tk
    body = _mm_body(nk, ["dep"] * len(deps), 1, ((0,), (0,)), lambda acc: (acc,), axis)
    return pl.pallas_call(
        body, name=name, grid=(K // tm, N // tn, nk),
        in_specs=[pl.BlockSpec((tk, tm), lambda i, j, k: (k, i)),
                  pl.BlockSpec((tk, tn), lambda i, j, k: (k, j))] + _extra_specs([(d, "dep") for d in deps], tm, tn),
        out_specs=[_wspec(wshape, axis, tm, tn, lambda i, j, k: i, lambda i, j, k: j)],
        out_shape=[jax.ShapeDtypeStruct(wshape, F32)],
        scratch_shapes=[pltpu.VMEM((tm, tn), F32)] if nk > 1 else [],
        compiler_params=_cparams(("parallel", "parallel", "arbitrary")),
    )(a, g, *deps)[0]


def _row_spec(tr, width):
    return pl.BlockSpec((tr, width), lambda i: (i, 0))


def _vec_spec(width):
    return pl.BlockSpec((1, width), lambda i: (0, 0))


def _fold8(x):
    r, d = x.shape
    return jnp.sum(x.reshape(r // 8, 8, d), axis=0)


def ln_fwd(name, f, prev, prev_g=None, prev_b=None):
    T, D = f.shape
    tr = min(ROW_TILE, T)
    affine = prev_g is not None

    def body(*refs):
        if affine:
            f_ref, p_ref, pg_ref, pb_ref, g_ref, b_ref, xhat_ref, rstd_ref, xbf_ref = refs
            xprev = p_ref[...] * pg_ref[...] + pb_ref[...]
        else:
            f_ref, p_ref, g_ref, b_ref, xhat_ref, rstd_ref, xbf_ref = refs
            xprev = p_ref[...]
        r = ALPHA * xprev + f_ref[...]
        mu = jnp.mean(r, axis=-1, keepdims=True)
        cen = r - mu
        var = jnp.mean(cen * cen, axis=-1, keepdims=True)
        rstd = lax.rsqrt(var + LN_EPS)
        xhat = cen * rstd
        xhat_ref[...] = xhat
        rstd_ref[...] = rstd
        xbf_ref[...] = (xhat * g_ref[...] + b_ref[...]).astype(BF16)

    def call(g, b):
        ins = [f, prev] + ([prev_g, prev_b] if affine else []) + [g, b]
        specs = [_row_spec(tr, D), _row_spec(tr, D)] + ([_vec_spec(D)] * 2 if affine else []) + [_vec_spec(D)] * 2
        return pl.pallas_call(
            body, name=name, grid=(T // tr,), in_specs=specs,
            out_specs=[_row_spec(tr, D), _row_spec(tr, 1), _row_spec(tr, D)],
            out_shape=[jax.ShapeDtypeStruct((T, D), F32), jax.ShapeDtypeStruct((T, 1), F32),
                       jax.ShapeDtypeStruct((T, D), BF16)],
            compiler_params=_cparams(("parallel",)),
        )(*ins)
    return call


def ln_bwd(name, xhat, rstd, gamma, dy=None, target=None, beta=None):
    T, D = xhat.shape
    tr = min(ROW_TILE, T)
    nt = T // tr
    head = target is not None

    def body(*refs):
        if head:
            xhat_ref, rstd_ref, g_ref, tgt_ref, b_ref = refs[:5]
            outs = refs[5:]
        else:
            xhat_ref, rstd_ref, g_ref, dy_ref = refs[:4]
            outs = refs[4:]
        dr_ref, drbf_ref, dg_ref, db_ref, cs_ref = outs[:5]
        rest = outs[5:]
        if head:
            loss_ref, acc_ref = rest
        else:
            (acc_ref,) = rest
        i = pl.program_id(0)
        xhat_v = xhat_ref[...]
        gam = g_ref[...]
        if head:
            diff = xhat_v * gam + b_ref[...] - tgt_ref[...]
            dyv = diff * (1.0 / D)
        else:
            dyv = dy_ref[...]
        dxh = dyv * gam
        m1 = jnp.mean(dxh, axis=-1, keepdims=True)
        m2 = jnp.mean(dxh * xhat_v, axis=-1, keepdims=True)
        dr = rstd_ref[...] * (dxh - m1 - xhat_v * m2)
        dr_ref[...] = dr
        drbf_ref[...] = dr.astype(BF16)

        @pl.when(i == 0)
        def _():
            acc_ref[...] = jnp.zeros_like(acc_ref)

        acc_ref[0] += _fold8(dyv * xhat_v)
        acc_ref[1] += _fold8(dyv)
        acc_ref[2] += _fold8(dr)
        if head:
            acc_ref[3] += _fold8(diff * diff)

        @pl.when(i == nt - 1)
        def _():
            dg_ref[...] = jnp.sum(acc_ref[0], axis=0, keepdims=True)
            db_ref[...] = jnp.sum(acc_ref[1], axis=0, keepdims=True)
            cs_ref[...] = jnp.sum(acc_ref[2], axis=0, keepdims=True)
            if head:
                loss_ref[...] = jnp.sum(jnp.sum(acc_ref[3], axis=0, keepdims=True), axis=1, keepdims=True)

    ins = [xhat, rstd, gamma] + ([target, beta] if head else [dy])
    specs = [_row_spec(tr, D), _row_spec(tr, 1), _vec_spec(D)] + ([_row_spec(tr, D), _vec_spec(D)] if head else [_row_spec(tr, D)])
    out_specs = [_row_spec(tr, D), _row_spec(tr, D), _vec_spec(D), _vec_spec(D), _vec_spec(D)]
    out_shape = [jax.ShapeDtypeStruct((T, D), F32), jax.ShapeDtypeStruct((T, D), BF16)] + [jax.ShapeDtypeStruct((1, D), F32)] * 3
    if head:
        out_specs.append(pl.BlockSpec((1, 1), lambda i: (0, 0)))
        out_shape.append(jax.ShapeDtypeStruct((1, 1), F32))
    return pl.pallas_call(
        body, name=name, grid=(nt,), in_specs=specs, out_specs=out_specs, out_shape=out_shape,
        scratch_shapes=[pltpu.VMEM((4, 8, D), F32)],
        compiler_params=_cparams(("arbitrary",)),
    )(*ins)


CONV_ROWS, CONV_COLS = 64, 512
CONV_COLS_BWD = 256


def _tap_chunks(tt, D, cols=CONV_COLS):
    for r0 in range(0, tt, min(CONV_ROWS, tt)):
        for c0 in range(0, D, min(cols, D)):
            yield r0, min(CONV_ROWS, tt), c0, min(cols, D)


SUBLANES = 8


def _shifted_copies(ext_ref, sh_ref):
    n = sh_ref.shape[1]
    zero = jnp.minimum(pl.program_id(0), 0)
    for b in range(1, SUBLANES):
        sh_ref[zero + (b - 1)] = ext_ref[pl.ds(b, n), :]


def _rows_at(ext_ref, sh_ref, off, nr, cols):
    a, b = divmod(off, SUBLANES)
    if b == 0:
        return ext_ref[pl.ds(off, nr), cols]
    return sh_ref[b - 1, pl.ds(a * SUBLANES, nr), cols]


def conv_fwd(name, h1, dw, dwb, lng, lnb):
    T, D2 = h1.shape
    D = D2 // 2
    tt = min(CONV_TILE, T)
    hb = tt // CONV_HALO
    KW = dw.shape[0]
    lead = CONV_HALO - (KW - 1)

    def body(a_ref, g_ref, ah_ref, gh_ref, dw_ref, dwb_ref, lng_ref, lnb_ref, u_ref, c_ref, s_ref, ext_ref, sh_ref):
        i = pl.program_id(0)
        u = a_ref[...] * _sigmoid(g_ref[...])
        u_ref[...] = u
        uh = ah_ref[...] * _sigmoid(gh_ref[...])
        ext_ref[pl.ds(0, CONV_HALO), :] = jnp.where(i > 0, uh, 0.0)
        ext_ref[pl.ds(CONV_HALO, tt), :] = u
        _shifted_copies(ext_ref, sh_ref)
        for r0, nr, c0, nc in _tap_chunks(tt, D):
            cols = pl.ds(c0, nc)
            acc = jnp.zeros((nr, nc), F32) + dwb_ref[:, cols]
            for k in range(KW):
                acc = acc + dw_ref[pl.ds(k, 1), cols] * _rows_at(ext_ref, sh_ref, r0 + lead + k, nr, cols)
            c_ref[pl.ds(r0, nr), cols] = acc
        c = c_ref[...]
        mu = jnp.mean(c, axis=-1, keepdims=True)
        cen = c - mu
        var = jnp.mean(cen * cen, axis=-1, keepdims=True)
        n = cen * lax.rsqrt(var + LN_EPS) * lng_ref[...] + lnb_ref[...]
        s_ref[...] = (n * _sigmoid(n)).astype(BF16)

    halo = lambda col: pl.BlockSpec((CONV_HALO, D), lambda i: (jnp.maximum(i * hb - 1, 0), col))
    return pl.pallas_call(
        body, name=name, grid=(T // tt,),
        in_specs=[pl.BlockSpec((tt, D), lambda i: (i, 0)), pl.BlockSpec((tt, D), lambda i: (i, 1)), halo(0), halo(1),
                  pl.BlockSpec((KW, D), lambda i: (0, 0)), _vec_spec(D), _vec_spec(D), _vec_spec(D)],
        out_specs=[_row_spec(tt, D)] * 3,
        out_shape=[jax.ShapeDtypeStruct((T, D), F32), jax.ShapeDtypeStruct((T, D), F32), jax.ShapeDtypeStruct((T, D), BF16)],
        scratch_shapes=[pltpu.VMEM((tt + CONV_HALO, D), F32),
                        pltpu.VMEM((SUBLANES - 1, tt + CONV_HALO - SUBLANES, D), F32)],
        compiler_params=_cparams(("parallel",)),
    )(h1, h1, h1, h1, dw, dwb, lng, lnb)


def conv_bwd_ln(name, ds, c, lng, lnb):
    T, D = c.shape
    tr = min(ROW_TILE, T)
    nt = T // tr

    def body(ds_ref, c_ref, g_ref, b_ref, dc_ref, dg_ref, db_ref, cs_ref, acc_ref):
        i = pl.program_id(0)
        cv = c_ref[...]
        mu = jnp.mean(cv, axis=-1, keepdims=True)
        cen = cv - mu
        var = jnp.mean(cen * cen, axis=-1, keepdims=True)
        rstd = lax.rsqrt(var + LN_EPS)
        chat = cen * rstd
        n = chat * g_ref[...] + b_ref[...]
        sg = _sigmoid(n)
        dn = ds_ref[...] * (sg * (1.0 + n * (1.0 - sg)))
        dxh = dn * g_ref[...]
        m1 = jnp.mean(dxh, axis=-1, keepdims=True)
        m2 = jnp.mean(dxh * chat, axis=-1, keepdims=True)
        dc = rstd * (dxh - m1 - chat * m2)
        dc_ref[...] = dc

        @pl.when(i == 0)
        def _():
            acc_ref[...] = jnp.zeros_like(acc_ref)

        acc_ref[0] += _fold8(dn * chat)
        acc_ref[1] += _fold8(dn)
        acc_ref[2] += _fold8(dc)

        @pl.when(i == nt - 1)
        def _():
            dg_ref[...] = jnp.sum(acc_ref[0], axis=0, keepdims=True)
            db_ref[...] = jnp.sum(acc_ref[1], axis=0, keepdims=True)
            cs_ref[...] = jnp.sum(acc_ref[2], axis=0, keepdims=True)

    return pl.pallas_call(
        body, name=name, grid=(nt,),
        in_specs=[_row_spec(tr, D), _row_spec(tr, D), _vec_spec(D), _vec_spec(D)],
        out_specs=[_row_spec(tr, D), _vec_spec(D), _vec_spec(D), _vec_spec(D)],
        out_shape=[jax.ShapeDtypeStruct((T, D), F32)] + [jax.ShapeDtypeStruct((1, D), F32)] * 3,
        scratch_shapes=[pltpu.VMEM((3, 8, D), F32)],
        compiler_params=_cparams(("arbitrary",)),
    )(ds, c, lng, lnb)


def conv_bwd_taps(name, dc, u, h1, dw):
    T, D = dc.shape
    tt = min(CONV_TILE, T)
    nt = T // tt
    hb = tt // CONV_HALO
    nhb = T // CONV_HALO
    KW = dw.shape[0]
    lead = CONV_HALO - (KW - 1)

    def body(dc_ref, dcn_ref, u_ref, uh_ref, a_ref, g_ref, dw_ref, dh1_ref, db1_ref, ddw_ref,
             edc_ref, eu_ref, du_ref, accw_ref, accb_ref, shdc_ref, shu_ref):
        i = pl.program_id(0)

        @pl.when(i == 0)
        def _():
            accw_ref[...] = jnp.zeros_like(accw_ref)
            accb_ref[...] = jnp.zeros_like(accb_ref)

        edc_ref[pl.ds(0, tt), :] = dc_ref[...]
        edc_ref[pl.ds(tt, CONV_HALO), :] = jnp.where(i < nt - 1, dcn_ref[...], 0.0)
        eu_ref[pl.ds(0, CONV_HALO), :] = jnp.where(i > 0, uh_ref[...], 0.0)
        eu_ref[pl.ds(CONV_HALO, tt), :] = u_ref[...]
        _shifted_copies(edc_ref, shdc_ref)
        _shifted_copies(eu_ref, shu_ref)
        for r0, nr, c0, nc in _tap_chunks(tt, D, CONV_COLS_BWD):
            cols = pl.ds(c0, nc)
            dcv = dc_ref[pl.ds(r0, nr), cols]
            acc = jnp.zeros((nr, nc), F32)
            for k in range(KW):
                acc = acc + dw_ref[pl.ds(k, 1), cols] * _rows_at(edc_ref, shdc_ref, r0 + (KW - 1) - k, nr, cols)
                accw_ref[k, :, cols] += _fold8(dcv * _rows_at(eu_ref, shu_ref, r0 + lead + k, nr, cols))
            du_ref[pl.ds(r0, nr), cols] = acc
        du = du_ref[...]
        sg = _sigmoid(g_ref[...])
        da = du * sg
        dg = du * a_ref[...] * sg * (1.0 - sg)
        dh1_ref[:, pl.ds(0, D)] = da.astype(BF16)
        dh1_ref[:, pl.ds(D, D)] = dg.astype(BF16)
        accb_ref[:, pl.ds(0, D)] += _fold8(da)
        accb_ref[:, pl.ds(D, D)] += _fold8(dg)

        @pl.when(i == nt - 1)
        def _():
            db1_ref[...] = jnp.sum(accb_ref[...], axis=0, keepdims=True)
            ddw_ref[...] = jnp.sum(accw_ref[...], axis=1)

    return pl.pallas_call(
        body, name=name, grid=(nt,),
        in_specs=[_row_spec(tt, D),
                  pl.BlockSpec((CONV_HALO, D), lambda i: (jnp.minimum((i + 1) * hb, nhb - 1), 0)),
                  _row_spec(tt, D),
                  pl.BlockSpec((CONV_HALO, D), lambda i: (jnp.maximum(i * hb - 1, 0), 0)),
                  pl.BlockSpec((tt, D), lambda i: (i, 0)), pl.BlockSpec((tt, D), lambda i: (i, 1)),
                  pl.BlockSpec((KW, D), lambda i: (0, 0))],
        out_specs=[_row_spec(tt, 2 * D), _vec_spec(2 * D), pl.BlockSpec((KW, D), lambda i: (0, 0))],
        out_shape=[jax.ShapeDtypeStruct((T, 2 * D), BF16), jax.ShapeDtypeStruct((1, 2 * D), F32),
                   jax.ShapeDtypeStruct((KW, D), F32)],
        scratch_shapes=[pltpu.VMEM((tt + CONV_HALO, D), F32), pltpu.VMEM((tt + CONV_HALO, D), F32),
                        pltpu.VMEM((tt, D), F32), pltpu.VMEM((KW, 8, D), F32), pltpu.VMEM((8, 2 * D), F32)]
                       + [pltpu.VMEM((SUBLANES - 1, tt + CONV_HALO - SUBLANES, D), F32)] * 2,
        compiler_params=_cparams(("arbitrary",)),
    )(dc, dc, u, u, h1, h1, dw)


def _t5_bucket(dist):
    max_exact = REL_BUCKETS // 2
    large = max_exact + (np.log(np.maximum(dist, 1) / max_exact) / math.log(REL_MAX_DIST / max_exact)
                         * (REL_BUCKETS - max_exact)).astype(np.int32)
    large = np.minimum(large, REL_BUCKETS - 1)
    return np.where(dist < max_exact, dist, large).astype(np.int32)


def _bucket_table(dil):
    i = np.arange(BAND)[:, None]
    j = np.arange(2 * BAND)[None, :]
    delta = i - j + BAND
    return _t5_bucket(np.clip(delta, 0, None) * dil)


def bias_expand(name, rel_bias, dil):
    n_heads = rel_bias.shape[1]
    idx = jnp.asarray(_bucket_table(dil))

    def body(rel_ref, idx_ref, out_ref):
        h = pl.program_id(0)
        idxv = idx_ref[...]
        b = jnp.zeros((BAND, 2 * BAND), F32)
        for bk in range(REL_BUCKETS):
            b = jnp.where(idxv == bk, rel_ref[bk, h], b)
        out_ref[...] = b

    return pl.pallas_call(
        body, name=name, grid=(n_heads,),
        in_specs=[pl.BlockSpec(memory_space=pltpu.SMEM), pl.BlockSpec((BAND, 2 * BAND), lambda h: (0, 0))],
        out_specs=pl.BlockSpec((None, BAND, 2 * BAND), lambda h: (h, 0, 0)),
        out_shape=jax.ShapeDtypeStruct((n_heads, BAND, 2 * BAND), F32),
        compiler_params=_cparams(("arbitrary",)),
    )(rel_bias, idx)


def relbias_grad(name, dsb_list):
    n_heads = dsb_list[0].shape[0]
    idxs = [jnp.asarray(_bucket_table(d)) for _, d in BRANCHES]
    nb = len(BRANCHES)

    def body(*refs):
        ds_refs, idx_refs, out_ref = refs[:nb], refs[nb:2 * nb], refs[2 * nb]
        lane = lax.broadcasted_iota(jnp.int32, (1, 128), 1)
        row = jnp.zeros((1, 128), F32)
        for bk in range(REL_BUCKETS):
            tot = jnp.zeros((1, 1), F32)
            for ds_ref, idx_ref in zip(ds_refs, idx_refs):
                sel = jnp.where(idx_ref[...] == bk, ds_ref[...], 0.0)
                tot = tot + jnp.sum(jnp.sum(sel, axis=0, keepdims=True), axis=1, keepdims=True)
            row = jnp.where(lane == bk, tot, row)
        out_ref[...] = row

    return pl.pallas_call(
        body, name=name, grid=(n_heads,),
        in_specs=[pl.BlockSpec((None, BAND, 2 * BAND), lambda h: (h, 0, 0))] * nb
                 + [pl.BlockSpec((BAND, 2 * BAND), lambda h: (0, 0))] * nb,
        out_specs=pl.BlockSpec((None, 1, 128), lambda h: (h, 0, 0)),
        out_shape=jax.ShapeDtypeStruct((n_heads, 1, 128), F32),
        compiler_params=_cparams(("arbitrary",)),
    )(*dsb_list, *idxs)


def _band_mask():
    i = lax.broadcasted_iota(jnp.int32, (BAND, 2 * BAND), 0)
    j = lax.broadcasted_iota(jnp.int32, (BAND, 2 * BAND), 1)
    return (j >= i) & (j <= i + BAND), j


def _rep2(x):
    return jnp.concatenate([x, x], axis=1)


ATTN_TOKENS = 2048
MERGE_ROWS = 256


def _rows(ref, start, n, dil):
    if dil == 1:
        return ref[pl.ds(start, n), :]
    return ref[pl.ds(start, n, stride=dil), :]


def _set_rows(ref, start, n, dil, val):
    if dil == 1:
        ref[pl.ds(start, n), :] = val
    else:
        ref[pl.ds(start, n, stride=dil), :] = val


def _attn_specs(ct, n_heads, chunk_of):
    cur = lambda col0: pl.BlockSpec((ct, HEAD_DIM), lambda h, s: (chunk_of(s), col0 + h))
    prev = lambda col0: pl.BlockSpec((ct, HEAD_DIM), lambda h, s: (jnp.maximum(chunk_of(s) - 1, 0), col0 + h))
    bias = pl.BlockSpec((None, BAND, 2 * BAND), lambda h, s: (h, 0, 0))
    return cur, prev, bias


def _load_keys(kext_ref, vext_ref, base, k_ref, v_ref, kp_ref, vp_ref, r, dil, ct):
    lc = ct // dil
    kext_ref[pl.ds(base, BAND), :] = _rows(kp_ref, ct - BAND * dil + r, BAND, dil).astype(BF16)
    vext_ref[pl.ds(base, BAND), :] = _rows(vp_ref, ct - BAND * dil + r, BAND, dil).astype(BF16)
    kext_ref[pl.ds(base + BAND, lc), :] = _rows(k_ref, r, lc, dil).astype(BF16)
    vext_ref[pl.ds(base + BAND, lc), :] = _rows(v_ref, r, lc, dil).astype(BF16)


ATTN_GROUP = 4


def _two_level(dil):
    if dil > ATTN_GROUP and dil % ATTN_GROUP == 0:
        return ATTN_GROUP, dil // ATTN_GROUP
    return 1, dil


def _slot_rows(ct):
    return max(ct + BAND, ATTN_GROUP * (ct // ATTN_GROUP + BAND))


def _window_mask(band, jcol, a, c):
    if a > 0:
        return band
    return band & jnp.logical_or(jcol >= BAND, c > 0)


def attn_fwd(name, q, kv, biases):
    T, D = q.shape
    n_heads = D // HEAD_DIM
    ct = min(ATTN_TOKENS, T)
    n_chunks = T // ct
    nbr = len(BRANCHES)
    scale = HEAD_DIM ** -0.5
    nt_dims = (((1,), (1,)), ((), ()))
    nn_dims = (((1,), (0,)), ((), ()))

    n_in = 5

    def body(*refs):
        ins = refs[:n_in]
        b_refs = refs[n_in:n_in + nbr]
        o_ref, obf_ref, lse_ref = refs[n_in + nbr:n_in + nbr + 3]
        kext_ref, vext_ref, acc_ref, m_ref, l_ref = refs[n_in + nbr + 3:n_in + nbr + 8]
        tmp_in = refs[n_in + nbr + 8:n_in + nbr + 8 + n_in]
        tmp_out = refs[n_in + nbr + 8 + n_in:]
        c = pl.program_id(1)
        band, jcol = _band_mask()

        def residue(src, dst, slot, r, dil, cte, bias_v):
            q_ref, k_ref, v_ref, kp_ref, vp_ref = src
            lc = cte // dil
            base = slot * (BAND + lc)
            _load_keys(kext_ref, vext_ref, base, k_ref, v_ref, kp_ref, vp_ref, r, dil, cte)
            for a in range(lc // BAND):
                tok = r + a * BAND * dil
                qa = _rows(q_ref, tok, BAND, dil).astype(BF16)
                kw = kext_ref[pl.ds(base + a * BAND, 2 * BAND), :]
                vw = vext_ref[pl.ds(base + a * BAND, 2 * BAND), :]
                s = lax.dot_general(qa, kw, nt_dims, preferred_element_type=F32) * scale + bias_v
                s = jnp.where(_window_mask(band, jcol, a, c), s, NEG_BIG)
                m = jnp.max(s, axis=-1, keepdims=True)
                p = jnp.exp(s - m)
                den = jnp.sum(p, axis=-1, keepdims=True)
                pv = lax.dot_general(p.astype(BF16), vw, nn_dims, preferred_element_type=F32)
                _set_rows(dst[0], tok, BAND, dil, pv)
                _set_rows(dst[1], tok, BAND, dil, jnp.broadcast_to(m, (BAND, HEAD_DIM)))
                _set_rows(dst[2], tok, BAND, dil, jnp.broadcast_to(den, (BAND, HEAD_DIM)))

        for bi, (win, dil) in enumerate(BRANCHES):
            bias_v = b_refs[bi][...]
            dst = (acc_ref.at[bi], m_ref.at[bi], l_ref.at[bi])
            outer, inner = _two_level(dil)
            if outer == 1:
                for r in range(dil):
                    residue(ins, dst, r % ATTN_GROUP, r, dil, ct, bias_v)
            else:
                cte = ct // outer

                def group(r1, carry, bias_v=bias_v, dst=dst, outer=outer, inner=inner, cte=cte):
                    for t_ref, x_ref in zip(tmp_in, ins):
                        t_ref[...] = _rows(x_ref, r1, cte, outer)
                    for r2 in range(inner):
                        residue(tmp_in, tmp_out, r2 % ATTN_GROUP, r2, inner, cte, bias_v)
                    for t_ref, d_ref in zip(tmp_out, dst):
                        _set_rows(d_ref, r1, cte, outer, t_ref[...])
                    return carry

                lax.fori_loop(0, outer, group, 0)

        def merge(i, carry):
            rows = pl.ds(pl.multiple_of(i * MERGE_ROWS, MERGE_ROWS), MERGE_ROWS)
            ms = [m_ref[bi, rows, :] for bi in range(nbr)]
            m = functools.reduce(jnp.maximum, ms)
            ws = [jnp.exp(mb - m) for mb in ms]
            tot = functools.reduce(lambda x, y: x + y, [w * l_ref[bi, rows, :] for bi, w in enumerate(ws)])
            o = functools.reduce(lambda x, y: x + y, [w * acc_ref[bi, rows, :] for bi, w in enumerate(ws)]) / tot
            o_ref[rows, :] = o
            obf_ref[rows, :] = o.astype(BF16)
            lse_ref[rows, :] = m + jnp.log(tot)
            return carry

        lax.fori_loop(0, ct // min(MERGE_ROWS, ct), merge, 0)

    cur, prev, bias = _attn_specs(ct, n_heads, lambda s: s)
    small = (ct // ATTN_GROUP, HEAD_DIM)
    return pl.pallas_call(
        body, name=name, grid=(n_heads, n_chunks),
        in_specs=[cur(0), cur(0), cur(n_heads), prev(0), prev(n_heads)] + [bias] * nbr,
        out_specs=[cur(0)] * 3,
        out_shape=[jax.ShapeDtypeStruct((T, D), F32), jax.ShapeDtypeStruct((T, D), BF16), jax.ShapeDtypeStruct((T, D), F32)],
        scratch_shapes=[pltpu.VMEM((_slot_rows(ct), HEAD_DIM), BF16)] * 2 + [pltpu.VMEM((nbr, ct, HEAD_DIM), F32)] * 3
                       + [pltpu.VMEM(small, F32)] * (n_in + 3),
        compiler_params=_cparams(("arbitrary", "arbitrary")),
    )(q, kv, kv, kv, kv, *biases)


def attn_bwd(name, q, kv, do, o, lse, biases):
    T, D = q.shape
    n_heads = D // HEAD_DIM
    ct = min(ATTN_TOKENS, T)
    n_chunks = T // ct
    nbr = len(BRANCHES)
    scale = HEAD_DIM ** -0.5
    nt_dims = (((1,), (1,)), ((), ()))
    tn_dims = (((0,), (0,)), ((), ()))
    nn_dims = (((1,), (0,)), ((), ()))
    mrows = min(MERGE_ROWS, ct)
    n_src = 8
    n_acc = 5

    def body(q_ref, k_ref, v_ref, do_ref, o_ref, lse_ref, kp_ref, vp_ref, *rest):
        b_refs = rest[:nbr]
        dq_ref, dk_ref, dv_ref = rest[nbr:nbr + 3]
        dsb_refs = rest[nbr + 3:2 * nbr + 3]
        sc = rest[2 * nbr + 3:]
        kext_ref, vext_ref, dkext_ref, dvext_ref, dqa_ref, dka_ref, dva_ref, dsum_ref, ck_ref, cv_ref = sc[:10]
        tmp_in = sc[10:10 + n_src]
        tmp_acc = sc[10 + n_src:10 + n_src + n_acc]
        dsacc_ref = sc[10 + n_src + n_acc]
        step = pl.program_id(1)
        c = n_chunks - 1 - step
        band, jcol = _band_mask()

        @pl.when(step == 0)
        def _():
            ck_ref[...] = jnp.zeros_like(ck_ref)
            cv_ref[...] = jnp.zeros_like(cv_ref)
            for r in dsb_refs:
                r[...] = jnp.zeros_like(r)

        def prep(i, carry):
            rows = pl.ds(pl.multiple_of(i * mrows, mrows), mrows)
            dsum_ref[rows, :] = jnp.broadcast_to(jnp.sum(do_ref[rows, :] * o_ref[rows, :], axis=-1, keepdims=True), (mrows, HEAD_DIM))
            dqa_ref[rows, :] = jnp.zeros((mrows, HEAD_DIM), F32)
            dka_ref[rows, :] = ck_ref[rows, :]
            dva_ref[rows, :] = cv_ref[rows, :]
            ck_ref[rows, :] = jnp.zeros((mrows, HEAD_DIM), F32)
            cv_ref[rows, :] = jnp.zeros((mrows, HEAD_DIM), F32)
            return carry

        lax.fori_loop(0, ct // mrows, prep, 0)

        def residue(src, acc, slot, r, dil, cte, bias_v):
            sq, sk, sv, sdo, slse, sdsum, skp, svp = src
            adq, adk, adv, ack, acv = acc
            lc = cte // dil
            base = slot * (BAND + lc)
            _load_keys(kext_ref, vext_ref, base, sk, sv, skp, svp, r, dil, cte)
            dkext_ref[pl.ds(base, BAND + lc), :] = jnp.zeros((BAND + lc, HEAD_DIM), F32)
            dvext_ref[pl.ds(base, BAND + lc), :] = jnp.zeros((BAND + lc, HEAD_DIM), F32)
            for a in range(lc // BAND):
                tok = r + a * BAND * dil
                qa = _rows(sq, tok, BAND, dil).astype(BF16)
                doa = _rows(sdo, tok, BAND, dil).astype(BF16)
                kw = kext_ref[pl.ds(base + a * BAND, 2 * BAND), :]
                vw = vext_ref[pl.ds(base + a * BAND, 2 * BAND), :]
                s = lax.dot_general(qa, kw, nt_dims, preferred_element_type=F32) * scale + bias_v
                p = jnp.where(_window_mask(band, jcol, a, c), jnp.exp(s - _rep2(_rows(slse, tok, BAND, dil))), 0.0)
                dp = lax.dot_general(doa, vw, nt_dims, preferred_element_type=F32)
                ds = p * (dp - _rep2(_rows(sdsum, tok, BAND, dil)))
                dsacc_ref[slot] += ds
                dsb16 = ds.astype(BF16)
                dqw = lax.dot_general(dsb16, kw, nn_dims, preferred_element_type=F32) * scale
                _set_rows(adq, tok, BAND, dil, _rows(adq, tok, BAND, dil) + dqw)
                dkext_ref[pl.ds(base + a * BAND, 2 * BAND), :] += lax.dot_general(dsb16, qa, tn_dims, preferred_element_type=F32) * scale
                dvext_ref[pl.ds(base + a * BAND, 2 * BAND), :] += lax.dot_general(p.astype(BF16), doa, tn_dims, preferred_element_type=F32)

            _set_rows(adk, r, lc, dil, _rows(adk, r, lc, dil) + dkext_ref[pl.ds(base + BAND, lc), :])
            _set_rows(adv, r, lc, dil, _rows(adv, r, lc, dil) + dvext_ref[pl.ds(base + BAND, lc), :])
            last = cte - BAND * dil + r
            _set_rows(ack, last, BAND, dil, _rows(ack, last, BAND, dil) + dkext_ref[pl.ds(base, BAND), :])
            _set_rows(acv, last, BAND, dil, _rows(acv, last, BAND, dil) + dvext_ref[pl.ds(base, BAND), :])

        full_src = (q_ref, k_ref, v_ref, do_ref, lse_ref, dsum_ref, kp_ref, vp_ref)
        full_acc = (dqa_ref, dka_ref, dva_ref, ck_ref, cv_ref)
        for bi, (win, dil) in enumerate(BRANCHES):
            bias_v = b_refs[bi][...]
            dsacc_ref[...] = jnp.zeros_like(dsacc_ref)
            outer, inner = _two_level(dil)
            if outer == 1:
                for r in range(dil):
                    residue(full_src, full_acc, r % ATTN_GROUP, r, dil, ct, bias_v)
            else:
                cte = ct // outer

                def group(r1, carry, bias_v=bias_v, outer=outer, inner=inner, cte=cte):
                    for t_ref, x_ref in zip(tmp_in, full_src):
                        t_ref[...] = _rows(x_ref, r1, cte, outer)
                    for t_ref in tmp_acc:
                        t_ref[...] = jnp.zeros_like(t_ref)
                    for r2 in range(inner):
                        residue(tmp_in, tmp_acc, r2 % ATTN_GROUP, r2, inner, cte, bias_v)
                    for t_ref, a_ref in zip(tmp_acc, full_acc):
                        _set_rows(a_ref, r1, cte, outer, _rows(a_ref, r1, cte, outer) + t_ref[...])
                    return carry

                lax.fori_loop(0, outer, group, 0)
            dsb_refs[bi][...] += functools.reduce(lambda x, y: x + y, [dsacc_ref[s] for s in range(ATTN_GROUP)])

        dq_ref[...] = dqa_ref[...].astype(BF16)
        dk_ref[...] = dka_ref[...].astype(BF16)
        dv_ref[...] = dva_ref[...].astype(BF16)

    cur, prev, bias = _attn_specs(ct, n_heads, lambda s: n_chunks - 1 - s)
    small = (ct // ATTN_GROUP, HEAD_DIM)
    res = pl.pallas_call(
        body, name=name, grid=(n_heads, n_chunks),
        in_specs=[cur(0), cur(0), cur(n_heads), cur(0), cur(0), cur(0), prev(0), prev(n_heads)] + [bias] * nbr,
        out_specs=[cur(0)] * 3 + [bias] * nbr,
        out_shape=[jax.ShapeDtypeStruct((T, D), BF16)] * 3 + [jax.ShapeDtypeStruct((n_heads, BAND, 2 * BAND), F32)] * nbr,
        scratch_shapes=[pltpu.VMEM((_slot_rows(ct), HEAD_DIM), BF16)] * 2 + [pltpu.VMEM((_slot_rows(ct), HEAD_DIM), F32)] * 2
                       + [pltpu.VMEM((ct, HEAD_DIM), F32)] * 6 + [pltpu.VMEM(small, F32)] * (n_src + n_acc)
                       + [pltpu.VMEM((ATTN_GROUP, BAND, 2 * BAND), F32)],
        compiler_params=_cparams(("arbitrary", "arbitrary")),
    )(q, kv, kv, do, o, lse, kv, kv, *biases)
    return res[0], res[1], res[2], list(res[3:])


def _divisor_tile(n, cap, mult):
    if n <= cap:
        return n
    t = cap - cap % mult
    while n % t:
        t -= mult
    return t


def _tile2(R, C):
    return _divisor_tile(R, 512, 8), _divisor_tile(C, 1024, 128)


def half_cast(name, dw, core):
    S, R, C = dw.shape
    hr = R // 2
    tr, tc = _tile2(hr, C)
    nrb = hr // tr

    def body(c_ref, x_ref, o_ref):
        o_ref[...] = x_ref[...].astype(BF16)

    return pl.pallas_call(
        body, name=name,
        grid_spec=pltpu.PrefetchScalarGridSpec(
            num_scalar_prefetch=1, grid=(S, nrb, C // tc),
            in_specs=[pl.BlockSpec((None, tr, tc), lambda s, i, j, c: (s, (1 - c[0]) * nrb + i, j))],
            out_specs=pl.BlockSpec((None, tr, tc), lambda s, i, j, c: (s, i, j))),
        out_shape=jax.ShapeDtypeStruct((S, hr, C), BF16),
        compiler_params=_cparams(("parallel", "parallel", "parallel")),
    )(core, dw)


def pair_sum(name, dw, recv, core):
    S, R, C = dw.shape
    hr = R // 2
    tr, tc = _tile2(hr, C)
    nrb = hr // tr

    def body(c_ref, x_ref, r_ref, p_ref, pbf_ref):
        p = x_ref[...] + r_ref[...].astype(F32)
        p_ref[...] = p
        pbf_ref[...] = p.astype(BF16)

    out = pl.BlockSpec((None, tr, tc), lambda s, i, j, c: (s, i, j))
    return pl.pallas_call(
        body, name=name,
        grid_spec=pltpu.PrefetchScalarGridSpec(
            num_scalar_prefetch=1, grid=(S, nrb, C // tc),
            in_specs=[pl.BlockSpec((None, tr, tc), lambda s, i, j, c: (s, c[0] * nrb + i, j)), out],
            out_specs=[out, out]),
        out_shape=[jax.ShapeDtypeStruct((S, hr, C), F32), jax.ShapeDtypeStruct((S, hr, C), BF16)],
        compiler_params=_cparams(("parallel", "parallel", "parallel")),
    )(core, dw, recv)


def chip_sum(name, p, recv, chip, core):
    S, hr, C = p.shape
    tr, tc = _tile2(hr, C)
    nrb = hr // tr

    def body(chip_ref, core_ref, p_ref, r_ref, o_ref):
        acc = p_ref[...]
        for t in range(N_CHIPS - 1):
            acc = acc + r_ref[t].astype(F32)
        o_ref[...] = acc

    return pl.pallas_call(
        body, name=name,
        grid_spec=pltpu.PrefetchScalarGridSpec(
            num_scalar_prefetch=2, grid=(nrb, C // tc),
            in_specs=[pl.BlockSpec((None, tr, tc), lambda i, j, s, c: (s[0], i, j)),
                      pl.BlockSpec((N_CHIPS - 1, tr, tc), lambda i, j, s, c: (0, i, j))],
            out_specs=pl.BlockSpec((tr, tc), lambda i, j, s, c: (c[0] * nrb + i, j))),
        out_shape=jax.ShapeDtypeStruct((2 * hr, C), F32),
        compiler_params=_cparams(("parallel", "parallel")),
    )(chip, core, p, recv)


def adamw(name, w, g, m, v):
    R, C = w.shape
    tr, tc = _tile2(R, C)
    c1 = 1.0 - ADAM_B1 ** ADAM_STEP
    c2 = 1.0 - ADAM_B2 ** ADAM_STEP

    def body(w_ref, g_ref, m_ref, v_ref, d_ref, nm_ref, nv_ref):
        gv = g_ref[...]
        nm = ADAM_B1 * m_ref[...] + (1.0 - ADAM_B1) * gv
        nv = ADAM_B2 * v_ref[...] + (1.0 - ADAM_B2) * (gv * gv)
        nm_ref[...] = nm
        nv_ref[...] = nv
        d_ref[...] = -ADAM_LR * ((nm / c1) / (jnp.sqrt(nv / c2) + ADAM_EPS) + ADAM_WD * w_ref[...])

    spec = pl.BlockSpec((tr, tc), lambda i, j: (i, j))
    return pl.pallas_call(
        body, name=name, grid=(R // tr, C // tc), in_specs=[spec] * 4, out_specs=[spec] * 3,
        out_shape=[jax.ShapeDtypeStruct((R, C), F32)] * 3,
        compiler_params=_cparams(("parallel", "parallel")),
    )(w, g, m, v)


def adamw_layers(name, w, g_layers, m, v):
    nl, R, C = w.shape
    tr, tc = _tile2(R, C)
    ni, nj = R // tr, C // tc
    c1 = 1.0 - ADAM_B1 ** ADAM_STEP
    c2 = 1.0 - ADAM_B2 ** ADAM_STEP

    def body(w_ref, *rest):
        g_refs = rest[:nl]
        m_ref, v_ref, g_ref, d_ref, nm_ref, nv_ref = rest[nl:]
        layer = pl.program_id(0)
        gv = g_refs[0][...]
        for l in range(1, nl):
            gv = jnp.where(layer == l, g_refs[l][...], gv)
        nm = ADAM_B1 * m_ref[...] + (1.0 - ADAM_B1) * gv
        nv = ADAM_B2 * v_ref[...] + (1.0 - ADAM_B2) * (gv * gv)
        g_ref[...] = gv
        nm_ref[...] = nm
        nv_ref[...] = nv
        d_ref[...] = -ADAM_LR * ((nm / c1) / (jnp.sqrt(nv / c2) + ADAM_EPS) + ADAM_WD * w_ref[...])

    def g_spec(l):
        def index(layer, i, j):
            return (jnp.where(layer == l, i, jnp.where(layer < l, 0, ni - 1)),
                    jnp.where(layer == l, j, jnp.where(layer < l, 0, nj - 1)))
        return pl.BlockSpec((tr, tc), index)

    spec = pl.BlockSpec((None, tr, tc), lambda layer, i, j: (layer, i, j))
    return pl.pallas_call(
        body, name=name, grid=(nl, ni, nj),
        in_specs=[spec] + [g_spec(l) for l in range(nl)] + [spec] * 2, out_specs=[spec] * 4,
        out_shape=[jax.ShapeDtypeStruct((nl, R, C), F32)] * 4,
        compiler_params=_cparams(("arbitrary", "arbitrary", "arbitrary")),
    )(w, *g_layers, m, v)


def sum_devices(name, gathered):
    n, R, C = gathered.shape

    def body(x_ref, o_ref):
        acc = x_ref[0]
        for d in range(1, n):
            acc = acc + x_ref[d]
        o_ref[...] = acc

    return pl.pallas_call(
        body, name=name, in_specs=[pl.BlockSpec(memory_space=pltpu.VMEM)],
        out_specs=pl.BlockSpec(memory_space=pltpu.VMEM),
        out_shape=jax.ShapeDtypeStruct((R, C), F32),
    )(gathered)


def _place():
    x, y, c = lax.axis_index("x"), lax.axis_index("y"), lax.axis_index("c")
    return x, y, c


def _other_chips(x, y):
    return [(1 - x, y), (x, 1 - y), (1 - x, 1 - y)]


def all_gather8(name, block):
    R, C = block.shape

    def body(x_ref, out_ref, send_sems, recv_sems, local_sem):
        x, y, c = _place()
        me, sibling = (x, y, c), (x, y, 1 - c)
        chips = _other_chips(x, y)

        def rows(px, py, pc):
            return out_ref.at[4 * px + 2 * py + pc]

        def copy(k, blk, to, src=None):
            return pltpu.make_async_remote_copy(
                src_ref=rows(*blk) if src is None else src, dst_ref=rows(*blk),
                send_sem=send_sems.at[k], recv_sem=recv_sems.at[k], device_id=to, device_id_type=MESH)

        mine = pltpu.make_async_copy(x_ref, rows(*me), local_sem)
        mine.start()
        first = [copy(0, me, sibling, src=x_ref)]
        first += [copy(1 + j, me, (*chip, c), src=x_ref) for j, chip in enumerate(chips)]
        for cp in first:
            cp.start()
        passed = [copy(4 + j, (*chip, c), sibling) for j, chip in enumerate(chips)]
        for j, chip in enumerate(chips):
            copy(1 + j, (*chip, c), me).wait_recv()
            passed[j].start()
        copy(0, sibling, me).wait_recv()
        for j, chip in enumerate(chips):
            copy(4 + j, (*chip, 1 - c), me).wait_recv()
        for cp in first + passed:
            cp.wait_send()
        mine.wait()

    return pl.pallas_call(
        body, name=name, out_shape=jax.ShapeDtypeStruct((N_DEV, R, C), block.dtype),
        in_specs=[pl.BlockSpec(memory_space=pltpu.VMEM)], out_specs=pl.BlockSpec(memory_space=pltpu.VMEM),
        scratch_shapes=[pltpu.SemaphoreType.DMA((7,)), pltpu.SemaphoreType.DMA((7,)), pltpu.SemaphoreType.DMA],
    )(block)


_HBM = pl.BlockSpec(memory_space=pltpu.HBM)
_SEM = pl.BlockSpec(memory_space=pltpu.SEMAPHORE)
_DATAFLOW = pltpu.SideEffectType.DATAFLOW_SIDE_EFFECTING


def _in_hbm(a):
    return pltpu.with_memory_space_constraint(a, pltpu.HBM)


def split_start(name, srcs, lands, n_sem, plan):
    ns, nl = len(srcs), len(lands)

    def body(*refs):
        src, land = refs[:ns], refs[ns:ns + nl]
        send_sems, recv_sems = refs[ns + nl], refs[ns + nl + 1]
        token = refs[-1]
        outgoing, _ = plan(src, land, send_sems, recv_sems)
        for cp in outgoing:
            cp.start()
        token[...] = jnp.zeros_like(token)

    bufs = list(srcs) + list(lands)
    res = pl.pallas_call(
        body, name=name,
        out_shape=(pltpu.SemaphoreType.DMA((n_sem,)), pltpu.SemaphoreType.DMA((n_sem,)),
                   *[pltpu.HBM(b.shape, b.dtype) for b in bufs], jax.ShapeDtypeStruct((8, 128), F32)),
        in_specs=[_HBM] * (ns + nl),
        out_specs=(_SEM, _SEM, *[_HBM] * (ns + nl), pl.BlockSpec(memory_space=pltpu.VMEM)),
        input_output_aliases={i: 2 + i for i in range(ns + nl)},
        compiler_params=pltpu.CompilerParams(has_side_effects=_DATAFLOW),
    )(*[_in_hbm(b) for b in bufs])
    return res[0], res[1], list(res[2:2 + ns]), list(res[2 + ns:2 + ns + nl]), res[-1]


def split_wait(name, started, after, plan):
    send_sems, recv_sems, srcs, lands, _ = started
    ns, nl = len(srcs), len(lands)

    def body(*refs):
        src, land = refs[:ns], refs[ns:ns + nl]
        send, recv = refs[ns + nl], refs[ns + nl + 1]
        outgoing, incoming = plan(src, land, send, recv)
        for cp in outgoing:
            cp.wait_send()
        for cp in incoming:
            cp.wait_recv()

    bufs = list(srcs) + list(lands)
    res = pl.pallas_call(
        body, name=name,
        out_shape=tuple(pltpu.HBM(b.shape, b.dtype) for b in bufs),
        in_specs=[_HBM] * (ns + nl) + [_SEM, _SEM, pl.BlockSpec(memory_space=pl.ANY)],
        out_specs=tuple([_HBM] * (ns + nl)),
        input_output_aliases={i: i for i in range(ns + nl)},
        compiler_params=pltpu.CompilerParams(has_side_effects=_DATAFLOW),
    )(*bufs, send_sems, recv_sems, after)
    return list(res[ns:])


def _rcopy(src, dst, send_sems, ks, recv_sems, kr, device):
    return pltpu.make_async_remote_copy(src_ref=src, dst_ref=dst, send_sem=send_sems.at[ks], recv_sem=recv_sems.at[kr],
                                        device_id=device, device_id_type=MESH)


def _half_rows(ref, h):
    hr = ref.shape[0] // 2
    return ref.at[pl.ds(h * hr, hr)]


def _gather_plan(src, land, send_sems, recv_sems):
    x, y, c = _place()
    me_chip = 2 * x + y
    chips = _other_chips(x, y)
    outgoing, incoming = [], []
    for w, buf in enumerate(land):
        mine = _half_rows(buf.at[me_chip], c)
        for t, chip in enumerate(chips):
            slot = 2 * chip[0] + chip[1]
            for cc in range(2):
                outgoing.append(_rcopy(mine, mine, send_sems, 6 * w + 2 * t + cc, recv_sems, 6 * w + 2 * t + c, (*chip, cc)))
                theirs = _half_rows(buf.at[slot], cc)
                incoming.append(_rcopy(theirs, theirs, send_sems, 6 * w + 2 * t + cc, recv_sems, 6 * w + 2 * t + cc, (*chip, cc)))
    return outgoing, incoming


def _gather_half_plan(src, land, send_sems, recv_sems):
    x, y, c = _place()
    me_chip = 2 * x + y
    mine = _half_rows(land[0].at[me_chip], c)
    outgoing, incoming = [], []
    for t, chip in enumerate(_other_chips(x, y)):
        outgoing.append(_rcopy(mine, mine, send_sems, t, recv_sems, t, (*chip, c)))
        theirs = _half_rows(land[0].at[2 * chip[0] + chip[1]], c)
        incoming.append(_rcopy(theirs, theirs, send_sems, t, recv_sems, t, (*chip, c)))
    return outgoing, incoming


def _forward_halves_plan(src, land, send_sems, recv_sems):
    x, y, c = _place()
    outgoing, incoming = [], []
    for t, chip in enumerate(_other_chips(x, y)):
        slot = land[0].at[2 * chip[0] + chip[1]]
        got, missing = _half_rows(slot, c), _half_rows(slot, 1 - c)
        outgoing.append(_rcopy(got, got, send_sems, t, recv_sems, t, (x, y, 1 - c)))
        incoming.append(_rcopy(missing, missing, send_sems, t, recv_sems, t, (x, y, 1 - c)))
    return outgoing, incoming


def _all_to_all_plan(src, land, send_sems, recv_sems):
    x, y, c = _place()
    mine = land[0].at[4 * x + 2 * y + c]
    outgoing, incoming = [], []
    for k in range(1, N_DEV):
        fx, fy, fc = (k >> 2) & 1, (k >> 1) & 1, k & 1
        px, py, pc = (1 - x if fx else x), (1 - y if fy else y), (1 - c if fc else c)
        outgoing.append(_rcopy(mine, mine, send_sems, k - 1, recv_sems, k - 1, (px, py, pc)))
        theirs = land[0].at[4 * px + 2 * py + pc]
        incoming.append(_rcopy(theirs, theirs, send_sems, k - 1, recv_sems, k - 1, (px, py, pc)))
    return outgoing, incoming


def place_block(name, block, slot, n_slots):
    R, C = block.shape

    def body(slot_ref, x_ref, o_ref):
        o_ref[...] = x_ref[...]

    return pl.pallas_call(
        body, name=name,
        grid_spec=pltpu.PrefetchScalarGridSpec(
            num_scalar_prefetch=1, grid=(1,),
            in_specs=[pl.BlockSpec((R, C), lambda i, s: (0, 0))],
            out_specs=pl.BlockSpec((None, R, C), lambda i, s: (s[0], 0, 0))),
        out_shape=jax.ShapeDtypeStruct((n_slots, R, C), block.dtype),
        compiler_params=_cparams(("arbitrary",)),
    )(slot, block)


def _swap_plan(src, land, send_sems, recv_sems):
    x, y, c = _place()
    cp = _rcopy(src[0], land[0], send_sems, 0, recv_sems, 0, (x, y, 1 - c))
    return [cp], [cp]


def _scatter_plan(src, land, send_sems, recv_sems):
    x, y, c = _place()
    cps = [_rcopy(src[0].at[2 * chip[0] + chip[1]], land[0].at[t], send_sems, t, recv_sems, t, (*chip, c))
           for t, chip in enumerate(_other_chips(x, y))]
    return cps, cps


def _share_plan(src, land, send_sems, recv_sems):
    x, y, c = _place()
    mine, theirs = _half_rows(land[0], c), _half_rows(land[0], 1 - c)
    return ([_rcopy(mine, mine, send_sems, 0, recv_sems, 0, (x, y, 1 - c))],
            [_rcopy(theirs, theirs, send_sems, 0, recv_sems, 0, (x, y, 1 - c))])


def place_shard(name, shard, chip, deps=(), layer=None):
    R, C = shard.shape[-2:]
    tr, tc = _tile2(R, C)

    def body(chip_ref, x_ref, *rest):
        rest[-1][...] = x_ref[...].astype(BF16)

    if layer is None:
        src = pl.BlockSpec((tr, tc), lambda i, j, s: (i, j))
    else:
        src = pl.BlockSpec((None, tr, tc), lambda i, j, s: (layer, i, j))
    return pl.pallas_call(
        body, name=name,
        grid_spec=pltpu.PrefetchScalarGridSpec(
            num_scalar_prefetch=1, grid=(R // tr, C // tc),
            in_specs=[src] + [pl.BlockSpec(d.shape, lambda i, j, s: (0, 0)) for d in deps],
            out_specs=pl.BlockSpec((None, tr, tc), lambda i, j, s: (s[0], i, j))),
        out_shape=jax.ShapeDtypeStruct((N_CHIPS, R, C), BF16),
        compiler_params=_cparams(("parallel", "parallel")),
    )(chip, shard, *deps)


class GradExchange:
    SCATTER_TICKS = 2

    def __init__(self, chip1, core, shard, mom, vel):
        self.chip1, self.core, self.shard, self.mom, self.vel = chip1, core, shard, mom, vel
        self.inflight, self.tokens, self.results = [], [], {}

    def take_deps(self):
        deps, self.tokens = self.tokens, []
        return deps

    def _start(self, name, srcs, lands, n_sem, plan):
        started = split_start(name, srcs, lands, n_sem, plan)
        self.tokens.append(started[-1])
        return started

    def add(self, n, dw):
        S, R, C = dw.shape
        to_sibling = half_cast("rs_cast_" + n, dw, self.core)
        started = self._start("rs_swap_start_" + n, [to_sibling], [lax.empty((S, R // 2, C), BF16)], 1, _swap_plan)
        self.inflight.append(dict(n=n, dw=dw, stage=0, started=started, ticks=0))

    def tick(self, after):
        for it in self.inflight:
            n = it["n"]
            if it["stage"] == 0:
                (recv,) = split_wait("rs_swap_wait_" + n, it["started"], after, _swap_plan)
                p, pbf = pair_sum("rs_pair_sum_" + n, it["dw"], recv, self.core)
                S, hr, C = pbf.shape
                it.update(stage=1, p=p, ticks=0,
                          started=self._start("rs_scatter_start_" + n, [pbf], [lax.empty((N_CHIPS - 1, hr, C), BF16)], 3, _scatter_plan))
            elif it["stage"] == 1:
                it["ticks"] += 1
                if it["ticks"] >= self.SCATTER_TICKS:
                    (recv,) = split_wait("rs_scatter_wait_" + n, it["started"], after, _scatter_plan)
                    half = chip_sum("rs_chip_sum_" + n, it["p"], recv, self.chip1, self.core)
                    it.update(stage=2, started=self._start("rs_share_start_" + n, [], [half], 1, _share_plan))
            elif it["stage"] == 2:
                (grad,) = split_wait("rs_share_wait_" + n, it["started"], after, _share_plan)
                if n in self.shard:
                    self.results[n] = (grad,) + tuple(adamw("adamw_" + n, self.shard[n], grad, self.mom[n], self.vel[n]))
                else:
                    self.results[n] = (grad,)
                it["stage"] = 3
        self.inflight = [it for it in self.inflight if it["stage"] < 3]

    def flush(self, after):
        while self.inflight:
            self.tick(after)


def _pack(arrs):
    parts = []
    for a in arrs:
        flat = a.reshape(-1).astype(F32)
        n = flat.shape[0]
        padded = -(-n // 1024) * 1024
        parts.append(jnp.pad(flat, (0, padded - n)).reshape(padded // 128, 128))
    return jnp.concatenate(parts, axis=0)


def _unpack(buf, shapes):
    out, row = [], 0
    for shp in shapes:
        n = int(np.prod(shp))
        rows = -(-n // 1024) * 8
        out.append(buf[row:row + rows].reshape(-1)[:n].reshape(shp))
        row += rows
    return out


def _bias_epi(acc, b):
    return (acc + b,)


def local_step(x, target, W, P, ex, first_deps=(), on_small=None):
    T, D = x.shape
    g = {}
    plain = lambda acc: (acc,)

    (h1,) = mm_nn("pw1_fwd", x, W("pw1", x), "col", _bias_epi, [F32],
                  extras=[(P["pw1_b"], "row")] + [(d, "dep") for d in first_deps])
    u, cpre, s = conv_fwd("conv_fwd", h1, P["dw_w"], P["dw_b"], P["cln_g"], P["cln_b"])
    (mix0,) = mm_nn("pw2_fwd", s, W("pw2", s), "row", _bias_epi, [F32], extras=[(P["pw2_b"], "row")])
    ln = [None] * 4
    gam = [P["ln_mix_g"][0:1], P["ln_mlp_g"][0:1], P["ln_mix_g"][1:2], P["ln_mlp_g"][1:2]]
    bet = [P["ln_mix_b"][0:1], P["ln_mlp_b"][0:1], P["ln_mix_b"][1:2], P["ln_mlp_b"][1:2]]
    ln[0] = ln_fwd("ln0_fwd", mix0, x)(gam[0], bet[0])

    def mlp_fwd(tag, i_ln, n1, n2):
        xhat, rstd, xbf = ln[i_ln]

        def up_epi(acc):
            r = jnp.maximum(acc, 0.0)
            return r * r, r

        hid, relu = mm_nn(tag + "_up", xbf, W(n1, xbf), "col", up_epi, [BF16, BF16])
        (mlp,) = mm_nn(tag + "_down", hid, W(n2, hid), "row", plain, [F32])
        ln[i_ln + 1] = ln_fwd(tag + "_ln", mlp, xhat, gam[i_ln], bet[i_ln])(gam[i_ln + 1], bet[i_ln + 1])
        return hid, relu

    hid0 = mlp_fwd("mlp0", 0, "w1_0", "w2_0")

    x2bf = ln[1][2]
    (kv,) = mm_nn("kv_fwd", x2bf, W("kv", x2bf), "col", plain, [F32])
    (q,) = mm_nn("q_fwd", x2bf, W("wq", kv), "row", plain, [F32])
    biases = [bias_expand("bias_d%d" % d, P["rel_bias"], d) for _, d in BRANCHES]
    assert all(win // d == BAND and min(ATTN_TOKENS, T) % (BAND * d) == 0 for win, d in BRANCHES)
    o, obf, lse = attn_fwd("attn_fwd", q, kv, biases)
    (attn,) = mm_nn("wo_fwd", obf, W("wo", obf), "row", plain, [F32])
    ln[2] = ln_fwd("ln2_fwd", attn, ln[1][0], gam[1], bet[1])(gam[2], bet[2])
    hid1 = mlp_fwd("mlp1", 2, "w1_1", "w2_1")

    dr3, dr3bf, g["ln_mlp_g1"], g["ln_mlp_b1"], _, loss_sum = ln_bwd(
        "ln3_bwd", ln[3][0], ln[3][1], gam[3], target=target, beta=bet[3])

    def dw_step(name, wname, a, cot, axis):
        dw = mm_tn(name, a, cot, W(wname, a).shape, axis, deps=ex.take_deps())
        ex.tick(dw)
        ex.add(wname, dw)

    def dx_step(name, cot, wname, axis, epilogue, out_dtype, extras):
        deps = [(d, "dep") for d in ex.take_deps()]
        (out,) = mm_nt(name, cot, W(wname, cot), axis, epilogue, [out_dtype], extras=list(extras) + deps)
        ex.tick(out)
        return out

    def mlp_bwd(tag, i_ln, n1, n2, hid_relu, dr, drbf):
        xbf = ln[i_ln][2]
        hid, relu = hid_relu
        dw_step(tag + "_dw2", n2, hid, drbf, "row")
        dp = dx_step(tag + "_dhid", drbf, n2, "row", lambda acc, r: (acc * (2.0 * r.astype(F32)),), BF16, [(relu, "tile")])
        dw_step(tag + "_dw1", n1, xbf, dp, "col")
        return dx_step(tag + "_dx", dp, n1, "col", lambda acc, e: (acc + ALPHA * e,), F32, [(dr, "tile")])

    dx3 = mlp_bwd("mlp1", 2, "w1_1", "w2_1", hid1, dr3, dr3bf)
    dr2, dr2bf, g["ln_mix_g1"], g["ln_mix_b1"], _ = ln_bwd("ln2_bwd", ln[2][0], ln[2][1], gam[2], dy=dx3)
    dw_step("wo_dw", "wo", obf, dr2bf, "row")
    do = dx_step("wo_dx", dr2bf, "wo", "row", plain, F32, [])
    dq, dk, dv, dsbs = attn_bwd("attn_bwd", q, kv, do, o, lse, biases)
    g["rel_bias"] = relbias_grad("relbias_grad", dsbs)[:, 0, :REL_BUCKETS].T
    dkv = jnp.concatenate([dk, dv], axis=1)
    dw_step("wq_dw", "wq", x2bf, dq, "row")
    dw_step("kv_dw", "kv", x2bf, dkv, "col")
    dx2a = dx_step("wq_dx", dq, "wq", "row", lambda acc, e: (acc + ALPHA * e,), F32, [(dr2, "tile")])
    dx2 = dx_step("kv_dx", dkv, "kv", "col", lambda acc, e: (acc + e,), F32, [(dx2a, "tile")])

    dr1, dr1bf, g["ln_mlp_g0"], g["ln_mlp_b0"], _ = ln_bwd("ln1_bwd", ln[1][0], ln[1][1], gam[1], dy=dx2)
    dx1 = mlp_bwd("mlp0", 0, "w1_0", "w2_0", hid0, dr1, dr1bf)
    dr0, dr0bf, g["ln_mix_g0"], g["ln_mix_b0"], g["pw2_b"] = ln_bwd("ln0_bwd", ln[0][0], ln[0][1], gam[0], dy=dx1)

    dw_step("pw2_dw", "pw2", s, dr0bf, "row")
    ds = dx_step("pw2_dx", dr0bf, "pw2", "row", plain, F32, [])
    dc, g["cln_g"], g["cln_b"], g["dw_b"] = conv_bwd_ln("conv_bwd_ln", ds, cpre, P["cln_g"], P["cln_b"])
    dh1, g["pw1_b"], g["dw_w"] = conv_bwd_taps("conv_bwd_taps", dc, u, h1, P["dw_w"])
    if on_small is not None:
        on_small(g)
    dw_step("pw1_dw", "pw1", x, dh1, "col")
    dx = dx_step("pw1_dx", dh1, "pw1", "col", lambda acc, e: (acc + ALPHA * e,), F32, [(dr0, "tile")])
    return loss_sum, dx, g


BIG = ("pw1", "pw2", "w1_0", "w2_0", "kv", "wq", "wo", "w1_1", "w2_1")


def kernel(x, conv_pw1_w, conv_pw1_b, conv_dw_w, conv_dw_b, conv_ln_g, conv_ln_b, conv_pw2_w, conv_pw2_b, w_kv, attn_wq, attn_wo, rel_bias, mlp_w1, mlp_w2, ln_mix_g, ln_mix_b, ln_mlp_g, ln_mlp_b, loss_target, m_conv_pw1_w, m_conv_pw1_b, m_conv_dw_w, m_conv_dw_b, m_conv_ln_g, m_conv_ln_b, m_conv_pw2_w, m_conv_pw2_b, m_w_kv, m_attn_wq, m_attn_wo, m_rel_bias, m_mlp_w1, m_mlp_w2, m_ln_mix_g, m_ln_mix_b, m_ln_mlp_g, m_ln_mlp_b, v_conv_pw1_w, v_conv_pw1_b, v_conv_dw_w, v_conv_dw_b, v_conv_ln_g, v_conv_ln_b, v_conv_pw2_w, v_conv_pw2_b, v_w_kv, v_attn_wq, v_attn_wo, v_rel_bias, v_mlp_w1, v_mlp_w2, v_ln_mix_g, v_ln_mix_b, v_ln_mlp_g, v_ln_mlp_b):
    _, T, D = x.shape
    xi, yi, ci = _place()
    chip = 2 * xi + yi
    core = jnp.reshape(ci, (1,)).astype(jnp.int32)
    chip1 = jnp.reshape(chip, (1,)).astype(jnp.int32)

    def two_d(a):
        return a.reshape(a.shape[-2:])

    shard = {"pw1": two_d(conv_pw1_w), "pw2": two_d(conv_pw2_w), "kv": w_kv, "wq": two_d(attn_wq), "wo": two_d(attn_wo)}
    mom = {"pw1": two_d(m_conv_pw1_w), "pw2": two_d(m_conv_pw2_w), "kv": m_w_kv, "wq": two_d(m_attn_wq), "wo": two_d(m_attn_wo)}
    vel = {"pw1": two_d(v_conv_pw1_w), "pw2": two_d(v_conv_pw2_w), "kv": v_w_kv, "wq": two_d(v_attn_wq), "wo": two_d(v_attn_wo)}
    stacked = {"w1_0": (mlp_w1, 0), "w1_1": (mlp_w1, 1), "w2_0": (mlp_w2, 0), "w2_1": (mlp_w2, 1)}

    sharded_small = [conv_pw1_b, conv_dw_w[0], conv_dw_b, conv_ln_g, conv_ln_b, conv_pw2_b]
    sh_shapes = [a.shape for a in sharded_small]
    small_all = all_gather8("gather_small", _pack(sharded_small))
    per_chip = [_unpack(small_all[2 * j], sh_shapes) for j in range(N_CHIPS)]
    full = [jnp.concatenate([per_chip[j][i] for j in range(N_CHIPS)], axis=-1) for i in range(len(sharded_small))]

    started = {}
    for n in BIG:
        deps = [started[prev][-1] for prev in list(started)[-1:]] or [full[0]]
        src, layer = stacked.get(n, (shard.get(n), None))
        land = place_shard("place_" + n, src, chip1, deps, layer)
        if n == BIG[0]:
            started[n] = split_start("gather_start_" + n, [], [land], 3, _gather_half_plan)
        else:
            started[n] = split_start("gather_start_" + n, [], [land], 6, _gather_plan)
    gathered = {}

    def W(n, after):
        if n not in gathered:
            if n == BIG[0]:
                lands = split_wait("gather_wait_" + n, started[n], started[BIG[-1]][-1], _gather_half_plan)
                passed = split_start("gather_pass_start_" + n, [], lands, 3, _forward_halves_plan)
                (gathered[n],) = split_wait("gather_pass_wait_" + n, passed, passed[-1], _forward_halves_plan)
            else:
                (gathered[n],) = split_wait("gather_wait_" + n, started[n], after, _gather_plan)
        return gathered[n]

    P = dict(pw1_b=full[0], dw_w=full[1], dw_b=full[2], cln_g=full[3], cln_b=full[4], pw2_b=full[5],
             rel_bias=rel_bias, ln_mix_g=ln_mix_g, ln_mix_b=ln_mix_b, ln_mlp_g=ln_mlp_g, ln_mlp_b=ln_mlp_b)

    ex = GradExchange(chip1, core, shard, mom, vel)

    small_names = ["pw1_b", "dw_w", "dw_b", "cln_g", "cln_b", "pw2_b", "rel_bias",
                   "ln_mix_g0", "ln_mix_g1", "ln_mix_b0", "ln_mix_b1", "ln_mlp_g0", "ln_mlp_g1", "ln_mlp_b0", "ln_mlp_b1"]
    small = {}

    def on_small(g):
        grads = [g[n] for n in small_names]
        small["shapes"] = [a.shape for a in grads]
        device1 = jnp.reshape(4 * xi + 2 * yi + ci, (1,)).astype(jnp.int32)
        land = place_block("place_small_grads", _pack(grads), device1, N_DEV)
        small["started"] = split_start("small_grads_start", [], [land], N_DEV - 1, _all_to_all_plan)
        ex.tokens.append(small["started"][-1])

    loss_sum, dx, g = local_step(x.reshape(T, D), loss_target.reshape(T, D), W, P, ex,
                                 first_deps=[started[n][-1] for n in BIG], on_small=on_small)
    loss = (0.5 / D) * lax.psum(loss_sum[0, 0], ("x", "y", "c"))
    (all_small,) = split_wait("small_grads_wait", small["started"], dx, _all_to_all_plan)
    summed = sum_devices("small_grad_sum", all_small)
    sg = dict(zip(small_names, _unpack(summed, small["shapes"])))

    def my_cols(a, width):
        return lax.dynamic_slice_in_dim(a, chip * width, width, axis=a.ndim - 1)

    small_g = [my_cols(sg["pw1_b"], conv_pw1_b.shape[-1]),
               my_cols(sg["dw_w"], conv_dw_w.shape[-1])[None],
               my_cols(sg["dw_b"], conv_dw_b.shape[-1]), my_cols(sg["cln_g"], conv_ln_g.shape[-1]),
               my_cols(sg["cln_b"], conv_ln_b.shape[-1]), my_cols(sg["pw2_b"], conv_pw2_b.shape[-1]),
               sg["rel_bias"],
               jnp.concatenate([sg["ln_mix_g0"], sg["ln_mix_g1"]], axis=0),
               jnp.concatenate([sg["ln_mix_b0"], sg["ln_mix_b1"]], axis=0),
               jnp.concatenate([sg["ln_mlp_g0"], sg["ln_mlp_g1"]], axis=0),
               jnp.concatenate([sg["ln_mlp_b0"], sg["ln_mlp_b1"]], axis=0)]
    small_w = [conv_pw1_b, conv_dw_w, conv_dw_b, conv_ln_g, conv_ln_b, conv_pw2_b, rel_bias, ln_mix_g, ln_mix_b, ln_mlp_g, ln_mlp_b]
    small_m = [m_conv_pw1_b, m_conv_dw_w, m_conv_dw_b, m_conv_ln_g, m_conv_ln_b, m_conv_pw2_b, m_rel_bias, m_ln_mix_g, m_ln_mix_b, m_ln_mlp_g, m_ln_mlp_b]
    small_v = [v_conv_pw1_b, v_conv_dw_w, v_conv_dw_b, v_conv_ln_g, v_conv_ln_b, v_conv_pw2_b, v_rel_bias, v_ln_mix_g, v_ln_mix_b, v_ln_mlp_g, v_ln_mlp_b]
    sw_shapes = [a.shape for a in small_w]
    small_g = [a.reshape(s) for a, s in zip(small_g, sw_shapes)]
    upd_small = adamw("adamw_small", _pack(small_w), _pack(small_g), _pack(small_m), _pack(small_v))
    sd, snm, snv = (_unpack(b, sw_shapes) for b in upd_small)

    res_w1 = adamw_layers("adamw_w1", mlp_w1, [ex.results["w1_0"][0], ex.results["w1_1"][0]], m_mlp_w1, v_mlp_w1)
    res_w2 = adamw_layers("adamw_w2", mlp_w2, [ex.results["w2_0"][0], ex.results["w2_1"][0]], m_mlp_w2, v_mlp_w2)
    ex.flush(res_w2[1])

    def big_out(k):
        one = {n: ex.results[n][k] for n in shard}
        return dict(pw1=one["pw1"][None], pw2=one["pw2"][None], kv=one["kv"], wq=one["wq"][None], wo=one["wo"][None],
                    w1=res_w1[k], w2=res_w2[k])

    def ordered(big, small):
        return [big["pw1"], small[0], small[1], small[2], small[3], small[4], big["pw2"], small[5], big["kv"], big["wq"],
                big["wo"], small[6], big["w1"], big["w2"], small[7], small[8], small[9], small[10]]

    grads = ordered(big_out(0), small_g)
    deltas = ordered(big_out(1), sd)
    new_m = ordered(big_out(2), snm)
    new_v = ordered(big_out(3), snv)
    return (loss, dx.reshape(1, T, D), *grads, *deltas, *new_m, *new_v)
```

```python
import functools
import math

import numpy as np
import jax
import jax.numpy as jnp
from jax import lax
from jax.experimental import pallas as pl
from jax.experimental.pallas import tpu as pltpu

F32 = jnp.float32
BF16 = jnp.bfloat16

HEAD_DIM = 128
BAND = 128
BRANCHES = ((128, 1), (512, 4), (2048, 16))
CONV_WIDTH = 31
CONV_HALO = 32
REL_BUCKETS = 32
REL_MAX_DIST = 2048
DEPTH = 2
ALPHA = (2 * DEPTH) ** 0.25
LN_EPS = 1e-5
ADAM_LR, ADAM_B1, ADAM_B2, ADAM_EPS, ADAM_WD, ADAM_STEP = 0.001, 0.9, 0.999, 1e-08, 0.01, 10

N_CHIPS = 4
N_DEV = 8
MESH = pl.DeviceIdType.MESH
VMEM_LIMIT_BYTES = 56 * 1024 * 1024
MM_TM, MM_TN, MM_TK = 1024, 1024, 2048
ROW_TILE = 256
CONV_TILE = 128
NEG_BIG = -1e30


def _cparams(sem):
    return pltpu.CompilerParams(dimension_semantics=sem, vmem_limit_bytes=VMEM_LIMIT_BYTES)


def _sigmoid(x):
    return 1.0 / (1.0 + jnp.exp(-x))


def _wspec(wshape, axis, br, bc, rsel, csel):
    _, R, C = wshape
    if axis == "col":
        if bc > C:
            assert bc % C == 0, (wshape, bc)
            return pl.BlockSpec((bc // C, br, C), lambda *g: (csel(*g), rsel(*g), 0))
        nb = C // bc
        assert nb * bc == C, (wshape, bc)
        return pl.BlockSpec((None, br, bc), lambda *g: (csel(*g) // nb, rsel(*g), csel(*g) % nb))
    if br > R:
        assert br % R == 0, (wshape, br)
        return pl.BlockSpec((br // R, R, bc), lambda *g: (rsel(*g), 0, csel(*g)))
    nb = R // br
    assert nb * br == R, (wshape, br)
    return pl.BlockSpec((None, br, bc), lambda *g: (rsel(*g) // nb, rsel(*g) % nb, csel(*g)))


def _join_shards(b, axis):
    if b.ndim == 2:
        return b
    if axis == "row":
        return b.reshape(b.shape[0] * b.shape[1], b.shape[2])
    return jnp.concatenate([b[s] for s in range(b.shape[0])], axis=1)


def _split_shards(r, shape, axis):
    if len(shape) == 2:
        return r
    if axis == "row":
        return r.reshape(shape)
    return jnp.stack([r[:, s * shape[2]:(s + 1) * shape[2]] for s in range(shape[0])])


def _full_dims(wshape, axis):
    _, R, C = wshape
    return (R, N_CHIPS * C) if axis == "col" else (N_CHIPS * R, C)


def _mm_body(nk, kinds, n_out, dims, epilogue, axis):
    n_extra = len(kinds)

    def body(*refs):
        a_ref, b_ref = refs[0], refs[1]
        extra = [r for r, kind in zip(refs[2:2 + n_extra], kinds) if kind != "dep"]
        outs = refs[2 + n_extra:2 + n_extra + n_out]
        part = lax.dot_general(a_ref[...].astype(BF16), _join_shards(b_ref[...], axis).astype(BF16), (dims, ((), ())),
                               preferred_element_type=F32)

        def write(res):
            for r, o in zip(res, outs):
                o[...] = _split_shards(r, o.shape, axis).astype(o.dtype)

        if nk == 1:
            write(epilogue(part, *[e[...] for e in extra]))
            return
        acc_ref = refs[2 + n_extra + n_out]
        k = pl.program_id(2)

        @pl.when(k == 0)
        def _():
            acc_ref[...] = part

        @pl.when(k > 0)
        def _():
            acc_ref[...] += part

        @pl.when(k == nk - 1)
        def _():
            write(epilogue(acc_ref[...], *[e[...] for e in extra]))
    return body


def _long_tk(a, k_dim):
    tk = min(MM_TK, k_dim)
    if a.dtype == BF16 and k_dim >= 4 * MM_TK:
        tk = 2 * MM_TK
    return tk


def _extra_specs(extras, tm, tn):
    specs = []
    for arr, kind in extras:
        if kind == "tile":
            specs.append(pl.BlockSpec((tm, tn), lambda i, j, k: (i, j)))
        elif kind == "dep":
            specs.append(pl.BlockSpec(arr.shape, lambda i, j, k: (0, 0)))
        else:
            specs.append(pl.BlockSpec((1, tn), lambda i, j, k: (0, j)))
    return specs


def mm_nn(name, a, w, axis, epilogue, out_dtypes, extras=()):
    M, K = a.shape
    Kw, N = _full_dims(w.shape, axis)
    assert K == Kw
    tm, tn, tk = min(MM_TM, M), min(MM_TN, N), _long_tk(a, K)
    nk = K // tk
    in_specs = [pl.BlockSpec((tm, tk), lambda i, j, k: (i, k)),
                _wspec(w.shape, axis, tk, tn, lambda i, j, k: k, lambda i, j, k: j)]
    in_specs += _extra_specs(extras, tm, tn)
    body = _mm_body(nk, [kind for _, kind in extras], len(out_dtypes), ((1,), (0,)), epilogue, axis)
    return pl.pallas_call(
        body, name=name, grid=(M // tm, N // tn, nk), in_specs=in_specs,
        out_specs=[pl.BlockSpec((tm, tn), lambda i, j, k: (i, j)) for _ in out_dtypes],
        out_shape=[jax.ShapeDtypeStruct((M, N), d) for d in out_dtypes],
        scratch_shapes=[pltpu.VMEM((tm, tn), F32)] if nk > 1 else [],
        compiler_params=_cparams(("parallel", "parallel", "arbitrary")),
    )(a, w, *[e for e, _ in extras])


def mm_nt(name, g, w, axis, epilogue, out_dtypes, extras=()):
    M, N = g.shape
    K, Nw = _full_dims(w.shape, axis)
    assert N == Nw
    tm, tn, tk = min(MM_TM, M), min(MM_TN, K), min(MM_TK, N)
    nk = N // tk
    in_specs = [pl.BlockSpec((tm, tk), lambda i, j, k: (i, k)),
                _wspec(w.shape, axis, tn, tk, lambda i, j, k: j, lambda i, j, k: k)]
    in_specs += _extra_specs(extras, tm, tn)
    body = _mm_body(nk, [kind for _, kind in extras], len(out_dtypes), ((1,), (1,)), epilogue, axis)
    return pl.pallas_call(
        body, name=name, grid=(M // tm, K // tn, nk), in_specs=in_specs,
        out_specs=[pl.BlockSpec((tm, tn), lambda i, j, k: (i, j)) for _ in out_dtypes],
        out_shape=[jax.ShapeDtypeStruct((M, K), d) for d in out_dtypes],
        scratch_shapes=[pltpu.VMEM((tm, tn), F32)] if nk > 1 else [],
        compiler_params=_cparams(("parallel", "parallel", "arbitrary")),
    )(g, w, *[e for e, _ in extras])


def mm_tn(name, a, g, wshape, axis, deps=()):
    M, K = a.shape
    Mg, N = g.shape
    assert M == Mg and (K, N) == _full_dims(wshape, axis)
    tm, tn, tk = min(MM_TM, K), min(MM_TN, N), _long_tk(a, M)
    nk = M // tk
    body = _mm_body(nk, ["dep"] * len(deps), 1, ((0,), (0,)), lambda acc: (acc,), axis)
    return pl.pallas_call(
        body, name=name, grid=(K // tm, N // tn, nk),
        in_specs=[pl.BlockSpec((tk, tm), lambda i, j, k: (k, i)),
                  pl.BlockSpec((tk, tn), lambda i, j, k: (k, j))] + _extra_specs([(d, "dep") for d in deps], tm, tn),
        out_specs=[_wspec(wshape, axis, tm, tn, lambda i, j, k: i, lambda i, j, k: j)],
        out_shape=[jax.ShapeDtypeStruct(wshape, F32)],
        scratch_shapes=[pltpu.VMEM((tm, tn), F32)] if nk > 1 else [],
        compiler_params=_cparams(("parallel", "parallel", "arbitrary")),
    )(a, g, *deps)[0]


def _row_spec(tr, width):
    return pl.BlockSpec((tr, width), lambda i: (i, 0))


def _vec_spec(width):
    return pl.BlockSpec((1, width), lambda i: (0, 0))


def _fold8(x):
    r, d = x.shape
    return jnp.sum(x.reshape(r // 8, 8, d), axis=0)


def cast_bf16(name, a):
    T, D = a.shape
    tr = min(2 * ROW_TILE, T)

    def body(a_ref, o_ref):
        o_ref[...] = a_ref[...].astype(BF16)

    return pl.pallas_call(
        body, name=name, grid=(T // tr,), in_specs=[_row_spec(tr, D)], out_specs=_row_spec(tr, D),
        out_shape=jax.ShapeDtypeStruct((T, D), BF16), compiler_params=_cparams(("parallel",)),
    )(a)


def ln_fwd(name, f, prev, prev_g=None, prev_b=None):
    T, D = f.shape
    tr = min(ROW_TILE, T)
    affine = prev_g is not None

    def body(*refs):
        if affine:
            f_ref, p_ref, pg_ref, pb_ref, g_ref, b_ref, xhat_ref, rstd_ref, xbf_ref = refs
            xprev = p_ref[...] * pg_ref[...] + pb_ref[...]
        else:
            f_ref, p_ref, g_ref, b_ref, xhat_ref, rstd_ref, xbf_ref = refs
            xprev = p_ref[...]
        r = ALPHA * xprev + f_ref[...]
        mu = jnp.mean(r, axis=-1, keepdims=True)
        cen = r - mu
        var = jnp.mean(cen * cen, axis=-1, keepdims=True)
        rstd = lax.rsqrt(var + LN_EPS)
        xhat = cen * rstd
        xhat_ref[...] = xhat
        rstd_ref[...] = rstd
        xbf_ref[...] = (xhat * g_ref[...] + b_ref[...]).astype(BF16)

    def call(g, b):
        ins = [f, prev] + ([prev_g, prev_b] if affine else []) + [g, b]
        specs = [_row_spec(tr, D), _row_spec(tr, D)] + ([_vec_spec(D)] * 2 if affine else []) + [_vec_spec(D)] * 2
        return pl.pallas_call(
            body, name=name, grid=(T // tr,), in_specs=specs,
            out_specs=[_row_spec(tr, D), _row_spec(tr, 1), _row_spec(tr, D)],
            out_shape=[jax.ShapeDtypeStruct((T, D), F32), jax.ShapeDtypeStruct((T, 1), F32),
                       jax.ShapeDtypeStruct((T, D), BF16)],
            compiler_params=_cparams(("parallel",)),
        )(*ins)
    return call


def ln_bwd(name, xhat, rstd, gamma, dy=None, target=None, beta=None):
    T, D = xhat.shape
    tr = min(ROW_TILE, T)
    nt = T // tr
    head = target is not None

    def body(*refs):
        if head:
            xhat_ref, rstd_ref, g_ref, tgt_ref, b_ref = refs[:5]
            outs = refs[5:]
        else:
            xhat_ref, rstd_ref, g_ref, dy_ref = refs[:4]
            outs = refs[4:]
        dr_ref, drbf_ref, dg_ref, db_ref, cs_ref = outs[:5]
        rest = outs[5:]
        if head:
            loss_ref, acc_ref = rest
        else:
            (acc_ref,) = rest
        i = pl.program_id(0)
        xhat_v = xhat_ref[...]
        gam = g_ref[...]
        if head:
            diff = xhat_v * gam + b_ref[...] - tgt_ref[...]
            dyv = diff * (1.0 / D)
        else:
            dyv = dy_ref[...]
        dxh = dyv * gam
        m1 = jnp.mean(dxh, axis=-1, keepdims=True)
        m2 = jnp.mean(dxh * xhat_v, axis=-1, keepdims=True)
        dr = rstd_ref[...] * (dxh - m1 - xhat_v * m2)
        dr_ref[...] = dr
        drbf_ref[...] = dr.astype(BF16)

        @pl.when(i == 0)
        def _():
            acc_ref[...] = jnp.zeros_like(acc_ref)

        acc_ref[0] += _fold8(dyv * xhat_v)
        acc_ref[1] += _fold8(dyv)
        acc_ref[2] += _fold8(dr)
        if head:
            acc_ref[3] += _fold8(diff * diff)

        @pl.when(i == nt - 1)
        def _():
            dg_ref[...] = jnp.sum(acc_ref[0], axis=0, keepdims=True)
            db_ref[...] = jnp.sum(acc_ref[1], axis=0, keepdims=True)
            cs_ref[...] = jnp.sum(acc_ref[2], axis=0, keepdims=True)
            if head:
                loss_ref[...] = jnp.sum(jnp.sum(acc_ref[3], axis=0, keepdims=True), axis=1, keepdims=True)

    ins = [xhat, rstd, gamma] + ([target, beta] if head else [dy])
    specs = [_row_spec(tr, D), _row_spec(tr, 1), _vec_spec(D)] + ([_row_spec(tr, D), _vec_spec(D)] if head else [_row_spec(tr, D)])
    out_specs = [_row_spec(tr, D), _row_spec(tr, D), _vec_spec(D), _vec_spec(D), _vec_spec(D)]
    out_shape = [jax.ShapeDtypeStruct((T, D), F32), jax.ShapeDtypeStruct((T, D), BF16)] + [jax.ShapeDtypeStruct((1, D), F32)] * 3
    if head:
        out_specs.append(pl.BlockSpec((1, 1), lambda i: (0, 0)))
        out_shape.append(jax.ShapeDtypeStruct((1, 1), F32))
    return pl.pallas_call(
        body, name=name, grid=(nt,), in_specs=specs, out_specs=out_specs, out_shape=out_shape,
        scratch_shapes=[pltpu.VMEM((4, 8, D), F32)],
        compiler_params=_cparams(("arbitrary",)),
    )(*ins)


CONV_ROWS, CONV_COLS = 64, 512
CONV_COLS_BWD = 256


def _tap_chunks(tt, D, cols=CONV_COLS):
    for r0 in range(0, tt, min(CONV_ROWS, tt)):
        for c0 in range(0, D, min(cols, D)):
            yield r0, min(CONV_ROWS, tt), c0, min(cols, D)


SUBLANES = 8


def _shifted_copies(ext_ref, sh_ref):
    n = sh_ref.shape[1]
    zero = jnp.minimum(pl.program_id(0), 0)
    for b in range(1, SUBLANES):
        sh_ref[zero + (b - 1)] = ext_ref[pl.ds(b, n), :]


def _rows_at(ext_ref, sh_ref, off, nr, cols):
    a, b = divmod(off, SUBLANES)
    if b == 0:
        return ext_ref[pl.ds(off, nr), cols]
    return sh_ref[b - 1, pl.ds(a * SUBLANES, nr), cols]


def conv_fwd(name, h1, dw, dwb, lng, lnb):
    T, D2 = h1.shape
    D = D2 // 2
    tt = min(CONV_TILE, T)
    hb = tt // CONV_HALO
    KW = dw.shape[0]
    lead = CONV_HALO - (KW - 1)

    def body(a_ref, g_ref, ah_ref, gh_ref, dw_ref, dwb_ref, lng_ref, lnb_ref, u_ref, c_ref, s_ref, ext_ref, sh_ref):
        i = pl.program_id(0)
        u = a_ref[...] * _sigmoid(g_ref[...])
        u_ref[...] = u
        uh = ah_ref[...] * _sigmoid(gh_ref[...])
        ext_ref[pl.ds(0, CONV_HALO), :] = jnp.where(i > 0, uh, 0.0)
        ext_ref[pl.ds(CONV_HALO, tt), :] = u
        _shifted_copies(ext_ref, sh_ref)
        for r0, nr, c0, nc in _tap_chunks(tt, D):
            cols = pl.ds(c0, nc)
            acc = jnp.zeros((nr, nc), F32) + dwb_ref[:, cols]
            for k in range(KW):
                acc = acc + dw_ref[pl.ds(k, 1), cols] * _rows_at(ext_ref, sh_ref, r0 + lead + k, nr, cols)
            c_ref[pl.ds(r0, nr), cols] = acc
        c = c_ref[...]
        mu = jnp.mean(c, axis=-1, keepdims=True)
        cen = c - mu
        var = jnp.mean(cen * cen, axis=-1, keepdims=True)
        n = cen * lax.rsqrt(var + LN_EPS) * lng_ref[...] + lnb_ref[...]
        s_ref[...] = (n * _sigmoid(n)).astype(BF16)

    halo = lambda col: pl.BlockSpec((CONV_HALO, D), lambda i: (jnp.maximum(i * hb - 1, 0), col))
    return pl.pallas_call(
        body, name=name, grid=(T // tt,),
        in_specs=[pl.BlockSpec((tt, D), lambda i: (i, 0)), pl.BlockSpec((tt, D), lambda i: (i, 1)), halo(0), halo(1),
                  pl.BlockSpec((KW, D), lambda i: (0, 0)), _vec_spec(D), _vec_spec(D), _vec_spec(D)],
        out_specs=[_row_spec(tt, D)] * 3,
        out_shape=[jax.ShapeDtypeStruct((T, D), F32), jax.ShapeDtypeStruct((T, D), F32), jax.ShapeDtypeStruct((T, D), BF16)],
        scratch_shapes=[pltpu.VMEM((tt + CONV_HALO, D), F32),
                        pltpu.VMEM((SUBLANES - 1, tt + CONV_HALO - SUBLANES, D), F32)],
        compiler_params=_cparams(("parallel",)),
    )(h1, h1, h1, h1, dw, dwb, lng, lnb)


def conv_bwd_ln(name, ds, c, lng, lnb):
    T, D = c.shape
    tr = min(ROW_TILE, T)
    nt = T // tr

    def body(ds_ref, c_ref, g_ref, b_ref, dc_ref, dg_ref, db_ref, cs_ref, acc_ref):
        i = pl.program_id(0)
        cv = c_ref[...]
        mu = jnp.mean(cv, axis=-1, keepdims=True)
        cen = cv - mu
        var = jnp.mean(cen * cen, axis=-1, keepdims=True)
        rstd = lax.rsqrt(var + LN_EPS)
        chat = cen * rstd
        n = chat * g_ref[...] + b_ref[...]
        sg = _sigmoid(n)
        dn = ds_ref[...] * (sg * (1.0 + n * (1.0 - sg)))
        dxh = dn * g_ref[...]
        m1 = jnp.mean(dxh, axis=-1, keepdims=True)
        m2 = jnp.mean(dxh * chat, axis=-1, keepdims=True)
        dc = rstd * (dxh - m1 - chat * m2)
        dc_ref[...] = dc

        @pl.when(i == 0)
        def _():
            acc_ref[...] = jnp.zeros_like(acc_ref)

        acc_ref[0] += _fold8(dn * chat)
        acc_ref[1] += _fold8(dn)
        acc_ref[2] += _fold8(dc)

        @pl.when(i == nt - 1)
        def _():
            dg_ref[...] = jnp.sum(acc_ref[0], axis=0, keepdims=True)
            db_ref[...] = jnp.sum(acc_ref[1], axis=0, keepdims=True)
            cs_ref[...] = jnp.sum(acc_ref[2], axis=0, keepdims=True)

    return pl.pallas_call(
        body, name=name, grid=(nt,),
        in_specs=[_row_spec(tr, D), _row_spec(tr, D), _vec_spec(D), _vec_spec(D)],
        out_specs=[_row_spec(tr, D), _vec_spec(D), _vec_spec(D), _vec_spec(D)],
        out_shape=[jax.ShapeDtypeStruct((T, D), F32)] + [jax.ShapeDtypeStruct((1, D), F32)] * 3,
        scratch_shapes=[pltpu.VMEM((3, 8, D), F32)],
        compiler_params=_cparams(("arbitrary",)),
    )(ds, c, lng, lnb)


def conv_bwd_taps(name, dc, u, h1, dw):
    T, D = dc.shape
    tt = min(CONV_TILE, T)
    nt = T // tt
    hb = tt // CONV_HALO
    nhb = T // CONV_HALO
    KW = dw.shape[0]
    lead = CONV_HALO - (KW - 1)

    def body(dc_ref, dcn_ref, u_ref, uh_ref, a_ref, g_ref, dw_ref, dh1_ref, db1_ref, ddw_ref,
             edc_ref, eu_ref, du_ref, accw_ref, accb_ref, shdc_ref, shu_ref):
        i = pl.program_id(0)

        @pl.when(i == 0)
        def _():
            accw_ref[...] = jnp.zeros_like(accw_ref)
            accb_ref[...] = jnp.zeros_like(accb_ref)

        edc_ref[pl.ds(0, tt), :] = dc_ref[...]
        edc_ref[pl.ds(tt, CONV_HALO), :] = jnp.where(i < nt - 1, dcn_ref[...], 0.0)
        eu_ref[pl.ds(0, CONV_HALO), :] = jnp.where(i > 0, uh_ref[...], 0.0)
        eu_ref[pl.ds(CONV_HALO, tt), :] = u_ref[...]
        _shifted_copies(edc_ref, shdc_ref)
        _shifted_copies(eu_ref, shu_ref)
        for r0, nr, c0, nc in _tap_chunks(tt, D, CONV_COLS_BWD):
            cols = pl.ds(c0, nc)
            dcv = dc_ref[pl.ds(r0, nr), cols]
            acc = jnp.zeros((nr, nc), F32)
            for k in range(KW):
                acc = acc + dw_ref[pl.ds(k, 1), cols] * _rows_at(edc_ref, shdc_ref, r0 + (KW - 1) - k, nr, cols)
                accw_ref[k, :, cols] += _fold8(dcv * _rows_at(eu_ref, shu_ref, r0 + lead + k, nr, cols))
            du_ref[pl.ds(r0, nr), cols] = acc
        du = du_ref[...]
        sg = _sigmoid(g_ref[...])
        da = du * sg
        dg = du * a_ref[...] * sg * (1.0 - sg)
        dh1_ref[:, pl.ds(0, D)] = da.astype(BF16)
        dh1_ref[:, pl.ds(D, D)] = dg.astype(BF16)
        accb_ref[:, pl.ds(0, D)] += _fold8(da)
        accb_ref[:, pl.ds(D, D)] += _fold8(dg)

        @pl.when(i == nt - 1)
        def _():
            db1_ref[...] = jnp.sum(accb_ref[...], axis=0, keepdims=True)
            ddw_ref[...] = jnp.sum(accw_ref[...], axis=1)

    return pl.pallas_call(
        body, name=name, grid=(nt,),
        in_specs=[_row_spec(tt, D),
                  pl.BlockSpec((CONV_HALO, D), lambda i: (jnp.minimum((i + 1) * hb, nhb - 1), 0)),
                  _row_spec(tt, D),
                  pl.BlockSpec((CONV_HALO, D), lambda i: (jnp.maximum(i * hb - 1, 0), 0)),
                  pl.BlockSpec((tt, D), lambda i: (i, 0)), pl.BlockSpec((tt, D), lambda i: (i, 1)),
                  pl.BlockSpec((KW, D), lambda i: (0, 0))],
        out_specs=[_row_spec(tt, 2 * D), _vec_spec(2 * D), pl.BlockSpec((KW, D), lambda i: (0, 0))],
        out_shape=[jax.ShapeDtypeStruct((T, 2 * D), BF16), jax.ShapeDtypeStruct((1, 2 * D), F32),
                   jax.ShapeDtypeStruct((KW, D), F32)],
        scratch_shapes=[pltpu.VMEM((tt + CONV_HALO, D), F32), pltpu.VMEM((tt + CONV_HALO, D), F32),
                        pltpu.VMEM((tt, D), F32), pltpu.VMEM((KW, 8, D), F32), pltpu.VMEM((8, 2 * D), F32)]
                       + [pltpu.VMEM((SUBLANES - 1, tt + CONV_HALO - SUBLANES, D), F32)] * 2,
        compiler_params=_cparams(("arbitrary",)),
    )(dc, dc, u, u, h1, h1, dw)


def _t5_bucket(dist):
    max_exact = REL_BUCKETS // 2
    large = max_exact + (np.log(np.maximum(dist, 1) / max_exact) / math.log(REL_MAX_DIST / max_exact)
                         * (REL_BUCKETS - max_exact)).astype(np.int32)
    large = np.minimum(large, REL_BUCKETS - 1)
    return np.where(dist < max_exact, dist, large).astype(np.int32)


def _bucket_table(dil):
    i = np.arange(BAND)[:, None]
    j = np.arange(2 * BAND)[None, :]
    delta = i - j + BAND
    return _t5_bucket(np.clip(delta, 0, None) * dil)


def bias_expand(name, rel_bias, dil):
    n_heads = rel_bias.shape[1]
    idx = jnp.asarray(_bucket_table(dil))

    def body(rel_ref, idx_ref, out_ref):
        h = pl.program_id(0)
        idxv = idx_ref[...]
        b = jnp.zeros((BAND, 2 * BAND), F32)
        for bk in range(REL_BUCKETS):
            b = jnp.where(idxv == bk, rel_ref[bk, h], b)
        out_ref[...] = b

    return pl.pallas_call(
        body, name=name, grid=(n_heads,),
        in_specs=[pl.BlockSpec(memory_space=pltpu.SMEM), pl.BlockSpec((BAND, 2 * BAND), lambda h: (0, 0))],
        out_specs=pl.BlockSpec((None, BAND, 2 * BAND), lambda h: (h, 0, 0)),
        out_shape=jax.ShapeDtypeStruct((n_heads, BAND, 2 * BAND), F32),
        compiler_params=_cparams(("arbitrary",)),
    )(rel_bias, idx)


def relbias_grad(name, dsb_list):
    n_heads = dsb_list[0].shape[0]
    idxs = [jnp.asarray(_bucket_table(d)) for _, d in BRANCHES]
    nb = len(BRANCHES)

    def body(*refs):
        ds_refs, idx_refs, out_ref = refs[:nb], refs[nb:2 * nb], refs[2 * nb]
        lane = lax.broadcasted_iota(jnp.int32, (1, 128), 1)
        row = jnp.zeros((1, 128), F32)
        for bk in range(REL_BUCKETS):
            tot = jnp.zeros((1, 1), F32)
            for ds_ref, idx_ref in zip(ds_refs, idx_refs):
                sel = jnp.where(idx_ref[...] == bk, ds_ref[...], 0.0)
                tot = tot + jnp.sum(jnp.sum(sel, axis=0, keepdims=True), axis=1, keepdims=True)
            row = jnp.where(lane == bk, tot, row)
        out_ref[...] = row

    return pl.pallas_call(
        body, name=name, grid=(n_heads,),
        in_specs=[pl.BlockSpec((None, BAND, 2 * BAND), lambda h: (h, 0, 0))] * nb
                 + [pl.BlockSpec((BAND, 2 * BAND), lambda h: (0, 0))] * nb,
        out_specs=pl.BlockSpec((None, 1, 128), lambda h: (h, 0, 0)),
        out_shape=jax.ShapeDtypeStruct((n_heads, 1, 128), F32),
        compiler_params=_cparams(("arbitrary",)),
    )(*dsb_list, *idxs)


def _band_mask():
    i = lax.broadcasted_iota(jnp.int32, (BAND, 2 * BAND), 0)
    j = lax.broadcasted_iota(jnp.int32, (BAND, 2 * BAND), 1)
    return (j >= i) & (j <= i + BAND), j


def _rep2(x):
    return jnp.concatenate([x, x], axis=1)


ATTN_TOKENS = 2048
MERGE_ROWS = 256


def _rows(ref, start, n, dil):
    if dil == 1:
        return ref[pl.ds(start, n), :]
    return ref[pl.ds(start, n, stride=dil), :]


def _set_rows(ref, start, n, dil, val):
    if dil == 1:
        ref[pl.ds(start, n), :] = val
    else:
        ref[pl.ds(start, n, stride=dil), :] = val


def _attn_specs(ct, n_heads, chunk_of):
    cur = lambda col0: pl.BlockSpec((ct, HEAD_DIM), lambda h, s: (chunk_of(s), col0 + h))
    prev = lambda col0: pl.BlockSpec((ct, HEAD_DIM), lambda h, s: (jnp.maximum(chunk_of(s) - 1, 0), col0 + h))
    bias = pl.BlockSpec((None, BAND, 2 * BAND), lambda h, s: (h, 0, 0))
    return cur, prev, bias


def _load_keys(kext_ref, vext_ref, base, k_ref, v_ref, kp_ref, vp_ref, r, dil, ct):
    lc = ct // dil
    kext_ref[pl.ds(base, BAND), :] = _rows(kp_ref, ct - BAND * dil + r, BAND, dil).astype(BF16)
    vext_ref[pl.ds(base, BAND), :] = _rows(vp_ref, ct - BAND * dil + r, BAND, dil).astype(BF16)
    kext_ref[pl.ds(base + BAND, lc), :] = _rows(k_ref, r, lc, dil).astype(BF16)
    vext_ref[pl.ds(base + BAND, lc), :] = _rows(v_ref, r, lc, dil).astype(BF16)


ATTN_GROUP = 4


def _two_level(dil):
    if dil > ATTN_GROUP and dil % ATTN_GROUP == 0:
        return ATTN_GROUP, dil // ATTN_GROUP
    return 1, dil


def _slot_rows(ct):
    return max(ct + BAND, ATTN_GROUP * (ct // ATTN_GROUP + BAND))


def _window_mask(band, jcol, a, c):
    if a > 0:
        return band
    return band & jnp.logical_or(jcol >= BAND, c > 0)


def attn_fwd(name, q, kv, biases):
    T, D = q.shape
    n_heads = D // HEAD_DIM
    ct = min(ATTN_TOKENS, T)
    n_chunks = T // ct
    nbr = len(BRANCHES)
    scale = HEAD_DIM ** -0.5
    nt_dims = (((1,), (1,)), ((), ()))
    nn_dims = (((1,), (0,)), ((), ()))

    n_in = 5

    def body(*refs):
        ins = refs[:n_in]
        b_refs = refs[n_in:n_in + nbr]
        o_ref, obf_ref, lse_ref = refs[n_in + nbr:n_in + nbr + 3]
        kext_ref, vext_ref, acc_ref, m_ref, l_ref = refs[n_in + nbr + 3:n_in + nbr + 8]
        tmp_in = refs[n_in + nbr + 8:n_in + nbr + 8 + n_in]
        tmp_out = refs[n_in + nbr + 8 + n_in:]
        c = pl.program_id(1)
        band, jcol = _band_mask()

        def residue(src, dst, slot, r, dil, cte, bias_v):
            q_ref, k_ref, v_ref, kp_ref, vp_ref = src
            lc = cte // dil
            base = slot * (BAND + lc)
            _load_keys(kext_ref, vext_ref, base, k_ref, v_ref, kp_ref, vp_ref, r, dil, cte)
            for a in range(lc // BAND):
                tok = r + a * BAND * dil
                qa = _rows(q_ref, tok, BAND, dil).astype(BF16)
                kw = kext_ref[pl.ds(base + a * BAND, 2 * BAND), :]
                vw = vext_ref[pl.ds(base + a * BAND, 2 * BAND), :]
                s = lax.dot_general(qa, kw, nt_dims, preferred_element_type=F32) * scale + bias_v
                s = jnp.where(_window_mask(band, jcol, a, c), s, NEG_BIG)
                m = jnp.max(s, axis=-1, keepdims=True)
                p = jnp.exp(s - m)
                den = jnp.sum(p, axis=-1, keepdims=True)
                pv = lax.dot_general(p.astype(BF16), vw, nn_dims, preferred_element_type=F32)
                _set_rows(dst[0], tok, BAND, dil, pv)
                _set_rows(dst[1], tok, BAND, dil, jnp.broadcast_to(m, (BAND, HEAD_DIM)))
                _set_rows(dst[2], tok, BAND, dil, jnp.broadcast_to(den, (BAND, HEAD_DIM)))

        for bi, (win, dil) in enumerate(BRANCHES):
            bias_v = b_refs[bi][...]
            dst = (acc_ref.at[bi], m_ref.at[bi], l_ref.at[bi])
            outer, inner = _two_level(dil)
            if outer == 1:
                for r in range(dil):
                    residue(ins, dst, r % ATTN_GROUP, r, dil, ct, bias_v)
            else:
                cte = ct // outer

                def group(r1, carry, bias_v=bias_v, dst=dst, outer=outer, inner=inner, cte=cte):
                    for t_ref, x_ref in zip(tmp_in, ins):
                        t_ref[...] = _rows(x_ref, r1, cte, outer)
                    for r2 in range(inner):
                        residue(tmp_in, tmp_out, r2 % ATTN_GROUP, r2, inner, cte, bias_v)
                    for t_ref, d_ref in zip(tmp_out, dst):
                        _set_rows(d_ref, r1, cte, outer, t_ref[...])
                    return carry

                lax.fori_loop(0, outer, group, 0)

        def merge(i, carry):
            rows = pl.ds(pl.multiple_of(i * MERGE_ROWS, MERGE_ROWS), MERGE_ROWS)
            ms = [m_ref[bi, rows, :] for bi in range(nbr)]
            m = functools.reduce(jnp.maximum, ms)
            ws = [jnp.exp(mb - m) for mb in ms]
            tot = functools.reduce(lambda x, y: x + y, [w * l_ref[bi, rows, :] for bi, w in enumerate(ws)])
            o = functools.reduce(lambda x, y: x + y, [w * acc_ref[bi, rows, :] for bi, w in enumerate(ws)]) / tot
            o_ref[rows, :] = o
            obf_ref[rows, :] = o.astype(BF16)
            lse_ref[rows, :] = m + jnp.log(tot)
            return carry

        lax.fori_loop(0, ct // min(MERGE_ROWS, ct), merge, 0)

    cur, prev, bias = _attn_specs(ct, n_heads, lambda s: s)
    small = (ct // ATTN_GROUP, HEAD_DIM)
    return pl.pallas_call(
        body, name=name, grid=(n_heads, n_chunks),
        in_specs=[cur(0), cur(0), cur(n_heads), prev(0), prev(n_heads)] + [bias] * nbr,
        out_specs=[cur(0)] * 3,
        out_shape=[jax.ShapeDtypeStruct((T, D), F32), jax.ShapeDtypeStruct((T, D), BF16), jax.ShapeDtypeStruct((T, D), F32)],
        scratch_shapes=[pltpu.VMEM((_slot_rows(ct), HEAD_DIM), BF16)] * 2 + [pltpu.VMEM((nbr, ct, HEAD_DIM), F32)] * 3
                       + [pltpu.VMEM(small, F32)] * (n_in + 3),
        compiler_params=_cparams(("arbitrary", "arbitrary")),
    )(q, kv, kv, kv, kv, *biases)


def attn_bwd(name, q, kv, do, o, lse, biases):
    T, D = q.shape
    n_heads = D // HEAD_DIM
    ct = min(ATTN_TOKENS, T)
    n_chunks = T // ct
    nbr = len(BRANCHES)
    scale = HEAD_DIM ** -0.5
    nt_dims = (((1,), (1,)), ((), ()))
    tn_dims = (((0,), (0,)), ((), ()))
    nn_dims = (((1,), (0,)), ((), ()))
    mrows = min(MERGE_ROWS, ct)
    n_src = 8
    n_acc = 5

    def body(q_ref, k_ref, v_ref, do_ref, o_ref, lse_ref, kp_ref, vp_ref, *rest):
        b_refs = rest[:nbr]
        dq_ref, dk_ref, dv_ref = rest[nbr:nbr + 3]
        dsb_refs = rest[nbr + 3:2 * nbr + 3]
        sc = rest[2 * nbr + 3:]
        kext_ref, vext_ref, dkext_ref, dvext_ref, dqa_ref, dka_ref, dva_ref, dsum_ref, ck_ref, cv_ref = sc[:10]
        tmp_in = sc[10:10 + n_src]
        tmp_acc = sc[10 + n_src:10 + n_src + n_acc]
        dsacc_ref = sc[10 + n_src + n_acc]
        step = pl.program_id(1)
        c = n_chunks - 1 - step
        band, jcol = _band_mask()

        @pl.when(step == 0)
        def _():
            ck_ref[...] = jnp.zeros_like(ck_ref)
            cv_ref[...] = jnp.zeros_like(cv_ref)
            for r in dsb_refs:
                r[...] = jnp.zeros_like(r)

        def prep(i, carry):
            rows = pl.ds(pl.multiple_of(i * mrows, mrows), mrows)
            dsum_ref[rows, :] = jnp.broadcast_to(jnp.sum(do_ref[rows, :] * o_ref[rows, :], axis=-1, keepdims=True), (mrows, HEAD_DIM))
            dqa_ref[rows, :] = jnp.zeros((mrows, HEAD_DIM), F32)
            dka_ref[rows, :] = ck_ref[rows, :]
            dva_ref[rows, :] = cv_ref[rows, :]
            ck_ref[rows, :] = jnp.zeros((mrows, HEAD_DIM), F32)
            cv_ref[rows, :] = jnp.zeros((mrows, HEAD_DIM), F32)
            return carry

        lax.fori_loop(0, ct // mrows, prep, 0)

        def residue(src, acc, slot, r, dil, cte, bias_v):
            sq, sk, sv, sdo, slse, sdsum, skp, svp = src
            adq, adk, adv, ack, acv = acc
            lc = cte // dil
            base = slot * (BAND + lc)
            _load_keys(kext_ref, vext_ref, base, sk, sv, skp, svp, r, dil, cte)
            dkext_ref[pl.ds(base, BAND + lc), :] = jnp.zeros((BAND + lc, HEAD_DIM), F32)
            dvext_ref[pl.ds(base, BAND + lc), :] = jnp.zeros((BAND + lc, HEAD_DIM), F32)
            for a in range(lc // BAND):
                tok = r + a * BAND * dil
                qa = _rows(sq, tok, BAND, dil).astype(BF16)
                doa = _rows(sdo, tok, BAND, dil).astype(BF16)
                kw = kext_ref[pl.ds(base + a * BAND, 2 * BAND), :]
                vw = vext_ref[pl.ds(base + a * BAND, 2 * BAND), :]
                s = lax.dot_general(qa, kw, nt_dims, preferred_element_type=F32) * scale + bias_v
                p = jnp.where(_window_mask(band, jcol, a, c), jnp.exp(s - _rep2(_rows(slse, tok, BAND, dil))), 0.0)
                dp = lax.dot_general(doa, vw, nt_dims, preferred_element_type=F32)
                ds = p * (dp - _rep2(_rows(sdsum, tok, BAND, dil)))
                dsacc_ref[slot] += ds
                dsb16 = ds.astype(BF16)
                dqw = lax.dot_general(dsb16, kw, nn_dims, preferred_element_type=F32) * scale
                _set_rows(adq, tok, BAND, dil, _rows(adq, tok, BAND, dil) + dqw)
                dkext_ref[pl.ds(base + a * BAND, 2 * BAND), :] += lax.dot_general(dsb16, qa, tn_dims, preferred_element_type=F32) * scale
                dvext_ref[pl.ds(base + a * BAND, 2 * BAND), :] += lax.dot_general(p.astype(BF16), doa, tn_dims, preferred_element_type=F32)

            _set_rows(adk, r, lc, dil, _rows(adk, r, lc, dil) + dkext_ref[pl.ds(base + BAND, lc), :])
            _set_rows(adv, r, lc, dil, _rows(adv, r, lc, dil) + dvext_ref[pl.ds(base + BAND, lc), :])
            last = cte - BAND * dil + r
            _set_rows(ack, last, BAND, dil, _rows(ack, last, BAND, dil) + dkext_ref[pl.ds(base, BAND), :])
            _set_rows(acv, last, BAND, dil, _rows(acv, last, BAND, dil) + dvext_ref[pl.ds(base, BAND), :])

        full_src = (q_ref, k_ref, v_ref, do_ref, lse_ref, dsum_ref, kp_ref, vp_ref)
        full_acc = (dqa_ref, dka_ref, dva_ref, ck_ref, cv_ref)
        for bi, (win, dil) in enumerate(BRANCHES):
            bias_v = b_refs[bi][...]
            dsacc_ref[...] = jnp.zeros_like(dsacc_ref)
            outer, inner = _two_level(dil)
            if outer == 1:
                for r in range(dil):
                    residue(full_src, full_acc, r % ATTN_GROUP, r, dil, ct, bias_v)
            else:
                cte = ct // outer

                def group(r1, carry, bias_v=bias_v, outer=outer, inner=inner, cte=cte):
                    for t_ref, x_ref in zip(tmp_in, full_src):
                        t_ref[...] = _rows(x_ref, r1, cte, outer)
                    for t_ref in tmp_acc:
                        t_ref[...] = jnp.zeros_like(t_ref)
                    for r2 in range(inner):
                        residue(tmp_in, tmp_acc, r2 % ATTN_GROUP, r2, inner, cte, bias_v)
                    for t_ref, a_ref in zip(tmp_acc, full_acc):
                        _set_rows(a_ref, r1, cte, outer, _rows(a_ref, r1, cte, outer) + t_ref[...])
                    return carry

                lax.fori_loop(0, outer, group, 0)
            dsb_refs[bi][...] += functools.reduce(lambda x, y: x + y, [dsacc_ref[s] for s in range(ATTN_GROUP)])

        dq_ref[...] = dqa_ref[...].astype(BF16)
        dk_ref[...] = dka_ref[...].astype(BF16)
        dv_ref[...] = dva_ref[...].astype(BF16)

    cur, prev, bias = _attn_specs(ct, n_heads, lambda s: n_chunks - 1 - s)
    small = (ct // ATTN_GROUP, HEAD_DIM)
    res = pl.pallas_call(
        body, name=name, grid=(n_heads, n_chunks),
        in_specs=[cur(0), cur(0), cur(n_heads), cur(0), cur(0), cur(0), prev(0), prev(n_heads)] + [bias] * nbr,
        out_specs=[cur(0)] * 3 + [bias] * nbr,
        out_shape=[jax.ShapeDtypeStruct((T, D), BF16)] * 3 + [jax.ShapeDtypeStruct((n_heads, BAND, 2 * BAND), F32)] * nbr,
        scratch_shapes=[pltpu.VMEM((_slot_rows(ct), HEAD_DIM), BF16)] * 2 + [pltpu.VMEM((_slot_rows(ct), HEAD_DIM), F32)] * 2
                       + [pltpu.VMEM((ct, HEAD_DIM), F32)] * 6 + [pltpu.VMEM(small, F32)] * (n_src + n_acc)
                       + [pltpu.VMEM((ATTN_GROUP, BAND, 2 * BAND), F32)],
        compiler_params=_cparams(("arbitrary", "arbitrary")),
    )(q, kv, kv, do, o, lse, kv, kv, *biases)
    return res[0], res[1], res[2], list(res[3:])


def _divisor_tile(n, cap, mult):
    if n <= cap:
        return n
    t = cap - cap % mult
    while n % t:
        t -= mult
    return t


def _tile2(R, C):
    return _divisor_tile(R, 512, 8), _divisor_tile(C, 1024, 128)


def half_cast(name, dw, core):
    S, R, C = dw.shape
    hr = R // 2
    tr, tc = _tile2(hr, C)
    nrb = hr // tr

    def body(c_ref, x_ref, o_ref):
        o_ref[...] = x_ref[...].astype(BF16)

    return pl.pallas_call(
        body, name=name,
        grid_spec=pltpu.PrefetchScalarGridSpec(
            num_scalar_prefetch=1, grid=(S, nrb, C // tc),
            in_specs=[pl.BlockSpec((None, tr, tc), lambda s, i, j, c: (s, (1 - c[0]) * nrb + i, j))],
            out_specs=pl.BlockSpec((None, tr, tc), lambda s, i, j, c: (s, i, j))),
        out_shape=jax.ShapeDtypeStruct((S, hr, C), BF16),
        compiler_params=_cparams(("parallel", "parallel", "parallel")),
    )(core, dw)


def pair_sum(name, dw, recv, core):
    S, R, C = dw.shape
    hr = R // 2
    tr, tc = _tile2(hr, C)
    nrb = hr // tr

    def body(c_ref, x_ref, r_ref, p_ref, pbf_ref):
        p = x_ref[...] + r_ref[...].astype(F32)
        p_ref[...] = p
        pbf_ref[...] = p.astype(BF16)

    out = pl.BlockSpec((None, tr, tc), lambda s, i, j, c: (s, i, j))
    return pl.pallas_call(
        body, name=name,
        grid_spec=pltpu.PrefetchScalarGridSpec(
            num_scalar_prefetch=1, grid=(S, nrb, C // tc),
            in_specs=[pl.BlockSpec((None, tr, tc), lambda s, i, j, c: (s, c[0] * nrb + i, j)), out],
            out_specs=[out, out]),
        out_shape=[jax.ShapeDtypeStruct((S, hr, C), F32), jax.ShapeDtypeStruct((S, hr, C), BF16)],
        compiler_params=_cparams(("parallel", "parallel", "parallel")),
    )(core, dw, recv)


def chip_sum(name, p, recv, chip, core):
    S, hr, C = p.shape
    tr, tc = _tile2(hr, C)
    nrb = hr // tr

    def body(chip_ref, core_ref, p_ref, r_ref, o_ref):
        acc = p_ref[...]
        for t in range(N_CHIPS - 1):
            acc = acc + r_ref[t].astype(F32)
        o_ref[...] = acc

    return pl.pallas_call(
        body, name=name,
        grid_spec=pltpu.PrefetchScalarGridSpec(
            num_scalar_prefetch=2, grid=(nrb, C // tc),
            in_specs=[pl.BlockSpec((None, tr, tc), lambda i, j, s, c: (s[0], i, j)),
                      pl.BlockSpec((N_CHIPS - 1, tr, tc), lambda i, j, s, c: (0, i, j))],
            out_specs=pl.BlockSpec((tr, tc), lambda i, j, s, c: (c[0] * nrb + i, j))),
        out_shape=jax.ShapeDtypeStruct((2 * hr, C), F32),
        compiler_params=_cparams(("parallel", "parallel")),
    )(chip, core, p, recv)


def adamw(name, w, g, m, v):
    R, C = w.shape
    tr, tc = _tile2(R, C)
    c1 = 1.0 - ADAM_B1 ** ADAM_STEP
    c2 = 1.0 - ADAM_B2 ** ADAM_STEP

    def body(w_ref, g_ref, m_ref, v_ref, d_ref, nm_ref, nv_ref):
        gv = g_ref[...]
        nm = ADAM_B1 * m_ref[...] + (1.0 - ADAM_B1) * gv
        nv = ADAM_B2 * v_ref[...] + (1.0 - ADAM_B2) * (gv * gv)
        nm_ref[...] = nm
        nv_ref[...] = nv
        d_ref[...] = -ADAM_LR * ((nm / c1) / (jnp.sqrt(nv / c2) + ADAM_EPS) + ADAM_WD * w_ref[...])

    spec = pl.BlockSpec((tr, tc), lambda i, j: (i, j))
    return pl.pallas_call(
        body, name=name, grid=(R // tr, C // tc), in_specs=[spec] * 4, out_specs=[spec] * 3,
        out_shape=[jax.ShapeDtypeStruct((R, C), F32)] * 3,
        compiler_params=_cparams(("parallel", "parallel")),
    )(w, g, m, v)


def adamw_layers(name, w, g_layers, m, v):
    nl, R, C = w.shape
    tr, tc = _tile2(R, C)
    ni, nj = R // tr, C // tc
    c1 = 1.0 - ADAM_B1 ** ADAM_STEP
    c2 = 1.0 - ADAM_B2 ** ADAM_STEP

    def body(w_ref, *rest):
        g_refs = rest[:nl]
        m_ref, v_ref, g_ref, d_ref, nm_ref, nv_ref = rest[nl:]
        layer = pl.program_id(0)
        gv = g_refs[0][...]
        for l in range(1, nl):
            gv = jnp.where(layer == l, g_refs[l][...], gv)
        nm = ADAM_B1 * m_ref[...] + (1.0 - ADAM_B1) * gv
        nv = ADAM_B2 * v_ref[...] + (1.0 - ADAM_B2) * (gv * gv)
        g_ref[...] = gv
        nm_ref[...] = nm
        nv_ref[...] = nv
        d_ref[...] = -ADAM_LR * ((nm / c1) / (jnp.sqrt(nv / c2) + ADAM_EPS) + ADAM_WD * w_ref[...])

    def g_spec(l):
        def index(layer, i, j):
            return (jnp.where(layer == l, i, jnp.where(layer < l, 0, ni - 1)),
                    jnp.where(layer == l, j, jnp.where(layer < l, 0, nj - 1)))
        return pl.BlockSpec((tr, tc), index)

    spec = pl.BlockSpec((None, tr, tc), lambda layer, i, j: (layer, i, j))
    return pl.pallas_call(
        body, name=name, grid=(nl, ni, nj),
        in_specs=[spec] + [g_spec(l) for l in range(nl)] + [spec] * 2, out_specs=[spec] * 4,
        out_shape=[jax.ShapeDtypeStruct((nl, R, C), F32)] * 4,
        compiler_params=_cparams(("arbitrary", "arbitrary", "arbitrary")),
    )(w, *g_layers, m, v)


def sum_devices(name, gathered):
    n, R, C = gathered.shape

    def body(x_ref, o_ref):
        acc = x_ref[0]
        for d in range(1, n):
            acc = acc + x_ref[d]
        o_ref[...] = acc

    return pl.pallas_call(
        body, name=name, in_specs=[pl.BlockSpec(memory_space=pltpu.VMEM)],
        out_specs=pl.BlockSpec(memory_space=pltpu.VMEM),
        out_shape=jax.ShapeDtypeStruct((R, C), F32),
    )(gathered)


def _place():
    x, y, c = lax.axis_index("x"), lax.axis_index("y"), lax.axis_index("c")
    return x, y, c


def _other_chips(x, y):
    return [(1 - x, y), (x, 1 - y), (1 - x, 1 - y)]


def all_gather8(name, block):
    R, C = block.shape

    def body(x_ref, out_ref, send_sems, recv_sems, local_sem):
        x, y, c = _place()
        me, sibling = (x, y, c), (x, y, 1 - c)
        chips = _other_chips(x, y)

        def rows(px, py, pc):
            return out_ref.at[4 * px + 2 * py + pc]

        def copy(k, blk, to, src=None):
            return pltpu.make_async_remote_copy(
                src_ref=rows(*blk) if src is None else src, dst_ref=rows(*blk),
                send_sem=send_sems.at[k], recv_sem=recv_sems.at[k], device_id=to, device_id_type=MESH)

        mine = pltpu.make_async_copy(x_ref, rows(*me), local_sem)
        mine.start()
        first = [copy(0, me, sibling, src=x_ref)]
        first += [copy(1 + j, me, (*chip, c), src=x_ref) for j, chip in enumerate(chips)]
        for cp in first:
            cp.start()
        passed = [copy(4 + j, (*chip, c), sibling) for j, chip in enumerate(chips)]
        for j, chip in enumerate(chips):
            copy(1 + j, (*chip, c), me).wait_recv()
            passed[j].start()
        copy(0, sibling, me).wait_recv()
        for j, chip in enumerate(chips):
            copy(4 + j, (*chip, 1 - c), me).wait_recv()
        for cp in first + passed:
            cp.wait_send()
        mine.wait()

    return pl.pallas_call(
        body, name=name, out_shape=jax.ShapeDtypeStruct((N_DEV, R, C), block.dtype),
        in_specs=[pl.BlockSpec(memory_space=pltpu.VMEM)], out_specs=pl.BlockSpec(memory_space=pltpu.VMEM),
        scratch_shapes=[pltpu.SemaphoreType.DMA((7,)), pltpu.SemaphoreType.DMA((7,)), pltpu.SemaphoreType.DMA],
    )(block)


_HBM = pl.BlockSpec(memory_space=pltpu.HBM)
_SEM = pl.BlockSpec(memory_space=pltpu.SEMAPHORE)
_DATAFLOW = pltpu.SideEffectType.DATAFLOW_SIDE_EFFECTING


def _in_hbm(a):
    return pltpu.with_memory_space_constraint(a, pltpu.HBM)


def split_start(name, srcs, lands, n_sem, plan):
    ns, nl = len(srcs), len(lands)

    def body(*refs):
        src, land = refs[:ns], refs[ns:ns + nl]
        send_sems, recv_sems = refs[ns + nl], refs[ns + nl + 1]
        token = refs[-1]
        outgoing, _ = plan(src, land, send_sems, recv_sems)
        for cp in outgoing:
            cp.start()
        token[...] = jnp.zeros_like(token)

    bufs = list(srcs) + list(lands)
    res = pl.pallas_call(
        body, name=name,
        out_shape=(pltpu.SemaphoreType.DMA((n_sem,)), pltpu.SemaphoreType.DMA((n_sem,)),
                   *[pltpu.HBM(b.shape, b.dtype) for b in bufs], jax.ShapeDtypeStruct((8, 128), F32)),
        in_specs=[_HBM] * (ns + nl),
        out_specs=(_SEM, _SEM, *[_HBM] * (ns + nl), pl.BlockSpec(memory_space=pltpu.VMEM)),
        input_output_aliases={i: 2 + i for i in range(ns + nl)},
        compiler_params=pltpu.CompilerParams(has_side_effects=_DATAFLOW),
    )(*[_in_hbm(b) for b in bufs])
    return res[0], res[1], list(res[2:2 + ns]), list(res[2 + ns:2 + ns + nl]), res[-1]


def split_wait(name, started, after, plan):
    send_sems, recv_sems, srcs, lands, _ = started
    ns, nl = len(srcs), len(lands)

    def body(*refs):
        src, land = refs[:ns], refs[ns:ns + nl]
        send, recv = refs[ns + nl], refs[ns + nl + 1]
        outgoing, incoming = plan(src, land, send, recv)
        for cp in outgoing:
            cp.wait_send()
        for cp in incoming:
            cp.wait_recv()

    bufs = list(srcs) + list(lands)
    res = pl.pallas_call(
        body, name=name,
        out_shape=tuple(pltpu.HBM(b.shape, b.dtype) for b in bufs),
        in_specs=[_HBM] * (ns + nl) + [_SEM, _SEM, pl.BlockSpec(memory_space=pl.ANY)],
        out_specs=tuple([_HBM] * (ns + nl)),
        input_output_aliases={i: i for i in range(ns + nl)},
        compiler_params=pltpu.CompilerParams(has_side_effects=_DATAFLOW),
    )(*bufs, send_sems, recv_sems, after)
    return list(res[ns:])


def _rcopy(src, dst, send_sems, ks, recv_sems, kr, device):
    return pltpu.make_async_remote_copy(src_ref=src, dst_ref=dst, send_sem=send_sems.at[ks], recv_sem=recv_sems.at[kr],
                                        device_id=device, device_id_type=MESH)


def _half_rows(ref, h):
    hr = ref.shape[0] // 2
    return ref.at[pl.ds(h * hr, hr)]


def _gather_plan(src, land, send_sems, recv_sems):
    x, y, c = _place()
    me_chip = 2 * x + y
    chips = _other_chips(x, y)
    outgoing, incoming = [], []
    for w, buf in enumerate(land):
        mine = _half_rows(buf.at[me_chip], c)
        for t, chip in enumerate(chips):
            slot = 2 * chip[0] + chip[1]
            for cc in range(2):
                outgoing.append(_rcopy(mine, mine, send_sems, 6 * w + 2 * t + cc, recv_sems, 6 * w + 2 * t + c, (*chip, cc)))
                theirs = _half_rows(buf.at[slot], cc)
                incoming.append(_rcopy(theirs, theirs, send_sems, 6 * w + 2 * t + cc, recv_sems, 6 * w + 2 * t + cc, (*chip, cc)))
    return outgoing, incoming


def _gather_half_plan(src, land, send_sems, recv_sems):
    x, y, c = _place()
    me_chip = 2 * x + y
    mine = _half_rows(land[0].at[me_chip], c)
    outgoing, incoming = [], []
    for t, chip in enumerate(_other_chips(x, y)):
        outgoing.append(_rcopy(mine, mine, send_sems, t, recv_sems, t, (*chip, c)))
        theirs = _half_rows(land[0].at[2 * chip[0] + chip[1]], c)
        incoming.append(_rcopy(theirs, theirs, send_sems, t, recv_sems, t, (*chip, c)))
    return outgoing, incoming


def _forward_halves_plan(src, land, send_sems, recv_sems):
    x, y, c = _place()
    outgoing, incoming = [], []
    for t, chip in enumerate(_other_chips(x, y)):
        slot = land[0].at[2 * chip[0] + chip[1]]
        got, missing = _half_rows(slot, c), _half_rows(slot, 1 - c)
        outgoing.append(_rcopy(got, got, send_sems, t, recv_sems, t, (x, y, 1 - c)))
        incoming.append(_rcopy(missing, missing, send_sems, t, recv_sems, t, (x, y, 1 - c)))
    return outgoing, incoming


def _all_to_all_plan(src, land, send_sems, recv_sems):
    x, y, c = _place()
    mine = land[0].at[4 * x + 2 * y + c]
    outgoing, incoming = [], []
    for k in range(1, N_DEV):
        fx, fy, fc = (k >> 2) & 1, (k >> 1) & 1, k & 1
        px, py, pc = (1 - x if fx else x), (1 - y if fy else y), (1 - c if fc else c)
        outgoing.append(_rcopy(mine, mine, send_sems, k - 1, recv_sems, k - 1, (px, py, pc)))
        theirs = land[0].at[4 * px + 2 * py + pc]
        incoming.append(_rcopy(theirs, theirs, send_sems, k - 1, recv_sems, k - 1, (px, py, pc)))
    return outgoing, incoming


def place_block(name, block, slot, n_slots):
    R, C = block.shape

    def body(slot_ref, x_ref, o_ref):
        o_ref[...] = x_ref[...]

    return pl.pallas_call(
        body, name=name,
        grid_spec=pltpu.PrefetchScalarGridSpec(
            num_scalar_prefetch=1, grid=(1,),
            in_specs=[pl.BlockSpec((R, C), lambda i, s: (0, 0))],
            out_specs=pl.BlockSpec((None, R, C), lambda i, s: (s[0], 0, 0))),
        out_shape=jax.ShapeDtypeStruct((n_slots, R, C), block.dtype),
        compiler_params=_cparams(("arbitrary",)),
    )(slot, block)


def _swap_plan(src, land, send_sems, recv_sems):
    x, y, c = _place()
    cp = _rcopy(src[0], land[0], send_sems, 0, recv_sems, 0, (x, y, 1 - c))
    return [cp], [cp]


def _scatter_plan(src, land, send_sems, recv_sems):
    x, y, c = _place()
    cps = [_rcopy(src[0].at[2 * chip[0] + chip[1]], land[0].at[t], send_sems, t, recv_sems, t, (*chip, c))
           for t, chip in enumerate(_other_chips(x, y))]
    return cps, cps


def _share_plan(src, land, send_sems, recv_sems):
    x, y, c = _place()
    mine, theirs = _half_rows(land[0], c), _half_rows(land[0], 1 - c)
    return ([_rcopy(mine, mine, send_sems, 0, recv_sems, 0, (x, y, 1 - c))],
            [_rcopy(theirs, theirs, send_sems, 0, recv_sems, 0, (x, y, 1 - c))])


def place_shard(name, shard, chip, deps=(), layer=None):
    R, C = shard.shape[-2:]
    tr, tc = _tile2(R, C)

    def body(chip_ref, x_ref, *rest):
        rest[-1][...] = x_ref[...].astype(BF16)

    if layer is None:
        src = pl.BlockSpec((tr, tc), lambda i, j, s: (i, j))
    else:
        src = pl.BlockSpec((None, tr, tc), lambda i, j, s: (layer, i, j))
    return pl.pallas_call(
        body, name=name,
        grid_spec=pltpu.PrefetchScalarGridSpec(
            num_scalar_prefetch=1, grid=(R // tr, C // tc),
            in_specs=[src] + [pl.BlockSpec(d.shape, lambda i, j, s: (0, 0)) for d in deps],
            out_specs=pl.BlockSpec((None, tr, tc), lambda i, j, s: (s[0], i, j))),
        out_shape=jax.ShapeDtypeStruct((N_CHIPS, R, C), BF16),
        compiler_params=_cparams(("parallel", "parallel")),
    )(chip, shard, *deps)


class GradExchange:
    SCATTER_TICKS = 2

    def __init__(self, chip1, core, shard, mom, vel):
        self.chip1, self.core, self.shard, self.mom, self.vel = chip1, core, shard, mom, vel
        self.inflight, self.tokens, self.results = [], [], {}

    def take_deps(self):
        deps, self.tokens = self.tokens, []
        return deps

    def _start(self, name, srcs, lands, n_sem, plan):
        started = split_start(name, srcs, lands, n_sem, plan)
        self.tokens.append(started[-1])
        return started

    def add(self, n, dw):
        S, R, C = dw.shape
        to_sibling = half_cast("rs_cast_" + n, dw, self.core)
        started = self._start("rs_swap_start_" + n, [to_sibling], [lax.empty((S, R // 2, C), BF16)], 1, _swap_plan)
        self.inflight.append(dict(n=n, dw=dw, stage=0, started=started, ticks=0))

    def tick(self, after):
        for it in self.inflight:
            n = it["n"]
            if it["stage"] == 0:
                (recv,) = split_wait("rs_swap_wait_" + n, it["started"], after, _swap_plan)
                p, pbf = pair_sum("rs_pair_sum_" + n, it["dw"], recv, self.core)
                S, hr, C = pbf.shape
                it.update(stage=1, p=p, ticks=0,
                          started=self._start("rs_scatter_start_" + n, [pbf], [lax.empty((N_CHIPS - 1, hr, C), BF16)], 3, _scatter_plan))
            elif it["stage"] == 1:
                it["ticks"] += 1
                if it["ticks"] >= self.SCATTER_TICKS:
                    (recv,) = split_wait("rs_scatter_wait_" + n, it["started"], after, _scatter_plan)
                    half = chip_sum("rs_chip_sum_" + n, it["p"], recv, self.chip1, self.core)
                    it.update(stage=2, started=self._start("rs_share_start_" + n, [], [half], 1, _share_plan))
            elif it["stage"] == 2:
                (grad,) = split_wait("rs_share_wait_" + n, it["started"], after, _share_plan)
                if n in self.shard:
                    self.results[n] = (grad,) + tuple(adamw("adamw_" + n, self.shard[n], grad, self.mom[n], self.vel[n]))
                else:
                    self.results[n] = (grad,)
                it["stage"] = 3
        self.inflight = [it for it in self.inflight if it["stage"] < 3]

    def flush(self, after):
        while self.inflight:
            self.tick(after)


def _pack(arrs):
    parts = []
    for a in arrs:
        flat = a.reshape(-1).astype(F32)
        n = flat.shape[0]
        padded = -(-n // 1024) * 1024
        parts.append(jnp.pad(flat, (0, padded - n)).reshape(padded // 128, 128))
    return jnp.concatenate(parts, axis=0)


def _unpack(buf, shapes):
    out, row = [], 0
    for shp in shapes:
        n = int(np.prod(shp))
        rows = -(-n // 1024) * 8
        out.append(buf[row:row + rows].reshape(-1)[:n].reshape(shp))
        row += rows
    return out


def _bias_epi(acc, b):
    return (acc + b,)


def local_step(x, target, W, P, ex, first_deps=(), on_small=None):
    T, D = x.shape
    g = {}
    plain = lambda acc: (acc,)

    xbf0 = cast_bf16("x_cast", x)
    (h1,) = mm_nn("pw1_fwd", xbf0, W("pw1", x), "col", _bias_epi, [F32],
                  extras=[(P["pw1_b"], "row")] + [(d, "dep") for d in first_deps])
    u, cpre, s = conv_fwd("conv_fwd", h1, P["dw_w"], P["dw_b"], P["cln_g"], P["cln_b"])
    (mix0,) = mm_nn("pw2_fwd", s, W("pw2", s), "row", _bias_epi, [F32], extras=[(P["pw2_b"], "row")])
    ln = [None] * 4
    gam = [P["ln_mix_g"][0:1], P["ln_mlp_g"][0:1], P["ln_mix_g"][1:2], P["ln_mlp_g"][1:2]]
    bet = [P["ln_mix_b"][0:1], P["ln_mlp_b"][0:1], P["ln_mix_b"][1:2], P["ln_mlp_b"][1:2]]
    ln[0] = ln_fwd("ln0_fwd", mix0, x)(gam[0], bet[0])

    def mlp_fwd(tag, i_ln, n1, n2):
        xhat, rstd, xbf = ln[i_ln]

        def up_epi(acc):
            r = jnp.maximum(acc, 0.0)
            return r * r, r

        hid, relu = mm_nn(tag + "_up", xbf, W(n1, xbf), "col", up_epi, [BF16, BF16])
        (mlp,) = mm_nn(tag + "_down", hid, W(n2, hid), "row", plain, [F32])
        ln[i_ln + 1] = ln_fwd(tag + "_ln", mlp, xhat, gam[i_ln], bet[i_ln])(gam[i_ln + 1], bet[i_ln + 1])
        return hid, relu

    hid0 = mlp_fwd("mlp0", 0, "w1_0", "w2_0")

    x2bf = ln[1][2]
    (kv,) = mm_nn("kv_fwd", x2bf, W("kv", x2bf), "col", plain, [F32])
    (q,) = mm_nn("q_fwd", x2bf, W("wq", kv), "row", plain, [F32])
    biases = [bias_expand("bias_d%d" % d, P["rel_bias"], d) for _, d in BRANCHES]
    assert all(win // d == BAND and min(ATTN_TOKENS, T) % (BAND * d) == 0 for win, d in BRANCHES)
    o, obf, lse = attn_fwd("attn_fwd", q, kv, biases)
    (attn,) = mm_nn("wo_fwd", obf, W("wo", obf), "row", plain, [F32])
    ln[2] = ln_fwd("ln2_fwd", attn, ln[1][0], gam[1], bet[1])(gam[2], bet[2])
    hid1 = mlp_fwd("mlp1", 2, "w1_1", "w2_1")

    dr3, dr3bf, g["ln_mlp_g1"], g["ln_mlp_b1"], _, loss_sum = ln_bwd(
        "ln3_bwd", ln[3][0], ln[3][1], gam[3], target=target, beta=bet[3])

    def dw_step(name, wname, a, cot, axis):
        dw = mm_tn(name, a, cot, W(wname, a).shape, axis, deps=ex.take_deps())
        ex.tick(dw)
        ex.add(wname, dw)

    def dx_step(name, cot, wname, axis, epilogue, out_dtype, extras):
        deps = [(d, "dep") for d in ex.take_deps()]
        (out,) = mm_nt(name, cot, W(wname, cot), axis, epilogue, [out_dtype], extras=list(extras) + deps)
        ex.tick(out)
        return out

    def mlp_bwd(tag, i_ln, n1, n2, hid_relu, dr, drbf):
        xbf = ln[i_ln][2]
        hid, relu = hid_relu
        dw_step(tag + "_dw2", n2, hid, drbf, "row")
        dp = dx_step(tag + "_dhid", drbf, n2, "row", lambda acc, r: (acc * (2.0 * r.astype(F32)),), BF16, [(relu, "tile")])
        dw_step(tag + "_dw1", n1, xbf, dp, "col")
        return dx_step(tag + "_dx", dp, n1, "col", lambda acc, e: (acc + ALPHA * e,), F32, [(dr, "tile")])

    dx3 = mlp_bwd("mlp1", 2, "w1_1", "w2_1", hid1, dr3, dr3bf)
    dr2, dr2bf, g["ln_mix_g1"], g["ln_mix_b1"], _ = ln_bwd("ln2_bwd", ln[2][0], ln[2][1], gam[2], dy=dx3)
    dw_step("wo_dw", "wo", obf, dr2bf, "row")
    do = dx_step("wo_dx", dr2bf, "wo", "row", plain, F32, [])
    dq, dk, dv, dsbs = attn_bwd("attn_bwd", q, kv, do, o, lse, biases)
    g["rel_bias"] = relbias_grad("relbias_grad", dsbs)[:, 0, :REL_BUCKETS].T
    dkv = jnp.concatenate([dk, dv], axis=1)
    dw_step("wq_dw", "wq", x2bf, dq, "row")
    dw_step("kv_dw", "kv", x2bf, dkv, "col")
    dx2a = dx_step("wq_dx", dq, "wq", "row", lambda acc, e: (acc + ALPHA * e,), F32, [(dr2, "tile")])
    dx2 = dx_step("kv_dx", dkv, "kv", "col", lambda acc, e: (acc + e,), F32, [(dx2a, "tile")])

    dr1, dr1bf, g["ln_mlp_g0"], g["ln_mlp_b0"], _ = ln_bwd("ln1_bwd", ln[1][0], ln[1][1], gam[1], dy=dx2)
    dx1 = mlp_bwd("mlp0", 0, "w1_0", "w2_0", hid0, dr1, dr1bf)
    dr0, dr0bf, g["ln_mix_g0"], g["ln_mix_b0"], g["pw2_b"] = ln_bwd("ln0_bwd", ln[0][0], ln[0][1], gam[0], dy=dx1)

    dw_step("pw2_dw", "pw2", s, dr0bf, "row")
    ds = dx_step("pw2_dx", dr0bf, "pw2", "row", plain, F32, [])
    dc, g["cln_g"], g["cln_b"], g["dw_b"] = conv_bwd_ln("conv_bwd_ln", ds, cpre, P["cln_g"], P["cln_b"])
    dh1, g["pw1_b"], g["dw_w"] = conv_bwd_taps("conv_bwd_taps", dc, u, h1, P["dw_w"])
    if on_small is not None:
        on_small(g)
    dw_step("pw1_dw", "pw1", xbf0, dh1, "col")
    dx = dx_step("pw1_dx", dh1, "pw1", "col", lambda acc, e: (acc + ALPHA * e,), F32, [(dr0, "tile")])
    return loss_sum, dx, g


BIG = ("pw1", "pw2", "w1_0", "w2_0", "kv", "wq", "wo", "w1_1", "w2_1")


def kernel(x, conv_pw1_w, conv_pw1_b, conv_dw_w, conv_dw_b, conv_ln_g, conv_ln_b, conv_pw2_w, conv_pw2_b, w_kv, attn_wq, attn_wo, rel_bias, mlp_w1, mlp_w2, ln_mix_g, ln_mix_b, ln_mlp_g, ln_mlp_b, loss_target, m_conv_pw1_w, m_conv_pw1_b, m_conv_dw_w, m_conv_dw_b, m_conv_ln_g, m_conv_ln_b, m_conv_pw2_w, m_conv_pw2_b, m_w_kv, m_attn_wq, m_attn_wo, m_rel_bias, m_mlp_w1, m_mlp_w2, m_ln_mix_g, m_ln_mix_b, m_ln_mlp_g, m_ln_mlp_b, v_conv_pw1_w, v_conv_pw1_b, v_conv_dw_w, v_conv_dw_b, v_conv_ln_g, v_conv_ln_b, v_conv_pw2_w, v_conv_pw2_b, v_w_kv, v_attn_wq, v_attn_wo, v_rel_bias, v_mlp_w1, v_mlp_w2, v_ln_mix_g, v_ln_mix_b, v_ln_mlp_g, v_ln_mlp_b):
    _, T, D = x.shape
    xi, yi, ci = _place()
    chip = 2 * xi + yi
    core = jnp.reshape(ci, (1,)).astype(jnp.int32)
    chip1 = jnp.reshape(chip, (1,)).astype(jnp.int32)

    def two_d(a):
        return a.reshape(a.shape[-2:])

    shard = {"pw1": two_d(conv_pw1_w), "pw2": two_d(conv_pw2_w), "kv": w_kv, "wq": two_d(attn_wq), "wo": two_d(attn_wo)}
    mom = {"pw1": two_d(m_conv_pw1_w), "pw2": two_d(m_conv_pw2_w), "kv": m_w_kv, "wq": two_d(m_attn_wq), "wo": two_d(m_attn_wo)}
    vel = {"pw1": two_d(v_conv_pw1_w), "pw2": two_d(v_conv_pw2_w), "kv": v_w_kv, "wq": two_d(v_attn_wq), "wo": two_d(v_attn_wo)}
    stacked = {"w1_0": (mlp_w1, 0), "w1_1": (mlp_w1, 1), "w2_0": (mlp_w2, 0), "w2_1": (mlp_w2, 1)}

    sharded_small = [conv_pw1_b, conv_dw_w[0], conv_dw_b, conv_ln_g, conv_ln_b, conv_pw2_b]
    sh_shapes = [a.shape for a in sharded_small]
    small_all = all_gather8("gather_small", _pack(sharded_small))
    per_chip = [_unpack(small_all[2 * j], sh_shapes) for j in range(N_CHIPS)]
    full = [jnp.concatenate([per_chip[j][i] for j in range(N_CHIPS)], axis=-1) for i in range(len(sharded_small))]

    started = {}
    for n in BIG:
        deps = [started[prev][-1] for prev in list(started)[-1:]] or [full[0]]
        src, layer = stacked.get(n, (shard.get(n), None))
        land = place_shard("place_" + n, src, chip1, deps, layer)
        if n == BIG[0]:
            started[n] = split_start("gather_start_" + n, [], [land], 3, _gather_half_plan)
        else:
            started[n] = split_start("gather_start_" + n, [], [land], 6, _gather_plan)
    gathered = {}

    def W(n, after):
        if n not in gathered:
            if n == BIG[0]:
                lands = split_wait("gather_wait_" + n, started[n], started[BIG[-1]][-1], _gather_half_plan)
                passed = split_start("gather_pass_start_" + n, [], lands, 3, _forward_halves_plan)
                (gathered[n],) = split_wait("gather_pass_wait_" + n, passed, passed[-1], _forward_halves_plan)
            else:
                (gathered[n],) = split_wait("gather_wait_" + n, started[n], after, _gather_plan)
        return gathered[n]

    P = dict(pw1_b=full[0], dw_w=full[1], dw_b=full[2], cln_g=full[3], cln_b=full[4], pw2_b=full[5],
             rel_bias=rel_bias, ln_mix_g=ln_mix_g, ln_mix_b=ln_mix_b, ln_mlp_g=ln_mlp_g, ln_mlp_b=ln_mlp_b)

    ex = GradExchange(chip1, core, shard, mom, vel)

    small_names = ["pw1_b", "dw_w", "dw_b", "cln_g", "cln_b", "pw2_b", "rel_bias",
                   "ln_mix_g0", "ln_mix_g1", "ln_mix_b0", "ln_mix_b1", "ln_mlp_g0", "ln_mlp_g1", "ln_mlp_b0", "ln_mlp_b1"]
    small = {}

    def on_small(g):
        grads = [g[n] for n in small_names]
        small["shapes"] = [a.shape for a in grads]
        device1 = jnp.reshape(4 * xi + 2 * yi + ci, (1,)).astype(jnp.int32)
        land = place_block("place_small_grads", _pack(grads), device1, N_DEV)
        small["started"] = split_start("small_grads_start", [], [land], N_DEV - 1, _all_to_all_plan)
        ex.tokens.append(small["started"][-1])

    loss_sum, dx, g = local_step(x.reshape(T, D), loss_target.reshape(T, D), W, P, ex,
                                 first_deps=[started[n][-1] for n in BIG], on_small=on_small)
    loss = (0.5 / D) * lax.psum(loss_sum[0, 0], ("x", "y", "c"))
    (all_small,) = split_wait("small_grads_wait", small["started"], dx, _all_to_all_plan)
    summed = sum_devices("small_grad_sum", all_small)
    sg = dict(zip(small_names, _unpack(summed, small["shapes"])))

    def my_cols(a, width):
        return lax.dynamic_slice_in_dim(a, chip * width, width, axis=a.ndim - 1)

    small_g = [my_cols(sg["pw1_b"], conv_pw1_b.shape[-1]),
               my_cols(sg["dw_w"], conv_dw_w.shape[-1])[None],
               my_cols(sg["dw_b"], conv_dw_b.shape[-1]), my_cols(sg["cln_g"], conv_ln_g.shape[-1]),
               my_cols(sg["cln_b"], conv_ln_b.shape[-1]), my_cols(sg["pw2_b"], conv_pw2_b.shape[-1]),
               sg["rel_bias"],
               jnp.concatenate([sg["ln_mix_g0"], sg["ln_mix_g1"]], axis=0),
               jnp.concatenate([sg["ln_mix_b0"], sg["ln_mix_b1"]], axis=0),
               jnp.concatenate([sg["ln_mlp_g0"], sg["ln_mlp_g1"]], axis=0),
               jnp.concatenate([sg["ln_mlp_b0"], sg["ln_mlp_b1"]], axis=0)]
    small_w = [conv_pw1_b, conv_dw_w, conv_dw_b, conv_ln_g, conv_ln_b, conv_pw2_b, rel_bias, ln_mix_g, ln_mix_b, ln_mlp_g, ln_mlp_b]
    small_m = [m_conv_pw1_b, m_conv_dw_w, m_conv_dw_b, m_conv_ln_g, m_conv_ln_b, m_conv_pw2_b, m_rel_bias, m_ln_mix_g, m_ln_mix_b, m_ln_mlp_g, m_ln_mlp_b]
    small_v = [v_conv_pw1_b, v_conv_dw_w, v_conv_dw_b, v_conv_ln_g, v_conv_ln_b, v_conv_pw2_b, v_rel_bias, v_ln_mix_g, v_ln_mix_b, v_ln_mlp_g, v_ln_mlp_b]
    sw_shapes = [a.shape for a in small_w]
    small_g = [a.reshape(s) for a, s in zip(small_g, sw_shapes)]
    upd_small = adamw("adamw_small", _pack(small_w), _pack(small_g), _pack(small_m), _pack(small_v))
    sd, snm, snv = (_unpack(b, sw_shapes) for b in upd_small)

    res_w1 = adamw_layers("adamw_w1", mlp_w1, [ex.results["w1_0"][0], ex.results["w1_1"][0]], m_mlp_w1, v_mlp_w1)
    res_w2 = adamw_layers("adamw_w2", mlp_w2, [ex.results["w2_0"][0], ex.results["w2_1"][0]], m_mlp_w2, v_mlp_w2)
    ex.flush(res_w2[1])

    def big_out(k):
        one = {n: ex.results[n][k] for n in shard}
        return dict(pw1=one["pw1"][None], pw2=one["pw2"][None], kv=one["kv"], wq=one["wq"][None], wo=one["wo"][None],
                    w1=res_w1[k], w2=res_w2[k])

    def ordered(big, small):
        return [big["pw1"], small[0], small[1], small[2], small[3], small[4], big["pw2"], small[5], big["kv"], big["wq"],
                big["wo"], small[6], big["w1"], big["w2"], small[7], small[8], small[9], small[10]]

    grads = ordered(big_out(0), small_g)
    deltas = ordered(big_out(1), sd)
    new_m = ordered(big_out(2), snm)
    new_v = ordered(big_out(3), snv)
    return (loss, dx.reshape(1, T, D), *grads, *deltas, *new_m, *new_v)
```
